```python
import jax
import jax.numpy as jnp
from jax import lax
import numpy as np

D_MODEL = 1024
BATCH = 8
SEQ = 4096
DEPTH = 1

HEAD_DIM = 64
N_Q_HEADS = 8
N_KV_HEADS = 2
Q_PER_KV = N_Q_HEADS // N_KV_HEADS
ATTN_WIDTH = N_Q_HEADS * HEAD_DIM
ATTN_KV_WIDTH = N_KV_HEADS * HEAD_DIM
ATTN_COLS = ATTN_WIDTH + 2 * ATTN_KV_WIDTH
WINDOW = 128
ATTN_BLOCK = 128
ROPE_THETA = 10000.0
N_RWKV_HEADS = 8
RWKV_WIDTH = N_RWKV_HEADS * HEAD_DIM
DECAY_RANK = 32
ICLR_RANK = 32
GATE_RANK = 96
RWKV_COLS = 3 * RWKV_WIDTH + 2 * DECAY_RANK + 2 * ICLR_RANK + GATE_RANK
IN_COLS = ATTN_COLS + RWKV_COLS
MIX_WIDTH = ATTN_WIDTH + RWKV_WIDTH
RWKV_GN_EPS = 64e-5
PEER_HEADS = 8
PEER_NKEYS = 128
PEER_EXPERTS = PEER_NKEYS * PEER_NKEYS
PEER_QDIM = 256
PEER_HALF = PEER_QDIM // 2
PEER_TOPK = 16
PEER_CHUNK = 128
LN_EPS = 1e-5
DEEPNORM_ALPHA = (2.0 * DEPTH) ** 0.25
DEEPNORM_BETA = (8.0 * DEPTH) ** -0.25

kernel_name = 'hybrid_attn_rwkv7_peer_encoder_layer'


def layer_norm(x, g, b):
    xf = x.astype(jnp.float32)
    mu = jnp.mean(xf, -1, keepdims=True)
    var = jnp.mean(jnp.square(xf - mu), -1, keepdims=True)
    y = (xf - mu) * lax.rsqrt(var + LN_EPS) * g.astype(jnp.float32) + b.astype(jnp.float32)
    return y.astype(x.dtype)


def rotary(t):
    s = t.shape[1]
    pos = jnp.arange(s, dtype=jnp.float32)
    inv_freq = ROPE_THETA ** (-jnp.arange(0, HEAD_DIM, 2, dtype=jnp.float32) / HEAD_DIM)
    ang = pos[:, None] * inv_freq[None, :]
    ang = jnp.concatenate([ang, ang], -1)[:, None, :]
    tf = t.astype(jnp.float32)
    t1, t2 = jnp.split(tf, 2, -1)
    rot = jnp.concatenate([-t2, t1], -1)
    return (tf * jnp.cos(ang) + rot * jnp.sin(ang)).astype(t.dtype)


def banded_window_attention(q, k, v, sink):
    b, s = q.shape[:2]
    nb = s // ATTN_BLOCK
    qb = q.reshape(b, nb, ATTN_BLOCK, N_KV_HEADS, Q_PER_KV, HEAD_DIM)

    def neighbours(t):
        tp = jnp.pad(t, ((0, 0), (ATTN_BLOCK, ATTN_BLOCK), (0, 0), (0, 0)))
        tb = tp.reshape(b, nb + 2, ATTN_BLOCK, N_KV_HEADS, HEAD_DIM)
        return jnp.concatenate([tb[:, :-2], tb[:, 1:-1], tb[:, 2:]], axis=2)

    kw = neighbours(k)
    vw = neighbours(v)
    logits = jnp.einsum('bnqhgd,bnkhd->bnhgqk', qb, kw).astype(jnp.float32) * (HEAD_DIM ** -0.5)
    qpos = jnp.arange(s).reshape(nb, ATTN_BLOCK)
    kpos = (jnp.arange(nb)[:, None] - 1) * ATTN_BLOCK + jnp.arange(3 * ATTN_BLOCK)[None, :]
    valid = ((jnp.abs(qpos[:, :, None] - kpos[:, None, :]) <= WINDOW)
             & (kpos[:, None, :] >= 0) & (kpos[:, None, :] < s))
    logits = jnp.where(valid[None, :, None, None], logits, -1e30)
    sink_l = sink.astype(jnp.float32).reshape(N_KV_HEADS, Q_PER_KV)[None, None, :, :, None, None]
    m = jnp.maximum(jnp.max(logits, -1, keepdims=True), sink_l)
    e = jnp.exp(logits - m)
    probs = e / (jnp.sum(e, -1, keepdims=True) + jnp.exp(sink_l - m))
    out = jnp.einsum('bnhgqk,bnkhd->bnqhgd', probs.astype(v.dtype), vw)
    return out.reshape(b, s, ATTN_WIDTH)


def centred_token_shift(p, mu_prev, mu_next):
    zero = jnp.zeros_like(p[:, :1])
    p_prev = jnp.concatenate([zero, p[:, :-1]], 1)
    p_next = jnp.concatenate([p[:, 1:], zero], 1)
    return p + mu_prev * (p_prev - p) + mu_next * (p_next - p)


def wkv7_step(state, inp):
    r, w, k, v, kk, a = inp
    sa = jnp.einsum('bhvk,bhk->bhv', state, -kk)
    state = (state * w[:, :, None, :]
             + sa[..., :, None] * (kk * a)[:, :, None, :]
             + v[..., :, None] * k[:, :, None, :])
    y = jnp.einsum('bhvk,bhk->bhv', state, r)
    return state, y


def wkv7_scan(r, w, k, v, kk, a, reverse):
    def tm(t):
        return jnp.moveaxis(t, 1, 0)
    s0 = jnp.zeros((r.shape[0], N_RWKV_HEADS, HEAD_DIM, HEAD_DIM), jnp.float32)
    _, y = lax.scan(wkv7_step, s0, (tm(r), tm(w), tm(k), tm(v), tm(kk), tm(a)), reverse=reverse)
    return jnp.moveaxis(y, 0, 1)


def rwkv7_bidirectional(p, mu_prev, mu_next, w0, w2, a0, a2, g2, k_k, k_a, r_k, lnx_g, lnx_b):
    f32 = jnp.float32
    b, s, _ = p.shape
    p = centred_token_shift(p.astype(f32), mu_prev.astype(f32), mu_next.astype(f32))
    cuts = np.cumsum([RWKV_WIDTH, RWKV_WIDTH, RWKV_WIDTH, 2 * DECAY_RANK, 2 * ICLR_RANK]).tolist()
    r, k, v, wd, ad, gd = jnp.split(p, cuts, -1)
    wd = wd.reshape(b, s, 2, DECAY_RANK)
    ad = ad.reshape(b, s, 2, ICLR_RANK)
    w_log = -jax.nn.softplus(-(w0.astype(f32) + jnp.einsum('bsdr,drc->bsdc', jnp.tanh(wd), w2.astype(f32)))) - 0.5
    decay = jnp.exp(-jnp.exp(w_log))
    a = jax.nn.sigmoid(a0.astype(f32) + jnp.einsum('bsdr,drc->bsdc', ad, a2.astype(f32)))
    g = jax.nn.sigmoid(gd) @ g2.astype(f32)

    def heads(t):
        return t.reshape(t.shape[:-1] + (N_RWKV_HEADS, HEAD_DIM))

    kk = heads(k * k_k.astype(f32))
    kk = kk * lax.rsqrt(jnp.maximum(jnp.sum(jnp.square(kk), -1, keepdims=True), 1e-24))
    k_dir = k[:, :, None, :] * (1.0 + (a - 1.0) * k_a.astype(f32))
    rh = heads(r)
    vh = heads(v)
    y = (wkv7_scan(rh, heads(decay[:, :, 0]), heads(k_dir[:, :, 0]), vh, kk, heads(a[:, :, 0]), False)
         + wkv7_scan(rh, heads(decay[:, :, 1]), heads(k_dir[:, :, 1]), vh, kk, heads(a[:, :, 1]), True))
    mu = jnp.mean(y, -1, keepdims=True)
    var = jnp.mean(jnp.square(y - mu), -1, keepdims=True)
    y = ((y - mu) * lax.rsqrt(var + RWKV_GN_EPS)).reshape(b, s, RWKV_WIDTH)
    y = y * lnx_g.astype(f32) + lnx_b.astype(f32)
    k_mean = heads(0.5 * (k_dir[:, :, 0] + k_dir[:, :, 1]))
    bonus = jnp.sum(rh * k_mean * r_k.astype(f32), -1, keepdims=True) * vh
    return (y + bonus.reshape(b, s, RWKV_WIDTH)) * g


def peer_ffn(h, wq, keys, u, v):
    b, s, d = h.shape
    t = b * s
    ht = h.reshape(t, d)
    q = (ht @ wq).reshape(t, PEER_HEADS, 2, PEER_HALF).astype(jnp.float32)
    sc = jnp.einsum('thpc,hpnc->thpn', q, keys.astype(jnp.float32))
    s1, i1 = lax.top_k(sc[:, :, 0], PEER_TOPK)
    s2, i2 = lax.top_k(sc[:, :, 1], PEER_TOPK)
    cand = (s1[..., :, None] + s2[..., None, :]).reshape(t, PEER_HEADS, PEER_TOPK * PEER_TOPK)
    cs, ci = lax.top_k(cand, PEER_TOPK)
    idx = (jnp.take_along_axis(i1, ci // PEER_TOPK, -1) * PEER_NKEYS
           + jnp.take_along_axis(i2, ci % PEER_TOPK, -1))
    gate = jax.nn.softmax(cs, -1).astype(h.dtype)
    n_chunks = t // PEER_CHUNK
    n_sel = PEER_HEADS * PEER_TOPK

    def chunk(args):
        hc, ic, gc = args
        z = jnp.einsum('cd,ced->ce', hc, u[ic])
        return jnp.einsum('ce,ced->cd', gc * jax.nn.gelu(z, approximate=False), v[ic])

    y = lax.map(chunk, (ht.reshape(n_chunks, PEER_CHUNK, d),
                        idx.reshape(n_chunks, PEER_CHUNK, n_sel),
                        gate.reshape(n_chunks, PEER_CHUNK, n_sel)))
    return y.reshape(b, s, d)


def setup_inputs(seed: int = 0) -> dict:
    key = jax.random.key(seed)
    ks = jax.random.split(key, 26)
    f32 = jnp.float32
    L = DEPTH

    def nrm(k, shape, scale):
        return jax.random.normal(k, shape, f32) * scale

    x = nrm(ks[0], (BATCH, SEQ, D_MODEL), 1.0)
    col_scale = jnp.ones((IN_COLS,), f32)
    col_scale = col_scale.at[ATTN_WIDTH + ATTN_KV_WIDTH:ATTN_COLS].set(DEEPNORM_BETA)
    col_scale = col_scale.at[ATTN_COLS + 2 * RWKV_WIDTH:ATTN_COLS + 3 * RWKV_WIDTH].set(DEEPNORM_BETA)
    w_in = nrm(ks[1], (L, D_MODEL, IN_COLS), D_MODEL ** -0.5) * col_scale
    mu_prev = jax.random.uniform(ks[2], (L, RWKV_COLS), f32, 0.0, 0.4)
    mu_next = jax.random.uniform(ks[3], (L, RWKV_COLS), f32, 0.0, 0.4)
    w0 = jax.random.uniform(ks[4], (L, 2, RWKV_WIDTH), f32, -4.0, 1.0)
    w2 = nrm(ks[5], (L, 2, DECAY_RANK, RWKV_WIDTH), 0.3 * DECAY_RANK ** -0.5)
    a0 = nrm(ks[6], (L, 2, RWKV_WIDTH), 0.5)
    a2 = nrm(ks[7], (L, 2, ICLR_RANK, RWKV_WIDTH), ICLR_RANK ** -0.5)
    g2 = nrm(ks[8], (L, GATE_RANK, RWKV_WIDTH), GATE_RANK ** -0.5)
    k_k = 0.85 + nrm(ks[9], (L, RWKV_WIDTH), 0.05)
    k_a = 1.0 + nrm(ks[10], (L, RWKV_WIDTH), 0.05)
    r_k = nrm(ks[11], (L, N_RWKV_HEADS, HEAD_DIM), 0.1)
    lnx_g = 1.0 + nrm(ks[12], (L, RWKV_WIDTH), 0.05)
    lnx_b = nrm(ks[13], (L, RWKV_WIDTH), 0.02)
    sink = nrm(ks[14], (L, N_Q_HEADS), 0.5)
    w_out = nrm(ks[15], (L, MIX_WIDTH, D_MODEL), DEEPNORM_BETA * MIX_WIDTH ** -0.5)
    ln1_g = 1.0 + nrm(ks[16], (L, D_MODEL), 0.05)
    ln1_b = nrm(ks[17], (L, D_MODEL), 0.02)
    peer_wq = nrm(ks[18], (L, D_MODEL, PEER_HEADS * PEER_QDIM), D_MODEL ** -0.5)
    peer_keys = nrm(ks[19], (L, PEER_HEADS, 2, PEER_NKEYS, PEER_HALF), PEER_HALF ** -0.5)
    peer_u = nrm(ks[20], (L, PEER_EXPERTS, D_MODEL), DEEPNORM_BETA * D_MODEL ** -0.5)
    peer_v = nrm(ks[21], (L, PEER_EXPERTS, D_MODEL), DEEPNORM_BETA)
    ln2_g = 1.0 + nrm(ks[22], (L, D_MODEL), 0.05)
    ln2_b = nrm(ks[23], (L, D_MODEL), 0.02)
    return {'x': x, 'w_in': w_in, 'mu_prev': mu_prev, 'mu_next': mu_next,
            'w0': w0, 'w2': w2, 'a0': a0, 'a2': a2, 'g2': g2,
            'k_k': k_k, 'k_a': k_a, 'r_k': r_k, 'lnx_g': lnx_g, 'lnx_b': lnx_b,
            'sink': sink, 'w_out': w_out, 'ln1_g': ln1_g, 'ln1_b': ln1_b,
            'peer_wq': peer_wq, 'peer_keys': peer_keys, 'peer_u': peer_u, 'peer_v': peer_v,
            'ln2_g': ln2_g, 'ln2_b': ln2_b}


def reference(x, w_in, mu_prev, mu_next, w0, w2, a0, a2, g2, k_k, k_a, r_k, lnx_g, lnx_b,
              sink, w_out, ln1_g, ln1_b, peer_wq, peer_keys, peer_u, peer_v, ln2_g, ln2_b):
    b, s, _ = x.shape
    h = x
    for l in range(DEPTH):
        proj = h @ w_in[l]
        q, k, v, pr = jnp.split(proj, [ATTN_WIDTH, ATTN_WIDTH + ATTN_KV_WIDTH, ATTN_COLS], -1)
        q = rotary(q.reshape(b, s, N_Q_HEADS, HEAD_DIM))
        k = rotary(k.reshape(b, s, N_KV_HEADS, HEAD_DIM))
        v = v.reshape(b, s, N_KV_HEADS, HEAD_DIM)
        y_attn = banded_window_attention(q, k, v, sink[l])
        y_rwkv = rwkv7_bidirectional(pr, mu_prev[l], mu_next[l], w0[l], w2[l], a0[l], a2[l], g2[l],
                                     k_k[l], k_a[l], r_k[l], lnx_g[l], lnx_b[l]).astype(h.dtype)
        mix = jnp.concatenate([y_attn, y_rwkv], -1) @ w_out[l]
        h = layer_norm(DEEPNORM_ALPHA * h + mix, ln1_g[l], ln1_b[l])
        f = peer_ffn(h, peer_wq[l], peer_keys[l], peer_u[l], peer_v[l])
        h = layer_norm(DEEPNORM_ALPHA * h + f, ln2_g[l], ln2_b[l])
    return h
```

```python
import functools

import numpy as np
import jax
import jax.numpy as jnp
from jax import lax
from jax.experimental import pallas as pl
from jax.experimental.pallas import tpu as pltpu

F32 = jnp.float32
BF16 = jnp.bfloat16
HI = lax.Precision.HIGHEST

HEAD_DIM = 64
N_Q_HEADS = 8
N_KV_HEADS = 2
Q_PER_KV = N_Q_HEADS // N_KV_HEADS
ATTN_WIDTH = N_Q_HEADS * HEAD_DIM
ATTN_KV_WIDTH = N_KV_HEADS * HEAD_DIM
ATTN_COLS = ATTN_WIDTH + 2 * ATTN_KV_WIDTH
WINDOW = 128
ATTN_BLOCK = 128
ROPE_THETA = 10000.0
N_RWKV_HEADS = 8
RWKV_WIDTH = N_RWKV_HEADS * HEAD_DIM
DECAY_RANK = 32
ICLR_RANK = 32
GATE_RANK = 96
RWKV_COLS = 3 * RWKV_WIDTH + 2 * DECAY_RANK + 2 * ICLR_RANK + GATE_RANK
RWKV_COLS_PAD = 1792
RWKV_GN_EPS = 64e-5
PEER_HEADS = 8
PEER_NKEYS = 128
PEER_QDIM = 256
PEER_HALF = PEER_QDIM // 2
PEER_TOPK = 16
N_SEL = PEER_HEADS * PEER_TOPK
LN_EPS = 1e-5
DEPTH = 1
DEEPNORM_ALPHA = (2.0 * DEPTH) ** 0.25

LANES = 128
SUBLANES = 8
WKV_CHUNK = 64
VMEM_LIMIT = 48 * 1024 * 1024


def _cparams(sem):
    return pltpu.CompilerParams(dimension_semantics=sem, vmem_limit_bytes=VMEM_LIMIT)


def _dot(a, b, precision=None):
    return jnp.dot(a, b, preferred_element_type=F32, precision=precision)


def _dot_nt(a, b, precision=None):
    return lax.dot_general(a, b, (((1,), (1,)), ((), ())), preferred_element_type=F32,
                           precision=precision)


def _dot_tn(a, b, precision=None):
    return lax.dot_general(a, b, (((0,), (0,)), ((), ())), preferred_element_type=F32,
                           precision=precision)


def _layer_norm(z, g, b):
    mu = jnp.mean(z, -1, keepdims=True)
    zc = z - mu
    var = jnp.mean(zc * zc, -1, keepdims=True)
    return zc * lax.rsqrt(var + LN_EPS) * g + b


def _proj_kernel(x_ref, w_ref, cos_ref, sin_ref, qkv_ref, pr_ref):
    xb = x_ref[...].astype(BF16)
    cos = cos_ref[...]
    sin = sin_ref[...]
    lane = lax.broadcasted_iota(jnp.int32, cos.shape, 1)
    first_half = (lane & (HEAD_DIM // 2)) == 0

    def rope(t):
        rot = jnp.where(first_half, pltpu.roll(t, LANES - HEAD_DIM // 2, 1),
                        pltpu.roll(t, HEAD_DIM // 2, 1))
        return t * cos + rot * sin

    for c in range(0, ATTN_COLS, 2 * LANES):
        acc = _dot(xb, w_ref[:, c:c + 2 * LANES])
        for half in range(2):
            col = c + half * LANES
            t = acc[:, half * LANES:(half + 1) * LANES]
            if col < ATTN_WIDTH + ATTN_KV_WIDTH:
                t = rope(t)
            qkv_ref[:, col:col + LANES] = t
    for c in range(0, RWKV_COLS_PAD, 2 * LANES):
        pr_ref[:, c:c + 2 * LANES] = _dot(xb, w_ref[:, ATTN_COLS + c:ATTN_COLS + c + 2 * LANES])


def _proj(x2, w_in_p, cos_t, sin_t, seq, tm=512):
    t, d = x2.shape
    n_pos = seq // tm
    return pl.pallas_call(
        _proj_kernel,
        grid=(t // tm,),
        in_specs=[
            pl.BlockSpec((tm, d), lambda i: (i, 0)),
            pl.BlockSpec(w_in_p.shape, lambda i: (0, 0)),
            pl.BlockSpec((tm, LANES), lambda i: (i % n_pos, 0)),
            pl.BlockSpec((tm, LANES), lambda i: (i % n_pos, 0)),
        ],
        out_specs=[
            pl.BlockSpec((tm, ATTN_COLS), lambda i: (i, 0)),
            pl.BlockSpec((tm, RWKV_COLS_PAD), lambda i: (i, 0)),
        ],
        out_shape=[
            jax.ShapeDtypeStruct((t, ATTN_COLS), F32),
            jax.ShapeDtypeStruct((t, RWKV_COLS_PAD), F32),
        ],
        compiler_params=_cparams(("parallel",)),
        name="proj",
    )(x2, w_in_p, cos_t, sin_t)


def _attn_kernel(sink_ref, q_ref, kp_ref, kc_ref, kn_ref, vp_ref, vc_ref, vn_ref, o_ref, *, nb):
    n = pl.program_id(1)
    blk = ATTN_BLOCK
    rows = Q_PER_KV * blk
    q = q_ref[0] * (HEAD_DIM ** -0.5)
    kwin = jnp.concatenate([kp_ref[0], kc_ref[0], kn_ref[0]], axis=0)
    vwin = jnp.concatenate([vp_ref[0], vc_ref[0], vn_ref[0]], axis=0)
    qi = lax.broadcasted_iota(jnp.int32, (rows, 3 * blk), 0) & (blk - 1)
    kj = lax.broadcasted_iota(jnp.int32, (rows, 3 * blk), 1)
    dist = kj - qi
    valid = ((dist >= blk - WINDOW) & (dist <= blk + WINDOW)
             & ((kj >= blk) | (n > 0)) & ((kj < 2 * blk) | (n < nb - 1)))
    rowg = lax.broadcasted_iota(jnp.int32, (rows, 1), 0) // blk
    outs = []
    for h in range(N_KV_HEADS):
        qs = jnp.concatenate(
            [q[:, (Q_PER_KV * h + g) * HEAD_DIM:(Q_PER_KV * h + g + 1) * HEAD_DIM]
             for g in range(Q_PER_KV)], axis=0)
        kh = kwin[:, h * HEAD_DIM:(h + 1) * HEAD_DIM]
        vh = vwin[:, h * HEAD_DIM:(h + 1) * HEAD_DIM]
        logits = _dot_nt(qs.astype(BF16), kh.astype(BF16))
        logits = jnp.where(valid, logits, -1e30)
        sk = jnp.zeros((rows, 1), F32)
        for g in range(Q_PER_KV):
            sk = jnp.where(rowg == g, sink_ref[Q_PER_KV * h + g], sk)
        m = jnp.maximum(jnp.max(logits, -1, keepdims=True), sk)
        e = jnp.exp(logits - m)
        den = jnp.sum(e, -1, keepdims=True) + jnp.exp(sk - m)
        p = e / den
        o = _dot(p.astype(BF16), vh.astype(BF16))
        for g in range(Q_PER_KV):
            outs.append(o[g * blk:(g + 1) * blk])
    o_ref[0] = jnp.concatenate(outs, axis=1)


def _attention(qkv3, sink):
    b, s, _ = qkv3.shape
    blk = ATTN_BLOCK
    nb = s // blk
    kcol = ATTN_WIDTH // LANES
    vcol = kcol + 1

    def spec(col, shift):
        def imap(bi, n):
            return (bi, jnp.clip(n + shift, 0, nb - 1), col)
        return pl.BlockSpec((1, blk, LANES), imap)

    return pl.pallas_call(
        functools.partial(_attn_kernel, nb=nb),
        grid=(b, nb),
        in_specs=[
            pl.BlockSpec(memory_space=pltpu.SMEM),
            pl.BlockSpec((1, blk, ATTN_WIDTH), lambda bi, n: (bi, n, 0)),
            spec(kcol, -1), spec(kcol, 0), spec(kcol, 1),
            spec(vcol, -1), spec(vcol, 0), spec(vcol, 1),
        ],
        out_specs=pl.BlockSpec((1, blk, ATTN_WIDTH), lambda bi, n: (bi, n, 0)),
        out_shape=jax.ShapeDtypeStruct((b, s, ATTN_WIDTH), F32),
        compiler_params=_cparams(("parallel", "parallel")),
        name="attn",
    )(sink, qkv3, qkv3, qkv3, qkv3, qkv3, qkv3, qkv3)


def _softplus(x):
    return jnp.maximum(x, 0.0) + jnp.log(1.0 + jnp.exp(-jnp.abs(x)))


def _sigmoid(x):
    return 1.0 / (1.0 + jnp.exp(-x))


def _prep_kernel(p_ref, hp_ref, hn_ref, mup_ref, mun_ref, wmix_ref, w0_ref, a0_ref, g2_ref,
                 kk_ref, ka_ref, bd_ref,
                 r_o, v_o, kk_o, g_o, lw0_o, lw1_o, kd0_o, kd1_o, b0_o, b1_o, *, tm, seq):
    i = pl.program_id(0)
    row = lax.broadcasted_iota(jnp.int32, (tm, 1), 0)
    seq_start = (i * tm) % seq == 0
    seq_end = ((i + 1) * tm) % seq == 0

    def shifted(c0, c1):
        p = p_ref[:, c0:c1]
        prev_row = jnp.where(seq_start, 0.0, hp_ref[SUBLANES - 1:SUBLANES, c0:c1])
        next_row = jnp.where(seq_end, 0.0, hn_ref[0:1, c0:c1])
        p_prev = jnp.where(row == 0, prev_row, pltpu.roll(p, 1, 0))
        p_next = jnp.where(row == tm - 1, next_row, pltpu.roll(p, tm - 1, 0))
        return p + mup_ref[:, c0:c1] * (p_prev - p) + mun_ref[:, c0:c1] * (p_next - p)

    w = RWKV_WIDTH
    r = shifted(0, w)
    k = shifted(w, 2 * w)
    v = shifted(2 * w, 3 * w)
    codes = shifted(3 * w, 3 * w + LANES)
    gd = shifted(3 * w + LANES, 3 * w + 2 * LANES)
    r_o[...] = r
    v_o[...] = v

    lane = lax.broadcasted_iota(jnp.int32, codes.shape, 1)
    codes = jnp.where(lane < 2 * DECAY_RANK, jnp.tanh(codes), codes)
    mm = _dot(codes, wmix_ref[...], HI)
    g_o[...] = _dot(_sigmoid(gd), g2_ref[...], HI)

    kkv = k * kk_ref[...]
    ss = _dot(kkv * kkv, bd_ref[...], HI)
    kkn = kkv * lax.rsqrt(jnp.maximum(ss, 1e-24))
    kk_o[...] = kkn

    ka = ka_ref[...]
    for d, (lw_o, kd_o, b_o) in enumerate(((lw0_o, kd0_o, b0_o), (lw1_o, kd1_o, b1_o))):
        w_log = -_softplus(-(w0_ref[d:d + 1, :] + mm[:, d * w:(d + 1) * w])) - 0.5
        lw_o[...] = -jnp.exp(w_log)
        a = _sigmoid(a0_ref[d:d + 1, :] + mm[:, (2 + d) * w:(3 + d) * w])
        kd_o[...] = k * (1.0 + (a - 1.0) * ka)
        b_o[...] = kkn * a


def _prep(pr, mup, mun, wmix, w0, a0, g2p, k_k, k_a, bd, seq, tm=256):
    t = pr.shape[0]
    nblk8 = t // SUBLANES
    per = tm // SUBLANES
    full = lambda a: pl.BlockSpec(a.shape, lambda i: (0,) * a.ndim)
    out = pl.BlockSpec((tm, RWKV_WIDTH), lambda i: (i, 0))
    return pl.pallas_call(
        functools.partial(_prep_kernel, tm=tm, seq=seq),
        grid=(t // tm,),
        in_specs=[
            pl.BlockSpec((tm, RWKV_COLS_PAD), lambda i: (i, 0)),
            pl.BlockSpec((SUBLANES, RWKV_COLS_PAD), lambda i: (jnp.maximum(i * per - 1, 0), 0)),
            pl.BlockSpec((SUBLANES, RWKV_COLS_PAD),
                         lambda i: (jnp.minimum((i + 1) * per, nblk8 - 1), 0)),
            full(mup), full(mun), full(wmix), full(w0), full(a0), full(g2p),
            full(k_k), full(k_a), full(bd),
        ],
        out_specs=[out] * 10,
        out_shape=[jax.ShapeDtypeStruct((t, RWKV_WIDTH), F32)] * 10,
        compiler_params=_cparams(("parallel",)),
        name="prep",
    )(pr, pr, pr, mup, mun, wmix, w0, a0, g2p, k_k, k_a, bd)


def _wkv_kernel(rf, vf, kkf, lwf, kdf, bf, rb, vb, kkb, lwb, kdb, bb, yf_ref, yb_ref, state):
    c = pl.program_id(2)
    n = WKV_CHUNK
    hd = HEAD_DIM

    @pl.when(c == 0)
    def _():
        state[...] = jnp.zeros_like(state)

    ti = lax.broadcasted_iota(jnp.int32, (n, n), 0)
    si = lax.broadcasted_iota(jnp.int32, (n, n), 1)
    eye = (ti == si).astype(F32)
    dirs = ((rf, vf, kkf, lwf, kdf, bf, yf_ref, si <= ti, si < ti, n - 1),
            (rb, vb, kkb, lwb, kdb, bb, yb_ref, si >= ti, si > ti, 0))
    for d, (r_ref, v_ref, kk_ref, lw_ref, kd_ref, b_ref, y_ref, incl, strict, last) in enumerate(dirs):
        lw = lw_ref[0]
        cum = _dot(incl.astype(F32), lw, HI)
        e_cum = jnp.exp(cum)
        e_neg = jnp.exp(-cum)
        at2 = -kk_ref[0] * jnp.exp(cum - lw)
        rt2 = r_ref[0] * e_cum
        e_last = jnp.exp(cum[last:last + 1, :] - cum)
        bt2 = b_ref[0] * e_neg
        kt2 = kd_ref[0] * e_neg
        bl2 = b_ref[0] * e_last
        kl2 = kd_ref[0] * e_last
        g_scale = jnp.exp(cum[last:last + 1, :])
        v2 = v_ref[0]
        ys = []
        for j in range(2):
            sl = slice(j * hd, (j + 1) * hd)
            at, rt, bt, kt, v = at2[:, sl], rt2[:, sl], bt2[:, sl], kt2[:, sl], v2[:, sl]
            g0 = state[2 * d + j]
            m1 = _dot_nt(jnp.concatenate([at, rt], 0), jnp.concatenate([bt, kt], 0), HI)
            a_ab = jnp.where(strict, m1[:n, :n], 0.0)
            a_ak = jnp.where(strict, m1[:n, n:], 0.0)
            a_rb = jnp.where(incl, m1[n:, :n], 0.0)
            a_rk = jnp.where(incl, m1[n:, n:], 0.0)
            tinv = eye + a_ab
            pw = a_ab
            for _ in range(int(np.log2(n)) - 1):
                pw = _dot(pw, pw, HI)
                tinv = tinv + _dot(tinv, pw, HI)
            rhs = _dot(a_ak, v, HI) + _dot_nt(at, g0, HI)
            u = _dot(tinv, rhs, HI)
            uv = jnp.concatenate([u, v], 0)
            y = _dot(jnp.concatenate([a_rb, a_rk], 1), uv, HI) + _dot_nt(rt, g0, HI)
            ys.append(y)
            bk = jnp.concatenate([bl2[:, sl], kl2[:, sl]], 0)
            state[2 * d + j] = g0 * g_scale[:, sl] + _dot_tn(uv, bk, HI)
        y_ref[0] = jnp.concatenate(ys, axis=1)


def _wkv(r, v, kk, lw0, lw1, kd0, kd1, b0, b1, batch, seq):
    n = WKV_CHUNK
    nc = seq // n
    shp = (batch, seq, RWKV_WIDTH)
    arrs = [a.reshape(shp) for a in (r, v, kk, lw0, kd0, b0, r, v, kk, lw1, kd1, b1)]
    fwd = pl.BlockSpec((1, n, LANES), lambda b, h, c: (b, c, h))
    bwd = pl.BlockSpec((1, n, LANES), lambda b, h, c: (b, nc - 1 - c, h))
    yf, yb = pl.pallas_call(
        _wkv_kernel,
        grid=(batch, RWKV_WIDTH // LANES, nc),
        in_specs=[fwd] * 6 + [bwd] * 6,
        out_specs=[fwd, bwd],
        out_shape=[jax.ShapeDtypeStruct(shp, F32)] * 2,
        scratch_shapes=[pltpu.VMEM((4, HEAD_DIM, HEAD_DIM), F32)],
        compiler_params=_cparams(("parallel", "parallel", "arbitrary")),
        name="wkv",
    )(*arrs)
    return yf.reshape(batch * seq, RWKV_WIDTH), yb.reshape(batch * seq, RWKV_WIDTH)


def _mix_kernel(x_ref, ya_ref, yf_ref, yb_ref, r_ref, v_ref, kd0_ref, kd1_ref, g_ref,
                lng_ref, lnb_ref, rk_ref, bd_ref, wout_ref, l1g_ref, l1b_ref, h_ref):
    bd = bd_ref[...]
    inv = 1.0 / HEAD_DIM
    y = yf_ref[...] + yb_ref[...]
    mu = _dot(y, bd, HI) * inv
    yc = y - mu
    var = _dot(yc * yc, bd, HI) * inv
    yn = yc * lax.rsqrt(var + RWKV_GN_EPS) * lng_ref[...] + lnb_ref[...]
    k_mean = 0.5 * (kd0_ref[...] + kd1_ref[...])
    v = v_ref[...]
    bonus = _dot(r_ref[...] * k_mean * rk_ref[...], bd, HI) * v
    yr = (yn + bonus) * g_ref[...]
    mix = (_dot(ya_ref[...].astype(BF16), wout_ref[:ATTN_WIDTH, :])
           + _dot(yr.astype(BF16), wout_ref[ATTN_WIDTH:, :]))
    h_ref[...] = _layer_norm(DEEPNORM_ALPHA * x_ref[...] + mix, l1g_ref[...], l1b_ref[...])


def _mix(x2, ya, yf, yb, r, v, kd0, kd1, g, lng, lnb, rk, bd, wout, l1g, l1b, tm=256):
    t, d = x2.shape
    full = lambda a: pl.BlockSpec(a.shape, lambda i: (0,) * a.ndim)
    half = pl.BlockSpec((tm, RWKV_WIDTH), lambda i: (i, 0))
    wide = pl.BlockSpec((tm, d), lambda i: (i, 0))
    return pl.pallas_call(
        _mix_kernel,
        grid=(t // tm,),
        in_specs=[wide] + [half] * 8 + [full(lng), full(lnb), full(rk), full(bd), full(wout),
                                        full(l1g), full(l1b)],
        out_specs=wide,
        out_shape=jax.ShapeDtypeStruct((t, d), F32),
        compiler_params=_cparams(("parallel",)),
        name="mix",
    )(x2, ya, yf, yb, r, v, kd0, kd1, g, lng, lnb, rk, bd, wout, l1g, l1b)


def _top_rows(sc, k, payload=None):
    n = sc.shape[0]
    iota = lax.broadcasted_iota(jnp.int32, sc.shape, 0)
    vals, picks = [], []
    for _ in range(k):
        m = jnp.max(sc, axis=0, keepdims=True)
        pos = jnp.min(jnp.where(sc == m, iota, n), axis=0, keepdims=True)
        hit = iota == pos
        vals.append(m)
        if payload is None:
            picks.append(pos)
        else:
            picks.append(jnp.max(jnp.where(hit, payload, -1), axis=0, keepdims=True))
        sc = jnp.where(hit, -jnp.inf, sc)
    return jnp.concatenate(vals, 0), jnp.concatenate(picks, 0)


def _route_kernel(h_ref, wq_ref, keys_ref, idx_ref, gate_ref, q_scr, idx_scr):
    k = PEER_TOPK
    q_scr[...] = _dot(h_ref[...].astype(BF16), wq_ref[...])

    def head(hh, carry):
        tops = []
        for p in range(2):
            col = pl.multiple_of(hh * PEER_QDIM + p * PEER_HALF, PEER_HALF)
            qp = q_scr[:, pl.ds(col, PEER_HALF)]
            sct = _dot_nt(keys_ref[hh, p], qp, HI)
            tops.append(_top_rows(sct, k))
        (s1, i1), (s2, i2) = tops
        cand = jnp.concatenate([s1[i:i + 1] + s2 for i in range(k)], 0)
        cand_id = jnp.concatenate([i1[i:i + 1] * PEER_NKEYS + i2 for i in range(k)], 0)
        cs, ids = _top_rows(cand, k, cand_id)
        e = jnp.exp(cs - jnp.max(cs, axis=0, keepdims=True))
        row = pl.multiple_of(hh * k, k)
        gate_ref[pl.ds(row, k), :] = e / jnp.sum(e, axis=0, keepdims=True)
        idx_scr[pl.ds(row, k), :] = ids
        return carry

    lax.fori_loop(0, PEER_HEADS, head, 0)
    idx_ref[...] = idx_scr[...].T


def _route(h1, wq, keys, tm=256):
    t, d = h1.shape
    return pl.pallas_call(
        _route_kernel,
        grid=(t // tm,),
        in_specs=[
            pl.BlockSpec((tm, d), lambda i: (i, 0)),
            pl.BlockSpec(wq.shape, lambda i: (0, 0)),
            pl.BlockSpec(keys.shape, lambda i: (0, 0, 0, 0)),
        ],
        out_specs=[
            pl.BlockSpec((tm, N_SEL), lambda i: (i, 0)),
            pl.BlockSpec((N_SEL, tm), lambda i: (0, i)),
        ],
        out_shape=[
            jax.ShapeDtypeStruct((t, N_SEL), jnp.int32),
            jax.ShapeDtypeStruct((N_SEL, t), F32),
        ],
        scratch_shapes=[pltpu.VMEM((tm, PEER_HEADS * PEER_QDIM), F32),
                        pltpu.VMEM((N_SEL, tm), jnp.int32)],
        compiler_params=_cparams(("parallel",)),
        name="route",
    )(h1, wq, keys)


def _peer_kernel(idx_ref, gate_ref, h_ref, uv_hbm, l2g_ref, l2b_ref, o_ref, buf, f_scr, sem, *, tb):
    d = h_ref.shape[1]

    def row_copy(src_row, slot, j):
        return pltpu.make_async_copy(uv_hbm.at[pl.ds(src_row, 1)], buf.at[slot, pl.ds(j, 1)],
                                     sem.at[slot])

    def issue(t, slot):
        for j in range(N_SEL):
            row_copy(idx_ref[t, j], slot, j).start()

    def wait_all(slot):
        pltpu.make_async_copy(uv_hbm.at[pl.ds(0, N_SEL)], buf.at[slot], sem.at[slot]).wait()

    issue(0, 0)
    lane = lax.broadcasted_iota(jnp.int32, (N_SEL, tb), 1)

    def body(t, carry):
        slot = t % 2

        @pl.when(t + 1 < tb)
        def _():
            issue(t + 1, 1 - slot)

        wait_all(slot)
        u = buf[slot, :, 0:d]
        v = buf[slot, :, d:2 * d]
        z = jnp.sum(u * h_ref[pl.ds(t, 1), :], axis=1, keepdims=True)
        gcol = jnp.sum(jnp.where(lane == t, gate_ref[...], 0.0), axis=1, keepdims=True)
        coef = gcol * (0.5 * z * (1.0 + lax.erf(z * (2.0 ** -0.5))))
        f_scr[pl.ds(t, 1), :] = jnp.sum(coef * v, axis=0, keepdims=True)
        return carry

    lax.fori_loop(0, tb, body, 0)
    o_ref[...] = _layer_norm(DEEPNORM_ALPHA * h_ref[...] + f_scr[...], l2g_ref[...], l2b_ref[...])


def _peer(idx, gate_t, h1, uv, l2g, l2b, tb=128):
    t, d = h1.shape
    return pl.pallas_call(
        functools.partial(_peer_kernel, tb=tb),
        grid=(t // tb,),
        in_specs=[
            pl.BlockSpec((tb, N_SEL), lambda i: (i, 0), memory_space=pltpu.SMEM),
            pl.BlockSpec((N_SEL, tb), lambda i: (0, i)),
            pl.BlockSpec((tb, d), lambda i: (i, 0)),
            pl.BlockSpec(memory_space=pl.ANY),
            pl.BlockSpec(l2g.shape, lambda i: (0, 0)),
            pl.BlockSpec(l2b.shape, lambda i: (0, 0)),
        ],
        out_specs=pl.BlockSpec((tb, d), lambda i: (i, 0)),
        out_shape=jax.ShapeDtypeStruct((t, d), F32),
        scratch_shapes=[pltpu.VMEM((2, N_SEL, 2 * d), F32),
                        pltpu.VMEM((tb, d), F32),
                        pltpu.SemaphoreType.DMA((2,))],
        compiler_params=_cparams(("arbitrary",)),
        name="peer",
    )(idx, gate_t, h1, uv, l2g, l2b)


def _rope_tables(seq):
    pos = jnp.arange(seq, dtype=F32)
    inv_freq = ROPE_THETA ** (-jnp.arange(0, HEAD_DIM, 2, dtype=F32) / HEAD_DIM)
    ang = pos[:, None] * inv_freq[None, :]
    ang = jnp.concatenate([ang, ang], -1)
    sign = jnp.concatenate([-jnp.ones((HEAD_DIM // 2,), F32), jnp.ones((HEAD_DIM // 2,), F32)])
    reps = LANES // HEAD_DIM
    return jnp.tile(jnp.cos(ang), (1, reps)), jnp.tile(jnp.sin(ang) * sign, (1, reps))


def _layer(h2d, batch, seq, w_in, mu_prev, mu_next, w0, w2, a0, a2, g2, k_k, k_a, r_k, lnx_g, lnx_b,
           sink, w_out, ln1_g, ln1_b, peer_wq, peer_keys, peer_u, peer_v, ln2_g, ln2_b):
    w = RWKV_WIDTH
    row = lambda a: a.reshape(1, -1).astype(F32)
    w_in_p = jnp.pad(w_in, ((0, 0), (0, RWKV_COLS_PAD - RWKV_COLS))).astype(BF16)
    mup = jnp.pad(mu_prev, (0, RWKV_COLS_PAD - RWKV_COLS)).reshape(1, -1)
    mun = jnp.pad(mu_next, (0, RWKV_COLS_PAD - RWKV_COLS)).reshape(1, -1)
    wmix = jnp.zeros((LANES, 4 * w), F32)
    for d in range(2):
        wmix = wmix.at[d * DECAY_RANK:(d + 1) * DECAY_RANK, d * w:(d + 1) * w].set(w2[d])
        r0 = 2 * DECAY_RANK + d * ICLR_RANK
        wmix = wmix.at[r0:r0 + ICLR_RANK, (2 + d) * w:(3 + d) * w].set(a2[d])
    g2p = jnp.pad(g2, ((0, LANES - GATE_RANK), (0, 0)))
    head_of = np.arange(w) // HEAD_DIM
    bd = jnp.asarray(head_of[:, None] == head_of[None, :], F32)
    cos_t, sin_t = _rope_tables(seq)

    qkv, pr = _proj(h2d, w_in_p, cos_t, sin_t, seq)
    y_attn = _attention(qkv.reshape(batch, seq, ATTN_COLS), sink.astype(F32))
    r, v, kk, g, lw0, lw1, kd0, kd1, b0, b1 = _prep(
        pr, mup, mun, wmix, w0.astype(F32), a0.astype(F32), g2p, row(k_k), row(k_a), bd, seq)
    yf, yb = _wkv(r, v, kk, lw0, lw1, kd0, kd1, b0, b1, batch, seq)
    h1 = _mix(h2d, y_attn.reshape(batch * seq, ATTN_WIDTH), yf, yb, r, v, kd0, kd1, g,
              row(lnx_g), row(lnx_b), row(r_k), bd, w_out.astype(BF16), row(ln1_g), row(ln1_b))
    idx, gate_t = _route(h1, peer_wq.astype(BF16), peer_keys.astype(F32))
    uv = jnp.concatenate([peer_u, peer_v], axis=1)
    return _peer(idx, gate_t, h1, uv, row(ln2_g), row(ln2_b))


def kernel(x, w_in, mu_prev, mu_next, w0, w2, a0, a2, g2, k_k, k_a, r_k, lnx_g, lnx_b, sink, w_out,
           ln1_g, ln1_b, peer_wq, peer_keys, peer_u, peer_v, ln2_g, ln2_b):
    batch, seq, d = x.shape
    h = x.reshape(batch * seq, d)
    for l in range(DEPTH):
        h = _layer(h, batch, seq, w_in[l], mu_prev[l], mu_next[l], w0[l], w2[l], a0[l], a2[l], g2[l],
                   k_k[l], k_a[l], r_k[l], lnx_g[l], lnx_b[l], sink[l], w_out[l], ln1_g[l], ln1_b[l],
                   peer_wq[l], peer_keys[l], peer_u[l], peer_v[l], ln2_g[l], ln2_b[l])
    return h.reshape(batch, seq, d)
```

```python
import functools

import numpy as np
import jax
import jax.numpy as jnp
from jax import lax
from jax.experimental import pallas as pl
from jax.experimental.pallas import tpu as pltpu

F32 = jnp.float32
BF16 = jnp.bfloat16
HI = lax.Precision.HIGHEST

HEAD_DIM = 64
N_Q_HEADS = 8
N_KV_HEADS = 2
Q_PER_KV = N_Q_HEADS // N_KV_HEADS
ATTN_WIDTH = N_Q_HEADS * HEAD_DIM
ATTN_KV_WIDTH = N_KV_HEADS * HEAD_DIM
ATTN_COLS = ATTN_WIDTH + 2 * ATTN_KV_WIDTH
WINDOW = 128
ATTN_BLOCK = 128
ROPE_THETA = 10000.0
N_RWKV_HEADS = 8
RWKV_WIDTH = N_RWKV_HEADS * HEAD_DIM
DECAY_RANK = 32
ICLR_RANK = 32
GATE_RANK = 96
RWKV_COLS = 3 * RWKV_WIDTH + 2 * DECAY_RANK + 2 * ICLR_RANK + GATE_RANK
RWKV_COLS_PAD = 1792
RWKV_GN_EPS = 64e-5
PEER_HEADS = 8
PEER_NKEYS = 128
PEER_QDIM = 256
PEER_HALF = PEER_QDIM // 2
PEER_TOPK = 16
N_SEL = PEER_HEADS * PEER_TOPK
LN_EPS = 1e-5
DEPTH = 1
DEEPNORM_ALPHA = (2.0 * DEPTH) ** 0.25

LANES = 128
SUBLANES = 8
WKV_CHUNK = 64
VMEM_LIMIT = 48 * 1024 * 1024


def _cparams(sem):
    return pltpu.CompilerParams(dimension_semantics=sem, vmem_limit_bytes=VMEM_LIMIT)


def _dot(a, b, precision=None):
    return jnp.dot(a, b, preferred_element_type=F32, precision=precision)


def _dot_nt(a, b, precision=None):
    return lax.dot_general(a, b, (((1,), (1,)), ((), ())), preferred_element_type=F32,
                           precision=precision)


def _dot_tn(a, b, precision=None):
    return lax.dot_general(a, b, (((0,), (0,)), ((), ())), preferred_element_type=F32,
                           precision=precision)


def _layer_norm(z, g, b):
    mu = jnp.mean(z, -1, keepdims=True)
    zc = z - mu
    var = jnp.mean(zc * zc, -1, keepdims=True)
    return zc * lax.rsqrt(var + LN_EPS) * g + b


def _proj_kernel(x_ref, w_ref, cos_ref, sin_ref, qkv_ref, pr_ref):
    xb = x_ref[...].astype(BF16)
    cos = cos_ref[...]
    sin = sin_ref[...]
    lane = lax.broadcasted_iota(jnp.int32, cos.shape, 1)
    first_half = (lane & (HEAD_DIM // 2)) == 0

    def rope(t):
        rot = jnp.where(first_half, pltpu.roll(t, LANES - HEAD_DIM // 2, 1),
                        pltpu.roll(t, HEAD_DIM // 2, 1))
        return t * cos + rot * sin

    for c in range(0, ATTN_COLS, 2 * LANES):
        acc = _dot(xb, w_ref[:, c:c + 2 * LANES])
        for half in range(2):
            col = c + half * LANES
            t = acc[:, half * LANES:(half + 1) * LANES]
            if col < ATTN_WIDTH + ATTN_KV_WIDTH:
                t = rope(t)
            qkv_ref[:, col:col + LANES] = t
    for c in range(0, RWKV_COLS_PAD, 2 * LANES):
        pr_ref[:, c:c + 2 * LANES] = _dot(xb, w_ref[:, ATTN_COLS + c:ATTN_COLS + c + 2 * LANES])


def _proj(x2, w_in_p, cos_t, sin_t, seq, tm=512):
    t, d = x2.shape
    n_pos = seq // tm
    return pl.pallas_call(
        _proj_kernel,
        grid=(t // tm,),
        in_specs=[
            pl.BlockSpec((tm, d), lambda i: (i, 0)),
            pl.BlockSpec(w_in_p.shape, lambda i: (0, 0)),
            pl.BlockSpec((tm, LANES), lambda i: (i % n_pos, 0)),
            pl.BlockSpec((tm, LANES), lambda i: (i % n_pos, 0)),
        ],
        out_specs=[
            pl.BlockSpec((tm, ATTN_COLS), lambda i: (i, 0)),
            pl.BlockSpec((tm, RWKV_COLS_PAD), lambda i: (i, 0)),
        ],
        out_shape=[
            jax.ShapeDtypeStruct((t, ATTN_COLS), F32),
            jax.ShapeDtypeStruct((t, RWKV_COLS_PAD), F32),
        ],
        compiler_params=_cparams(("parallel",)),
        name="proj",
    )(x2, w_in_p, cos_t, sin_t)


def _attn_kernel(sink_ref, q_ref, kp_ref, kc_ref, kn_ref, vp_ref, vc_ref, vn_ref, o_ref, *, nb):
    n = pl.program_id(1)
    blk = ATTN_BLOCK
    rows = Q_PER_KV * blk
    q = q_ref[0] * (HEAD_DIM ** -0.5)
    kwin = jnp.concatenate([kp_ref[0], kc_ref[0], kn_ref[0]], axis=0)
    vwin = jnp.concatenate([vp_ref[0], vc_ref[0], vn_ref[0]], axis=0)
    qi = lax.broadcasted_iota(jnp.int32, (rows, 3 * blk), 0) & (blk - 1)
    kj = lax.broadcasted_iota(jnp.int32, (rows, 3 * blk), 1)
    dist = kj - qi
    valid = ((dist >= blk - WINDOW) & (dist <= blk + WINDOW)
             & ((kj >= blk) | (n > 0)) & ((kj < 2 * blk) | (n < nb - 1)))
    rowg = lax.broadcasted_iota(jnp.int32, (rows, 1), 0) // blk
    outs = []
    for h in range(N_KV_HEADS):
        qs = jnp.concatenate(
            [q[:, (Q_PER_KV * h + g) * HEAD_DIM:(Q_PER_KV * h + g + 1) * HEAD_DIM]
             for g in range(Q_PER_KV)], axis=0)
        kh = kwin[:, h * HEAD_DIM:(h + 1) * HEAD_DIM]
        vh = vwin[:, h * HEAD_DIM:(h + 1) * HEAD_DIM]
        logits = _dot_nt(qs.astype(BF16), kh.astype(BF16))
        logits = jnp.where(valid, logits, -1e30)
        sk = jnp.zeros((rows, 1), F32)
        for g in range(Q_PER_KV):
            sk = jnp.where(rowg == g, sink_ref[Q_PER_KV * h + g], sk)
        m = jnp.maximum(jnp.max(logits, -1, keepdims=True), sk)
        e = jnp.exp(logits - m)
        den = jnp.sum(e, -1, keepdims=True) + jnp.exp(sk - m)
        p = e / den
        o = _dot(p.astype(BF16), vh.astype(BF16))
        for g in range(Q_PER_KV):
            outs.append(o[g * blk:(g + 1) * blk])
    o_ref[0] = jnp.concatenate(outs, axis=1)


def _attention(qkv3, sink):
    b, s, _ = qkv3.shape
    blk = ATTN_BLOCK
    nb = s // blk
    kcol = ATTN_WIDTH // LANES
    vcol = kcol + 1

    def spec(col, shift):
        def imap(bi, n):
            return (bi, jnp.clip(n + shift, 0, nb - 1), col)
        return pl.BlockSpec((1, blk, LANES), imap)

    return pl.pallas_call(
        functools.partial(_attn_kernel, nb=nb),
        grid=(b, nb),
        in_specs=[
            pl.BlockSpec(memory_space=pltpu.SMEM),
            pl.BlockSpec((1, blk, ATTN_WIDTH), lambda bi, n: (bi, n, 0)),
            spec(kcol, -1), spec(kcol, 0), spec(kcol, 1),
            spec(vcol, -1), spec(vcol, 0), spec(vcol, 1),
        ],
        out_specs=pl.BlockSpec((1, blk, ATTN_WIDTH), lambda bi, n: (bi, n, 0)),
        out_shape=jax.ShapeDtypeStruct((b, s, ATTN_WIDTH), F32),
        compiler_params=_cparams(("parallel", "parallel")),
        name="attn",
    )(sink, qkv3, qkv3, qkv3, qkv3, qkv3, qkv3, qkv3)


def _softplus(x):
    return jnp.maximum(x, 0.0) + jnp.log(1.0 + jnp.exp(-jnp.abs(x)))


def _sigmoid(x):
    return 1.0 / (1.0 + jnp.exp(-x))


def _prep_kernel(p_ref, hp_ref, hn_ref, mup_ref, mun_ref, wmix_ref, w0_ref, a0_ref, g2_ref,
                 kk_ref, ka_ref, bd_ref,
                 r_o, v_o, kk_o, g_o, lw0_o, lw1_o, kd0_o, kd1_o, b0_o, b1_o, *, tm, seq):
    i = pl.program_id(0)
    row = lax.broadcasted_iota(jnp.int32, (tm, 1), 0)
    seq_start = (i * tm) % seq == 0
    seq_end = ((i + 1) * tm) % seq == 0

    def shifted(c0, c1):
        p = p_ref[:, c0:c1]
        prev_row = jnp.where(seq_start, 0.0, hp_ref[SUBLANES - 1:SUBLANES, c0:c1])
        next_row = jnp.where(seq_end, 0.0, hn_ref[0:1, c0:c1])
        p_prev = jnp.where(row == 0, prev_row, pltpu.roll(p, 1, 0))
        p_next = jnp.where(row == tm - 1, next_row, pltpu.roll(p, tm - 1, 0))
        return p + mup_ref[:, c0:c1] * (p_prev - p) + mun_ref[:, c0:c1] * (p_next - p)

    w = RWKV_WIDTH
    r = shifted(0, w)
    k = shifted(w, 2 * w)
    v = shifted(2 * w, 3 * w)
    codes = shifted(3 * w, 3 * w + LANES)
    gd = shifted(3 * w + LANES, 3 * w + 2 * LANES)
    r_o[...] = r
    v_o[...] = v

    lane = lax.broadcasted_iota(jnp.int32, codes.shape, 1)
    codes = jnp.where(lane < 2 * DECAY_RANK, jnp.tanh(codes), codes)
    mm = _dot(codes, wmix_ref[...], HI)
    g_o[...] = _dot(_sigmoid(gd), g2_ref[...], HI)

    kkv = k * kk_ref[...]
    ss = _dot(kkv * kkv, bd_ref[...], HI)
    kkn = kkv * lax.rsqrt(jnp.maximum(ss, 1e-24))
    kk_o[...] = kkn

    ka = ka_ref[...]
    for d, (lw_o, kd_o, b_o) in enumerate(((lw0_o, kd0_o, b0_o), (lw1_o, kd1_o, b1_o))):
        w_log = -_softplus(-(w0_ref[d:d + 1, :] + mm[:, d * w:(d + 1) * w])) - 0.5
        lw_o[...] = -jnp.exp(w_log)
        a = _sigmoid(a0_ref[d:d + 1, :] + mm[:, (2 + d) * w:(3 + d) * w])
        kd_o[...] = k * (1.0 + (a - 1.0) * ka)
        b_o[...] = kkn * a


def _prep(pr, mup, mun, wmix, w0, a0, g2p, k_k, k_a, bd, seq, tm=256):
    t = pr.shape[0]
    nblk8 = t // SUBLANES
    per = tm // SUBLANES
    full = lambda a: pl.BlockSpec(a.shape, lambda i: (0,) * a.ndim)
    out = pl.BlockSpec((tm, RWKV_WIDTH), lambda i: (i, 0))
    return pl.pallas_call(
        functools.partial(_prep_kernel, tm=tm, seq=seq),
        grid=(t // tm,),
        in_specs=[
            pl.BlockSpec((tm, RWKV_COLS_PAD), lambda i: (i, 0)),
            pl.BlockSpec((SUBLANES, RWKV_COLS_PAD), lambda i: (jnp.maximum(i * per - 1, 0), 0)),
            pl.BlockSpec((SUBLANES, RWKV_COLS_PAD),
                         lambda i: (jnp.minimum((i + 1) * per, nblk8 - 1), 0)),
            full(mup), full(mun), full(wmix), full(w0), full(a0), full(g2p),
            full(k_k), full(k_a), full(bd),
        ],
        out_specs=[out] * 10,
        out_shape=[jax.ShapeDtypeStruct((t, RWKV_WIDTH), F32)] * 10,
        compiler_params=_cparams(("parallel",)),
        name="prep",
    )(pr, pr, pr, mup, mun, wmix, w0, a0, g2p, k_k, k_a, bd)


def _split3(x):
    hi = x.astype(BF16)
    r1 = x - hi.astype(F32)
    mid = r1.astype(BF16)
    lo = (r1 - mid.astype(F32)).astype(BF16)
    return hi, mid, lo


def _wkv_kernel(rf, vf, kkf, lwf, kdf, bf, rb, vb, kkb, lwb, kdb, bb, yf_ref, yb_ref, state):
    c = pl.program_id(1)
    n = WKV_CHUNK
    hd = HEAD_DIM
    nh = N_RWKV_HEADS
    def bmm(a, b, ca, cb):
        return lax.dot_general(a.astype(BF16), b.astype(BF16), (((ca,), (cb,)), ((0,), (0,))),
                               preferred_element_type=F32)

    @pl.when(c == 0)
    def _():
        state[...] = jnp.zeros_like(state)

    ti = lax.broadcasted_iota(jnp.int32, (n, n), 0)
    si = lax.broadcasted_iota(jnp.int32, (n, n), 1)
    dirs = ((rf, vf, kkf, lwf, kdf, bf, si <= ti, si < ti, n - 1),
            (rb, vb, kkb, lwb, kdb, bb, si >= ti, si > ti, 0))
    heads = lambda x: jnp.stack([x[:, j * hd:(j + 1) * hd] for j in range(nh)], 0)
    parts = []
    for r_ref, v_ref, kk_ref, lw_ref, kd_ref, b_ref, incl, strict, last in dirs:
        lw = lw_ref[0]
        tri = incl.astype(BF16)
        cum = sum(_dot(tri, part) for part in _split3(lw))
        e_neg = jnp.exp(-cum)
        e_last = jnp.exp(cum[last:last + 1, :] - cum)
        g_scale = jnp.exp(cum[last:last + 1, :])
        parts.append(dict(
            at=heads((-kk_ref[0] * jnp.exp(cum - lw)).astype(BF16)),
            rt=heads((r_ref[0] * jnp.exp(cum)).astype(BF16)),
            bt=heads((b_ref[0] * e_neg).astype(BF16)),
            kt=heads((kd_ref[0] * e_neg).astype(BF16)),
            bl=heads((b_ref[0] * e_last).astype(BF16)),
            kl=heads((kd_ref[0] * e_last).astype(BF16)),
            v=heads(v_ref[0].astype(BF16)),
            gs=heads(g_scale),
            incl=jnp.broadcast_to(incl[None], (nh, n, n)),
            strict=jnp.broadcast_to(strict[None], (nh, n, n))))
    cat = lambda key: jnp.concatenate([parts[0][key], parts[1][key]], 0)
    at, rt, bt, kt, bl, kl, v, gs = (cat(k) for k in ("at", "rt", "bt", "kt", "bl", "kl", "v", "gs"))
    incl, strict = cat("incl"), cat("strict")
    g0 = state[...]
    g0b = g0.astype(BF16)
    m1 = bmm(jnp.concatenate([at, rt], 1), jnp.concatenate([bt, kt], 1), 2, 2)
    a_ab = jnp.where(strict, m1[:, :n, :n], 0.0)
    a_ak = jnp.where(strict, m1[:, :n, n:], 0.0)
    a_rb = jnp.where(incl, m1[:, n:, :n], 0.0)
    a_rk = jnp.where(incl, m1[:, n:, n:], 0.0)
    tinv = jnp.where((ti == si)[None], 1.0, a_ab)
    pw = a_ab
    for _ in range(int(np.log2(n)) - 1):
        pw = bmm(pw, pw, 2, 1)
        tinv = tinv + bmm(tinv, pw, 2, 1)
    rhs = bmm(a_ak, v, 2, 1) + bmm(at, g0b, 2, 2)
    u = bmm(tinv, rhs, 2, 1)
    uv = jnp.concatenate([u.astype(BF16), v], 1)
    y = bmm(jnp.concatenate([a_rb, a_rk], 2), uv, 2, 1) + bmm(rt, g0b, 2, 2)
    state[...] = g0 * gs + bmm(uv, jnp.concatenate([bl, kl], 1), 1, 1)
    for d, y_ref in enumerate((yf_ref, yb_ref)):
        y_ref[0] = jnp.concatenate([y[d * nh + j] for j in range(nh)], axis=1)


def _wkv(r, v, kk, lw0, lw1, kd0, kd1, b0, b1, batch, seq):
    n = WKV_CHUNK
    nc = seq // n
    shp = (batch, seq, RWKV_WIDTH)
    arrs = [a.reshape(shp) for a in (r, v, kk, lw0, kd0, b0, r, v, kk, lw1, kd1, b1)]
    fwd = pl.BlockSpec((1, n, RWKV_WIDTH), lambda b, c: (b, c, 0))
    bwd = pl.BlockSpec((1, n, RWKV_WIDTH), lambda b, c: (b, nc - 1 - c, 0))
    yf, yb = pl.pallas_call(
        _wkv_kernel,
        grid=(batch, nc),
        in_specs=[fwd] * 6 + [bwd] * 6,
        out_specs=[fwd, bwd],
        out_shape=[jax.ShapeDtypeStruct(shp, F32)] * 2,
        scratch_shapes=[pltpu.VMEM((2 * N_RWKV_HEADS, HEAD_DIM, HEAD_DIM), F32)],
        compiler_params=_cparams(("parallel", "arbitrary")),
        name="wkv",
    )(*arrs)
    return yf.reshape(batch * seq, RWKV_WIDTH), yb.reshape(batch * seq, RWKV_WIDTH)


def _mix_kernel(x_ref, ya_ref, yf_ref, yb_ref, r_ref, v_ref, kd0_ref, kd1_ref, g_ref,
                lng_ref, lnb_ref, rk_ref, bd_ref, wout_ref, l1g_ref, l1b_ref, h_ref):
    bd = bd_ref[...]
    inv = 1.0 / HEAD_DIM
    y = yf_ref[...] + yb_ref[...]
    mu = _dot(y, bd, HI) * inv
    yc = y - mu
    var = _dot(yc * yc, bd, HI) * inv
    yn = yc * lax.rsqrt(var + RWKV_GN_EPS) * lng_ref[...] + lnb_ref[...]
    k_mean = 0.5 * (kd0_ref[...] + kd1_ref[...])
    v = v_ref[...]
    bonus = _dot(r_ref[...] * k_mean * rk_ref[...], bd, HI) * v
    yr = (yn + bonus) * g_ref[...]
    mix = (_dot(ya_ref[...].astype(BF16), wout_ref[:ATTN_WIDTH, :])
           + _dot(yr.astype(BF16), wout_ref[ATTN_WIDTH:, :]))
    h_ref[...] = _layer_norm(DEEPNORM_ALPHA * x_ref[...] + mix, l1g_ref[...], l1b_ref[...])


def _mix(x2, ya, yf, yb, r, v, kd0, kd1, g, lng, lnb, rk, bd, wout, l1g, l1b, tm=256):
    t, d = x2.shape
    full = lambda a: pl.BlockSpec(a.shape, lambda i: (0,) * a.ndim)
    half = pl.BlockSpec((tm, RWKV_WIDTH), lambda i: (i, 0))
    wide = pl.BlockSpec((tm, d), lambda i: (i, 0))
    return pl.pallas_call(
        _mix_kernel,
        grid=(t // tm,),
        in_specs=[wide] + [half] * 8 + [full(lng), full(lnb), full(rk), full(bd), full(wout),
                                        full(l1g), full(l1b)],
        out_specs=wide,
        out_shape=jax.ShapeDtypeStruct((t, d), F32),
        compiler_params=_cparams(("parallel",)),
        name="mix",
    )(x2, ya, yf, yb, r, v, kd0, kd1, g, lng, lnb, rk, bd, wout, l1g, l1b)


def _top_rows(sc, k, payload=None):
    n = sc.shape[0]
    iota = lax.broadcasted_iota(jnp.int32, sc.shape, 0)
    vals, picks = [], []
    for _ in range(k):
        m = jnp.max(sc, axis=0, keepdims=True)
        pos = jnp.min(jnp.where(sc == m, iota, n), axis=0, keepdims=True)
        hit = iota == pos
        vals.append(m)
        if payload is None:
            picks.append(pos)
        else:
            picks.append(jnp.max(jnp.where(hit, payload, -1), axis=0, keepdims=True))
        sc = jnp.where(hit, -jnp.inf, sc)
    return jnp.concatenate(vals, 0), jnp.concatenate(picks, 0)


def _route_kernel(h_ref, wq_ref, keys_ref, idx_ref, gate_ref, q_scr, idx_scr):
    k = PEER_TOPK
    q_scr[...] = _dot(h_ref[...].astype(BF16), wq_ref[...])

    def head(hh, carry):
        tops = []
        for p in range(2):
            col = pl.multiple_of(hh * PEER_QDIM + p * PEER_HALF, PEER_HALF)
            qp = q_scr[:, pl.ds(col, PEER_HALF)]
            sct = _dot_nt(keys_ref[hh, p], qp, HI)
            tops.append(_top_rows(sct, k))
        (s1, i1), (s2, i2) = tops
        cand = jnp.concatenate([s1[i:i + 1] + s2[:k // (i + 1)] for i in range(k)], 0)
        cand_id = jnp.concatenate([i1[i:i + 1] * PEER_NKEYS + i2[:k // (i + 1)] for i in range(k)], 0)
        cs, ids = _top_rows(cand, k, cand_id)
        e = jnp.exp(cs - jnp.max(cs, axis=0, keepdims=True))
        row = pl.multiple_of(hh * k, k)
        gate_ref[pl.ds(row, k), :] = e / jnp.sum(e, axis=0, keepdims=True)
        idx_scr[pl.ds(row, k), :] = ids
        return carry

    lax.fori_loop(0, PEER_HEADS, head, 0)
    idx_ref[...] = idx_scr[...].T


def _route(h1, wq, keys, tm=256):
    t, d = h1.shape
    return pl.pallas_call(
        _route_kernel,
        grid=(t // tm,),
        in_specs=[
            pl.BlockSpec((tm, d), lambda i: (i, 0)),
            pl.BlockSpec(wq.shape, lambda i: (0, 0)),
            pl.BlockSpec(keys.shape, lambda i: (0, 0, 0, 0)),
        ],
        out_specs=[
            pl.BlockSpec((tm, N_SEL), lambda i: (i, 0)),
            pl.BlockSpec((N_SEL, tm), lambda i: (0, i)),
        ],
        out_shape=[
            jax.ShapeDtypeStruct((t, N_SEL), jnp.int32),
            jax.ShapeDtypeStruct((N_SEL, t), F32),
        ],
        scratch_shapes=[pltpu.VMEM((tm, PEER_HEADS * PEER_QDIM), F32),
                        pltpu.VMEM((N_SEL, tm), jnp.int32)],
        compiler_params=_cparams(("parallel",)),
        name="route",
    )(h1, wq, keys)


def _peer_kernel(idx_ref, gate_ref, h_ref, uv_hbm, l2g_ref, l2b_ref, o_ref, buf, f_scr, sem, *, tb):
    d = h_ref.shape[1]
    nk = d // LANES

    def issue(t, slot):
        for j in range(N_SEL):
            pltpu.make_async_copy(uv_hbm.at[idx_ref[t, j]], buf.at[slot, :, pl.ds(j, 1), :],
                                  sem.at[slot]).start()

    def wait_all(slot):
        pltpu.make_async_copy(buf.at[slot], buf.at[slot], sem.at[slot]).wait()

    issue(0, 0)
    lane = lax.broadcasted_iota(jnp.int32, (N_SEL, tb), 1)

    def body(t, carry):
        slot = t % 2

        @pl.when(t + 1 < tb)
        def _():
            issue(t + 1, 1 - slot)

        wait_all(slot)
        hrow = h_ref[pl.ds(t, 1), :]
        acc = buf[slot, 0] * hrow[:, 0:LANES]
        for k in range(1, nk):
            acc = acc + buf[slot, k] * hrow[:, k * LANES:(k + 1) * LANES]
        z = jnp.sum(acc, axis=1, keepdims=True)
        gcol = jnp.sum(jnp.where(lane == t, gate_ref[...], 0.0), axis=1, keepdims=True)
        coef = gcol * (0.5 * z * (1.0 + lax.erf(z * (2.0 ** -0.5))))
        f_scr[pl.ds(t, 1), :] = jnp.concatenate(
            [jnp.sum(coef * buf[slot, nk + k], axis=0, keepdims=True) for k in range(nk)], axis=1)
        return carry

    lax.fori_loop(0, tb, body, 0)
    o_ref[...] = _layer_norm(DEEPNORM_ALPHA * h_ref[...] + f_scr[...], l2g_ref[...], l2b_ref[...])


def _peer(idx, gate_t, h1, uv, l2g, l2b, tb=128):
    t, d = h1.shape
    return pl.pallas_call(
        functools.partial(_peer_kernel, tb=tb),
        grid=(t // tb,),
        in_specs=[
            pl.BlockSpec((tb, N_SEL), lambda i: (i, 0), memory_space=pltpu.SMEM),
            pl.BlockSpec((N_SEL, tb), lambda i: (0, i)),
            pl.BlockSpec((tb, d), lambda i: (i, 0)),
            pl.BlockSpec(memory_space=pl.ANY),
            pl.BlockSpec(l2g.shape, lambda i: (0, 0)),
            pl.BlockSpec(l2b.shape, lambda i: (0, 0)),
        ],
        out_specs=pl.BlockSpec((tb, d), lambda i: (i, 0)),
        out_shape=jax.ShapeDtypeStruct((t, d), F32),
        scratch_shapes=[pltpu.VMEM((2, 2 * d // LANES, N_SEL, LANES), F32),
                        pltpu.VMEM((tb, d), F32),
                        pltpu.SemaphoreType.DMA((2,))],
        compiler_params=_cparams(("arbitrary",)),
        name="peer",
    )(idx, gate_t, h1, uv, l2g, l2b)


def _rope_tables(seq):
    pos = jnp.arange(seq, dtype=F32)
    inv_freq = ROPE_THETA ** (-jnp.arange(0, HEAD_DIM, 2, dtype=F32) / HEAD_DIM)
    ang = pos[:, None] * inv_freq[None, :]
    ang = jnp.concatenate([ang, ang], -1)
    sign = jnp.concatenate([-jnp.ones((HEAD_DIM // 2,), F32), jnp.ones((HEAD_DIM // 2,), F32)])
    reps = LANES // HEAD_DIM
    return jnp.tile(jnp.cos(ang), (1, reps)), jnp.tile(jnp.sin(ang) * sign, (1, reps))


def _layer(h2d, batch, seq, w_in, mu_prev, mu_next, w0, w2, a0, a2, g2, k_k, k_a, r_k, lnx_g, lnx_b,
           sink, w_out, ln1_g, ln1_b, peer_wq, peer_keys, peer_u, peer_v, ln2_g, ln2_b):
    w = RWKV_WIDTH
    row = lambda a: a.reshape(1, -1).astype(F32)
    w_in_p = jnp.pad(w_in, ((0, 0), (0, RWKV_COLS_PAD - RWKV_COLS))).astype(BF16)
    mup = jnp.pad(mu_prev, (0, RWKV_COLS_PAD - RWKV_COLS)).reshape(1, -1)
    mun = jnp.pad(mu_next, (0, RWKV_COLS_PAD - RWKV_COLS)).reshape(1, -1)
    wmix = jnp.zeros((LANES, 4 * w), F32)
    for d in range(2):
        wmix = wmix.at[d * DECAY_RANK:(d + 1) * DECAY_RANK, d * w:(d + 1) * w].set(w2[d])
        r0 = 2 * DECAY_RANK + d * ICLR_RANK
        wmix = wmix.at[r0:r0 + ICLR_RANK, (2 + d) * w:(3 + d) * w].set(a2[d])
    g2p = jnp.pad(g2, ((0, LANES - GATE_RANK), (0, 0)))
    head_of = np.arange(w) // HEAD_DIM
    bd = jnp.asarray(head_of[:, None] == head_of[None, :], F32)
    cos_t, sin_t = _rope_tables(seq)

    qkv, pr = _proj(h2d, w_in_p, cos_t, sin_t, seq)
    y_attn = _attention(qkv.reshape(batch, seq, ATTN_COLS), sink.astype(F32))
    r, v, kk, g, lw0, lw1, kd0, kd1, b0, b1 = _prep(
        pr, mup, mun, wmix, w0.astype(F32), a0.astype(F32), g2p, row(k_k), row(k_a), bd, seq)
    yf, yb = _wkv(r, v, kk, lw0, lw1, kd0, kd1, b0, b1, batch, seq)
    h1 = _mix(h2d, y_attn.reshape(batch * seq, ATTN_WIDTH), yf, yb, r, v, kd0, kd1, g,
              row(lnx_g), row(lnx_b), row(r_k), bd, w_out.astype(BF16), row(ln1_g), row(ln1_b))
    idx, gate_t = _route(h1, peer_wq.astype(BF16), peer_keys.astype(F32))
    uv = jnp.concatenate([peer_u, peer_v], axis=1)
    uv = uv.reshape(uv.shape[0], uv.shape[1] // LANES, 1, LANES)
    return _peer(idx, gate_t, h1, uv, row(ln2_g), row(ln2_b))


def kernel(x, w_in, mu_prev, mu_next, w0, w2, a0, a2, g2, k_k, k_a, r_k, lnx_g, lnx_b, sink, w_out,
           ln1_g, ln1_b, peer_wq, peer_keys, peer_u, peer_v, ln2_g, ln2_b):
    batch, seq, d = x.shape
    h = x.reshape(batch * seq, d)
    for l in range(DEPTH):
        h = _layer(h, batch, seq, w_in[l], mu_prev[l], mu_next[l], w0[l], w2[l], a0[l], a2[l], g2[l],
                   k_k[l], k_a[l], r_k[l], lnx_g[l], lnx_b[l], sink[l], w_out[l], ln1_g[l], ln1_b[l],
                   peer_wq[l], peer_keys[l], peer_u[l], peer_v[l], ln2_g[l], ln2_b[l])
    return h.reshape(batch, seq, d)
```

```python
import functools

import numpy as np
import jax
import jax.numpy as jnp
from jax import lax
from jax.experimental import pallas as pl
from jax.experimental.pallas import tpu as pltpu
from jax.experimental.pallas import tpu_sc as plsc

F32 = jnp.float32
BF16 = jnp.bfloat16
HI = lax.Precision.HIGHEST

HEAD_DIM = 64
N_Q_HEADS = 8
N_KV_HEADS = 2
Q_PER_KV = N_Q_HEADS // N_KV_HEADS
ATTN_WIDTH = N_Q_HEADS * HEAD_DIM
ATTN_KV_WIDTH = N_KV_HEADS * HEAD_DIM
ATTN_COLS = ATTN_WIDTH + 2 * ATTN_KV_WIDTH
WINDOW = 128
ATTN_BLOCK = 128
ROPE_THETA = 10000.0
N_RWKV_HEADS = 8
RWKV_WIDTH = N_RWKV_HEADS * HEAD_DIM
DECAY_RANK = 32
ICLR_RANK = 32
GATE_RANK = 96
RWKV_COLS = 3 * RWKV_WIDTH + 2 * DECAY_RANK + 2 * ICLR_RANK + GATE_RANK
RWKV_COLS_PAD = 1792
RWKV_GN_EPS = 64e-5
PEER_HEADS = 8
PEER_NKEYS = 128
PEER_QDIM = 256
PEER_HALF = PEER_QDIM // 2
PEER_TOPK = 16
N_SEL = PEER_HEADS * PEER_TOPK
LN_EPS = 1e-5
DEPTH = 1
DEEPNORM_ALPHA = (2.0 * DEPTH) ** 0.25

LANES = 128
SUBLANES = 8
WKV_CHUNK = 64
VMEM_LIMIT = 48 * 1024 * 1024
PEER_TB = 128
SC_LANES = 16
SC_WORKERS = 32
SC_GROUP = 32
SC_CHUNK_UNROLL = 8
SC_TOKEN_SHARE = 17 / 32
SC_SPLIT_ALIGN = 1024
TC_PEER_CALLS = 2


def _cparams(sem):
    return pltpu.CompilerParams(dimension_semantics=sem, vmem_limit_bytes=VMEM_LIMIT)


def _dot(a, b, precision=None):
    return jnp.dot(a, b, preferred_element_type=F32, precision=precision)


def _dot_nt(a, b, precision=None):
    return lax.dot_general(a, b, (((1,), (1,)), ((), ())), preferred_element_type=F32,
                           precision=precision)


def _dot_tn(a, b, precision=None):
    return lax.dot_general(a, b, (((0,), (0,)), ((), ())), preferred_element_type=F32,
                           precision=precision)


def _layer_norm(z, g, b):
    mu = jnp.mean(z, -1, keepdims=True)
    zc = z - mu
    var = jnp.mean(zc * zc, -1, keepdims=True)
    return zc * lax.rsqrt(var + LN_EPS) * g + b


def _proj_kernel(x_ref, w_ref, cos_ref, sin_ref, qkv_ref, pr_ref):
    xb = x_ref[...].astype(BF16)
    cos = cos_ref[...]
    sin = sin_ref[...]
    lane = lax.broadcasted_iota(jnp.int32, cos.shape, 1)
    first_half = (lane & (HEAD_DIM // 2)) == 0

    def rope(t):
        rot = jnp.where(first_half, pltpu.roll(t, LANES - HEAD_DIM // 2, 1),
                        pltpu.roll(t, HEAD_DIM // 2, 1))
        return t * cos + rot * sin

    for c in range(0, ATTN_COLS, 2 * LANES):
        acc = _dot(xb, w_ref[:, c:c + 2 * LANES])
        for half in range(2):
            col = c + half * LANES
            t = acc[:, half * LANES:(half + 1) * LANES]
            if col < ATTN_WIDTH + ATTN_KV_WIDTH:
                t = rope(t)
            qkv_ref[:, col:col + LANES] = t
    for c in range(0, RWKV_COLS_PAD, 2 * LANES):
        pr_ref[:, c:c + 2 * LANES] = _dot(xb, w_ref[:, ATTN_COLS + c:ATTN_COLS + c + 2 * LANES])


def _proj(x2, w_in_p, cos_t, sin_t, seq, tm=512):
    t, d = x2.shape
    n_pos = seq // tm
    return pl.pallas_call(
        _proj_kernel,
        grid=(t // tm,),
        in_specs=[
            pl.BlockSpec((tm, d), lambda i: (i, 0)),
            pl.BlockSpec(w_in_p.shape, lambda i: (0, 0)),
            pl.BlockSpec((tm, LANES), lambda i: (i % n_pos, 0)),
            pl.BlockSpec((tm, LANES), lambda i: (i % n_pos, 0)),
        ],
        out_specs=[
            pl.BlockSpec((tm, ATTN_COLS), lambda i: (i, 0)),
            pl.BlockSpec((tm, RWKV_COLS_PAD), lambda i: (i, 0)),
        ],
        out_shape=[
            jax.ShapeDtypeStruct((t, ATTN_COLS), F32),
            jax.ShapeDtypeStruct((t, RWKV_COLS_PAD), F32),
        ],
        compiler_params=_cparams(("parallel",)),
        name="proj",
    )(x2, w_in_p, cos_t, sin_t)


def _attn_kernel(sink_ref, q_ref, kp_ref, kc_ref, kn_ref, vp_ref, vc_ref, vn_ref, o_ref, *, nb):
    n = pl.program_id(1)
    blk = ATTN_BLOCK
    rows = Q_PER_KV * blk
    q = q_ref[0] * (HEAD_DIM ** -0.5)
    kwin = jnp.concatenate([kp_ref[0], kc_ref[0], kn_ref[0]], axis=0)
    vwin = jnp.concatenate([vp_ref[0], vc_ref[0], vn_ref[0]], axis=0)
    qi = lax.broadcasted_iota(jnp.int32, (rows, 3 * blk), 0) & (blk - 1)
    kj = lax.broadcasted_iota(jnp.int32, (rows, 3 * blk), 1)
    dist = kj - qi
    valid = ((dist >= blk - WINDOW) & (dist <= blk + WINDOW)
             & ((kj >= blk) | (n > 0)) & ((kj < 2 * blk) | (n < nb - 1)))
    rowg = lax.broadcasted_iota(jnp.int32, (rows, 1), 0) // blk
    outs = []
    for h in range(N_KV_HEADS):
        qs = jnp.concatenate(
            [q[:, (Q_PER_KV * h + g) * HEAD_DIM:(Q_PER_KV * h + g + 1) * HEAD_DIM]
             for g in range(Q_PER_KV)], axis=0)
        kh = kwin[:, h * HEAD_DIM:(h + 1) * HEAD_DIM]
        vh = vwin[:, h * HEAD_DIM:(h + 1) * HEAD_DIM]
        logits = _dot_nt(qs.astype(BF16), kh.astype(BF16))
        logits = jnp.where(valid, logits, -1e30)
        sk = jnp.zeros((rows, 1), F32)
        for g in range(Q_PER_KV):
            sk = jnp.where(rowg == g, sink_ref[Q_PER_KV * h + g], sk)
        m = jnp.maximum(jnp.max(logits, -1, keepdims=True), sk)
        e = jnp.exp(logits - m)
        den = jnp.sum(e, -1, keepdims=True) + jnp.exp(sk - m)
        p = e / den
        o = _dot(p.astype(BF16), vh.astype(BF16))
        for g in range(Q_PER_KV):
            outs.append(o[g * blk:(g + 1) * blk])
    o_ref[0] = jnp.concatenate(outs, axis=1)


def _attention(qkv3, sink):
    b, s, _ = qkv3.shape
    blk = ATTN_BLOCK
    nb = s // blk
    kcol = ATTN_WIDTH // LANES
    vcol = kcol + 1

    def spec(col, shift):
        def imap(bi, n):
            return (bi, jnp.clip(n + shift, 0, nb - 1), col)
        return pl.BlockSpec((1, blk, LANES), imap)

    return pl.pallas_call(
        functools.partial(_attn_kernel, nb=nb),
        grid=(b, nb),
        in_specs=[
            pl.BlockSpec(memory_space=pltpu.SMEM),
            pl.BlockSpec((1, blk, ATTN_WIDTH), lambda bi, n: (bi, n, 0)),
            spec(kcol, -1), spec(kcol, 0), spec(kcol, 1),
            spec(vcol, -1), spec(vcol, 0), spec(vcol, 1),
        ],
        out_specs=pl.BlockSpec((1, blk, ATTN_WIDTH), lambda bi, n: (bi, n, 0)),
        out_shape=jax.ShapeDtypeStruct((b, s, ATTN_WIDTH), F32),
        compiler_params=_cparams(("parallel", "parallel")),
        name="attn",
    )(sink, qkv3, qkv3, qkv3, qkv3, qkv3, qkv3, qkv3)


def _softplus(x):
    return jnp.maximum(x, 0.0) + jnp.log(1.0 + jnp.exp(-jnp.abs(x)))


def _sigmoid(x):
    return 1.0 / (1.0 + jnp.exp(-x))


def _prep_kernel(p_ref, hp_ref, hn_ref, mup_ref, mun_ref, wmix_ref, w0_ref, a0_ref, g2_ref,
                 kk_ref, ka_ref, bd_ref,
                 r_o, v_o, kk_o, g_o, lw0_o, lw1_o, kd0_o, kd1_o, b0_o, b1_o, *, tm, seq):
    i = pl.program_id(0)
    row = lax.broadcasted_iota(jnp.int32, (tm, 1), 0)
    seq_start = (i * tm) % seq == 0
    seq_end = ((i + 1) * tm) % seq == 0

    def shifted(c0, c1):
        p = p_ref[:, c0:c1]
        prev_row = jnp.where(seq_start, 0.0, hp_ref[SUBLANES - 1:SUBLANES, c0:c1])
        next_row = jnp.where(seq_end, 0.0, hn_ref[0:1, c0:c1])
        p_prev = jnp.where(row == 0, prev_row, pltpu.roll(p, 1, 0))
        p_next = jnp.where(row == tm - 1, next_row, pltpu.roll(p, tm - 1, 0))
        return p + mup_ref[:, c0:c1] * (p_prev - p) + mun_ref[:, c0:c1] * (p_next - p)

    w = RWKV_WIDTH
    r = shifted(0, w)
    k = shifted(w, 2 * w)
    v = shifted(2 * w, 3 * w)
    codes = shifted(3 * w, 3 * w + LANES)
    gd = shifted(3 * w + LANES, 3 * w + 2 * LANES)
    r_o[...] = r
    v_o[...] = v

    lane = lax.broadcasted_iota(jnp.int32, codes.shape, 1)
    codes = jnp.where(lane < 2 * DECAY_RANK, jnp.tanh(codes), codes)
    mm = _dot(codes, wmix_ref[...], HI)
    g_o[...] = _dot(_sigmoid(gd), g2_ref[...], HI)

    kkv = k * kk_ref[...]
    ss = _dot(kkv * kkv, bd_ref[...], HI)
    kkn = kkv * lax.rsqrt(jnp.maximum(ss, 1e-24))
    kk_o[...] = kkn

    ka = ka_ref[...]
    for d, (lw_o, kd_o, b_o) in enumerate(((lw0_o, kd0_o, b0_o), (lw1_o, kd1_o, b1_o))):
        w_log = -_softplus(-(w0_ref[d:d + 1, :] + mm[:, d * w:(d + 1) * w])) - 0.5
        lw_o[...] = -jnp.exp(w_log)
        a = _sigmoid(a0_ref[d:d + 1, :] + mm[:, (2 + d) * w:(3 + d) * w])
        kd_o[...] = k * (1.0 + (a - 1.0) * ka)
        b_o[...] = kkn * a


def _prep(pr, mup, mun, wmix, w0, a0, g2p, k_k, k_a, bd, seq, tm=256):
    t = pr.shape[0]
    nblk8 = t // SUBLANES
    per = tm // SUBLANES
    full = lambda a: pl.BlockSpec(a.shape, lambda i: (0,) * a.ndim)
    out = pl.BlockSpec((tm, RWKV_WIDTH), lambda i: (i, 0))
    return pl.pallas_call(
        functools.partial(_prep_kernel, tm=tm, seq=seq),
        grid=(t // tm,),
        in_specs=[
            pl.BlockSpec((tm, RWKV_COLS_PAD), lambda i: (i, 0)),
            pl.BlockSpec((SUBLANES, RWKV_COLS_PAD), lambda i: (jnp.maximum(i * per - 1, 0), 0)),
            pl.BlockSpec((SUBLANES, RWKV_COLS_PAD),
                         lambda i: (jnp.minimum((i + 1) * per, nblk8 - 1), 0)),
            full(mup), full(mun), full(wmix), full(w0), full(a0), full(g2p),
            full(k_k), full(k_a), full(bd),
        ],
        out_specs=[out] * 10,
        out_shape=[jax.ShapeDtypeStruct((t, RWKV_WIDTH), F32)] * 10,
        compiler_params=_cparams(("parallel",)),
        name="prep",
    )(pr, pr, pr, mup, mun, wmix, w0, a0, g2p, k_k, k_a, bd)


def _split3(x):
    hi = x.astype(BF16)
    r1 = x - hi.astype(F32)
    mid = r1.astype(BF16)
    lo = (r1 - mid.astype(F32)).astype(BF16)
    return hi, mid, lo


def _wkv_kernel(rf, vf, kkf, lwf, kdf, bf, rb, vb, kkb, lwb, kdb, bb, yf_ref, yb_ref, state):
    c = pl.program_id(1)
    n = WKV_CHUNK
    hd = HEAD_DIM
    nh = N_RWKV_HEADS
    def bmm(a, b, ca, cb):
        return lax.dot_general(a.astype(BF16), b.astype(BF16), (((ca,), (cb,)), ((0,), (0,))),
                               preferred_element_type=F32)

    @pl.when(c == 0)
    def _():
        state[...] = jnp.zeros_like(state)

    ti = lax.broadcasted_iota(jnp.int32, (n, n), 0)
    si = lax.broadcasted_iota(jnp.int32, (n, n), 1)
    dirs = ((rf, vf, kkf, lwf, kdf, bf, si <= ti, si < ti, n - 1),
            (rb, vb, kkb, lwb, kdb, bb, si >= ti, si > ti, 0))
    heads = lambda x: jnp.stack([x[:, j * hd:(j + 1) * hd] for j in range(nh)], 0)
    parts = []
    for r_ref, v_ref, kk_ref, lw_ref, kd_ref, b_ref, incl, strict, last in dirs:
        lw = lw_ref[0]
        tri = incl.astype(BF16)
        cum = sum(_dot(tri, part) for part in _split3(lw))
        e_neg = jnp.exp(-cum)
        e_last = jnp.exp(cum[last:last + 1, :] - cum)
        g_scale = jnp.exp(cum[last:last + 1, :])
        parts.append(dict(
            at=heads((-kk_ref[0] * jnp.exp(cum - lw)).astype(BF16)),
            rt=heads((r_ref[0] * jnp.exp(cum)).astype(BF16)),
            bt=heads((b_ref[0] * e_neg).astype(BF16)),
            kt=heads((kd_ref[0] * e_neg).astype(BF16)),
            bl=heads((b_ref[0] * e_last).astype(BF16)),
            kl=heads((kd_ref[0] * e_last).astype(BF16)),
            v=heads(v_ref[0].astype(BF16)),
            gs=heads(g_scale),
            incl=jnp.broadcast_to(incl[None], (nh, n, n)),
            strict=jnp.broadcast_to(strict[None], (nh, n, n))))
    cat = lambda key: jnp.concatenate([parts[0][key], parts[1][key]], 0)
    at, rt, bt, kt, bl, kl, v, gs = (cat(k) for k in ("at", "rt", "bt", "kt", "bl", "kl", "v", "gs"))
    incl, strict = cat("incl"), cat("strict")
    g0 = state[...]
    g0b = g0.astype(BF16)
    m1 = bmm(jnp.concatenate([at, rt], 1), jnp.concatenate([bt, kt], 1), 2, 2)
    a_ab = jnp.where(strict, m1[:, :n, :n], 0.0)
    a_ak = jnp.where(strict, m1[:, :n, n:], 0.0)
    a_rb = jnp.where(incl, m1[:, n:, :n], 0.0)
    a_rk = jnp.where(incl, m1[:, n:, n:], 0.0)
    tinv = jnp.where((ti == si)[None], 1.0, a_ab)
    pw = a_ab
    for _ in range(int(np.log2(n)) - 1):
        pw = bmm(pw, pw, 2, 1)
        tinv = tinv + bmm(tinv, pw, 2, 1)
    rhs = bmm(a_ak, v, 2, 1) + bmm(at, g0b, 2, 2)
    u = bmm(tinv, rhs, 2, 1)
    uv = jnp.concatenate([u.astype(BF16), v], 1)
    y = bmm(jnp.concatenate([a_rb, a_rk], 2), uv, 2, 1) + bmm(rt, g0b, 2, 2)
    state[...] = g0 * gs + bmm(uv, jnp.concatenate([bl, kl], 1), 1, 1)
    for d, y_ref in enumerate((yf_ref, yb_ref)):
        y_ref[0] = jnp.concatenate([y[d * nh + j] for j in range(nh)], axis=1)


def _wkv(r, v, kk, lw0, lw1, kd0, kd1, b0, b1, batch, seq):
    n = WKV_CHUNK
    nc = seq // n
    shp = (batch, seq, RWKV_WIDTH)
    arrs = [a.reshape(shp) for a in (r, v, kk, lw0, kd0, b0, r, v, kk, lw1, kd1, b1)]
    fwd = pl.BlockSpec((1, n, RWKV_WIDTH), lambda b, c: (b, c, 0))
    bwd = pl.BlockSpec((1, n, RWKV_WIDTH), lambda b, c: (b, nc - 1 - c, 0))
    yf, yb = pl.pallas_call(
        _wkv_kernel,
        grid=(batch, nc),
        in_specs=[fwd] * 6 + [bwd] * 6,
        out_specs=[fwd, bwd],
        out_shape=[jax.ShapeDtypeStruct(shp, F32)] * 2,
        scratch_shapes=[pltpu.VMEM((2 * N_RWKV_HEADS, HEAD_DIM, HEAD_DIM), F32)],
        compiler_params=_cparams(("parallel", "arbitrary")),
        name="wkv",
    )(*arrs)
    return yf.reshape(batch * seq, RWKV_WIDTH), yb.reshape(batch * seq, RWKV_WIDTH)


def _mix_kernel(x_ref, ya_ref, yf_ref, yb_ref, r_ref, v_ref, kd0_ref, kd1_ref, g_ref,
                lng_ref, lnb_ref, rk_ref, bd_ref, wout_ref, l1g_ref, l1b_ref, h_ref):
    bd = bd_ref[...]
    inv = 1.0 / HEAD_DIM
    y = yf_ref[...] + yb_ref[...]
    mu = _dot(y, bd, HI) * inv
    yc = y - mu
    var = _dot(yc * yc, bd, HI) * inv
    yn = yc * lax.rsqrt(var + RWKV_GN_EPS) * lng_ref[...] + lnb_ref[...]
    k_mean = 0.5 * (kd0_ref[...] + kd1_ref[...])
    v = v_ref[...]
    bonus = _dot(r_ref[...] * k_mean * rk_ref[...], bd, HI) * v
    yr = (yn + bonus) * g_ref[...]
    mix = (_dot(ya_ref[...].astype(BF16), wout_ref[:ATTN_WIDTH, :])
           + _dot(yr.astype(BF16), wout_ref[ATTN_WIDTH:, :]))
    h_ref[...] = _layer_norm(DEEPNORM_ALPHA * x_ref[...] + mix, l1g_ref[...], l1b_ref[...])


def _mix(x2, ya, yf, yb, r, v, kd0, kd1, g, lng, lnb, rk, bd, wout, l1g, l1b, tm=256):
    t, d = x2.shape
    full = lambda a: pl.BlockSpec(a.shape, lambda i: (0,) * a.ndim)
    half = pl.BlockSpec((tm, RWKV_WIDTH), lambda i: (i, 0))
    wide = pl.BlockSpec((tm, d), lambda i: (i, 0))
    return pl.pallas_call(
        _mix_kernel,
        grid=(t // tm,),
        in_specs=[wide] + [half] * 8 + [full(lng), full(lnb), full(rk), full(bd), full(wout),
                                        full(l1g), full(l1b)],
        out_specs=wide,
        out_shape=jax.ShapeDtypeStruct((t, d), F32),
        compiler_params=_cparams(("parallel",)),
        name="mix",
    )(x2, ya, yf, yb, r, v, kd0, kd1, g, lng, lnb, rk, bd, wout, l1g, l1b)


def _top_rows(sc, k, payload=None):
    n = sc.shape[0]
    iota = lax.broadcasted_iota(jnp.int32, sc.shape, 0)
    vals, picks = [], []
    for _ in range(k):
        m = jnp.max(sc, axis=0, keepdims=True)
        pos = jnp.min(jnp.where(sc == m, iota, n), axis=0, keepdims=True)
        hit = iota == pos
        vals.append(m)
        if payload is None:
            picks.append(pos)
        else:
            picks.append(jnp.max(jnp.where(hit, payload, -1), axis=0, keepdims=True))
        sc = jnp.where(hit, -jnp.inf, sc)
    return jnp.concatenate(vals, 0), jnp.concatenate(picks, 0)


def _route_kernel(h_ref, wq_ref, keys_ref, idx_ref, gate_ref, gate_tok_ref, q_scr, idx_scr):
    k = PEER_TOPK
    q_scr[...] = _dot(h_ref[...].astype(BF16), wq_ref[...])

    def head(hh, carry):
        tops = []
        for p in range(2):
            col = pl.multiple_of(hh * PEER_QDIM + p * PEER_HALF, PEER_HALF)
            qp = q_scr[:, pl.ds(col, PEER_HALF)]
            sct = _dot_nt(keys_ref[hh, p], qp, HI)
            tops.append(_top_rows(sct, k))
        (s1, i1), (s2, i2) = tops
        cand = jnp.concatenate([s1[i:i + 1] + s2[:k // (i + 1)] for i in range(k)], 0)
        cand_id = jnp.concatenate([i1[i:i + 1] * PEER_NKEYS + i2[:k // (i + 1)] for i in range(k)], 0)
        cs, ids = _top_rows(cand, k, cand_id)
        e = jnp.exp(cs - jnp.max(cs, axis=0, keepdims=True))
        row = pl.multiple_of(hh * k, k)
        gate_ref[pl.ds(row, k), :] = e / jnp.sum(e, axis=0, keepdims=True)
        idx_scr[pl.ds(row, k), :] = ids
        return carry

    lax.fori_loop(0, PEER_HEADS, head, 0)
    idx_ref[...] = idx_scr[...].T
    gate_tok_ref[...] = gate_ref[...].T


def _route(h1, wq, keys, tm=256):
    t, d = h1.shape
    return pl.pallas_call(
        _route_kernel,
        grid=(t // tm,),
        in_specs=[
            pl.BlockSpec((tm, d), lambda i: (i, 0)),
            pl.BlockSpec(wq.shape, lambda i: (0, 0)),
            pl.BlockSpec(keys.shape, lambda i: (0, 0, 0, 0)),
        ],
        out_specs=[
            pl.BlockSpec((tm, N_SEL), lambda i: (i, 0)),
            pl.BlockSpec((N_SEL, tm), lambda i: (0, i)),
            pl.BlockSpec((tm, N_SEL), lambda i: (i, 0)),
        ],
        out_shape=[
            jax.ShapeDtypeStruct((t, N_SEL), jnp.int32),
            jax.ShapeDtypeStruct((N_SEL, t), F32),
            jax.ShapeDtypeStruct((t, N_SEL), F32),
        ],
        scratch_shapes=[pltpu.VMEM((tm, PEER_HEADS * PEER_QDIM), F32),
                        pltpu.VMEM((N_SEL, tm), jnp.int32)],
        compiler_params=_cparams(("parallel",)),
        name="route",
    )(h1, wq, keys)


def _peer_kernel(idx_ref, gate_ref, h_ref, uv_hbm, l2g_ref, l2b_ref, o_ref, buf, f_scr, sem, *, tb):
    d = h_ref.shape[1]
    nk = d // LANES

    def issue(t, slot):
        for j in range(N_SEL):
            pltpu.make_async_copy(uv_hbm.at[idx_ref[t, j]], buf.at[slot, :, pl.ds(j, 1), :],
                                  sem.at[slot]).start()

    def wait_all(slot):
        pltpu.make_async_copy(buf.at[slot], buf.at[slot], sem.at[slot]).wait()

    issue(0, 0)
    lane = lax.broadcasted_iota(jnp.int32, (N_SEL, tb), 1)

    def body(t, carry):
        slot = t % 2

        @pl.when(t + 1 < tb)
        def _():
            issue(t + 1, 1 - slot)

        wait_all(slot)
        hrow = h_ref[pl.ds(t, 1), :]
        acc = buf[slot, 0] * hrow[:, 0:LANES]
        for k in range(1, nk):
            acc = acc + buf[slot, k] * hrow[:, k * LANES:(k + 1) * LANES]
        z = jnp.sum(acc, axis=1, keepdims=True)
        gcol = jnp.sum(jnp.where(lane == t, gate_ref[...], 0.0), axis=1, keepdims=True)
        coef = gcol * (0.5 * z * (1.0 + lax.erf(z * (2.0 ** -0.5))))
        f_scr[pl.ds(t, 1), :] = jnp.concatenate(
            [jnp.sum(coef * buf[slot, nk + k], axis=0, keepdims=True) for k in range(nk)], axis=1)
        return carry

    lax.fori_loop(0, tb, body, 0)
    o_ref[...] = _layer_norm(DEEPNORM_ALPHA * h_ref[...] + f_scr[...], l2g_ref[...], l2b_ref[...])


def _peer(idx, gate_t, h1, uv, l2g, l2b, tok0, ntok, tb=PEER_TB):
    d = h1.shape[1]
    b0 = tok0 // tb
    return pl.pallas_call(
        functools.partial(_peer_kernel, tb=tb),
        grid=(ntok // tb,),
        in_specs=[
            pl.BlockSpec((tb, N_SEL), lambda i: (b0 + i, 0), memory_space=pltpu.SMEM),
            pl.BlockSpec((N_SEL, tb), lambda i: (0, b0 + i)),
            pl.BlockSpec((tb, d), lambda i: (b0 + i, 0)),
            pl.BlockSpec(memory_space=pl.ANY),
            pl.BlockSpec(l2g.shape, lambda i: (0, 0)),
            pl.BlockSpec(l2b.shape, lambda i: (0, 0)),
        ],
        out_specs=pl.BlockSpec((tb, d), lambda i: (i, 0)),
        out_shape=jax.ShapeDtypeStruct((ntok, d), F32),
        scratch_shapes=[pltpu.VMEM((2, 2 * d // LANES, N_SEL, LANES), F32),
                        pltpu.VMEM((tb, d), F32),
                        pltpu.SemaphoreType.DMA((2,))],
        compiler_params=_cparams(("arbitrary",)),
        name="peer",
    )(idx, gate_t, h1, uv, l2g, l2b)


def _sc_mesh():
    return plsc.VectorSubcoreMesh(core_axis_name="c", subcore_axis_name="s")


def _worker_id():
    return lax.axis_index("s") * 2 + lax.axis_index("c")


def _sc_z_body(u_hbm, idx_hbm, h_hbm, z_hbm, idx_v, h_v, rows_v, z_v, sem, *, tpw, d, tok0):
    ngrp = N_SEL // SC_GROUP
    nchunk = d // SC_LANES
    nsteps = tpw * ngrp
    base = _worker_id() * tpw

    def gather(tok_buf, g, buf):
        return pltpu.make_async_copy(u_hbm.at[idx_v.at[tok_buf, g]], rows_v.at[buf], sem.at[buf])

    pltpu.sync_copy(idx_hbm.at[tok0 + base], idx_v.at[0])
    gather(0, 0, 0).start()

    @pl.loop(0, nsteps)
    def _(q):
        t_loc = q // ngrp
        g = q % ngrp
        buf = q % 2
        t = base + t_loc

        @pl.when(g == 0)
        def _():
            pltpu.sync_copy(h_hbm.at[tok0 + t], h_v)

            @pl.loop(0, N_SEL)
            def _(r):
                z_v[pl.ds(pl.multiple_of(r * SC_LANES, SC_LANES), SC_LANES)] = jnp.zeros((SC_LANES,), F32)

        @pl.when(q + 1 < nsteps)
        def _():
            tn = (q + 1) // ngrp
            gn = (q + 1) % ngrp

            @pl.when(gn == 0)
            def _():
                pltpu.sync_copy(idx_hbm.at[tok0 + base + tn], idx_v.at[tn % 2])

            gather(tn % 2, gn, 1 - buf).start()

        gather(t_loc % 2, g, buf).wait()

        @pl.loop(0, SC_GROUP, step=4)
        def _(rb):
            @pl.loop(0, nchunk, step=SC_CHUNK_UNROLL)
            def _(c0):
                accs = [jnp.zeros((SC_LANES,), F32) for _ in range(4)]
                for cc in range(SC_CHUNK_UNROLL):
                    sl = pl.ds(pl.multiple_of((c0 + cc) * SC_LANES, SC_LANES), SC_LANES)
                    hc = h_v[sl]
                    for i in range(4):
                        accs[i] = accs[i] + rows_v[buf, rb + i, sl] * hc
                for i in range(4):
                    row = pl.multiple_of((g * SC_GROUP + rb + i) * SC_LANES, SC_LANES)
                    plsc.addupdate(z_v.at[pl.ds(row, SC_LANES)], accs[i])

        @pl.when(g == ngrp - 1)
        def _():
            pltpu.sync_copy(z_v, z_hbm.at[t])


def _sc_z(u, idx, h, tok0, ntok):
    t, d = h.shape
    tpw = ntok // SC_WORKERS
    idx4 = idx.reshape(t, N_SEL // SC_GROUP, SC_GROUP)
    body = functools.partial(_sc_z_body, tpw=tpw, d=d, tok0=tok0)
    return pl.kernel(
        body,
        out_type=jax.ShapeDtypeStruct((ntok, N_SEL * SC_LANES), F32),
        mesh=_sc_mesh(),
        scratch_types=[
            pltpu.VMEM((2, N_SEL // SC_GROUP, SC_GROUP), jnp.int32),
            pltpu.VMEM((d,), F32),
            pltpu.VMEM((2, SC_GROUP, d), F32),
            pltpu.VMEM((N_SEL * SC_LANES,), F32),
            pltpu.SemaphoreType.DMA((2,)),
        ],
        name="sc_z",
    )(u, idx4, h)


def _sc_out_body(v_hbm, idx_hbm, coef_hbm, f_hbm, idx_v, coef_v, rows_v, out_v, sem, *, tpw, d, tok0):
    ngrp = N_SEL // SC_GROUP
    nchunk = d // SC_LANES
    nsteps = tpw * ngrp
    base = _worker_id() * tpw

    def gather(tok_buf, g, buf):
        return pltpu.make_async_copy(v_hbm.at[idx_v.at[tok_buf, g]], rows_v.at[buf], sem.at[buf])

    pltpu.sync_copy(idx_hbm.at[tok0 + base], idx_v.at[0])
    gather(0, 0, 0).start()

    @pl.loop(0, nsteps)
    def _(q):
        t_loc = q // ngrp
        g = q % ngrp
        buf = q % 2
        t = base + t_loc

        @pl.when(g == 0)
        def _():
            pltpu.sync_copy(coef_hbm.at[t], coef_v)

            @pl.loop(0, nchunk)
            def _(c):
                out_v[pl.ds(pl.multiple_of(c * SC_LANES, SC_LANES), SC_LANES)] = jnp.zeros((SC_LANES,), F32)

        @pl.when(q + 1 < nsteps)
        def _():
            tn = (q + 1) // ngrp
            gn = (q + 1) % ngrp

            @pl.when(gn == 0)
            def _():
                pltpu.sync_copy(idx_hbm.at[tok0 + base + tn], idx_v.at[tn % 2])

            gather(tn % 2, gn, 1 - buf).start()

        gather(t_loc % 2, g, buf).wait()

        @pl.loop(0, SC_GROUP, step=4)
        def _(rb):
            cf = [coef_v[pl.ds(pl.multiple_of((g * SC_GROUP + rb + i) * SC_LANES, SC_LANES), SC_LANES)]
                  for i in range(4)]

            @pl.loop(0, nchunk, step=SC_CHUNK_UNROLL)
            def _(c0):
                for cc in range(SC_CHUNK_UNROLL):
                    sl = pl.ds(pl.multiple_of((c0 + cc) * SC_LANES, SC_LANES), SC_LANES)
                    acc = rows_v[buf, rb, sl] * cf[0]
                    for i in range(1, 4):
                        acc = acc + rows_v[buf, rb + i, sl] * cf[i]
                    plsc.addupdate(out_v.at[sl], acc)

        @pl.when(g == ngrp - 1)
        def _():
            pltpu.sync_copy(out_v, f_hbm.at[t])


def _sc_out(v, idx, coef16, tok0):
    t = idx.shape[0]
    ntok = coef16.shape[0]
    d = v.shape[1]
    tpw = ntok // SC_WORKERS
    idx4 = idx.reshape(t, N_SEL // SC_GROUP, SC_GROUP)
    body = functools.partial(_sc_out_body, tpw=tpw, d=d, tok0=tok0)
    return pl.kernel(
        body,
        out_type=jax.ShapeDtypeStruct((ntok, d), F32),
        mesh=_sc_mesh(),
        scratch_types=[
            pltpu.VMEM((2, N_SEL // SC_GROUP, SC_GROUP), jnp.int32),
            pltpu.VMEM((N_SEL * SC_LANES,), F32),
            pltpu.VMEM((2, SC_GROUP, d), F32),
            pltpu.VMEM((d,), F32),
            pltpu.SemaphoreType.DMA((2,)),
        ],
        name="sc_out",
    )(v, idx4, coef16)


def _coef_kernel(zp_ref, gate_ref, sel_ref, selt_ref, o_ref):
    z = sum(_dot(part, sel_ref[...]) for part in _split3(zp_ref[...]))
    coef = gate_ref[...] * (0.5 * z * (1.0 + lax.erf(z * (2.0 ** -0.5))))
    o_ref[...] = sum(_dot(part, selt_ref[...]) for part in _split3(coef))


def _coef(zp, gate_tok, tok0, tm=256):
    ntok, wide = zp.shape
    b0 = tok0 // tm
    grp = np.arange(wide) // SC_LANES
    sel = jnp.asarray(grp[:, None] == np.arange(N_SEL)[None, :], BF16)
    return pl.pallas_call(
        _coef_kernel,
        grid=(ntok // tm,),
        in_specs=[
            pl.BlockSpec((tm, wide), lambda i: (i, 0)),
            pl.BlockSpec((tm, N_SEL), lambda i: (b0 + i, 0)),
            pl.BlockSpec(sel.shape, lambda i: (0, 0)),
            pl.BlockSpec(sel.shape[::-1], lambda i: (0, 0)),
        ],
        out_specs=pl.BlockSpec((tm, wide), lambda i: (i, 0)),
        out_shape=jax.ShapeDtypeStruct((ntok, wide), F32),
        compiler_params=_cparams(("parallel",)),
        name="coef",
    )(zp, gate_tok, sel, sel.T)


def _ln_out_kernel(h_ref, f_ref, g_ref, b_ref, o_ref):
    o_ref[...] = _layer_norm(DEEPNORM_ALPHA * h_ref[...] + f_ref[...], g_ref[...], b_ref[...])


def _ln_out(h1, f, l2g, l2b, tok0, tm=256):
    ntok, d = f.shape
    b0 = tok0 // tm
    return pl.pallas_call(
        _ln_out_kernel,
        grid=(ntok // tm,),
        in_specs=[
            pl.BlockSpec((tm, d), lambda i: (b0 + i, 0)),
            pl.BlockSpec((tm, d), lambda i: (i, 0)),
            pl.BlockSpec(l2g.shape, lambda i: (0, 0)),
            pl.BlockSpec(l2b.shape, lambda i: (0, 0)),
        ],
        out_specs=pl.BlockSpec((tm, d), lambda i: (i, 0)),
        out_shape=jax.ShapeDtypeStruct((ntok, d), F32),
        compiler_params=_cparams(("parallel",)),
        name="ln_out",
    )(h1, f, l2g, l2b)


def _rope_tables(seq):
    pos = jnp.arange(seq, dtype=F32)
    inv_freq = ROPE_THETA ** (-jnp.arange(0, HEAD_DIM, 2, dtype=F32) / HEAD_DIM)
    ang = pos[:, None] * inv_freq[None, :]
    ang = jnp.concatenate([ang, ang], -1)
    sign = jnp.concatenate([-jnp.ones((HEAD_DIM // 2,), F32), jnp.ones((HEAD_DIM // 2,), F32)])
    reps = LANES // HEAD_DIM
    return jnp.tile(jnp.cos(ang), (1, reps)), jnp.tile(jnp.sin(ang) * sign, (1, reps))


def _layer(h2d, batch, seq, w_in, mu_prev, mu_next, w0, w2, a0, a2, g2, k_k, k_a, r_k, lnx_g, lnx_b,
           sink, w_out, ln1_g, ln1_b, peer_wq, peer_keys, peer_u, peer_v, ln2_g, ln2_b):
    w = RWKV_WIDTH
    row = lambda a: a.reshape(1, -1).astype(F32)
    w_in_p = jnp.pad(w_in, ((0, 0), (0, RWKV_COLS_PAD - RWKV_COLS))).astype(BF16)
    mup = jnp.pad(mu_prev, (0, RWKV_COLS_PAD - RWKV_COLS)).reshape(1, -1)
    mun = jnp.pad(mu_next, (0, RWKV_COLS_PAD - RWKV_COLS)).reshape(1, -1)
    wmix = jnp.zeros((LANES, 4 * w), F32)
    for d in range(2):
        wmix = wmix.at[d * DECAY_RANK:(d + 1) * DECAY_RANK, d * w:(d + 1) * w].set(w2[d])
        r0 = 2 * DECAY_RANK + d * ICLR_RANK
        wmix = wmix.at[r0:r0 + ICLR_RANK, (2 + d) * w:(3 + d) * w].set(a2[d])
    g2p = jnp.pad(g2, ((0, LANES - GATE_RANK), (0, 0)))
    head_of = np.arange(w) // HEAD_DIM
    bd = jnp.asarray(head_of[:, None] == head_of[None, :], F32)
    cos_t, sin_t = _rope_tables(seq)

    qkv, pr = _proj(h2d, w_in_p, cos_t, sin_t, seq)
    y_attn = _attention(qkv.reshape(batch, seq, ATTN_COLS), sink.astype(F32))
    r, v, kk, g, lw0, lw1, kd0, kd1, b0, b1 = _prep(
        pr, mup, mun, wmix, w0.astype(F32), a0.astype(F32), g2p, row(k_k), row(k_a), bd, seq)
    yf, yb = _wkv(r, v, kk, lw0, lw1, kd0, kd1, b0, b1, batch, seq)
    h1 = _mix(h2d, y_attn.reshape(batch * seq, ATTN_WIDTH), yf, yb, r, v, kd0, kd1, g,
              row(lnx_g), row(lnx_b), row(r_k), bd, w_out.astype(BF16), row(ln1_g), row(ln1_b))
    idx, gate_t, gate_tok = _route(h1, peer_wq.astype(BF16), peer_keys.astype(F32))
    uv = jnp.concatenate([peer_u, peer_v], axis=1)
    uv = uv.reshape(uv.shape[0], uv.shape[1] // LANES, 1, LANES)
    l2g, l2b = row(ln2_g), row(ln2_b)

    t = batch * seq
    n_sc = (int(t * SC_TOKEN_SHARE) // SC_SPLIT_ALIGN) * SC_SPLIT_ALIGN
    n_tc = t - n_sc
    per_call = n_tc // TC_PEER_CALLS
    assert n_sc % SC_WORKERS == 0 and per_call * TC_PEER_CALLS == n_tc and per_call % PEER_TB == 0
    outs = []
    if n_sc:
        zp = _sc_z(peer_u.astype(F32), idx, h1, 0, n_sc)
    if n_tc:
        outs.append(_peer(idx, gate_t, h1, uv, l2g, l2b, n_sc, per_call))
    if n_sc:
        f_sc = _sc_out(peer_v.astype(F32), idx, _coef(zp, gate_tok, 0), 0)
    for c in range(1, TC_PEER_CALLS):
        outs.append(_peer(idx, gate_t, h1, uv, l2g, l2b, n_sc + c * per_call, per_call))
    if n_sc:
        outs.insert(0, _ln_out(h1, f_sc, l2g, l2b, 0))
    return jnp.concatenate(outs, axis=0)


def kernel(x, w_in, mu_prev, mu_next, w0, w2, a0, a2, g2, k_k, k_a, r_k, lnx_g, lnx_b, sink, w_out,
           ln1_g, ln1_b, peer_wq, peer_keys, peer_u, peer_v, ln2_g, ln2_b):
    batch, seq, d = x.shape
    h = x.reshape(batch * seq, d)
    for l in range(DEPTH):
        h = _layer(h, batch, seq, w_in[l], mu_prev[l], mu_next[l], w0[l], w2[l], a0[l], a2[l], g2[l],
                   k_k[l], k_a[l], r_k[l], lnx_g[l], lnx_b[l], sink[l], w_out[l], ln1_g[l], ln1_b[l],
                   peer_wq[l], peer_keys[l], peer_u[l], peer_v[l], ln2_g[l], ln2_b[l])
    return h.reshape(batch, seq, d)
```

```python
import functools

import numpy as np
import jax
import jax.numpy as jnp
from jax import lax
from jax.experimental import pallas as pl
from jax.experimental.pallas import tpu as pltpu
from jax.experimental.pallas import tpu_sc as plsc

F32 = jnp.float32
BF16 = jnp.bfloat16
HI = lax.Precision.HIGHEST

HEAD_DIM = 64
N_Q_HEADS = 8
N_KV_HEADS = 2
Q_PER_KV = N_Q_HEADS // N_KV_HEADS
ATTN_WIDTH = N_Q_HEADS * HEAD_DIM
ATTN_KV_WIDTH = N_KV_HEADS * HEAD_DIM
ATTN_COLS = ATTN_WIDTH + 2 * ATTN_KV_WIDTH
WINDOW = 128
ATTN_BLOCK = 128
ROPE_THETA = 10000.0
N_RWKV_HEADS = 8
RWKV_WIDTH = N_RWKV_HEADS * HEAD_DIM
DECAY_RANK = 32
ICLR_RANK = 32
GATE_RANK = 96
RWKV_COLS = 3 * RWKV_WIDTH + 2 * DECAY_RANK + 2 * ICLR_RANK + GATE_RANK
RWKV_COLS_PAD = 1792
RWKV_GN_EPS = 64e-5
PEER_HEADS = 8
PEER_NKEYS = 128
PEER_QDIM = 256
PEER_HALF = PEER_QDIM // 2
PEER_TOPK = 16
N_SEL = PEER_HEADS * PEER_TOPK
LN_EPS = 1e-5
DEPTH = 1
DEEPNORM_ALPHA = (2.0 * DEPTH) ** 0.25

LANES = 128
SUBLANES = 8
WKV_CHUNK = 64
VMEM_LIMIT = 48 * 1024 * 1024
PEER_TB = 128
SC_LANES = 16
SC_WORKERS = 32
SC_GROUP = 32
SC_CHUNK_UNROLL = 8
SC_TOKEN_SHARE = 17 / 32
SC_SPLIT_ALIGN = 1024
TC_PEER_CALLS = 2


def _cparams(sem):
    return pltpu.CompilerParams(dimension_semantics=sem, vmem_limit_bytes=VMEM_LIMIT)


def _dot(a, b, precision=None):
    return jnp.dot(a, b, preferred_element_type=F32, precision=precision)


def _dot_nt(a, b, precision=None):
    return lax.dot_general(a, b, (((1,), (1,)), ((), ())), preferred_element_type=F32,
                           precision=precision)


def _dot_tn(a, b, precision=None):
    return lax.dot_general(a, b, (((0,), (0,)), ((), ())), preferred_element_type=F32,
                           precision=precision)


def _layer_norm(z, g, b):
    mu = jnp.mean(z, -1, keepdims=True)
    zc = z - mu
    var = jnp.mean(zc * zc, -1, keepdims=True)
    return zc * lax.rsqrt(var + LN_EPS) * g + b


def _proj_kernel(x_ref, w_ref, cos_ref, sin_ref, qkv_ref, pr_ref):
    xb = x_ref[...].astype(BF16)
    cos = cos_ref[...]
    sin = sin_ref[...]
    lane = lax.broadcasted_iota(jnp.int32, cos.shape, 1)
    first_half = (lane & (HEAD_DIM // 2)) == 0

    def rope(t):
        rot = jnp.where(first_half, pltpu.roll(t, LANES - HEAD_DIM // 2, 1),
                        pltpu.roll(t, HEAD_DIM // 2, 1))
        return t * cos + rot * sin

    for c in range(0, ATTN_COLS, 2 * LANES):
        acc = _dot(xb, w_ref[:, c:c + 2 * LANES])
        for half in range(2):
            col = c + half * LANES
            t = acc[:, half * LANES:(half + 1) * LANES]
            if col < ATTN_WIDTH + ATTN_KV_WIDTH:
                t = rope(t)
            qkv_ref[:, col:col + LANES] = t
    for c in range(0, RWKV_COLS_PAD, 2 * LANES):
        pr_ref[:, c:c + 2 * LANES] = _dot(xb, w_ref[:, ATTN_COLS + c:ATTN_COLS + c + 2 * LANES])


def _proj(x2, w_in_p, cos_t, sin_t, seq, tm=512):
    t, d = x2.shape
    n_pos = seq // tm
    return pl.pallas_call(
        _proj_kernel,
        grid=(t // tm,),
        in_specs=[
            pl.BlockSpec((tm, d), lambda i: (i, 0)),
            pl.BlockSpec(w_in_p.shape, lambda i: (0, 0)),
            pl.BlockSpec((tm, LANES), lambda i: (i % n_pos, 0)),
            pl.BlockSpec((tm, LANES), lambda i: (i % n_pos, 0)),
        ],
        out_specs=[
            pl.BlockSpec((tm, ATTN_COLS), lambda i: (i, 0)),
            pl.BlockSpec((tm, RWKV_COLS_PAD), lambda i: (i, 0)),
        ],
        out_shape=[
            jax.ShapeDtypeStruct((t, ATTN_COLS), F32),
            jax.ShapeDtypeStruct((t, RWKV_COLS_PAD), F32),
        ],
        compiler_params=_cparams(("parallel",)),
        name="proj",
    )(x2, w_in_p, cos_t, sin_t)


def _attn_kernel(sink_ref, q_ref, kp_ref, kc_ref, kn_ref, vp_ref, vc_ref, vn_ref, o_ref, *, nb):
    n = pl.program_id(1)
    blk = ATTN_BLOCK
    rows = Q_PER_KV * blk
    q = q_ref[0] * (HEAD_DIM ** -0.5)
    kwin = jnp.concatenate([kp_ref[0], kc_ref[0], kn_ref[0]], axis=0)
    vwin = jnp.concatenate([vp_ref[0], vc_ref[0], vn_ref[0]], axis=0)
    qi = lax.broadcasted_iota(jnp.int32, (rows, 3 * blk), 0) & (blk - 1)
    kj = lax.broadcasted_iota(jnp.int32, (rows, 3 * blk), 1)
    dist = kj - qi
    valid = ((dist >= blk - WINDOW) & (dist <= blk + WINDOW)
             & ((kj >= blk) | (n > 0)) & ((kj < 2 * blk) | (n < nb - 1)))
    rowg = lax.broadcasted_iota(jnp.int32, (rows, 1), 0) // blk
    outs = []
    for h in range(N_KV_HEADS):
        qs = jnp.concatenate(
            [q[:, (Q_PER_KV * h + g) * HEAD_DIM:(Q_PER_KV * h + g + 1) * HEAD_DIM]
             for g in range(Q_PER_KV)], axis=0)
        kh = kwin[:, h * HEAD_DIM:(h + 1) * HEAD_DIM]
        vh = vwin[:, h * HEAD_DIM:(h + 1) * HEAD_DIM]
        logits = _dot_nt(qs.astype(BF16), kh.astype(BF16))
        logits = jnp.where(valid, logits, -1e30)
        sk = jnp.zeros((rows, 1), F32)
        for g in range(Q_PER_KV):
            sk = jnp.where(rowg == g, sink_ref[Q_PER_KV * h + g], sk)
        m = jnp.maximum(jnp.max(logits, -1, keepdims=True), sk)
        e = jnp.exp(logits - m)
        den = jnp.sum(e, -1, keepdims=True) + jnp.exp(sk - m)
        p = e / den
        o = _dot(p.astype(BF16), vh.astype(BF16))
        for g in range(Q_PER_KV):
            outs.append(o[g * blk:(g + 1) * blk])
    o_ref[0] = jnp.concatenate(outs, axis=1)


def _attention(qkv3, sink):
    b, s, _ = qkv3.shape
    blk = ATTN_BLOCK
    nb = s // blk
    kcol = ATTN_WIDTH // LANES
    vcol = kcol + 1

    def spec(col, shift):
        def imap(bi, n):
            return (bi, jnp.clip(n + shift, 0, nb - 1), col)
        return pl.BlockSpec((1, blk, LANES), imap)

    return pl.pallas_call(
        functools.partial(_attn_kernel, nb=nb),
        grid=(b, nb),
        in_specs=[
            pl.BlockSpec(memory_space=pltpu.SMEM),
            pl.BlockSpec((1, blk, ATTN_WIDTH), lambda bi, n: (bi, n, 0)),
            spec(kcol, -1), spec(kcol, 0), spec(kcol, 1),
            spec(vcol, -1), spec(vcol, 0), spec(vcol, 1),
        ],
        out_specs=pl.BlockSpec((1, blk, ATTN_WIDTH), lambda bi, n: (bi, n, 0)),
        out_shape=jax.ShapeDtypeStruct((b, s, ATTN_WIDTH), F32),
        compiler_params=_cparams(("parallel", "parallel")),
        name="attn",
    )(sink, qkv3, qkv3, qkv3, qkv3, qkv3, qkv3, qkv3)


def _softplus(x):
    return jnp.maximum(x, 0.0) + jnp.log(1.0 + jnp.exp(-jnp.abs(x)))


def _sigmoid(x):
    return 1.0 / (1.0 + jnp.exp(-x))


def _prep_kernel(p_ref, hp_ref, hn_ref, mup_ref, mun_ref, wmix_ref, w0_ref, a0_ref, g2_ref,
                 kk_ref, ka_ref, bd_ref,
                 r_o, v_o, kk_o, g_o, lw0_o, lw1_o, kd0_o, kd1_o, b0_o, b1_o, *, tm, seq):
    i = pl.program_id(0)
    row = lax.broadcasted_iota(jnp.int32, (tm, 1), 0)
    seq_start = (i * tm) % seq == 0
    seq_end = ((i + 1) * tm) % seq == 0

    def shifted(c0, c1):
        p = p_ref[:, c0:c1]
        prev_row = jnp.where(seq_start, 0.0, hp_ref[SUBLANES - 1:SUBLANES, c0:c1])
        next_row = jnp.where(seq_end, 0.0, hn_ref[0:1, c0:c1])
        p_prev = jnp.where(row == 0, prev_row, pltpu.roll(p, 1, 0))
        p_next = jnp.where(row == tm - 1, next_row, pltpu.roll(p, tm - 1, 0))
        return p + mup_ref[:, c0:c1] * (p_prev - p) + mun_ref[:, c0:c1] * (p_next - p)

    w = RWKV_WIDTH
    r = shifted(0, w)
    k = shifted(w, 2 * w)
    v = shifted(2 * w, 3 * w)
    codes = shifted(3 * w, 3 * w + LANES)
    gd = shifted(3 * w + LANES, 3 * w + 2 * LANES)
    r_o[...] = r
    v_o[...] = v

    lane = lax.broadcasted_iota(jnp.int32, codes.shape, 1)
    codes = jnp.where(lane < 2 * DECAY_RANK, jnp.tanh(codes), codes)
    mm = _dot(codes, wmix_ref[...], HI)
    g_o[...] = _dot(_sigmoid(gd), g2_ref[...], HI)

    kkv = k * kk_ref[...]
    ss = _dot(kkv * kkv, bd_ref[...], HI)
    kkn = kkv * lax.rsqrt(jnp.maximum(ss, 1e-24))
    kk_o[...] = kkn

    ka = ka_ref[...]
    for d, (lw_o, kd_o, b_o) in enumerate(((lw0_o, kd0_o, b0_o), (lw1_o, kd1_o, b1_o))):
        w_log = -_softplus(-(w0_ref[d:d + 1, :] + mm[:, d * w:(d + 1) * w])) - 0.5
        lw_o[...] = -jnp.exp(w_log)
        a = _sigmoid(a0_ref[d:d + 1, :] + mm[:, (2 + d) * w:(3 + d) * w])
        kd_o[...] = k * (1.0 + (a - 1.0) * ka)
        b_o[...] = kkn * a


def _prep(pr, mup, mun, wmix, w0, a0, g2p, k_k, k_a, bd, seq, tm=256):
    t = pr.shape[0]
    nblk8 = t // SUBLANES
    per = tm // SUBLANES
    full = lambda a: pl.BlockSpec(a.shape, lambda i: (0,) * a.ndim)
    out = pl.BlockSpec((tm, RWKV_WIDTH), lambda i: (i, 0))
    return pl.pallas_call(
        functools.partial(_prep_kernel, tm=tm, seq=seq),
        grid=(t // tm,),
        in_specs=[
            pl.BlockSpec((tm, RWKV_COLS_PAD), lambda i: (i, 0)),
            pl.BlockSpec((SUBLANES, RWKV_COLS_PAD), lambda i: (jnp.maximum(i * per - 1, 0), 0)),
            pl.BlockSpec((SUBLANES, RWKV_COLS_PAD),
                         lambda i: (jnp.minimum((i + 1) * per, nblk8 - 1), 0)),
            full(mup), full(mun), full(wmix), full(w0), full(a0), full(g2p),
            full(k_k), full(k_a), full(bd),
        ],
        out_specs=[out] * 10,
        out_shape=[jax.ShapeDtypeStruct((t, RWKV_WIDTH), F32)] * 10,
        compiler_params=_cparams(("parallel",)),
        name="prep",
    )(pr, pr, pr, mup, mun, wmix, w0, a0, g2p, k_k, k_a, bd)


def _split3(x):
    hi = x.astype(BF16)
    r1 = x - hi.astype(F32)
    mid = r1.astype(BF16)
    lo = (r1 - mid.astype(F32)).astype(BF16)
    return hi, mid, lo


def _wkv_kernel(rf, vf, kkf, lwf, kdf, bf, rb, vb, kkb, lwb, kdb, bb, yf_ref, yb_ref, state):
    c = pl.program_id(1)
    n = WKV_CHUNK
    hd = HEAD_DIM
    nh = N_RWKV_HEADS
    def bmm(a, b, ca, cb):
        return lax.dot_general(a.astype(BF16), b.astype(BF16), (((ca,), (cb,)), ((0,), (0,))),
                               preferred_element_type=F32)

    @pl.when(c == 0)
    def _():
        state[...] = jnp.zeros_like(state)

    ti = lax.broadcasted_iota(jnp.int32, (n, n), 0)
    si = lax.broadcasted_iota(jnp.int32, (n, n), 1)
    dirs = ((rf, vf, kkf, lwf, kdf, bf, si <= ti, si < ti, n - 1),
            (rb, vb, kkb, lwb, kdb, bb, si >= ti, si > ti, 0))
    heads = lambda x: jnp.stack([x[:, j * hd:(j + 1) * hd] for j in range(nh)], 0)
    parts = []
    for r_ref, v_ref, kk_ref, lw_ref, kd_ref, b_ref, incl, strict, last in dirs:
        lw = lw_ref[0]
        tri = incl.astype(BF16)
        cum = sum(_dot(tri, part) for part in _split3(lw))
        e_neg = jnp.exp(-cum)
        e_last = jnp.exp(cum[last:last + 1, :] - cum)
        g_scale = jnp.exp(cum[last:last + 1, :])
        parts.append(dict(
            at=heads((-kk_ref[0] * jnp.exp(cum - lw)).astype(BF16)),
            rt=heads((r_ref[0] * jnp.exp(cum)).astype(BF16)),
            bt=heads((b_ref[0] * e_neg).astype(BF16)),
            kt=heads((kd_ref[0] * e_neg).astype(BF16)),
            bl=heads((b_ref[0] * e_last).astype(BF16)),
            kl=heads((kd_ref[0] * e_last).astype(BF16)),
            v=heads(v_ref[0].astype(BF16)),
            gs=heads(g_scale),
            incl=jnp.broadcast_to(incl[None], (nh, n, n)),
            strict=jnp.broadcast_to(strict[None], (nh, n, n))))
    cat = lambda key: jnp.concatenate([parts[0][key], parts[1][key]], 0)
    at, rt, bt, kt, bl, kl, v, gs = (cat(k) for k in ("at", "rt", "bt", "kt", "bl", "kl", "v", "gs"))
    incl, strict = cat("incl"), cat("strict")
    g0 = state[...]
    g0b = g0.astype(BF16)
    m1 = bmm(jnp.concatenate([at, rt], 1), jnp.concatenate([bt, kt], 1), 2, 2)
    a_ab = jnp.where(strict, m1[:, :n, :n], 0.0)
    a_ak = jnp.where(strict, m1[:, :n, n:], 0.0)
    a_rb = jnp.where(incl, m1[:, n:, :n], 0.0)
    a_rk = jnp.where(incl, m1[:, n:, n:], 0.0)
    tinv = jnp.where((ti == si)[None], 1.0, a_ab)
    pw = a_ab
    for _ in range(int(np.log2(n)) - 1):
        pw = bmm(pw, pw, 2, 1)
        tinv = tinv + bmm(tinv, pw, 2, 1)
    rhs = bmm(a_ak, v, 2, 1) + bmm(at, g0b, 2, 2)
    u = bmm(tinv, rhs, 2, 1)
    uv = jnp.concatenate([u.astype(BF16), v], 1)
    y = bmm(jnp.concatenate([a_rb, a_rk], 2), uv, 2, 1) + bmm(rt, g0b, 2, 2)
    state[...] = g0 * gs + bmm(uv, jnp.concatenate([bl, kl], 1), 1, 1)
    for d, y_ref in enumerate((yf_ref, yb_ref)):
        y_ref[0] = jnp.concatenate([y[d * nh + j] for j in range(nh)], axis=1)


def _wkv(r, v, kk, lw0, lw1, kd0, kd1, b0, b1, batch, seq):
    n = WKV_CHUNK
    nc = seq // n
    shp = (batch, seq, RWKV_WIDTH)
    arrs = [a.reshape(shp) for a in (r, v, kk, lw0, kd0, b0, r, v, kk, lw1, kd1, b1)]
    fwd = pl.BlockSpec((1, n, RWKV_WIDTH), lambda b, c: (b, c, 0))
    bwd = pl.BlockSpec((1, n, RWKV_WIDTH), lambda b, c: (b, nc - 1 - c, 0))
    yf, yb = pl.pallas_call(
        _wkv_kernel,
        grid=(batch, nc),
        in_specs=[fwd] * 6 + [bwd] * 6,
        out_specs=[fwd, bwd],
        out_shape=[jax.ShapeDtypeStruct(shp, F32)] * 2,
        scratch_shapes=[pltpu.VMEM((2 * N_RWKV_HEADS, HEAD_DIM, HEAD_DIM), F32)],
        compiler_params=_cparams(("parallel", "arbitrary")),
        name="wkv",
    )(*arrs)
    return yf.reshape(batch * seq, RWKV_WIDTH), yb.reshape(batch * seq, RWKV_WIDTH)


def _mix_kernel(x_ref, ya_ref, yf_ref, yb_ref, r_ref, v_ref, kd0_ref, kd1_ref, g_ref,
                lng_ref, lnb_ref, rk_ref, bd_ref, wout_ref, l1g_ref, l1b_ref, h_ref):
    bd = bd_ref[...]
    inv = 1.0 / HEAD_DIM
    y = yf_ref[...] + yb_ref[...]
    mu = _dot(y, bd, HI) * inv
    yc = y - mu
    var = _dot(yc * yc, bd, HI) * inv
    yn = yc * lax.rsqrt(var + RWKV_GN_EPS) * lng_ref[...] + lnb_ref[...]
    k_mean = 0.5 * (kd0_ref[...] + kd1_ref[...])
    v = v_ref[...]
    bonus = _dot(r_ref[...] * k_mean * rk_ref[...], bd, HI) * v
    yr = (yn + bonus) * g_ref[...]
    mix = (_dot(ya_ref[...].astype(BF16), wout_ref[:ATTN_WIDTH, :])
           + _dot(yr.astype(BF16), wout_ref[ATTN_WIDTH:, :]))
    h_ref[...] = _layer_norm(DEEPNORM_ALPHA * x_ref[...] + mix, l1g_ref[...], l1b_ref[...])


def _mix(x2, ya, yf, yb, r, v, kd0, kd1, g, lng, lnb, rk, bd, wout, l1g, l1b, tm=256):
    t, d = x2.shape
    full = lambda a: pl.BlockSpec(a.shape, lambda i: (0,) * a.ndim)
    half = pl.BlockSpec((tm, RWKV_WIDTH), lambda i: (i, 0))
    wide = pl.BlockSpec((tm, d), lambda i: (i, 0))
    return pl.pallas_call(
        _mix_kernel,
        grid=(t // tm,),
        in_specs=[wide] + [half] * 8 + [full(lng), full(lnb), full(rk), full(bd), full(wout),
                                        full(l1g), full(l1b)],
        out_specs=wide,
        out_shape=jax.ShapeDtypeStruct((t, d), F32),
        compiler_params=_cparams(("parallel",)),
        name="mix",
    )(x2, ya, yf, yb, r, v, kd0, kd1, g, lng, lnb, rk, bd, wout, l1g, l1b)


def _top_rows(sc, k, payload=None):
    n = sc.shape[0]
    iota = lax.broadcasted_iota(jnp.int32, sc.shape, 0)
    vals, picks = [], []
    for _ in range(k):
        m = jnp.max(sc, axis=0, keepdims=True)
        pos = jnp.min(jnp.where(sc == m, iota, n), axis=0, keepdims=True)
        hit = iota == pos
        vals.append(m)
        if payload is None:
            picks.append(pos)
        else:
            picks.append(jnp.max(jnp.where(hit, payload, -1), axis=0, keepdims=True))
        sc = jnp.where(hit, -jnp.inf, sc)
    return jnp.concatenate(vals, 0), jnp.concatenate(picks, 0)


def _route_kernel(h_ref, wq_ref, keys_ref, idx_ref, gate_ref, gate_tok_ref, q_scr, idx_scr):
    k = PEER_TOPK
    q_scr[...] = _dot(h_ref[...].astype(BF16), wq_ref[...])

    def head(hh, carry):
        tops = []
        for p in range(2):
            col = pl.multiple_of(hh * PEER_QDIM + p * PEER_HALF, PEER_HALF)
            qp = q_scr[:, pl.ds(col, PEER_HALF)]
            sct = _dot_nt(keys_ref[hh, p], qp, HI)
            tops.append(_top_rows(sct, k))
        (s1, i1), (s2, i2) = tops
        cand = jnp.concatenate([s1[i:i + 1] + s2[:k // (i + 1)] for i in range(k)], 0)
        cand_id = jnp.concatenate([i1[i:i + 1] * PEER_NKEYS + i2[:k // (i + 1)] for i in range(k)], 0)
        cs, ids = _top_rows(cand, k, cand_id)
        e = jnp.exp(cs - jnp.max(cs, axis=0, keepdims=True))
        row = pl.multiple_of(hh * k, k)
        gate_ref[pl.ds(row, k), :] = e / jnp.sum(e, axis=0, keepdims=True)
        idx_scr[pl.ds(row, k), :] = ids
        return carry

    lax.fori_loop(0, PEER_HEADS, head, 0)
    idx_ref[...] = idx_scr[...].T
    gate_tok_ref[...] = gate_ref[...].T


def _route(h1, wq, keys, tm=256):
    t, d = h1.shape
    return pl.pallas_call(
        _route_kernel,
        grid=(t // tm,),
        in_specs=[
            pl.BlockSpec((tm, d), lambda i: (i, 0)),
            pl.BlockSpec(wq.shape, lambda i: (0, 0)),
            pl.BlockSpec(keys.shape, lambda i: (0, 0, 0, 0)),
        ],
        out_specs=[
            pl.BlockSpec((tm, N_SEL), lambda i: (i, 0)),
            pl.BlockSpec((N_SEL, tm), lambda i: (0, i)),
            pl.BlockSpec((tm, N_SEL), lambda i: (i, 0)),
        ],
        out_shape=[
            jax.ShapeDtypeStruct((t, N_SEL), jnp.int32),
            jax.ShapeDtypeStruct((N_SEL, t), F32),
            jax.ShapeDtypeStruct((t, N_SEL), F32),
        ],
        scratch_shapes=[pltpu.VMEM((tm, PEER_HEADS * PEER_QDIM), F32),
                        pltpu.VMEM((N_SEL, tm), jnp.int32)],
        compiler_params=_cparams(("parallel",)),
        name="route",
    )(h1, wq, keys)


def _peer_kernel(idx_ref, gate_ref, h_ref, uv_hbm, l2g_ref, l2b_ref, o_ref, buf, f_scr, sem, *, tb):
    d = h_ref.shape[1]
    nk = d // LANES

    def issue(t, slot):
        for j in range(N_SEL):
            pltpu.make_async_copy(uv_hbm.at[idx_ref[t, j]], buf.at[slot, :, pl.ds(j, 1), :],
                                  sem.at[slot]).start()

    def wait_all(slot):
        pltpu.make_async_copy(buf.at[slot], buf.at[slot], sem.at[slot]).wait()

    issue(0, 0)
    lane = lax.broadcasted_iota(jnp.int32, (N_SEL, tb), 1)

    def body(t, carry):
        slot = t % 2

        @pl.when(t + 1 < tb)
        def _():
            issue(t + 1, 1 - slot)

        wait_all(slot)
        hrow = h_ref[pl.ds(t, 1), :]
        acc = buf[slot, 0] * hrow[:, 0:LANES]
        for k in range(1, nk):
            acc = acc + buf[slot, k] * hrow[:, k * LANES:(k + 1) * LANES]
        z = jnp.sum(acc, axis=1, keepdims=True)
        gcol = jnp.sum(jnp.where(lane == t, gate_ref[...], 0.0), axis=1, keepdims=True)
        coef = gcol * (0.5 * z * (1.0 + lax.erf(z * (2.0 ** -0.5))))
        f_scr[pl.ds(t, 1), :] = jnp.concatenate(
            [jnp.sum(coef * buf[slot, nk + k], axis=0, keepdims=True) for k in range(nk)], axis=1)
        return carry

    lax.fori_loop(0, tb, body, 0)
    o_ref[...] = _layer_norm(DEEPNORM_ALPHA * h_ref[...] + f_scr[...], l2g_ref[...], l2b_ref[...])


def _peer(idx, gate_t, h1, uv, l2g, l2b, tok0, ntok, tb=PEER_TB):
    d = h1.shape[1]
    b0 = tok0 // tb
    return pl.pallas_call(
        functools.partial(_peer_kernel, tb=tb),
        grid=(ntok // tb,),
        in_specs=[
            pl.BlockSpec((tb, N_SEL), lambda i: (b0 + i, 0), memory_space=pltpu.SMEM),
            pl.BlockSpec((N_SEL, tb), lambda i: (0, b0 + i)),
            pl.BlockSpec((tb, d), lambda i: (b0 + i, 0)),
            pl.BlockSpec(memory_space=pl.ANY),
            pl.BlockSpec(l2g.shape, lambda i: (0, 0)),
            pl.BlockSpec(l2b.shape, lambda i: (0, 0)),
        ],
        out_specs=pl.BlockSpec((tb, d), lambda i: (i, 0)),
        out_shape=jax.ShapeDtypeStruct((ntok, d), F32),
        scratch_shapes=[pltpu.VMEM((2, 2 * d // LANES, N_SEL, LANES), F32),
                        pltpu.VMEM((tb, d), F32),
                        pltpu.SemaphoreType.DMA((2,))],
        compiler_params=_cparams(("arbitrary",)),
        name="peer",
    )(idx, gate_t, h1, uv, l2g, l2b)


def _sc_mesh():
    return plsc.VectorSubcoreMesh(core_axis_name="c", subcore_axis_name="s")


def _worker_id():
    return lax.axis_index("s") * 2 + lax.axis_index("c")


def _sc_z_body(u_hbm, idx_hbm, h_hbm, z_hbm, idx_v, h_v, rows_v, z_v, sem, *, tpw, d, tok0):
    ngrp = N_SEL // SC_GROUP
    nchunk = d // SC_LANES
    nsteps = tpw * ngrp
    base = _worker_id() * tpw

    def gather(tok_buf, g, buf):
        return pltpu.make_async_copy(u_hbm.at[idx_v.at[tok_buf, g]], rows_v.at[buf], sem.at[buf])

    pltpu.sync_copy(idx_hbm.at[tok0 + base], idx_v.at[0])
    gather(0, 0, 0).start()

    @pl.loop(0, nsteps)
    def _(q):
        t_loc = q // ngrp
        g = q % ngrp
        buf = q % 2
        t = base + t_loc

        @pl.when(g == 0)
        def _():
            pltpu.sync_copy(h_hbm.at[tok0 + t], h_v)

            @pl.loop(0, N_SEL)
            def _(r):
                z_v[pl.ds(pl.multiple_of(r * SC_LANES, SC_LANES), SC_LANES)] = jnp.zeros((SC_LANES,), F32)

        @pl.when(q + 1 < nsteps)
        def _():
            tn = (q + 1) // ngrp
            gn = (q + 1) % ngrp

            @pl.when(gn == 0)
            def _():
                pltpu.sync_copy(idx_hbm.at[tok0 + base + tn], idx_v.at[tn % 2])

            gather(tn % 2, gn, 1 - buf).start()

        gather(t_loc % 2, g, buf).wait()

        @pl.loop(0, SC_GROUP, step=4)
        def _(rb):
            @pl.loop(0, nchunk, step=SC_CHUNK_UNROLL)
            def _(c0):
                accs = [jnp.zeros((SC_LANES,), F32) for _ in range(4)]
                for cc in range(SC_CHUNK_UNROLL):
                    sl = pl.ds(pl.multiple_of((c0 + cc) * SC_LANES, SC_LANES), SC_LANES)
                    hc = h_v[sl]
                    for i in range(4):
                        accs[i] = accs[i] + rows_v[buf, rb + i, sl] * hc
                for i in range(4):
                    row = pl.multiple_of((g * SC_GROUP + rb + i) * SC_LANES, SC_LANES)
                    plsc.addupdate(z_v.at[pl.ds(row, SC_LANES)], accs[i])

        @pl.when(g == ngrp - 1)
        def _():
            pltpu.sync_copy(z_v, z_hbm.at[t])


def _sc_z(u, idx, h, tok0, ntok):
    t, d = h.shape
    tpw = ntok // SC_WORKERS
    idx4 = idx.reshape(t, N_SEL // SC_GROUP, SC_GROUP)
    body = functools.partial(_sc_z_body, tpw=tpw, d=d, tok0=tok0)
    return pl.kernel(
        body,
        out_type=jax.ShapeDtypeStruct((ntok, N_SEL * SC_LANES), F32),
        mesh=_sc_mesh(),
        scratch_types=[
            pltpu.VMEM((2, N_SEL // SC_GROUP, SC_GROUP), jnp.int32),
            pltpu.VMEM((d,), F32),
            pltpu.VMEM((2, SC_GROUP, d), F32),
            pltpu.VMEM((N_SEL * SC_LANES,), F32),
            pltpu.SemaphoreType.DMA((2,)),
        ],
        name="sc_z",
    )(u, idx4, h)


def _sc_out_body(v_hbm, idx_hbm, coef_hbm, f_hbm, idx_v, coef_v, rows_v, out_v, sem, *, tpw, d, tok0):
    ngrp = N_SEL // SC_GROUP
    nchunk = d // SC_LANES
    nsteps = tpw * ngrp
    base = _worker_id() * tpw

    def gather(tok_buf, g, buf):
        return pltpu.make_async_copy(v_hbm.at[idx_v.at[tok_buf, g]], rows_v.at[buf], sem.at[buf])

    pltpu.sync_copy(idx_hbm.at[tok0 + base], idx_v.at[0])
    gather(0, 0, 0).start()

    @pl.loop(0, nsteps)
    def _(q):
        t_loc = q // ngrp
        g = q % ngrp
        buf = q % 2
        t = base + t_loc

        @pl.when(g == 0)
        def _():
            pltpu.sync_copy(coef_hbm.at[t], coef_v)

            @pl.loop(0, nchunk)
            def _(c):
                out_v[pl.ds(pl.multiple_of(c * SC_LANES, SC_LANES), SC_LANES)] = jnp.zeros((SC_LANES,), F32)

        @pl.when(q + 1 < nsteps)
        def _():
            tn = (q + 1) // ngrp
            gn = (q + 1) % ngrp

            @pl.when(gn == 0)
            def _():
                pltpu.sync_copy(idx_hbm.at[tok0 + base + tn], idx_v.at[tn % 2])

            gather(tn % 2, gn, 1 - buf).start()

        gather(t_loc % 2, g, buf).wait()

        @pl.loop(0, SC_GROUP, step=4)
        def _(rb):
            cf = [coef_v[pl.ds(pl.multiple_of((g * SC_GROUP + rb + i) * SC_LANES, SC_LANES), SC_LANES)]
                  for i in range(4)]

            @pl.loop(0, nchunk, step=SC_CHUNK_UNROLL)
            def _(c0):
                for cc in range(SC_CHUNK_UNROLL):
                    sl = pl.ds(pl.multiple_of((c0 + cc) * SC_LANES, SC_LANES), SC_LANES)
                    acc = rows_v[buf, rb, sl] * cf[0]
                    for i in range(1, 4):
                        acc = acc + rows_v[buf, rb + i, sl] * cf[i]
                    plsc.addupdate(out_v.at[sl], acc)

        @pl.when(g == ngrp - 1)
        def _():
            pltpu.sync_copy(out_v, f_hbm.at[t])


def _sc_out(v, idx, coef16, tok0):
    t = idx.shape[0]
    ntok = coef16.shape[0]
    d = v.shape[1]
    tpw = ntok // SC_WORKERS
    idx4 = idx.reshape(t, N_SEL // SC_GROUP, SC_GROUP)
    body = functools.partial(_sc_out_body, tpw=tpw, d=d, tok0=tok0)
    return pl.kernel(
        body,
        out_type=jax.ShapeDtypeStruct((ntok, d), F32),
        mesh=_sc_mesh(),
        scratch_types=[
            pltpu.VMEM((2, N_SEL // SC_GROUP, SC_GROUP), jnp.int32),
            pltpu.VMEM((N_SEL * SC_LANES,), F32),
            pltpu.VMEM((2, SC_GROUP, d), F32),
            pltpu.VMEM((d,), F32),
            pltpu.SemaphoreType.DMA((2,)),
        ],
        name="sc_out",
    )(v, idx4, coef16)


def _coef_kernel(zp_ref, gate_ref, sel_ref, selt_ref, after_ref, o_ref):
    z = sum(_dot(part, sel_ref[...]) for part in _split3(zp_ref[...]))
    coef = gate_ref[...] * (0.5 * z * (1.0 + lax.erf(z * (2.0 ** -0.5))))
    o_ref[...] = sum(_dot(part, selt_ref[...]) for part in _split3(coef))


def _coef(zp, gate_tok, tok0, after, tm=256):
    ntok, wide = zp.shape
    b0 = tok0 // tm
    grp = np.arange(wide) // SC_LANES
    sel = jnp.asarray(grp[:, None] == np.arange(N_SEL)[None, :], BF16)
    return pl.pallas_call(
        _coef_kernel,
        grid=(ntok // tm,),
        in_specs=[
            pl.BlockSpec((tm, wide), lambda i: (i, 0)),
            pl.BlockSpec((tm, N_SEL), lambda i: (b0 + i, 0)),
            pl.BlockSpec(sel.shape, lambda i: (0, 0)),
            pl.BlockSpec(sel.shape[::-1], lambda i: (0, 0)),
            pl.BlockSpec((SUBLANES, LANES), lambda i: (0, 0)),
        ],
        out_specs=pl.BlockSpec((tm, wide), lambda i: (i, 0)),
        out_shape=jax.ShapeDtypeStruct((ntok, wide), F32),
        compiler_params=_cparams(("parallel",)),
        name="coef",
    )(zp, gate_tok, sel, sel.T, after)


def _ln_out_kernel(h_ref, f_ref, g_ref, b_ref, o_ref):
    o_ref[...] = _layer_norm(DEEPNORM_ALPHA * h_ref[...] + f_ref[...], g_ref[...], b_ref[...])


def _ln_out(h1, f, l2g, l2b, tok0, tm=256):
    ntok, d = f.shape
    b0 = tok0 // tm
    return pl.pallas_call(
        _ln_out_kernel,
        grid=(ntok // tm,),
        in_specs=[
            pl.BlockSpec((tm, d), lambda i: (b0 + i, 0)),
            pl.BlockSpec((tm, d), lambda i: (i, 0)),
            pl.BlockSpec(l2g.shape, lambda i: (0, 0)),
            pl.BlockSpec(l2b.shape, lambda i: (0, 0)),
        ],
        out_specs=pl.BlockSpec((tm, d), lambda i: (i, 0)),
        out_shape=jax.ShapeDtypeStruct((ntok, d), F32),
        compiler_params=_cparams(("parallel",)),
        name="ln_out",
    )(h1, f, l2g, l2b)


def _rope_tables(seq):
    pos = jnp.arange(seq, dtype=F32)
    inv_freq = ROPE_THETA ** (-jnp.arange(0, HEAD_DIM, 2, dtype=F32) / HEAD_DIM)
    ang = pos[:, None] * inv_freq[None, :]
    ang = jnp.concatenate([ang, ang], -1)
    sign = jnp.concatenate([-jnp.ones((HEAD_DIM // 2,), F32), jnp.ones((HEAD_DIM // 2,), F32)])
    reps = LANES // HEAD_DIM
    return jnp.tile(jnp.cos(ang), (1, reps)), jnp.tile(jnp.sin(ang) * sign, (1, reps))


def _layer(h2d, batch, seq, w_in, mu_prev, mu_next, w0, w2, a0, a2, g2, k_k, k_a, r_k, lnx_g, lnx_b,
           sink, w_out, ln1_g, ln1_b, peer_wq, peer_keys, peer_u, peer_v, ln2_g, ln2_b):
    w = RWKV_WIDTH
    row = lambda a: a.reshape(1, -1).astype(F32)
    w_in_p = jnp.pad(w_in, ((0, 0), (0, RWKV_COLS_PAD - RWKV_COLS))).astype(BF16)
    mup = jnp.pad(mu_prev, (0, RWKV_COLS_PAD - RWKV_COLS)).reshape(1, -1)
    mun = jnp.pad(mu_next, (0, RWKV_COLS_PAD - RWKV_COLS)).reshape(1, -1)
    wmix = jnp.zeros((LANES, 4 * w), F32)
    for d in range(2):
        wmix = wmix.at[d * DECAY_RANK:(d + 1) * DECAY_RANK, d * w:(d + 1) * w].set(w2[d])
        r0 = 2 * DECAY_RANK + d * ICLR_RANK
        wmix = wmix.at[r0:r0 + ICLR_RANK, (2 + d) * w:(3 + d) * w].set(a2[d])
    g2p = jnp.pad(g2, ((0, LANES - GATE_RANK), (0, 0)))
    head_of = np.arange(w) // HEAD_DIM
    bd = jnp.asarray(head_of[:, None] == head_of[None, :], F32)
    cos_t, sin_t = _rope_tables(seq)

    qkv, pr = _proj(h2d, w_in_p, cos_t, sin_t, seq)
    y_attn = _attention(qkv.reshape(batch, seq, ATTN_COLS), sink.astype(F32))
    r, v, kk, g, lw0, lw1, kd0, kd1, b0, b1 = _prep(
        pr, mup, mun, wmix, w0.astype(F32), a0.astype(F32), g2p, row(k_k), row(k_a), bd, seq)
    yf, yb = _wkv(r, v, kk, lw0, lw1, kd0, kd1, b0, b1, batch, seq)
    h1 = _mix(h2d, y_attn.reshape(batch * seq, ATTN_WIDTH), yf, yb, r, v, kd0, kd1, g,
              row(lnx_g), row(lnx_b), row(r_k), bd, w_out.astype(BF16), row(ln1_g), row(ln1_b))
    idx, gate_t, gate_tok = _route(h1, peer_wq.astype(BF16), peer_keys.astype(F32))
    uv = jnp.concatenate([peer_u, peer_v], axis=1)
    uv = uv.reshape(uv.shape[0], uv.shape[1] // LANES, 1, LANES)
    l2g, l2b = row(ln2_g), row(ln2_b)

    t = batch * seq
    n_sc = (int(t * SC_TOKEN_SHARE) // SC_SPLIT_ALIGN) * SC_SPLIT_ALIGN
    n_tc = t - n_sc
    per_call = n_tc // TC_PEER_CALLS
    assert n_sc % SC_WORKERS == 0 and per_call * TC_PEER_CALLS == n_tc and per_call % PEER_TB == 0
    outs = []
    if n_sc:
        zp = _sc_z(peer_u.astype(F32), idx, h1, 0, n_sc)
    if n_tc:
        outs.append(_peer(idx, gate_t, h1, uv, l2g, l2b, n_sc, per_call))
    if n_sc:
        f_sc = _sc_out(peer_v.astype(F32), idx, _coef(zp, gate_tok, 0, outs[0]), 0)
    for c in range(1, TC_PEER_CALLS):
        outs.append(_peer(idx, gate_t, h1, uv, l2g, l2b, n_sc + c * per_call, per_call))
    if n_sc:
        outs.insert(0, _ln_out(h1, f_sc, l2g, l2b, 0))
    return jnp.concatenate(outs, axis=0)


def kernel(x, w_in, mu_prev, mu_next, w0, w2, a0, a2, g2, k_k, k_a, r_k, lnx_g, lnx_b, sink, w_out,
           ln1_g, ln1_b, peer_wq, peer_keys, peer_u, peer_v, ln2_g, ln2_b):
    batch, seq, d = x.shape
    h = x.reshape(batch * seq, d)
    for l in range(DEPTH):
        h = _layer(h, batch, seq, w_in[l], mu_prev[l], mu_next[l], w0[l], w2[l], a0[l], a2[l], g2[l],
                   k_k[l], k_a[l], r_k[l], lnx_g[l], lnx_b[l], sink[l], w_out[l], ln1_g[l], ln1_b[l],
                   peer_wq[l], peer_keys[l], peer_u[l], peer_v[l], ln2_g[l], ln2_b[l])
    return h.reshape(batch, seq, d)
```

```python
import functools

import numpy as np
import jax
import jax.numpy as jnp
from jax import lax
from jax.experimental import pallas as pl
from jax.experimental.pallas import tpu as pltpu
from jax.experimental.pallas import tpu_sc as plsc

F32 = jnp.float32
BF16 = jnp.bfloat16
HI = lax.Precision.HIGHEST

HEAD_DIM = 64
N_Q_HEADS = 8
N_KV_HEADS = 2
Q_PER_KV = N_Q_HEADS // N_KV_HEADS
ATTN_WIDTH = N_Q_HEADS * HEAD_DIM
ATTN_KV_WIDTH = N_KV_HEADS * HEAD_DIM
ATTN_COLS = ATTN_WIDTH + 2 * ATTN_KV_WIDTH
WINDOW = 128
ATTN_BLOCK = 128
ROPE_THETA = 10000.0
N_RWKV_HEADS = 8
RWKV_WIDTH = N_RWKV_HEADS * HEAD_DIM
DECAY_RANK = 32
ICLR_RANK = 32
GATE_RANK = 96
RWKV_COLS = 3 * RWKV_WIDTH + 2 * DECAY_RANK + 2 * ICLR_RANK + GATE_RANK
RWKV_COLS_PAD = 1792
RWKV_GN_EPS = 64e-5
PEER_HEADS = 8
PEER_NKEYS = 128
PEER_QDIM = 256
PEER_HALF = PEER_QDIM // 2
PEER_TOPK = 16
N_SEL = PEER_HEADS * PEER_TOPK
LN_EPS = 1e-5
DEPTH = 1
DEEPNORM_ALPHA = (2.0 * DEPTH) ** 0.25

LANES = 128
SUBLANES = 8
WKV_CHUNK = 64
VMEM_LIMIT = 48 * 1024 * 1024
PEER_TB = 128
SC_LANES = 16
SC_WORKERS = 32
SC_GROUP = 32
SC_CHUNK_UNROLL = 8
SC_OUT_CHUNKS = 4
PEER_SPLIT_64THS = (36, 12, 16)


def _cparams(sem):
    return pltpu.CompilerParams(dimension_semantics=sem, vmem_limit_bytes=VMEM_LIMIT)


def _dot(a, b, precision=None):
    return jnp.dot(a, b, preferred_element_type=F32, precision=precision)


def _dot_nt(a, b, precision=None):
    return lax.dot_general(a, b, (((1,), (1,)), ((), ())), preferred_element_type=F32,
                           precision=precision)


def _dot_tn(a, b, precision=None):
    return lax.dot_general(a, b, (((0,), (0,)), ((), ())), preferred_element_type=F32,
                           precision=precision)


def _layer_norm(z, g, b):
    mu = jnp.mean(z, -1, keepdims=True)
    zc = z - mu
    var = jnp.mean(zc * zc, -1, keepdims=True)
    return zc * lax.rsqrt(var + LN_EPS) * g + b


def _proj_kernel(x_ref, w_ref, cos_ref, sin_ref, qkv_ref, pr_ref):
    xb = x_ref[...].astype(BF16)
    cos = cos_ref[...]
    sin = sin_ref[...]
    lane = lax.broadcasted_iota(jnp.int32, cos.shape, 1)
    first_half = (lane & (HEAD_DIM // 2)) == 0

    def rope(t):
        rot = jnp.where(first_half, pltpu.roll(t, LANES - HEAD_DIM // 2, 1),
                        pltpu.roll(t, HEAD_DIM // 2, 1))
        return t * cos + rot * sin

    for c in range(0, ATTN_COLS, 2 * LANES):
        acc = _dot(xb, w_ref[:, c:c + 2 * LANES])
        for half in range(2):
            col = c + half * LANES
            t = acc[:, half * LANES:(half + 1) * LANES]
            if col < ATTN_WIDTH + ATTN_KV_WIDTH:
                t = rope(t)
            qkv_ref[:, col:col + LANES] = t
    for c in range(0, RWKV_COLS_PAD, 2 * LANES):
        pr_ref[:, c:c + 2 * LANES] = _dot(xb, w_ref[:, ATTN_COLS + c:ATTN_COLS + c + 2 * LANES])


def _proj(x2, w_in_p, cos_t, sin_t, seq, tm=512):
    t, d = x2.shape
    n_pos = seq // tm
    return pl.pallas_call(
        _proj_kernel,
        grid=(t // tm,),
        in_specs=[
            pl.BlockSpec((tm, d), lambda i: (i, 0)),
            pl.BlockSpec(w_in_p.shape, lambda i: (0, 0)),
            pl.BlockSpec((tm, LANES), lambda i: (i % n_pos, 0)),
            pl.BlockSpec((tm, LANES), lambda i: (i % n_pos, 0)),
        ],
        out_specs=[
            pl.BlockSpec((tm, ATTN_COLS), lambda i: (i, 0)),
            pl.BlockSpec((tm, RWKV_COLS_PAD), lambda i: (i, 0)),
        ],
        out_shape=[
            jax.ShapeDtypeStruct((t, ATTN_COLS), F32),
            jax.ShapeDtypeStruct((t, RWKV_COLS_PAD), F32),
        ],
        compiler_params=_cparams(("parallel",)),
        name="proj",
    )(x2, w_in_p, cos_t, sin_t)


def _attn_kernel(sink_ref, q_ref, kp_ref, kc_ref, kn_ref, vp_ref, vc_ref, vn_ref, o_ref, *, nb):
    n = pl.program_id(1)
    blk = ATTN_BLOCK
    rows = Q_PER_KV * blk
    q = q_ref[0] * (HEAD_DIM ** -0.5)
    kwin = jnp.concatenate([kp_ref[0], kc_ref[0], kn_ref[0]], axis=0)
    vwin = jnp.concatenate([vp_ref[0], vc_ref[0], vn_ref[0]], axis=0)
    qi = lax.broadcasted_iota(jnp.int32, (rows, 3 * blk), 0) & (blk - 1)
    kj = lax.broadcasted_iota(jnp.int32, (rows, 3 * blk), 1)
    dist = kj - qi
    valid = ((dist >= blk - WINDOW) & (dist <= blk + WINDOW)
             & ((kj >= blk) | (n > 0)) & ((kj < 2 * blk) | (n < nb - 1)))
    rowg = lax.broadcasted_iota(jnp.int32, (rows, 1), 0) // blk
    outs = []
    for h in range(N_KV_HEADS):
        qs = jnp.concatenate(
            [q[:, (Q_PER_KV * h + g) * HEAD_DIM:(Q_PER_KV * h + g + 1) * HEAD_DIM]
             for g in range(Q_PER_KV)], axis=0)
        kh = kwin[:, h * HEAD_DIM:(h + 1) * HEAD_DIM]
        vh = vwin[:, h * HEAD_DIM:(h + 1) * HEAD_DIM]
        logits = _dot_nt(qs.astype(BF16), kh.astype(BF16))
        logits = jnp.where(valid, logits, -1e30)
        sk = jnp.zeros((rows, 1), F32)
        for g in range(Q_PER_KV):
            sk = jnp.where(rowg == g, sink_ref[Q_PER_KV * h + g], sk)
        m = jnp.maximum(jnp.max(logits, -1, keepdims=True), sk)
        e = jnp.exp(logits - m)
        den = jnp.sum(e, -1, keepdims=True) + jnp.exp(sk - m)
        p = e / den
        o = _dot(p.astype(BF16), vh.astype(BF16))
        for g in range(Q_PER_KV):
            outs.append(o[g * blk:(g + 1) * blk])
    o_ref[0] = jnp.concatenate(outs, axis=1)


def _attention(qkv3, sink):
    b, s, _ = qkv3.shape
    blk = ATTN_BLOCK
    nb = s // blk
    kcol = ATTN_WIDTH // LANES
    vcol = kcol + 1

    def spec(col, shift):
        def imap(bi, n):
            return (bi, jnp.clip(n + shift, 0, nb - 1), col)
        return pl.BlockSpec((1, blk, LANES), imap)

    return pl.pallas_call(
        functools.partial(_attn_kernel, nb=nb),
        grid=(b, nb),
        in_specs=[
            pl.BlockSpec(memory_space=pltpu.SMEM),
            pl.BlockSpec((1, blk, ATTN_WIDTH), lambda bi, n: (bi, n, 0)),
            spec(kcol, -1), spec(kcol, 0), spec(kcol, 1),
            spec(vcol, -1), spec(vcol, 0), spec(vcol, 1),
        ],
        out_specs=pl.BlockSpec((1, blk, ATTN_WIDTH), lambda bi, n: (bi, n, 0)),
        out_shape=jax.ShapeDtypeStruct((b, s, ATTN_WIDTH), F32),
        compiler_params=_cparams(("parallel", "parallel")),
        name="attn",
    )(sink, qkv3, qkv3, qkv3, qkv3, qkv3, qkv3, qkv3)


def _softplus(x):
    return jnp.maximum(x, 0.0) + jnp.log(1.0 + jnp.exp(-jnp.abs(x)))


def _sigmoid(x):
    return 1.0 / (1.0 + jnp.exp(-x))


def _prep_kernel(p_ref, hp_ref, hn_ref, mup_ref, mun_ref, wmix_ref, w0_ref, a0_ref, g2_ref,
                 kk_ref, ka_ref, bd_ref,
                 r_o, v_o, kk_o, g_o, lw0_o, lw1_o, kd0_o, kd1_o, b0_o, b1_o, *, tm, seq):
    i = pl.program_id(0)
    row = lax.broadcasted_iota(jnp.int32, (tm, 1), 0)
    seq_start = (i * tm) % seq == 0
    seq_end = ((i + 1) * tm) % seq == 0

    def shifted(c0, c1):
        p = p_ref[:, c0:c1]
        prev_row = jnp.where(seq_start, 0.0, hp_ref[SUBLANES - 1:SUBLANES, c0:c1])
        next_row = jnp.where(seq_end, 0.0, hn_ref[0:1, c0:c1])
        p_prev = jnp.where(row == 0, prev_row, pltpu.roll(p, 1, 0))
        p_next = jnp.where(row == tm - 1, next_row, pltpu.roll(p, tm - 1, 0))
        return p + mup_ref[:, c0:c1] * (p_prev - p) + mun_ref[:, c0:c1] * (p_next - p)

    w = RWKV_WIDTH
    r = shifted(0, w)
    k = shifted(w, 2 * w)
    v = shifted(2 * w, 3 * w)
    codes = shifted(3 * w, 3 * w + LANES)
    gd = shifted(3 * w + LANES, 3 * w + 2 * LANES)
    r_o[...] = r
    v_o[...] = v

    lane = lax.broadcasted_iota(jnp.int32, codes.shape, 1)
    codes = jnp.where(lane < 2 * DECAY_RANK, jnp.tanh(codes), codes)
    mm = _dot(codes, wmix_ref[...], HI)
    g_o[...] = _dot(_sigmoid(gd), g2_ref[...], HI)

    kkv = k * kk_ref[...]
    ss = _dot(kkv * kkv, bd_ref[...], HI)
    kkn = kkv * lax.rsqrt(jnp.maximum(ss, 1e-24))
    kk_o[...] = kkn

    ka = ka_ref[...]
    for d, (lw_o, kd_o, b_o) in enumerate(((lw0_o, kd0_o, b0_o), (lw1_o, kd1_o, b1_o))):
        w_log = -_softplus(-(w0_ref[d:d + 1, :] + mm[:, d * w:(d + 1) * w])) - 0.5
        lw_o[...] = -jnp.exp(w_log)
        a = _sigmoid(a0_ref[d:d + 1, :] + mm[:, (2 + d) * w:(3 + d) * w])
        kd_o[...] = k * (1.0 + (a - 1.0) * ka)
        b_o[...] = kkn * a


def _prep(pr, mup, mun, wmix, w0, a0, g2p, k_k, k_a, bd, seq, tm=256):
    t = pr.shape[0]
    nblk8 = t // SUBLANES
    per = tm // SUBLANES
    full = lambda a: pl.BlockSpec(a.shape, lambda i: (0,) * a.ndim)
    out = pl.BlockSpec((tm, RWKV_WIDTH), lambda i: (i, 0))
    return pl.pallas_call(
        functools.partial(_prep_kernel, tm=tm, seq=seq),
        grid=(t // tm,),
        in_specs=[
            pl.BlockSpec((tm, RWKV_COLS_PAD), lambda i: (i, 0)),
            pl.BlockSpec((SUBLANES, RWKV_COLS_PAD), lambda i: (jnp.maximum(i * per - 1, 0), 0)),
            pl.BlockSpec((SUBLANES, RWKV_COLS_PAD),
                         lambda i: (jnp.minimum((i + 1) * per, nblk8 - 1), 0)),
            full(mup), full(mun), full(wmix), full(w0), full(a0), full(g2p),
            full(k_k), full(k_a), full(bd),
        ],
        out_specs=[out] * 10,
        out_shape=[jax.ShapeDtypeStruct((t, RWKV_WIDTH), F32)] * 10,
        compiler_params=_cparams(("parallel",)),
        name="prep",
    )(pr, pr, pr, mup, mun, wmix, w0, a0, g2p, k_k, k_a, bd)


def _split3(x):
    hi = x.astype(BF16)
    r1 = x - hi.astype(F32)
    mid = r1.astype(BF16)
    lo = (r1 - mid.astype(F32)).astype(BF16)
    return hi, mid, lo


def _wkv_kernel(rf, vf, kkf, lwf, kdf, bf, rb, vb, kkb, lwb, kdb, bb, yf_ref, yb_ref, state):
    c = pl.program_id(1)
    n = WKV_CHUNK
    hd = HEAD_DIM
    nh = N_RWKV_HEADS
    def bmm(a, b, ca, cb):
        return lax.dot_general(a.astype(BF16), b.astype(BF16), (((ca,), (cb,)), ((0,), (0,))),
                               preferred_element_type=F32)

    @pl.when(c == 0)
    def _():
        state[...] = jnp.zeros_like(state)

    ti = lax.broadcasted_iota(jnp.int32, (n, n), 0)
    si = lax.broadcasted_iota(jnp.int32, (n, n), 1)
    dirs = ((rf, vf, kkf, lwf, kdf, bf, si <= ti, si < ti, n - 1),
            (rb, vb, kkb, lwb, kdb, bb, si >= ti, si > ti, 0))
    heads = lambda x: jnp.stack([x[:, j * hd:(j + 1) * hd] for j in range(nh)], 0)
    parts = []
    for r_ref, v_ref, kk_ref, lw_ref, kd_ref, b_ref, incl, strict, last in dirs:
        lw = lw_ref[0]
        tri = incl.astype(BF16)
        cum = sum(_dot(tri, part) for part in _split3(lw))
        e_neg = jnp.exp(-cum)
        e_last = jnp.exp(cum[last:last + 1, :] - cum)
        g_scale = jnp.exp(cum[last:last + 1, :])
        parts.append(dict(
            at=heads((-kk_ref[0] * jnp.exp(cum - lw)).astype(BF16)),
            rt=heads((r_ref[0] * jnp.exp(cum)).astype(BF16)),
            bt=heads((b_ref[0] * e_neg).astype(BF16)),
            kt=heads((kd_ref[0] * e_neg).astype(BF16)),
            bl=heads((b_ref[0] * e_last).astype(BF16)),
            kl=heads((kd_ref[0] * e_last).astype(BF16)),
            v=heads(v_ref[0].astype(BF16)),
            gs=heads(g_scale),
            incl=jnp.broadcast_to(incl[None], (nh, n, n)),
            strict=jnp.broadcast_to(strict[None], (nh, n, n))))
    cat = lambda key: jnp.concatenate([parts[0][key], parts[1][key]], 0)
    at, rt, bt, kt, bl, kl, v, gs = (cat(k) for k in ("at", "rt", "bt", "kt", "bl", "kl", "v", "gs"))
    incl, strict = cat("incl"), cat("strict")
    g0 = state[...]
    g0b = g0.astype(BF16)
    m1 = bmm(jnp.concatenate([at, rt], 1), jnp.concatenate([bt, kt], 1), 2, 2)
    a_ab = jnp.where(strict, m1[:, :n, :n], 0.0)
    a_ak = jnp.where(strict, m1[:, :n, n:], 0.0)
    a_rb = jnp.where(incl, m1[:, n:, :n], 0.0)
    a_rk = jnp.where(incl, m1[:, n:, n:], 0.0)
    tinv = jnp.where((ti == si)[None], 1.0, a_ab)
    pw = a_ab
    for _ in range(int(np.log2(n)) - 1):
        pw = bmm(pw, pw, 2, 1)
        tinv = tinv + bmm(tinv, pw, 2, 1)
    rhs = bmm(a_ak, v, 2, 1) + bmm(at, g0b, 2, 2)
    u = bmm(tinv, rhs, 2, 1)
    uv = jnp.concatenate([u.astype(BF16), v], 1)
    y = bmm(jnp.concatenate([a_rb, a_rk], 2), uv, 2, 1) + bmm(rt, g0b, 2, 2)
    state[...] = g0 * gs + bmm(uv, jnp.concatenate([bl, kl], 1), 1, 1)
    for d, y_ref in enumerate((yf_ref, yb_ref)):
        y_ref[0] = jnp.concatenate([y[d * nh + j] for j in range(nh)], axis=1)


def _wkv(r, v, kk, lw0, lw1, kd0, kd1, b0, b1, batch, seq):
    n = WKV_CHUNK
    nc = seq // n
    shp = (batch, seq, RWKV_WIDTH)
    arrs = [a.reshape(shp) for a in (r, v, kk, lw0, kd0, b0, r, v, kk, lw1, kd1, b1)]
    fwd = pl.BlockSpec((1, n, RWKV_WIDTH), lambda b, c: (b, c, 0))
    bwd = pl.BlockSpec((1, n, RWKV_WIDTH), lambda b, c: (b, nc - 1 - c, 0))
    yf, yb = pl.pallas_call(
        _wkv_kernel,
        grid=(batch, nc),
        in_specs=[fwd] * 6 + [bwd] * 6,
        out_specs=[fwd, bwd],
        out_shape=[jax.ShapeDtypeStruct(shp, F32)] * 2,
        scratch_shapes=[pltpu.VMEM((2 * N_RWKV_HEADS, HEAD_DIM, HEAD_DIM), F32)],
        compiler_params=_cparams(("parallel", "arbitrary")),
        name="wkv",
    )(*arrs)
    return yf.reshape(batch * seq, RWKV_WIDTH), yb.reshape(batch * seq, RWKV_WIDTH)


def _mix_kernel(x_ref, ya_ref, yf_ref, yb_ref, r_ref, v_ref, kd0_ref, kd1_ref, g_ref,
                lng_ref, lnb_ref, rk_ref, bd_ref, wout_ref, l1g_ref, l1b_ref, h_ref):
    bd = bd_ref[...]
    inv = 1.0 / HEAD_DIM
    y = yf_ref[...] + yb_ref[...]
    mu = _dot(y, bd, HI) * inv
    yc = y - mu
    var = _dot(yc * yc, bd, HI) * inv
    yn = yc * lax.rsqrt(var + RWKV_GN_EPS) * lng_ref[...] + lnb_ref[...]
    k_mean = 0.5 * (kd0_ref[...] + kd1_ref[...])
    v = v_ref[...]
    bonus = _dot(r_ref[...] * k_mean * rk_ref[...], bd, HI) * v
    yr = (yn + bonus) * g_ref[...]
    mix = (_dot(ya_ref[...].astype(BF16), wout_ref[:ATTN_WIDTH, :])
           + _dot(yr.astype(BF16), wout_ref[ATTN_WIDTH:, :]))
    h_ref[...] = _layer_norm(DEEPNORM_ALPHA * x_ref[...] + mix, l1g_ref[...], l1b_ref[...])


def _mix(x2, ya, yf, yb, r, v, kd0, kd1, g, lng, lnb, rk, bd, wout, l1g, l1b, tm=256):
    t, d = x2.shape
    full = lambda a: pl.BlockSpec(a.shape, lambda i: (0,) * a.ndim)
    half = pl.BlockSpec((tm, RWKV_WIDTH), lambda i: (i, 0))
    wide = pl.BlockSpec((tm, d), lambda i: (i, 0))
    return pl.pallas_call(
        _mix_kernel,
        grid=(t // tm,),
        in_specs=[wide] + [half] * 8 + [full(lng), full(lnb), full(rk), full(bd), full(wout),
                                        full(l1g), full(l1b)],
        out_specs=wide,
        out_shape=jax.ShapeDtypeStruct((t, d), F32),
        compiler_params=_cparams(("parallel",)),
        name="mix",
    )(x2, ya, yf, yb, r, v, kd0, kd1, g, lng, lnb, rk, bd, wout, l1g, l1b)


def _top_rows(sc, k, payload=None):
    n = sc.shape[0]
    iota = lax.broadcasted_iota(jnp.int32, sc.shape, 0)
    vals, picks = [], []
    for _ in range(k):
        m = jnp.max(sc, axis=0, keepdims=True)
        pos = jnp.min(jnp.where(sc == m, iota, n), axis=0, keepdims=True)
        hit = iota == pos
        vals.append(m)
        if payload is None:
            picks.append(pos)
        else:
            picks.append(jnp.max(jnp.where(hit, payload, -1), axis=0, keepdims=True))
        sc = jnp.where(hit, -jnp.inf, sc)
    return jnp.concatenate(vals, 0), jnp.concatenate(picks, 0)


def _route_kernel(h_ref, wq_ref, keys_ref, idx_ref, gate_ref, gate_tok_ref, q_scr, idx_scr):
    k = PEER_TOPK
    q_scr[...] = _dot(h_ref[...].astype(BF16), wq_ref[...])

    def head(hh, carry):
        tops = []
        for p in range(2):
            col = pl.multiple_of(hh * PEER_QDIM + p * PEER_HALF, PEER_HALF)
            qp = q_scr[:, pl.ds(col, PEER_HALF)]
            sct = _dot_nt(keys_ref[hh, p], qp, HI)
            tops.append(_top_rows(sct, k))
        (s1, i1), (s2, i2) = tops
        cand = jnp.concatenate([s1[i:i + 1] + s2[:k // (i + 1)] for i in range(k)], 0)
        cand_id = jnp.concatenate([i1[i:i + 1] * PEER_NKEYS + i2[:k // (i + 1)] for i in range(k)], 0)
        cs, ids = _top_rows(cand, k, cand_id)
        e = jnp.exp(cs - jnp.max(cs, axis=0, keepdims=True))
        row = pl.multiple_of(hh * k, k)
        gate_ref[pl.ds(row, k), :] = e / jnp.sum(e, axis=0, keepdims=True)
        idx_scr[pl.ds(row, k), :] = ids
        return carry

    lax.fori_loop(0, PEER_HEADS, head, 0)
    idx_ref[...] = idx_scr[...].T
    gate_tok_ref[...] = gate_ref[...].T


def _route(h1, wq, keys, tm=256):
    t, d = h1.shape
    return pl.pallas_call(
        _route_kernel,
        grid=(t // tm,),
        in_specs=[
            pl.BlockSpec((tm, d), lambda i: (i, 0)),
            pl.BlockSpec(wq.shape, lambda i: (0, 0)),
            pl.BlockSpec(keys.shape, lambda i: (0, 0, 0, 0)),
        ],
        out_specs=[
            pl.BlockSpec((tm, N_SEL), lambda i: (i, 0)),
            pl.BlockSpec((N_SEL, tm), lambda i: (0, i)),
            pl.BlockSpec((tm, N_SEL), lambda i: (i, 0)),
        ],
        out_shape=[
            jax.ShapeDtypeStruct((t, N_SEL), jnp.int32),
            jax.ShapeDtypeStruct((N_SEL, t), F32),
            jax.ShapeDtypeStruct((t, N_SEL), F32),
        ],
        scratch_shapes=[pltpu.VMEM((tm, PEER_HEADS * PEER_QDIM), F32),
                        pltpu.VMEM((N_SEL, tm), jnp.int32)],
        compiler_params=_cparams(("parallel",)),
        name="route",
    )(h1, wq, keys)


def _peer_kernel(idx_ref, gate_ref, h_ref, uv_hbm, l2g_ref, l2b_ref, o_ref, buf, f_scr, sem, *, tb):
    d = h_ref.shape[1]

    def issue(t, slot):
        for j in range(N_SEL):
            pltpu.make_async_copy(uv_hbm.at[pl.ds(idx_ref[t, j], 1)], buf.at[slot, pl.ds(j, 1)],
                                  sem.at[slot]).start()

    def wait_all(slot):
        pltpu.make_async_copy(uv_hbm.at[pl.ds(0, N_SEL)], buf.at[slot], sem.at[slot]).wait()

    issue(0, 0)
    lane = lax.broadcasted_iota(jnp.int32, (N_SEL, tb), 1)

    def body(t, carry):
        slot = t % 2

        @pl.when(t + 1 < tb)
        def _():
            issue(t + 1, 1 - slot)

        wait_all(slot)
        z = jnp.sum(buf[slot, :, 0:d] * h_ref[pl.ds(t, 1), :], axis=1, keepdims=True)
        gcol = jnp.sum(jnp.where(lane == t, gate_ref[...], 0.0), axis=1, keepdims=True)
        coef = gcol * (0.5 * z * (1.0 + lax.erf(z * (2.0 ** -0.5))))
        f_scr[pl.ds(t, 1), :] = jnp.sum(coef * buf[slot, :, d:2 * d], axis=0, keepdims=True)
        return carry

    lax.fori_loop(0, tb, body, 0)
    o_ref[...] = _layer_norm(DEEPNORM_ALPHA * h_ref[...] + f_scr[...], l2g_ref[...], l2b_ref[...])


def _peer(idx, gate_t, h1, uv, l2g, l2b, tok0, ntok, tb=PEER_TB):
    d = h1.shape[1]
    b0 = tok0 // tb
    return pl.pallas_call(
        functools.partial(_peer_kernel, tb=tb),
        grid=(ntok // tb,),
        in_specs=[
            pl.BlockSpec((tb, N_SEL), lambda i: (b0 + i, 0), memory_space=pltpu.SMEM),
            pl.BlockSpec((N_SEL, tb), lambda i: (0, b0 + i)),
            pl.BlockSpec((tb, d), lambda i: (b0 + i, 0)),
            pl.BlockSpec(memory_space=pl.ANY),
            pl.BlockSpec(l2g.shape, lambda i: (0, 0)),
            pl.BlockSpec(l2b.shape, lambda i: (0, 0)),
        ],
        out_specs=pl.BlockSpec((tb, d), lambda i: (i, 0)),
        out_shape=jax.ShapeDtypeStruct((ntok, d), F32),
        scratch_shapes=[pltpu.VMEM((2, N_SEL, 2 * d), F32),
                        pltpu.VMEM((tb, d), F32),
                        pltpu.SemaphoreType.DMA((2,))],
        compiler_params=_cparams(("arbitrary",)),
        name="peer",
    )(idx, gate_t, h1, uv, l2g, l2b)


def _sc_mesh():
    return plsc.VectorSubcoreMesh(core_axis_name="c", subcore_axis_name="s")


def _worker_id():
    return lax.axis_index("s") * 2 + lax.axis_index("c")


def _sc_z_body(u_hbm, idx_hbm, h_hbm, z_hbm, idx_v, h_v, rows_v, z_v, sem, *, tpw, d, tok0):
    ngrp = N_SEL // SC_GROUP
    nchunk = d // SC_LANES
    nsteps = tpw * ngrp
    base = _worker_id() * tpw

    def gather(tok_buf, g, buf):
        return pltpu.make_async_copy(u_hbm.at[idx_v.at[tok_buf, g]], rows_v.at[buf], sem.at[buf])

    pltpu.sync_copy(idx_hbm.at[tok0 + base], idx_v.at[0])
    gather(0, 0, 0).start()

    @pl.loop(0, nsteps)
    def _(q):
        t_loc = q // ngrp
        g = q % ngrp
        buf = q % 2
        t = base + t_loc

        @pl.when(g == 0)
        def _():
            pltpu.sync_copy(h_hbm.at[tok0 + t], h_v)

            @pl.loop(0, N_SEL)
            def _(r):
                z_v[pl.ds(pl.multiple_of(r * SC_LANES, SC_LANES), SC_LANES)] = jnp.zeros((SC_LANES,), F32)

        @pl.when(q + 1 < nsteps)
        def _():
            tn = (q + 1) // ngrp
            gn = (q + 1) % ngrp

            @pl.when(gn == 0)
            def _():
                pltpu.sync_copy(idx_hbm.at[tok0 + base + tn], idx_v.at[tn % 2])

            gather(tn % 2, gn, 1 - buf).start()

        gather(t_loc % 2, g, buf).wait()

        @pl.loop(0, SC_GROUP, step=4)
        def _(rb):
            @pl.loop(0, nchunk, step=SC_CHUNK_UNROLL)
            def _(c0):
                accs = [jnp.zeros((SC_LANES,), F32) for _ in range(4)]
                for cc in range(SC_CHUNK_UNROLL):
                    sl = pl.ds(pl.multiple_of((c0 + cc) * SC_LANES, SC_LANES), SC_LANES)
                    hc = h_v[sl]
                    for i in range(4):
                        accs[i] = accs[i] + rows_v[buf, rb + i, sl] * hc
                for i in range(4):
                    row = pl.multiple_of((g * SC_GROUP + rb + i) * SC_LANES, SC_LANES)
                    plsc.addupdate(z_v.at[pl.ds(row, SC_LANES)], accs[i])

        @pl.when(g == ngrp - 1)
        def _():
            pltpu.sync_copy(z_v, z_hbm.at[t])


def _sc_z(u, idx, h, tok0, ntok):
    t, d = h.shape
    tpw = ntok // SC_WORKERS
    idx4 = idx.reshape(t, N_SEL // SC_GROUP, SC_GROUP)
    body = functools.partial(_sc_z_body, tpw=tpw, d=d, tok0=tok0)
    return pl.kernel(
        body,
        out_type=jax.ShapeDtypeStruct((ntok, N_SEL * SC_LANES), F32),
        mesh=_sc_mesh(),
        scratch_types=[
            pltpu.VMEM((2, N_SEL // SC_GROUP, SC_GROUP), jnp.int32),
            pltpu.VMEM((d,), F32),
            pltpu.VMEM((2, SC_GROUP, d), F32),
            pltpu.VMEM((N_SEL * SC_LANES,), F32),
            pltpu.SemaphoreType.DMA((2,)),
        ],
        name="sc_z",
    )(u, idx4, h)


def _sc_out_body(v_hbm, idx_hbm, coef_hbm, f_hbm, idx_v, coef_v, rows_v, out_v, sem, *, tpw, d, tok0):
    ngrp = N_SEL // SC_GROUP
    nchunk = d // SC_LANES
    nsteps = tpw * ngrp
    base = _worker_id() * tpw

    def gather(tok_buf, g, buf):
        return pltpu.make_async_copy(v_hbm.at[idx_v.at[tok_buf, g]], rows_v.at[buf], sem.at[buf])

    pltpu.sync_copy(idx_hbm.at[tok0 + base], idx_v.at[0])
    gather(0, 0, 0).start()

    @pl.loop(0, nsteps)
    def _(q):
        t_loc = q // ngrp
        g = q % ngrp
        buf = q % 2
        t = base + t_loc

        @pl.when(g == 0)
        def _():
            pltpu.sync_copy(coef_hbm.at[t], coef_v)

        @pl.when(q + 1 < nsteps)
        def _():
            tn = (q + 1) // ngrp
            gn = (q + 1) % ngrp

            @pl.when(gn == 0)
            def _():
                pltpu.sync_copy(idx_hbm.at[tok0 + base + tn], idx_v.at[tn % 2])

            gather(tn % 2, gn, 1 - buf).start()

        gather(t_loc % 2, g, buf).wait()

        @pl.loop(0, nchunk, step=SC_OUT_CHUNKS)
        def _(c0):
            sls = [pl.ds(pl.multiple_of((c0 + cc) * SC_LANES, SC_LANES), SC_LANES)
                   for cc in range(SC_OUT_CHUNKS)]
            accs = [None] * SC_OUT_CHUNKS
            for r in range(SC_GROUP):
                cf = coef_v[pl.ds(pl.multiple_of((g * SC_GROUP + r) * SC_LANES, SC_LANES), SC_LANES)]
                for cc in range(SC_OUT_CHUNKS):
                    p = rows_v[buf, r, sls[cc]] * cf
                    accs[cc] = p if accs[cc] is None else accs[cc] + p
            for cc in range(SC_OUT_CHUNKS):
                @pl.when(g == 0)
                def _():
                    out_v[sls[cc]] = accs[cc]

                @pl.when(g != 0)
                def _():
                    plsc.addupdate(out_v.at[sls[cc]], accs[cc])

        @pl.when(g == ngrp - 1)
        def _():
            pltpu.sync_copy(out_v, f_hbm.at[t])


def _sc_out(v, idx, coef16, tok0):
    t = idx.shape[0]
    ntok = coef16.shape[0]
    d = v.shape[1]
    tpw = ntok // SC_WORKERS
    idx4 = idx.reshape(t, N_SEL // SC_GROUP, SC_GROUP)
    body = functools.partial(_sc_out_body, tpw=tpw, d=d, tok0=tok0)
    return pl.kernel(
        body,
        out_type=jax.ShapeDtypeStruct((ntok, d), F32),
        mesh=_sc_mesh(),
        scratch_types=[
            pltpu.VMEM((2, N_SEL // SC_GROUP, SC_GROUP), jnp.int32),
            pltpu.VMEM((N_SEL * SC_LANES,), F32),
            pltpu.VMEM((2, SC_GROUP, d), F32),
            pltpu.VMEM((d,), F32),
            pltpu.SemaphoreType.DMA((2,)),
        ],
        name="sc_out",
    )(v, idx4, coef16)


def _coef_kernel(zp_ref, gate_ref, sel_ref, selt_ref, after_ref, o_ref):
    z = sum(_dot(part, sel_ref[...]) for part in _split3(zp_ref[...]))
    coef = gate_ref[...] * (0.5 * z * (1.0 + lax.erf(z * (2.0 ** -0.5))))
    o_ref[...] = sum(_dot(part, selt_ref[...]) for part in _split3(coef))


def _coef(zp, gate_tok, tok0, after, tm=256):
    ntok, wide = zp.shape
    b0 = tok0 // tm
    grp = np.arange(wide) // SC_LANES
    sel = jnp.asarray(grp[:, None] == np.arange(N_SEL)[None, :], BF16)
    return pl.pallas_call(
        _coef_kernel,
        grid=(ntok // tm,),
        in_specs=[
            pl.BlockSpec((tm, wide), lambda i: (i, 0)),
            pl.BlockSpec((tm, N_SEL), lambda i: (b0 + i, 0)),
            pl.BlockSpec(sel.shape, lambda i: (0, 0)),
            pl.BlockSpec(sel.shape[::-1], lambda i: (0, 0)),
            pl.BlockSpec((SUBLANES, LANES), lambda i: (0, 0)),
        ],
        out_specs=pl.BlockSpec((tm, wide), lambda i: (i, 0)),
        out_shape=jax.ShapeDtypeStruct((ntok, wide), F32),
        compiler_params=_cparams(("parallel",)),
        name="coef",
    )(zp, gate_tok, sel, sel.T, after)


def _ln_out_kernel(h_ref, f_ref, g_ref, b_ref, o_ref):
    o_ref[...] = _layer_norm(DEEPNORM_ALPHA * h_ref[...] + f_ref[...], g_ref[...], b_ref[...])


def _ln_out(h1, f, l2g, l2b, tok0, tm=256):
    ntok, d = f.shape
    b0 = tok0 // tm
    return pl.pallas_call(
        _ln_out_kernel,
        grid=(ntok // tm,),
        in_specs=[
            pl.BlockSpec((tm, d), lambda i: (b0 + i, 0)),
            pl.BlockSpec((tm, d), lambda i: (i, 0)),
            pl.BlockSpec(l2g.shape, lambda i: (0, 0)),
            pl.BlockSpec(l2b.shape, lambda i: (0, 0)),
        ],
        out_specs=pl.BlockSpec((tm, d), lambda i: (i, 0)),
        out_shape=jax.ShapeDtypeStruct((ntok, d), F32),
        compiler_params=_cparams(("parallel",)),
        name="ln_out",
    )(h1, f, l2g, l2b)


def _rope_tables(seq):
    pos = jnp.arange(seq, dtype=F32)
    inv_freq = ROPE_THETA ** (-jnp.arange(0, HEAD_DIM, 2, dtype=F32) / HEAD_DIM)
    ang = pos[:, None] * inv_freq[None, :]
    ang = jnp.concatenate([ang, ang], -1)
    sign = jnp.concatenate([-jnp.ones((HEAD_DIM // 2,), F32), jnp.ones((HEAD_DIM // 2,), F32)])
    reps = LANES // HEAD_DIM
    return jnp.tile(jnp.cos(ang), (1, reps)), jnp.tile(jnp.sin(ang) * sign, (1, reps))


def _layer(h2d, batch, seq, w_in, mu_prev, mu_next, w0, w2, a0, a2, g2, k_k, k_a, r_k, lnx_g, lnx_b,
           sink, w_out, ln1_g, ln1_b, peer_wq, peer_keys, peer_u, peer_v, ln2_g, ln2_b):
    w = RWKV_WIDTH
    row = lambda a: a.reshape(1, -1).astype(F32)
    w_in_p = jnp.pad(w_in, ((0, 0), (0, RWKV_COLS_PAD - RWKV_COLS))).astype(BF16)
    mup = jnp.pad(mu_prev, (0, RWKV_COLS_PAD - RWKV_COLS)).reshape(1, -1)
    mun = jnp.pad(mu_next, (0, RWKV_COLS_PAD - RWKV_COLS)).reshape(1, -1)
    wmix = jnp.zeros((LANES, 4 * w), F32)
    for d in range(2):
        wmix = wmix.at[d * DECAY_RANK:(d + 1) * DECAY_RANK, d * w:(d + 1) * w].set(w2[d])
        r0 = 2 * DECAY_RANK + d * ICLR_RANK
        wmix = wmix.at[r0:r0 + ICLR_RANK, (2 + d) * w:(3 + d) * w].set(a2[d])
    g2p = jnp.pad(g2, ((0, LANES - GATE_RANK), (0, 0)))
    head_of = np.arange(w) // HEAD_DIM
    bd = jnp.asarray(head_of[:, None] == head_of[None, :], F32)
    cos_t, sin_t = _rope_tables(seq)

    qkv, pr = _proj(h2d, w_in_p, cos_t, sin_t, seq)
    y_attn = _attention(qkv.reshape(batch, seq, ATTN_COLS), sink.astype(F32))
    r, v, kk, g, lw0, lw1, kd0, kd1, b0, b1 = _prep(
        pr, mup, mun, wmix, w0.astype(F32), a0.astype(F32), g2p, row(k_k), row(k_a), bd, seq)
    yf, yb = _wkv(r, v, kk, lw0, lw1, kd0, kd1, b0, b1, batch, seq)
    h1 = _mix(h2d, y_attn.reshape(batch * seq, ATTN_WIDTH), yf, yb, r, v, kd0, kd1, g,
              row(lnx_g), row(lnx_b), row(r_k), bd, w_out.astype(BF16), row(ln1_g), row(ln1_b))
    idx, gate_t, gate_tok = _route(h1, peer_wq.astype(BF16), peer_keys.astype(F32))
    uv = jnp.concatenate([peer_u, peer_v], axis=1)
    l2g, l2b = row(ln2_g), row(ln2_b)

    t = batch * seq
    n_sc, n_a, n_b = (t * share // 64 for share in PEER_SPLIT_64THS)
    assert n_sc + n_a + n_b == t and n_sc % (SC_WORKERS * SUBLANES) == 0
    assert n_a % PEER_TB == 0 and n_b % PEER_TB == 0 and n_sc % PEER_TB == 0
    zp = _sc_z(peer_u.astype(F32), idx, h1, 0, n_sc)
    out_a = _peer(idx, gate_t, h1, uv, l2g, l2b, n_sc, n_a)
    f_sc = _sc_out(peer_v.astype(F32), idx, _coef(zp, gate_tok, 0, out_a), 0)
    out_b = _peer(idx, gate_t, h1, uv, l2g, l2b, n_sc + n_a, n_b)
    out_sc = _ln_out(h1, f_sc, l2g, l2b, 0)
    return jnp.concatenate([out_sc, out_a, out_b], axis=0)


def kernel(x, w_in, mu_prev, mu_next, w0, w2, a0, a2, g2, k_k, k_a, r_k, lnx_g, lnx_b, sink, w_out,
           ln1_g, ln1_b, peer_wq, peer_keys, peer_u, peer_v, ln2_g, ln2_b):
    batch, seq, d = x.shape
    h = x.reshape(batch * seq, d)
    for l in range(DEPTH):
        h = _layer(h, batch, seq, w_in[l], mu_prev[l], mu_next[l], w0[l], w2[l], a0[l], a2[l], g2[l],
                   k_k[l], k_a[l], r_k[l], lnx_g[l], lnx_b[l], sink[l], w_out[l], ln1_g[l], ln1_b[l],
                   peer_wq[l], peer_keys[l], peer_u[l], peer_v[l], ln2_g[l], ln2_b[l])
    return h.reshape(batch, seq, d)
```

```python
import functools

import numpy as np
import jax
import jax.numpy as jnp
from jax import lax
from jax.experimental import pallas as pl
from jax.experimental.pallas import tpu as pltpu
from jax.experimental.pallas import tpu_sc as plsc

F32 = jnp.float32
BF16 = jnp.bfloat16
HI = lax.Precision.HIGHEST

HEAD_DIM = 64
N_Q_HEADS = 8
N_KV_HEADS = 2
Q_PER_KV = N_Q_HEADS // N_KV_HEADS
ATTN_WIDTH = N_Q_HEADS * HEAD_DIM
ATTN_KV_WIDTH = N_KV_HEADS * HEAD_DIM
ATTN_COLS = ATTN_WIDTH + 2 * ATTN_KV_WIDTH
WINDOW = 128
ATTN_BLOCK = 128
ROPE_THETA = 10000.0
N_RWKV_HEADS = 8
RWKV_WIDTH = N_RWKV_HEADS * HEAD_DIM
DECAY_RANK = 32
ICLR_RANK = 32
GATE_RANK = 96
RWKV_COLS = 3 * RWKV_WIDTH + 2 * DECAY_RANK + 2 * ICLR_RANK + GATE_RANK
RWKV_COLS_PAD = 1792
RWKV_GN_EPS = 64e-5
PEER_HEADS = 8
PEER_NKEYS = 128
PEER_QDIM = 256
PEER_HALF = PEER_QDIM // 2
PEER_TOPK = 16
N_SEL = PEER_HEADS * PEER_TOPK
LN_EPS = 1e-5
DEPTH = 1
DEEPNORM_ALPHA = (2.0 * DEPTH) ** 0.25

LANES = 128
SUBLANES = 8
WKV_CHUNK = 64
VMEM_LIMIT = 48 * 1024 * 1024
PEER_TB = 128
SC_LANES = 16
SC_WORKERS = 32
SC_GROUP = 32
SC_CHUNK_UNROLL = 8
SC_OUT_CHUNKS = 4
PEER_SPLIT_64THS = (38, 13, 13)


def _cparams(sem):
    return pltpu.CompilerParams(dimension_semantics=sem, vmem_limit_bytes=VMEM_LIMIT)


def _dot(a, b, precision=None):
    return jnp.dot(a, b, preferred_element_type=F32, precision=precision)


def _dot_nt(a, b, precision=None):
    return lax.dot_general(a, b, (((1,), (1,)), ((), ())), preferred_element_type=F32,
                           precision=precision)


def _dot_tn(a, b, precision=None):
    return lax.dot_general(a, b, (((0,), (0,)), ((), ())), preferred_element_type=F32,
                           precision=precision)


def _layer_norm(z, g, b):
    mu = jnp.mean(z, -1, keepdims=True)
    zc = z - mu
    var = jnp.mean(zc * zc, -1, keepdims=True)
    return zc * lax.rsqrt(var + LN_EPS) * g + b


def _proj_kernel(x_ref, w_ref, cos_ref, sin_ref, qkv_ref, pr_ref):
    xb = x_ref[...].astype(BF16)
    cos = cos_ref[...]
    sin = sin_ref[...]
    lane = lax.broadcasted_iota(jnp.int32, cos.shape, 1)
    first_half = (lane & (HEAD_DIM // 2)) == 0

    def rope(t):
        rot = jnp.where(first_half, pltpu.roll(t, LANES - HEAD_DIM // 2, 1),
                        pltpu.roll(t, HEAD_DIM // 2, 1))
        return t * cos + rot * sin

    for c in range(0, ATTN_COLS, 2 * LANES):
        acc = _dot(xb, w_ref[:, c:c + 2 * LANES])
        for half in range(2):
            col = c + half * LANES
            t = acc[:, half * LANES:(half + 1) * LANES]
            if col < ATTN_WIDTH + ATTN_KV_WIDTH:
                t = rope(t)
            qkv_ref[:, col:col + LANES] = t
    for c in range(0, RWKV_COLS_PAD, 2 * LANES):
        pr_ref[:, c:c + 2 * LANES] = _dot(xb, w_ref[:, ATTN_COLS + c:ATTN_COLS + c + 2 * LANES])


def _proj(x2, w_in_p, cos_t, sin_t, seq, tm=512):
    t, d = x2.shape
    n_pos = seq // tm
    return pl.pallas_call(
        _proj_kernel,
        grid=(t // tm,),
        in_specs=[
            pl.BlockSpec((tm, d), lambda i: (i, 0)),
            pl.BlockSpec(w_in_p.shape, lambda i: (0, 0)),
            pl.BlockSpec((tm, LANES), lambda i: (i % n_pos, 0)),
            pl.BlockSpec((tm, LANES), lambda i: (i % n_pos, 0)),
        ],
        out_specs=[
            pl.BlockSpec((tm, ATTN_COLS), lambda i: (i, 0)),
            pl.BlockSpec((tm, RWKV_COLS_PAD), lambda i: (i, 0)),
        ],
        out_shape=[
            jax.ShapeDtypeStruct((t, ATTN_COLS), F32),
            jax.ShapeDtypeStruct((t, RWKV_COLS_PAD), F32),
        ],
        compiler_params=_cparams(("parallel",)),
        name="proj",
    )(x2, w_in_p, cos_t, sin_t)


def _attn_kernel(sink_ref, q_ref, kp_ref, kc_ref, kn_ref, vp_ref, vc_ref, vn_ref, o_ref, *, nb):
    n = pl.program_id(1)
    blk = ATTN_BLOCK
    rows = Q_PER_KV * blk
    q = q_ref[0] * (HEAD_DIM ** -0.5)
    kwin = jnp.concatenate([kp_ref[0], kc_ref[0], kn_ref[0]], axis=0)
    vwin = jnp.concatenate([vp_ref[0], vc_ref[0], vn_ref[0]], axis=0)
    qi = lax.broadcasted_iota(jnp.int32, (rows, 3 * blk), 0) & (blk - 1)
    kj = lax.broadcasted_iota(jnp.int32, (rows, 3 * blk), 1)
    dist = kj - qi
    valid = ((dist >= blk - WINDOW) & (dist <= blk + WINDOW)
             & ((kj >= blk) | (n > 0)) & ((kj < 2 * blk) | (n < nb - 1)))
    rowg = lax.broadcasted_iota(jnp.int32, (rows, 1), 0) // blk
    outs = []
    for h in range(N_KV_HEADS):
        qs = jnp.concatenate(
            [q[:, (Q_PER_KV * h + g) * HEAD_DIM:(Q_PER_KV * h + g + 1) * HEAD_DIM]
             for g in range(Q_PER_KV)], axis=0)
        kh = kwin[:, h * HEAD_DIM:(h + 1) * HEAD_DIM]
        vh = vwin[:, h * HEAD_DIM:(h + 1) * HEAD_DIM]
        logits = _dot_nt(qs.astype(BF16), kh.astype(BF16))
        logits = jnp.where(valid, logits, -1e30)
        sk = jnp.zeros((rows, 1), F32)
        for g in range(Q_PER_KV):
            sk = jnp.where(rowg == g, sink_ref[Q_PER_KV * h + g], sk)
        m = jnp.maximum(jnp.max(logits, -1, keepdims=True), sk)
        e = jnp.exp(logits - m)
        den = jnp.sum(e, -1, keepdims=True) + jnp.exp(sk - m)
        p = e / den
        o = _dot(p.astype(BF16), vh.astype(BF16))
        for g in range(Q_PER_KV):
            outs.append(o[g * blk:(g + 1) * blk])
    o_ref[0] = jnp.concatenate(outs, axis=1)


def _attention(qkv3, sink):
    b, s, _ = qkv3.shape
    blk = ATTN_BLOCK
    nb = s // blk
    kcol = ATTN_WIDTH // LANES
    vcol = kcol + 1

    def spec(col, shift):
        def imap(bi, n):
            return (bi, jnp.clip(n + shift, 0, nb - 1), col)
        return pl.BlockSpec((1, blk, LANES), imap)

    return pl.pallas_call(
        functools.partial(_attn_kernel, nb=nb),
        grid=(b, nb),
        in_specs=[
            pl.BlockSpec(memory_space=pltpu.SMEM),
            pl.BlockSpec((1, blk, ATTN_WIDTH), lambda bi, n: (bi, n, 0)),
            spec(kcol, -1), spec(kcol, 0), spec(kcol, 1),
            spec(vcol, -1), spec(vcol, 0), spec(vcol, 1),
        ],
        out_specs=pl.BlockSpec((1, blk, ATTN_WIDTH), lambda bi, n: (bi, n, 0)),
        out_shape=jax.ShapeDtypeStruct((b, s, ATTN_WIDTH), F32),
        compiler_params=_cparams(("parallel", "parallel")),
        name="attn",
    )(sink, qkv3, qkv3, qkv3, qkv3, qkv3, qkv3, qkv3)


def _softplus(x):
    return jnp.maximum(x, 0.0) + jnp.log(1.0 + jnp.exp(-jnp.abs(x)))


def _sigmoid(x):
    return 1.0 / (1.0 + jnp.exp(-x))


def _prep_kernel(p_ref, hp_ref, hn_ref, mup_ref, mun_ref, wmix_ref, w0_ref, a0_ref, g2_ref,
                 kk_ref, ka_ref, bd_ref,
                 r_o, v_o, kk_o, g_o, lw0_o, lw1_o, kd0_o, kd1_o, b0_o, b1_o, *, tm, seq):
    i = pl.program_id(0)
    row = lax.broadcasted_iota(jnp.int32, (tm, 1), 0)
    seq_start = (i * tm) % seq == 0
    seq_end = ((i + 1) * tm) % seq == 0

    def shifted(c0, c1):
        p = p_ref[:, c0:c1]
        prev_row = jnp.where(seq_start, 0.0, hp_ref[SUBLANES - 1:SUBLANES, c0:c1])
        next_row = jnp.where(seq_end, 0.0, hn_ref[0:1, c0:c1])
        p_prev = jnp.where(row == 0, prev_row, pltpu.roll(p, 1, 0))
        p_next = jnp.where(row == tm - 1, next_row, pltpu.roll(p, tm - 1, 0))
        return p + mup_ref[:, c0:c1] * (p_prev - p) + mun_ref[:, c0:c1] * (p_next - p)

    w = RWKV_WIDTH
    r = shifted(0, w)
    k = shifted(w, 2 * w)
    v = shifted(2 * w, 3 * w)
    codes = shifted(3 * w, 3 * w + LANES)
    gd = shifted(3 * w + LANES, 3 * w + 2 * LANES)
    r_o[...] = r
    v_o[...] = v

    lane = lax.broadcasted_iota(jnp.int32, codes.shape, 1)
    codes = jnp.where(lane < 2 * DECAY_RANK, jnp.tanh(codes), codes)
    mm = _dot(codes, wmix_ref[...], HI)
    g_o[...] = _dot(_sigmoid(gd), g2_ref[...], HI)

    kkv = k * kk_ref[...]
    ss = _dot(kkv * kkv, bd_ref[...], HI)
    kkn = kkv * lax.rsqrt(jnp.maximum(ss, 1e-24))
    kk_o[...] = kkn

    ka = ka_ref[...]
    for d, (lw_o, kd_o, b_o) in enumerate(((lw0_o, kd0_o, b0_o), (lw1_o, kd1_o, b1_o))):
        w_log = -_softplus(-(w0_ref[d:d + 1, :] + mm[:, d * w:(d + 1) * w])) - 0.5
        lw_o[...] = -jnp.exp(w_log)
        a = _sigmoid(a0_ref[d:d + 1, :] + mm[:, (2 + d) * w:(3 + d) * w])
        kd_o[...] = k * (1.0 + (a - 1.0) * ka)
        b_o[...] = kkn * a


def _prep(pr, mup, mun, wmix, w0, a0, g2p, k_k, k_a, bd, seq, tm=256):
    t = pr.shape[0]
    nblk8 = t // SUBLANES
    per = tm // SUBLANES
    full = lambda a: pl.BlockSpec(a.shape, lambda i: (0,) * a.ndim)
    out = pl.BlockSpec((tm, RWKV_WIDTH), lambda i: (i, 0))
    return pl.pallas_call(
        functools.partial(_prep_kernel, tm=tm, seq=seq),
        grid=(t // tm,),
        in_specs=[
            pl.BlockSpec((tm, RWKV_COLS_PAD), lambda i: (i, 0)),
            pl.BlockSpec((SUBLANES, RWKV_COLS_PAD), lambda i: (jnp.maximum(i * per - 1, 0), 0)),
            pl.BlockSpec((SUBLANES, RWKV_COLS_PAD),
                         lambda i: (jnp.minimum((i + 1) * per, nblk8 - 1), 0)),
            full(mup), full(mun), full(wmix), full(w0), full(a0), full(g2p),
            full(k_k), full(k_a), full(bd),
        ],
        out_specs=[out] * 10,
        out_shape=[jax.ShapeDtypeStruct((t, RWKV_WIDTH), F32)] * 10,
        compiler_params=_cparams(("parallel",)),
        name="prep",
    )(pr, pr, pr, mup, mun, wmix, w0, a0, g2p, k_k, k_a, bd)


def _split3(x):
    hi = x.astype(BF16)
    r1 = x - hi.astype(F32)
    mid = r1.astype(BF16)
    lo = (r1 - mid.astype(F32)).astype(BF16)
    return hi, mid, lo


def _wkv_kernel(rf, vf, kkf, lwf, kdf, bf, rb, vb, kkb, lwb, kdb, bb, yf_ref, yb_ref, state):
    c = pl.program_id(1)
    n = WKV_CHUNK
    hd = HEAD_DIM
    nh = N_RWKV_HEADS
    def bmm(a, b, ca, cb):
        return lax.dot_general(a.astype(BF16), b.astype(BF16), (((ca,), (cb,)), ((0,), (0,))),
                               preferred_element_type=F32)

    @pl.when(c == 0)
    def _():
        state[...] = jnp.zeros_like(state)

    ti = lax.broadcasted_iota(jnp.int32, (n, n), 0)
    si = lax.broadcasted_iota(jnp.int32, (n, n), 1)
    dirs = ((rf, vf, kkf, lwf, kdf, bf, si <= ti, si < ti, n - 1),
            (rb, vb, kkb, lwb, kdb, bb, si >= ti, si > ti, 0))
    heads = lambda x: jnp.stack([x[:, j * hd:(j + 1) * hd] for j in range(nh)], 0)
    parts = []
    for r_ref, v_ref, kk_ref, lw_ref, kd_ref, b_ref, incl, strict, last in dirs:
        lw = lw_ref[0]
        tri = incl.astype(BF16)
        cum = sum(_dot(tri, part) for part in _split3(lw))
        e_neg = jnp.exp(-cum)
        e_last = jnp.exp(cum[last:last + 1, :] - cum)
        g_scale = jnp.exp(cum[last:last + 1, :])
        parts.append(dict(
            at=heads((-kk_ref[0] * jnp.exp(cum - lw)).astype(BF16)),
            rt=heads((r_ref[0] * jnp.exp(cum)).astype(BF16)),
            bt=heads((b_ref[0] * e_neg).astype(BF16)),
            kt=heads((kd_ref[0] * e_neg).astype(BF16)),
            bl=heads((b_ref[0] * e_last).astype(BF16)),
            kl=heads((kd_ref[0] * e_last).astype(BF16)),
            v=heads(v_ref[0].astype(BF16)),
            gs=heads(g_scale),
            incl=jnp.broadcast_to(incl[None], (nh, n, n)),
            strict=jnp.broadcast_to(strict[None], (nh, n, n))))
    cat = lambda key: jnp.concatenate([parts[0][key], parts[1][key]], 0)
    at, rt, bt, kt, bl, kl, v, gs = (cat(k) for k in ("at", "rt", "bt", "kt", "bl", "kl", "v", "gs"))
    incl, strict = cat("incl"), cat("strict")
    g0 = state[...]
    g0b = g0.astype(BF16)
    m1 = bmm(jnp.concatenate([at, rt], 1), jnp.concatenate([bt, kt], 1), 2, 2)
    a_ab = jnp.where(strict, m1[:, :n, :n], 0.0)
    a_ak = jnp.where(strict, m1[:, :n, n:], 0.0)
    a_rb = jnp.where(incl, m1[:, n:, :n], 0.0)
    a_rk = jnp.where(incl, m1[:, n:, n:], 0.0)
    tinv = jnp.where((ti == si)[None], 1.0, a_ab)
    pw = a_ab
    for _ in range(int(np.log2(n)) - 1):
        pw = bmm(pw, pw, 2, 1)
        tinv = tinv + bmm(tinv, pw, 2, 1)
    rhs = bmm(a_ak, v, 2, 1) + bmm(at, g0b, 2, 2)
    u = bmm(tinv, rhs, 2, 1)
    uv = jnp.concatenate([u.astype(BF16), v], 1)
    y = bmm(jnp.concatenate([a_rb, a_rk], 2), uv, 2, 1) + bmm(rt, g0b, 2, 2)
    state[...] = g0 * gs + bmm(uv, jnp.concatenate([bl, kl], 1), 1, 1)
    for d, y_ref in enumerate((yf_ref, yb_ref)):
        y_ref[0] = jnp.concatenate([y[d * nh + j] for j in range(nh)], axis=1)


def _wkv(r, v, kk, lw0, lw1, kd0, kd1, b0, b1, batch, seq):
    n = WKV_CHUNK
    nc = seq // n
    shp = (batch, seq, RWKV_WIDTH)
    arrs = [a.reshape(shp) for a in (r, v, kk, lw0, kd0, b0, r, v, kk, lw1, kd1, b1)]
    fwd = pl.BlockSpec((1, n, RWKV_WIDTH), lambda b, c: (b, c, 0))
    bwd = pl.BlockSpec((1, n, RWKV_WIDTH), lambda b, c: (b, nc - 1 - c, 0))
    yf, yb = pl.pallas_call(
        _wkv_kernel,
        grid=(batch, nc),
        in_specs=[fwd] * 6 + [bwd] * 6,
        out_specs=[fwd, bwd],
        out_shape=[jax.ShapeDtypeStruct(shp, F32)] * 2,
        scratch_shapes=[pltpu.VMEM((2 * N_RWKV_HEADS, HEAD_DIM, HEAD_DIM), F32)],
        compiler_params=_cparams(("parallel", "arbitrary")),
        name="wkv",
    )(*arrs)
    return yf.reshape(batch * seq, RWKV_WIDTH), yb.reshape(batch * seq, RWKV_WIDTH)


def _mix_kernel(x_ref, ya_ref, yf_ref, yb_ref, r_ref, v_ref, kd0_ref, kd1_ref, g_ref,
                lng_ref, lnb_ref, rk_ref, bd_ref, wout_ref, l1g_ref, l1b_ref, h_ref):
    bd = bd_ref[...]
    inv = 1.0 / HEAD_DIM
    y = yf_ref[...] + yb_ref[...]
    mu = _dot(y, bd, HI) * inv
    yc = y - mu
    var = _dot(yc * yc, bd, HI) * inv
    yn = yc * lax.rsqrt(var + RWKV_GN_EPS) * lng_ref[...] + lnb_ref[...]
    k_mean = 0.5 * (kd0_ref[...] + kd1_ref[...])
    v = v_ref[...]
    bonus = _dot(r_ref[...] * k_mean * rk_ref[...], bd, HI) * v
    yr = (yn + bonus) * g_ref[...]
    mix = (_dot(ya_ref[...].astype(BF16), wout_ref[:ATTN_WIDTH, :])
           + _dot(yr.astype(BF16), wout_ref[ATTN_WIDTH:, :]))
    h_ref[...] = _layer_norm(DEEPNORM_ALPHA * x_ref[...] + mix, l1g_ref[...], l1b_ref[...])


def _mix(x2, ya, yf, yb, r, v, kd0, kd1, g, lng, lnb, rk, bd, wout, l1g, l1b, tm=256):
    t, d = x2.shape
    full = lambda a: pl.BlockSpec(a.shape, lambda i: (0,) * a.ndim)
    half = pl.BlockSpec((tm, RWKV_WIDTH), lambda i: (i, 0))
    wide = pl.BlockSpec((tm, d), lambda i: (i, 0))
    return pl.pallas_call(
        _mix_kernel,
        grid=(t // tm,),
        in_specs=[wide] + [half] * 8 + [full(lng), full(lnb), full(rk), full(bd), full(wout),
                                        full(l1g), full(l1b)],
        out_specs=wide,
        out_shape=jax.ShapeDtypeStruct((t, d), F32),
        compiler_params=_cparams(("parallel",)),
        name="mix",
    )(x2, ya, yf, yb, r, v, kd0, kd1, g, lng, lnb, rk, bd, wout, l1g, l1b)


def _top_rows(sc, k, payload=None):
    n = sc.shape[0]
    iota = lax.broadcasted_iota(jnp.int32, sc.shape, 0)
    vals, picks = [], []
    for _ in range(k):
        m = jnp.max(sc, axis=0, keepdims=True)
        pos = jnp.min(jnp.where(sc == m, iota, n), axis=0, keepdims=True)
        hit = iota == pos
        vals.append(m)
        if payload is None:
            picks.append(pos)
        else:
            picks.append(jnp.max(jnp.where(hit, payload, -1), axis=0, keepdims=True))
        sc = jnp.where(hit, -jnp.inf, sc)
    return jnp.concatenate(vals, 0), jnp.concatenate(picks, 0)


def _route_kernel(h_ref, wq_ref, keys_ref, idx_ref, gate_ref, gate_tok_ref, q_scr, idx_scr):
    k = PEER_TOPK
    q_scr[...] = _dot(h_ref[...].astype(BF16), wq_ref[...])

    def head(hh, carry):
        tops = []
        for p in range(2):
            col = pl.multiple_of(hh * PEER_QDIM + p * PEER_HALF, PEER_HALF)
            qp = q_scr[:, pl.ds(col, PEER_HALF)]
            sct = _dot_nt(keys_ref[hh, p], qp, HI)
            tops.append(_top_rows(sct, k))
        (s1, i1), (s2, i2) = tops
        cand = jnp.concatenate([s1[i:i + 1] + s2[:k // (i + 1)] for i in range(k)], 0)
        cand_id = jnp.concatenate([i1[i:i + 1] * PEER_NKEYS + i2[:k // (i + 1)] for i in range(k)], 0)
        cs, ids = _top_rows(cand, k, cand_id)
        e = jnp.exp(cs - jnp.max(cs, axis=0, keepdims=True))
        row = pl.multiple_of(hh * k, k)
        gate_ref[pl.ds(row, k), :] = e / jnp.sum(e, axis=0, keepdims=True)
        idx_scr[pl.ds(row, k), :] = ids
        return carry

    lax.fori_loop(0, PEER_HEADS, head, 0)
    idx_ref[...] = idx_scr[...].T
    gate_tok_ref[...] = gate_ref[...].T


def _route(h1, wq, keys, tm=256):
    t, d = h1.shape
    return pl.pallas_call(
        _route_kernel,
        grid=(t // tm,),
        in_specs=[
            pl.BlockSpec((tm, d), lambda i: (i, 0)),
            pl.BlockSpec(wq.shape, lambda i: (0, 0)),
            pl.BlockSpec(keys.shape, lambda i: (0, 0, 0, 0)),
        ],
        out_specs=[
            pl.BlockSpec((tm, N_SEL), lambda i: (i, 0)),
            pl.BlockSpec((N_SEL, tm), lambda i: (0, i)),
            pl.BlockSpec((tm, N_SEL), lambda i: (i, 0)),
        ],
        out_shape=[
            jax.ShapeDtypeStruct((t, N_SEL), jnp.int32),
            jax.ShapeDtypeStruct((N_SEL, t), F32),
            jax.ShapeDtypeStruct((t, N_SEL), F32),
        ],
        scratch_shapes=[pltpu.VMEM((tm, PEER_HEADS * PEER_QDIM), F32),
                        pltpu.VMEM((N_SEL, tm), jnp.int32)],
        compiler_params=_cparams(("parallel",)),
        name="route",
    )(h1, wq, keys)


def _peer_kernel(idx_ref, gate_ref, h_ref, uv_hbm, l2g_ref, l2b_ref, o_ref, buf, f_scr, sem, *, tb):
    d = h_ref.shape[1]

    def issue(t, slot):
        for j in range(N_SEL):
            pltpu.make_async_copy(uv_hbm.at[pl.ds(idx_ref[t, j], 1)], buf.at[slot, pl.ds(j, 1)],
                                  sem.at[slot]).start()

    def wait_all(slot):
        pltpu.make_async_copy(uv_hbm.at[pl.ds(0, N_SEL)], buf.at[slot], sem.at[slot]).wait()

    issue(0, 0)
    lane = lax.broadcasted_iota(jnp.int32, (N_SEL, tb), 1)

    def body(t, carry):
        slot = t % 2

        @pl.when(t + 1 < tb)
        def _():
            issue(t + 1, 1 - slot)

        wait_all(slot)
        z = jnp.sum(buf[slot, :, 0:d] * h_ref[pl.ds(t, 1), :], axis=1, keepdims=True)
        gcol = jnp.sum(jnp.where(lane == t, gate_ref[...], 0.0), axis=1, keepdims=True)
        coef = gcol * (0.5 * z * (1.0 + lax.erf(z * (2.0 ** -0.5))))
        f_scr[pl.ds(t, 1), :] = jnp.sum(coef * buf[slot, :, d:2 * d], axis=0, keepdims=True)
        return carry

    lax.fori_loop(0, tb, body, 0)
    o_ref[...] = _layer_norm(DEEPNORM_ALPHA * h_ref[...] + f_scr[...], l2g_ref[...], l2b_ref[...])


def _peer(idx, gate_t, h1, uv, l2g, l2b, tok0, ntok, tb=PEER_TB):
    d = h1.shape[1]
    b0 = tok0 // tb
    return pl.pallas_call(
        functools.partial(_peer_kernel, tb=tb),
        grid=(ntok // tb,),
        in_specs=[
            pl.BlockSpec((tb, N_SEL), lambda i: (b0 + i, 0), memory_space=pltpu.SMEM),
            pl.BlockSpec((N_SEL, tb), lambda i: (0, b0 + i)),
            pl.BlockSpec((tb, d), lambda i: (b0 + i, 0)),
            pl.BlockSpec(memory_space=pl.ANY),
            pl.BlockSpec(l2g.shape, lambda i: (0, 0)),
            pl.BlockSpec(l2b.shape, lambda i: (0, 0)),
        ],
        out_specs=pl.BlockSpec((tb, d), lambda i: (i, 0)),
        out_shape=jax.ShapeDtypeStruct((ntok, d), F32),
        scratch_shapes=[pltpu.VMEM((2, N_SEL, 2 * d), F32),
                        pltpu.VMEM((tb, d), F32),
                        pltpu.SemaphoreType.DMA((2,))],
        compiler_params=_cparams(("arbitrary",)),
        name="peer",
    )(idx, gate_t, h1, uv, l2g, l2b)


def _sc_mesh():
    return plsc.VectorSubcoreMesh(core_axis_name="c", subcore_axis_name="s")


def _worker_id():
    return lax.axis_index("s") * 2 + lax.axis_index("c")


def _sc_z_body(u_hbm, idx_hbm, h_hbm, z_hbm, idx_v, h_v, rows_v, z_v, sem, *, tpw, d, tok0):
    ngrp = N_SEL // SC_GROUP
    nchunk = d // SC_LANES
    nsteps = tpw * ngrp
    base = _worker_id() * tpw

    def gather(tok_buf, g, buf):
        return pltpu.make_async_copy(u_hbm.at[idx_v.at[tok_buf, g]], rows_v.at[buf], sem.at[buf])

    pltpu.sync_copy(idx_hbm.at[tok0 + base], idx_v.at[0])
    gather(0, 0, 0).start()

    @pl.loop(0, nsteps)
    def _(q):
        t_loc = q // ngrp
        g = q % ngrp
        buf = q % 2
        t = base + t_loc

        @pl.when(g == 0)
        def _():
            pltpu.sync_copy(h_hbm.at[tok0 + t], h_v)

            @pl.loop(0, N_SEL)
            def _(r):
                z_v[pl.ds(pl.multiple_of(r * SC_LANES, SC_LANES), SC_LANES)] = jnp.zeros((SC_LANES,), F32)

        @pl.when(q + 1 < nsteps)
        def _():
            tn = (q + 1) // ngrp
            gn = (q + 1) % ngrp

            @pl.when(gn == 0)
            def _():
                pltpu.sync_copy(idx_hbm.at[tok0 + base + tn], idx_v.at[tn % 2])

            gather(tn % 2, gn, 1 - buf).start()

        gather(t_loc % 2, g, buf).wait()

        @pl.loop(0, SC_GROUP, step=4)
        def _(rb):
            @pl.loop(0, nchunk, step=SC_CHUNK_UNROLL)
            def _(c0):
                accs = [jnp.zeros((SC_LANES,), F32) for _ in range(4)]
                for cc in range(SC_CHUNK_UNROLL):
                    sl = pl.ds(pl.multiple_of((c0 + cc) * SC_LANES, SC_LANES), SC_LANES)
                    hc = h_v[sl]
                    for i in range(4):
                        accs[i] = accs[i] + rows_v[buf, rb + i, sl] * hc
                for i in range(4):
                    row = pl.multiple_of((g * SC_GROUP + rb + i) * SC_LANES, SC_LANES)
                    plsc.addupdate(z_v.at[pl.ds(row, SC_LANES)], accs[i])

        @pl.when(g == ngrp - 1)
        def _():
            pltpu.sync_copy(z_v, z_hbm.at[t])


def _sc_z(u, idx, h, tok0, ntok):
    t, d = h.shape
    tpw = ntok // SC_WORKERS
    idx4 = idx.reshape(t, N_SEL // SC_GROUP, SC_GROUP)
    body = functools.partial(_sc_z_body, tpw=tpw, d=d, tok0=tok0)
    return pl.kernel(
        body,
        out_type=jax.ShapeDtypeStruct((ntok, N_SEL * SC_LANES), F32),
        mesh=_sc_mesh(),
        scratch_types=[
            pltpu.VMEM((2, N_SEL // SC_GROUP, SC_GROUP), jnp.int32),
            pltpu.VMEM((d,), F32),
            pltpu.VMEM((2, SC_GROUP, d), F32),
            pltpu.VMEM((N_SEL * SC_LANES,), F32),
            pltpu.SemaphoreType.DMA((2,)),
        ],
        name="sc_z",
    )(u, idx4, h)


def _sc_out_body(v_hbm, idx_hbm, coef_hbm, f_hbm, idx_v, coef_v, rows_v, out_v, sem, *, tpw, d, tok0):
    ngrp = N_SEL // SC_GROUP
    nchunk = d // SC_LANES
    nsteps = tpw * ngrp
    base = _worker_id() * tpw

    def gather(tok_buf, g, buf):
        return pltpu.make_async_copy(v_hbm.at[idx_v.at[tok_buf, g]], rows_v.at[buf], sem.at[buf])

    pltpu.sync_copy(idx_hbm.at[tok0 + base], idx_v.at[0])
    gather(0, 0, 0).start()

    @pl.loop(0, nsteps)
    def _(q):
        t_loc = q // ngrp
        g = q % ngrp
        buf = q % 2
        t = base + t_loc

        @pl.when(g == 0)
        def _():
            pltpu.sync_copy(coef_hbm.at[t], coef_v)

        @pl.when(q + 1 < nsteps)
        def _():
            tn = (q + 1) // ngrp
            gn = (q + 1) % ngrp

            @pl.when(gn == 0)
            def _():
                pltpu.sync_copy(idx_hbm.at[tok0 + base + tn], idx_v.at[tn % 2])

            gather(tn % 2, gn, 1 - buf).start()

        gather(t_loc % 2, g, buf).wait()

        @pl.loop(0, nchunk, step=SC_OUT_CHUNKS)
        def _(c0):
            sls = [pl.ds(pl.multiple_of((c0 + cc) * SC_LANES, SC_LANES), SC_LANES)
                   for cc in range(SC_OUT_CHUNKS)]
            accs = [None] * SC_OUT_CHUNKS
            for r in range(SC_GROUP):
                cf = coef_v[pl.ds(pl.multiple_of((g * SC_GROUP + r) * SC_LANES, SC_LANES), SC_LANES)]
                for cc in range(SC_OUT_CHUNKS):
                    p = rows_v[buf, r, sls[cc]] * cf
                    accs[cc] = p if accs[cc] is None else accs[cc] + p
            for cc in range(SC_OUT_CHUNKS):
                @pl.when(g == 0)
                def _():
                    out_v[sls[cc]] = accs[cc]

                @pl.when(g != 0)
                def _():
                    plsc.addupdate(out_v.at[sls[cc]], accs[cc])

        @pl.when(g == ngrp - 1)
        def _():
            pltpu.sync_copy(out_v, f_hbm.at[t])


def _sc_out(v, idx, coef16, tok0):
    t = idx.shape[0]
    ntok = coef16.shape[0]
    d = v.shape[1]
    tpw = ntok // SC_WORKERS
    idx4 = idx.reshape(t, N_SEL // SC_GROUP, SC_GROUP)
    body = functools.partial(_sc_out_body, tpw=tpw, d=d, tok0=tok0)
    return pl.kernel(
        body,
        out_type=jax.ShapeDtypeStruct((ntok, d), F32),
        mesh=_sc_mesh(),
        scratch_types=[
            pltpu.VMEM((2, N_SEL // SC_GROUP, SC_GROUP), jnp.int32),
            pltpu.VMEM((N_SEL * SC_LANES,), F32),
            pltpu.VMEM((2, SC_GROUP, d), F32),
            pltpu.VMEM((d,), F32),
            pltpu.SemaphoreType.DMA((2,)),
        ],
        name="sc_out",
    )(v, idx4, coef16)


def _coef_kernel(zp_ref, gate_ref, sel_ref, selt_ref, after_ref, o_ref):
    z = sum(_dot(part, sel_ref[...]) for part in _split3(zp_ref[...]))
    coef = gate_ref[...] * (0.5 * z * (1.0 + lax.erf(z * (2.0 ** -0.5))))
    o_ref[...] = sum(_dot(part, selt_ref[...]) for part in _split3(coef))


def _coef(zp, gate_tok, tok0, after, tm=256):
    ntok, wide = zp.shape
    b0 = tok0 // tm
    grp = np.arange(wide) // SC_LANES
    sel = jnp.asarray(grp[:, None] == np.arange(N_SEL)[None, :], BF16)
    return pl.pallas_call(
        _coef_kernel,
        grid=(ntok // tm,),
        in_specs=[
            pl.BlockSpec((tm, wide), lambda i: (i, 0)),
            pl.BlockSpec((tm, N_SEL), lambda i: (b0 + i, 0)),
            pl.BlockSpec(sel.shape, lambda i: (0, 0)),
            pl.BlockSpec(sel.shape[::-1], lambda i: (0, 0)),
            pl.BlockSpec((SUBLANES, LANES), lambda i: (0, 0)),
        ],
        out_specs=pl.BlockSpec((tm, wide), lambda i: (i, 0)),
        out_shape=jax.ShapeDtypeStruct((ntok, wide), F32),
        compiler_params=_cparams(("parallel",)),
        name="coef",
    )(zp, gate_tok, sel, sel.T, after)


def _ln_out_kernel(h_ref, f_ref, g_ref, b_ref, o_ref):
    o_ref[...] = _layer_norm(DEEPNORM_ALPHA * h_ref[...] + f_ref[...], g_ref[...], b_ref[...])


def _ln_out(h1, f, l2g, l2b, tok0, tm=256):
    ntok, d = f.shape
    b0 = tok0 // tm
    return pl.pallas_call(
        _ln_out_kernel,
        grid=(ntok // tm,),
        in_specs=[
            pl.BlockSpec((tm, d), lambda i: (b0 + i, 0)),
            pl.BlockSpec((tm, d), lambda i: (i, 0)),
            pl.BlockSpec(l2g.shape, lambda i: (0, 0)),
            pl.BlockSpec(l2b.shape, lambda i: (0, 0)),
        ],
        out_specs=pl.BlockSpec((tm, d), lambda i: (i, 0)),
        out_shape=jax.ShapeDtypeStruct((ntok, d), F32),
        compiler_params=_cparams(("parallel",)),
        name="ln_out",
    )(h1, f, l2g, l2b)


def _rope_tables(seq):
    pos = jnp.arange(seq, dtype=F32)
    inv_freq = ROPE_THETA ** (-jnp.arange(0, HEAD_DIM, 2, dtype=F32) / HEAD_DIM)
    ang = pos[:, None] * inv_freq[None, :]
    ang = jnp.concatenate([ang, ang], -1)
    sign = jnp.concatenate([-jnp.ones((HEAD_DIM // 2,), F32), jnp.ones((HEAD_DIM // 2,), F32)])
    reps = LANES // HEAD_DIM
    return jnp.tile(jnp.cos(ang), (1, reps)), jnp.tile(jnp.sin(ang) * sign, (1, reps))


def _layer(h2d, batch, seq, w_in, mu_prev, mu_next, w0, w2, a0, a2, g2, k_k, k_a, r_k, lnx_g, lnx_b,
           sink, w_out, ln1_g, ln1_b, peer_wq, peer_keys, peer_u, peer_v, ln2_g, ln2_b):
    w = RWKV_WIDTH
    row = lambda a: a.reshape(1, -1).astype(F32)
    w_in_p = jnp.pad(w_in, ((0, 0), (0, RWKV_COLS_PAD - RWKV_COLS))).astype(BF16)
    mup = jnp.pad(mu_prev, (0, RWKV_COLS_PAD - RWKV_COLS)).reshape(1, -1)
    mun = jnp.pad(mu_next, (0, RWKV_COLS_PAD - RWKV_COLS)).reshape(1, -1)
    wmix = jnp.zeros((LANES, 4 * w), F32)
    for d in range(2):
        wmix = wmix.at[d * DECAY_RANK:(d + 1) * DECAY_RANK, d * w:(d + 1) * w].set(w2[d])
        r0 = 2 * DECAY_RANK + d * ICLR_RANK
        wmix = wmix.at[r0:r0 + ICLR_RANK, (2 + d) * w:(3 + d) * w].set(a2[d])
    g2p = jnp.pad(g2, ((0, LANES - GATE_RANK), (0, 0)))
    head_of = np.arange(w) // HEAD_DIM
    bd = jnp.asarray(head_of[:, None] == head_of[None, :], F32)
    cos_t, sin_t = _rope_tables(seq)

    qkv, pr = _proj(h2d, w_in_p, cos_t, sin_t, seq)
    y_attn = _attention(qkv.reshape(batch, seq, ATTN_COLS), sink.astype(F32))
    r, v, kk, g, lw0, lw1, kd0, kd1, b0, b1 = _prep(
        pr, mup, mun, wmix, w0.astype(F32), a0.astype(F32), g2p, row(k_k), row(k_a), bd, seq)
    yf, yb = _wkv(r, v, kk, lw0, lw1, kd0, kd1, b0, b1, batch, seq)
    h1 = _mix(h2d, y_attn.reshape(batch * seq, ATTN_WIDTH), yf, yb, r, v, kd0, kd1, g,
              row(lnx_g), row(lnx_b), row(r_k), bd, w_out.astype(BF16), row(ln1_g), row(ln1_b))
    idx, gate_t, gate_tok = _route(h1, peer_wq.astype(BF16), peer_keys.astype(F32))
    uv = jnp.concatenate([peer_u, peer_v], axis=1)
    l2g, l2b = row(ln2_g), row(ln2_b)

    t = batch * seq
    n_sc, n_a, n_b = (t * share // 64 for share in PEER_SPLIT_64THS)
    assert n_sc + n_a + n_b == t and n_sc % (SC_WORKERS * SUBLANES) == 0
    assert n_a % PEER_TB == 0 and n_b % PEER_TB == 0 and n_sc % PEER_TB == 0
    zp = _sc_z(peer_u.astype(F32), idx, h1, 0, n_sc)
    out_a = _peer(idx, gate_t, h1, uv, l2g, l2b, n_sc, n_a)
    f_sc = _sc_out(peer_v.astype(F32), idx, _coef(zp, gate_tok, 0, out_a), 0)
    out_b = _peer(idx, gate_t, h1, uv, l2g, l2b, n_sc + n_a, n_b)
    out_sc = _ln_out(h1, f_sc, l2g, l2b, 0)
    return jnp.concatenate([out_sc, out_a, out_b], axis=0)


def kernel(x, w_in, mu_prev, mu_next, w0, w2, a0, a2, g2, k_k, k_a, r_k, lnx_g, lnx_b, sink, w_out,
           ln1_g, ln1_b, peer_wq, peer_keys, peer_u, peer_v, ln2_g, ln2_b):
    batch, seq, d = x.shape
    h = x.reshape(batch * seq, d)
    for l in range(DEPTH):
        h = _layer(h, batch, seq, w_in[l], mu_prev[l], mu_next[l], w0[l], w2[l], a0[l], a2[l], g2[l],
                   k_k[l], k_a[l], r_k[l], lnx_g[l], lnx_b[l], sink[l], w_out[l], ln1_g[l], ln1_b[l],
                   peer_wq[l], peer_keys[l], peer_u[l], peer_v[l], ln2_g[l], ln2_b[l])
    return h.reshape(batch, seq, d)
```

```python
import functools

import numpy as np
import jax
import jax.numpy as jnp
from jax import lax
from jax.experimental import pallas as pl
from jax.experimental.pallas import tpu as pltpu
from jax.experimental.pallas import tpu_sc as plsc

F32 = jnp.float32
BF16 = jnp.bfloat16
HI = lax.Precision.HIGHEST

HEAD_DIM = 64
N_Q_HEADS = 8
N_KV_HEADS = 2
Q_PER_KV = N_Q_HEADS // N_KV_HEADS
ATTN_WIDTH = N_Q_HEADS * HEAD_DIM
ATTN_KV_WIDTH = N_KV_HEADS * HEAD_DIM
ATTN_COLS = ATTN_WIDTH + 2 * ATTN_KV_WIDTH
WINDOW = 128
ATTN_BLOCK = 128
ROPE_THETA = 10000.0
N_RWKV_HEADS = 8
RWKV_WIDTH = N_RWKV_HEADS * HEAD_DIM
DECAY_RANK = 32
ICLR_RANK = 32
GATE_RANK = 96
RWKV_COLS = 3 * RWKV_WIDTH + 2 * DECAY_RANK + 2 * ICLR_RANK + GATE_RANK
RWKV_COLS_PAD = 1792
RWKV_GN_EPS = 64e-5
PEER_HEADS = 8
PEER_NKEYS = 128
PEER_QDIM = 256
PEER_HALF = PEER_QDIM // 2
PEER_TOPK = 16
N_SEL = PEER_HEADS * PEER_TOPK
LN_EPS = 1e-5
DEPTH = 1
DEEPNORM_ALPHA = (2.0 * DEPTH) ** 0.25

LANES = 128
SUBLANES = 8
WKV_CHUNK = 64
VMEM_LIMIT = 48 * 1024 * 1024
PEER_TB = 128
SC_LANES = 16
SC_WORKERS = 32
SC_GROUP = 32
SC_CHUNK_UNROLL = 8
SC_OUT_CHUNKS = 4
PEER_SC_SEQ_SHARE_8THS = 5
PEER_TC_FIRST_CALL_32NDS = 13


def _cparams(sem):
    return pltpu.CompilerParams(dimension_semantics=sem, vmem_limit_bytes=VMEM_LIMIT)


def _dot(a, b, precision=None):
    return jnp.dot(a, b, preferred_element_type=F32, precision=precision)


def _dot_nt(a, b, precision=None):
    return lax.dot_general(a, b, (((1,), (1,)), ((), ())), preferred_element_type=F32,
                           precision=precision)


def _dot_tn(a, b, precision=None):
    return lax.dot_general(a, b, (((0,), (0,)), ((), ())), preferred_element_type=F32,
                           precision=precision)


def _layer_norm(z, g, b):
    mu = jnp.mean(z, -1, keepdims=True)
    zc = z - mu
    var = jnp.mean(zc * zc, -1, keepdims=True)
    return zc * lax.rsqrt(var + LN_EPS) * g + b


def _proj_kernel(x_ref, w_ref, cos_ref, sin_ref, qkv_ref, pr_ref):
    xb = x_ref[...].astype(BF16)
    cos = cos_ref[...]
    sin = sin_ref[...]
    lane = lax.broadcasted_iota(jnp.int32, cos.shape, 1)
    first_half = (lane & (HEAD_DIM // 2)) == 0

    def rope(t):
        rot = jnp.where(first_half, pltpu.roll(t, LANES - HEAD_DIM // 2, 1),
                        pltpu.roll(t, HEAD_DIM // 2, 1))
        return t * cos + rot * sin

    for c in range(0, ATTN_COLS, 2 * LANES):
        acc = _dot(xb, w_ref[:, c:c + 2 * LANES])
        for half in range(2):
            col = c + half * LANES
            t = acc[:, half * LANES:(half + 1) * LANES]
            if col < ATTN_WIDTH + ATTN_KV_WIDTH:
                t = rope(t)
            qkv_ref[:, col:col + LANES] = t
    for c in range(0, RWKV_COLS_PAD, 2 * LANES):
        pr_ref[:, c:c + 2 * LANES] = _dot(xb, w_ref[:, ATTN_COLS + c:ATTN_COLS + c + 2 * LANES])


def _proj(x2, w_in_p, cos_t, sin_t, seq, tm=512):
    t, d = x2.shape
    n_pos = seq // tm
    return pl.pallas_call(
        _proj_kernel,
        grid=(t // tm,),
        in_specs=[
            pl.BlockSpec((tm, d), lambda i: (i, 0)),
            pl.BlockSpec(w_in_p.shape, lambda i: (0, 0)),
            pl.BlockSpec((tm, LANES), lambda i: (i % n_pos, 0)),
            pl.BlockSpec((tm, LANES), lambda i: (i % n_pos, 0)),
        ],
        out_specs=[
            pl.BlockSpec((tm, ATTN_COLS), lambda i: (i, 0)),
            pl.BlockSpec((tm, RWKV_COLS_PAD), lambda i: (i, 0)),
        ],
        out_shape=[
            jax.ShapeDtypeStruct((t, ATTN_COLS), F32),
            jax.ShapeDtypeStruct((t, RWKV_COLS_PAD), F32),
        ],
        compiler_params=_cparams(("parallel",)),
        name="proj",
    )(x2, w_in_p, cos_t, sin_t)


def _attn_kernel(sink_ref, q_ref, kp_ref, kc_ref, kn_ref, vp_ref, vc_ref, vn_ref, o_ref, *, nb):
    n = pl.program_id(1)
    blk = ATTN_BLOCK
    rows = Q_PER_KV * blk
    q = q_ref[0] * (HEAD_DIM ** -0.5)
    kwin = jnp.concatenate([kp_ref[0], kc_ref[0], kn_ref[0]], axis=0)
    vwin = jnp.concatenate([vp_ref[0], vc_ref[0], vn_ref[0]], axis=0)
    qi = lax.broadcasted_iota(jnp.int32, (rows, 3 * blk), 0) & (blk - 1)
    kj = lax.broadcasted_iota(jnp.int32, (rows, 3 * blk), 1)
    dist = kj - qi
    valid = ((dist >= blk - WINDOW) & (dist <= blk + WINDOW)
             & ((kj >= blk) | (n > 0)) & ((kj < 2 * blk) | (n < nb - 1)))
    rowg = lax.broadcasted_iota(jnp.int32, (rows, 1), 0) // blk
    outs = []
    for h in range(N_KV_HEADS):
        qs = jnp.concatenate(
            [q[:, (Q_PER_KV * h + g) * HEAD_DIM:(Q_PER_KV * h + g + 1) * HEAD_DIM]
             for g in range(Q_PER_KV)], axis=0)
        kh = kwin[:, h * HEAD_DIM:(h + 1) * HEAD_DIM]
        vh = vwin[:, h * HEAD_DIM:(h + 1) * HEAD_DIM]
        logits = _dot_nt(qs.astype(BF16), kh.astype(BF16))
        logits = jnp.where(valid, logits, -1e30)
        sk = jnp.zeros((rows, 1), F32)
        for g in range(Q_PER_KV):
            sk = jnp.where(rowg == g, sink_ref[Q_PER_KV * h + g], sk)
        m = jnp.maximum(jnp.max(logits, -1, keepdims=True), sk)
        e = jnp.exp(logits - m)
        den = jnp.sum(e, -1, keepdims=True) + jnp.exp(sk - m)
        p = e / den
        o = _dot(p.astype(BF16), vh.astype(BF16))
        for g in range(Q_PER_KV):
            outs.append(o[g * blk:(g + 1) * blk])
    o_ref[0] = jnp.concatenate(outs, axis=1)


def _attention(qkv3, sink):
    b, s, _ = qkv3.shape
    blk = ATTN_BLOCK
    nb = s // blk
    kcol = ATTN_WIDTH // LANES
    vcol = kcol + 1

    def spec(col, shift):
        def imap(bi, n):
            return (bi, jnp.clip(n + shift, 0, nb - 1), col)
        return pl.BlockSpec((1, blk, LANES), imap)

    return pl.pallas_call(
        functools.partial(_attn_kernel, nb=nb),
        grid=(b, nb),
        in_specs=[
            pl.BlockSpec(memory_space=pltpu.SMEM),
            pl.BlockSpec((1, blk, ATTN_WIDTH), lambda bi, n: (bi, n, 0)),
            spec(kcol, -1), spec(kcol, 0), spec(kcol, 1),
            spec(vcol, -1), spec(vcol, 0), spec(vcol, 1),
        ],
        out_specs=pl.BlockSpec((1, blk, ATTN_WIDTH), lambda bi, n: (bi, n, 0)),
        out_shape=jax.ShapeDtypeStruct((b, s, ATTN_WIDTH), F32),
        compiler_params=_cparams(("parallel", "parallel")),
        name="attn",
    )(sink, qkv3, qkv3, qkv3, qkv3, qkv3, qkv3, qkv3)


def _softplus(x):
    return jnp.maximum(x, 0.0) + jnp.log(1.0 + jnp.exp(-jnp.abs(x)))


def _sigmoid(x):
    return 1.0 / (1.0 + jnp.exp(-x))


def _prep_kernel(p_ref, hp_ref, hn_ref, mup_ref, mun_ref, wmix_ref, w0_ref, a0_ref, g2_ref,
                 kk_ref, ka_ref, bd_ref,
                 r_o, v_o, kk_o, g_o, lw0_o, lw1_o, kd0_o, kd1_o, b0_o, b1_o, *, tm, seq):
    i = pl.program_id(0)
    row = lax.broadcasted_iota(jnp.int32, (tm, 1), 0)
    seq_start = (i * tm) % seq == 0
    seq_end = ((i + 1) * tm) % seq == 0

    def shifted(c0, c1):
        p = p_ref[:, c0:c1]
        prev_row = jnp.where(seq_start, 0.0, hp_ref[SUBLANES - 1:SUBLANES, c0:c1])
        next_row = jnp.where(seq_end, 0.0, hn_ref[0:1, c0:c1])
        p_prev = jnp.where(row == 0, prev_row, pltpu.roll(p, 1, 0))
        p_next = jnp.where(row == tm - 1, next_row, pltpu.roll(p, tm - 1, 0))
        return p + mup_ref[:, c0:c1] * (p_prev - p) + mun_ref[:, c0:c1] * (p_next - p)

    w = RWKV_WIDTH
    r = shifted(0, w)
    k = shifted(w, 2 * w)
    v = shifted(2 * w, 3 * w)
    codes = shifted(3 * w, 3 * w + LANES)
    gd = shifted(3 * w + LANES, 3 * w + 2 * LANES)
    r_o[...] = r
    v_o[...] = v

    lane = lax.broadcasted_iota(jnp.int32, codes.shape, 1)
    codes = jnp.where(lane < 2 * DECAY_RANK, jnp.tanh(codes), codes)
    mm = _dot(codes, wmix_ref[...], HI)
    g_o[...] = _dot(_sigmoid(gd), g2_ref[...], HI)

    kkv = k * kk_ref[...]
    ss = _dot(kkv * kkv, bd_ref[...], HI)
    kkn = kkv * lax.rsqrt(jnp.maximum(ss, 1e-24))
    kk_o[...] = kkn

    ka = ka_ref[...]
    for d, (lw_o, kd_o, b_o) in enumerate(((lw0_o, kd0_o, b0_o), (lw1_o, kd1_o, b1_o))):
        w_log = -_softplus(-(w0_ref[d:d + 1, :] + mm[:, d * w:(d + 1) * w])) - 0.5
        lw_o[...] = -jnp.exp(w_log)
        a = _sigmoid(a0_ref[d:d + 1, :] + mm[:, (2 + d) * w:(3 + d) * w])
        kd_o[...] = k * (1.0 + (a - 1.0) * ka)
        b_o[...] = kkn * a


def _prep(pr, mup, mun, wmix, w0, a0, g2p, k_k, k_a, bd, seq, tm=256):
    t = pr.shape[0]
    nblk8 = t // SUBLANES
    per = tm // SUBLANES
    full = lambda a: pl.BlockSpec(a.shape, lambda i: (0,) * a.ndim)
    out = pl.BlockSpec((tm, RWKV_WIDTH), lambda i: (i, 0))
    return pl.pallas_call(
        functools.partial(_prep_kernel, tm=tm, seq=seq),
        grid=(t // tm,),
        in_specs=[
            pl.BlockSpec((tm, RWKV_COLS_PAD), lambda i: (i, 0)),
            pl.BlockSpec((SUBLANES, RWKV_COLS_PAD), lambda i: (jnp.maximum(i * per - 1, 0), 0)),
            pl.BlockSpec((SUBLANES, RWKV_COLS_PAD),
                         lambda i: (jnp.minimum((i + 1) * per, nblk8 - 1), 0)),
            full(mup), full(mun), full(wmix), full(w0), full(a0), full(g2p),
            full(k_k), full(k_a), full(bd),
        ],
        out_specs=[out] * 10,
        out_shape=[jax.ShapeDtypeStruct((t, RWKV_WIDTH), F32)] * 10,
        compiler_params=_cparams(("parallel",)),
        name="prep",
    )(pr, pr, pr, mup, mun, wmix, w0, a0, g2p, k_k, k_a, bd)


def _split3(x):
    hi = x.astype(BF16)
    r1 = x - hi.astype(F32)
    mid = r1.astype(BF16)
    lo = (r1 - mid.astype(F32)).astype(BF16)
    return hi, mid, lo


def _wkv_kernel(rf, vf, kkf, lwf, kdf, bf, rb, vb, kkb, lwb, kdb, bb, yf_ref, yb_ref, state):
    c = pl.program_id(1)
    n = WKV_CHUNK
    hd = HEAD_DIM
    nh = N_RWKV_HEADS
    def bmm(a, b, ca, cb):
        return lax.dot_general(a.astype(BF16), b.astype(BF16), (((ca,), (cb,)), ((0,), (0,))),
                               preferred_element_type=F32)

    @pl.when(c == 0)
    def _():
        state[...] = jnp.zeros_like(state)

    ti = lax.broadcasted_iota(jnp.int32, (n, n), 0)
    si = lax.broadcasted_iota(jnp.int32, (n, n), 1)
    dirs = ((rf, vf, kkf, lwf, kdf, bf, si <= ti, si < ti, n - 1),
            (rb, vb, kkb, lwb, kdb, bb, si >= ti, si > ti, 0))
    heads = lambda x: jnp.stack([x[:, j * hd:(j + 1) * hd] for j in range(nh)], 0)
    parts = []
    for r_ref, v_ref, kk_ref, lw_ref, kd_ref, b_ref, incl, strict, last in dirs:
        lw = lw_ref[0]
        tri = incl.astype(BF16)
        cum = sum(_dot(tri, part) for part in _split3(lw))
        e_neg = jnp.exp(-cum)
        e_last = jnp.exp(cum[last:last + 1, :] - cum)
        g_scale = jnp.exp(cum[last:last + 1, :])
        parts.append(dict(
            at=heads((-kk_ref[0] * jnp.exp(cum - lw)).astype(BF16)),
            rt=heads((r_ref[0] * jnp.exp(cum)).astype(BF16)),
            bt=heads((b_ref[0] * e_neg).astype(BF16)),
            kt=heads((kd_ref[0] * e_neg).astype(BF16)),
            bl=heads((b_ref[0] * e_last).astype(BF16)),
            kl=heads((kd_ref[0] * e_last).astype(BF16)),
            v=heads(v_ref[0].astype(BF16)),
            gs=heads(g_scale),
            incl=jnp.broadcast_to(incl[None], (nh, n, n)),
            strict=jnp.broadcast_to(strict[None], (nh, n, n))))
    cat = lambda key: jnp.concatenate([parts[0][key], parts[1][key]], 0)
    at, rt, bt, kt, bl, kl, v, gs = (cat(k) for k in ("at", "rt", "bt", "kt", "bl", "kl", "v", "gs"))
    incl, strict = cat("incl"), cat("strict")
    g0 = state[...]
    g0b = g0.astype(BF16)
    m1 = bmm(jnp.concatenate([at, rt], 1), jnp.concatenate([bt, kt], 1), 2, 2)
    a_ab = jnp.where(strict, m1[:, :n, :n], 0.0)
    a_ak = jnp.where(strict, m1[:, :n, n:], 0.0)
    a_rb = jnp.where(incl, m1[:, n:, :n], 0.0)
    a_rk = jnp.where(incl, m1[:, n:, n:], 0.0)
    tinv = jnp.where((ti == si)[None], 1.0, a_ab)
    pw = a_ab
    for _ in range(int(np.log2(n)) - 1):
        pw = bmm(pw, pw, 2, 1)
        tinv = tinv + bmm(tinv, pw, 2, 1)
    rhs = bmm(a_ak, v, 2, 1) + bmm(at, g0b, 2, 2)
    u = bmm(tinv, rhs, 2, 1)
    uv = jnp.concatenate([u.astype(BF16), v], 1)
    y = bmm(jnp.concatenate([a_rb, a_rk], 2), uv, 2, 1) + bmm(rt, g0b, 2, 2)
    state[...] = g0 * gs + bmm(uv, jnp.concatenate([bl, kl], 1), 1, 1)
    for d, y_ref in enumerate((yf_ref, yb_ref)):
        y_ref[0] = jnp.concatenate([y[d * nh + j] for j in range(nh)], axis=1)


def _wkv(r, v, kk, lw0, lw1, kd0, kd1, b0, b1, batch, seq):
    n = WKV_CHUNK
    nc = seq // n
    shp = (batch, seq, RWKV_WIDTH)
    arrs = [a.reshape(shp) for a in (r, v, kk, lw0, kd0, b0, r, v, kk, lw1, kd1, b1)]
    fwd = pl.BlockSpec((1, n, RWKV_WIDTH), lambda b, c: (b, c, 0))
    bwd = pl.BlockSpec((1, n, RWKV_WIDTH), lambda b, c: (b, nc - 1 - c, 0))
    yf, yb = pl.pallas_call(
        _wkv_kernel,
        grid=(batch, nc),
        in_specs=[fwd] * 6 + [bwd] * 6,
        out_specs=[fwd, bwd],
        out_shape=[jax.ShapeDtypeStruct(shp, F32)] * 2,
        scratch_shapes=[pltpu.VMEM((2 * N_RWKV_HEADS, HEAD_DIM, HEAD_DIM), F32)],
        compiler_params=_cparams(("parallel", "arbitrary")),
        name="wkv",
    )(*arrs)
    return yf.reshape(batch * seq, RWKV_WIDTH), yb.reshape(batch * seq, RWKV_WIDTH)


def _mix_kernel(x_ref, ya_ref, yf_ref, yb_ref, r_ref, v_ref, kd0_ref, kd1_ref, g_ref,
                lng_ref, lnb_ref, rk_ref, bd_ref, wout_ref, l1g_ref, l1b_ref, h_ref):
    bd = bd_ref[...]
    inv = 1.0 / HEAD_DIM
    y = yf_ref[...] + yb_ref[...]
    mu = _dot(y, bd, HI) * inv
    yc = y - mu
    var = _dot(yc * yc, bd, HI) * inv
    yn = yc * lax.rsqrt(var + RWKV_GN_EPS) * lng_ref[...] + lnb_ref[...]
    k_mean = 0.5 * (kd0_ref[...] + kd1_ref[...])
    v = v_ref[...]
    bonus = _dot(r_ref[...] * k_mean * rk_ref[...], bd, HI) * v
    yr = (yn + bonus) * g_ref[...]
    mix = (_dot(ya_ref[...].astype(BF16), wout_ref[:ATTN_WIDTH, :])
           + _dot(yr.astype(BF16), wout_ref[ATTN_WIDTH:, :]))
    h_ref[...] = _layer_norm(DEEPNORM_ALPHA * x_ref[...] + mix, l1g_ref[...], l1b_ref[...])


def _mix(x2, ya, yf, yb, r, v, kd0, kd1, g, lng, lnb, rk, bd, wout, l1g, l1b, tm=256):
    t, d = x2.shape
    full = lambda a: pl.BlockSpec(a.shape, lambda i: (0,) * a.ndim)
    half = pl.BlockSpec((tm, RWKV_WIDTH), lambda i: (i, 0))
    wide = pl.BlockSpec((tm, d), lambda i: (i, 0))
    return pl.pallas_call(
        _mix_kernel,
        grid=(t // tm,),
        in_specs=[wide] + [half] * 8 + [full(lng), full(lnb), full(rk), full(bd), full(wout),
                                        full(l1g), full(l1b)],
        out_specs=wide,
        out_shape=jax.ShapeDtypeStruct((t, d), F32),
        compiler_params=_cparams(("parallel",)),
        name="mix",
    )(x2, ya, yf, yb, r, v, kd0, kd1, g, lng, lnb, rk, bd, wout, l1g, l1b)


def _top_rows(sc, k, payload=None):
    n = sc.shape[0]
    iota = lax.broadcasted_iota(jnp.int32, sc.shape, 0)
    vals, picks = [], []
    for _ in range(k):
        m = jnp.max(sc, axis=0, keepdims=True)
        pos = jnp.min(jnp.where(sc == m, iota, n), axis=0, keepdims=True)
        hit = iota == pos
        vals.append(m)
        if payload is None:
            picks.append(pos)
        else:
            picks.append(jnp.max(jnp.where(hit, payload, -1), axis=0, keepdims=True))
        sc = jnp.where(hit, -jnp.inf, sc)
    return jnp.concatenate(vals, 0), jnp.concatenate(picks, 0)


def _route_kernel(h_ref, wq_ref, keys_ref, idx_ref, gate_ref, gate_tok_ref, q_scr, idx_scr):
    k = PEER_TOPK
    q_scr[...] = _dot(h_ref[...].astype(BF16), wq_ref[...])

    def head(hh, carry):
        tops = []
        for p in range(2):
            col = pl.multiple_of(hh * PEER_QDIM + p * PEER_HALF, PEER_HALF)
            qp = q_scr[:, pl.ds(col, PEER_HALF)]
            sct = _dot_nt(keys_ref[hh, p], qp, HI)
            tops.append(_top_rows(sct, k))
        (s1, i1), (s2, i2) = tops
        cand = jnp.concatenate([s1[i:i + 1] + s2[:k // (i + 1)] for i in range(k)], 0)
        cand_id = jnp.concatenate([i1[i:i + 1] * PEER_NKEYS + i2[:k // (i + 1)] for i in range(k)], 0)
        cs, ids = _top_rows(cand, k, cand_id)
        e = jnp.exp(cs - jnp.max(cs, axis=0, keepdims=True))
        row = pl.multiple_of(hh * k, k)
        gate_ref[pl.ds(row, k), :] = e / jnp.sum(e, axis=0, keepdims=True)
        idx_scr[pl.ds(row, k), :] = ids
        return carry

    lax.fori_loop(0, PEER_HEADS, head, 0)
    idx_ref[...] = idx_scr[...].T
    gate_tok_ref[...] = gate_ref[...].T


def _route(h1, wq, keys, tm=256):
    t, d = h1.shape
    return pl.pallas_call(
        _route_kernel,
        grid=(t // tm,),
        in_specs=[
            pl.BlockSpec((tm, d), lambda i: (i, 0)),
            pl.BlockSpec(wq.shape, lambda i: (0, 0)),
            pl.BlockSpec(keys.shape, lambda i: (0, 0, 0, 0)),
        ],
        out_specs=[
            pl.BlockSpec((tm, N_SEL), lambda i: (i, 0)),
            pl.BlockSpec((N_SEL, tm), lambda i: (0, i)),
            pl.BlockSpec((tm, N_SEL), lambda i: (i, 0)),
        ],
        out_shape=[
            jax.ShapeDtypeStruct((t, N_SEL), jnp.int32),
            jax.ShapeDtypeStruct((N_SEL, t), F32),
            jax.ShapeDtypeStruct((t, N_SEL), F32),
        ],
        scratch_shapes=[pltpu.VMEM((tm, PEER_HEADS * PEER_QDIM), F32),
                        pltpu.VMEM((N_SEL, tm), jnp.int32)],
        compiler_params=_cparams(("parallel",)),
        name="route",
    )(h1, wq, keys)


def _peer_kernel(idx_ref, gate_ref, h_ref, uv_hbm, l2g_ref, l2b_ref, o_ref, buf, f_scr, sem, *, tb):
    d = h_ref.shape[1]

    def issue(t, slot):
        for j in range(N_SEL):
            pltpu.make_async_copy(uv_hbm.at[pl.ds(idx_ref[t, j], 1)], buf.at[slot, pl.ds(j, 1)],
                                  sem.at[slot]).start()

    def wait_all(slot):
        pltpu.make_async_copy(uv_hbm.at[pl.ds(0, N_SEL)], buf.at[slot], sem.at[slot]).wait()

    issue(0, 0)
    lane = lax.broadcasted_iota(jnp.int32, (N_SEL, tb), 1)

    def body(t, carry):
        slot = t % 2

        @pl.when(t + 1 < tb)
        def _():
            issue(t + 1, 1 - slot)

        wait_all(slot)
        z = jnp.sum(buf[slot, :, 0:d] * h_ref[pl.ds(t, 1), :], axis=1, keepdims=True)
        gcol = jnp.sum(jnp.where(lane == t, gate_ref[...], 0.0), axis=1, keepdims=True)
        coef = gcol * (0.5 * z * (1.0 + lax.erf(z * (2.0 ** -0.5))))
        f_scr[pl.ds(t, 1), :] = jnp.sum(coef * buf[slot, :, d:2 * d], axis=0, keepdims=True)
        return carry

    lax.fori_loop(0, tb, body, 0)
    o_ref[...] = _layer_norm(DEEPNORM_ALPHA * h_ref[...] + f_scr[...], l2g_ref[...], l2b_ref[...])


def _peer(idx, gate_t, h1, uv, l2g, l2b, tok0, ntok, tb=PEER_TB):
    d = h1.shape[1]
    b0 = tok0 // tb
    return pl.pallas_call(
        functools.partial(_peer_kernel, tb=tb),
        grid=(ntok // tb,),
        in_specs=[
            pl.BlockSpec((tb, N_SEL), lambda i: (b0 + i, 0), memory_space=pltpu.SMEM),
            pl.BlockSpec((N_SEL, tb), lambda i: (0, b0 + i)),
            pl.BlockSpec((tb, d), lambda i: (b0 + i, 0)),
            pl.BlockSpec(memory_space=pl.ANY),
            pl.BlockSpec(l2g.shape, lambda i: (0, 0)),
            pl.BlockSpec(l2b.shape, lambda i: (0, 0)),
        ],
        out_specs=pl.BlockSpec((tb, d), lambda i: (i, 0)),
        out_shape=jax.ShapeDtypeStruct((ntok, d), F32),
        scratch_shapes=[pltpu.VMEM((2, N_SEL, 2 * d), F32),
                        pltpu.VMEM((tb, d), F32),
                        pltpu.SemaphoreType.DMA((2,))],
        compiler_params=_cparams(("arbitrary",)),
        name="peer",
    )(idx, gate_t, h1, uv, l2g, l2b)


def _sc_mesh():
    return plsc.VectorSubcoreMesh(core_axis_name="c", subcore_axis_name="s")


def _worker_id():
    return lax.axis_index("s") * 2 + lax.axis_index("c")


def _sc_z_body(u_hbm, idx_hbm, h_hbm, z_hbm, idx_v, h_v, rows_v, z_v, sem, *, tpw, d, tok0):
    ngrp = N_SEL // SC_GROUP
    nchunk = d // SC_LANES
    nsteps = tpw * ngrp
    base = _worker_id() * tpw

    def gather(tok_buf, g, buf):
        return pltpu.make_async_copy(u_hbm.at[idx_v.at[tok_buf, g]], rows_v.at[buf], sem.at[buf])

    pltpu.sync_copy(idx_hbm.at[tok0 + base], idx_v.at[0])
    gather(0, 0, 0).start()

    @pl.loop(0, nsteps)
    def _(q):
        t_loc = q // ngrp
        g = q % ngrp
        buf = q % 2
        t = base + t_loc

        @pl.when(g == 0)
        def _():
            pltpu.sync_copy(h_hbm.at[tok0 + t], h_v)

            @pl.loop(0, N_SEL)
            def _(r):
                z_v[pl.ds(pl.multiple_of(r * SC_LANES, SC_LANES), SC_LANES)] = jnp.zeros((SC_LANES,), F32)

        @pl.when(q + 1 < nsteps)
        def _():
            tn = (q + 1) // ngrp
            gn = (q + 1) % ngrp

            @pl.when(gn == 0)
            def _():
                pltpu.sync_copy(idx_hbm.at[tok0 + base + tn], idx_v.at[tn % 2])

            gather(tn % 2, gn, 1 - buf).start()

        gather(t_loc % 2, g, buf).wait()

        @pl.loop(0, SC_GROUP, step=4)
        def _(rb):
            @pl.loop(0, nchunk, step=SC_CHUNK_UNROLL)
            def _(c0):
                accs = [jnp.zeros((SC_LANES,), F32) for _ in range(4)]
                for cc in range(SC_CHUNK_UNROLL):
                    sl = pl.ds(pl.multiple_of((c0 + cc) * SC_LANES, SC_LANES), SC_LANES)
                    hc = h_v[sl]
                    for i in range(4):
                        accs[i] = accs[i] + rows_v[buf, rb + i, sl] * hc
                for i in range(4):
                    row = pl.multiple_of((g * SC_GROUP + rb + i) * SC_LANES, SC_LANES)
                    plsc.addupdate(z_v.at[pl.ds(row, SC_LANES)], accs[i])

        @pl.when(g == ngrp - 1)
        def _():
            pltpu.sync_copy(z_v, z_hbm.at[t])


def _sc_z(u, idx, h, tok0, ntok):
    t, d = h.shape
    tpw = ntok // SC_WORKERS
    idx4 = idx.reshape(t, N_SEL // SC_GROUP, SC_GROUP)
    body = functools.partial(_sc_z_body, tpw=tpw, d=d, tok0=tok0)
    return pl.kernel(
        body,
        out_type=jax.ShapeDtypeStruct((ntok, N_SEL * SC_LANES), F32),
        mesh=_sc_mesh(),
        scratch_types=[
            pltpu.VMEM((2, N_SEL // SC_GROUP, SC_GROUP), jnp.int32),
            pltpu.VMEM((d,), F32),
            pltpu.VMEM((2, SC_GROUP, d), F32),
            pltpu.VMEM((N_SEL * SC_LANES,), F32),
            pltpu.SemaphoreType.DMA((2,)),
        ],
        name="sc_z",
    )(u, idx4, h)


def _sc_out_body(v_hbm, idx_hbm, coef_hbm, f_hbm, idx_v, coef_v, rows_v, out_v, sem, *, tpw, d, tok0):
    ngrp = N_SEL // SC_GROUP
    nchunk = d // SC_LANES
    nsteps = tpw * ngrp
    base = _worker_id() * tpw

    def gather(tok_buf, g, buf):
        return pltpu.make_async_copy(v_hbm.at[idx_v.at[tok_buf, g]], rows_v.at[buf], sem.at[buf])

    pltpu.sync_copy(idx_hbm.at[tok0 + base], idx_v.at[0])
    gather(0, 0, 0).start()

    @pl.loop(0, nsteps)
    def _(q):
        t_loc = q // ngrp
        g = q % ngrp
        buf = q % 2
        t = base + t_loc

        @pl.when(g == 0)
        def _():
            pltpu.sync_copy(coef_hbm.at[t], coef_v)

        @pl.when(q + 1 < nsteps)
        def _():
            tn = (q + 1) // ngrp
            gn = (q + 1) % ngrp

            @pl.when(gn == 0)
            def _():
                pltpu.sync_copy(idx_hbm.at[tok0 + base + tn], idx_v.at[tn % 2])

            gather(tn % 2, gn, 1 - buf).start()

        gather(t_loc % 2, g, buf).wait()

        @pl.loop(0, nchunk, step=SC_OUT_CHUNKS)
        def _(c0):
            sls = [pl.ds(pl.multiple_of((c0 + cc) * SC_LANES, SC_LANES), SC_LANES)
                   for cc in range(SC_OUT_CHUNKS)]
            accs = [None] * SC_OUT_CHUNKS
            for r in range(SC_GROUP):
                cf = coef_v[pl.ds(pl.multiple_of((g * SC_GROUP + r) * SC_LANES, SC_LANES), SC_LANES)]
                for cc in range(SC_OUT_CHUNKS):
                    p = rows_v[buf, r, sls[cc]] * cf
                    accs[cc] = p if accs[cc] is None else accs[cc] + p
            for cc in range(SC_OUT_CHUNKS):
                @pl.when(g == 0)
                def _():
                    out_v[sls[cc]] = accs[cc]

                @pl.when(g != 0)
                def _():
                    plsc.addupdate(out_v.at[sls[cc]], accs[cc])

        @pl.when(g == ngrp - 1)
        def _():
            pltpu.sync_copy(out_v, f_hbm.at[t])


def _sc_out(v, idx, coef16, tok0):
    t = idx.shape[0]
    ntok = coef16.shape[0]
    d = v.shape[1]
    tpw = ntok // SC_WORKERS
    idx4 = idx.reshape(t, N_SEL // SC_GROUP, SC_GROUP)
    body = functools.partial(_sc_out_body, tpw=tpw, d=d, tok0=tok0)
    return pl.kernel(
        body,
        out_type=jax.ShapeDtypeStruct((ntok, d), F32),
        mesh=_sc_mesh(),
        scratch_types=[
            pltpu.VMEM((2, N_SEL // SC_GROUP, SC_GROUP), jnp.int32),
            pltpu.VMEM((N_SEL * SC_LANES,), F32),
            pltpu.VMEM((2, SC_GROUP, d), F32),
            pltpu.VMEM((d,), F32),
            pltpu.SemaphoreType.DMA((2,)),
        ],
        name="sc_out",
    )(v, idx4, coef16)


def _coef_kernel(zp_ref, gate_ref, sel_ref, selt_ref, after_ref, o_ref):
    z = sum(_dot(part, sel_ref[...]) for part in _split3(zp_ref[...]))
    coef = gate_ref[...] * (0.5 * z * (1.0 + lax.erf(z * (2.0 ** -0.5))))
    o_ref[...] = sum(_dot(part, selt_ref[...]) for part in _split3(coef))


def _coef(zp, gate_tok, tok0, after, tm=256):
    ntok, wide = zp.shape
    b0 = tok0 // tm
    grp = np.arange(wide) // SC_LANES
    sel = jnp.asarray(grp[:, None] == np.arange(N_SEL)[None, :], BF16)
    return pl.pallas_call(
        _coef_kernel,
        grid=(ntok // tm,),
        in_specs=[
            pl.BlockSpec((tm, wide), lambda i: (i, 0)),
            pl.BlockSpec((tm, N_SEL), lambda i: (b0 + i, 0)),
            pl.BlockSpec(sel.shape, lambda i: (0, 0)),
            pl.BlockSpec(sel.shape[::-1], lambda i: (0, 0)),
            pl.BlockSpec((SUBLANES, LANES), lambda i: (0, 0)),
        ],
        out_specs=pl.BlockSpec((tm, wide), lambda i: (i, 0)),
        out_shape=jax.ShapeDtypeStruct((ntok, wide), F32),
        compiler_params=_cparams(("parallel",)),
        name="coef",
    )(zp, gate_tok, sel, sel.T, after)


def _ln_out_kernel(h_ref, f_ref, g_ref, b_ref, o_ref):
    o_ref[...] = _layer_norm(DEEPNORM_ALPHA * h_ref[...] + f_ref[...], g_ref[...], b_ref[...])


def _ln_out(h1, f, l2g, l2b, tok0, tm=256):
    ntok, d = f.shape
    b0 = tok0 // tm
    return pl.pallas_call(
        _ln_out_kernel,
        grid=(ntok // tm,),
        in_specs=[
            pl.BlockSpec((tm, d), lambda i: (b0 + i, 0)),
            pl.BlockSpec((tm, d), lambda i: (i, 0)),
            pl.BlockSpec(l2g.shape, lambda i: (0, 0)),
            pl.BlockSpec(l2b.shape, lambda i: (0, 0)),
        ],
        out_specs=pl.BlockSpec((tm, d), lambda i: (i, 0)),
        out_shape=jax.ShapeDtypeStruct((ntok, d), F32),
        compiler_params=_cparams(("parallel",)),
        name="ln_out",
    )(h1, f, l2g, l2b)


def _rope_tables(seq):
    pos = jnp.arange(seq, dtype=F32)
    inv_freq = ROPE_THETA ** (-jnp.arange(0, HEAD_DIM, 2, dtype=F32) / HEAD_DIM)
    ang = pos[:, None] * inv_freq[None, :]
    ang = jnp.concatenate([ang, ang], -1)
    sign = jnp.concatenate([-jnp.ones((HEAD_DIM // 2,), F32), jnp.ones((HEAD_DIM // 2,), F32)])
    reps = LANES // HEAD_DIM
    return jnp.tile(jnp.cos(ang), (1, reps)), jnp.tile(jnp.sin(ang) * sign, (1, reps))


def _layer(h2d, batch, seq, w_in, mu_prev, mu_next, w0, w2, a0, a2, g2, k_k, k_a, r_k, lnx_g, lnx_b,
           sink, w_out, ln1_g, ln1_b, peer_wq, peer_keys, peer_u, peer_v, ln2_g, ln2_b):
    w = RWKV_WIDTH
    row = lambda a: a.reshape(1, -1).astype(F32)
    w_in_p = jnp.pad(w_in, ((0, 0), (0, RWKV_COLS_PAD - RWKV_COLS))).astype(BF16)
    mup = jnp.pad(mu_prev, (0, RWKV_COLS_PAD - RWKV_COLS)).reshape(1, -1)
    mun = jnp.pad(mu_next, (0, RWKV_COLS_PAD - RWKV_COLS)).reshape(1, -1)
    wmix = jnp.zeros((LANES, 4 * w), F32)
    for d in range(2):
        wmix = wmix.at[d * DECAY_RANK:(d + 1) * DECAY_RANK, d * w:(d + 1) * w].set(w2[d])
        r0 = 2 * DECAY_RANK + d * ICLR_RANK
        wmix = wmix.at[r0:r0 + ICLR_RANK, (2 + d) * w:(3 + d) * w].set(a2[d])
    g2p = jnp.pad(g2, ((0, LANES - GATE_RANK), (0, 0)))
    head_of = np.arange(w) // HEAD_DIM
    bd = jnp.asarray(head_of[:, None] == head_of[None, :], F32)
    cos_t, sin_t = _rope_tables(seq)

    wq_b, keys_f, wout_b = peer_wq.astype(BF16), peer_keys.astype(F32), w_out.astype(BF16)

    def to_routing(x2d, nb):
        qkv, pr = _proj(x2d, w_in_p, cos_t, sin_t, seq)
        y_attn = _attention(qkv.reshape(nb, seq, ATTN_COLS), sink.astype(F32))
        r, v, kk, g, lw0, lw1, kd0, kd1, b0, b1 = _prep(
            pr, mup, mun, wmix, w0.astype(F32), a0.astype(F32), g2p, row(k_k), row(k_a), bd, seq)
        yf, yb = _wkv(r, v, kk, lw0, lw1, kd0, kd1, b0, b1, nb, seq)
        h1 = _mix(x2d, y_attn.reshape(nb * seq, ATTN_WIDTH), yf, yb, r, v, kd0, kd1, g,
                  row(lnx_g), row(lnx_b), row(r_k), bd, wout_b, row(ln1_g), row(ln1_b))
        return (h1,) + tuple(_route(h1, wq_b, keys_f))

    uv = jnp.concatenate([peer_u, peer_v], axis=1)
    l2g, l2b = row(ln2_g), row(ln2_b)

    nb_sc = batch * PEER_SC_SEQ_SHARE_8THS // 8
    t_sc = nb_sc * seq
    t_tc = (batch - nb_sc) * seq
    n_a = (t_tc * PEER_TC_FIRST_CALL_32NDS // 32) // PEER_TB * PEER_TB
    n_b = t_tc - n_a
    assert 0 < nb_sc < batch and t_sc % (SC_WORKERS * SUBLANES) == 0 and n_a > 0 and n_b % PEER_TB == 0
    h1s, idxs, _, gate_tok_s = to_routing(h2d[:t_sc], nb_sc)
    zp = _sc_z(peer_u.astype(F32), idxs, h1s, 0, t_sc)
    h1t, idxt, gate_t_t, _ = to_routing(h2d[t_sc:], batch - nb_sc)
    out_a = _peer(idxt, gate_t_t, h1t, uv, l2g, l2b, 0, n_a)
    f_sc = _sc_out(peer_v.astype(F32), idxs, _coef(zp, gate_tok_s, 0, out_a), 0)
    out_b = _peer(idxt, gate_t_t, h1t, uv, l2g, l2b, n_a, n_b)
    out_sc = _ln_out(h1s, f_sc, l2g, l2b, 0)
    return jnp.concatenate([out_sc, out_a, out_b], axis=0)


def kernel(x, w_in, mu_prev, mu_next, w0, w2, a0, a2, g2, k_k, k_a, r_k, lnx_g, lnx_b, sink, w_out,
           ln1_g, ln1_b, peer_wq, peer_keys, peer_u, peer_v, ln2_g, ln2_b):
    batch, seq, d = x.shape
    h = x.reshape(batch * seq, d)
    for l in range(DEPTH):
        h = _layer(h, batch, seq, w_in[l], mu_prev[l], mu_next[l], w0[l], w2[l], a0[l], a2[l], g2[l],
                   k_k[l], k_a[l], r_k[l], lnx_g[l], lnx_b[l], sink[l], w_out[l], ln1_g[l], ln1_b[l],
                   peer_wq[l], peer_keys[l], peer_u[l], peer_v[l], ln2_g[l], ln2_b[l])
    return h.reshape(batch, seq, d)
```

```python
import functools

import numpy as np
import jax
import jax.numpy as jnp
from jax import lax
from jax.experimental import pallas as pl
from jax.experimental.pallas import tpu as pltpu
from jax.experimental.pallas import tpu_sc as plsc

F32 = jnp.float32
BF16 = jnp.bfloat16
HI = lax.Precision.HIGHEST

HEAD_DIM = 64
N_Q_HEADS = 8
N_KV_HEADS = 2
Q_PER_KV = N_Q_HEADS // N_KV_HEADS
ATTN_WIDTH = N_Q_HEADS * HEAD_DIM
ATTN_KV_WIDTH = N_KV_HEADS * HEAD_DIM
ATTN_COLS = ATTN_WIDTH + 2 * ATTN_KV_WIDTH
WINDOW = 128
ATTN_BLOCK = 128
ROPE_THETA = 10000.0
N_RWKV_HEADS = 8
RWKV_WIDTH = N_RWKV_HEADS * HEAD_DIM
DECAY_RANK = 32
ICLR_RANK = 32
GATE_RANK = 96
RWKV_COLS = 3 * RWKV_WIDTH + 2 * DECAY_RANK + 2 * ICLR_RANK + GATE_RANK
RWKV_COLS_PAD = 1792
RWKV_GN_EPS = 64e-5
PEER_HEADS = 8
PEER_NKEYS = 128
PEER_QDIM = 256
PEER_HALF = PEER_QDIM // 2
PEER_TOPK = 16
N_SEL = PEER_HEADS * PEER_TOPK
LN_EPS = 1e-5
DEPTH = 1
DEEPNORM_ALPHA = (2.0 * DEPTH) ** 0.25

LANES = 128
SUBLANES = 8
WKV_CHUNK = 64
VMEM_LIMIT = 48 * 1024 * 1024
PEER_TB = 128
SC_LANES = 16
SC_WORKERS = 32
SC_GROUP = 32
SC_CHUNK_UNROLL = 8
SC_OUT_CHUNKS = 4
PEER_SC_SEQ_SHARE_8THS = 5
PEER_TC_FIRST_CALL_32NDS = 13


def _cparams(sem):
    return pltpu.CompilerParams(dimension_semantics=sem, vmem_limit_bytes=VMEM_LIMIT)


def _dot(a, b, precision=None):
    return jnp.dot(a, b, preferred_element_type=F32, precision=precision)


def _dot_nt(a, b, precision=None):
    return lax.dot_general(a, b, (((1,), (1,)), ((), ())), preferred_element_type=F32,
                           precision=precision)


def _dot_tn(a, b, precision=None):
    return lax.dot_general(a, b, (((0,), (0,)), ((), ())), preferred_element_type=F32,
                           precision=precision)


def _layer_norm(z, g, b):
    mu = jnp.mean(z, -1, keepdims=True)
    zc = z - mu
    var = jnp.mean(zc * zc, -1, keepdims=True)
    return zc * lax.rsqrt(var + LN_EPS) * g + b


def _proj_kernel(x_ref, w_ref, cos_ref, sin_ref, qkv_ref, pr_ref):
    xb = x_ref[...].astype(BF16)
    cos = cos_ref[...]
    sin = sin_ref[...]
    lane = lax.broadcasted_iota(jnp.int32, cos.shape, 1)
    first_half = (lane & (HEAD_DIM // 2)) == 0

    def rope(t):
        rot = jnp.where(first_half, pltpu.roll(t, LANES - HEAD_DIM // 2, 1),
                        pltpu.roll(t, HEAD_DIM // 2, 1))
        return t * cos + rot * sin

    for c in range(0, ATTN_COLS, 2 * LANES):
        acc = _dot(xb, w_ref[:, c:c + 2 * LANES])
        for half in range(2):
            col = c + half * LANES
            t = acc[:, half * LANES:(half + 1) * LANES]
            if col < ATTN_WIDTH + ATTN_KV_WIDTH:
                t = rope(t)
            qkv_ref[:, col:col + LANES] = t
    for c in range(0, RWKV_COLS_PAD, 2 * LANES):
        pr_ref[:, c:c + 2 * LANES] = _dot(xb, w_ref[:, ATTN_COLS + c:ATTN_COLS + c + 2 * LANES])


def _proj(x2, w_in_p, cos_t, sin_t, seq, tm=512):
    t, d = x2.shape
    n_pos = seq // tm
    return pl.pallas_call(
        _proj_kernel,
        grid=(t // tm,),
        in_specs=[
            pl.BlockSpec((tm, d), lambda i: (i, 0)),
            pl.BlockSpec(w_in_p.shape, lambda i: (0, 0)),
            pl.BlockSpec((tm, LANES), lambda i: (i % n_pos, 0)),
            pl.BlockSpec((tm, LANES), lambda i: (i % n_pos, 0)),
        ],
        out_specs=[
            pl.BlockSpec((tm, ATTN_COLS), lambda i: (i, 0)),
            pl.BlockSpec((tm, RWKV_COLS_PAD), lambda i: (i, 0)),
        ],
        out_shape=[
            jax.ShapeDtypeStruct((t, ATTN_COLS), F32),
            jax.ShapeDtypeStruct((t, RWKV_COLS_PAD), F32),
        ],
        compiler_params=_cparams(("parallel",)),
        name="proj",
    )(x2, w_in_p, cos_t, sin_t)


def _attn_kernel(sink_ref, q_ref, kp_ref, kc_ref, kn_ref, vp_ref, vc_ref, vn_ref, o_ref, *, nb):
    n = pl.program_id(1)
    blk = ATTN_BLOCK
    rows = Q_PER_KV * blk
    q = q_ref[0] * (HEAD_DIM ** -0.5)
    kwin = jnp.concatenate([kp_ref[0], kc_ref[0], kn_ref[0]], axis=0)
    vwin = jnp.concatenate([vp_ref[0], vc_ref[0], vn_ref[0]], axis=0)
    qi = lax.broadcasted_iota(jnp.int32, (rows, 3 * blk), 0) & (blk - 1)
    kj = lax.broadcasted_iota(jnp.int32, (rows, 3 * blk), 1)
    dist = kj - qi
    valid = ((dist >= blk - WINDOW) & (dist <= blk + WINDOW)
             & ((kj >= blk) | (n > 0)) & ((kj < 2 * blk) | (n < nb - 1)))
    rowg = lax.broadcasted_iota(jnp.int32, (rows, 1), 0) // blk
    outs = []
    for h in range(N_KV_HEADS):
        qs = jnp.concatenate(
            [q[:, (Q_PER_KV * h + g) * HEAD_DIM:(Q_PER_KV * h + g + 1) * HEAD_DIM]
             for g in range(Q_PER_KV)], axis=0)
        kh = kwin[:, h * HEAD_DIM:(h + 1) * HEAD_DIM]
        vh = vwin[:, h * HEAD_DIM:(h + 1) * HEAD_DIM]
        logits = _dot_nt(qs.astype(BF16), kh.astype(BF16))
        logits = jnp.where(valid, logits, -1e30)
        sk = jnp.zeros((rows, 1), F32)
        for g in range(Q_PER_KV):
            sk = jnp.where(rowg == g, sink_ref[Q_PER_KV * h + g], sk)
        m = jnp.maximum(jnp.max(logits, -1, keepdims=True), sk)
        e = jnp.exp(logits - m)
        den = jnp.sum(e, -1, keepdims=True) + jnp.exp(sk - m)
        p = e / den
        o = _dot(p.astype(BF16), vh.astype(BF16))
        for g in range(Q_PER_KV):
            outs.append(o[g * blk:(g + 1) * blk])
    o_ref[0] = jnp.concatenate(outs, axis=1)


def _attention(qkv3, sink):
    b, s, _ = qkv3.shape
    blk = ATTN_BLOCK
    nb = s // blk
    kcol = ATTN_WIDTH // LANES
    vcol = kcol + 1

    def spec(col, shift):
        def imap(bi, n):
            return (bi, jnp.clip(n + shift, 0, nb - 1), col)
        return pl.BlockSpec((1, blk, LANES), imap)

    return pl.pallas_call(
        functools.partial(_attn_kernel, nb=nb),
        grid=(b, nb),
        in_specs=[
            pl.BlockSpec(memory_space=pltpu.SMEM),
            pl.BlockSpec((1, blk, ATTN_WIDTH), lambda bi, n: (bi, n, 0)),
            spec(kcol, -1), spec(kcol, 0), spec(kcol, 1),
            spec(vcol, -1), spec(vcol, 0), spec(vcol, 1),
        ],
        out_specs=pl.BlockSpec((1, blk, ATTN_WIDTH), lambda bi, n: (bi, n, 0)),
        out_shape=jax.ShapeDtypeStruct((b, s, ATTN_WIDTH), F32),
        compiler_params=_cparams(("parallel", "parallel")),
        name="attn",
    )(sink, qkv3, qkv3, qkv3, qkv3, qkv3, qkv3, qkv3)


def _softplus(x):
    return jnp.maximum(x, 0.0) + jnp.log(1.0 + jnp.exp(-jnp.abs(x)))


def _sigmoid(x):
    return 1.0 / (1.0 + jnp.exp(-x))


def _prep_kernel(p_ref, hp_ref, hn_ref, mup_ref, mun_ref, wmix_ref, w0_ref, a0_ref, g2_ref,
                 kk_ref, ka_ref, bd_ref,
                 r_o, v_o, kk_o, g_o, lw0_o, lw1_o, kd0_o, kd1_o, b0_o, b1_o, *, tm, seq):
    i = pl.program_id(0)
    row = lax.broadcasted_iota(jnp.int32, (tm, 1), 0)
    seq_start = (i * tm) % seq == 0
    seq_end = ((i + 1) * tm) % seq == 0

    def shifted(c0, c1):
        p = p_ref[:, c0:c1]
        prev_row = jnp.where(seq_start, 0.0, hp_ref[SUBLANES - 1:SUBLANES, c0:c1])
        next_row = jnp.where(seq_end, 0.0, hn_ref[0:1, c0:c1])
        p_prev = jnp.where(row == 0, prev_row, pltpu.roll(p, 1, 0))
        p_next = jnp.where(row == tm - 1, next_row, pltpu.roll(p, tm - 1, 0))
        return p + mup_ref[:, c0:c1] * (p_prev - p) + mun_ref[:, c0:c1] * (p_next - p)

    w = RWKV_WIDTH
    r = shifted(0, w)
    k = shifted(w, 2 * w)
    v = shifted(2 * w, 3 * w)
    codes = shifted(3 * w, 3 * w + LANES)
    gd = shifted(3 * w + LANES, 3 * w + 2 * LANES)
    r_o[...] = r
    v_o[...] = v

    lane = lax.broadcasted_iota(jnp.int32, codes.shape, 1)
    codes = jnp.where(lane < 2 * DECAY_RANK, jnp.tanh(codes), codes)
    mm = _dot(codes, wmix_ref[...], HI)
    g_o[...] = _dot(_sigmoid(gd), g2_ref[...], HI)

    kkv = k * kk_ref[...]
    bd = bd_ref[...].astype(BF16)
    ss = sum(_dot(part, bd) for part in _split3(kkv * kkv))
    kkn = kkv * lax.rsqrt(jnp.maximum(ss, 1e-24))
    kk_o[...] = kkn

    ka = ka_ref[...]
    for d, (lw_o, kd_o, b_o) in enumerate(((lw0_o, kd0_o, b0_o), (lw1_o, kd1_o, b1_o))):
        w_log = -_softplus(-(w0_ref[d:d + 1, :] + mm[:, d * w:(d + 1) * w])) - 0.5
        lw_o[...] = -jnp.exp(w_log)
        a = _sigmoid(a0_ref[d:d + 1, :] + mm[:, (2 + d) * w:(3 + d) * w])
        kd_o[...] = k * (1.0 + (a - 1.0) * ka)
        b_o[...] = kkn * a


def _prep(pr, mup, mun, wmix, w0, a0, g2p, k_k, k_a, bd, seq, tm=256):
    t = pr.shape[0]
    nblk8 = t // SUBLANES
    per = tm // SUBLANES
    full = lambda a: pl.BlockSpec(a.shape, lambda i: (0,) * a.ndim)
    out = pl.BlockSpec((tm, RWKV_WIDTH), lambda i: (i, 0))
    return pl.pallas_call(
        functools.partial(_prep_kernel, tm=tm, seq=seq),
        grid=(t // tm,),
        in_specs=[
            pl.BlockSpec((tm, RWKV_COLS_PAD), lambda i: (i, 0)),
            pl.BlockSpec((SUBLANES, RWKV_COLS_PAD), lambda i: (jnp.maximum(i * per - 1, 0), 0)),
            pl.BlockSpec((SUBLANES, RWKV_COLS_PAD),
                         lambda i: (jnp.minimum((i + 1) * per, nblk8 - 1), 0)),
            full(mup), full(mun), full(wmix), full(w0), full(a0), full(g2p),
            full(k_k), full(k_a), full(bd),
        ],
        out_specs=[out] * 10,
        out_shape=[jax.ShapeDtypeStruct((t, RWKV_WIDTH), F32)] * 10,
        compiler_params=_cparams(("parallel",)),
        name="prep",
    )(pr, pr, pr, mup, mun, wmix, w0, a0, g2p, k_k, k_a, bd)


def _split3(x):
    hi = x.astype(BF16)
    r1 = x - hi.astype(F32)
    mid = r1.astype(BF16)
    lo = (r1 - mid.astype(F32)).astype(BF16)
    return hi, mid, lo


def _wkv_kernel(rf, vf, kkf, lwf, kdf, bf, rb, vb, kkb, lwb, kdb, bb, yf_ref, yb_ref, state):
    c = pl.program_id(1)
    n = WKV_CHUNK
    hd = HEAD_DIM
    nh = N_RWKV_HEADS
    def bmm(a, b, ca, cb):
        return lax.dot_general(a.astype(BF16), b.astype(BF16), (((ca,), (cb,)), ((0,), (0,))),
                               preferred_element_type=F32)

    @pl.when(c == 0)
    def _():
        state[...] = jnp.zeros_like(state)

    ti = lax.broadcasted_iota(jnp.int32, (n, n), 0)
    si = lax.broadcasted_iota(jnp.int32, (n, n), 1)
    dirs = ((rf, vf, kkf, lwf, kdf, bf, si <= ti, si < ti, n - 1),
            (rb, vb, kkb, lwb, kdb, bb, si >= ti, si > ti, 0))
    heads = lambda x: jnp.stack([x[:, j * hd:(j + 1) * hd] for j in range(nh)], 0)
    parts = []
    for r_ref, v_ref, kk_ref, lw_ref, kd_ref, b_ref, incl, strict, last in dirs:
        lw = lw_ref[0]
        tri = incl.astype(BF16)
        cum = sum(_dot(tri, part) for part in _split3(lw))
        e_neg = jnp.exp(-cum)
        e_last = jnp.exp(cum[last:last + 1, :] - cum)
        g_scale = jnp.exp(cum[last:last + 1, :])
        parts.append(dict(
            at=heads((-kk_ref[0] * jnp.exp(cum - lw)).astype(BF16)),
            rt=heads((r_ref[0] * jnp.exp(cum)).astype(BF16)),
            bt=heads((b_ref[0] * e_neg).astype(BF16)),
            kt=heads((kd_ref[0] * e_neg).astype(BF16)),
            bl=heads((b_ref[0] * e_last).astype(BF16)),
            kl=heads((kd_ref[0] * e_last).astype(BF16)),
            v=heads(v_ref[0].astype(BF16)),
            gs=heads(g_scale),
            incl=jnp.broadcast_to(incl[None], (nh, n, n)),
            strict=jnp.broadcast_to(strict[None], (nh, n, n))))
    cat = lambda key: jnp.concatenate([parts[0][key], parts[1][key]], 0)
    at, rt, bt, kt, bl, kl, v, gs = (cat(k) for k in ("at", "rt", "bt", "kt", "bl", "kl", "v", "gs"))
    incl, strict = cat("incl"), cat("strict")
    g0 = state[...]
    g0b = g0.astype(BF16)
    m1 = bmm(jnp.concatenate([at, rt], 1), jnp.concatenate([bt, kt], 1), 2, 2)
    a_ab = jnp.where(strict, m1[:, :n, :n], 0.0)
    a_ak = jnp.where(strict, m1[:, :n, n:], 0.0)
    a_rb = jnp.where(incl, m1[:, n:, :n], 0.0)
    a_rk = jnp.where(incl, m1[:, n:, n:], 0.0)
    tinv = jnp.where((ti == si)[None], 1.0, a_ab)
    pw = a_ab
    for _ in range(int(np.log2(n)) - 1):
        pw = bmm(pw, pw, 2, 1)
        tinv = tinv + bmm(tinv, pw, 2, 1)
    rhs = bmm(a_ak, v, 2, 1) + bmm(at, g0b, 2, 2)
    u = bmm(tinv, rhs, 2, 1)
    uv = jnp.concatenate([u.astype(BF16), v], 1)
    y = bmm(jnp.concatenate([a_rb, a_rk], 2), uv, 2, 1) + bmm(rt, g0b, 2, 2)
    state[...] = g0 * gs + bmm(uv, jnp.concatenate([bl, kl], 1), 1, 1)
    for d, y_ref in enumerate((yf_ref, yb_ref)):
        y_ref[0] = jnp.concatenate([y[d * nh + j] for j in range(nh)], axis=1)


def _wkv(r, v, kk, lw0, lw1, kd0, kd1, b0, b1, batch, seq):
    n = WKV_CHUNK
    nc = seq // n
    shp = (batch, seq, RWKV_WIDTH)
    arrs = [a.reshape(shp) for a in (r, v, kk, lw0, kd0, b0, r, v, kk, lw1, kd1, b1)]
    fwd = pl.BlockSpec((1, n, RWKV_WIDTH), lambda b, c: (b, c, 0))
    bwd = pl.BlockSpec((1, n, RWKV_WIDTH), lambda b, c: (b, nc - 1 - c, 0))
    yf, yb = pl.pallas_call(
        _wkv_kernel,
        grid=(batch, nc),
        in_specs=[fwd] * 6 + [bwd] * 6,
        out_specs=[fwd, bwd],
        out_shape=[jax.ShapeDtypeStruct(shp, F32)] * 2,
        scratch_shapes=[pltpu.VMEM((2 * N_RWKV_HEADS, HEAD_DIM, HEAD_DIM), F32)],
        compiler_params=_cparams(("parallel", "arbitrary")),
        name="wkv",
    )(*arrs)
    return yf.reshape(batch * seq, RWKV_WIDTH), yb.reshape(batch * seq, RWKV_WIDTH)


def _mix_kernel(x_ref, ya_ref, yf_ref, yb_ref, r_ref, v_ref, kd0_ref, kd1_ref, g_ref,
                lng_ref, lnb_ref, rk_ref, bd_ref, wout_ref, l1g_ref, l1b_ref, h_ref):
    bd = bd_ref[...].astype(BF16)
    head_sum = lambda t: sum(_dot(part, bd) for part in _split3(t))
    inv = 1.0 / HEAD_DIM
    y = yf_ref[...] + yb_ref[...]
    mu = head_sum(y) * inv
    yc = y - mu
    var = head_sum(yc * yc) * inv
    yn = yc * lax.rsqrt(var + RWKV_GN_EPS) * lng_ref[...] + lnb_ref[...]
    k_mean = 0.5 * (kd0_ref[...] + kd1_ref[...])
    v = v_ref[...]
    bonus = head_sum(r_ref[...] * k_mean * rk_ref[...]) * v
    yr = (yn + bonus) * g_ref[...]
    mix = (_dot(ya_ref[...].astype(BF16), wout_ref[:ATTN_WIDTH, :])
           + _dot(yr.astype(BF16), wout_ref[ATTN_WIDTH:, :]))
    h_ref[...] = _layer_norm(DEEPNORM_ALPHA * x_ref[...] + mix, l1g_ref[...], l1b_ref[...])


def _mix(x2, ya, yf, yb, r, v, kd0, kd1, g, lng, lnb, rk, bd, wout, l1g, l1b, tm=256):
    t, d = x2.shape
    full = lambda a: pl.BlockSpec(a.shape, lambda i: (0,) * a.ndim)
    half = pl.BlockSpec((tm, RWKV_WIDTH), lambda i: (i, 0))
    wide = pl.BlockSpec((tm, d), lambda i: (i, 0))
    return pl.pallas_call(
        _mix_kernel,
        grid=(t // tm,),
        in_specs=[wide] + [half] * 8 + [full(lng), full(lnb), full(rk), full(bd), full(wout),
                                        full(l1g), full(l1b)],
        out_specs=wide,
        out_shape=jax.ShapeDtypeStruct((t, d), F32),
        compiler_params=_cparams(("parallel",)),
        name="mix",
    )(x2, ya, yf, yb, r, v, kd0, kd1, g, lng, lnb, rk, bd, wout, l1g, l1b)


def _top_rows(sc, k, payload=None):
    n = sc.shape[0]
    iota = lax.broadcasted_iota(jnp.int32, sc.shape, 0).astype(F32)
    vals, picks = [], []
    for _ in range(k):
        m = jnp.max(sc, axis=0, keepdims=True)
        pos = jnp.min(jnp.where(sc == m, iota, float(n)), axis=0, keepdims=True)
        hit = iota == pos
        vals.append(m)
        if payload is None:
            picks.append(pos)
        else:
            picks.append(jnp.max(jnp.where(hit, payload, -1.0), axis=0, keepdims=True))
        sc = jnp.where(hit, -jnp.inf, sc)
    return jnp.concatenate(vals, 0), jnp.concatenate(picks, 0)


def _route_kernel(h_ref, wq_ref, keys_ref, idx_ref, gate_ref, gate_tok_ref, q_scr, idx_scr):
    k = PEER_TOPK
    q_scr[...] = _dot(h_ref[...].astype(BF16), wq_ref[...])

    def head(hh, carry):
        tops = []
        for p in range(2):
            col = pl.multiple_of(hh * PEER_QDIM + p * PEER_HALF, PEER_HALF)
            qp = q_scr[:, pl.ds(col, PEER_HALF)]
            sct = _dot_nt(keys_ref[hh, p], qp, HI)
            tops.append(_top_rows(sct, k))
        (s1, i1), (s2, i2) = tops
        cand = jnp.concatenate([s1[i:i + 1] + s2[:k // (i + 1)] for i in range(k)], 0)
        cand_id = jnp.concatenate([i1[i:i + 1] * PEER_NKEYS + i2[:k // (i + 1)] for i in range(k)], 0)
        cs, ids = _top_rows(cand, k, cand_id)
        e = jnp.exp(cs - jnp.max(cs, axis=0, keepdims=True))
        row = pl.multiple_of(hh * k, k)
        gate_ref[pl.ds(row, k), :] = e / jnp.sum(e, axis=0, keepdims=True)
        idx_scr[pl.ds(row, k), :] = ids.astype(jnp.int32)
        return carry

    lax.fori_loop(0, PEER_HEADS, head, 0)
    idx_ref[...] = idx_scr[...].T
    gate_tok_ref[...] = gate_ref[...].T


def _route(h1, wq, keys, tm=256):
    t, d = h1.shape
    return pl.pallas_call(
        _route_kernel,
        grid=(t // tm,),
        in_specs=[
            pl.BlockSpec((tm, d), lambda i: (i, 0)),
            pl.BlockSpec(wq.shape, lambda i: (0, 0)),
            pl.BlockSpec(keys.shape, lambda i: (0, 0, 0, 0)),
        ],
        out_specs=[
            pl.BlockSpec((tm, N_SEL), lambda i: (i, 0)),
            pl.BlockSpec((N_SEL, tm), lambda i: (0, i)),
            pl.BlockSpec((tm, N_SEL), lambda i: (i, 0)),
        ],
        out_shape=[
            jax.ShapeDtypeStruct((t, N_SEL), jnp.int32),
            jax.ShapeDtypeStruct((N_SEL, t), F32),
            jax.ShapeDtypeStruct((t, N_SEL), F32),
        ],
        scratch_shapes=[pltpu.VMEM((tm, PEER_HEADS * PEER_QDIM), F32),
                        pltpu.VMEM((N_SEL, tm), jnp.int32)],
        compiler_params=_cparams(("parallel",)),
        name="route",
    )(h1, wq, keys)


def _peer_kernel(idx_ref, gate_ref, h_ref, uv_hbm, l2g_ref, l2b_ref, o_ref, buf, f_scr, sem, *, tb):
    d = h_ref.shape[1]

    def issue(t, slot):
        for j in range(N_SEL):
            pltpu.make_async_copy(uv_hbm.at[pl.ds(idx_ref[t, j], 1)], buf.at[slot, pl.ds(j, 1)],
                                  sem.at[slot]).start()

    def wait_all(slot):
        pltpu.make_async_copy(uv_hbm.at[pl.ds(0, N_SEL)], buf.at[slot], sem.at[slot]).wait()

    issue(0, 0)
    lane = lax.broadcasted_iota(jnp.int32, (N_SEL, tb), 1)

    def body(t, carry):
        slot = t % 2

        @pl.when(t + 1 < tb)
        def _():
            issue(t + 1, 1 - slot)

        wait_all(slot)
        z = jnp.sum(buf[slot, :, 0:d] * h_ref[pl.ds(t, 1), :], axis=1, keepdims=True)
        gcol = jnp.sum(jnp.where(lane == t, gate_ref[...], 0.0), axis=1, keepdims=True)
        coef = gcol * (0.5 * z * (1.0 + lax.erf(z * (2.0 ** -0.5))))
        f_scr[pl.ds(t, 1), :] = jnp.sum(coef * buf[slot, :, d:2 * d], axis=0, keepdims=True)
        return carry

    lax.fori_loop(0, tb, body, 0)
    o_ref[...] = _layer_norm(DEEPNORM_ALPHA * h_ref[...] + f_scr[...], l2g_ref[...], l2b_ref[...])


def _peer(idx, gate_t, h1, uv, l2g, l2b, tok0, ntok, tb=PEER_TB):
    d = h1.shape[1]
    b0 = tok0 // tb
    return pl.pallas_call(
        functools.partial(_peer_kernel, tb=tb),
        grid=(ntok // tb,),
        in_specs=[
            pl.BlockSpec((tb, N_SEL), lambda i: (b0 + i, 0), memory_space=pltpu.SMEM),
            pl.BlockSpec((N_SEL, tb), lambda i: (0, b0 + i)),
            pl.BlockSpec((tb, d), lambda i: (b0 + i, 0)),
            pl.BlockSpec(memory_space=pl.ANY),
            pl.BlockSpec(l2g.shape, lambda i: (0, 0)),
            pl.BlockSpec(l2b.shape, lambda i: (0, 0)),
        ],
        out_specs=pl.BlockSpec((tb, d), lambda i: (i, 0)),
        out_shape=jax.ShapeDtypeStruct((ntok, d), F32),
        scratch_shapes=[pltpu.VMEM((2, N_SEL, 2 * d), F32),
                        pltpu.VMEM((tb, d), F32),
                        pltpu.SemaphoreType.DMA((2,))],
        compiler_params=_cparams(("arbitrary",)),
        name="peer",
    )(idx, gate_t, h1, uv, l2g, l2b)


def _sc_mesh():
    return plsc.VectorSubcoreMesh(core_axis_name="c", subcore_axis_name="s")


def _worker_id():
    return lax.axis_index("s") * 2 + lax.axis_index("c")


def _sc_z_body(u_hbm, idx_hbm, h_hbm, z_hbm, idx_v, h_v, rows_v, z_v, sem, *, tpw, d, tok0):
    ngrp = N_SEL // SC_GROUP
    nchunk = d // SC_LANES
    nsteps = tpw * ngrp
    base = _worker_id() * tpw

    def gather(tok_buf, g, buf):
        return pltpu.make_async_copy(u_hbm.at[idx_v.at[tok_buf, g]], rows_v.at[buf], sem.at[buf])

    pltpu.sync_copy(idx_hbm.at[tok0 + base], idx_v.at[0])
    gather(0, 0, 0).start()

    @pl.loop(0, nsteps)
    def _(q):
        t_loc = q // ngrp
        g = q % ngrp
        buf = q % 2
        t = base + t_loc

        @pl.when(g == 0)
        def _():
            pltpu.sync_copy(h_hbm.at[tok0 + t], h_v)

            @pl.loop(0, N_SEL)
            def _(r):
                z_v[pl.ds(pl.multiple_of(r * SC_LANES, SC_LANES), SC_LANES)] = jnp.zeros((SC_LANES,), F32)

        @pl.when(q + 1 < nsteps)
        def _():
            tn = (q + 1) // ngrp
            gn = (q + 1) % ngrp

            @pl.when(gn == 0)
            def _():
                pltpu.sync_copy(idx_hbm.at[tok0 + base + tn], idx_v.at[tn % 2])

            gather(tn % 2, gn, 1 - buf).start()

        gather(t_loc % 2, g, buf).wait()

        @pl.loop(0, SC_GROUP, step=4)
        def _(rb):
            @pl.loop(0, nchunk, step=SC_CHUNK_UNROLL)
            def _(c0):
                accs = [jnp.zeros((SC_LANES,), F32) for _ in range(4)]
                for cc in range(SC_CHUNK_UNROLL):
                    sl = pl.ds(pl.multiple_of((c0 + cc) * SC_LANES, SC_LANES), SC_LANES)
                    hc = h_v[sl]
                    for i in range(4):
                        accs[i] = accs[i] + rows_v[buf, rb + i, sl] * hc
                for i in range(4):
                    row = pl.multiple_of((g * SC_GROUP + rb + i) * SC_LANES, SC_LANES)
                    plsc.addupdate(z_v.at[pl.ds(row, SC_LANES)], accs[i])

        @pl.when(g == ngrp - 1)
        def _():
            pltpu.sync_copy(z_v, z_hbm.at[t])


def _sc_z(u, idx, h, tok0, ntok):
    t, d = h.shape
    tpw = ntok // SC_WORKERS
    idx4 = idx.reshape(t, N_SEL // SC_GROUP, SC_GROUP)
    body = functools.partial(_sc_z_body, tpw=tpw, d=d, tok0=tok0)
    return pl.kernel(
        body,
        out_type=jax.ShapeDtypeStruct((ntok, N_SEL * SC_LANES), F32),
        mesh=_sc_mesh(),
        scratch_types=[
            pltpu.VMEM((2, N_SEL // SC_GROUP, SC_GROUP), jnp.int32),
            pltpu.VMEM((d,), F32),
            pltpu.VMEM((2, SC_GROUP, d), F32),
            pltpu.VMEM((N_SEL * SC_LANES,), F32),
            pltpu.SemaphoreType.DMA((2,)),
        ],
        name="sc_z",
    )(u, idx4, h)


def _sc_out_body(v_hbm, idx_hbm, coef_hbm, f_hbm, idx_v, coef_v, rows_v, out_v, sem, *, tpw, d, tok0):
    ngrp = N_SEL // SC_GROUP
    nchunk = d // SC_LANES
    nsteps = tpw * ngrp
    base = _worker_id() * tpw

    def gather(tok_buf, g, buf):
        return pltpu.make_async_copy(v_hbm.at[idx_v.at[tok_buf, g]], rows_v.at[buf], sem.at[buf])

    pltpu.sync_copy(idx_hbm.at[tok0 + base], idx_v.at[0])
    gather(0, 0, 0).start()

    @pl.loop(0, nsteps)
    def _(q):
        t_loc = q // ngrp
        g = q % ngrp
        buf = q % 2
        t = base + t_loc

        @pl.when(g == 0)
        def _():
            pltpu.sync_copy(coef_hbm.at[t], coef_v)

        @pl.when(q + 1 < nsteps)
        def _():
            tn = (q + 1) // ngrp
            gn = (q + 1) % ngrp

            @pl.when(gn == 0)
            def _():
                pltpu.sync_copy(idx_hbm.at[tok0 + base + tn], idx_v.at[tn % 2])

            gather(tn % 2, gn, 1 - buf).start()

        gather(t_loc % 2, g, buf).wait()

        @pl.loop(0, nchunk, step=SC_OUT_CHUNKS)
        def _(c0):
            sls = [pl.ds(pl.multiple_of((c0 + cc) * SC_LANES, SC_LANES), SC_LANES)
                   for cc in range(SC_OUT_CHUNKS)]
            accs = [None] * SC_OUT_CHUNKS
            for r in range(SC_GROUP):
                cf = coef_v[pl.ds(pl.multiple_of((g * SC_GROUP + r) * SC_LANES, SC_LANES), SC_LANES)]
                for cc in range(SC_OUT_CHUNKS):
                    p = rows_v[buf, r, sls[cc]] * cf
                    accs[cc] = p if accs[cc] is None else accs[cc] + p
            for cc in range(SC_OUT_CHUNKS):
                @pl.when(g == 0)
                def _():
                    out_v[sls[cc]] = accs[cc]

                @pl.when(g != 0)
                def _():
                    plsc.addupdate(out_v.at[sls[cc]], accs[cc])

        @pl.when(g == ngrp - 1)
        def _():
            pltpu.sync_copy(out_v, f_hbm.at[t])


def _sc_out(v, idx, coef16, tok0):
    t = idx.shape[0]
    ntok = coef16.shape[0]
    d = v.shape[1]
    tpw = ntok // SC_WORKERS
    idx4 = idx.reshape(t, N_SEL // SC_GROUP, SC_GROUP)
    body = functools.partial(_sc_out_body, tpw=tpw, d=d, tok0=tok0)
    return pl.kernel(
        body,
        out_type=jax.ShapeDtypeStruct((ntok, d), F32),
        mesh=_sc_mesh(),
        scratch_types=[
            pltpu.VMEM((2, N_SEL // SC_GROUP, SC_GROUP), jnp.int32),
            pltpu.VMEM((N_SEL * SC_LANES,), F32),
            pltpu.VMEM((2, SC_GROUP, d), F32),
            pltpu.VMEM((d,), F32),
            pltpu.SemaphoreType.DMA((2,)),
        ],
        name="sc_out",
    )(v, idx4, coef16)


def _coef_kernel(zp_ref, gate_ref, sel_ref, selt_ref, after_ref, o_ref):
    z = sum(_dot(part, sel_ref[...]) for part in _split3(zp_ref[...]))
    coef = gate_ref[...] * (0.5 * z * (1.0 + lax.erf(z * (2.0 ** -0.5))))
    o_ref[...] = sum(_dot(part, selt_ref[...]) for part in _split3(coef))


def _coef(zp, gate_tok, tok0, after, tm=256):
    ntok, wide = zp.shape
    b0 = tok0 // tm
    grp = np.arange(wide) // SC_LANES
    sel = jnp.asarray(grp[:, None] == np.arange(N_SEL)[None, :], BF16)
    return pl.pallas_call(
        _coef_kernel,
        grid=(ntok // tm,),
        in_specs=[
            pl.BlockSpec((tm, wide), lambda i: (i, 0)),
            pl.BlockSpec((tm, N_SEL), lambda i: (b0 + i, 0)),
            pl.BlockSpec(sel.shape, lambda i: (0, 0)),
            pl.BlockSpec(sel.shape[::-1], lambda i: (0, 0)),
            pl.BlockSpec((SUBLANES, LANES), lambda i: (0, 0)),
        ],
        out_specs=pl.BlockSpec((tm, wide), lambda i: (i, 0)),
        out_shape=jax.ShapeDtypeStruct((ntok, wide), F32),
        compiler_params=_cparams(("parallel",)),
        name="coef",
    )(zp, gate_tok, sel, sel.T, after)


def _ln_out_kernel(h_ref, f_ref, g_ref, b_ref, o_ref):
    o_ref[...] = _layer_norm(DEEPNORM_ALPHA * h_ref[...] + f_ref[...], g_ref[...], b_ref[...])


def _ln_out(h1, f, l2g, l2b, tok0, tm=256):
    ntok, d = f.shape
    b0 = tok0 // tm
    return pl.pallas_call(
        _ln_out_kernel,
        grid=(ntok // tm,),
        in_specs=[
            pl.BlockSpec((tm, d), lambda i: (b0 + i, 0)),
            pl.BlockSpec((tm, d), lambda i: (i, 0)),
            pl.BlockSpec(l2g.shape, lambda i: (0, 0)),
            pl.BlockSpec(l2b.shape, lambda i: (0, 0)),
        ],
        out_specs=pl.BlockSpec((tm, d), lambda i: (i, 0)),
        out_shape=jax.ShapeDtypeStruct((ntok, d), F32),
        compiler_params=_cparams(("parallel",)),
        name="ln_out",
    )(h1, f, l2g, l2b)


def _rope_tables(seq):
    pos = jnp.arange(seq, dtype=F32)
    inv_freq = ROPE_THETA ** (-jnp.arange(0, HEAD_DIM, 2, dtype=F32) / HEAD_DIM)
    ang = pos[:, None] * inv_freq[None, :]
    ang = jnp.concatenate([ang, ang], -1)
    sign = jnp.concatenate([-jnp.ones((HEAD_DIM // 2,), F32), jnp.ones((HEAD_DIM // 2,), F32)])
    reps = LANES // HEAD_DIM
    return jnp.tile(jnp.cos(ang), (1, reps)), jnp.tile(jnp.sin(ang) * sign, (1, reps))


def _layer(h2d, batch, seq, w_in, mu_prev, mu_next, w0, w2, a0, a2, g2, k_k, k_a, r_k, lnx_g, lnx_b,
           sink, w_out, ln1_g, ln1_b, peer_wq, peer_keys, peer_u, peer_v, ln2_g, ln2_b):
    w = RWKV_WIDTH
    row = lambda a: a.reshape(1, -1).astype(F32)
    w_in_p = jnp.pad(w_in, ((0, 0), (0, RWKV_COLS_PAD - RWKV_COLS))).astype(BF16)
    mup = jnp.pad(mu_prev, (0, RWKV_COLS_PAD - RWKV_COLS)).reshape(1, -1)
    mun = jnp.pad(mu_next, (0, RWKV_COLS_PAD - RWKV_COLS)).reshape(1, -1)
    wmix = jnp.zeros((LANES, 4 * w), F32)
    for d in range(2):
        wmix = wmix.at[d * DECAY_RANK:(d + 1) * DECAY_RANK, d * w:(d + 1) * w].set(w2[d])
        r0 = 2 * DECAY_RANK + d * ICLR_RANK
        wmix = wmix.at[r0:r0 + ICLR_RANK, (2 + d) * w:(3 + d) * w].set(a2[d])
    g2p = jnp.pad(g2, ((0, LANES - GATE_RANK), (0, 0)))
    head_of = np.arange(w) // HEAD_DIM
    bd = jnp.asarray(head_of[:, None] == head_of[None, :], F32)
    cos_t, sin_t = _rope_tables(seq)

    wq_b, keys_f, wout_b = peer_wq.astype(BF16), peer_keys.astype(F32), w_out.astype(BF16)

    def to_routing(x2d, nb):
        qkv, pr = _proj(x2d, w_in_p, cos_t, sin_t, seq)
        y_attn = _attention(qkv.reshape(nb, seq, ATTN_COLS), sink.astype(F32))
        r, v, kk, g, lw0, lw1, kd0, kd1, b0, b1 = _prep(
            pr, mup, mun, wmix, w0.astype(F32), a0.astype(F32), g2p, row(k_k), row(k_a), bd, seq)
        yf, yb = _wkv(r, v, kk, lw0, lw1, kd0, kd1, b0, b1, nb, seq)
        h1 = _mix(x2d, y_attn.reshape(nb * seq, ATTN_WIDTH), yf, yb, r, v, kd0, kd1, g,
                  row(lnx_g), row(lnx_b), row(r_k), bd, wout_b, row(ln1_g), row(ln1_b))
        return (h1,) + tuple(_route(h1, wq_b, keys_f))

    uv = jnp.concatenate([peer_u, peer_v], axis=1)
    l2g, l2b = row(ln2_g), row(ln2_b)

    nb_sc = batch * PEER_SC_SEQ_SHARE_8THS // 8
    t_sc = nb_sc * seq
    t_tc = (batch - nb_sc) * seq
    n_a = (t_tc * PEER_TC_FIRST_CALL_32NDS // 32) // PEER_TB * PEER_TB
    n_b = t_tc - n_a
    assert 0 < nb_sc < batch and t_sc % (SC_WORKERS * SUBLANES) == 0 and n_a > 0 and n_b % PEER_TB == 0
    h1s, idxs, _, gate_tok_s = to_routing(h2d[:t_sc], nb_sc)
    zp = _sc_z(peer_u.astype(F32), idxs, h1s, 0, t_sc)
    h1t, idxt, gate_t_t, _ = to_routing(h2d[t_sc:], batch - nb_sc)
    out_a = _peer(idxt, gate_t_t, h1t, uv, l2g, l2b, 0, n_a)
    f_sc = _sc_out(peer_v.astype(F32), idxs, _coef(zp, gate_tok_s, 0, out_a), 0)
    out_b = _peer(idxt, gate_t_t, h1t, uv, l2g, l2b, n_a, n_b)
    out_sc = _ln_out(h1s, f_sc, l2g, l2b, 0)
    return jnp.concatenate([out_sc, out_a, out_b], axis=0)


def kernel(x, w_in, mu_prev, mu_next, w0, w2, a0, a2, g2, k_k, k_a, r_k, lnx_g, lnx_b, sink, w_out,
           ln1_g, ln1_b, peer_wq, peer_keys, peer_u, peer_v, ln2_g, ln2_b):
    batch, seq, d = x.shape
    h = x.reshape(batch * seq, d)
    for l in range(DEPTH):
        h = _layer(h, batch, seq, w_in[l], mu_prev[l], mu_next[l], w0[l], w2[l], a0[l], a2[l], g2[l],
                   k_k[l], k_a[l], r_k[l], lnx_g[l], lnx_b[l], sink[l], w_out[l], ln1_g[l], ln1_b[l],
                   peer_wq[l], peer_keys[l], peer_u[l], peer_v[l], ln2_g[l], ln2_b[l])
    return h.reshape(batch, seq, d)
```

```python
import functools

import numpy as np
import jax
import jax.numpy as jnp
from jax import lax
from jax.experimental import pallas as pl
from jax.experimental.pallas import tpu as pltpu
from jax.experimental.pallas import tpu_sc as plsc

F32 = jnp.float32
BF16 = jnp.bfloat16
HI = lax.Precision.HIGHEST

HEAD_DIM = 64
N_Q_HEADS = 8
N_KV_HEADS = 2
Q_PER_KV = N_Q_HEADS // N_KV_HEADS
ATTN_WIDTH = N_Q_HEADS * HEAD_DIM
ATTN_KV_WIDTH = N_KV_HEADS * HEAD_DIM
ATTN_COLS = ATTN_WIDTH + 2 * ATTN_KV_WIDTH
WINDOW = 128
ATTN_BLOCK = 128
ROPE_THETA = 10000.0
N_RWKV_HEADS = 8
RWKV_WIDTH = N_RWKV_HEADS * HEAD_DIM
DECAY_RANK = 32
ICLR_RANK = 32
GATE_RANK = 96
RWKV_COLS = 3 * RWKV_WIDTH + 2 * DECAY_RANK + 2 * ICLR_RANK + GATE_RANK
RWKV_COLS_PAD = 1792
RWKV_GN_EPS = 64e-5
PEER_HEADS = 8
PEER_NKEYS = 128
PEER_QDIM = 256
PEER_HALF = PEER_QDIM // 2
PEER_TOPK = 16
N_SEL = PEER_HEADS * PEER_TOPK
LN_EPS = 1e-5
DEPTH = 1
DEEPNORM_ALPHA = (2.0 * DEPTH) ** 0.25

LANES = 128
SUBLANES = 8
WKV_CHUNK = 64
VMEM_LIMIT = 48 * 1024 * 1024
PEER_TB = 128
ROUTE_HEADS_PER_STEP = 4
SC_LANES = 16
SC_WORKERS = 32
SC_GROUP = 32
SC_CHUNK_UNROLL = 8
SC_OUT_CHUNKS = 4
PEER_SC_SEQ_SHARE_8THS = 5
PEER_TC_FIRST_CALL_32NDS = 13


def _cparams(sem):
    return pltpu.CompilerParams(dimension_semantics=sem, vmem_limit_bytes=VMEM_LIMIT)


def _dot(a, b, precision=None):
    return jnp.dot(a, b, preferred_element_type=F32, precision=precision)


def _dot_nt(a, b, precision=None):
    return lax.dot_general(a, b, (((1,), (1,)), ((), ())), preferred_element_type=F32,
                           precision=precision)


def _dot_tn(a, b, precision=None):
    return lax.dot_general(a, b, (((0,), (0,)), ((), ())), preferred_element_type=F32,
                           precision=precision)


def _layer_norm(z, g, b):
    mu = jnp.mean(z, -1, keepdims=True)
    zc = z - mu
    var = jnp.mean(zc * zc, -1, keepdims=True)
    return zc * lax.rsqrt(var + LN_EPS) * g + b


def _proj_kernel(x_ref, w_ref, cos_ref, sin_ref, qkv_ref, pr_ref):
    xb = x_ref[...].astype(BF16)
    cos = cos_ref[...]
    sin = sin_ref[...]
    lane = lax.broadcasted_iota(jnp.int32, cos.shape, 1)
    first_half = (lane & (HEAD_DIM // 2)) == 0

    def rope(t):
        rot = jnp.where(first_half, pltpu.roll(t, LANES - HEAD_DIM // 2, 1),
                        pltpu.roll(t, HEAD_DIM // 2, 1))
        return t * cos + rot * sin

    for c in range(0, ATTN_COLS, 2 * LANES):
        acc = _dot(xb, w_ref[:, c:c + 2 * LANES])
        for half in range(2):
            col = c + half * LANES
            t = acc[:, half * LANES:(half + 1) * LANES]
            if col < ATTN_WIDTH + ATTN_KV_WIDTH:
                t = rope(t)
            qkv_ref[:, col:col + LANES] = t
    for c in range(0, RWKV_COLS_PAD, 2 * LANES):
        pr_ref[:, c:c + 2 * LANES] = _dot(xb, w_ref[:, ATTN_COLS + c:ATTN_COLS + c + 2 * LANES])


def _proj(x2, w_in_p, cos_t, sin_t, seq, tm=512):
    t, d = x2.shape
    n_pos = seq // tm
    return pl.pallas_call(
        _proj_kernel,
        grid=(t // tm,),
        in_specs=[
            pl.BlockSpec((tm, d), lambda i: (i, 0)),
            pl.BlockSpec(w_in_p.shape, lambda i: (0, 0)),
            pl.BlockSpec((tm, LANES), lambda i: (i % n_pos, 0)),
            pl.BlockSpec((tm, LANES), lambda i: (i % n_pos, 0)),
        ],
        out_specs=[
            pl.BlockSpec((tm, ATTN_COLS), lambda i: (i, 0)),
            pl.BlockSpec((tm, RWKV_COLS_PAD), lambda i: (i, 0)),
        ],
        out_shape=[
            jax.ShapeDtypeStruct((t, ATTN_COLS), F32),
            jax.ShapeDtypeStruct((t, RWKV_COLS_PAD), F32),
        ],
        compiler_params=_cparams(("parallel",)),
        name="proj",
    )(x2, w_in_p, cos_t, sin_t)


def _attn_kernel(sink_ref, q_ref, kp_ref, kc_ref, kn_ref, vp_ref, vc_ref, vn_ref, o_ref, *, nb):
    n = pl.program_id(1)
    blk = ATTN_BLOCK
    rows = Q_PER_KV * blk
    q = q_ref[0] * (HEAD_DIM ** -0.5)
    kwin = jnp.concatenate([kp_ref[0], kc_ref[0], kn_ref[0]], axis=0)
    vwin = jnp.concatenate([vp_ref[0], vc_ref[0], vn_ref[0]], axis=0)
    qi = lax.broadcasted_iota(jnp.int32, (rows, 3 * blk), 0) & (blk - 1)
    kj = lax.broadcasted_iota(jnp.int32, (rows, 3 * blk), 1)
    dist = kj - qi
    valid = ((dist >= blk - WINDOW) & (dist <= blk + WINDOW)
             & ((kj >= blk) | (n > 0)) & ((kj < 2 * blk) | (n < nb - 1)))
    rowg = lax.broadcasted_iota(jnp.int32, (rows, 1), 0) // blk
    outs = []
    for h in range(N_KV_HEADS):
        qs = jnp.concatenate(
            [q[:, (Q_PER_KV * h + g) * HEAD_DIM:(Q_PER_KV * h + g + 1) * HEAD_DIM]
             for g in range(Q_PER_KV)], axis=0)
        kh = kwin[:, h * HEAD_DIM:(h + 1) * HEAD_DIM]
        vh = vwin[:, h * HEAD_DIM:(h + 1) * HEAD_DIM]
        logits = _dot_nt(qs.astype(BF16), kh.astype(BF16))
        logits = jnp.where(valid, logits, -1e30)
        sk = jnp.zeros((rows, 1), F32)
        for g in range(Q_PER_KV):
            sk = jnp.where(rowg == g, sink_ref[Q_PER_KV * h + g], sk)
        m = jnp.maximum(jnp.max(logits, -1, keepdims=True), sk)
        e = jnp.exp(logits - m)
        den = jnp.sum(e, -1, keepdims=True) + jnp.exp(sk - m)
        p = e / den
        o = _dot(p.astype(BF16), vh.astype(BF16))
        for g in range(Q_PER_KV):
            outs.append(o[g * blk:(g + 1) * blk])
    o_ref[0] = jnp.concatenate(outs, axis=1)


def _attention(qkv3, sink):
    b, s, _ = qkv3.shape
    blk = ATTN_BLOCK
    nb = s // blk
    kcol = ATTN_WIDTH // LANES
    vcol = kcol + 1

    def spec(col, shift):
        def imap(bi, n):
            return (bi, jnp.clip(n + shift, 0, nb - 1), col)
        return pl.BlockSpec((1, blk, LANES), imap)

    return pl.pallas_call(
        functools.partial(_attn_kernel, nb=nb),
        grid=(b, nb),
        in_specs=[
            pl.BlockSpec(memory_space=pltpu.SMEM),
            pl.BlockSpec((1, blk, ATTN_WIDTH), lambda bi, n: (bi, n, 0)),
            spec(kcol, -1), spec(kcol, 0), spec(kcol, 1),
            spec(vcol, -1), spec(vcol, 0), spec(vcol, 1),
        ],
        out_specs=pl.BlockSpec((1, blk, ATTN_WIDTH), lambda bi, n: (bi, n, 0)),
        out_shape=jax.ShapeDtypeStruct((b, s, ATTN_WIDTH), F32),
        compiler_params=_cparams(("parallel", "parallel")),
        name="attn",
    )(sink, qkv3, qkv3, qkv3, qkv3, qkv3, qkv3, qkv3)


def _softplus(x):
    return jnp.maximum(x, 0.0) + jnp.log(1.0 + jnp.exp(-jnp.abs(x)))


def _sigmoid(x):
    return 1.0 / (1.0 + jnp.exp(-x))


def _prep_kernel(p_ref, hp_ref, hn_ref, mup_ref, mun_ref, wmix_ref, w0_ref, a0_ref, g2_ref,
                 kk_ref, ka_ref, bd_ref,
                 r_o, v_o, kk_o, g_o, lw0_o, lw1_o, kd0_o, kd1_o, b0_o, b1_o, *, tm, seq):
    i = pl.program_id(0)
    row = lax.broadcasted_iota(jnp.int32, (tm, 1), 0)
    seq_start = (i * tm) % seq == 0
    seq_end = ((i + 1) * tm) % seq == 0

    def shifted(c0, c1):
        p = p_ref[:, c0:c1]
        prev_row = jnp.where(seq_start, 0.0, hp_ref[SUBLANES - 1:SUBLANES, c0:c1])
        next_row = jnp.where(seq_end, 0.0, hn_ref[0:1, c0:c1])
        p_prev = jnp.where(row == 0, prev_row, pltpu.roll(p, 1, 0))
        p_next = jnp.where(row == tm - 1, next_row, pltpu.roll(p, tm - 1, 0))
        return p + mup_ref[:, c0:c1] * (p_prev - p) + mun_ref[:, c0:c1] * (p_next - p)

    w = RWKV_WIDTH
    r = shifted(0, w)
    k = shifted(w, 2 * w)
    v = shifted(2 * w, 3 * w)
    codes = shifted(3 * w, 3 * w + LANES)
    gd = shifted(3 * w + LANES, 3 * w + 2 * LANES)
    r_o[...] = r
    v_o[...] = v

    lane = lax.broadcasted_iota(jnp.int32, codes.shape, 1)
    codes = jnp.where(lane < 2 * DECAY_RANK, jnp.tanh(codes), codes)
    mm = _dot(codes, wmix_ref[...], HI)
    g_o[...] = _dot(_sigmoid(gd), g2_ref[...], HI)

    kkv = k * kk_ref[...]
    bd = bd_ref[...].astype(BF16)
    ss = sum(_dot(part, bd) for part in _split3(kkv * kkv))
    kkn = kkv * lax.rsqrt(jnp.maximum(ss, 1e-24))
    kk_o[...] = kkn

    ka = ka_ref[...]
    for d, (lw_o, kd_o, b_o) in enumerate(((lw0_o, kd0_o, b0_o), (lw1_o, kd1_o, b1_o))):
        w_log = -_softplus(-(w0_ref[d:d + 1, :] + mm[:, d * w:(d + 1) * w])) - 0.5
        lw_o[...] = -jnp.exp(w_log)
        a = _sigmoid(a0_ref[d:d + 1, :] + mm[:, (2 + d) * w:(3 + d) * w])
        kd_o[...] = k * (1.0 + (a - 1.0) * ka)
        b_o[...] = kkn * a


def _prep(pr, mup, mun, wmix, w0, a0, g2p, k_k, k_a, bd, seq, tm=256):
    t = pr.shape[0]
    nblk8 = t // SUBLANES
    per = tm // SUBLANES
    full = lambda a: pl.BlockSpec(a.shape, lambda i: (0,) * a.ndim)
    out = pl.BlockSpec((tm, RWKV_WIDTH), lambda i: (i, 0))
    return pl.pallas_call(
        functools.partial(_prep_kernel, tm=tm, seq=seq),
        grid=(t // tm,),
        in_specs=[
            pl.BlockSpec((tm, RWKV_COLS_PAD), lambda i: (i, 0)),
            pl.BlockSpec((SUBLANES, RWKV_COLS_PAD), lambda i: (jnp.maximum(i * per - 1, 0), 0)),
            pl.BlockSpec((SUBLANES, RWKV_COLS_PAD),
                         lambda i: (jnp.minimum((i + 1) * per, nblk8 - 1), 0)),
            full(mup), full(mun), full(wmix), full(w0), full(a0), full(g2p),
            full(k_k), full(k_a), full(bd),
        ],
        out_specs=[out] * 10,
        out_shape=[jax.ShapeDtypeStruct((t, RWKV_WIDTH), F32)] * 10,
        compiler_params=_cparams(("parallel",)),
        name="prep",
    )(pr, pr, pr, mup, mun, wmix, w0, a0, g2p, k_k, k_a, bd)


def _split3(x):
    hi = x.astype(BF16)
    r1 = x - hi.astype(F32)
    mid = r1.astype(BF16)
    lo = (r1 - mid.astype(F32)).astype(BF16)
    return hi, mid, lo


def _wkv_kernel(rf, vf, kkf, lwf, kdf, bf, rb, vb, kkb, lwb, kdb, bb, yf_ref, yb_ref, state):
    c = pl.program_id(1)
    n = WKV_CHUNK
    hd = HEAD_DIM
    nh = N_RWKV_HEADS
    def bmm(a, b, ca, cb):
        return lax.dot_general(a.astype(BF16), b.astype(BF16), (((ca,), (cb,)), ((0,), (0,))),
                               preferred_element_type=F32)

    @pl.when(c == 0)
    def _():
        state[...] = jnp.zeros_like(state)

    ti = lax.broadcasted_iota(jnp.int32, (n, n), 0)
    si = lax.broadcasted_iota(jnp.int32, (n, n), 1)
    dirs = ((rf, vf, kkf, lwf, kdf, bf, si <= ti, si < ti, n - 1),
            (rb, vb, kkb, lwb, kdb, bb, si >= ti, si > ti, 0))
    heads = lambda x: jnp.stack([x[:, j * hd:(j + 1) * hd] for j in range(nh)], 0)
    parts = []
    for r_ref, v_ref, kk_ref, lw_ref, kd_ref, b_ref, incl, strict, last in dirs:
        lw = lw_ref[0]
        tri = incl.astype(BF16)
        cum = sum(_dot(tri, part) for part in _split3(lw))
        e_neg = jnp.exp(-cum)
        e_last = jnp.exp(cum[last:last + 1, :] - cum)
        g_scale = jnp.exp(cum[last:last + 1, :])
        parts.append(dict(
            at=heads((-kk_ref[0] * jnp.exp(cum - lw)).astype(BF16)),
            rt=heads((r_ref[0] * jnp.exp(cum)).astype(BF16)),
            bt=heads((b_ref[0] * e_neg).astype(BF16)),
            kt=heads((kd_ref[0] * e_neg).astype(BF16)),
            bl=heads((b_ref[0] * e_last).astype(BF16)),
            kl=heads((kd_ref[0] * e_last).astype(BF16)),
            v=heads(v_ref[0].astype(BF16)),
            gs=heads(g_scale),
            incl=jnp.broadcast_to(incl[None], (nh, n, n)),
            strict=jnp.broadcast_to(strict[None], (nh, n, n))))
    cat = lambda key: jnp.concatenate([parts[0][key], parts[1][key]], 0)
    at, rt, bt, kt, bl, kl, v, gs = (cat(k) for k in ("at", "rt", "bt", "kt", "bl", "kl", "v", "gs"))
    incl, strict = cat("incl"), cat("strict")
    g0 = state[...]
    g0b = g0.astype(BF16)
    m1 = bmm(jnp.concatenate([at, rt], 1), jnp.concatenate([bt, kt], 1), 2, 2)
    a_ab = jnp.where(strict, m1[:, :n, :n], 0.0)
    a_ak = jnp.where(strict, m1[:, :n, n:], 0.0)
    a_rb = jnp.where(incl, m1[:, n:, :n], 0.0)
    a_rk = jnp.where(incl, m1[:, n:, n:], 0.0)
    tinv = jnp.where((ti == si)[None], 1.0, a_ab)
    pw = a_ab
    for _ in range(int(np.log2(n)) - 1):
        pw = bmm(pw, pw, 2, 1)
        tinv = tinv + bmm(tinv, pw, 2, 1)
    rhs = bmm(a_ak, v, 2, 1) + bmm(at, g0b, 2, 2)
    u = bmm(tinv, rhs, 2, 1)
    uv = jnp.concatenate([u.astype(BF16), v], 1)
    y = bmm(jnp.concatenate([a_rb, a_rk], 2), uv, 2, 1) + bmm(rt, g0b, 2, 2)
    state[...] = g0 * gs + bmm(uv, jnp.concatenate([bl, kl], 1), 1, 1)
    for d, y_ref in enumerate((yf_ref, yb_ref)):
        y_ref[0] = jnp.concatenate([y[d * nh + j] for j in range(nh)], axis=1)


def _wkv(r, v, kk, lw0, lw1, kd0, kd1, b0, b1, batch, seq):
    n = WKV_CHUNK
    nc = seq // n
    shp = (batch, seq, RWKV_WIDTH)
    arrs = [a.reshape(shp) for a in (r, v, kk, lw0, kd0, b0, r, v, kk, lw1, kd1, b1)]
    fwd = pl.BlockSpec((1, n, RWKV_WIDTH), lambda b, c: (b, c, 0))
    bwd = pl.BlockSpec((1, n, RWKV_WIDTH), lambda b, c: (b, nc - 1 - c, 0))
    yf, yb = pl.pallas_call(
        _wkv_kernel,
        grid=(batch, nc),
        in_specs=[fwd] * 6 + [bwd] * 6,
        out_specs=[fwd, bwd],
        out_shape=[jax.ShapeDtypeStruct(shp, F32)] * 2,
        scratch_shapes=[pltpu.VMEM((2 * N_RWKV_HEADS, HEAD_DIM, HEAD_DIM), F32)],
        compiler_params=_cparams(("parallel", "arbitrary")),
        name="wkv",
    )(*arrs)
    return yf.reshape(batch * seq, RWKV_WIDTH), yb.reshape(batch * seq, RWKV_WIDTH)


def _mix_kernel(x_ref, ya_ref, yf_ref, yb_ref, r_ref, v_ref, kd0_ref, kd1_ref, g_ref,
                lng_ref, lnb_ref, rk_ref, bd_ref, wout_ref, l1g_ref, l1b_ref, h_ref):
    bd = bd_ref[...].astype(BF16)
    head_sum = lambda t: sum(_dot(part, bd) for part in _split3(t))
    inv = 1.0 / HEAD_DIM
    y = yf_ref[...] + yb_ref[...]
    mu = head_sum(y) * inv
    yc = y - mu
    var = head_sum(yc * yc) * inv
    yn = yc * lax.rsqrt(var + RWKV_GN_EPS) * lng_ref[...] + lnb_ref[...]
    k_mean = 0.5 * (kd0_ref[...] + kd1_ref[...])
    v = v_ref[...]
    bonus = head_sum(r_ref[...] * k_mean * rk_ref[...]) * v
    yr = (yn + bonus) * g_ref[...]
    mix = (_dot(ya_ref[...].astype(BF16), wout_ref[:ATTN_WIDTH, :])
           + _dot(yr.astype(BF16), wout_ref[ATTN_WIDTH:, :]))
    h_ref[...] = _layer_norm(DEEPNORM_ALPHA * x_ref[...] + mix, l1g_ref[...], l1b_ref[...])


def _mix(x2, ya, yf, yb, r, v, kd0, kd1, g, lng, lnb, rk, bd, wout, l1g, l1b, tm=256):
    t, d = x2.shape
    full = lambda a: pl.BlockSpec(a.shape, lambda i: (0,) * a.ndim)
    half = pl.BlockSpec((tm, RWKV_WIDTH), lambda i: (i, 0))
    wide = pl.BlockSpec((tm, d), lambda i: (i, 0))
    return pl.pallas_call(
        _mix_kernel,
        grid=(t // tm,),
        in_specs=[wide] + [half] * 8 + [full(lng), full(lnb), full(rk), full(bd), full(wout),
                                        full(l1g), full(l1b)],
        out_specs=wide,
        out_shape=jax.ShapeDtypeStruct((t, d), F32),
        compiler_params=_cparams(("parallel",)),
        name="mix",
    )(x2, ya, yf, yb, r, v, kd0, kd1, g, lng, lnb, rk, bd, wout, l1g, l1b)


def _top_rows(scs, k, payloads=None):
    n = scs[0].shape[0]
    iota = lax.broadcasted_iota(jnp.int32, scs[0].shape, 0).astype(F32)
    scs = list(scs)
    vals = [[] for _ in scs]
    picks = [[] for _ in scs]
    for _ in range(k):
        for a, sc in enumerate(scs):
            m = jnp.max(sc, axis=0, keepdims=True)
            pos = jnp.min(jnp.where(sc == m, iota, float(n)), axis=0, keepdims=True)
            hit = iota == pos
            vals[a].append(m)
            if payloads is None:
                picks[a].append(pos)
            else:
                picks[a].append(jnp.max(jnp.where(hit, payloads[a], -1.0), axis=0, keepdims=True))
            scs[a] = jnp.where(hit, -jnp.inf, sc)
    return [(jnp.concatenate(v, 0), jnp.concatenate(p, 0)) for v, p in zip(vals, picks)]


def _route_kernel(h_ref, wq_ref, keys_ref, idx_ref, gate_ref, gate_tok_ref, q_scr, idx_scr):
    k = PEER_TOPK
    q_scr[...] = _dot(h_ref[...].astype(BF16), wq_ref[...])

    def head_group(hg, carry):
        heads = [hg * ROUTE_HEADS_PER_STEP + i for i in range(ROUTE_HEADS_PER_STEP)]
        scores = []
        for hh in heads:
            for p in range(2):
                col = pl.multiple_of(hh * PEER_QDIM + p * PEER_HALF, PEER_HALF)
                qp = q_scr[:, pl.ds(col, PEER_HALF)]
                scores.append(_dot_nt(keys_ref[hh, p], qp, HI))
        tops = _top_rows(scores, k)
        cands, cand_ids = [], []
        for a in range(len(heads)):
            (s1, i1), (s2, i2) = tops[2 * a], tops[2 * a + 1]
            cands.append(jnp.concatenate([s1[i:i + 1] + s2[:k // (i + 1)] for i in range(k)], 0))
            cand_ids.append(jnp.concatenate(
                [i1[i:i + 1] * PEER_NKEYS + i2[:k // (i + 1)] for i in range(k)], 0))
        for hh, (cs, ids) in zip(heads, _top_rows(cands, k, cand_ids)):
            e = jnp.exp(cs - jnp.max(cs, axis=0, keepdims=True))
            row = pl.multiple_of(hh * k, k)
            gate_ref[pl.ds(row, k), :] = e / jnp.sum(e, axis=0, keepdims=True)
            idx_scr[pl.ds(row, k), :] = ids.astype(jnp.int32)
        return carry

    lax.fori_loop(0, PEER_HEADS // ROUTE_HEADS_PER_STEP, head_group, 0)
    idx_ref[...] = idx_scr[...].T
    gate_tok_ref[...] = gate_ref[...].T


def _route(h1, wq, keys, tm=256):
    t, d = h1.shape
    return pl.pallas_call(
        _route_kernel,
        grid=(t // tm,),
        in_specs=[
            pl.BlockSpec((tm, d), lambda i: (i, 0)),
            pl.BlockSpec(wq.shape, lambda i: (0, 0)),
            pl.BlockSpec(keys.shape, lambda i: (0, 0, 0, 0)),
        ],
        out_specs=[
            pl.BlockSpec((tm, N_SEL), lambda i: (i, 0)),
            pl.BlockSpec((N_SEL, tm), lambda i: (0, i)),
            pl.BlockSpec((tm, N_SEL), lambda i: (i, 0)),
        ],
        out_shape=[
            jax.ShapeDtypeStruct((t, N_SEL), jnp.int32),
            jax.ShapeDtypeStruct((N_SEL, t), F32),
            jax.ShapeDtypeStruct((t, N_SEL), F32),
        ],
        scratch_shapes=[pltpu.VMEM((tm, PEER_HEADS * PEER_QDIM), F32),
                        pltpu.VMEM((N_SEL, tm), jnp.int32)],
        compiler_params=_cparams(("parallel",)),
        name="route",
    )(h1, wq, keys)


def _peer_kernel(idx_ref, gate_ref, h_ref, uv_hbm, l2g_ref, l2b_ref, o_ref, buf, f_scr, sem, *, tb):
    d = h_ref.shape[1]

    def issue(t, slot):
        for j in range(N_SEL):
            pltpu.make_async_copy(uv_hbm.at[pl.ds(idx_ref[t, j], 1)], buf.at[slot, pl.ds(j, 1)],
                                  sem.at[slot]).start()

    def wait_all(slot):
        pltpu.make_async_copy(uv_hbm.at[pl.ds(0, N_SEL)], buf.at[slot], sem.at[slot]).wait()

    issue(0, 0)
    lane = lax.broadcasted_iota(jnp.int32, (N_SEL, tb), 1)

    def body(t, carry):
        slot = t % 2

        @pl.when(t + 1 < tb)
        def _():
            issue(t + 1, 1 - slot)

        wait_all(slot)
        z = jnp.sum(buf[slot, :, 0:d] * h_ref[pl.ds(t, 1), :], axis=1, keepdims=True)
        gcol = jnp.sum(jnp.where(lane == t, gate_ref[...], 0.0), axis=1, keepdims=True)
        coef = gcol * (0.5 * z * (1.0 + lax.erf(z * (2.0 ** -0.5))))
        f_scr[pl.ds(t, 1), :] = jnp.sum(coef * buf[slot, :, d:2 * d], axis=0, keepdims=True)
        return carry

    lax.fori_loop(0, tb, body, 0)
    o_ref[...] = _layer_norm(DEEPNORM_ALPHA * h_ref[...] + f_scr[...], l2g_ref[...], l2b_ref[...])


def _peer(idx, gate_t, h1, uv, l2g, l2b, tok0, ntok, tb=PEER_TB):
    d = h1.shape[1]
    b0 = tok0 // tb
    return pl.pallas_call(
        functools.partial(_peer_kernel, tb=tb),
        grid=(ntok // tb,),
        in_specs=[
            pl.BlockSpec((tb, N_SEL), lambda i: (b0 + i, 0), memory_space=pltpu.SMEM),
            pl.BlockSpec((N_SEL, tb), lambda i: (0, b0 + i)),
            pl.BlockSpec((tb, d), lambda i: (b0 + i, 0)),
            pl.BlockSpec(memory_space=pl.ANY),
            pl.BlockSpec(l2g.shape, lambda i: (0, 0)),
            pl.BlockSpec(l2b.shape, lambda i: (0, 0)),
        ],
        out_specs=pl.BlockSpec((tb, d), lambda i: (i, 0)),
        out_shape=jax.ShapeDtypeStruct((ntok, d), F32),
        scratch_shapes=[pltpu.VMEM((2, N_SEL, 2 * d), F32),
                        pltpu.VMEM((tb, d), F32),
                        pltpu.SemaphoreType.DMA((2,))],
        compiler_params=_cparams(("arbitrary",)),
        name="peer",
    )(idx, gate_t, h1, uv, l2g, l2b)


def _sc_mesh():
    return plsc.VectorSubcoreMesh(core_axis_name="c", subcore_axis_name="s")


def _worker_id():
    return lax.axis_index("s") * 2 + lax.axis_index("c")


def _sc_z_body(u_hbm, idx_hbm, h_hbm, z_hbm, idx_v, h_v, rows_v, z_v, sem, *, tpw, d, tok0):
    ngrp = N_SEL // SC_GROUP
    nchunk = d // SC_LANES
    nsteps = tpw * ngrp
    base = _worker_id() * tpw

    def gather(tok_buf, g, buf):
        return pltpu.make_async_copy(u_hbm.at[idx_v.at[tok_buf, g]], rows_v.at[buf], sem.at[buf])

    pltpu.sync_copy(idx_hbm.at[tok0 + base], idx_v.at[0])
    gather(0, 0, 0).start()

    @pl.loop(0, nsteps)
    def _(q):
        t_loc = q // ngrp
        g = q % ngrp
        buf = q % 2
        t = base + t_loc

        @pl.when(g == 0)
        def _():
            pltpu.sync_copy(h_hbm.at[tok0 + t], h_v)

            @pl.loop(0, N_SEL)
            def _(r):
                z_v[pl.ds(pl.multiple_of(r * SC_LANES, SC_LANES), SC_LANES)] = jnp.zeros((SC_LANES,), F32)

        @pl.when(q + 1 < nsteps)
        def _():
            tn = (q + 1) // ngrp
            gn = (q + 1) % ngrp

            @pl.when(gn == 0)
            def _():
                pltpu.sync_copy(idx_hbm.at[tok0 + base + tn], idx_v.at[tn % 2])

            gather(tn % 2, gn, 1 - buf).start()

        gather(t_loc % 2, g, buf).wait()

        @pl.loop(0, SC_GROUP, step=4)
        def _(rb):
            @pl.loop(0, nchunk, step=SC_CHUNK_UNROLL)
            def _(c0):
                accs = [jnp.zeros((SC_LANES,), F32) for _ in range(4)]
                for cc in range(SC_CHUNK_UNROLL):
                    sl = pl.ds(pl.multiple_of((c0 + cc) * SC_LANES, SC_LANES), SC_LANES)
                    hc = h_v[sl]
                    for i in range(4):
                        accs[i] = accs[i] + rows_v[buf, rb + i, sl] * hc
                for i in range(4):
                    row = pl.multiple_of((g * SC_GROUP + rb + i) * SC_LANES, SC_LANES)
                    plsc.addupdate(z_v.at[pl.ds(row, SC_LANES)], accs[i])

        @pl.when(g == ngrp - 1)
        def _():
            pltpu.sync_copy(z_v, z_hbm.at[t])


def _sc_z(u, idx, h, tok0, ntok):
    t, d = h.shape
    tpw = ntok // SC_WORKERS
    idx4 = idx.reshape(t, N_SEL // SC_GROUP, SC_GROUP)
    body = functools.partial(_sc_z_body, tpw=tpw, d=d, tok0=tok0)
    return pl.kernel(
        body,
        out_type=jax.ShapeDtypeStruct((ntok, N_SEL * SC_LANES), F32),
        mesh=_sc_mesh(),
        scratch_types=[
            pltpu.VMEM((2, N_SEL // SC_GROUP, SC_GROUP), jnp.int32),
            pltpu.VMEM((d,), F32),
            pltpu.VMEM((2, SC_GROUP, d), F32),
            pltpu.VMEM((N_SEL * SC_LANES,), F32),
            pltpu.SemaphoreType.DMA((2,)),
        ],
        name="sc_z",
    )(u, idx4, h)


def _sc_out_body(v_hbm, idx_hbm, coef_hbm, f_hbm, idx_v, coef_v, rows_v, out_v, sem, *, tpw, d, tok0):
    ngrp = N_SEL // SC_GROUP
    nchunk = d // SC_LANES
    nsteps = tpw * ngrp
    base = _worker_id() * tpw

    def gather(tok_buf, g, buf):
        return pltpu.make_async_copy(v_hbm.at[idx_v.at[tok_buf, g]], rows_v.at[buf], sem.at[buf])

    pltpu.sync_copy(idx_hbm.at[tok0 + base], idx_v.at[0])
    gather(0, 0, 0).start()

    @pl.loop(0, nsteps)
    def _(q):
        t_loc = q // ngrp
        g = q % ngrp
        buf = q % 2
        t = base + t_loc

        @pl.when(g == 0)
        def _():
            pltpu.sync_copy(coef_hbm.at[t], coef_v)

        @pl.when(q + 1 < nsteps)
        def _():
            tn = (q + 1) // ngrp
            gn = (q + 1) % ngrp

            @pl.when(gn == 0)
            def _():
                pltpu.sync_copy(idx_hbm.at[tok0 + base + tn], idx_v.at[tn % 2])

            gather(tn % 2, gn, 1 - buf).start()

        gather(t_loc % 2, g, buf).wait()

        @pl.loop(0, nchunk, step=SC_OUT_CHUNKS)
        def _(c0):
            sls = [pl.ds(pl.multiple_of((c0 + cc) * SC_LANES, SC_LANES), SC_LANES)
                   for cc in range(SC_OUT_CHUNKS)]
            accs = [None] * SC_OUT_CHUNKS
            for r in range(SC_GROUP):
                cf = coef_v[pl.ds(pl.multiple_of((g * SC_GROUP + r) * SC_LANES, SC_LANES), SC_LANES)]
                for cc in range(SC_OUT_CHUNKS):
                    p = rows_v[buf, r, sls[cc]] * cf
                    accs[cc] = p if accs[cc] is None else accs[cc] + p
            for cc in range(SC_OUT_CHUNKS):
                @pl.when(g == 0)
                def _():
                    out_v[sls[cc]] = accs[cc]

                @pl.when(g != 0)
                def _():
                    plsc.addupdate(out_v.at[sls[cc]], accs[cc])

        @pl.when(g == ngrp - 1)
        def _():
            pltpu.sync_copy(out_v, f_hbm.at[t])


def _sc_out(v, idx, coef16, tok0):
    t = idx.shape[0]
    ntok = coef16.shape[0]
    d = v.shape[1]
    tpw = ntok // SC_WORKERS
    idx4 = idx.reshape(t, N_SEL // SC_GROUP, SC_GROUP)
    body = functools.partial(_sc_out_body, tpw=tpw, d=d, tok0=tok0)
    return pl.kernel(
        body,
        out_type=jax.ShapeDtypeStruct((ntok, d), F32),
        mesh=_sc_mesh(),
        scratch_types=[
            pltpu.VMEM((2, N_SEL // SC_GROUP, SC_GROUP), jnp.int32),
            pltpu.VMEM((N_SEL * SC_LANES,), F32),
            pltpu.VMEM((2, SC_GROUP, d), F32),
            pltpu.VMEM((d,), F32),
            pltpu.SemaphoreType.DMA((2,)),
        ],
        name="sc_out",
    )(v, idx4, coef16)


def _coef_kernel(zp_ref, gate_ref, sel_ref, selt_ref, after_ref, o_ref):
    z = sum(_dot(part, sel_ref[...]) for part in _split3(zp_ref[...]))
    coef = gate_ref[...] * (0.5 * z * (1.0 + lax.erf(z * (2.0 ** -0.5))))
    o_ref[...] = sum(_dot(part, selt_ref[...]) for part in _split3(coef))


def _coef(zp, gate_tok, tok0, after, tm=256):
    ntok, wide = zp.shape
    b0 = tok0 // tm
    grp = np.arange(wide) // SC_LANES
    sel = jnp.asarray(grp[:, None] == np.arange(N_SEL)[None, :], BF16)
    return pl.pallas_call(
        _coef_kernel,
        grid=(ntok // tm,),
        in_specs=[
            pl.BlockSpec((tm, wide), lambda i: (i, 0)),
            pl.BlockSpec((tm, N_SEL), lambda i: (b0 + i, 0)),
            pl.BlockSpec(sel.shape, lambda i: (0, 0)),
            pl.BlockSpec(sel.shape[::-1], lambda i: (0, 0)),
            pl.BlockSpec((SUBLANES, LANES), lambda i: (0, 0)),
        ],
        out_specs=pl.BlockSpec((tm, wide), lambda i: (i, 0)),
        out_shape=jax.ShapeDtypeStruct((ntok, wide), F32),
        compiler_params=_cparams(("parallel",)),
        name="coef",
    )(zp, gate_tok, sel, sel.T, after)


def _ln_out_kernel(h_ref, f_ref, g_ref, b_ref, o_ref):
    o_ref[...] = _layer_norm(DEEPNORM_ALPHA * h_ref[...] + f_ref[...], g_ref[...], b_ref[...])


def _ln_out(h1, f, l2g, l2b, tok0, tm=256):
    ntok, d = f.shape
    b0 = tok0 // tm
    return pl.pallas_call(
        _ln_out_kernel,
        grid=(ntok // tm,),
        in_specs=[
            pl.BlockSpec((tm, d), lambda i: (b0 + i, 0)),
            pl.BlockSpec((tm, d), lambda i: (i, 0)),
            pl.BlockSpec(l2g.shape, lambda i: (0, 0)),
            pl.BlockSpec(l2b.shape, lambda i: (0, 0)),
        ],
        out_specs=pl.BlockSpec((tm, d), lambda i: (i, 0)),
        out_shape=jax.ShapeDtypeStruct((ntok, d), F32),
        compiler_params=_cparams(("parallel",)),
        name="ln_out",
    )(h1, f, l2g, l2b)


def _rope_tables(seq):
    pos = jnp.arange(seq, dtype=F32)
    inv_freq = ROPE_THETA ** (-jnp.arange(0, HEAD_DIM, 2, dtype=F32) / HEAD_DIM)
    ang = pos[:, None] * inv_freq[None, :]
    ang = jnp.concatenate([ang, ang], -1)
    sign = jnp.concatenate([-jnp.ones((HEAD_DIM // 2,), F32), jnp.ones((HEAD_DIM // 2,), F32)])
    reps = LANES // HEAD_DIM
    return jnp.tile(jnp.cos(ang), (1, reps)), jnp.tile(jnp.sin(ang) * sign, (1, reps))


def _layer(h2d, batch, seq, w_in, mu_prev, mu_next, w0, w2, a0, a2, g2, k_k, k_a, r_k, lnx_g, lnx_b,
           sink, w_out, ln1_g, ln1_b, peer_wq, peer_keys, peer_u, peer_v, ln2_g, ln2_b):
    w = RWKV_WIDTH
    row = lambda a: a.reshape(1, -1).astype(F32)
    w_in_p = jnp.pad(w_in, ((0, 0), (0, RWKV_COLS_PAD - RWKV_COLS))).astype(BF16)
    mup = jnp.pad(mu_prev, (0, RWKV_COLS_PAD - RWKV_COLS)).reshape(1, -1)
    mun = jnp.pad(mu_next, (0, RWKV_COLS_PAD - RWKV_COLS)).reshape(1, -1)
    wmix = jnp.zeros((LANES, 4 * w), F32)
    for d in range(2):
        wmix = wmix.at[d * DECAY_RANK:(d + 1) * DECAY_RANK, d * w:(d + 1) * w].set(w2[d])
        r0 = 2 * DECAY_RANK + d * ICLR_RANK
        wmix = wmix.at[r0:r0 + ICLR_RANK, (2 + d) * w:(3 + d) * w].set(a2[d])
    g2p = jnp.pad(g2, ((0, LANES - GATE_RANK), (0, 0)))
    head_of = np.arange(w) // HEAD_DIM
    bd = jnp.asarray(head_of[:, None] == head_of[None, :], F32)
    cos_t, sin_t = _rope_tables(seq)

    wq_b, keys_f, wout_b = peer_wq.astype(BF16), peer_keys.astype(F32), w_out.astype(BF16)

    def to_routing(x2d, nb):
        qkv, pr = _proj(x2d, w_in_p, cos_t, sin_t, seq)
        y_attn = _attention(qkv.reshape(nb, seq, ATTN_COLS), sink.astype(F32))
        r, v, kk, g, lw0, lw1, kd0, kd1, b0, b1 = _prep(
            pr, mup, mun, wmix, w0.astype(F32), a0.astype(F32), g2p, row(k_k), row(k_a), bd, seq)
        yf, yb = _wkv(r, v, kk, lw0, lw1, kd0, kd1, b0, b1, nb, seq)
        h1 = _mix(x2d, y_attn.reshape(nb * seq, ATTN_WIDTH), yf, yb, r, v, kd0, kd1, g,
                  row(lnx_g), row(lnx_b), row(r_k), bd, wout_b, row(ln1_g), row(ln1_b))
        return (h1,) + tuple(_route(h1, wq_b, keys_f))

    uv = jnp.concatenate([peer_u, peer_v], axis=1)
    l2g, l2b = row(ln2_g), row(ln2_b)

    nb_sc = batch * PEER_SC_SEQ_SHARE_8THS // 8
    t_sc = nb_sc * seq
    t_tc = (batch - nb_sc) * seq
    n_a = (t_tc * PEER_TC_FIRST_CALL_32NDS // 32) // PEER_TB * PEER_TB
    n_b = t_tc - n_a
    assert 0 < nb_sc < batch and t_sc % (SC_WORKERS * SUBLANES) == 0 and n_a > 0 and n_b % PEER_TB == 0
    h1s, idxs, _, gate_tok_s = to_routing(h2d[:t_sc], nb_sc)
    zp = _sc_z(peer_u.astype(F32), idxs, h1s, 0, t_sc)
    h1t, idxt, gate_t_t, _ = to_routing(h2d[t_sc:], batch - nb_sc)
    out_a = _peer(idxt, gate_t_t, h1t, uv, l2g, l2b, 0, n_a)
    f_sc = _sc_out(peer_v.astype(F32), idxs, _coef(zp, gate_tok_s, 0, out_a), 0)
    out_b = _peer(idxt, gate_t_t, h1t, uv, l2g, l2b, n_a, n_b)
    out_sc = _ln_out(h1s, f_sc, l2g, l2b, 0)
    return jnp.concatenate([out_sc, out_a, out_b], axis=0)


def kernel(x, w_in, mu_prev, mu_next, w0, w2, a0, a2, g2, k_k, k_a, r_k, lnx_g, lnx_b, sink, w_out,
           ln1_g, ln1_b, peer_wq, peer_keys, peer_u, peer_v, ln2_g, ln2_b):
    batch, seq, d = x.shape
    h = x.reshape(batch * seq, d)
    for l in range(DEPTH):
        h = _layer(h, batch, seq, w_in[l], mu_prev[l], mu_next[l], w0[l], w2[l], a0[l], a2[l], g2[l],
                   k_k[l], k_a[l], r_k[l], lnx_g[l], lnx_b[l], sink[l], w_out[l], ln1_g[l], ln1_b[l],
                   peer_wq[l], peer_keys[l], peer_u[l], peer_v[l], ln2_g[l], ln2_b[l])
    return h.reshape(batch, seq, d)
```

```python
import functools

import numpy as np
import jax
import jax.numpy as jnp
from jax import lax
from jax.experimental import pallas as pl
from jax.experimental.pallas import tpu as pltpu
from jax.experimental.pallas import tpu_sc as plsc

F32 = jnp.float32
BF16 = jnp.bfloat16
HI = lax.Precision.HIGHEST

HEAD_DIM = 64
N_Q_HEADS = 8
N_KV_HEADS = 2
Q_PER_KV = N_Q_HEADS // N_KV_HEADS
ATTN_WIDTH = N_Q_HEADS * HEAD_DIM
ATTN_KV_WIDTH = N_KV_HEADS * HEAD_DIM
ATTN_COLS = ATTN_WIDTH + 2 * ATTN_KV_WIDTH
WINDOW = 128
ATTN_BLOCK = 128
ROPE_THETA = 10000.0
N_RWKV_HEADS = 8
RWKV_WIDTH = N_RWKV_HEADS * HEAD_DIM
DECAY_RANK = 32
ICLR_RANK = 32
GATE_RANK = 96
RWKV_COLS = 3 * RWKV_WIDTH + 2 * DECAY_RANK + 2 * ICLR_RANK + GATE_RANK
RWKV_COLS_PAD = 1792
RWKV_GN_EPS = 64e-5
PEER_HEADS = 8
PEER_NKEYS = 128
PEER_QDIM = 256
PEER_HALF = PEER_QDIM // 2
PEER_TOPK = 16
N_SEL = PEER_HEADS * PEER_TOPK
LN_EPS = 1e-5
DEPTH = 1
DEEPNORM_ALPHA = (2.0 * DEPTH) ** 0.25

LANES = 128
SUBLANES = 8
WKV_CHUNK = 64
VMEM_LIMIT = 48 * 1024 * 1024
PEER_TB = 256
ROUTE_HEADS_PER_STEP = 4
SC_LANES = 16
SC_WORKERS = 32
SC_GROUP = 32
SC_CHUNK_UNROLL = 8
SC_OUT_CHUNKS = 4
PEER_SC_SEQ_SHARE_8THS = 5
PEER_TC_FIRST_CALL_32NDS = 13


def _cparams(sem):
    return pltpu.CompilerParams(dimension_semantics=sem, vmem_limit_bytes=VMEM_LIMIT)


def _dot(a, b, precision=None):
    return jnp.dot(a, b, preferred_element_type=F32, precision=precision)


def _dot_nt(a, b, precision=None):
    return lax.dot_general(a, b, (((1,), (1,)), ((), ())), preferred_element_type=F32,
                           precision=precision)


def _dot_tn(a, b, precision=None):
    return lax.dot_general(a, b, (((0,), (0,)), ((), ())), preferred_element_type=F32,
                           precision=precision)


def _layer_norm(z, g, b):
    mu = jnp.mean(z, -1, keepdims=True)
    zc = z - mu
    var = jnp.mean(zc * zc, -1, keepdims=True)
    return zc * lax.rsqrt(var + LN_EPS) * g + b


def _proj_kernel(x_ref, w_ref, cos_ref, sin_ref, qkv_ref, pr_ref):
    xb = x_ref[...].astype(BF16)
    cos = cos_ref[...]
    sin = sin_ref[...]
    lane = lax.broadcasted_iota(jnp.int32, cos.shape, 1)
    first_half = (lane & (HEAD_DIM // 2)) == 0

    def rope(t):
        rot = jnp.where(first_half, pltpu.roll(t, LANES - HEAD_DIM // 2, 1),
                        pltpu.roll(t, HEAD_DIM // 2, 1))
        return t * cos + rot * sin

    for c in range(0, ATTN_COLS, 2 * LANES):
        acc = _dot(xb, w_ref[:, c:c + 2 * LANES])
        for half in range(2):
            col = c + half * LANES
            t = acc[:, half * LANES:(half + 1) * LANES]
            if col < ATTN_WIDTH + ATTN_KV_WIDTH:
                t = rope(t)
            qkv_ref[:, col:col + LANES] = t
    for c in range(0, RWKV_COLS_PAD, 2 * LANES):
        pr_ref[:, c:c + 2 * LANES] = _dot(xb, w_ref[:, ATTN_COLS + c:ATTN_COLS + c + 2 * LANES])


def _proj(x2, w_in_p, cos_t, sin_t, seq, tm=512):
    t, d = x2.shape
    n_pos = seq // tm
    return pl.pallas_call(
        _proj_kernel,
        grid=(t // tm,),
        in_specs=[
            pl.BlockSpec((tm, d), lambda i: (i, 0)),
            pl.BlockSpec(w_in_p.shape, lambda i: (0, 0)),
            pl.BlockSpec((tm, LANES), lambda i: (i % n_pos, 0)),
            pl.BlockSpec((tm, LANES), lambda i: (i % n_pos, 0)),
        ],
        out_specs=[
            pl.BlockSpec((tm, ATTN_COLS), lambda i: (i, 0)),
            pl.BlockSpec((tm, RWKV_COLS_PAD), lambda i: (i, 0)),
        ],
        out_shape=[
            jax.ShapeDtypeStruct((t, ATTN_COLS), F32),
            jax.ShapeDtypeStruct((t, RWKV_COLS_PAD), F32),
        ],
        compiler_params=_cparams(("parallel",)),
        name="proj",
    )(x2, w_in_p, cos_t, sin_t)


def _attn_kernel(sink_ref, q_ref, kp_ref, kc_ref, kn_ref, vp_ref, vc_ref, vn_ref, o_ref, *, nb):
    n = pl.program_id(1)
    blk = ATTN_BLOCK
    rows = Q_PER_KV * blk
    q = q_ref[0] * (HEAD_DIM ** -0.5)
    kwin = jnp.concatenate([kp_ref[0], kc_ref[0], kn_ref[0]], axis=0)
    vwin = jnp.concatenate([vp_ref[0], vc_ref[0], vn_ref[0]], axis=0)
    qi = lax.broadcasted_iota(jnp.int32, (rows, 3 * blk), 0) & (blk - 1)
    kj = lax.broadcasted_iota(jnp.int32, (rows, 3 * blk), 1)
    dist = kj - qi
    valid = ((dist >= blk - WINDOW) & (dist <= blk + WINDOW)
             & ((kj >= blk) | (n > 0)) & ((kj < 2 * blk) | (n < nb - 1)))
    rowg = lax.broadcasted_iota(jnp.int32, (rows, 1), 0) // blk
    outs = []
    for h in range(N_KV_HEADS):
        qs = jnp.concatenate(
            [q[:, (Q_PER_KV * h + g) * HEAD_DIM:(Q_PER_KV * h + g + 1) * HEAD_DIM]
             for g in range(Q_PER_KV)], axis=0)
        kh = kwin[:, h * HEAD_DIM:(h + 1) * HEAD_DIM]
        vh = vwin[:, h * HEAD_DIM:(h + 1) * HEAD_DIM]
        logits = _dot_nt(qs.astype(BF16), kh.astype(BF16))
        logits = jnp.where(valid, logits, -1e30)
        sk = jnp.zeros((rows, 1), F32)
        for g in range(Q_PER_KV):
            sk = jnp.where(rowg == g, sink_ref[Q_PER_KV * h + g], sk)
        m = jnp.maximum(jnp.max(logits, -1, keepdims=True), sk)
        e = jnp.exp(logits - m)
        den = jnp.sum(e, -1, keepdims=True) + jnp.exp(sk - m)
        p = e / den
        o = _dot(p.astype(BF16), vh.astype(BF16))
        for g in range(Q_PER_KV):
            outs.append(o[g * blk:(g + 1) * blk])
    o_ref[0] = jnp.concatenate(outs, axis=1)


def _attention(qkv3, sink):
    b, s, _ = qkv3.shape
    blk = ATTN_BLOCK
    nb = s // blk
    kcol = ATTN_WIDTH // LANES
    vcol = kcol + 1

    def spec(col, shift):
        def imap(bi, n):
            return (bi, jnp.clip(n + shift, 0, nb - 1), col)
        return pl.BlockSpec((1, blk, LANES), imap)

    return pl.pallas_call(
        functools.partial(_attn_kernel, nb=nb),
        grid=(b, nb),
        in_specs=[
            pl.BlockSpec(memory_space=pltpu.SMEM),
            pl.BlockSpec((1, blk, ATTN_WIDTH), lambda bi, n: (bi, n, 0)),
            spec(kcol, -1), spec(kcol, 0), spec(kcol, 1),
            spec(vcol, -1), spec(vcol, 0), spec(vcol, 1),
        ],
        out_specs=pl.BlockSpec((1, blk, ATTN_WIDTH), lambda bi, n: (bi, n, 0)),
        out_shape=jax.ShapeDtypeStruct((b, s, ATTN_WIDTH), F32),
        compiler_params=_cparams(("parallel", "parallel")),
        name="attn",
    )(sink, qkv3, qkv3, qkv3, qkv3, qkv3, qkv3, qkv3)


def _softplus(x):
    return jnp.maximum(x, 0.0) + jnp.log(1.0 + jnp.exp(-jnp.abs(x)))


def _sigmoid(x):
    return 1.0 / (1.0 + jnp.exp(-x))


def _prep_kernel(p_ref, hp_ref, hn_ref, mup_ref, mun_ref, wmix_ref, w0_ref, a0_ref, g2_ref,
                 kk_ref, ka_ref, bd_ref,
                 r_o, v_o, kk_o, g_o, lw0_o, lw1_o, kd0_o, kd1_o, b0_o, b1_o, *, tm, seq):
    i = pl.program_id(0)
    row = lax.broadcasted_iota(jnp.int32, (tm, 1), 0)
    seq_start = (i * tm) % seq == 0
    seq_end = ((i + 1) * tm) % seq == 0

    def shifted(c0, c1):
        p = p_ref[:, c0:c1]
        prev_row = jnp.where(seq_start, 0.0, hp_ref[SUBLANES - 1:SUBLANES, c0:c1])
        next_row = jnp.where(seq_end, 0.0, hn_ref[0:1, c0:c1])
        p_prev = jnp.where(row == 0, prev_row, pltpu.roll(p, 1, 0))
        p_next = jnp.where(row == tm - 1, next_row, pltpu.roll(p, tm - 1, 0))
        return p + mup_ref[:, c0:c1] * (p_prev - p) + mun_ref[:, c0:c1] * (p_next - p)

    w = RWKV_WIDTH
    r = shifted(0, w)
    k = shifted(w, 2 * w)
    v = shifted(2 * w, 3 * w)
    codes = shifted(3 * w, 3 * w + LANES)
    gd = shifted(3 * w + LANES, 3 * w + 2 * LANES)
    r_o[...] = r
    v_o[...] = v

    lane = lax.broadcasted_iota(jnp.int32, codes.shape, 1)
    codes = jnp.where(lane < 2 * DECAY_RANK, jnp.tanh(codes), codes)
    mm = _dot(codes, wmix_ref[...], HI)
    g_o[...] = _dot(_sigmoid(gd), g2_ref[...], HI)

    kkv = k * kk_ref[...]
    bd = bd_ref[...].astype(BF16)
    ss = sum(_dot(part, bd) for part in _split3(kkv * kkv))
    kkn = kkv * lax.rsqrt(jnp.maximum(ss, 1e-24))
    kk_o[...] = kkn

    ka = ka_ref[...]
    for d, (lw_o, kd_o, b_o) in enumerate(((lw0_o, kd0_o, b0_o), (lw1_o, kd1_o, b1_o))):
        w_log = -_softplus(-(w0_ref[d:d + 1, :] + mm[:, d * w:(d + 1) * w])) - 0.5
        lw_o[...] = -jnp.exp(w_log)
        a = _sigmoid(a0_ref[d:d + 1, :] + mm[:, (2 + d) * w:(3 + d) * w])
        kd_o[...] = k * (1.0 + (a - 1.0) * ka)
        b_o[...] = kkn * a


def _prep(pr, mup, mun, wmix, w0, a0, g2p, k_k, k_a, bd, seq, tm=256):
    t = pr.shape[0]
    nblk8 = t // SUBLANES
    per = tm // SUBLANES
    full = lambda a: pl.BlockSpec(a.shape, lambda i: (0,) * a.ndim)
    out = pl.BlockSpec((tm, RWKV_WIDTH), lambda i: (i, 0))
    return pl.pallas_call(
        functools.partial(_prep_kernel, tm=tm, seq=seq),
        grid=(t // tm,),
        in_specs=[
            pl.BlockSpec((tm, RWKV_COLS_PAD), lambda i: (i, 0)),
            pl.BlockSpec((SUBLANES, RWKV_COLS_PAD), lambda i: (jnp.maximum(i * per - 1, 0), 0)),
            pl.BlockSpec((SUBLANES, RWKV_COLS_PAD),
                         lambda i: (jnp.minimum((i + 1) * per, nblk8 - 1), 0)),
            full(mup), full(mun), full(wmix), full(w0), full(a0), full(g2p),
            full(k_k), full(k_a), full(bd),
        ],
        out_specs=[out] * 10,
        out_shape=[jax.ShapeDtypeStruct((t, RWKV_WIDTH), F32)] * 10,
        compiler_params=_cparams(("parallel",)),
        name="prep",
    )(pr, pr, pr, mup, mun, wmix, w0, a0, g2p, k_k, k_a, bd)


def _split3(x):
    hi = x.astype(BF16)
    r1 = x - hi.astype(F32)
    mid = r1.astype(BF16)
    lo = (r1 - mid.astype(F32)).astype(BF16)
    return hi, mid, lo


def _wkv_kernel(rf, vf, kkf, lwf, kdf, bf, rb, vb, kkb, lwb, kdb, bb, yf_ref, yb_ref, state):
    c = pl.program_id(1)
    n = WKV_CHUNK
    hd = HEAD_DIM
    nh = N_RWKV_HEADS
    def bmm(a, b, ca, cb):
        return lax.dot_general(a.astype(BF16), b.astype(BF16), (((ca,), (cb,)), ((0,), (0,))),
                               preferred_element_type=F32)

    @pl.when(c == 0)
    def _():
        state[...] = jnp.zeros_like(state)

    ti = lax.broadcasted_iota(jnp.int32, (n, n), 0)
    si = lax.broadcasted_iota(jnp.int32, (n, n), 1)
    dirs = ((rf, vf, kkf, lwf, kdf, bf, si <= ti, si < ti, n - 1),
            (rb, vb, kkb, lwb, kdb, bb, si >= ti, si > ti, 0))
    heads = lambda x: jnp.stack([x[:, j * hd:(j + 1) * hd] for j in range(nh)], 0)
    parts = []
    for r_ref, v_ref, kk_ref, lw_ref, kd_ref, b_ref, incl, strict, last in dirs:
        lw = lw_ref[0]
        tri = incl.astype(BF16)
        cum = sum(_dot(tri, part) for part in _split3(lw))
        e_neg = jnp.exp(-cum)
        e_last = jnp.exp(cum[last:last + 1, :] - cum)
        g_scale = jnp.exp(cum[last:last + 1, :])
        parts.append(dict(
            at=heads((-kk_ref[0] * jnp.exp(cum - lw)).astype(BF16)),
            rt=heads((r_ref[0] * jnp.exp(cum)).astype(BF16)),
            bt=heads((b_ref[0] * e_neg).astype(BF16)),
            kt=heads((kd_ref[0] * e_neg).astype(BF16)),
            bl=heads((b_ref[0] * e_last).astype(BF16)),
            kl=heads((kd_ref[0] * e_last).astype(BF16)),
            v=heads(v_ref[0].astype(BF16)),
            gs=heads(g_scale),
            incl=jnp.broadcast_to(incl[None], (nh, n, n)),
            strict=jnp.broadcast_to(strict[None], (nh, n, n))))
    cat = lambda key: jnp.concatenate([parts[0][key], parts[1][key]], 0)
    at, rt, bt, kt, bl, kl, v, gs = (cat(k) for k in ("at", "rt", "bt", "kt", "bl", "kl", "v", "gs"))
    incl, strict = cat("incl"), cat("strict")
    g0 = state[...]
    g0b = g0.astype(BF16)
    m1 = bmm(jnp.concatenate([at, rt], 1), jnp.concatenate([bt, kt], 1), 2, 2)
    a_ab = jnp.where(strict, m1[:, :n, :n], 0.0)
    a_ak = jnp.where(strict, m1[:, :n, n:], 0.0)
    a_rb = jnp.where(incl, m1[:, n:, :n], 0.0)
    a_rk = jnp.where(incl, m1[:, n:, n:], 0.0)
    tinv = jnp.where((ti == si)[None], 1.0, a_ab)
    pw = a_ab
    for _ in range(int(np.log2(n)) - 1):
        pw = bmm(pw, pw, 2, 1)
        tinv = tinv + bmm(tinv, pw, 2, 1)
    rhs = bmm(a_ak, v, 2, 1) + bmm(at, g0b, 2, 2)
    u = bmm(tinv, rhs, 2, 1)
    uv = jnp.concatenate([u.astype(BF16), v], 1)
    y = bmm(jnp.concatenate([a_rb, a_rk], 2), uv, 2, 1) + bmm(rt, g0b, 2, 2)
    state[...] = g0 * gs + bmm(uv, jnp.concatenate([bl, kl], 1), 1, 1)
    for d, y_ref in enumerate((yf_ref, yb_ref)):
        y_ref[0] = jnp.concatenate([y[d * nh + j] for j in range(nh)], axis=1)


def _wkv(r, v, kk, lw0, lw1, kd0, kd1, b0, b1, batch, seq):
    n = WKV_CHUNK
    nc = seq // n
    shp = (batch, seq, RWKV_WIDTH)
    arrs = [a.reshape(shp) for a in (r, v, kk, lw0, kd0, b0, r, v, kk, lw1, kd1, b1)]
    fwd = pl.BlockSpec((1, n, RWKV_WIDTH), lambda b, c: (b, c, 0))
    bwd = pl.BlockSpec((1, n, RWKV_WIDTH), lambda b, c: (b, nc - 1 - c, 0))
    yf, yb = pl.pallas_call(
        _wkv_kernel,
        grid=(batch, nc),
        in_specs=[fwd] * 6 + [bwd] * 6,
        out_specs=[fwd, bwd],
        out_shape=[jax.ShapeDtypeStruct(shp, F32)] * 2,
        scratch_shapes=[pltpu.VMEM((2 * N_RWKV_HEADS, HEAD_DIM, HEAD_DIM), F32)],
        compiler_params=_cparams(("parallel", "arbitrary")),
        name="wkv",
    )(*arrs)
    return yf.reshape(batch * seq, RWKV_WIDTH), yb.reshape(batch * seq, RWKV_WIDTH)


def _mix_kernel(x_ref, ya_ref, yf_ref, yb_ref, r_ref, v_ref, kd0_ref, kd1_ref, g_ref,
                lng_ref, lnb_ref, rk_ref, bd_ref, wout_ref, l1g_ref, l1b_ref, h_ref):
    bd = bd_ref[...].astype(BF16)
    head_sum = lambda t: sum(_dot(part, bd) for part in _split3(t))
    inv = 1.0 / HEAD_DIM
    y = yf_ref[...] + yb_ref[...]
    mu = head_sum(y) * inv
    yc = y - mu
    var = head_sum(yc * yc) * inv
    yn = yc * lax.rsqrt(var + RWKV_GN_EPS) * lng_ref[...] + lnb_ref[...]
    k_mean = 0.5 * (kd0_ref[...] + kd1_ref[...])
    v = v_ref[...]
    bonus = head_sum(r_ref[...] * k_mean * rk_ref[...]) * v
    yr = (yn + bonus) * g_ref[...]
    mix = (_dot(ya_ref[...].astype(BF16), wout_ref[:ATTN_WIDTH, :])
           + _dot(yr.astype(BF16), wout_ref[ATTN_WIDTH:, :]))
    h_ref[...] = _layer_norm(DEEPNORM_ALPHA * x_ref[...] + mix, l1g_ref[...], l1b_ref[...])


def _mix(x2, ya, yf, yb, r, v, kd0, kd1, g, lng, lnb, rk, bd, wout, l1g, l1b, tm=256):
    t, d = x2.shape
    full = lambda a: pl.BlockSpec(a.shape, lambda i: (0,) * a.ndim)
    half = pl.BlockSpec((tm, RWKV_WIDTH), lambda i: (i, 0))
    wide = pl.BlockSpec((tm, d), lambda i: (i, 0))
    return pl.pallas_call(
        _mix_kernel,
        grid=(t // tm,),
        in_specs=[wide] + [half] * 8 + [full(lng), full(lnb), full(rk), full(bd), full(wout),
                                        full(l1g), full(l1b)],
        out_specs=wide,
        out_shape=jax.ShapeDtypeStruct((t, d), F32),
        compiler_params=_cparams(("parallel",)),
        name="mix",
    )(x2, ya, yf, yb, r, v, kd0, kd1, g, lng, lnb, rk, bd, wout, l1g, l1b)


def _top_rows(scs, k, payloads=None):
    n = scs[0].shape[0]
    iota = lax.broadcasted_iota(jnp.int32, scs[0].shape, 0).astype(F32)
    scs = list(scs)
    vals = [[] for _ in scs]
    picks = [[] for _ in scs]
    for _ in range(k):
        for a, sc in enumerate(scs):
            m = jnp.max(sc, axis=0, keepdims=True)
            pos = jnp.min(jnp.where(sc == m, iota, float(n)), axis=0, keepdims=True)
            hit = iota == pos
            vals[a].append(m)
            if payloads is None:
                picks[a].append(pos)
            else:
                picks[a].append(jnp.max(jnp.where(hit, payloads[a], -1.0), axis=0, keepdims=True))
            scs[a] = jnp.where(hit, -jnp.inf, sc)
    return [(jnp.concatenate(v, 0), jnp.concatenate(p, 0)) for v, p in zip(vals, picks)]


def _route_kernel(h_ref, wq_ref, keys_ref, idx_ref, gate_ref, gate_tok_ref, q_scr, idx_scr):
    k = PEER_TOPK
    q_scr[...] = _dot(h_ref[...].astype(BF16), wq_ref[...])

    def head_group(hg, carry):
        heads = [hg * ROUTE_HEADS_PER_STEP + i for i in range(ROUTE_HEADS_PER_STEP)]
        scores = []
        for hh in heads:
            for p in range(2):
                col = pl.multiple_of(hh * PEER_QDIM + p * PEER_HALF, PEER_HALF)
                qp = q_scr[:, pl.ds(col, PEER_HALF)]
                scores.append(_dot_nt(keys_ref[hh, p], qp, HI))
        tops = _top_rows(scores, k)
        cands, cand_ids = [], []
        for a in range(len(heads)):
            (s1, i1), (s2, i2) = tops[2 * a], tops[2 * a + 1]
            cands.append(jnp.concatenate([s1[i:i + 1] + s2[:k // (i + 1)] for i in range(k)], 0))
            cand_ids.append(jnp.concatenate(
                [i1[i:i + 1] * PEER_NKEYS + i2[:k // (i + 1)] for i in range(k)], 0))
        for hh, (cs, ids) in zip(heads, _top_rows(cands, k, cand_ids)):
            e = jnp.exp(cs - jnp.max(cs, axis=0, keepdims=True))
            row = pl.multiple_of(hh * k, k)
            gate_ref[pl.ds(row, k), :] = e / jnp.sum(e, axis=0, keepdims=True)
            idx_scr[pl.ds(row, k), :] = ids.astype(jnp.int32)
        return carry

    lax.fori_loop(0, PEER_HEADS // ROUTE_HEADS_PER_STEP, head_group, 0)
    idx_ref[...] = idx_scr[...].T
    gate_tok_ref[...] = gate_ref[...].T


def _route(h1, wq, keys, tm=256):
    t, d = h1.shape
    return pl.pallas_call(
        _route_kernel,
        grid=(t // tm,),
        in_specs=[
            pl.BlockSpec((tm, d), lambda i: (i, 0)),
            pl.BlockSpec(wq.shape, lambda i: (0, 0)),
            pl.BlockSpec(keys.shape, lambda i: (0, 0, 0, 0)),
        ],
        out_specs=[
            pl.BlockSpec((tm, N_SEL), lambda i: (i, 0)),
            pl.BlockSpec((N_SEL, tm), lambda i: (0, i)),
            pl.BlockSpec((tm, N_SEL), lambda i: (i, 0)),
        ],
        out_shape=[
            jax.ShapeDtypeStruct((t, N_SEL), jnp.int32),
            jax.ShapeDtypeStruct((N_SEL, t), F32),
            jax.ShapeDtypeStruct((t, N_SEL), F32),
        ],
        scratch_shapes=[pltpu.VMEM((tm, PEER_HEADS * PEER_QDIM), F32),
                        pltpu.VMEM((N_SEL, tm), jnp.int32)],
        compiler_params=_cparams(("parallel",)),
        name="route",
    )(h1, wq, keys)


def _peer_kernel(idx_ref, gate_ref, h_ref, uv_hbm, l2g_ref, l2b_ref, o_ref, buf, f_scr, sem, *, tb):
    d = h_ref.shape[1]

    def issue(t, slot):
        for j in range(N_SEL):
            pltpu.make_async_copy(uv_hbm.at[pl.ds(idx_ref[t, j], 1)], buf.at[slot, pl.ds(j, 1)],
                                  sem.at[slot]).start()

    def wait_all(slot):
        pltpu.make_async_copy(uv_hbm.at[pl.ds(0, N_SEL)], buf.at[slot], sem.at[slot]).wait()

    issue(0, 0)
    lane = lax.broadcasted_iota(jnp.int32, (N_SEL, tb), 1)

    def body(t, carry):
        slot = t % 2

        @pl.when(t + 1 < tb)
        def _():
            issue(t + 1, 1 - slot)

        wait_all(slot)
        z = jnp.sum(buf[slot, :, 0:d] * h_ref[pl.ds(t, 1), :], axis=1, keepdims=True)
        gcol = jnp.sum(jnp.where(lane == t, gate_ref[...], 0.0), axis=1, keepdims=True)
        coef = gcol * (0.5 * z * (1.0 + lax.erf(z * (2.0 ** -0.5))))
        f_scr[pl.ds(t, 1), :] = jnp.sum(coef * buf[slot, :, d:2 * d], axis=0, keepdims=True)
        return carry

    lax.fori_loop(0, tb, body, 0)
    o_ref[...] = _layer_norm(DEEPNORM_ALPHA * h_ref[...] + f_scr[...], l2g_ref[...], l2b_ref[...])


def _peer(idx, gate_t, h1, uv, l2g, l2b, tok0, ntok, tb=PEER_TB):
    d = h1.shape[1]
    b0 = tok0 // tb
    return pl.pallas_call(
        functools.partial(_peer_kernel, tb=tb),
        grid=(ntok // tb,),
        in_specs=[
            pl.BlockSpec((tb, N_SEL), lambda i: (b0 + i, 0), memory_space=pltpu.SMEM),
            pl.BlockSpec((N_SEL, tb), lambda i: (0, b0 + i)),
            pl.BlockSpec((tb, d), lambda i: (b0 + i, 0)),
            pl.BlockSpec(memory_space=pl.ANY),
            pl.BlockSpec(l2g.shape, lambda i: (0, 0)),
            pl.BlockSpec(l2b.shape, lambda i: (0, 0)),
        ],
        out_specs=pl.BlockSpec((tb, d), lambda i: (i, 0)),
        out_shape=jax.ShapeDtypeStruct((ntok, d), F32),
        scratch_shapes=[pltpu.VMEM((2, N_SEL, 2 * d), F32),
                        pltpu.VMEM((tb, d), F32),
                        pltpu.SemaphoreType.DMA((2,))],
        compiler_params=_cparams(("arbitrary",)),
        name="peer",
    )(idx, gate_t, h1, uv, l2g, l2b)


def _sc_mesh():
    return plsc.VectorSubcoreMesh(core_axis_name="c", subcore_axis_name="s")


def _worker_id():
    return lax.axis_index("s") * 2 + lax.axis_index("c")


def _sc_z_body(u_hbm, idx_hbm, h_hbm, z_hbm, idx_v, h_v, rows_v, z_v, sem, *, tpw, d, tok0):
    ngrp = N_SEL // SC_GROUP
    nchunk = d // SC_LANES
    nsteps = tpw * ngrp
    base = _worker_id() * tpw

    def gather(tok_buf, g, buf):
        return pltpu.make_async_copy(u_hbm.at[idx_v.at[tok_buf, g]], rows_v.at[buf], sem.at[buf])

    pltpu.sync_copy(idx_hbm.at[tok0 + base], idx_v.at[0])
    gather(0, 0, 0).start()

    @pl.loop(0, nsteps)
    def _(q):
        t_loc = q // ngrp
        g = q % ngrp
        buf = q % 2
        t = base + t_loc

        @pl.when(g == 0)
        def _():
            pltpu.sync_copy(h_hbm.at[tok0 + t], h_v)

            @pl.loop(0, N_SEL)
            def _(r):
                z_v[pl.ds(pl.multiple_of(r * SC_LANES, SC_LANES), SC_LANES)] = jnp.zeros((SC_LANES,), F32)

        @pl.when(q + 1 < nsteps)
        def _():
            tn = (q + 1) // ngrp
            gn = (q + 1) % ngrp

            @pl.when(gn == 0)
            def _():
                pltpu.sync_copy(idx_hbm.at[tok0 + base + tn], idx_v.at[tn % 2])

            gather(tn % 2, gn, 1 - buf).start()

        gather(t_loc % 2, g, buf).wait()

        @pl.loop(0, SC_GROUP, step=4)
        def _(rb):
            @pl.loop(0, nchunk, step=SC_CHUNK_UNROLL)
            def _(c0):
                accs = [jnp.zeros((SC_LANES,), F32) for _ in range(4)]
                for cc in range(SC_CHUNK_UNROLL):
                    sl = pl.ds(pl.multiple_of((c0 + cc) * SC_LANES, SC_LANES), SC_LANES)
                    hc = h_v[sl]
                    for i in range(4):
                        accs[i] = accs[i] + rows_v[buf, rb + i, sl] * hc
                for i in range(4):
                    row = pl.multiple_of((g * SC_GROUP + rb + i) * SC_LANES, SC_LANES)
                    plsc.addupdate(z_v.at[pl.ds(row, SC_LANES)], accs[i])

        @pl.when(g == ngrp - 1)
        def _():
            pltpu.sync_copy(z_v, z_hbm.at[t])


def _sc_z(u, idx, h, tok0, ntok):
    t, d = h.shape
    tpw = ntok // SC_WORKERS
    idx4 = idx.reshape(t, N_SEL // SC_GROUP, SC_GROUP)
    body = functools.partial(_sc_z_body, tpw=tpw, d=d, tok0=tok0)
    return pl.kernel(
        body,
        out_type=jax.ShapeDtypeStruct((ntok, N_SEL * SC_LANES), F32),
        mesh=_sc_mesh(),
        scratch_types=[
            pltpu.VMEM((2, N_SEL // SC_GROUP, SC_GROUP), jnp.int32),
            pltpu.VMEM((d,), F32),
            pltpu.VMEM((2, SC_GROUP, d), F32),
            pltpu.VMEM((N_SEL * SC_LANES,), F32),
            pltpu.SemaphoreType.DMA((2,)),
        ],
        name="sc_z",
    )(u, idx4, h)


def _sc_out_body(v_hbm, idx_hbm, coef_hbm, f_hbm, idx_v, coef_v, rows_v, out_v, sem, *, tpw, d, tok0):
    ngrp = N_SEL // SC_GROUP
    nchunk = d // SC_LANES
    nsteps = tpw * ngrp
    base = _worker_id() * tpw

    def gather(tok_buf, g, buf):
        return pltpu.make_async_copy(v_hbm.at[idx_v.at[tok_buf, g]], rows_v.at[buf], sem.at[buf])

    pltpu.sync_copy(idx_hbm.at[tok0 + base], idx_v.at[0])
    gather(0, 0, 0).start()

    @pl.loop(0, nsteps)
    def _(q):
        t_loc = q // ngrp
        g = q % ngrp
        buf = q % 2
        t = base + t_loc

        @pl.when(g == 0)
        def _():
            pltpu.sync_copy(coef_hbm.at[t], coef_v)

        @pl.when(q + 1 < nsteps)
        def _():
            tn = (q + 1) // ngrp
            gn = (q + 1) % ngrp

            @pl.when(gn == 0)
            def _():
                pltpu.sync_copy(idx_hbm.at[tok0 + base + tn], idx_v.at[tn % 2])

            gather(tn % 2, gn, 1 - buf).start()

        gather(t_loc % 2, g, buf).wait()

        @pl.loop(0, nchunk, step=SC_OUT_CHUNKS)
        def _(c0):
            sls = [pl.ds(pl.multiple_of((c0 + cc) * SC_LANES, SC_LANES), SC_LANES)
                   for cc in range(SC_OUT_CHUNKS)]
            accs = [None] * SC_OUT_CHUNKS
            for r in range(SC_GROUP):
                cf = coef_v[pl.ds(pl.multiple_of((g * SC_GROUP + r) * SC_LANES, SC_LANES), SC_LANES)]
                for cc in range(SC_OUT_CHUNKS):
                    p = rows_v[buf, r, sls[cc]] * cf
                    accs[cc] = p if accs[cc] is None else accs[cc] + p
            for cc in range(SC_OUT_CHUNKS):
                @pl.when(g == 0)
                def _():
                    out_v[sls[cc]] = accs[cc]

                @pl.when(g != 0)
                def _():
                    plsc.addupdate(out_v.at[sls[cc]], accs[cc])

        @pl.when(g == ngrp - 1)
        def _():
            pltpu.sync_copy(out_v, f_hbm.at[t])


def _sc_out(v, idx, coef16, tok0):
    t = idx.shape[0]
    ntok = coef16.shape[0]
    d = v.shape[1]
    tpw = ntok // SC_WORKERS
    idx4 = idx.reshape(t, N_SEL // SC_GROUP, SC_GROUP)
    body = functools.partial(_sc_out_body, tpw=tpw, d=d, tok0=tok0)
    return pl.kernel(
        body,
        out_type=jax.ShapeDtypeStruct((ntok, d), F32),
        mesh=_sc_mesh(),
        scratch_types=[
            pltpu.VMEM((2, N_SEL // SC_GROUP, SC_GROUP), jnp.int32),
            pltpu.VMEM((N_SEL * SC_LANES,), F32),
            pltpu.VMEM((2, SC_GROUP, d), F32),
            pltpu.VMEM((d,), F32),
            pltpu.SemaphoreType.DMA((2,)),
        ],
        name="sc_out",
    )(v, idx4, coef16)


def _coef_kernel(zp_ref, gate_ref, sel_ref, selt_ref, after_ref, o_ref):
    z = sum(_dot(part, sel_ref[...]) for part in _split3(zp_ref[...]))
    coef = gate_ref[...] * (0.5 * z * (1.0 + lax.erf(z * (2.0 ** -0.5))))
    o_ref[...] = sum(_dot(part, selt_ref[...]) for part in _split3(coef))


def _coef(zp, gate_tok, tok0, after, tm=256):
    ntok, wide = zp.shape
    b0 = tok0 // tm
    grp = np.arange(wide) // SC_LANES
    sel = jnp.asarray(grp[:, None] == np.arange(N_SEL)[None, :], BF16)
    return pl.pallas_call(
        _coef_kernel,
        grid=(ntok // tm,),
        in_specs=[
            pl.BlockSpec((tm, wide), lambda i: (i, 0)),
            pl.BlockSpec((tm, N_SEL), lambda i: (b0 + i, 0)),
            pl.BlockSpec(sel.shape, lambda i: (0, 0)),
            pl.BlockSpec(sel.shape[::-1], lambda i: (0, 0)),
            pl.BlockSpec((SUBLANES, LANES), lambda i: (0, 0)),
        ],
        out_specs=pl.BlockSpec((tm, wide), lambda i: (i, 0)),
        out_shape=jax.ShapeDtypeStruct((ntok, wide), F32),
        compiler_params=_cparams(("parallel",)),
        name="coef",
    )(zp, gate_tok, sel, sel.T, after)


def _ln_out_kernel(h_ref, f_ref, g_ref, b_ref, o_ref):
    o_ref[...] = _layer_norm(DEEPNORM_ALPHA * h_ref[...] + f_ref[...], g_ref[...], b_ref[...])


def _ln_out(h1, f, l2g, l2b, tok0, tm=256):
    ntok, d = f.shape
    b0 = tok0 // tm
    return pl.pallas_call(
        _ln_out_kernel,
        grid=(ntok // tm,),
        in_specs=[
            pl.BlockSpec((tm, d), lambda i: (b0 + i, 0)),
            pl.BlockSpec((tm, d), lambda i: (i, 0)),
            pl.BlockSpec(l2g.shape, lambda i: (0, 0)),
            pl.BlockSpec(l2b.shape, lambda i: (0, 0)),
        ],
        out_specs=pl.BlockSpec((tm, d), lambda i: (i, 0)),
        out_shape=jax.ShapeDtypeStruct((ntok, d), F32),
        compiler_params=_cparams(("parallel",)),
        name="ln_out",
    )(h1, f, l2g, l2b)


def _rope_tables(seq):
    pos = jnp.arange(seq, dtype=F32)
    inv_freq = ROPE_THETA ** (-jnp.arange(0, HEAD_DIM, 2, dtype=F32) / HEAD_DIM)
    ang = pos[:, None] * inv_freq[None, :]
    ang = jnp.concatenate([ang, ang], -1)
    sign = jnp.concatenate([-jnp.ones((HEAD_DIM // 2,), F32), jnp.ones((HEAD_DIM // 2,), F32)])
    reps = LANES // HEAD_DIM
    return jnp.tile(jnp.cos(ang), (1, reps)), jnp.tile(jnp.sin(ang) * sign, (1, reps))


def _layer(h2d, batch, seq, w_in, mu_prev, mu_next, w0, w2, a0, a2, g2, k_k, k_a, r_k, lnx_g, lnx_b,
           sink, w_out, ln1_g, ln1_b, peer_wq, peer_keys, peer_u, peer_v, ln2_g, ln2_b):
    w = RWKV_WIDTH
    row = lambda a: a.reshape(1, -1).astype(F32)
    w_in_p = jnp.pad(w_in, ((0, 0), (0, RWKV_COLS_PAD - RWKV_COLS))).astype(BF16)
    mup = jnp.pad(mu_prev, (0, RWKV_COLS_PAD - RWKV_COLS)).reshape(1, -1)
    mun = jnp.pad(mu_next, (0, RWKV_COLS_PAD - RWKV_COLS)).reshape(1, -1)
    wmix = jnp.zeros((LANES, 4 * w), F32)
    for d in range(2):
        wmix = wmix.at[d * DECAY_RANK:(d + 1) * DECAY_RANK, d * w:(d + 1) * w].set(w2[d])
        r0 = 2 * DECAY_RANK + d * ICLR_RANK
        wmix = wmix.at[r0:r0 + ICLR_RANK, (2 + d) * w:(3 + d) * w].set(a2[d])
    g2p = jnp.pad(g2, ((0, LANES - GATE_RANK), (0, 0)))
    head_of = np.arange(w) // HEAD_DIM
    bd = jnp.asarray(head_of[:, None] == head_of[None, :], F32)
    cos_t, sin_t = _rope_tables(seq)

    wq_b, keys_f, wout_b = peer_wq.astype(BF16), peer_keys.astype(F32), w_out.astype(BF16)

    def to_routing(x2d, nb):
        qkv, pr = _proj(x2d, w_in_p, cos_t, sin_t, seq)
        y_attn = _attention(qkv.reshape(nb, seq, ATTN_COLS), sink.astype(F32))
        r, v, kk, g, lw0, lw1, kd0, kd1, b0, b1 = _prep(
            pr, mup, mun, wmix, w0.astype(F32), a0.astype(F32), g2p, row(k_k), row(k_a), bd, seq)
        yf, yb = _wkv(r, v, kk, lw0, lw1, kd0, kd1, b0, b1, nb, seq)
        h1 = _mix(x2d, y_attn.reshape(nb * seq, ATTN_WIDTH), yf, yb, r, v, kd0, kd1, g,
                  row(lnx_g), row(lnx_b), row(r_k), bd, wout_b, row(ln1_g), row(ln1_b))
        return (h1,) + tuple(_route(h1, wq_b, keys_f))

    uv = jnp.concatenate([peer_u, peer_v], axis=1)
    l2g, l2b = row(ln2_g), row(ln2_b)

    nb_sc = batch * PEER_SC_SEQ_SHARE_8THS // 8
    t_sc = nb_sc * seq
    t_tc = (batch - nb_sc) * seq
    n_a = (t_tc * PEER_TC_FIRST_CALL_32NDS // 32) // PEER_TB * PEER_TB
    n_b = t_tc - n_a
    assert 0 < nb_sc < batch and t_sc % (SC_WORKERS * SUBLANES) == 0 and n_a > 0 and n_b % PEER_TB == 0
    h1s, idxs, _, gate_tok_s = to_routing(h2d[:t_sc], nb_sc)
    zp = _sc_z(peer_u.astype(F32), idxs, h1s, 0, t_sc)
    h1t, idxt, gate_t_t, _ = to_routing(h2d[t_sc:], batch - nb_sc)
    out_a = _peer(idxt, gate_t_t, h1t, uv, l2g, l2b, 0, n_a)
    f_sc = _sc_out(peer_v.astype(F32), idxs, _coef(zp, gate_tok_s, 0, out_a), 0)
    out_b = _peer(idxt, gate_t_t, h1t, uv, l2g, l2b, n_a, n_b)
    out_sc = _ln_out(h1s, f_sc, l2g, l2b, 0)
    return jnp.concatenate([out_sc, out_a, out_b], axis=0)


def kernel(x, w_in, mu_prev, mu_next, w0, w2, a0, a2, g2, k_k, k_a, r_k, lnx_g, lnx_b, sink, w_out,
           ln1_g, ln1_b, peer_wq, peer_keys, peer_u, peer_v, ln2_g, ln2_b):
    batch, seq, d = x.shape
    h = x.reshape(batch * seq, d)
    for l in range(DEPTH):
        h = _layer(h, batch, seq, w_in[l], mu_prev[l], mu_next[l], w0[l], w2[l], a0[l], a2[l], g2[l],
                   k_k[l], k_a[l], r_k[l], lnx_g[l], lnx_b[l], sink[l], w_out[l], ln1_g[l], ln1_b[l],
                   peer_wq[l], peer_keys[l], peer_u[l], peer_v[l], ln2_g[l], ln2_b[l])
    return h.reshape(batch, seq, d)
```

```python
import functools

import numpy as np
import jax
import jax.numpy as jnp
from jax import lax
from jax.experimental import pallas as pl
from jax.experimental.pallas import tpu as pltpu
from jax.experimental.pallas import tpu_sc as plsc

F32 = jnp.float32
BF16 = jnp.bfloat16
HI = lax.Precision.HIGHEST

HEAD_DIM = 64
N_Q_HEADS = 8
N_KV_HEADS = 2
Q_PER_KV = N_Q_HEADS // N_KV_HEADS
ATTN_WIDTH = N_Q_HEADS * HEAD_DIM
ATTN_KV_WIDTH = N_KV_HEADS * HEAD_DIM
ATTN_COLS = ATTN_WIDTH + 2 * ATTN_KV_WIDTH
WINDOW = 128
ATTN_BLOCK = 128
ROPE_THETA = 10000.0
N_RWKV_HEADS = 8
RWKV_WIDTH = N_RWKV_HEADS * HEAD_DIM
DECAY_RANK = 32
ICLR_RANK = 32
GATE_RANK = 96
RWKV_COLS = 3 * RWKV_WIDTH + 2 * DECAY_RANK + 2 * ICLR_RANK + GATE_RANK
RWKV_COLS_PAD = 1792
RWKV_GN_EPS = 64e-5
PEER_HEADS = 8
PEER_NKEYS = 128
PEER_QDIM = 256
PEER_HALF = PEER_QDIM // 2
PEER_TOPK = 16
N_SEL = PEER_HEADS * PEER_TOPK
LN_EPS = 1e-5
DEPTH = 1
DEEPNORM_ALPHA = (2.0 * DEPTH) ** 0.25

LANES = 128
SUBLANES = 8
WKV_CHUNK = 64
VMEM_LIMIT = 48 * 1024 * 1024
PEER_TB = 256
ROUTE_HEADS_PER_STEP = 4
SC_LANES = 16
SC_WORKERS = 32
SC_GROUP = 32
SC_CHUNK_UNROLL = 8
SC_OUT_CHUNKS = 4
PEER_FIRST_GROUP_SEQ_8THS = 4
PEER_SECOND_GROUP_SC_16THS = 5
PEER_TC_FIRST_CALL_32NDS = 13


def _cparams(sem):
    return pltpu.CompilerParams(dimension_semantics=sem, vmem_limit_bytes=VMEM_LIMIT)


def _dot(a, b, precision=None):
    return jnp.dot(a, b, preferred_element_type=F32, precision=precision)


def _dot_nt(a, b, precision=None):
    return lax.dot_general(a, b, (((1,), (1,)), ((), ())), preferred_element_type=F32,
                           precision=precision)


def _dot_tn(a, b, precision=None):
    return lax.dot_general(a, b, (((0,), (0,)), ((), ())), preferred_element_type=F32,
                           precision=precision)


def _layer_norm(z, g, b):
    mu = jnp.mean(z, -1, keepdims=True)
    zc = z - mu
    var = jnp.mean(zc * zc, -1, keepdims=True)
    return zc * lax.rsqrt(var + LN_EPS) * g + b


def _proj_kernel(x_ref, w_ref, cos_ref, sin_ref, qkv_ref, pr_ref):
    xb = x_ref[...].astype(BF16)
    cos = cos_ref[...]
    sin = sin_ref[...]
    lane = lax.broadcasted_iota(jnp.int32, cos.shape, 1)
    first_half = (lane & (HEAD_DIM // 2)) == 0

    def rope(t):
        rot = jnp.where(first_half, pltpu.roll(t, LANES - HEAD_DIM // 2, 1),
                        pltpu.roll(t, HEAD_DIM // 2, 1))
        return t * cos + rot * sin

    for c in range(0, ATTN_COLS, 2 * LANES):
        acc = _dot(xb, w_ref[:, c:c + 2 * LANES])
        for half in range(2):
            col = c + half * LANES
            t = acc[:, half * LANES:(half + 1) * LANES]
            if col < ATTN_WIDTH + ATTN_KV_WIDTH:
                t = rope(t)
            qkv_ref[:, col:col + LANES] = t
    for c in range(0, RWKV_COLS_PAD, 2 * LANES):
        pr_ref[:, c:c + 2 * LANES] = _dot(xb, w_ref[:, ATTN_COLS + c:ATTN_COLS + c + 2 * LANES])


def _proj(x2, w_in_p, cos_t, sin_t, seq, tm=512):
    t, d = x2.shape
    n_pos = seq // tm
    return pl.pallas_call(
        _proj_kernel,
        grid=(t // tm,),
        in_specs=[
            pl.BlockSpec((tm, d), lambda i: (i, 0)),
            pl.BlockSpec(w_in_p.shape, lambda i: (0, 0)),
            pl.BlockSpec((tm, LANES), lambda i: (i % n_pos, 0)),
            pl.BlockSpec((tm, LANES), lambda i: (i % n_pos, 0)),
        ],
        out_specs=[
            pl.BlockSpec((tm, ATTN_COLS), lambda i: (i, 0)),
            pl.BlockSpec((tm, RWKV_COLS_PAD), lambda i: (i, 0)),
        ],
        out_shape=[
            jax.ShapeDtypeStruct((t, ATTN_COLS), F32),
            jax.ShapeDtypeStruct((t, RWKV_COLS_PAD), F32),
        ],
        compiler_params=_cparams(("parallel",)),
        name="proj",
    )(x2, w_in_p, cos_t, sin_t)


def _attn_kernel(sink_ref, q_ref, kp_ref, kc_ref, kn_ref, vp_ref, vc_ref, vn_ref, o_ref, *, nb):
    n = pl.program_id(1)
    blk = ATTN_BLOCK
    rows = Q_PER_KV * blk
    q = q_ref[0] * (HEAD_DIM ** -0.5)
    kwin = jnp.concatenate([kp_ref[0], kc_ref[0], kn_ref[0]], axis=0)
    vwin = jnp.concatenate([vp_ref[0], vc_ref[0], vn_ref[0]], axis=0)
    qi = lax.broadcasted_iota(jnp.int32, (rows, 3 * blk), 0) & (blk - 1)
    kj = lax.broadcasted_iota(jnp.int32, (rows, 3 * blk), 1)
    dist = kj - qi
    valid = ((dist >= blk - WINDOW) & (dist <= blk + WINDOW)
             & ((kj >= blk) | (n > 0)) & ((kj < 2 * blk) | (n < nb - 1)))
    rowg = lax.broadcasted_iota(jnp.int32, (rows, 1), 0) // blk
    outs = []
    for h in range(N_KV_HEADS):
        qs = jnp.concatenate(
            [q[:, (Q_PER_KV * h + g) * HEAD_DIM:(Q_PER_KV * h + g + 1) * HEAD_DIM]
             for g in range(Q_PER_KV)], axis=0)
        kh = kwin[:, h * HEAD_DIM:(h + 1) * HEAD_DIM]
        vh = vwin[:, h * HEAD_DIM:(h + 1) * HEAD_DIM]
        logits = _dot_nt(qs.astype(BF16), kh.astype(BF16))
        logits = jnp.where(valid, logits, -1e30)
        sk = jnp.zeros((rows, 1), F32)
        for g in range(Q_PER_KV):
            sk = jnp.where(rowg == g, sink_ref[Q_PER_KV * h + g], sk)
        m = jnp.maximum(jnp.max(logits, -1, keepdims=True), sk)
        e = jnp.exp(logits - m)
        den = jnp.sum(e, -1, keepdims=True) + jnp.exp(sk - m)
        p = e / den
        o = _dot(p.astype(BF16), vh.astype(BF16))
        for g in range(Q_PER_KV):
            outs.append(o[g * blk:(g + 1) * blk])
    o_ref[0] = jnp.concatenate(outs, axis=1)


def _attention(qkv3, sink):
    b, s, _ = qkv3.shape
    blk = ATTN_BLOCK
    nb = s // blk
    kcol = ATTN_WIDTH // LANES
    vcol = kcol + 1

    def spec(col, shift):
        def imap(bi, n):
            return (bi, jnp.clip(n + shift, 0, nb - 1), col)
        return pl.BlockSpec((1, blk, LANES), imap)

    return pl.pallas_call(
        functools.partial(_attn_kernel, nb=nb),
        grid=(b, nb),
        in_specs=[
            pl.BlockSpec(memory_space=pltpu.SMEM),
            pl.BlockSpec((1, blk, ATTN_WIDTH), lambda bi, n: (bi, n, 0)),
            spec(kcol, -1), spec(kcol, 0), spec(kcol, 1),
            spec(vcol, -1), spec(vcol, 0), spec(vcol, 1),
        ],
        out_specs=pl.BlockSpec((1, blk, ATTN_WIDTH), lambda bi, n: (bi, n, 0)),
        out_shape=jax.ShapeDtypeStruct((b, s, ATTN_WIDTH), F32),
        compiler_params=_cparams(("parallel", "parallel")),
        name="attn",
    )(sink, qkv3, qkv3, qkv3, qkv3, qkv3, qkv3, qkv3)


def _softplus(x):
    return jnp.maximum(x, 0.0) + jnp.log(1.0 + jnp.exp(-jnp.abs(x)))


def _sigmoid(x):
    return 1.0 / (1.0 + jnp.exp(-x))


def _prep_kernel(p_ref, hp_ref, hn_ref, mup_ref, mun_ref, wmix_ref, w0_ref, a0_ref, g2_ref,
                 kk_ref, ka_ref, bd_ref,
                 r_o, v_o, kk_o, g_o, lw0_o, lw1_o, kd0_o, kd1_o, b0_o, b1_o, *, tm, seq):
    i = pl.program_id(0)
    row = lax.broadcasted_iota(jnp.int32, (tm, 1), 0)
    seq_start = (i * tm) % seq == 0
    seq_end = ((i + 1) * tm) % seq == 0

    def shifted(c0, c1):
        p = p_ref[:, c0:c1]
        prev_row = jnp.where(seq_start, 0.0, hp_ref[SUBLANES - 1:SUBLANES, c0:c1])
        next_row = jnp.where(seq_end, 0.0, hn_ref[0:1, c0:c1])
        p_prev = jnp.where(row == 0, prev_row, pltpu.roll(p, 1, 0))
        p_next = jnp.where(row == tm - 1, next_row, pltpu.roll(p, tm - 1, 0))
        return p + mup_ref[:, c0:c1] * (p_prev - p) + mun_ref[:, c0:c1] * (p_next - p)

    w = RWKV_WIDTH
    r = shifted(0, w)
    k = shifted(w, 2 * w)
    v = shifted(2 * w, 3 * w)
    codes = shifted(3 * w, 3 * w + LANES)
    gd = shifted(3 * w + LANES, 3 * w + 2 * LANES)
    r_o[...] = r
    v_o[...] = v

    lane = lax.broadcasted_iota(jnp.int32, codes.shape, 1)
    codes = jnp.where(lane < 2 * DECAY_RANK, jnp.tanh(codes), codes)
    mm = _dot(codes, wmix_ref[...], HI)
    g_o[...] = _dot(_sigmoid(gd), g2_ref[...], HI)

    kkv = k * kk_ref[...]
    bd = bd_ref[...].astype(BF16)
    ss = sum(_dot(part, bd) for part in _split3(kkv * kkv))
    kkn = kkv * lax.rsqrt(jnp.maximum(ss, 1e-24))
    kk_o[...] = kkn

    ka = ka_ref[...]
    for d, (lw_o, kd_o, b_o) in enumerate(((lw0_o, kd0_o, b0_o), (lw1_o, kd1_o, b1_o))):
        w_log = -_softplus(-(w0_ref[d:d + 1, :] + mm[:, d * w:(d + 1) * w])) - 0.5
        lw_o[...] = -jnp.exp(w_log)
        a = _sigmoid(a0_ref[d:d + 1, :] + mm[:, (2 + d) * w:(3 + d) * w])
        kd_o[...] = k * (1.0 + (a - 1.0) * ka)
        b_o[...] = kkn * a


def _prep(pr, mup, mun, wmix, w0, a0, g2p, k_k, k_a, bd, seq, tm=256):
    t = pr.shape[0]
    nblk8 = t // SUBLANES
    per = tm // SUBLANES
    full = lambda a: pl.BlockSpec(a.shape, lambda i: (0,) * a.ndim)
    out = pl.BlockSpec((tm, RWKV_WIDTH), lambda i: (i, 0))
    return pl.pallas_call(
        functools.partial(_prep_kernel, tm=tm, seq=seq),
        grid=(t // tm,),
        in_specs=[
            pl.BlockSpec((tm, RWKV_COLS_PAD), lambda i: (i, 0)),
            pl.BlockSpec((SUBLANES, RWKV_COLS_PAD), lambda i: (jnp.maximum(i * per - 1, 0), 0)),
            pl.BlockSpec((SUBLANES, RWKV_COLS_PAD),
                         lambda i: (jnp.minimum((i + 1) * per, nblk8 - 1), 0)),
            full(mup), full(mun), full(wmix), full(w0), full(a0), full(g2p),
            full(k_k), full(k_a), full(bd),
        ],
        out_specs=[out] * 10,
        out_shape=[jax.ShapeDtypeStruct((t, RWKV_WIDTH), F32)] * 10,
        compiler_params=_cparams(("parallel",)),
        name="prep",
    )(pr, pr, pr, mup, mun, wmix, w0, a0, g2p, k_k, k_a, bd)


def _split3(x):
    hi = x.astype(BF16)
    r1 = x - hi.astype(F32)
    mid = r1.astype(BF16)
    lo = (r1 - mid.astype(F32)).astype(BF16)
    return hi, mid, lo


def _wkv_kernel(rf, vf, kkf, lwf, kdf, bf, rb, vb, kkb, lwb, kdb, bb, yf_ref, yb_ref, state):
    c = pl.program_id(1)
    n = WKV_CHUNK
    hd = HEAD_DIM
    nh = N_RWKV_HEADS
    def bmm(a, b, ca, cb):
        return lax.dot_general(a.astype(BF16), b.astype(BF16), (((ca,), (cb,)), ((0,), (0,))),
                               preferred_element_type=F32)

    @pl.when(c == 0)
    def _():
        state[...] = jnp.zeros_like(state)

    ti = lax.broadcasted_iota(jnp.int32, (n, n), 0)
    si = lax.broadcasted_iota(jnp.int32, (n, n), 1)
    dirs = ((rf, vf, kkf, lwf, kdf, bf, si <= ti, si < ti, n - 1),
            (rb, vb, kkb, lwb, kdb, bb, si >= ti, si > ti, 0))
    heads = lambda x: jnp.stack([x[:, j * hd:(j + 1) * hd] for j in range(nh)], 0)
    parts = []
    for r_ref, v_ref, kk_ref, lw_ref, kd_ref, b_ref, incl, strict, last in dirs:
        lw = lw_ref[0]
        tri = incl.astype(BF16)
        cum = sum(_dot(tri, part) for part in _split3(lw))
        e_neg = jnp.exp(-cum)
        e_last = jnp.exp(cum[last:last + 1, :] - cum)
        g_scale = jnp.exp(cum[last:last + 1, :])
        parts.append(dict(
            at=heads((-kk_ref[0] * jnp.exp(cum - lw)).astype(BF16)),
            rt=heads((r_ref[0] * jnp.exp(cum)).astype(BF16)),
            bt=heads((b_ref[0] * e_neg).astype(BF16)),
            kt=heads((kd_ref[0] * e_neg).astype(BF16)),
            bl=heads((b_ref[0] * e_last).astype(BF16)),
            kl=heads((kd_ref[0] * e_last).astype(BF16)),
            v=heads(v_ref[0].astype(BF16)),
            gs=heads(g_scale),
            incl=jnp.broadcast_to(incl[None], (nh, n, n)),
            strict=jnp.broadcast_to(strict[None], (nh, n, n))))
    cat = lambda key: jnp.concatenate([parts[0][key], parts[1][key]], 0)
    at, rt, bt, kt, bl, kl, v, gs = (cat(k) for k in ("at", "rt", "bt", "kt", "bl", "kl", "v", "gs"))
    incl, strict = cat("incl"), cat("strict")
    g0 = state[...]
    g0b = g0.astype(BF16)
    m1 = bmm(jnp.concatenate([at, rt], 1), jnp.concatenate([bt, kt], 1), 2, 2)
    a_ab = jnp.where(strict, m1[:, :n, :n], 0.0)
    a_ak = jnp.where(strict, m1[:, :n, n:], 0.0)
    a_rb = jnp.where(incl, m1[:, n:, :n], 0.0)
    a_rk = jnp.where(incl, m1[:, n:, n:], 0.0)
    tinv = jnp.where((ti == si)[None], 1.0, a_ab)
    pw = a_ab
    for _ in range(int(np.log2(n)) - 1):
        pw = bmm(pw, pw, 2, 1)
        tinv = tinv + bmm(tinv, pw, 2, 1)
    rhs = bmm(a_ak, v, 2, 1) + bmm(at, g0b, 2, 2)
    u = bmm(tinv, rhs, 2, 1)
    uv = jnp.concatenate([u.astype(BF16), v], 1)
    y = bmm(jnp.concatenate([a_rb, a_rk], 2), uv, 2, 1) + bmm(rt, g0b, 2, 2)
    state[...] = g0 * gs + bmm(uv, jnp.concatenate([bl, kl], 1), 1, 1)
    for d, y_ref in enumerate((yf_ref, yb_ref)):
        y_ref[0] = jnp.concatenate([y[d * nh + j] for j in range(nh)], axis=1)


def _wkv(r, v, kk, lw0, lw1, kd0, kd1, b0, b1, batch, seq):
    n = WKV_CHUNK
    nc = seq // n
    shp = (batch, seq, RWKV_WIDTH)
    arrs = [a.reshape(shp) for a in (r, v, kk, lw0, kd0, b0, r, v, kk, lw1, kd1, b1)]
    fwd = pl.BlockSpec((1, n, RWKV_WIDTH), lambda b, c: (b, c, 0))
    bwd = pl.BlockSpec((1, n, RWKV_WIDTH), lambda b, c: (b, nc - 1 - c, 0))
    yf, yb = pl.pallas_call(
        _wkv_kernel,
        grid=(batch, nc),
        in_specs=[fwd] * 6 + [bwd] * 6,
        out_specs=[fwd, bwd],
        out_shape=[jax.ShapeDtypeStruct(shp, F32)] * 2,
        scratch_shapes=[pltpu.VMEM((2 * N_RWKV_HEADS, HEAD_DIM, HEAD_DIM), F32)],
        compiler_params=_cparams(("parallel", "arbitrary")),
        name="wkv",
    )(*arrs)
    return yf.reshape(batch * seq, RWKV_WIDTH), yb.reshape(batch * seq, RWKV_WIDTH)


def _mix_kernel(x_ref, ya_ref, yf_ref, yb_ref, r_ref, v_ref, kd0_ref, kd1_ref, g_ref,
                lng_ref, lnb_ref, rk_ref, bd_ref, wout_ref, l1g_ref, l1b_ref, h_ref):
    bd = bd_ref[...].astype(BF16)
    head_sum = lambda t: sum(_dot(part, bd) for part in _split3(t))
    inv = 1.0 / HEAD_DIM
    y = yf_ref[...] + yb_ref[...]
    mu = head_sum(y) * inv
    yc = y - mu
    var = head_sum(yc * yc) * inv
    yn = yc * lax.rsqrt(var + RWKV_GN_EPS) * lng_ref[...] + lnb_ref[...]
    k_mean = 0.5 * (kd0_ref[...] + kd1_ref[...])
    v = v_ref[...]
    bonus = head_sum(r_ref[...] * k_mean * rk_ref[...]) * v
    yr = (yn + bonus) * g_ref[...]
    mix = (_dot(ya_ref[...].astype(BF16), wout_ref[:ATTN_WIDTH, :])
           + _dot(yr.astype(BF16), wout_ref[ATTN_WIDTH:, :]))
    h_ref[...] = _layer_norm(DEEPNORM_ALPHA * x_ref[...] + mix, l1g_ref[...], l1b_ref[...])


def _mix(x2, ya, yf, yb, r, v, kd0, kd1, g, lng, lnb, rk, bd, wout, l1g, l1b, tm=256):
    t, d = x2.shape
    full = lambda a: pl.BlockSpec(a.shape, lambda i: (0,) * a.ndim)
    half = pl.BlockSpec((tm, RWKV_WIDTH), lambda i: (i, 0))
    wide = pl.BlockSpec((tm, d), lambda i: (i, 0))
    return pl.pallas_call(
        _mix_kernel,
        grid=(t // tm,),
        in_specs=[wide] + [half] * 8 + [full(lng), full(lnb), full(rk), full(bd), full(wout),
                                        full(l1g), full(l1b)],
        out_specs=wide,
        out_shape=jax.ShapeDtypeStruct((t, d), F32),
        compiler_params=_cparams(("parallel",)),
        name="mix",
    )(x2, ya, yf, yb, r, v, kd0, kd1, g, lng, lnb, rk, bd, wout, l1g, l1b)


def _top_rows(scs, k, payloads=None):
    n = scs[0].shape[0]
    iota = lax.broadcasted_iota(jnp.int32, scs[0].shape, 0).astype(F32)
    scs = list(scs)
    vals = [[] for _ in scs]
    picks = [[] for _ in scs]
    for _ in range(k):
        for a, sc in enumerate(scs):
            m = jnp.max(sc, axis=0, keepdims=True)
            pos = jnp.min(jnp.where(sc == m, iota, float(n)), axis=0, keepdims=True)
            hit = iota == pos
            vals[a].append(m)
            if payloads is None:
                picks[a].append(pos)
            else:
                picks[a].append(jnp.max(jnp.where(hit, payloads[a], -1.0), axis=0, keepdims=True))
            scs[a] = jnp.where(hit, -jnp.inf, sc)
    return [(jnp.concatenate(v, 0), jnp.concatenate(p, 0)) for v, p in zip(vals, picks)]


def _route_kernel(h_ref, wq_ref, keys_ref, idx_ref, gate_ref, gate_tok_ref, q_scr, idx_scr):
    k = PEER_TOPK
    q_scr[...] = _dot(h_ref[...].astype(BF16), wq_ref[...])

    def head_group(hg, carry):
        heads = [hg * ROUTE_HEADS_PER_STEP + i for i in range(ROUTE_HEADS_PER_STEP)]
        scores = []
        for hh in heads:
            for p in range(2):
                col = pl.multiple_of(hh * PEER_QDIM + p * PEER_HALF, PEER_HALF)
                qp = q_scr[:, pl.ds(col, PEER_HALF)]
                scores.append(_dot_nt(keys_ref[hh, p], qp, HI))
        tops = _top_rows(scores, k)
        cands, cand_ids = [], []
        for a in range(len(heads)):
            (s1, i1), (s2, i2) = tops[2 * a], tops[2 * a + 1]
            cands.append(jnp.concatenate([s1[i:i + 1] + s2[:k // (i + 1)] for i in range(k)], 0))
            cand_ids.append(jnp.concatenate(
                [i1[i:i + 1] * PEER_NKEYS + i2[:k // (i + 1)] for i in range(k)], 0))
        for hh, (cs, ids) in zip(heads, _top_rows(cands, k, cand_ids)):
            e = jnp.exp(cs - jnp.max(cs, axis=0, keepdims=True))
            row = pl.multiple_of(hh * k, k)
            gate_ref[pl.ds(row, k), :] = e / jnp.sum(e, axis=0, keepdims=True)
            idx_scr[pl.ds(row, k), :] = ids.astype(jnp.int32)
        return carry

    lax.fori_loop(0, PEER_HEADS // ROUTE_HEADS_PER_STEP, head_group, 0)
    idx_ref[...] = idx_scr[...].T
    gate_tok_ref[...] = gate_ref[...].T


def _route(h1, wq, keys, tm=256):
    t, d = h1.shape
    return pl.pallas_call(
        _route_kernel,
        grid=(t // tm,),
        in_specs=[
            pl.BlockSpec((tm, d), lambda i: (i, 0)),
            pl.BlockSpec(wq.shape, lambda i: (0, 0)),
            pl.BlockSpec(keys.shape, lambda i: (0, 0, 0, 0)),
        ],
        out_specs=[
            pl.BlockSpec((tm, N_SEL), lambda i: (i, 0)),
            pl.BlockSpec((N_SEL, tm), lambda i: (0, i)),
            pl.BlockSpec((tm, N_SEL), lambda i: (i, 0)),
        ],
        out_shape=[
            jax.ShapeDtypeStruct((t, N_SEL), jnp.int32),
            jax.ShapeDtypeStruct((N_SEL, t), F32),
            jax.ShapeDtypeStruct((t, N_SEL), F32),
        ],
        scratch_shapes=[pltpu.VMEM((tm, PEER_HEADS * PEER_QDIM), F32),
                        pltpu.VMEM((N_SEL, tm), jnp.int32)],
        compiler_params=_cparams(("parallel",)),
        name="route",
    )(h1, wq, keys)


def _peer_kernel(idx_ref, gate_ref, h_ref, uv_hbm, l2g_ref, l2b_ref, o_ref, buf, f_scr, sem, *, tb):
    d = h_ref.shape[1]

    def issue(t, slot):
        for j in range(N_SEL):
            pltpu.make_async_copy(uv_hbm.at[pl.ds(idx_ref[t, j], 1)], buf.at[slot, pl.ds(j, 1)],
                                  sem.at[slot]).start()

    def wait_all(slot):
        pltpu.make_async_copy(uv_hbm.at[pl.ds(0, N_SEL)], buf.at[slot], sem.at[slot]).wait()

    issue(0, 0)
    lane = lax.broadcasted_iota(jnp.int32, (N_SEL, tb), 1)

    def body(t, carry):
        slot = t % 2

        @pl.when(t + 1 < tb)
        def _():
            issue(t + 1, 1 - slot)

        wait_all(slot)
        z = jnp.sum(buf[slot, :, 0:d] * h_ref[pl.ds(t, 1), :], axis=1, keepdims=True)
        gcol = jnp.sum(jnp.where(lane == t, gate_ref[...], 0.0), axis=1, keepdims=True)
        coef = gcol * (0.5 * z * (1.0 + lax.erf(z * (2.0 ** -0.5))))
        f_scr[pl.ds(t, 1), :] = jnp.sum(coef * buf[slot, :, d:2 * d], axis=0, keepdims=True)
        return carry

    lax.fori_loop(0, tb, body, 0)
    o_ref[...] = _layer_norm(DEEPNORM_ALPHA * h_ref[...] + f_scr[...], l2g_ref[...], l2b_ref[...])


def _peer(idx, gate_t, h1, uv, l2g, l2b, tok0, ntok, tb=PEER_TB):
    d = h1.shape[1]
    b0 = tok0 // tb
    return pl.pallas_call(
        functools.partial(_peer_kernel, tb=tb),
        grid=(ntok // tb,),
        in_specs=[
            pl.BlockSpec((tb, N_SEL), lambda i: (b0 + i, 0), memory_space=pltpu.SMEM),
            pl.BlockSpec((N_SEL, tb), lambda i: (0, b0 + i)),
            pl.BlockSpec((tb, d), lambda i: (b0 + i, 0)),
            pl.BlockSpec(memory_space=pl.ANY),
            pl.BlockSpec(l2g.shape, lambda i: (0, 0)),
            pl.BlockSpec(l2b.shape, lambda i: (0, 0)),
        ],
        out_specs=pl.BlockSpec((tb, d), lambda i: (i, 0)),
        out_shape=jax.ShapeDtypeStruct((ntok, d), F32),
        scratch_shapes=[pltpu.VMEM((2, N_SEL, 2 * d), F32),
                        pltpu.VMEM((tb, d), F32),
                        pltpu.SemaphoreType.DMA((2,))],
        compiler_params=_cparams(("arbitrary",)),
        name="peer",
    )(idx, gate_t, h1, uv, l2g, l2b)


def _sc_mesh():
    return plsc.VectorSubcoreMesh(core_axis_name="c", subcore_axis_name="s")


def _worker_id():
    return lax.axis_index("s") * 2 + lax.axis_index("c")


def _sc_z_body(u_hbm, idx_hbm, h_hbm, z_hbm, idx_v, h_v, rows_v, z_v, sem, *, tpw, d, tok0):
    ngrp = N_SEL // SC_GROUP
    nchunk = d // SC_LANES
    nsteps = tpw * ngrp
    base = _worker_id() * tpw

    def gather(tok_buf, g, buf):
        return pltpu.make_async_copy(u_hbm.at[idx_v.at[tok_buf, g]], rows_v.at[buf], sem.at[buf])

    pltpu.sync_copy(idx_hbm.at[tok0 + base], idx_v.at[0])
    gather(0, 0, 0).start()

    @pl.loop(0, nsteps)
    def _(q):
        t_loc = q // ngrp
        g = q % ngrp
        buf = q % 2
        t = base + t_loc

        @pl.when(g == 0)
        def _():
            pltpu.sync_copy(h_hbm.at[tok0 + t], h_v)

            @pl.loop(0, N_SEL)
            def _(r):
                z_v[pl.ds(pl.multiple_of(r * SC_LANES, SC_LANES), SC_LANES)] = jnp.zeros((SC_LANES,), F32)

        @pl.when(q + 1 < nsteps)
        def _():
            tn = (q + 1) // ngrp
            gn = (q + 1) % ngrp

            @pl.when(gn == 0)
            def _():
                pltpu.sync_copy(idx_hbm.at[tok0 + base + tn], idx_v.at[tn % 2])

            gather(tn % 2, gn, 1 - buf).start()

        gather(t_loc % 2, g, buf).wait()

        @pl.loop(0, SC_GROUP, step=4)
        def _(rb):
            @pl.loop(0, nchunk, step=SC_CHUNK_UNROLL)
            def _(c0):
                accs = [jnp.zeros((SC_LANES,), F32) for _ in range(4)]
                for cc in range(SC_CHUNK_UNROLL):
                    sl = pl.ds(pl.multiple_of((c0 + cc) * SC_LANES, SC_LANES), SC_LANES)
                    hc = h_v[sl]
                    for i in range(4):
                        accs[i] = accs[i] + rows_v[buf, rb + i, sl] * hc
                for i in range(4):
                    row = pl.multiple_of((g * SC_GROUP + rb + i) * SC_LANES, SC_LANES)
                    plsc.addupdate(z_v.at[pl.ds(row, SC_LANES)], accs[i])

        @pl.when(g == ngrp - 1)
        def _():
            pltpu.sync_copy(z_v, z_hbm.at[t])


def _sc_z(u, idx, h, tok0, ntok):
    t, d = h.shape
    tpw = ntok // SC_WORKERS
    idx4 = idx.reshape(t, N_SEL // SC_GROUP, SC_GROUP)
    body = functools.partial(_sc_z_body, tpw=tpw, d=d, tok0=tok0)
    return pl.kernel(
        body,
        out_type=jax.ShapeDtypeStruct((ntok, N_SEL * SC_LANES), F32),
        mesh=_sc_mesh(),
        scratch_types=[
            pltpu.VMEM((2, N_SEL // SC_GROUP, SC_GROUP), jnp.int32),
            pltpu.VMEM((d,), F32),
            pltpu.VMEM((2, SC_GROUP, d), F32),
            pltpu.VMEM((N_SEL * SC_LANES,), F32),
            pltpu.SemaphoreType.DMA((2,)),
        ],
        name="sc_z",
    )(u, idx4, h)


def _sc_out_body(v_hbm, idx_hbm, coef_hbm, f_hbm, idx_v, coef_v, rows_v, out_v, sem, *, tpw, d, tok0):
    ngrp = N_SEL // SC_GROUP
    nchunk = d // SC_LANES
    nsteps = tpw * ngrp
    base = _worker_id() * tpw

    def gather(tok_buf, g, buf):
        return pltpu.make_async_copy(v_hbm.at[idx_v.at[tok_buf, g]], rows_v.at[buf], sem.at[buf])

    pltpu.sync_copy(idx_hbm.at[tok0 + base], idx_v.at[0])
    gather(0, 0, 0).start()

    @pl.loop(0, nsteps)
    def _(q):
        t_loc = q // ngrp
        g = q % ngrp
        buf = q % 2
        t = base + t_loc

        @pl.when(g == 0)
        def _():
            pltpu.sync_copy(coef_hbm.at[t], coef_v)

        @pl.when(q + 1 < nsteps)
        def _():
            tn = (q + 1) // ngrp
            gn = (q + 1) % ngrp

            @pl.when(gn == 0)
            def _():
                pltpu.sync_copy(idx_hbm.at[tok0 + base + tn], idx_v.at[tn % 2])

            gather(tn % 2, gn, 1 - buf).start()

        gather(t_loc % 2, g, buf).wait()

        @pl.loop(0, nchunk, step=SC_OUT_CHUNKS)
        def _(c0):
            sls = [pl.ds(pl.multiple_of((c0 + cc) * SC_LANES, SC_LANES), SC_LANES)
                   for cc in range(SC_OUT_CHUNKS)]
            accs = [None] * SC_OUT_CHUNKS
            for r in range(SC_GROUP):
                cf = coef_v[pl.ds(pl.multiple_of((g * SC_GROUP + r) * SC_LANES, SC_LANES), SC_LANES)]
                for cc in range(SC_OUT_CHUNKS):
                    p = rows_v[buf, r, sls[cc]] * cf
                    accs[cc] = p if accs[cc] is None else accs[cc] + p
            for cc in range(SC_OUT_CHUNKS):
                @pl.when(g == 0)
                def _():
                    out_v[sls[cc]] = accs[cc]

                @pl.when(g != 0)
                def _():
                    plsc.addupdate(out_v.at[sls[cc]], accs[cc])

        @pl.when(g == ngrp - 1)
        def _():
            pltpu.sync_copy(out_v, f_hbm.at[t])


def _sc_out(v, idx, coef16, tok0):
    t = idx.shape[0]
    ntok = coef16.shape[0]
    d = v.shape[1]
    tpw = ntok // SC_WORKERS
    idx4 = idx.reshape(t, N_SEL // SC_GROUP, SC_GROUP)
    body = functools.partial(_sc_out_body, tpw=tpw, d=d, tok0=tok0)
    return pl.kernel(
        body,
        out_type=jax.ShapeDtypeStruct((ntok, d), F32),
        mesh=_sc_mesh(),
        scratch_types=[
            pltpu.VMEM((2, N_SEL // SC_GROUP, SC_GROUP), jnp.int32),
            pltpu.VMEM((N_SEL * SC_LANES,), F32),
            pltpu.VMEM((2, SC_GROUP, d), F32),
            pltpu.VMEM((d,), F32),
            pltpu.SemaphoreType.DMA((2,)),
        ],
        name="sc_out",
    )(v, idx4, coef16)


def _coef_kernel(zp_ref, gate_ref, sel_ref, selt_ref, after_ref, o_ref):
    z = sum(_dot(part, sel_ref[...]) for part in _split3(zp_ref[...]))
    coef = gate_ref[...] * (0.5 * z * (1.0 + lax.erf(z * (2.0 ** -0.5))))
    o_ref[...] = sum(_dot(part, selt_ref[...]) for part in _split3(coef))


def _coef(zp, gate_tok, tok0, after, tm=256):
    ntok, wide = zp.shape
    b0 = tok0 // tm
    grp = np.arange(wide) // SC_LANES
    sel = jnp.asarray(grp[:, None] == np.arange(N_SEL)[None, :], BF16)
    return pl.pallas_call(
        _coef_kernel,
        grid=(ntok // tm,),
        in_specs=[
            pl.BlockSpec((tm, wide), lambda i: (i, 0)),
            pl.BlockSpec((tm, N_SEL), lambda i: (b0 + i, 0)),
            pl.BlockSpec(sel.shape, lambda i: (0, 0)),
            pl.BlockSpec(sel.shape[::-1], lambda i: (0, 0)),
            pl.BlockSpec((SUBLANES, LANES), lambda i: (0, 0)),
        ],
        out_specs=pl.BlockSpec((tm, wide), lambda i: (i, 0)),
        out_shape=jax.ShapeDtypeStruct((ntok, wide), F32),
        compiler_params=_cparams(("parallel",)),
        name="coef",
    )(zp, gate_tok, sel, sel.T, after)


def _ln_out_kernel(h_ref, f_ref, g_ref, b_ref, o_ref):
    o_ref[...] = _layer_norm(DEEPNORM_ALPHA * h_ref[...] + f_ref[...], g_ref[...], b_ref[...])


def _ln_out(h1, f, l2g, l2b, tok0, tm=256):
    ntok, d = f.shape
    b0 = tok0 // tm
    return pl.pallas_call(
        _ln_out_kernel,
        grid=(ntok // tm,),
        in_specs=[
            pl.BlockSpec((tm, d), lambda i: (b0 + i, 0)),
            pl.BlockSpec((tm, d), lambda i: (i, 0)),
            pl.BlockSpec(l2g.shape, lambda i: (0, 0)),
            pl.BlockSpec(l2b.shape, lambda i: (0, 0)),
        ],
        out_specs=pl.BlockSpec((tm, d), lambda i: (i, 0)),
        out_shape=jax.ShapeDtypeStruct((ntok, d), F32),
        compiler_params=_cparams(("parallel",)),
        name="ln_out",
    )(h1, f, l2g, l2b)


def _rope_tables(seq):
    pos = jnp.arange(seq, dtype=F32)
    inv_freq = ROPE_THETA ** (-jnp.arange(0, HEAD_DIM, 2, dtype=F32) / HEAD_DIM)
    ang = pos[:, None] * inv_freq[None, :]
    ang = jnp.concatenate([ang, ang], -1)
    sign = jnp.concatenate([-jnp.ones((HEAD_DIM // 2,), F32), jnp.ones((HEAD_DIM // 2,), F32)])
    reps = LANES // HEAD_DIM
    return jnp.tile(jnp.cos(ang), (1, reps)), jnp.tile(jnp.sin(ang) * sign, (1, reps))


def _layer(h2d, batch, seq, w_in, mu_prev, mu_next, w0, w2, a0, a2, g2, k_k, k_a, r_k, lnx_g, lnx_b,
           sink, w_out, ln1_g, ln1_b, peer_wq, peer_keys, peer_u, peer_v, ln2_g, ln2_b):
    w = RWKV_WIDTH
    row = lambda a: a.reshape(1, -1).astype(F32)
    w_in_p = jnp.pad(w_in, ((0, 0), (0, RWKV_COLS_PAD - RWKV_COLS))).astype(BF16)
    mup = jnp.pad(mu_prev, (0, RWKV_COLS_PAD - RWKV_COLS)).reshape(1, -1)
    mun = jnp.pad(mu_next, (0, RWKV_COLS_PAD - RWKV_COLS)).reshape(1, -1)
    wmix = jnp.zeros((LANES, 4 * w), F32)
    for d in range(2):
        wmix = wmix.at[d * DECAY_RANK:(d + 1) * DECAY_RANK, d * w:(d + 1) * w].set(w2[d])
        r0 = 2 * DECAY_RANK + d * ICLR_RANK
        wmix = wmix.at[r0:r0 + ICLR_RANK, (2 + d) * w:(3 + d) * w].set(a2[d])
    g2p = jnp.pad(g2, ((0, LANES - GATE_RANK), (0, 0)))
    head_of = np.arange(w) // HEAD_DIM
    bd = jnp.asarray(head_of[:, None] == head_of[None, :], F32)
    cos_t, sin_t = _rope_tables(seq)

    wq_b, keys_f, wout_b = peer_wq.astype(BF16), peer_keys.astype(F32), w_out.astype(BF16)

    def to_routing(x2d, nb):
        qkv, pr = _proj(x2d, w_in_p, cos_t, sin_t, seq)
        y_attn = _attention(qkv.reshape(nb, seq, ATTN_COLS), sink.astype(F32))
        r, v, kk, g, lw0, lw1, kd0, kd1, b0, b1 = _prep(
            pr, mup, mun, wmix, w0.astype(F32), a0.astype(F32), g2p, row(k_k), row(k_a), bd, seq)
        yf, yb = _wkv(r, v, kk, lw0, lw1, kd0, kd1, b0, b1, nb, seq)
        h1 = _mix(x2d, y_attn.reshape(nb * seq, ATTN_WIDTH), yf, yb, r, v, kd0, kd1, g,
                  row(lnx_g), row(lnx_b), row(r_k), bd, wout_b, row(ln1_g), row(ln1_b))
        return (h1,) + tuple(_route(h1, wq_b, keys_f))

    uv = jnp.concatenate([peer_u, peer_v], axis=1)
    l2g, l2b = row(ln2_g), row(ln2_b)

    nb_1 = batch * PEER_FIRST_GROUP_SEQ_8THS // 8
    t_1 = nb_1 * seq
    t_2 = (batch - nb_1) * seq
    n_x = (t_2 * PEER_SECOND_GROUP_SC_16THS // 16) // (SC_WORKERS * SUBLANES) * (SC_WORKERS * SUBLANES)
    n_a = ((t_2 - n_x) * PEER_TC_FIRST_CALL_32NDS // 32) // PEER_TB * PEER_TB
    n_b = t_2 - n_x - n_a
    assert 0 < nb_1 < batch and t_1 % (SC_WORKERS * SUBLANES) == 0
    assert n_x > 0 and n_a > 0 and n_b > 0 and n_x % PEER_TB == 0 and n_b % PEER_TB == 0
    u32, v32 = peer_u.astype(F32), peer_v.astype(F32)
    h1_1, idx_1, _, gate_tok_1 = to_routing(h2d[:t_1], nb_1)
    zp_1 = _sc_z(u32, idx_1, h1_1, 0, t_1)
    h1_2, idx_2, gate_t_2, gate_tok_2 = to_routing(h2d[t_1:], batch - nb_1)
    zp_2 = _sc_z(u32, idx_2, h1_2, 0, n_x)
    out_a = _peer(idx_2, gate_t_2, h1_2, uv, l2g, l2b, n_x, n_a)
    f_1 = _sc_out(v32, idx_1, _coef(zp_1, gate_tok_1, 0, out_a), 0)
    f_2 = _sc_out(v32, idx_2, _coef(zp_2, gate_tok_2, 0, out_a), 0)
    out_b = _peer(idx_2, gate_t_2, h1_2, uv, l2g, l2b, n_x + n_a, n_b)
    out_1 = _ln_out(h1_1, f_1, l2g, l2b, 0)
    out_2 = _ln_out(h1_2, f_2, l2g, l2b, 0)
    return jnp.concatenate([out_1, out_2, out_a, out_b], axis=0)


def kernel(x, w_in, mu_prev, mu_next, w0, w2, a0, a2, g2, k_k, k_a, r_k, lnx_g, lnx_b, sink, w_out,
           ln1_g, ln1_b, peer_wq, peer_keys, peer_u, peer_v, ln2_g, ln2_b):
    batch, seq, d = x.shape
    h = x.reshape(batch * seq, d)
    for l in range(DEPTH):
        h = _layer(h, batch, seq, w_in[l], mu_prev[l], mu_next[l], w0[l], w2[l], a0[l], a2[l], g2[l],
                   k_k[l], k_a[l], r_k[l], lnx_g[l], lnx_b[l], sink[l], w_out[l], ln1_g[l], ln1_b[l],
                   peer_wq[l], peer_keys[l], peer_u[l], peer_v[l], ln2_g[l], ln2_b[l])
    return h.reshape(batch, seq, d)
```

```python
import functools

import numpy as np
import jax
import jax.numpy as jnp
from jax import lax
from jax.experimental import pallas as pl
from jax.experimental.pallas import tpu as pltpu
from jax.experimental.pallas import tpu_sc as plsc

F32 = jnp.float32
BF16 = jnp.bfloat16
HI = lax.Precision.HIGHEST

HEAD_DIM = 64
N_Q_HEADS = 8
N_KV_HEADS = 2
Q_PER_KV = N_Q_HEADS // N_KV_HEADS
ATTN_WIDTH = N_Q_HEADS * HEAD_DIM
ATTN_KV_WIDTH = N_KV_HEADS * HEAD_DIM
ATTN_COLS = ATTN_WIDTH + 2 * ATTN_KV_WIDTH
WINDOW = 128
ATTN_BLOCK = 128
ROPE_THETA = 10000.0
N_RWKV_HEADS = 8
RWKV_WIDTH = N_RWKV_HEADS * HEAD_DIM
DECAY_RANK = 32
ICLR_RANK = 32
GATE_RANK = 96
RWKV_COLS = 3 * RWKV_WIDTH + 2 * DECAY_RANK + 2 * ICLR_RANK + GATE_RANK
RWKV_COLS_PAD = 1792
RWKV_GN_EPS = 64e-5
PEER_HEADS = 8
PEER_NKEYS = 128
PEER_QDIM = 256
PEER_HALF = PEER_QDIM // 2
PEER_TOPK = 16
N_SEL = PEER_HEADS * PEER_TOPK
LN_EPS = 1e-5
DEPTH = 1
DEEPNORM_ALPHA = (2.0 * DEPTH) ** 0.25

LANES = 128
SUBLANES = 8
WKV_CHUNK = 64
VMEM_LIMIT = 48 * 1024 * 1024
PEER_TB = 256
ROUTE_HEADS_PER_STEP = 4
SC_LANES = 16
SC_WORKERS = 32
SC_GROUP = 32
SC_CHUNK_UNROLL = 8
SC_OUT_CHUNKS = 4
PEER_FIRST_GROUP_SEQ_8THS = 4
PEER_SECOND_GROUP_SC_16THS = 5
PEER_TC_FIRST_CALL_32NDS = 13


def _cparams(sem):
    return pltpu.CompilerParams(dimension_semantics=sem, vmem_limit_bytes=VMEM_LIMIT)


def _dot(a, b, precision=None):
    return jnp.dot(a, b, preferred_element_type=F32, precision=precision)


def _dot_nt(a, b, precision=None):
    return lax.dot_general(a, b, (((1,), (1,)), ((), ())), preferred_element_type=F32,
                           precision=precision)


def _dot_tn(a, b, precision=None):
    return lax.dot_general(a, b, (((0,), (0,)), ((), ())), preferred_element_type=F32,
                           precision=precision)


def _layer_norm(z, g, b):
    mu = jnp.mean(z, -1, keepdims=True)
    zc = z - mu
    var = jnp.mean(zc * zc, -1, keepdims=True)
    return zc * lax.rsqrt(var + LN_EPS) * g + b


def _proj_kernel(x_ref, w_ref, cos_ref, sin_ref, after_ref, qkv_ref, pr_ref):
    xb = x_ref[...].astype(BF16)
    cos = cos_ref[...]
    sin = sin_ref[...]
    lane = lax.broadcasted_iota(jnp.int32, cos.shape, 1)
    first_half = (lane & (HEAD_DIM // 2)) == 0

    def rope(t):
        rot = jnp.where(first_half, pltpu.roll(t, LANES - HEAD_DIM // 2, 1),
                        pltpu.roll(t, HEAD_DIM // 2, 1))
        return t * cos + rot * sin

    for c in range(0, ATTN_COLS, 2 * LANES):
        acc = _dot(xb, w_ref[:, c:c + 2 * LANES])
        for half in range(2):
            col = c + half * LANES
            t = acc[:, half * LANES:(half + 1) * LANES]
            if col < ATTN_WIDTH + ATTN_KV_WIDTH:
                t = rope(t)
            qkv_ref[:, col:col + LANES] = t
    for c in range(0, RWKV_COLS_PAD, 2 * LANES):
        pr_ref[:, c:c + 2 * LANES] = _dot(xb, w_ref[:, ATTN_COLS + c:ATTN_COLS + c + 2 * LANES])


def _proj(x2, w_in_p, cos_t, sin_t, seq, after, tm=512):
    t, d = x2.shape
    n_pos = seq // tm
    after = after.reshape(-1)[:SUBLANES * LANES].reshape(SUBLANES, LANES)
    return pl.pallas_call(
        _proj_kernel,
        grid=(t // tm,),
        in_specs=[
            pl.BlockSpec((tm, d), lambda i: (i, 0)),
            pl.BlockSpec(w_in_p.shape, lambda i: (0, 0)),
            pl.BlockSpec((tm, LANES), lambda i: (i % n_pos, 0)),
            pl.BlockSpec((tm, LANES), lambda i: (i % n_pos, 0)),
            pl.BlockSpec((SUBLANES, LANES), lambda i: (0, 0)),
        ],
        out_specs=[
            pl.BlockSpec((tm, ATTN_COLS), lambda i: (i, 0)),
            pl.BlockSpec((tm, RWKV_COLS_PAD), lambda i: (i, 0)),
        ],
        out_shape=[
            jax.ShapeDtypeStruct((t, ATTN_COLS), F32),
            jax.ShapeDtypeStruct((t, RWKV_COLS_PAD), F32),
        ],
        compiler_params=_cparams(("parallel",)),
        name="proj",
    )(x2, w_in_p, cos_t, sin_t, after)


def _attn_kernel(sink_ref, q_ref, kp_ref, kc_ref, kn_ref, vp_ref, vc_ref, vn_ref, o_ref, *, nb):
    n = pl.program_id(1)
    blk = ATTN_BLOCK
    rows = Q_PER_KV * blk
    q = q_ref[0] * (HEAD_DIM ** -0.5)
    kwin = jnp.concatenate([kp_ref[0], kc_ref[0], kn_ref[0]], axis=0)
    vwin = jnp.concatenate([vp_ref[0], vc_ref[0], vn_ref[0]], axis=0)
    qi = lax.broadcasted_iota(jnp.int32, (rows, 3 * blk), 0) & (blk - 1)
    kj = lax.broadcasted_iota(jnp.int32, (rows, 3 * blk), 1)
    dist = kj - qi
    valid = ((dist >= blk - WINDOW) & (dist <= blk + WINDOW)
             & ((kj >= blk) | (n > 0)) & ((kj < 2 * blk) | (n < nb - 1)))
    rowg = lax.broadcasted_iota(jnp.int32, (rows, 1), 0) // blk
    outs = []
    for h in range(N_KV_HEADS):
        qs = jnp.concatenate(
            [q[:, (Q_PER_KV * h + g) * HEAD_DIM:(Q_PER_KV * h + g + 1) * HEAD_DIM]
             for g in range(Q_PER_KV)], axis=0)
        kh = kwin[:, h * HEAD_DIM:(h + 1) * HEAD_DIM]
        vh = vwin[:, h * HEAD_DIM:(h + 1) * HEAD_DIM]
        logits = _dot_nt(qs.astype(BF16), kh.astype(BF16))
        logits = jnp.where(valid, logits, -1e30)
        sk = jnp.zeros((rows, 1), F32)
        for g in range(Q_PER_KV):
            sk = jnp.where(rowg == g, sink_ref[Q_PER_KV * h + g], sk)
        m = jnp.maximum(jnp.max(logits, -1, keepdims=True), sk)
        e = jnp.exp(logits - m)
        den = jnp.sum(e, -1, keepdims=True) + jnp.exp(sk - m)
        p = e / den
        o = _dot(p.astype(BF16), vh.astype(BF16))
        for g in range(Q_PER_KV):
            outs.append(o[g * blk:(g + 1) * blk])
    o_ref[0] = jnp.concatenate(outs, axis=1)


def _attention(qkv3, sink):
    b, s, _ = qkv3.shape
    blk = ATTN_BLOCK
    nb = s // blk
    kcol = ATTN_WIDTH // LANES
    vcol = kcol + 1

    def spec(col, shift):
        def imap(bi, n):
            return (bi, jnp.clip(n + shift, 0, nb - 1), col)
        return pl.BlockSpec((1, blk, LANES), imap)

    return pl.pallas_call(
        functools.partial(_attn_kernel, nb=nb),
        grid=(b, nb),
        in_specs=[
            pl.BlockSpec(memory_space=pltpu.SMEM),
            pl.BlockSpec((1, blk, ATTN_WIDTH), lambda bi, n: (bi, n, 0)),
            spec(kcol, -1), spec(kcol, 0), spec(kcol, 1),
            spec(vcol, -1), spec(vcol, 0), spec(vcol, 1),
        ],
        out_specs=pl.BlockSpec((1, blk, ATTN_WIDTH), lambda bi, n: (bi, n, 0)),
        out_shape=jax.ShapeDtypeStruct((b, s, ATTN_WIDTH), F32),
        compiler_params=_cparams(("parallel", "parallel")),
        name="attn",
    )(sink, qkv3, qkv3, qkv3, qkv3, qkv3, qkv3, qkv3)


def _softplus(x):
    return jnp.maximum(x, 0.0) + jnp.log(1.0 + jnp.exp(-jnp.abs(x)))


def _sigmoid(x):
    return 1.0 / (1.0 + jnp.exp(-x))


def _prep_kernel(p_ref, hp_ref, hn_ref, mup_ref, mun_ref, wmix_ref, w0_ref, a0_ref, g2_ref,
                 kk_ref, ka_ref, bd_ref,
                 r_o, v_o, kk_o, g_o, lw0_o, lw1_o, kd0_o, kd1_o, b0_o, b1_o, *, tm, seq):
    i = pl.program_id(0)
    row = lax.broadcasted_iota(jnp.int32, (tm, 1), 0)
    seq_start = (i * tm) % seq == 0
    seq_end = ((i + 1) * tm) % seq == 0

    def shifted(c0, c1):
        p = p_ref[:, c0:c1]
        prev_row = jnp.where(seq_start, 0.0, hp_ref[SUBLANES - 1:SUBLANES, c0:c1])
        next_row = jnp.where(seq_end, 0.0, hn_ref[0:1, c0:c1])
        p_prev = jnp.where(row == 0, prev_row, pltpu.roll(p, 1, 0))
        p_next = jnp.where(row == tm - 1, next_row, pltpu.roll(p, tm - 1, 0))
        return p + mup_ref[:, c0:c1] * (p_prev - p) + mun_ref[:, c0:c1] * (p_next - p)

    w = RWKV_WIDTH
    r = shifted(0, w)
    k = shifted(w, 2 * w)
    v = shifted(2 * w, 3 * w)
    codes = shifted(3 * w, 3 * w + LANES)
    gd = shifted(3 * w + LANES, 3 * w + 2 * LANES)
    r_o[...] = r
    v_o[...] = v

    lane = lax.broadcasted_iota(jnp.int32, codes.shape, 1)
    codes = jnp.where(lane < 2 * DECAY_RANK, jnp.tanh(codes), codes)
    mm = _dot(codes, wmix_ref[...], HI)
    g_o[...] = _dot(_sigmoid(gd), g2_ref[...], HI)

    kkv = k * kk_ref[...]
    bd = bd_ref[...].astype(BF16)
    ss = sum(_dot(part, bd) for part in _split3(kkv * kkv))
    kkn = kkv * lax.rsqrt(jnp.maximum(ss, 1e-24))
    kk_o[...] = kkn

    ka = ka_ref[...]
    for d, (lw_o, kd_o, b_o) in enumerate(((lw0_o, kd0_o, b0_o), (lw1_o, kd1_o, b1_o))):
        w_log = -_softplus(-(w0_ref[d:d + 1, :] + mm[:, d * w:(d + 1) * w])) - 0.5
        lw_o[...] = -jnp.exp(w_log)
        a = _sigmoid(a0_ref[d:d + 1, :] + mm[:, (2 + d) * w:(3 + d) * w])
        kd_o[...] = k * (1.0 + (a - 1.0) * ka)
        b_o[...] = kkn * a


def _prep(pr, mup, mun, wmix, w0, a0, g2p, k_k, k_a, bd, seq, tm=256):
    t = pr.shape[0]
    nblk8 = t // SUBLANES
    per = tm // SUBLANES
    full = lambda a: pl.BlockSpec(a.shape, lambda i: (0,) * a.ndim)
    out = pl.BlockSpec((tm, RWKV_WIDTH), lambda i: (i, 0))
    return pl.pallas_call(
        functools.partial(_prep_kernel, tm=tm, seq=seq),
        grid=(t // tm,),
        in_specs=[
            pl.BlockSpec((tm, RWKV_COLS_PAD), lambda i: (i, 0)),
            pl.BlockSpec((SUBLANES, RWKV_COLS_PAD), lambda i: (jnp.maximum(i * per - 1, 0), 0)),
            pl.BlockSpec((SUBLANES, RWKV_COLS_PAD),
                         lambda i: (jnp.minimum((i + 1) * per, nblk8 - 1), 0)),
            full(mup), full(mun), full(wmix), full(w0), full(a0), full(g2p),
            full(k_k), full(k_a), full(bd),
        ],
        out_specs=[out] * 10,
        out_shape=[jax.ShapeDtypeStruct((t, RWKV_WIDTH), F32)] * 10,
        compiler_params=_cparams(("parallel",)),
        name="prep",
    )(pr, pr, pr, mup, mun, wmix, w0, a0, g2p, k_k, k_a, bd)


def _split3(x):
    hi = x.astype(BF16)
    r1 = x - hi.astype(F32)
    mid = r1.astype(BF16)
    lo = (r1 - mid.astype(F32)).astype(BF16)
    return hi, mid, lo


def _wkv_kernel(rf, vf, kkf, lwf, kdf, bf, rb, vb, kkb, lwb, kdb, bb, yf_ref, yb_ref, state):
    c = pl.program_id(1)
    n = WKV_CHUNK
    hd = HEAD_DIM
    nh = N_RWKV_HEADS
    def bmm(a, b, ca, cb):
        return lax.dot_general(a.astype(BF16), b.astype(BF16), (((ca,), (cb,)), ((0,), (0,))),
                               preferred_element_type=F32)

    @pl.when(c == 0)
    def _():
        state[...] = jnp.zeros_like(state)

    ti = lax.broadcasted_iota(jnp.int32, (n, n), 0)
    si = lax.broadcasted_iota(jnp.int32, (n, n), 1)
    dirs = ((rf, vf, kkf, lwf, kdf, bf, si <= ti, si < ti, n - 1),
            (rb, vb, kkb, lwb, kdb, bb, si >= ti, si > ti, 0))
    heads = lambda x: jnp.stack([x[:, j * hd:(j + 1) * hd] for j in range(nh)], 0)
    parts = []
    for r_ref, v_ref, kk_ref, lw_ref, kd_ref, b_ref, incl, strict, last in dirs:
        lw = lw_ref[0]
        tri = incl.astype(BF16)
        cum = sum(_dot(tri, part) for part in _split3(lw))
        e_neg = jnp.exp(-cum)
        e_last = jnp.exp(cum[last:last + 1, :] - cum)
        g_scale = jnp.exp(cum[last:last + 1, :])
        parts.append(dict(
            at=heads((-kk_ref[0] * jnp.exp(cum - lw)).astype(BF16)),
            rt=heads((r_ref[0] * jnp.exp(cum)).astype(BF16)),
            bt=heads((b_ref[0] * e_neg).astype(BF16)),
            kt=heads((kd_ref[0] * e_neg).astype(BF16)),
            bl=heads((b_ref[0] * e_last).astype(BF16)),
            kl=heads((kd_ref[0] * e_last).astype(BF16)),
            v=heads(v_ref[0].astype(BF16)),
            gs=heads(g_scale),
            incl=jnp.broadcast_to(incl[None], (nh, n, n)),
            strict=jnp.broadcast_to(strict[None], (nh, n, n))))
    cat = lambda key: jnp.concatenate([parts[0][key], parts[1][key]], 0)
    at, rt, bt, kt, bl, kl, v, gs = (cat(k) for k in ("at", "rt", "bt", "kt", "bl", "kl", "v", "gs"))
    incl, strict = cat("incl"), cat("strict")
    g0 = state[...]
    g0b = g0.astype(BF16)
    m1 = bmm(jnp.concatenate([at, rt], 1), jnp.concatenate([bt, kt], 1), 2, 2)
    a_ab = jnp.where(strict, m1[:, :n, :n], 0.0)
    a_ak = jnp.where(strict, m1[:, :n, n:], 0.0)
    a_rb = jnp.where(incl, m1[:, n:, :n], 0.0)
    a_rk = jnp.where(incl, m1[:, n:, n:], 0.0)
    tinv = jnp.where((ti == si)[None], 1.0, a_ab)
    pw = a_ab
    for _ in range(int(np.log2(n)) - 1):
        pw = bmm(pw, pw, 2, 1)
        tinv = tinv + bmm(tinv, pw, 2, 1)
    rhs = bmm(a_ak, v, 2, 1) + bmm(at, g0b, 2, 2)
    u = bmm(tinv, rhs, 2, 1)
    uv = jnp.concatenate([u.astype(BF16), v], 1)
    y = bmm(jnp.concatenate([a_rb, a_rk], 2), uv, 2, 1) + bmm(rt, g0b, 2, 2)
    state[...] = g0 * gs + bmm(uv, jnp.concatenate([bl, kl], 1), 1, 1)
    for d, y_ref in enumerate((yf_ref, yb_ref)):
        y_ref[0] = jnp.concatenate([y[d * nh + j] for j in range(nh)], axis=1)


def _wkv(r, v, kk, lw0, lw1, kd0, kd1, b0, b1, batch, seq):
    n = WKV_CHUNK
    nc = seq // n
    shp = (batch, seq, RWKV_WIDTH)
    arrs = [a.reshape(shp) for a in (r, v, kk, lw0, kd0, b0, r, v, kk, lw1, kd1, b1)]
    fwd = pl.BlockSpec((1, n, RWKV_WIDTH), lambda b, c: (b, c, 0))
    bwd = pl.BlockSpec((1, n, RWKV_WIDTH), lambda b, c: (b, nc - 1 - c, 0))
    yf, yb = pl.pallas_call(
        _wkv_kernel,
        grid=(batch, nc),
        in_specs=[fwd] * 6 + [bwd] * 6,
        out_specs=[fwd, bwd],
        out_shape=[jax.ShapeDtypeStruct(shp, F32)] * 2,
        scratch_shapes=[pltpu.VMEM((2 * N_RWKV_HEADS, HEAD_DIM, HEAD_DIM), F32)],
        compiler_params=_cparams(("parallel", "arbitrary")),
        name="wkv",
    )(*arrs)
    return yf.reshape(batch * seq, RWKV_WIDTH), yb.reshape(batch * seq, RWKV_WIDTH)


def _mix_kernel(x_ref, ya_ref, yf_ref, yb_ref, r_ref, v_ref, kd0_ref, kd1_ref, g_ref,
                lng_ref, lnb_ref, rk_ref, bd_ref, wout_ref, l1g_ref, l1b_ref, h_ref):
    bd = bd_ref[...].astype(BF16)
    head_sum = lambda t: sum(_dot(part, bd) for part in _split3(t))
    inv = 1.0 / HEAD_DIM
    y = yf_ref[...] + yb_ref[...]
    mu = head_sum(y) * inv
    yc = y - mu
    var = head_sum(yc * yc) * inv
    yn = yc * lax.rsqrt(var + RWKV_GN_EPS) * lng_ref[...] + lnb_ref[...]
    k_mean = 0.5 * (kd0_ref[...] + kd1_ref[...])
    v = v_ref[...]
    bonus = head_sum(r_ref[...] * k_mean * rk_ref[...]) * v
    yr = (yn + bonus) * g_ref[...]
    mix = (_dot(ya_ref[...].astype(BF16), wout_ref[:ATTN_WIDTH, :])
           + _dot(yr.astype(BF16), wout_ref[ATTN_WIDTH:, :]))
    h_ref[...] = _layer_norm(DEEPNORM_ALPHA * x_ref[...] + mix, l1g_ref[...], l1b_ref[...])


def _mix(x2, ya, yf, yb, r, v, kd0, kd1, g, lng, lnb, rk, bd, wout, l1g, l1b, tm=256):
    t, d = x2.shape
    full = lambda a: pl.BlockSpec(a.shape, lambda i: (0,) * a.ndim)
    half = pl.BlockSpec((tm, RWKV_WIDTH), lambda i: (i, 0))
    wide = pl.BlockSpec((tm, d), lambda i: (i, 0))
    return pl.pallas_call(
        _mix_kernel,
        grid=(t // tm,),
        in_specs=[wide] + [half] * 8 + [full(lng), full(lnb), full(rk), full(bd), full(wout),
                                        full(l1g), full(l1b)],
        out_specs=wide,
        out_shape=jax.ShapeDtypeStruct((t, d), F32),
        compiler_params=_cparams(("parallel",)),
        name="mix",
    )(x2, ya, yf, yb, r, v, kd0, kd1, g, lng, lnb, rk, bd, wout, l1g, l1b)


def _top_rows(scs, k, payloads=None):
    n = scs[0].shape[0]
    iota = lax.broadcasted_iota(jnp.int32, scs[0].shape, 0).astype(F32)
    scs = list(scs)
    vals = [[] for _ in scs]
    picks = [[] for _ in scs]
    for _ in range(k):
        for a, sc in enumerate(scs):
            m = jnp.max(sc, axis=0, keepdims=True)
            pos = jnp.min(jnp.where(sc == m, iota, float(n)), axis=0, keepdims=True)
            hit = iota == pos
            vals[a].append(m)
            if payloads is None:
                picks[a].append(pos)
            else:
                picks[a].append(jnp.max(jnp.where(hit, payloads[a], -1.0), axis=0, keepdims=True))
            scs[a] = jnp.where(hit, -jnp.inf, sc)
    return [(jnp.concatenate(v, 0), jnp.concatenate(p, 0)) for v, p in zip(vals, picks)]


def _route_kernel(h_ref, wq_ref, keys_ref, idx_ref, gate_ref, gate_tok_ref, q_scr, idx_scr):
    k = PEER_TOPK
    q_scr[...] = _dot(h_ref[...].astype(BF16), wq_ref[...])

    def head_group(hg, carry):
        heads = [hg * ROUTE_HEADS_PER_STEP + i for i in range(ROUTE_HEADS_PER_STEP)]
        scores = []
        for hh in heads:
            for p in range(2):
                col = pl.multiple_of(hh * PEER_QDIM + p * PEER_HALF, PEER_HALF)
                qp = q_scr[:, pl.ds(col, PEER_HALF)]
                scores.append(_dot_nt(keys_ref[hh, p], qp, HI))
        tops = _top_rows(scores, k)
        cands, cand_ids = [], []
        for a in range(len(heads)):
            (s1, i1), (s2, i2) = tops[2 * a], tops[2 * a + 1]
            cands.append(jnp.concatenate([s1[i:i + 1] + s2[:k // (i + 1)] for i in range(k)], 0))
            cand_ids.append(jnp.concatenate(
                [i1[i:i + 1] * PEER_NKEYS + i2[:k // (i + 1)] for i in range(k)], 0))
        for hh, (cs, ids) in zip(heads, _top_rows(cands, k, cand_ids)):
            e = jnp.exp(cs - jnp.max(cs, axis=0, keepdims=True))
            row = pl.multiple_of(hh * k, k)
            gate_ref[pl.ds(row, k), :] = e / jnp.sum(e, axis=0, keepdims=True)
            idx_scr[pl.ds(row, k), :] = ids.astype(jnp.int32)
        return carry

    lax.fori_loop(0, PEER_HEADS // ROUTE_HEADS_PER_STEP, head_group, 0)
    idx_ref[...] = idx_scr[...].T
    gate_tok_ref[...] = gate_ref[...].T


def _route(h1, wq, keys, tm=256):
    t, d = h1.shape
    return pl.pallas_call(
        _route_kernel,
        grid=(t // tm,),
        in_specs=[
            pl.BlockSpec((tm, d), lambda i: (i, 0)),
            pl.BlockSpec(wq.shape, lambda i: (0, 0)),
            pl.BlockSpec(keys.shape, lambda i: (0, 0, 0, 0)),
        ],
        out_specs=[
            pl.BlockSpec((tm, N_SEL), lambda i: (i, 0)),
            pl.BlockSpec((N_SEL, tm), lambda i: (0, i)),
            pl.BlockSpec((tm, N_SEL), lambda i: (i, 0)),
        ],
        out_shape=[
            jax.ShapeDtypeStruct((t, N_SEL), jnp.int32),
            jax.ShapeDtypeStruct((N_SEL, t), F32),
            jax.ShapeDtypeStruct((t, N_SEL), F32),
        ],
        scratch_shapes=[pltpu.VMEM((tm, PEER_HEADS * PEER_QDIM), F32),
                        pltpu.VMEM((N_SEL, tm), jnp.int32)],
        compiler_params=_cparams(("parallel",)),
        name="route",
    )(h1, wq, keys)


def _peer_kernel(idx_ref, gate_ref, h_ref, uv_hbm, l2g_ref, l2b_ref, o_ref, buf, f_scr, sem, *, tb):
    d = h_ref.shape[1]

    def issue(t, slot):
        for j in range(N_SEL):
            pltpu.make_async_copy(uv_hbm.at[pl.ds(idx_ref[t, j], 1)], buf.at[slot, pl.ds(j, 1)],
                                  sem.at[slot]).start()

    def wait_all(slot):
        pltpu.make_async_copy(uv_hbm.at[pl.ds(0, N_SEL)], buf.at[slot], sem.at[slot]).wait()

    issue(0, 0)
    lane = lax.broadcasted_iota(jnp.int32, (N_SEL, tb), 1)

    def body(t, carry):
        slot = t % 2

        @pl.when(t + 1 < tb)
        def _():
            issue(t + 1, 1 - slot)

        wait_all(slot)
        z = jnp.sum(buf[slot, :, 0:d] * h_ref[pl.ds(t, 1), :], axis=1, keepdims=True)
        gcol = jnp.sum(jnp.where(lane == t, gate_ref[...], 0.0), axis=1, keepdims=True)
        coef = gcol * (0.5 * z * (1.0 + lax.erf(z * (2.0 ** -0.5))))
        f_scr[pl.ds(t, 1), :] = jnp.sum(coef * buf[slot, :, d:2 * d], axis=0, keepdims=True)
        return carry

    lax.fori_loop(0, tb, body, 0)
    o_ref[...] = _layer_norm(DEEPNORM_ALPHA * h_ref[...] + f_scr[...], l2g_ref[...], l2b_ref[...])


def _peer(idx, gate_t, h1, uv, l2g, l2b, tok0, ntok, tb=PEER_TB):
    d = h1.shape[1]
    b0 = tok0 // tb
    return pl.pallas_call(
        functools.partial(_peer_kernel, tb=tb),
        grid=(ntok // tb,),
        in_specs=[
            pl.BlockSpec((tb, N_SEL), lambda i: (b0 + i, 0), memory_space=pltpu.SMEM),
            pl.BlockSpec((N_SEL, tb), lambda i: (0, b0 + i)),
            pl.BlockSpec((tb, d), lambda i: (b0 + i, 0)),
            pl.BlockSpec(memory_space=pl.ANY),
            pl.BlockSpec(l2g.shape, lambda i: (0, 0)),
            pl.BlockSpec(l2b.shape, lambda i: (0, 0)),
        ],
        out_specs=pl.BlockSpec((tb, d), lambda i: (i, 0)),
        out_shape=jax.ShapeDtypeStruct((ntok, d), F32),
        scratch_shapes=[pltpu.VMEM((2, N_SEL, 2 * d), F32),
                        pltpu.VMEM((tb, d), F32),
                        pltpu.SemaphoreType.DMA((2,))],
        compiler_params=_cparams(("arbitrary",)),
        name="peer",
    )(idx, gate_t, h1, uv, l2g, l2b)


def _sc_mesh():
    return plsc.VectorSubcoreMesh(core_axis_name="c", subcore_axis_name="s")


def _worker_id():
    return lax.axis_index("s") * 2 + lax.axis_index("c")


def _sc_z_body(u_hbm, idx_hbm, h_hbm, z_hbm, idx_v, h_v, rows_v, z_v, sem, *, tpw, d, tok0):
    ngrp = N_SEL // SC_GROUP
    nchunk = d // SC_LANES
    nsteps = tpw * ngrp
    base = _worker_id() * tpw

    def gather(tok_buf, g, buf):
        return pltpu.make_async_copy(u_hbm.at[idx_v.at[tok_buf, g]], rows_v.at[buf], sem.at[buf])

    pltpu.sync_copy(idx_hbm.at[tok0 + base], idx_v.at[0])
    gather(0, 0, 0).start()

    @pl.loop(0, nsteps)
    def _(q):
        t_loc = q // ngrp
        g = q % ngrp
        buf = q % 2
        t = base + t_loc

        @pl.when(g == 0)
        def _():
            pltpu.sync_copy(h_hbm.at[tok0 + t], h_v)

            @pl.loop(0, N_SEL)
            def _(r):
                z_v[pl.ds(pl.multiple_of(r * SC_LANES, SC_LANES), SC_LANES)] = jnp.zeros((SC_LANES,), F32)

        @pl.when(q + 1 < nsteps)
        def _():
            tn = (q + 1) // ngrp
            gn = (q + 1) % ngrp

            @pl.when(gn == 0)
            def _():
                pltpu.sync_copy(idx_hbm.at[tok0 + base + tn], idx_v.at[tn % 2])

            gather(tn % 2, gn, 1 - buf).start()

        gather(t_loc % 2, g, buf).wait()

        @pl.loop(0, SC_GROUP, step=4)
        def _(rb):
            @pl.loop(0, nchunk, step=SC_CHUNK_UNROLL)
            def _(c0):
                accs = [jnp.zeros((SC_LANES,), F32) for _ in range(4)]
                for cc in range(SC_CHUNK_UNROLL):
                    sl = pl.ds(pl.multiple_of((c0 + cc) * SC_LANES, SC_LANES), SC_LANES)
                    hc = h_v[sl]
                    for i in range(4):
                        accs[i] = accs[i] + rows_v[buf, rb + i, sl] * hc
                for i in range(4):
                    row = pl.multiple_of((g * SC_GROUP + rb + i) * SC_LANES, SC_LANES)
                    plsc.addupdate(z_v.at[pl.ds(row, SC_LANES)], accs[i])

        @pl.when(g == ngrp - 1)
        def _():
            pltpu.sync_copy(z_v, z_hbm.at[t])


def _sc_z(u, idx, h, tok0, ntok):
    t, d = h.shape
    tpw = ntok // SC_WORKERS
    idx4 = idx.reshape(t, N_SEL // SC_GROUP, SC_GROUP)
    body = functools.partial(_sc_z_body, tpw=tpw, d=d, tok0=tok0)
    return pl.kernel(
        body,
        out_type=jax.ShapeDtypeStruct((ntok, N_SEL * SC_LANES), F32),
        mesh=_sc_mesh(),
        scratch_types=[
            pltpu.VMEM((2, N_SEL // SC_GROUP, SC_GROUP), jnp.int32),
            pltpu.VMEM((d,), F32),
            pltpu.VMEM((2, SC_GROUP, d), F32),
            pltpu.VMEM((N_SEL * SC_LANES,), F32),
            pltpu.SemaphoreType.DMA((2,)),
        ],
        name="sc_z",
    )(u, idx4, h)


def _sc_out_body(v_hbm, idx_hbm, coef_hbm, f_hbm, idx_v, coef_v, rows_v, out_v, sem, *, tpw, d, tok0):
    ngrp = N_SEL // SC_GROUP
    nchunk = d // SC_LANES
    nsteps = tpw * ngrp
    base = _worker_id() * tpw

    def gather(tok_buf, g, buf):
        return pltpu.make_async_copy(v_hbm.at[idx_v.at[tok_buf, g]], rows_v.at[buf], sem.at[buf])

    pltpu.sync_copy(idx_hbm.at[tok0 + base], idx_v.at[0])
    gather(0, 0, 0).start()

    @pl.loop(0, nsteps)
    def _(q):
        t_loc = q // ngrp
        g = q % ngrp
        buf = q % 2
        t = base + t_loc

        @pl.when(g == 0)
        def _():
            pltpu.sync_copy(coef_hbm.at[t], coef_v)

        @pl.when(q + 1 < nsteps)
        def _():
            tn = (q + 1) // ngrp
            gn = (q + 1) % ngrp

            @pl.when(gn == 0)
            def _():
                pltpu.sync_copy(idx_hbm.at[tok0 + base + tn], idx_v.at[tn % 2])

            gather(tn % 2, gn, 1 - buf).start()

        gather(t_loc % 2, g, buf).wait()

        @pl.loop(0, nchunk, step=SC_OUT_CHUNKS)
        def _(c0):
            sls = [pl.ds(pl.multiple_of((c0 + cc) * SC_LANES, SC_LANES), SC_LANES)
                   for cc in range(SC_OUT_CHUNKS)]
            accs = [None] * SC_OUT_CHUNKS
            for r in range(SC_GROUP):
                cf = coef_v[pl.ds(pl.multiple_of((g * SC_GROUP + r) * SC_LANES, SC_LANES), SC_LANES)]
                for cc in range(SC_OUT_CHUNKS):
                    p = rows_v[buf, r, sls[cc]] * cf
                    accs[cc] = p if accs[cc] is None else accs[cc] + p
            for cc in range(SC_OUT_CHUNKS):
                @pl.when(g == 0)
                def _():
                    out_v[sls[cc]] = accs[cc]

                @pl.when(g != 0)
                def _():
                    plsc.addupdate(out_v.at[sls[cc]], accs[cc])

        @pl.when(g == ngrp - 1)
        def _():
            pltpu.sync_copy(out_v, f_hbm.at[t])


def _sc_out(v, idx, coef16, tok0):
    t = idx.shape[0]
    ntok = coef16.shape[0]
    d = v.shape[1]
    tpw = ntok // SC_WORKERS
    idx4 = idx.reshape(t, N_SEL // SC_GROUP, SC_GROUP)
    body = functools.partial(_sc_out_body, tpw=tpw, d=d, tok0=tok0)
    return pl.kernel(
        body,
        out_type=jax.ShapeDtypeStruct((ntok, d), F32),
        mesh=_sc_mesh(),
        scratch_types=[
            pltpu.VMEM((2, N_SEL // SC_GROUP, SC_GROUP), jnp.int32),
            pltpu.VMEM((N_SEL * SC_LANES,), F32),
            pltpu.VMEM((2, SC_GROUP, d), F32),
            pltpu.VMEM((d,), F32),
            pltpu.SemaphoreType.DMA((2,)),
        ],
        name="sc_out",
    )(v, idx4, coef16)


def _coef_kernel(zp_ref, gate_ref, sel_ref, selt_ref, after_ref, o_ref):
    z = sum(_dot(part, sel_ref[...]) for part in _split3(zp_ref[...]))
    coef = gate_ref[...] * (0.5 * z * (1.0 + lax.erf(z * (2.0 ** -0.5))))
    o_ref[...] = sum(_dot(part, selt_ref[...]) for part in _split3(coef))


def _coef(zp, gate_tok, tok0, after, tm=256):
    ntok, wide = zp.shape
    b0 = tok0 // tm
    grp = np.arange(wide) // SC_LANES
    sel = jnp.asarray(grp[:, None] == np.arange(N_SEL)[None, :], BF16)
    return pl.pallas_call(
        _coef_kernel,
        grid=(ntok // tm,),
        in_specs=[
            pl.BlockSpec((tm, wide), lambda i: (i, 0)),
            pl.BlockSpec((tm, N_SEL), lambda i: (b0 + i, 0)),
            pl.BlockSpec(sel.shape, lambda i: (0, 0)),
            pl.BlockSpec(sel.shape[::-1], lambda i: (0, 0)),
            pl.BlockSpec((SUBLANES, LANES), lambda i: (0, 0)),
        ],
        out_specs=pl.BlockSpec((tm, wide), lambda i: (i, 0)),
        out_shape=jax.ShapeDtypeStruct((ntok, wide), F32),
        compiler_params=_cparams(("parallel",)),
        name="coef",
    )(zp, gate_tok, sel, sel.T, after)


def _ln_out_kernel(h_ref, f_ref, g_ref, b_ref, o_ref):
    o_ref[...] = _layer_norm(DEEPNORM_ALPHA * h_ref[...] + f_ref[...], g_ref[...], b_ref[...])


def _ln_out(h1, f, l2g, l2b, tok0, tm=256):
    ntok, d = f.shape
    b0 = tok0 // tm
    return pl.pallas_call(
        _ln_out_kernel,
        grid=(ntok // tm,),
        in_specs=[
            pl.BlockSpec((tm, d), lambda i: (b0 + i, 0)),
            pl.BlockSpec((tm, d), lambda i: (i, 0)),
            pl.BlockSpec(l2g.shape, lambda i: (0, 0)),
            pl.BlockSpec(l2b.shape, lambda i: (0, 0)),
        ],
        out_specs=pl.BlockSpec((tm, d), lambda i: (i, 0)),
        out_shape=jax.ShapeDtypeStruct((ntok, d), F32),
        compiler_params=_cparams(("parallel",)),
        name="ln_out",
    )(h1, f, l2g, l2b)


def _rope_tables(seq):
    pos = jnp.arange(seq, dtype=F32)
    inv_freq = ROPE_THETA ** (-jnp.arange(0, HEAD_DIM, 2, dtype=F32) / HEAD_DIM)
    ang = pos[:, None] * inv_freq[None, :]
    ang = jnp.concatenate([ang, ang], -1)
    sign = jnp.concatenate([-jnp.ones((HEAD_DIM // 2,), F32), jnp.ones((HEAD_DIM // 2,), F32)])
    reps = LANES // HEAD_DIM
    return jnp.tile(jnp.cos(ang), (1, reps)), jnp.tile(jnp.sin(ang) * sign, (1, reps))


def _layer(h2d, batch, seq, w_in, mu_prev, mu_next, w0, w2, a0, a2, g2, k_k, k_a, r_k, lnx_g, lnx_b,
           sink, w_out, ln1_g, ln1_b, peer_wq, peer_keys, peer_u, peer_v, ln2_g, ln2_b):
    w = RWKV_WIDTH
    row = lambda a: a.reshape(1, -1).astype(F32)
    w_in_p = jnp.pad(w_in, ((0, 0), (0, RWKV_COLS_PAD - RWKV_COLS))).astype(BF16)
    mup = jnp.pad(mu_prev, (0, RWKV_COLS_PAD - RWKV_COLS)).reshape(1, -1)
    mun = jnp.pad(mu_next, (0, RWKV_COLS_PAD - RWKV_COLS)).reshape(1, -1)
    wmix = jnp.zeros((LANES, 4 * w), F32)
    for d in range(2):
        wmix = wmix.at[d * DECAY_RANK:(d + 1) * DECAY_RANK, d * w:(d + 1) * w].set(w2[d])
        r0 = 2 * DECAY_RANK + d * ICLR_RANK
        wmix = wmix.at[r0:r0 + ICLR_RANK, (2 + d) * w:(3 + d) * w].set(a2[d])
    g2p = jnp.pad(g2, ((0, LANES - GATE_RANK), (0, 0)))
    head_of = np.arange(w) // HEAD_DIM
    bd = jnp.asarray(head_of[:, None] == head_of[None, :], F32)
    cos_t, sin_t = _rope_tables(seq)

    wq_b, keys_f, wout_b = peer_wq.astype(BF16), peer_keys.astype(F32), w_out.astype(BF16)

    def to_routing(x2d, nb, after):
        qkv, pr = _proj(x2d, w_in_p, cos_t, sin_t, seq, after)
        y_attn = _attention(qkv.reshape(nb, seq, ATTN_COLS), sink.astype(F32))
        r, v, kk, g, lw0, lw1, kd0, kd1, b0, b1 = _prep(
            pr, mup, mun, wmix, w0.astype(F32), a0.astype(F32), g2p, row(k_k), row(k_a), bd, seq)
        yf, yb = _wkv(r, v, kk, lw0, lw1, kd0, kd1, b0, b1, nb, seq)
        h1 = _mix(x2d, y_attn.reshape(nb * seq, ATTN_WIDTH), yf, yb, r, v, kd0, kd1, g,
                  row(lnx_g), row(lnx_b), row(r_k), bd, wout_b, row(ln1_g), row(ln1_b))
        return (h1,) + tuple(_route(h1, wq_b, keys_f))

    uv = jnp.concatenate([peer_u, peer_v], axis=1)
    l2g, l2b = row(ln2_g), row(ln2_b)

    nb_1 = batch * PEER_FIRST_GROUP_SEQ_8THS // 8
    t_1 = nb_1 * seq
    t_2 = (batch - nb_1) * seq
    n_x = (t_2 * PEER_SECOND_GROUP_SC_16THS // 16) // (SC_WORKERS * SUBLANES) * (SC_WORKERS * SUBLANES)
    n_a = ((t_2 - n_x) * PEER_TC_FIRST_CALL_32NDS // 32) // PEER_TB * PEER_TB
    n_b = t_2 - n_x - n_a
    assert 0 < nb_1 < batch and t_1 % (SC_WORKERS * SUBLANES) == 0
    assert n_x > 0 and n_a > 0 and n_b > 0 and n_x % PEER_TB == 0 and n_b % PEER_TB == 0
    u32, v32 = peer_u.astype(F32), peer_v.astype(F32)
    h1_1, idx_1, _, gate_tok_1 = to_routing(h2d[:t_1], nb_1, cos_t)
    zp_1 = _sc_z(u32, idx_1, h1_1, 0, t_1)
    h1_2, idx_2, gate_t_2, gate_tok_2 = to_routing(h2d[t_1:], batch - nb_1, gate_tok_1)
    zp_2 = _sc_z(u32, idx_2, h1_2, 0, n_x)
    out_a = _peer(idx_2, gate_t_2, h1_2, uv, l2g, l2b, n_x, n_a)
    f_1 = _sc_out(v32, idx_1, _coef(zp_1, gate_tok_1, 0, out_a), 0)
    f_2 = _sc_out(v32, idx_2, _coef(zp_2, gate_tok_2, 0, out_a), 0)
    out_b = _peer(idx_2, gate_t_2, h1_2, uv, l2g, l2b, n_x + n_a, n_b)
    out_1 = _ln_out(h1_1, f_1, l2g, l2b, 0)
    out_2 = _ln_out(h1_2, f_2, l2g, l2b, 0)
    return jnp.concatenate([out_1, out_2, out_a, out_b], axis=0)


def kernel(x, w_in, mu_prev, mu_next, w0, w2, a0, a2, g2, k_k, k_a, r_k, lnx_g, lnx_b, sink, w_out,
           ln1_g, ln1_b, peer_wq, peer_keys, peer_u, peer_v, ln2_g, ln2_b):
    batch, seq, d = x.shape
    h = x.reshape(batch * seq, d)
    for l in range(DEPTH):
        h = _layer(h, batch, seq, w_in[l], mu_prev[l], mu_next[l], w0[l], w2[l], a0[l], a2[l], g2[l],
                   k_k[l], k_a[l], r_k[l], lnx_g[l], lnx_b[l], sink[l], w_out[l], ln1_g[l], ln1_b[l],
                   peer_wq[l], peer_keys[l], peer_u[l], peer_v[l], ln2_g[l], ln2_b[l])
    return h.reshape(batch, seq, d)
```

```python
import functools

import numpy as np
import jax
import jax.numpy as jnp
from jax import lax
from jax.experimental import pallas as pl
from jax.experimental.pallas import tpu as pltpu
from jax.experimental.pallas import tpu_sc as plsc

F32 = jnp.float32
BF16 = jnp.bfloat16
HI = lax.Precision.HIGHEST

HEAD_DIM = 64
N_Q_HEADS = 8
N_KV_HEADS = 2
Q_PER_KV = N_Q_HEADS // N_KV_HEADS
ATTN_WIDTH = N_Q_HEADS * HEAD_DIM
ATTN_KV_WIDTH = N_KV_HEADS * HEAD_DIM
ATTN_COLS = ATTN_WIDTH + 2 * ATTN_KV_WIDTH
WINDOW = 128
ATTN_BLOCK = 128
ROPE_THETA = 10000.0
N_RWKV_HEADS = 8
RWKV_WIDTH = N_RWKV_HEADS * HEAD_DIM
DECAY_RANK = 32
ICLR_RANK = 32
GATE_RANK = 96
RWKV_COLS = 3 * RWKV_WIDTH + 2 * DECAY_RANK + 2 * ICLR_RANK + GATE_RANK
RWKV_COLS_PAD = 1792
RWKV_GN_EPS = 64e-5
PEER_HEADS = 8
PEER_NKEYS = 128
PEER_QDIM = 256
PEER_HALF = PEER_QDIM // 2
PEER_TOPK = 16
N_SEL = PEER_HEADS * PEER_TOPK
LN_EPS = 1e-5
DEPTH = 1
DEEPNORM_ALPHA = (2.0 * DEPTH) ** 0.25

LANES = 128
SUBLANES = 8
WKV_CHUNK = 64
VMEM_LIMIT = 48 * 1024 * 1024
PEER_TB = 256
ROUTE_HEADS_PER_STEP = 8
SC_LANES = 16
SC_WORKERS = 32
SC_GROUP = 32
SC_CHUNK_UNROLL = 8
SC_OUT_CHUNKS = 4
PEER_SC_SEQ_SHARE_8THS = 5
PEER_TC_FIRST_CALL_32NDS = 13


def _cparams(sem):
    return pltpu.CompilerParams(dimension_semantics=sem, vmem_limit_bytes=VMEM_LIMIT)


def _dot(a, b, precision=None):
    return jnp.dot(a, b, preferred_element_type=F32, precision=precision)


def _dot_nt(a, b, precision=None):
    return lax.dot_general(a, b, (((1,), (1,)), ((), ())), preferred_element_type=F32,
                           precision=precision)


def _layer_norm(z, g, b):
    mu = jnp.mean(z, -1, keepdims=True)
    zc = z - mu
    var = jnp.mean(zc * zc, -1, keepdims=True)
    return zc * lax.rsqrt(var + LN_EPS) * g + b


def _proj_kernel(x_ref, w_ref, cos_ref, sin_ref, qkv_ref, pr_ref):
    xb = x_ref[...].astype(BF16)
    cos = cos_ref[...]
    sin = sin_ref[...]
    lane = lax.broadcasted_iota(jnp.int32, cos.shape, 1)
    first_half = (lane & (HEAD_DIM // 2)) == 0

    def rope(t):
        rot = jnp.where(first_half, pltpu.roll(t, LANES - HEAD_DIM // 2, 1),
                        pltpu.roll(t, HEAD_DIM // 2, 1))
        return t * cos + rot * sin

    for c in range(0, ATTN_COLS, 2 * LANES):
        acc = _dot(xb, w_ref[:, c:c + 2 * LANES])
        for half in range(2):
            col = c + half * LANES
            t = acc[:, half * LANES:(half + 1) * LANES]
            if col < ATTN_WIDTH + ATTN_KV_WIDTH:
                t = rope(t)
            qkv_ref[:, col:col + LANES] = t
    for c in range(0, RWKV_COLS_PAD, 2 * LANES):
        pr_ref[:, c:c + 2 * LANES] = _dot(xb, w_ref[:, ATTN_COLS + c:ATTN_COLS + c + 2 * LANES])


def _proj(x2, w_in_p, cos_t, sin_t, seq, tm=512):
    t, d = x2.shape
    n_pos = seq // tm
    return pl.pallas_call(
        _proj_kernel,
        grid=(t // tm,),
        in_specs=[
            pl.BlockSpec((tm, d), lambda i: (i, 0)),
            pl.BlockSpec(w_in_p.shape, lambda i: (0, 0)),
            pl.BlockSpec((tm, LANES), lambda i: (i % n_pos, 0)),
            pl.BlockSpec((tm, LANES), lambda i: (i % n_pos, 0)),
        ],
        out_specs=[
            pl.BlockSpec((tm, ATTN_COLS), lambda i: (i, 0)),
            pl.BlockSpec((tm, RWKV_COLS_PAD), lambda i: (i, 0)),
        ],
        out_shape=[
            jax.ShapeDtypeStruct((t, ATTN_COLS), F32),
            jax.ShapeDtypeStruct((t, RWKV_COLS_PAD), F32),
        ],
        compiler_params=_cparams(("parallel",)),
        name="proj",
    )(x2, w_in_p, cos_t, sin_t)


def _attn_kernel(sink_ref, q_ref, kp_ref, kc_ref, kn_ref, vp_ref, vc_ref, vn_ref, o_ref, *, nb):
    n = pl.program_id(1)
    blk = ATTN_BLOCK
    rows = Q_PER_KV * blk
    q = q_ref[0] * (HEAD_DIM ** -0.5)
    kwin = jnp.concatenate([kp_ref[0], kc_ref[0], kn_ref[0]], axis=0)
    vwin = jnp.concatenate([vp_ref[0], vc_ref[0], vn_ref[0]], axis=0)
    qi = lax.broadcasted_iota(jnp.int32, (rows, 3 * blk), 0) & (blk - 1)
    kj = lax.broadcasted_iota(jnp.int32, (rows, 3 * blk), 1)
    dist = kj - qi
    valid = ((dist >= blk - WINDOW) & (dist <= blk + WINDOW)
             & ((kj >= blk) | (n > 0)) & ((kj < 2 * blk) | (n < nb - 1)))
    rowg = lax.broadcasted_iota(jnp.int32, (rows, 1), 0) // blk
    outs = []
    for h in range(N_KV_HEADS):
        qs = jnp.concatenate(
            [q[:, (Q_PER_KV * h + g) * HEAD_DIM:(Q_PER_KV * h + g + 1) * HEAD_DIM]
             for g in range(Q_PER_KV)], axis=0)
        kh = kwin[:, h * HEAD_DIM:(h + 1) * HEAD_DIM]
        vh = vwin[:, h * HEAD_DIM:(h + 1) * HEAD_DIM]
        logits = _dot_nt(qs.astype(BF16), kh.astype(BF16))
        logits = jnp.where(valid, logits, -1e30)
        sk = jnp.zeros((rows, 1), F32)
        for g in range(Q_PER_KV):
            sk = jnp.where(rowg == g, sink_ref[Q_PER_KV * h + g], sk)
        m = jnp.maximum(jnp.max(logits, -1, keepdims=True), sk)
        e = jnp.exp(logits - m)
        den = jnp.sum(e, -1, keepdims=True) + jnp.exp(sk - m)
        p = e / den
        o = _dot(p.astype(BF16), vh.astype(BF16))
        for g in range(Q_PER_KV):
            outs.append(o[g * blk:(g + 1) * blk])
    o_ref[0] = jnp.concatenate(outs, axis=1)


def _attention(qkv3, sink):
    b, s, _ = qkv3.shape
    blk = ATTN_BLOCK
    nb = s // blk
    kcol = ATTN_WIDTH // LANES
    vcol = kcol + 1

    def spec(col, shift):
        def imap(bi, n):
            return (bi, jnp.clip(n + shift, 0, nb - 1), col)
        return pl.BlockSpec((1, blk, LANES), imap)

    return pl.pallas_call(
        functools.partial(_attn_kernel, nb=nb),
        grid=(b, nb),
        in_specs=[
            pl.BlockSpec(memory_space=pltpu.SMEM),
            pl.BlockSpec((1, blk, ATTN_WIDTH), lambda bi, n: (bi, n, 0)),
            spec(kcol, -1), spec(kcol, 0), spec(kcol, 1),
            spec(vcol, -1), spec(vcol, 0), spec(vcol, 1),
        ],
        out_specs=pl.BlockSpec((1, blk, ATTN_WIDTH), lambda bi, n: (bi, n, 0)),
        out_shape=jax.ShapeDtypeStruct((b, s, ATTN_WIDTH), F32),
        compiler_params=_cparams(("parallel", "parallel")),
        name="attn",
    )(sink, qkv3, qkv3, qkv3, qkv3, qkv3, qkv3, qkv3)


def _softplus(x):
    return jnp.maximum(x, 0.0) + jnp.log(1.0 + jnp.exp(-jnp.abs(x)))


def _sigmoid(x):
    return 1.0 / (1.0 + jnp.exp(-x))


def _prep_kernel(p_ref, hp_ref, hn_ref, mup_ref, mun_ref, wmix_ref, w0_ref, a0_ref, g2_ref,
                 kk_ref, ka_ref, bd_ref,
                 r_o, v_o, kk_o, g_o, lw0_o, lw1_o, kd0_o, kd1_o, b0_o, b1_o, *, tm, seq):
    i = pl.program_id(0)
    row = lax.broadcasted_iota(jnp.int32, (tm, 1), 0)
    seq_start = (i * tm) % seq == 0
    seq_end = ((i + 1) * tm) % seq == 0

    def shifted(c0, c1):
        p = p_ref[:, c0:c1]
        prev_row = jnp.where(seq_start, 0.0, hp_ref[SUBLANES - 1:SUBLANES, c0:c1])
        next_row = jnp.where(seq_end, 0.0, hn_ref[0:1, c0:c1])
        p_prev = jnp.where(row == 0, prev_row, pltpu.roll(p, 1, 0))
        p_next = jnp.where(row == tm - 1, next_row, pltpu.roll(p, tm - 1, 0))
        return p + mup_ref[:, c0:c1] * (p_prev - p) + mun_ref[:, c0:c1] * (p_next - p)

    w = RWKV_WIDTH
    r = shifted(0, w)
    k = shifted(w, 2 * w)
    v = shifted(2 * w, 3 * w)
    codes = shifted(3 * w, 3 * w + LANES)
    gd = shifted(3 * w + LANES, 3 * w + 2 * LANES)
    r_o[...] = r
    v_o[...] = v

    lane = lax.broadcasted_iota(jnp.int32, codes.shape, 1)
    codes = jnp.where(lane < 2 * DECAY_RANK, jnp.tanh(codes), codes)
    mm = _dot(codes, wmix_ref[...], HI)
    g_o[...] = _dot(_sigmoid(gd), g2_ref[...], HI)

    kkv = k * kk_ref[...]
    bd = bd_ref[...].astype(BF16)
    ss = sum(_dot(part, bd) for part in _split3(kkv * kkv))
    kkn = kkv * lax.rsqrt(jnp.maximum(ss, 1e-24))
    kk_o[...] = kkn

    ka = ka_ref[...]
    for d, (lw_o, kd_o, b_o) in enumerate(((lw0_o, kd0_o, b0_o), (lw1_o, kd1_o, b1_o))):
        w_log = -_softplus(-(w0_ref[d:d + 1, :] + mm[:, d * w:(d + 1) * w])) - 0.5
        lw_o[...] = -jnp.exp(w_log)
        a = _sigmoid(a0_ref[d:d + 1, :] + mm[:, (2 + d) * w:(3 + d) * w])
        kd_o[...] = k * (1.0 + (a - 1.0) * ka)
        b_o[...] = kkn * a


def _prep(pr, mup, mun, wmix, w0, a0, g2p, k_k, k_a, bd, seq, tm=256):
    t = pr.shape[0]
    nblk8 = t // SUBLANES
    per = tm // SUBLANES
    full = lambda a: pl.BlockSpec(a.shape, lambda i: (0,) * a.ndim)
    out = pl.BlockSpec((tm, RWKV_WIDTH), lambda i: (i, 0))
    return pl.pallas_call(
        functools.partial(_prep_kernel, tm=tm, seq=seq),
        grid=(t // tm,),
        in_specs=[
            pl.BlockSpec((tm, RWKV_COLS_PAD), lambda i: (i, 0)),
            pl.BlockSpec((SUBLANES, RWKV_COLS_PAD), lambda i: (jnp.maximum(i * per - 1, 0), 0)),
            pl.BlockSpec((SUBLANES, RWKV_COLS_PAD),
                         lambda i: (jnp.minimum((i + 1) * per, nblk8 - 1), 0)),
            full(mup), full(mun), full(wmix), full(w0), full(a0), full(g2p),
            full(k_k), full(k_a), full(bd),
        ],
        out_specs=[out] * 10,
        out_shape=[jax.ShapeDtypeStruct((t, RWKV_WIDTH), F32)] * 10,
        compiler_params=_cparams(("parallel",)),
        name="prep",
    )(pr, pr, pr, mup, mun, wmix, w0, a0, g2p, k_k, k_a, bd)


def _split3(x):
    hi = x.astype(BF16)
    r1 = x - hi.astype(F32)
    mid = r1.astype(BF16)
    lo = (r1 - mid.astype(F32)).astype(BF16)
    return hi, mid, lo


def _wkv_kernel(rf, vf, kkf, lwf, kdf, bf, rb, vb, kkb, lwb, kdb, bb, yf_ref, yb_ref, state):
    c = pl.program_id(1)
    n = WKV_CHUNK
    hd = HEAD_DIM
    nh = N_RWKV_HEADS
    def bmm(a, b, ca, cb):
        return lax.dot_general(a.astype(BF16), b.astype(BF16), (((ca,), (cb,)), ((0,), (0,))),
                               preferred_element_type=F32)

    @pl.when(c == 0)
    def _():
        state[...] = jnp.zeros_like(state)

    ti = lax.broadcasted_iota(jnp.int32, (n, n), 0)
    si = lax.broadcasted_iota(jnp.int32, (n, n), 1)
    dirs = ((rf, vf, kkf, lwf, kdf, bf, si <= ti, si < ti, n - 1),
            (rb, vb, kkb, lwb, kdb, bb, si >= ti, si > ti, 0))
    heads = lambda x: jnp.stack([x[:, j * hd:(j + 1) * hd] for j in range(nh)], 0)
    parts = []
    for r_ref, v_ref, kk_ref, lw_ref, kd_ref, b_ref, incl, strict, last in dirs:
        lw = lw_ref[0]
        tri = incl.astype(BF16)
        cum = sum(_dot(tri, part) for part in _split3(lw))
        e_neg = jnp.exp(-cum)
        e_last = jnp.exp(cum[last:last + 1, :] - cum)
        g_scale = jnp.exp(cum[last:last + 1, :])
        parts.append(dict(
            at=heads((-kk_ref[0] * jnp.exp(cum - lw)).astype(BF16)),
            rt=heads((r_ref[0] * jnp.exp(cum)).astype(BF16)),
            bt=heads((b_ref[0] * e_neg).astype(BF16)),
            kt=heads((kd_ref[0] * e_neg).astype(BF16)),
            bl=heads((b_ref[0] * e_last).astype(BF16)),
            kl=heads((kd_ref[0] * e_last).astype(BF16)),
            v=heads(v_ref[0].astype(BF16)),
            gs=heads(g_scale),
            incl=jnp.broadcast_to(incl[None], (nh, n, n)),
            strict=jnp.broadcast_to(strict[None], (nh, n, n))))
    cat = lambda key: jnp.concatenate([parts[0][key], parts[1][key]], 0)
    at, rt, bt, kt, bl, kl, v, gs = (cat(k) for k in ("at", "rt", "bt", "kt", "bl", "kl", "v", "gs"))
    incl, strict = cat("incl"), cat("strict")
    g0 = state[...]
    g0b = g0.astype(BF16)
    m1 = bmm(jnp.concatenate([at, rt], 1), jnp.concatenate([bt, kt], 1), 2, 2)
    a_ab = jnp.where(strict, m1[:, :n, :n], 0.0)
    a_ak = jnp.where(strict, m1[:, :n, n:], 0.0)
    a_rb = jnp.where(incl, m1[:, n:, :n], 0.0)
    a_rk = jnp.where(incl, m1[:, n:, n:], 0.0)
    tinv = jnp.where((ti == si)[None], 1.0, a_ab)
    pw = a_ab
    for _ in range(int(np.log2(n)) - 1):
        pw = bmm(pw, pw, 2, 1)
        tinv = tinv + bmm(tinv, pw, 2, 1)
    rhs = bmm(a_ak, v, 2, 1) + bmm(at, g0b, 2, 2)
    u = bmm(tinv, rhs, 2, 1)
    uv = jnp.concatenate([u.astype(BF16), v], 1)
    y = bmm(jnp.concatenate([a_rb, a_rk], 2), uv, 2, 1) + bmm(rt, g0b, 2, 2)
    state[...] = g0 * gs + bmm(uv, jnp.concatenate([bl, kl], 1), 1, 1)
    for d, y_ref in enumerate((yf_ref, yb_ref)):
        y_ref[0] = jnp.concatenate([y[d * nh + j] for j in range(nh)], axis=1)


def _wkv(r, v, kk, lw0, lw1, kd0, kd1, b0, b1, batch, seq):
    n = WKV_CHUNK
    nc = seq // n
    shp = (batch, seq, RWKV_WIDTH)
    arrs = [a.reshape(shp) for a in (r, v, kk, lw0, kd0, b0, r, v, kk, lw1, kd1, b1)]
    fwd = pl.BlockSpec((1, n, RWKV_WIDTH), lambda b, c: (b, c, 0))
    bwd = pl.BlockSpec((1, n, RWKV_WIDTH), lambda b, c: (b, nc - 1 - c, 0))
    yf, yb = pl.pallas_call(
        _wkv_kernel,
        grid=(batch, nc),
        in_specs=[fwd] * 6 + [bwd] * 6,
        out_specs=[fwd, bwd],
        out_shape=[jax.ShapeDtypeStruct(shp, F32)] * 2,
        scratch_shapes=[pltpu.VMEM((2 * N_RWKV_HEADS, HEAD_DIM, HEAD_DIM), F32)],
        compiler_params=_cparams(("parallel", "arbitrary")),
        name="wkv",
    )(*arrs)
    return yf.reshape(batch * seq, RWKV_WIDTH), yb.reshape(batch * seq, RWKV_WIDTH)


def _mix_kernel(x_ref, ya_ref, yf_ref, yb_ref, r_ref, v_ref, kd0_ref, kd1_ref, g_ref,
                lng_ref, lnb_ref, rk_ref, bd_ref, wout_ref, l1g_ref, l1b_ref, h_ref):
    bd = bd_ref[...].astype(BF16)
    head_sum = lambda t: sum(_dot(part, bd) for part in _split3(t))
    inv = 1.0 / HEAD_DIM
    y = yf_ref[...] + yb_ref[...]
    mu = head_sum(y) * inv
    yc = y - mu
    var = head_sum(yc * yc) * inv
    yn = yc * lax.rsqrt(var + RWKV_GN_EPS) * lng_ref[...] + lnb_ref[...]
    k_mean = 0.5 * (kd0_ref[...] + kd1_ref[...])
    v = v_ref[...]
    bonus = head_sum(r_ref[...] * k_mean * rk_ref[...]) * v
    yr = (yn + bonus) * g_ref[...]
    mix = (_dot(ya_ref[...].astype(BF16), wout_ref[:ATTN_WIDTH, :])
           + _dot(yr.astype(BF16), wout_ref[ATTN_WIDTH:, :]))
    h_ref[...] = _layer_norm(DEEPNORM_ALPHA * x_ref[...] + mix, l1g_ref[...], l1b_ref[...])


def _mix(x2, ya, yf, yb, r, v, kd0, kd1, g, lng, lnb, rk, bd, wout, l1g, l1b, tm=256):
    t, d = x2.shape
    full = lambda a: pl.BlockSpec(a.shape, lambda i: (0,) * a.ndim)
    half = pl.BlockSpec((tm, RWKV_WIDTH), lambda i: (i, 0))
    wide = pl.BlockSpec((tm, d), lambda i: (i, 0))
    return pl.pallas_call(
        _mix_kernel,
        grid=(t // tm,),
        in_specs=[wide] + [half] * 8 + [full(lng), full(lnb), full(rk), full(bd), full(wout),
                                        full(l1g), full(l1b)],
        out_specs=wide,
        out_shape=jax.ShapeDtypeStruct((t, d), F32),
        compiler_params=_cparams(("parallel",)),
        name="mix",
    )(x2, ya, yf, yb, r, v, kd0, kd1, g, lng, lnb, rk, bd, wout, l1g, l1b)


def _top_rows(scs, k, payloads=None):
    n = scs[0].shape[0]
    iota = lax.broadcasted_iota(jnp.int32, scs[0].shape, 0).astype(F32)
    scs = list(scs)
    vals = [[] for _ in scs]
    picks = [[] for _ in scs]
    for _ in range(k):
        for a, sc in enumerate(scs):
            m = jnp.max(sc, axis=0, keepdims=True)
            pos = jnp.min(jnp.where(sc == m, iota, float(n)), axis=0, keepdims=True)
            hit = iota == pos
            vals[a].append(m)
            if payloads is None:
                picks[a].append(pos)
            else:
                picks[a].append(jnp.max(jnp.where(hit, payloads[a], -1.0), axis=0, keepdims=True))
            scs[a] = jnp.where(hit, -jnp.inf, sc)
    return [(jnp.concatenate(v, 0), jnp.concatenate(p, 0)) for v, p in zip(vals, picks)]


def _route_kernel(h_ref, wq_ref, keys_ref, idx_ref, gate_ref, gate_tok_ref, q_scr, idx_scr):
    k = PEER_TOPK
    q_scr[...] = _dot(h_ref[...].astype(BF16), wq_ref[...])

    def head_group(hg, carry):
        heads = [hg * ROUTE_HEADS_PER_STEP + i for i in range(ROUTE_HEADS_PER_STEP)]
        scores = []
        for hh in heads:
            for p in range(2):
                col = pl.multiple_of(hh * PEER_QDIM + p * PEER_HALF, PEER_HALF)
                qp = q_scr[:, pl.ds(col, PEER_HALF)]
                scores.append(_dot_nt(keys_ref[hh, p], qp, HI))
        tops = _top_rows(scores, k)
        cands, cand_ids = [], []
        for a in range(len(heads)):
            (s1, i1), (s2, i2) = tops[2 * a], tops[2 * a + 1]
            cands.append(jnp.concatenate([s1[i:i + 1] + s2[:k // (i + 1)] for i in range(k)], 0))
            cand_ids.append(jnp.concatenate(
                [i1[i:i + 1] * PEER_NKEYS + i2[:k // (i + 1)] for i in range(k)], 0))
        for hh, (cs, ids) in zip(heads, _top_rows(cands, k, cand_ids)):
            e = jnp.exp(cs - jnp.max(cs, axis=0, keepdims=True))
            row = pl.multiple_of(hh * k, k)
            gate_ref[pl.ds(row, k), :] = e / jnp.sum(e, axis=0, keepdims=True)
            idx_scr[pl.ds(row, k), :] = ids.astype(jnp.int32)
        return carry

    lax.fori_loop(0, PEER_HEADS // ROUTE_HEADS_PER_STEP, head_group, 0)
    idx_ref[...] = idx_scr[...].T
    gate_tok_ref[...] = gate_ref[...].T


def _route(h1, wq, keys, tm=256):
    t, d = h1.shape
    return pl.pallas_call(
        _route_kernel,
        grid=(t // tm,),
        in_specs=[
            pl.BlockSpec((tm, d), lambda i: (i, 0)),
            pl.BlockSpec(wq.shape, lambda i: (0, 0)),
            pl.BlockSpec(keys.shape, lambda i: (0, 0, 0, 0)),
        ],
        out_specs=[
            pl.BlockSpec((tm, N_SEL), lambda i: (i, 0)),
            pl.BlockSpec((N_SEL, tm), lambda i: (0, i)),
            pl.BlockSpec((tm, N_SEL), lambda i: (i, 0)),
        ],
        out_shape=[
            jax.ShapeDtypeStruct((t, N_SEL), jnp.int32),
            jax.ShapeDtypeStruct((N_SEL, t), F32),
            jax.ShapeDtypeStruct((t, N_SEL), F32),
        ],
        scratch_shapes=[pltpu.VMEM((tm, PEER_HEADS * PEER_QDIM), F32),
                        pltpu.VMEM((N_SEL, tm), jnp.int32)],
        compiler_params=_cparams(("parallel",)),
        name="route",
    )(h1, wq, keys)


def _peer_kernel(idx_ref, gate_ref, h_ref, uv_hbm, l2g_ref, l2b_ref, o_ref, buf, f_scr, sem, *, tb):
    d = h_ref.shape[1]

    def issue(t, slot):
        for j in range(N_SEL):
            pltpu.make_async_copy(uv_hbm.at[pl.ds(idx_ref[t, j], 1)], buf.at[slot, pl.ds(j, 1)],
                                  sem.at[slot]).start()

    def wait_all(slot):
        pltpu.make_async_copy(uv_hbm.at[pl.ds(0, N_SEL)], buf.at[slot], sem.at[slot]).wait()

    issue(0, 0)
    lane = lax.broadcasted_iota(jnp.int32, (N_SEL, tb), 1)

    def body(t, carry):
        slot = t % 2

        @pl.when(t + 1 < tb)
        def _():
            issue(t + 1, 1 - slot)

        wait_all(slot)
        z = jnp.sum(buf[slot, :, 0:d] * h_ref[pl.ds(t, 1), :], axis=1, keepdims=True)
        gcol = jnp.sum(jnp.where(lane == t, gate_ref[...], 0.0), axis=1, keepdims=True)
        coef = gcol * (0.5 * z * (1.0 + lax.erf(z * (2.0 ** -0.5))))
        f_scr[pl.ds(t, 1), :] = jnp.sum(coef * buf[slot, :, d:2 * d], axis=0, keepdims=True)
        return carry

    lax.fori_loop(0, tb, body, 0)
    o_ref[...] = _layer_norm(DEEPNORM_ALPHA * h_ref[...] + f_scr[...], l2g_ref[...], l2b_ref[...])


def _peer(idx, gate_t, h1, uv, l2g, l2b, tok0, ntok, tb=PEER_TB):
    d = h1.shape[1]
    b0 = tok0 // tb
    return pl.pallas_call(
        functools.partial(_peer_kernel, tb=tb),
        grid=(ntok // tb,),
        in_specs=[
            pl.BlockSpec((tb, N_SEL), lambda i: (b0 + i, 0), memory_space=pltpu.SMEM),
            pl.BlockSpec((N_SEL, tb), lambda i: (0, b0 + i)),
            pl.BlockSpec((tb, d), lambda i: (b0 + i, 0)),
            pl.BlockSpec(memory_space=pl.ANY),
            pl.BlockSpec(l2g.shape, lambda i: (0, 0)),
            pl.BlockSpec(l2b.shape, lambda i: (0, 0)),
        ],
        out_specs=pl.BlockSpec((tb, d), lambda i: (i, 0)),
        out_shape=jax.ShapeDtypeStruct((ntok, d), F32),
        scratch_shapes=[pltpu.VMEM((2, N_SEL, 2 * d), F32),
                        pltpu.VMEM((tb, d), F32),
                        pltpu.SemaphoreType.DMA((2,))],
        compiler_params=_cparams(("arbitrary",)),
        name="peer",
    )(idx, gate_t, h1, uv, l2g, l2b)


def _sc_mesh():
    return plsc.VectorSubcoreMesh(core_axis_name="c", subcore_axis_name="s")


def _worker_id():
    return lax.axis_index("s") * 2 + lax.axis_index("c")


def _sc_z_body(u_hbm, idx_hbm, h_hbm, z_hbm, idx_v, h_v, rows_v, z_v, sem, *, tpw, d, tok0):
    ngrp = N_SEL // SC_GROUP
    nchunk = d // SC_LANES
    nsteps = tpw * ngrp
    base = _worker_id() * tpw

    def gather(tok_buf, g, buf):
        return pltpu.make_async_copy(u_hbm.at[idx_v.at[tok_buf, g]], rows_v.at[buf], sem.at[buf])

    pltpu.sync_copy(idx_hbm.at[tok0 + base], idx_v.at[0])
    gather(0, 0, 0).start()

    @pl.loop(0, nsteps)
    def _(q):
        t_loc = q // ngrp
        g = q % ngrp
        buf = q % 2
        t = base + t_loc

        @pl.when(g == 0)
        def _():
            pltpu.sync_copy(h_hbm.at[tok0 + t], h_v)

            @pl.loop(0, N_SEL)
            def _(r):
                z_v[pl.ds(pl.multiple_of(r * SC_LANES, SC_LANES), SC_LANES)] = jnp.zeros((SC_LANES,), F32)

        @pl.when(q + 1 < nsteps)
        def _():
            tn = (q + 1) // ngrp
            gn = (q + 1) % ngrp

            @pl.when(gn == 0)
            def _():
                pltpu.sync_copy(idx_hbm.at[tok0 + base + tn], idx_v.at[tn % 2])

            gather(tn % 2, gn, 1 - buf).start()

        gather(t_loc % 2, g, buf).wait()

        @pl.loop(0, SC_GROUP, step=4)
        def _(rb):
            @pl.loop(0, nchunk, step=SC_CHUNK_UNROLL)
            def _(c0):
                accs = [jnp.zeros((SC_LANES,), F32) for _ in range(4)]
                for cc in range(SC_CHUNK_UNROLL):
                    sl = pl.ds(pl.multiple_of((c0 + cc) * SC_LANES, SC_LANES), SC_LANES)
                    hc = h_v[sl]
                    for i in range(4):
                        accs[i] = accs[i] + rows_v[buf, rb + i, sl] * hc
                for i in range(4):
                    row = pl.multiple_of((g * SC_GROUP + rb + i) * SC_LANES, SC_LANES)
                    plsc.addupdate(z_v.at[pl.ds(row, SC_LANES)], accs[i])

        @pl.when(g == ngrp - 1)
        def _():
            pltpu.sync_copy(z_v, z_hbm.at[t])


def _sc_z(u, idx, h, tok0, ntok):
    t, d = h.shape
    tpw = ntok // SC_WORKERS
    idx4 = idx.reshape(t, N_SEL // SC_GROUP, SC_GROUP)
    body = functools.partial(_sc_z_body, tpw=tpw, d=d, tok0=tok0)
    return pl.kernel(
        body,
        out_type=jax.ShapeDtypeStruct((ntok, N_SEL * SC_LANES), F32),
        mesh=_sc_mesh(),
        scratch_types=[
            pltpu.VMEM((2, N_SEL // SC_GROUP, SC_GROUP), jnp.int32),
            pltpu.VMEM((d,), F32),
            pltpu.VMEM((2, SC_GROUP, d), F32),
            pltpu.VMEM((N_SEL * SC_LANES,), F32),
            pltpu.SemaphoreType.DMA((2,)),
        ],
        name="sc_z",
    )(u, idx4, h)


def _sc_out_body(v_hbm, idx_hbm, coef_hbm, f_hbm, idx_v, coef_v, rows_v, out_v, sem, *, tpw, d, tok0):
    ngrp = N_SEL // SC_GROUP
    nchunk = d // SC_LANES
    nsteps = tpw * ngrp
    base = _worker_id() * tpw

    def gather(tok_buf, g, buf):
        return pltpu.make_async_copy(v_hbm.at[idx_v.at[tok_buf, g]], rows_v.at[buf], sem.at[buf])

    pltpu.sync_copy(idx_hbm.at[tok0 + base], idx_v.at[0])
    gather(0, 0, 0).start()

    @pl.loop(0, nsteps)
    def _(q):
        t_loc = q // ngrp
        g = q % ngrp
        buf = q % 2
        t = base + t_loc

        @pl.when(g == 0)
        def _():
            pltpu.sync_copy(coef_hbm.at[t], coef_v)

        @pl.when(q + 1 < nsteps)
        def _():
            tn = (q + 1) // ngrp
            gn = (q + 1) % ngrp

            @pl.when(gn == 0)
            def _():
                pltpu.sync_copy(idx_hbm.at[tok0 + base + tn], idx_v.at[tn % 2])

            gather(tn % 2, gn, 1 - buf).start()

        gather(t_loc % 2, g, buf).wait()

        @pl.loop(0, nchunk, step=SC_OUT_CHUNKS)
        def _(c0):
            sls = [pl.ds(pl.multiple_of((c0 + cc) * SC_LANES, SC_LANES), SC_LANES)
                   for cc in range(SC_OUT_CHUNKS)]
            accs = [None] * SC_OUT_CHUNKS
            for r in range(SC_GROUP):
                cf = coef_v[pl.ds(pl.multiple_of((g * SC_GROUP + r) * SC_LANES, SC_LANES), SC_LANES)]
                for cc in range(SC_OUT_CHUNKS):
                    p = rows_v[buf, r, sls[cc]] * cf
                    accs[cc] = p if accs[cc] is None else accs[cc] + p
            for cc in range(SC_OUT_CHUNKS):
                @pl.when(g == 0)
                def _():
                    out_v[sls[cc]] = accs[cc]

                @pl.when(g != 0)
                def _():
                    plsc.addupdate(out_v.at[sls[cc]], accs[cc])

        @pl.when(g == ngrp - 1)
        def _():
            pltpu.sync_copy(out_v, f_hbm.at[t])


def _sc_out(v, idx, coef16, tok0):
    t = idx.shape[0]
    ntok = coef16.shape[0]
    d = v.shape[1]
    tpw = ntok // SC_WORKERS
    idx4 = idx.reshape(t, N_SEL // SC_GROUP, SC_GROUP)
    body = functools.partial(_sc_out_body, tpw=tpw, d=d, tok0=tok0)
    return pl.kernel(
        body,
        out_type=jax.ShapeDtypeStruct((ntok, d), F32),
        mesh=_sc_mesh(),
        scratch_types=[
            pltpu.VMEM((2, N_SEL // SC_GROUP, SC_GROUP), jnp.int32),
            pltpu.VMEM((N_SEL * SC_LANES,), F32),
            pltpu.VMEM((2, SC_GROUP, d), F32),
            pltpu.VMEM((d,), F32),
            pltpu.SemaphoreType.DMA((2,)),
        ],
        name="sc_out",
    )(v, idx4, coef16)


def _coef_kernel(zp_ref, gate_ref, sel_ref, selt_ref, after_ref, o_ref):
    z = sum(_dot(part, sel_ref[...]) for part in _split3(zp_ref[...]))
    coef = gate_ref[...] * (0.5 * z * (1.0 + lax.erf(z * (2.0 ** -0.5))))
    o_ref[...] = sum(_dot(part, selt_ref[...]) for part in _split3(coef))


def _coef(zp, gate_tok, tok0, after, tm=256):
    ntok, wide = zp.shape
    b0 = tok0 // tm
    grp = np.arange(wide) // SC_LANES
    sel = jnp.asarray(grp[:, None] == np.arange(N_SEL)[None, :], BF16)
    return pl.pallas_call(
        _coef_kernel,
        grid=(ntok // tm,),
        in_specs=[
            pl.BlockSpec((tm, wide), lambda i: (i, 0)),
            pl.BlockSpec((tm, N_SEL), lambda i: (b0 + i, 0)),
            pl.BlockSpec(sel.shape, lambda i: (0, 0)),
            pl.BlockSpec(sel.shape[::-1], lambda i: (0, 0)),
            pl.BlockSpec((SUBLANES, LANES), lambda i: (0, 0)),
        ],
        out_specs=pl.BlockSpec((tm, wide), lambda i: (i, 0)),
        out_shape=jax.ShapeDtypeStruct((ntok, wide), F32),
        compiler_params=_cparams(("parallel",)),
        name="coef",
    )(zp, gate_tok, sel, sel.T, after)


def _ln_out_kernel(h_ref, f_ref, g_ref, b_ref, o_ref):
    o_ref[...] = _layer_norm(DEEPNORM_ALPHA * h_ref[...] + f_ref[...], g_ref[...], b_ref[...])


def _ln_out(h1, f, l2g, l2b, tok0, tm=256):
    ntok, d = f.shape
    b0 = tok0 // tm
    return pl.pallas_call(
        _ln_out_kernel,
        grid=(ntok // tm,),
        in_specs=[
            pl.BlockSpec((tm, d), lambda i: (b0 + i, 0)),
            pl.BlockSpec((tm, d), lambda i: (i, 0)),
            pl.BlockSpec(l2g.shape, lambda i: (0, 0)),
            pl.BlockSpec(l2b.shape, lambda i: (0, 0)),
        ],
        out_specs=pl.BlockSpec((tm, d), lambda i: (i, 0)),
        out_shape=jax.ShapeDtypeStruct((ntok, d), F32),
        compiler_params=_cparams(("parallel",)),
        name="ln_out",
    )(h1, f, l2g, l2b)


def _rope_tables(seq):
    pos = jnp.arange(seq, dtype=F32)
    inv_freq = ROPE_THETA ** (-jnp.arange(0, HEAD_DIM, 2, dtype=F32) / HEAD_DIM)
    ang = pos[:, None] * inv_freq[None, :]
    ang = jnp.concatenate([ang, ang], -1)
    sign = jnp.concatenate([-jnp.ones((HEAD_DIM // 2,), F32), jnp.ones((HEAD_DIM // 2,), F32)])
    reps = LANES // HEAD_DIM
    return jnp.tile(jnp.cos(ang), (1, reps)), jnp.tile(jnp.sin(ang) * sign, (1, reps))


def _layer(h2d, batch, seq, w_in, mu_prev, mu_next, w0, w2, a0, a2, g2, k_k, k_a, r_k, lnx_g, lnx_b,
           sink, w_out, ln1_g, ln1_b, peer_wq, peer_keys, peer_u, peer_v, ln2_g, ln2_b):
    w = RWKV_WIDTH
    row = lambda a: a.reshape(1, -1).astype(F32)
    w_in_p = jnp.pad(w_in, ((0, 0), (0, RWKV_COLS_PAD - RWKV_COLS))).astype(BF16)
    mup = jnp.pad(mu_prev, (0, RWKV_COLS_PAD - RWKV_COLS)).reshape(1, -1)
    mun = jnp.pad(mu_next, (0, RWKV_COLS_PAD - RWKV_COLS)).reshape(1, -1)
    wmix = jnp.zeros((LANES, 4 * w), F32)
    for d in range(2):
        wmix = wmix.at[d * DECAY_RANK:(d + 1) * DECAY_RANK, d * w:(d + 1) * w].set(w2[d])
        r0 = 2 * DECAY_RANK + d * ICLR_RANK
        wmix = wmix.at[r0:r0 + ICLR_RANK, (2 + d) * w:(3 + d) * w].set(a2[d])
    g2p = jnp.pad(g2, ((0, LANES - GATE_RANK), (0, 0)))
    head_of = np.arange(w) // HEAD_DIM
    bd = jnp.asarray(head_of[:, None] == head_of[None, :], F32)
    cos_t, sin_t = _rope_tables(seq)

    wq_b, keys_f, wout_b = peer_wq.astype(BF16), peer_keys.astype(F32), w_out.astype(BF16)

    def to_routing(x2d, nb):
        qkv, pr = _proj(x2d, w_in_p, cos_t, sin_t, seq)
        y_attn = _attention(qkv.reshape(nb, seq, ATTN_COLS), sink.astype(F32))
        r, v, kk, g, lw0, lw1, kd0, kd1, b0, b1 = _prep(
            pr, mup, mun, wmix, w0.astype(F32), a0.astype(F32), g2p, row(k_k), row(k_a), bd, seq)
        yf, yb = _wkv(r, v, kk, lw0, lw1, kd0, kd1, b0, b1, nb, seq)
        h1 = _mix(x2d, y_attn.reshape(nb * seq, ATTN_WIDTH), yf, yb, r, v, kd0, kd1, g,
                  row(lnx_g), row(lnx_b), row(r_k), bd, wout_b, row(ln1_g), row(ln1_b))
        return (h1,) + tuple(_route(h1, wq_b, keys_f))

    uv = jnp.concatenate([peer_u, peer_v], axis=1)
    l2g, l2b = row(ln2_g), row(ln2_b)

    nb_sc = batch * PEER_SC_SEQ_SHARE_8THS // 8
    t_sc = nb_sc * seq
    t_tc = (batch - nb_sc) * seq
    n_a = (t_tc * PEER_TC_FIRST_CALL_32NDS // 32) // PEER_TB * PEER_TB
    n_b = t_tc - n_a
    assert 0 < nb_sc < batch and t_sc % (SC_WORKERS * SUBLANES) == 0 and n_a > 0 and n_b % PEER_TB == 0
    h1s, idxs, _, gate_tok_s = to_routing(h2d[:t_sc], nb_sc)
    zp = _sc_z(peer_u.astype(F32), idxs, h1s, 0, t_sc)
    h1t, idxt, gate_t_t, _ = to_routing(h2d[t_sc:], batch - nb_sc)
    out_a = _peer(idxt, gate_t_t, h1t, uv, l2g, l2b, 0, n_a)
    f_sc = _sc_out(peer_v.astype(F32), idxs, _coef(zp, gate_tok_s, 0, out_a), 0)
    out_b = _peer(idxt, gate_t_t, h1t, uv, l2g, l2b, n_a, n_b)
    out_sc = _ln_out(h1s, f_sc, l2g, l2b, 0)
    return jnp.concatenate([out_sc, out_a, out_b], axis=0)


def kernel(x, w_in, mu_prev, mu_next, w0, w2, a0, a2, g2, k_k, k_a, r_k, lnx_g, lnx_b, sink, w_out,
           ln1_g, ln1_b, peer_wq, peer_keys, peer_u, peer_v, ln2_g, ln2_b):
    batch, seq, d = x.shape
    h = x.reshape(batch * seq, d)
    for l in range(DEPTH):
        h = _layer(h, batch, seq, w_in[l], mu_prev[l], mu_next[l], w0[l], w2[l], a0[l], a2[l], g2[l],
                   k_k[l], k_a[l], r_k[l], lnx_g[l], lnx_b[l], sink[l], w_out[l], ln1_g[l], ln1_b[l],
                   peer_wq[l], peer_keys[l], peer_u[l], peer_v[l], ln2_g[l], ln2_b[l])
    return h.reshape(batch, seq, d)
```

```python
import functools

import numpy as np
import jax
import jax.numpy as jnp
from jax import lax
from jax.experimental import pallas as pl
from jax.experimental.pallas import tpu as pltpu
from jax.experimental.pallas import tpu_sc as plsc

F32 = jnp.float32
BF16 = jnp.bfloat16
HI = lax.Precision.HIGHEST

HEAD_DIM = 64
N_Q_HEADS = 8
N_KV_HEADS = 2
Q_PER_KV = N_Q_HEADS // N_KV_HEADS
ATTN_WIDTH = N_Q_HEADS * HEAD_DIM
ATTN_KV_WIDTH = N_KV_HEADS * HEAD_DIM
ATTN_COLS = ATTN_WIDTH + 2 * ATTN_KV_WIDTH
WINDOW = 128
ATTN_BLOCK = 128
ROPE_THETA = 10000.0
N_RWKV_HEADS = 8
RWKV_WIDTH = N_RWKV_HEADS * HEAD_DIM
DECAY_RANK = 32
ICLR_RANK = 32
GATE_RANK = 96
RWKV_COLS = 3 * RWKV_WIDTH + 2 * DECAY_RANK + 2 * ICLR_RANK + GATE_RANK
RWKV_COLS_PAD = 1792
RWKV_GN_EPS = 64e-5
PEER_HEADS = 8
PEER_NKEYS = 128
PEER_QDIM = 256
PEER_HALF = PEER_QDIM // 2
PEER_TOPK = 16
N_SEL = PEER_HEADS * PEER_TOPK
LN_EPS = 1e-5
DEPTH = 1
DEEPNORM_ALPHA = (2.0 * DEPTH) ** 0.25

LANES = 128
SUBLANES = 8
PREP_OUT_DTYPES = (BF16, BF16, BF16, F32, F32, F32, BF16, BF16, BF16, BF16)
WKV_CHUNK = 64
VMEM_LIMIT = 48 * 1024 * 1024
PEER_TB = 256
ROUTE_HEADS_PER_STEP = 8
SC_LANES = 16
SC_WORKERS = 32
SC_GROUP = 32
SC_CHUNK_UNROLL = 8
SC_OUT_CHUNKS = 4
PEER_SC_SEQ_SHARE_8THS = 5
PEER_TC_FIRST_CALL_32NDS = 13


def _cparams(sem):
    return pltpu.CompilerParams(dimension_semantics=sem, vmem_limit_bytes=VMEM_LIMIT)


def _dot(a, b, precision=None):
    return jnp.dot(a, b, preferred_element_type=F32, precision=precision)


def _dot_nt(a, b, precision=None):
    return lax.dot_general(a, b, (((1,), (1,)), ((), ())), preferred_element_type=F32,
                           precision=precision)


def _layer_norm(z, g, b):
    mu = jnp.mean(z, -1, keepdims=True)
    zc = z - mu
    var = jnp.mean(zc * zc, -1, keepdims=True)
    return zc * lax.rsqrt(var + LN_EPS) * g + b


def _proj_kernel(x_ref, w_ref, cos_ref, sin_ref, qkv_ref, pr_ref):
    xb = x_ref[...].astype(BF16)
    cos = cos_ref[...]
    sin = sin_ref[...]
    lane = lax.broadcasted_iota(jnp.int32, cos.shape, 1)
    first_half = (lane & (HEAD_DIM // 2)) == 0

    def rope(t):
        rot = jnp.where(first_half, pltpu.roll(t, LANES - HEAD_DIM // 2, 1),
                        pltpu.roll(t, HEAD_DIM // 2, 1))
        return t * cos + rot * sin

    for c in range(0, ATTN_COLS, 2 * LANES):
        acc = _dot(xb, w_ref[:, c:c + 2 * LANES])
        for half in range(2):
            col = c + half * LANES
            t = acc[:, half * LANES:(half + 1) * LANES]
            if col < ATTN_WIDTH + ATTN_KV_WIDTH:
                t = rope(t)
            qkv_ref[:, col:col + LANES] = t
    for c in range(0, RWKV_COLS_PAD, 2 * LANES):
        pr_ref[:, c:c + 2 * LANES] = _dot(xb, w_ref[:, ATTN_COLS + c:ATTN_COLS + c + 2 * LANES])


def _proj(x2, w_in_p, cos_t, sin_t, seq, tm=512):
    t, d = x2.shape
    n_pos = seq // tm
    return pl.pallas_call(
        _proj_kernel,
        grid=(t // tm,),
        in_specs=[
            pl.BlockSpec((tm, d), lambda i: (i, 0)),
            pl.BlockSpec(w_in_p.shape, lambda i: (0, 0)),
            pl.BlockSpec((tm, LANES), lambda i: (i % n_pos, 0)),
            pl.BlockSpec((tm, LANES), lambda i: (i % n_pos, 0)),
        ],
        out_specs=[
            pl.BlockSpec((tm, ATTN_COLS), lambda i: (i, 0)),
            pl.BlockSpec((tm, RWKV_COLS_PAD), lambda i: (i, 0)),
        ],
        out_shape=[
            jax.ShapeDtypeStruct((t, ATTN_COLS), F32),
            jax.ShapeDtypeStruct((t, RWKV_COLS_PAD), F32),
        ],
        compiler_params=_cparams(("parallel",)),
        name="proj",
    )(x2, w_in_p, cos_t, sin_t)


def _attn_kernel(sink_ref, q_ref, kp_ref, kc_ref, kn_ref, vp_ref, vc_ref, vn_ref, o_ref, *, nb):
    n = pl.program_id(1)
    blk = ATTN_BLOCK
    rows = Q_PER_KV * blk
    q = q_ref[0] * (HEAD_DIM ** -0.5)
    kwin = jnp.concatenate([kp_ref[0], kc_ref[0], kn_ref[0]], axis=0)
    vwin = jnp.concatenate([vp_ref[0], vc_ref[0], vn_ref[0]], axis=0)
    qi = lax.broadcasted_iota(jnp.int32, (rows, 3 * blk), 0) & (blk - 1)
    kj = lax.broadcasted_iota(jnp.int32, (rows, 3 * blk), 1)
    dist = kj - qi
    valid = ((dist >= blk - WINDOW) & (dist <= blk + WINDOW)
             & ((kj >= blk) | (n > 0)) & ((kj < 2 * blk) | (n < nb - 1)))
    rowg = lax.broadcasted_iota(jnp.int32, (rows, 1), 0) // blk
    outs = []
    for h in range(N_KV_HEADS):
        qs = jnp.concatenate(
            [q[:, (Q_PER_KV * h + g) * HEAD_DIM:(Q_PER_KV * h + g + 1) * HEAD_DIM]
             for g in range(Q_PER_KV)], axis=0)
        kh = kwin[:, h * HEAD_DIM:(h + 1) * HEAD_DIM]
        vh = vwin[:, h * HEAD_DIM:(h + 1) * HEAD_DIM]
        logits = _dot_nt(qs.astype(BF16), kh.astype(BF16))
        logits = jnp.where(valid, logits, -1e30)
        sk = jnp.zeros((rows, 1), F32)
        for g in range(Q_PER_KV):
            sk = jnp.where(rowg == g, sink_ref[Q_PER_KV * h + g], sk)
        m = jnp.maximum(jnp.max(logits, -1, keepdims=True), sk)
        e = jnp.exp(logits - m)
        den = jnp.sum(e, -1, keepdims=True) + jnp.exp(sk - m)
        p = e / den
        o = _dot(p.astype(BF16), vh.astype(BF16))
        for g in range(Q_PER_KV):
            outs.append(o[g * blk:(g + 1) * blk])
    o_ref[0] = jnp.concatenate(outs, axis=1)


def _attention(qkv3, sink):
    b, s, _ = qkv3.shape
    blk = ATTN_BLOCK
    nb = s // blk
    kcol = ATTN_WIDTH // LANES
    vcol = kcol + 1

    def spec(col, shift):
        def imap(bi, n):
            return (bi, jnp.clip(n + shift, 0, nb - 1), col)
        return pl.BlockSpec((1, blk, LANES), imap)

    return pl.pallas_call(
        functools.partial(_attn_kernel, nb=nb),
        grid=(b, nb),
        in_specs=[
            pl.BlockSpec(memory_space=pltpu.SMEM),
            pl.BlockSpec((1, blk, ATTN_WIDTH), lambda bi, n: (bi, n, 0)),
            spec(kcol, -1), spec(kcol, 0), spec(kcol, 1),
            spec(vcol, -1), spec(vcol, 0), spec(vcol, 1),
        ],
        out_specs=pl.BlockSpec((1, blk, ATTN_WIDTH), lambda bi, n: (bi, n, 0)),
        out_shape=jax.ShapeDtypeStruct((b, s, ATTN_WIDTH), F32),
        compiler_params=_cparams(("parallel", "parallel")),
        name="attn",
    )(sink, qkv3, qkv3, qkv3, qkv3, qkv3, qkv3, qkv3)


def _softplus(x):
    return jnp.maximum(x, 0.0) + jnp.log(1.0 + jnp.exp(-jnp.abs(x)))


def _sigmoid(x):
    return 1.0 / (1.0 + jnp.exp(-x))


def _prep_kernel(p_ref, hp_ref, hn_ref, mup_ref, mun_ref, wmix_ref, w0_ref, a0_ref, g2_ref,
                 kk_ref, ka_ref, bd_ref,
                 r_o, v_o, kk_o, g_o, lw0_o, lw1_o, kd0_o, kd1_o, b0_o, b1_o, *, tm, seq):
    i = pl.program_id(0)
    row = lax.broadcasted_iota(jnp.int32, (tm, 1), 0)
    seq_start = (i * tm) % seq == 0
    seq_end = ((i + 1) * tm) % seq == 0

    def shifted(c0, c1):
        p = p_ref[:, c0:c1]
        prev_row = jnp.where(seq_start, 0.0, hp_ref[SUBLANES - 1:SUBLANES, c0:c1])
        next_row = jnp.where(seq_end, 0.0, hn_ref[0:1, c0:c1])
        p_prev = jnp.where(row == 0, prev_row, pltpu.roll(p, 1, 0))
        p_next = jnp.where(row == tm - 1, next_row, pltpu.roll(p, tm - 1, 0))
        return p + mup_ref[:, c0:c1] * (p_prev - p) + mun_ref[:, c0:c1] * (p_next - p)

    w = RWKV_WIDTH
    r = shifted(0, w)
    k = shifted(w, 2 * w)
    v = shifted(2 * w, 3 * w)
    codes = shifted(3 * w, 3 * w + LANES)
    gd = shifted(3 * w + LANES, 3 * w + 2 * LANES)
    r_o[...] = r.astype(r_o.dtype)
    v_o[...] = v.astype(v_o.dtype)

    lane = lax.broadcasted_iota(jnp.int32, codes.shape, 1)
    codes = jnp.where(lane < 2 * DECAY_RANK, jnp.tanh(codes), codes)
    mm = _dot(codes, wmix_ref[...], HI)
    g_o[...] = _dot(_sigmoid(gd), g2_ref[...], HI)

    kkv = k * kk_ref[...]
    bd = bd_ref[...].astype(BF16)
    ss = sum(_dot(part, bd) for part in _split3(kkv * kkv))
    kkn = kkv * lax.rsqrt(jnp.maximum(ss, 1e-24))
    kk_o[...] = kkn.astype(kk_o.dtype)

    ka = ka_ref[...]
    for d, (lw_o, kd_o, b_o) in enumerate(((lw0_o, kd0_o, b0_o), (lw1_o, kd1_o, b1_o))):
        w_log = -_softplus(-(w0_ref[d:d + 1, :] + mm[:, d * w:(d + 1) * w])) - 0.5
        lw_o[...] = -jnp.exp(w_log)
        a = _sigmoid(a0_ref[d:d + 1, :] + mm[:, (2 + d) * w:(3 + d) * w])
        kd_o[...] = (k * (1.0 + (a - 1.0) * ka)).astype(kd_o.dtype)
        b_o[...] = (kkn * a).astype(b_o.dtype)


def _prep(pr, mup, mun, wmix, w0, a0, g2p, k_k, k_a, bd, seq, tm=256):
    t = pr.shape[0]
    nblk8 = t // SUBLANES
    per = tm // SUBLANES
    full = lambda a: pl.BlockSpec(a.shape, lambda i: (0,) * a.ndim)
    out = pl.BlockSpec((tm, RWKV_WIDTH), lambda i: (i, 0))
    return pl.pallas_call(
        functools.partial(_prep_kernel, tm=tm, seq=seq),
        grid=(t // tm,),
        in_specs=[
            pl.BlockSpec((tm, RWKV_COLS_PAD), lambda i: (i, 0)),
            pl.BlockSpec((SUBLANES, RWKV_COLS_PAD), lambda i: (jnp.maximum(i * per - 1, 0), 0)),
            pl.BlockSpec((SUBLANES, RWKV_COLS_PAD),
                         lambda i: (jnp.minimum((i + 1) * per, nblk8 - 1), 0)),
            full(mup), full(mun), full(wmix), full(w0), full(a0), full(g2p),
            full(k_k), full(k_a), full(bd),
        ],
        out_specs=[out] * 10,
        out_shape=[jax.ShapeDtypeStruct((t, RWKV_WIDTH), dt) for dt in PREP_OUT_DTYPES],
        compiler_params=_cparams(("parallel",)),
        name="prep",
    )(pr, pr, pr, mup, mun, wmix, w0, a0, g2p, k_k, k_a, bd)


def _split3(x):
    hi = x.astype(BF16)
    r1 = x - hi.astype(F32)
    mid = r1.astype(BF16)
    lo = (r1 - mid.astype(F32)).astype(BF16)
    return hi, mid, lo


def _wkv_kernel(rf, vf, kkf, lwf, kdf, bf, rb, vb, kkb, lwb, kdb, bb, yf_ref, yb_ref, state):
    c = pl.program_id(1)
    n = WKV_CHUNK
    hd = HEAD_DIM
    nh = N_RWKV_HEADS
    def bmm(a, b, ca, cb):
        return lax.dot_general(a.astype(BF16), b.astype(BF16), (((ca,), (cb,)), ((0,), (0,))),
                               preferred_element_type=F32)

    @pl.when(c == 0)
    def _():
        state[...] = jnp.zeros_like(state)

    ti = lax.broadcasted_iota(jnp.int32, (n, n), 0)
    si = lax.broadcasted_iota(jnp.int32, (n, n), 1)
    dirs = ((rf, vf, kkf, lwf, kdf, bf, si <= ti, si < ti, n - 1),
            (rb, vb, kkb, lwb, kdb, bb, si >= ti, si > ti, 0))
    heads = lambda x: jnp.stack([x[:, j * hd:(j + 1) * hd] for j in range(nh)], 0)
    parts = []
    for r_ref, v_ref, kk_ref, lw_ref, kd_ref, b_ref, incl, strict, last in dirs:
        lw = lw_ref[0]
        tri = incl.astype(BF16)
        cum = sum(_dot(tri, part) for part in _split3(lw))
        e_neg = jnp.exp(-cum)
        e_last = jnp.exp(cum[last:last + 1, :] - cum)
        g_scale = jnp.exp(cum[last:last + 1, :])
        parts.append(dict(
            at=heads((-kk_ref[0] * jnp.exp(cum - lw)).astype(BF16)),
            rt=heads((r_ref[0] * jnp.exp(cum)).astype(BF16)),
            bt=heads((b_ref[0] * e_neg).astype(BF16)),
            kt=heads((kd_ref[0] * e_neg).astype(BF16)),
            bl=heads((b_ref[0] * e_last).astype(BF16)),
            kl=heads((kd_ref[0] * e_last).astype(BF16)),
            v=heads(v_ref[0].astype(BF16)),
            gs=heads(g_scale),
            incl=jnp.broadcast_to(incl[None], (nh, n, n)),
            strict=jnp.broadcast_to(strict[None], (nh, n, n))))
    cat = lambda key: jnp.concatenate([parts[0][key], parts[1][key]], 0)
    at, rt, bt, kt, bl, kl, v, gs = (cat(k) for k in ("at", "rt", "bt", "kt", "bl", "kl", "v", "gs"))
    incl, strict = cat("incl"), cat("strict")
    g0 = state[...]
    g0b = g0.astype(BF16)
    m1 = bmm(jnp.concatenate([at, rt], 1), jnp.concatenate([bt, kt], 1), 2, 2)
    a_ab = jnp.where(strict, m1[:, :n, :n], 0.0)
    a_ak = jnp.where(strict, m1[:, :n, n:], 0.0)
    a_rb = jnp.where(incl, m1[:, n:, :n], 0.0)
    a_rk = jnp.where(incl, m1[:, n:, n:], 0.0)
    tinv = jnp.where((ti == si)[None], 1.0, a_ab)
    pw = a_ab
    for _ in range(int(np.log2(n)) - 1):
        pw = bmm(pw, pw, 2, 1)
        tinv = tinv + bmm(tinv, pw, 2, 1)
    rhs = bmm(a_ak, v, 2, 1) + bmm(at, g0b, 2, 2)
    u = bmm(tinv, rhs, 2, 1)
    uv = jnp.concatenate([u.astype(BF16), v], 1)
    y = bmm(jnp.concatenate([a_rb, a_rk], 2), uv, 2, 1) + bmm(rt, g0b, 2, 2)
    state[...] = g0 * gs + bmm(uv, jnp.concatenate([bl, kl], 1), 1, 1)
    for d, y_ref in enumerate((yf_ref, yb_ref)):
        y_ref[0] = jnp.concatenate([y[d * nh + j] for j in range(nh)], axis=1)


def _wkv(r, v, kk, lw0, lw1, kd0, kd1, b0, b1, batch, seq):
    n = WKV_CHUNK
    nc = seq // n
    shp = (batch, seq, RWKV_WIDTH)
    arrs = [a.reshape(shp) for a in (r, v, kk, lw0, kd0, b0, r, v, kk, lw1, kd1, b1)]
    fwd = pl.BlockSpec((1, n, RWKV_WIDTH), lambda b, c: (b, c, 0))
    bwd = pl.BlockSpec((1, n, RWKV_WIDTH), lambda b, c: (b, nc - 1 - c, 0))
    yf, yb = pl.pallas_call(
        _wkv_kernel,
        grid=(batch, nc),
        in_specs=[fwd] * 6 + [bwd] * 6,
        out_specs=[fwd, bwd],
        out_shape=[jax.ShapeDtypeStruct(shp, F32)] * 2,
        scratch_shapes=[pltpu.VMEM((2 * N_RWKV_HEADS, HEAD_DIM, HEAD_DIM), F32)],
        compiler_params=_cparams(("parallel", "arbitrary")),
        name="wkv",
    )(*arrs)
    return yf.reshape(batch * seq, RWKV_WIDTH), yb.reshape(batch * seq, RWKV_WIDTH)


def _mix_kernel(x_ref, ya_ref, yf_ref, yb_ref, r_ref, v_ref, kd0_ref, kd1_ref, g_ref,
                lng_ref, lnb_ref, rk_ref, bd_ref, wout_ref, l1g_ref, l1b_ref, h_ref):
    bd = bd_ref[...].astype(BF16)
    head_sum = lambda t: sum(_dot(part, bd) for part in _split3(t))
    inv = 1.0 / HEAD_DIM
    y = yf_ref[...] + yb_ref[...]
    mu = head_sum(y) * inv
    yc = y - mu
    var = head_sum(yc * yc) * inv
    yn = yc * lax.rsqrt(var + RWKV_GN_EPS) * lng_ref[...] + lnb_ref[...]
    k_mean = 0.5 * (kd0_ref[...] + kd1_ref[...])
    v = v_ref[...]
    bonus = head_sum(r_ref[...] * k_mean * rk_ref[...]) * v
    yr = (yn + bonus) * g_ref[...]
    mix = (_dot(ya_ref[...].astype(BF16), wout_ref[:ATTN_WIDTH, :])
           + _dot(yr.astype(BF16), wout_ref[ATTN_WIDTH:, :]))
    h_ref[...] = _layer_norm(DEEPNORM_ALPHA * x_ref[...] + mix, l1g_ref[...], l1b_ref[...])


def _mix(x2, ya, yf, yb, r, v, kd0, kd1, g, lng, lnb, rk, bd, wout, l1g, l1b, tm=256):
    t, d = x2.shape
    full = lambda a: pl.BlockSpec(a.shape, lambda i: (0,) * a.ndim)
    half = pl.BlockSpec((tm, RWKV_WIDTH), lambda i: (i, 0))
    wide = pl.BlockSpec((tm, d), lambda i: (i, 0))
    return pl.pallas_call(
        _mix_kernel,
        grid=(t // tm,),
        in_specs=[wide] + [half] * 8 + [full(lng), full(lnb), full(rk), full(bd), full(wout),
                                        full(l1g), full(l1b)],
        out_specs=wide,
        out_shape=jax.ShapeDtypeStruct((t, d), F32),
        compiler_params=_cparams(("parallel",)),
        name="mix",
    )(x2, ya, yf, yb, r, v, kd0, kd1, g, lng, lnb, rk, bd, wout, l1g, l1b)


def _top_rows(scs, k, payloads=None):
    n = scs[0].shape[0]
    iota = lax.broadcasted_iota(jnp.int32, scs[0].shape, 0).astype(F32)
    scs = list(scs)
    vals = [[] for _ in scs]
    picks = [[] for _ in scs]
    for _ in range(k):
        for a, sc in enumerate(scs):
            m = jnp.max(sc, axis=0, keepdims=True)
            pos = jnp.min(jnp.where(sc == m, iota, float(n)), axis=0, keepdims=True)
            hit = iota == pos
            vals[a].append(m)
            if payloads is None:
                picks[a].append(pos)
            else:
                picks[a].append(jnp.max(jnp.where(hit, payloads[a], -1.0), axis=0, keepdims=True))
            scs[a] = jnp.where(hit, -jnp.inf, sc)
    return [(jnp.concatenate(v, 0), jnp.concatenate(p, 0)) for v, p in zip(vals, picks)]


def _route_kernel(h_ref, wq_ref, keys_ref, idx_ref, gate_ref, gate_tok_ref, q_scr, idx_scr):
    k = PEER_TOPK
    q_scr[...] = _dot(h_ref[...].astype(BF16), wq_ref[...])

    def head_group(hg, carry):
        heads = [hg * ROUTE_HEADS_PER_STEP + i for i in range(ROUTE_HEADS_PER_STEP)]
        scores = []
        for hh in heads:
            for p in range(2):
                col = pl.multiple_of(hh * PEER_QDIM + p * PEER_HALF, PEER_HALF)
                qp = q_scr[:, pl.ds(col, PEER_HALF)]
                scores.append(_dot_nt(keys_ref[hh, p], qp, HI))
        tops = _top_rows(scores, k)
        cands, cand_ids = [], []
        for a in range(len(heads)):
            (s1, i1), (s2, i2) = tops[2 * a], tops[2 * a + 1]
            cands.append(jnp.concatenate([s1[i:i + 1] + s2[:k // (i + 1)] for i in range(k)], 0))
            cand_ids.append(jnp.concatenate(
                [i1[i:i + 1] * PEER_NKEYS + i2[:k // (i + 1)] for i in range(k)], 0))
        for hh, (cs, ids) in zip(heads, _top_rows(cands, k, cand_ids)):
            e = jnp.exp(cs - jnp.max(cs, axis=0, keepdims=True))
            row = pl.multiple_of(hh * k, k)
            gate_ref[pl.ds(row, k), :] = e / jnp.sum(e, axis=0, keepdims=True)
            idx_scr[pl.ds(row, k), :] = ids.astype(jnp.int32)
        return carry

    lax.fori_loop(0, PEER_HEADS // ROUTE_HEADS_PER_STEP, head_group, 0)
    idx_ref[...] = idx_scr[...].T
    gate_tok_ref[...] = gate_ref[...].T


def _route(h1, wq, keys, tm=256):
    t, d = h1.shape
    return pl.pallas_call(
        _route_kernel,
        grid=(t // tm,),
        in_specs=[
            pl.BlockSpec((tm, d), lambda i: (i, 0)),
            pl.BlockSpec(wq.shape, lambda i: (0, 0)),
            pl.BlockSpec(keys.shape, lambda i: (0, 0, 0, 0)),
        ],
        out_specs=[
            pl.BlockSpec((tm, N_SEL), lambda i: (i, 0)),
            pl.BlockSpec((N_SEL, tm), lambda i: (0, i)),
            pl.BlockSpec((tm, N_SEL), lambda i: (i, 0)),
        ],
        out_shape=[
            jax.ShapeDtypeStruct((t, N_SEL), jnp.int32),
            jax.ShapeDtypeStruct((N_SEL, t), F32),
            jax.ShapeDtypeStruct((t, N_SEL), F32),
        ],
        scratch_shapes=[pltpu.VMEM((tm, PEER_HEADS * PEER_QDIM), F32),
                        pltpu.VMEM((N_SEL, tm), jnp.int32)],
        compiler_params=_cparams(("parallel",)),
        name="route",
    )(h1, wq, keys)


def _peer_kernel(idx_ref, gate_ref, h_ref, uv_hbm, l2g_ref, l2b_ref, o_ref, buf, f_scr, sem, *, tb):
    d = h_ref.shape[1]

    def issue(t, slot):
        for j in range(N_SEL):
            pltpu.make_async_copy(uv_hbm.at[pl.ds(idx_ref[t, j], 1)], buf.at[slot, pl.ds(j, 1)],
                                  sem.at[slot]).start()

    def wait_all(slot):
        pltpu.make_async_copy(uv_hbm.at[pl.ds(0, N_SEL)], buf.at[slot], sem.at[slot]).wait()

    issue(0, 0)
    lane = lax.broadcasted_iota(jnp.int32, (N_SEL, tb), 1)

    def body(t, carry):
        slot = t % 2

        @pl.when(t + 1 < tb)
        def _():
            issue(t + 1, 1 - slot)

        wait_all(slot)
        z = jnp.sum(buf[slot, :, 0:d] * h_ref[pl.ds(t, 1), :], axis=1, keepdims=True)
        gcol = jnp.sum(jnp.where(lane == t, gate_ref[...], 0.0), axis=1, keepdims=True)
        coef = gcol * (0.5 * z * (1.0 + lax.erf(z * (2.0 ** -0.5))))
        f_scr[pl.ds(t, 1), :] = jnp.sum(coef * buf[slot, :, d:2 * d], axis=0, keepdims=True)
        return carry

    lax.fori_loop(0, tb, body, 0)
    o_ref[...] = _layer_norm(DEEPNORM_ALPHA * h_ref[...] + f_scr[...], l2g_ref[...], l2b_ref[...])


def _peer(idx, gate_t, h1, uv, l2g, l2b, tok0, ntok, tb=PEER_TB):
    d = h1.shape[1]
    b0 = tok0 // tb
    return pl.pallas_call(
        functools.partial(_peer_kernel, tb=tb),
        grid=(ntok // tb,),
        in_specs=[
            pl.BlockSpec((tb, N_SEL), lambda i: (b0 + i, 0), memory_space=pltpu.SMEM),
            pl.BlockSpec((N_SEL, tb), lambda i: (0, b0 + i)),
            pl.BlockSpec((tb, d), lambda i: (b0 + i, 0)),
            pl.BlockSpec(memory_space=pl.ANY),
            pl.BlockSpec(l2g.shape, lambda i: (0, 0)),
            pl.BlockSpec(l2b.shape, lambda i: (0, 0)),
        ],
        out_specs=pl.BlockSpec((tb, d), lambda i: (i, 0)),
        out_shape=jax.ShapeDtypeStruct((ntok, d), F32),
        scratch_shapes=[pltpu.VMEM((2, N_SEL, 2 * d), F32),
                        pltpu.VMEM((tb, d), F32),
                        pltpu.SemaphoreType.DMA((2,))],
        compiler_params=_cparams(("arbitrary",)),
        name="peer",
    )(idx, gate_t, h1, uv, l2g, l2b)


def _sc_mesh():
    return plsc.VectorSubcoreMesh(core_axis_name="c", subcore_axis_name="s")


def _worker_id():
    return lax.axis_index("s") * 2 + lax.axis_index("c")


def _sc_z_body(u_hbm, idx_hbm, h_hbm, z_hbm, idx_v, h_v, rows_v, z_v, sem, *, tpw, d, tok0):
    ngrp = N_SEL // SC_GROUP
    nchunk = d // SC_LANES
    nsteps = tpw * ngrp
    base = _worker_id() * tpw

    def gather(tok_buf, g, buf):
        return pltpu.make_async_copy(u_hbm.at[idx_v.at[tok_buf, g]], rows_v.at[buf], sem.at[buf])

    pltpu.sync_copy(idx_hbm.at[tok0 + base], idx_v.at[0])
    gather(0, 0, 0).start()

    @pl.loop(0, nsteps)
    def _(q):
        t_loc = q // ngrp
        g = q % ngrp
        buf = q % 2
        t = base + t_loc

        @pl.when(g == 0)
        def _():
            pltpu.sync_copy(h_hbm.at[tok0 + t], h_v)

            @pl.loop(0, N_SEL)
            def _(r):
                z_v[pl.ds(pl.multiple_of(r * SC_LANES, SC_LANES), SC_LANES)] = jnp.zeros((SC_LANES,), F32)

        @pl.when(q + 1 < nsteps)
        def _():
            tn = (q + 1) // ngrp
            gn = (q + 1) % ngrp

            @pl.when(gn == 0)
            def _():
                pltpu.sync_copy(idx_hbm.at[tok0 + base + tn], idx_v.at[tn % 2])

            gather(tn % 2, gn, 1 - buf).start()

        gather(t_loc % 2, g, buf).wait()

        @pl.loop(0, SC_GROUP, step=4)
        def _(rb):
            @pl.loop(0, nchunk, step=SC_CHUNK_UNROLL)
            def _(c0):
                accs = [jnp.zeros((SC_LANES,), F32) for _ in range(4)]
                for cc in range(SC_CHUNK_UNROLL):
                    sl = pl.ds(pl.multiple_of((c0 + cc) * SC_LANES, SC_LANES), SC_LANES)
                    hc = h_v[sl]
                    for i in range(4):
                        accs[i] = accs[i] + rows_v[buf, rb + i, sl] * hc
                for i in range(4):
                    row = pl.multiple_of((g * SC_GROUP + rb + i) * SC_LANES, SC_LANES)
                    plsc.addupdate(z_v.at[pl.ds(row, SC_LANES)], accs[i])

        @pl.when(g == ngrp - 1)
        def _():
            pltpu.sync_copy(z_v, z_hbm.at[t])


def _sc_z(u, idx, h, tok0, ntok):
    t, d = h.shape
    tpw = ntok // SC_WORKERS
    idx4 = idx.reshape(t, N_SEL // SC_GROUP, SC_GROUP)
    body = functools.partial(_sc_z_body, tpw=tpw, d=d, tok0=tok0)
    return pl.kernel(
        body,
        out_type=jax.ShapeDtypeStruct((ntok, N_SEL * SC_LANES), F32),
        mesh=_sc_mesh(),
        scratch_types=[
            pltpu.VMEM((2, N_SEL // SC_GROUP, SC_GROUP), jnp.int32),
            pltpu.VMEM((d,), F32),
            pltpu.VMEM((2, SC_GROUP, d), F32),
            pltpu.VMEM((N_SEL * SC_LANES,), F32),
            pltpu.SemaphoreType.DMA((2,)),
        ],
        name="sc_z",
    )(u, idx4, h)


def _sc_out_body(v_hbm, idx_hbm, coef_hbm, f_hbm, idx_v, coef_v, rows_v, out_v, sem, *, tpw, d, tok0):
    ngrp = N_SEL // SC_GROUP
    nchunk = d // SC_LANES
    nsteps = tpw * ngrp
    base = _worker_id() * tpw

    def gather(tok_buf, g, buf):
        return pltpu.make_async_copy(v_hbm.at[idx_v.at[tok_buf, g]], rows_v.at[buf], sem.at[buf])

    pltpu.sync_copy(idx_hbm.at[tok0 + base], idx_v.at[0])
    gather(0, 0, 0).start()

    @pl.loop(0, nsteps)
    def _(q):
        t_loc = q // ngrp
        g = q % ngrp
        buf = q % 2
        t = base + t_loc

        @pl.when(g == 0)
        def _():
            pltpu.sync_copy(coef_hbm.at[t], coef_v)

        @pl.when(q + 1 < nsteps)
        def _():
            tn = (q + 1) // ngrp
            gn = (q + 1) % ngrp

            @pl.when(gn == 0)
            def _():
                pltpu.sync_copy(idx_hbm.at[tok0 + base + tn], idx_v.at[tn % 2])

            gather(tn % 2, gn, 1 - buf).start()

        gather(t_loc % 2, g, buf).wait()

        @pl.loop(0, nchunk, step=SC_OUT_CHUNKS)
        def _(c0):
            sls = [pl.ds(pl.multiple_of((c0 + cc) * SC_LANES, SC_LANES), SC_LANES)
                   for cc in range(SC_OUT_CHUNKS)]
            accs = [None] * SC_OUT_CHUNKS
            for r in range(SC_GROUP):
                cf = coef_v[pl.ds(pl.multiple_of((g * SC_GROUP + r) * SC_LANES, SC_LANES), SC_LANES)]
                for cc in range(SC_OUT_CHUNKS):
                    p = rows_v[buf, r, sls[cc]] * cf
                    accs[cc] = p if accs[cc] is None else accs[cc] + p
            for cc in range(SC_OUT_CHUNKS):
                @pl.when(g == 0)
                def _():
                    out_v[sls[cc]] = accs[cc]

                @pl.when(g != 0)
                def _():
                    plsc.addupdate(out_v.at[sls[cc]], accs[cc])

        @pl.when(g == ngrp - 1)
        def _():
            pltpu.sync_copy(out_v, f_hbm.at[t])


def _sc_out(v, idx, coef16, tok0):
    t = idx.shape[0]
    ntok = coef16.shape[0]
    d = v.shape[1]
    tpw = ntok // SC_WORKERS
    idx4 = idx.reshape(t, N_SEL // SC_GROUP, SC_GROUP)
    body = functools.partial(_sc_out_body, tpw=tpw, d=d, tok0=tok0)
    return pl.kernel(
        body,
        out_type=jax.ShapeDtypeStruct((ntok, d), F32),
        mesh=_sc_mesh(),
        scratch_types=[
            pltpu.VMEM((2, N_SEL // SC_GROUP, SC_GROUP), jnp.int32),
            pltpu.VMEM((N_SEL * SC_LANES,), F32),
            pltpu.VMEM((2, SC_GROUP, d), F32),
            pltpu.VMEM((d,), F32),
            pltpu.SemaphoreType.DMA((2,)),
        ],
        name="sc_out",
    )(v, idx4, coef16)


def _coef_kernel(zp_ref, gate_ref, sel_ref, selt_ref, after_ref, o_ref):
    z = sum(_dot(part, sel_ref[...]) for part in _split3(zp_ref[...]))
    coef = gate_ref[...] * (0.5 * z * (1.0 + lax.erf(z * (2.0 ** -0.5))))
    o_ref[...] = sum(_dot(part, selt_ref[...]) for part in _split3(coef))


def _coef(zp, gate_tok, tok0, after, tm=256):
    ntok, wide = zp.shape
    b0 = tok0 // tm
    grp = np.arange(wide) // SC_LANES
    sel = jnp.asarray(grp[:, None] == np.arange(N_SEL)[None, :], BF16)
    return pl.pallas_call(
        _coef_kernel,
        grid=(ntok // tm,),
        in_specs=[
            pl.BlockSpec((tm, wide), lambda i: (i, 0)),
            pl.BlockSpec((tm, N_SEL), lambda i: (b0 + i, 0)),
            pl.BlockSpec(sel.shape, lambda i: (0, 0)),
            pl.BlockSpec(sel.shape[::-1], lambda i: (0, 0)),
            pl.BlockSpec((SUBLANES, LANES), lambda i: (0, 0)),
        ],
        out_specs=pl.BlockSpec((tm, wide), lambda i: (i, 0)),
        out_shape=jax.ShapeDtypeStruct((ntok, wide), F32),
        compiler_params=_cparams(("parallel",)),
        name="coef",
    )(zp, gate_tok, sel, sel.T, after)


def _ln_out_kernel(h_ref, f_ref, g_ref, b_ref, o_ref):
    o_ref[...] = _layer_norm(DEEPNORM_ALPHA * h_ref[...] + f_ref[...], g_ref[...], b_ref[...])


def _ln_out(h1, f, l2g, l2b, tok0, tm=256):
    ntok, d = f.shape
    b0 = tok0 // tm
    return pl.pallas_call(
        _ln_out_kernel,
        grid=(ntok // tm,),
        in_specs=[
            pl.BlockSpec((tm, d), lambda i: (b0 + i, 0)),
            pl.BlockSpec((tm, d), lambda i: (i, 0)),
            pl.BlockSpec(l2g.shape, lambda i: (0, 0)),
            pl.BlockSpec(l2b.shape, lambda i: (0, 0)),
        ],
        out_specs=pl.BlockSpec((tm, d), lambda i: (i, 0)),
        out_shape=jax.ShapeDtypeStruct((ntok, d), F32),
        compiler_params=_cparams(("parallel",)),
        name="ln_out",
    )(h1, f, l2g, l2b)


def _rope_tables(seq):
    pos = jnp.arange(seq, dtype=F32)
    inv_freq = ROPE_THETA ** (-jnp.arange(0, HEAD_DIM, 2, dtype=F32) / HEAD_DIM)
    ang = pos[:, None] * inv_freq[None, :]
    ang = jnp.concatenate([ang, ang], -1)
    sign = jnp.concatenate([-jnp.ones((HEAD_DIM // 2,), F32), jnp.ones((HEAD_DIM // 2,), F32)])
    reps = LANES // HEAD_DIM
    return jnp.tile(jnp.cos(ang), (1, reps)), jnp.tile(jnp.sin(ang) * sign, (1, reps))


def _layer(h2d, batch, seq, w_in, mu_prev, mu_next, w0, w2, a0, a2, g2, k_k, k_a, r_k, lnx_g, lnx_b,
           sink, w_out, ln1_g, ln1_b, peer_wq, peer_keys, peer_u, peer_v, ln2_g, ln2_b):
    w = RWKV_WIDTH
    row = lambda a: a.reshape(1, -1).astype(F32)
    w_in_p = jnp.pad(w_in, ((0, 0), (0, RWKV_COLS_PAD - RWKV_COLS))).astype(BF16)
    mup = jnp.pad(mu_prev, (0, RWKV_COLS_PAD - RWKV_COLS)).reshape(1, -1)
    mun = jnp.pad(mu_next, (0, RWKV_COLS_PAD - RWKV_COLS)).reshape(1, -1)
    wmix = jnp.zeros((LANES, 4 * w), F32)
    for d in range(2):
        wmix = wmix.at[d * DECAY_RANK:(d + 1) * DECAY_RANK, d * w:(d + 1) * w].set(w2[d])
        r0 = 2 * DECAY_RANK + d * ICLR_RANK
        wmix = wmix.at[r0:r0 + ICLR_RANK, (2 + d) * w:(3 + d) * w].set(a2[d])
    g2p = jnp.pad(g2, ((0, LANES - GATE_RANK), (0, 0)))
    head_of = np.arange(w) // HEAD_DIM
    bd = jnp.asarray(head_of[:, None] == head_of[None, :], F32)
    cos_t, sin_t = _rope_tables(seq)

    wq_b, keys_f, wout_b = peer_wq.astype(BF16), peer_keys.astype(F32), w_out.astype(BF16)

    def to_routing(x2d, nb):
        qkv, pr = _proj(x2d, w_in_p, cos_t, sin_t, seq)
        y_attn = _attention(qkv.reshape(nb, seq, ATTN_COLS), sink.astype(F32))
        r, v, kk, g, lw0, lw1, kd0, kd1, b0, b1 = _prep(
            pr, mup, mun, wmix, w0.astype(F32), a0.astype(F32), g2p, row(k_k), row(k_a), bd, seq)
        yf, yb = _wkv(r, v, kk, lw0, lw1, kd0, kd1, b0, b1, nb, seq)
        h1 = _mix(x2d, y_attn.reshape(nb * seq, ATTN_WIDTH), yf, yb, r, v, kd0, kd1, g,
                  row(lnx_g), row(lnx_b), row(r_k), bd, wout_b, row(ln1_g), row(ln1_b))
        return (h1,) + tuple(_route(h1, wq_b, keys_f))

    uv = jnp.concatenate([peer_u, peer_v], axis=1)
    l2g, l2b = row(ln2_g), row(ln2_b)

    nb_sc = batch * PEER_SC_SEQ_SHARE_8THS // 8
    t_sc = nb_sc * seq
    t_tc = (batch - nb_sc) * seq
    n_a = (t_tc * PEER_TC_FIRST_CALL_32NDS // 32) // PEER_TB * PEER_TB
    n_b = t_tc - n_a
    assert 0 < nb_sc < batch and t_sc % (SC_WORKERS * SUBLANES) == 0 and n_a > 0 and n_b % PEER_TB == 0
    h1s, idxs, _, gate_tok_s = to_routing(h2d[:t_sc], nb_sc)
    zp = _sc_z(peer_u.astype(F32), idxs, h1s, 0, t_sc)
    h1t, idxt, gate_t_t, _ = to_routing(h2d[t_sc:], batch - nb_sc)
    out_a = _peer(idxt, gate_t_t, h1t, uv, l2g, l2b, 0, n_a)
    f_sc = _sc_out(peer_v.astype(F32), idxs, _coef(zp, gate_tok_s, 0, out_a), 0)
    out_b = _peer(idxt, gate_t_t, h1t, uv, l2g, l2b, n_a, n_b)
    out_sc = _ln_out(h1s, f_sc, l2g, l2b, 0)
    return jnp.concatenate([out_sc, out_a, out_b], axis=0)


def kernel(x, w_in, mu_prev, mu_next, w0, w2, a0, a2, g2, k_k, k_a, r_k, lnx_g, lnx_b, sink, w_out,
           ln1_g, ln1_b, peer_wq, peer_keys, peer_u, peer_v, ln2_g, ln2_b):
    batch, seq, d = x.shape
    h = x.reshape(batch * seq, d)
    for l in range(DEPTH):
        h = _layer(h, batch, seq, w_in[l], mu_prev[l], mu_next[l], w0[l], w2[l], a0[l], a2[l], g2[l],
                   k_k[l], k_a[l], r_k[l], lnx_g[l], lnx_b[l], sink[l], w_out[l], ln1_g[l], ln1_b[l],
                   peer_wq[l], peer_keys[l], peer_u[l], peer_v[l], ln2_g[l], ln2_b[l])
    return h.reshape(batch, seq, d)
```

```python
import functools

import numpy as np
import jax
import jax.numpy as jnp
from jax import lax
from jax.experimental import pallas as pl
from jax.experimental.pallas import tpu as pltpu
from jax.experimental.pallas import tpu_sc as plsc

F32 = jnp.float32
BF16 = jnp.bfloat16
HI = lax.Precision.HIGHEST

HEAD_DIM = 64
N_Q_HEADS = 8
N_KV_HEADS = 2
Q_PER_KV = N_Q_HEADS // N_KV_HEADS
ATTN_WIDTH = N_Q_HEADS * HEAD_DIM
ATTN_KV_WIDTH = N_KV_HEADS * HEAD_DIM
ATTN_COLS = ATTN_WIDTH + 2 * ATTN_KV_WIDTH
WINDOW = 128
ATTN_BLOCK = 128
ROPE_THETA = 10000.0
N_RWKV_HEADS = 8
RWKV_WIDTH = N_RWKV_HEADS * HEAD_DIM
DECAY_RANK = 32
ICLR_RANK = 32
GATE_RANK = 96
RWKV_COLS = 3 * RWKV_WIDTH + 2 * DECAY_RANK + 2 * ICLR_RANK + GATE_RANK
RWKV_COLS_PAD = 1792
RWKV_GN_EPS = 64e-5
PEER_HEADS = 8
PEER_NKEYS = 128
PEER_QDIM = 256
PEER_HALF = PEER_QDIM // 2
PEER_TOPK = 16
N_SEL = PEER_HEADS * PEER_TOPK
LN_EPS = 1e-5
DEPTH = 1
DEEPNORM_ALPHA = (2.0 * DEPTH) ** 0.25

LANES = 128
SUBLANES = 8
PREP_OUT_DTYPES = (BF16, BF16, BF16, F32, F32, F32, BF16, BF16, BF16, BF16)
WKV_CHUNK = 64
VMEM_LIMIT = 48 * 1024 * 1024
PEER_SLOTS = 3
PEER_TB = 256
ROUTE_HEADS_PER_STEP = 8
SC_LANES = 16
SC_WORKERS = 32
SC_GROUP = 32
SC_CHUNK_UNROLL = 8
SC_OUT_CHUNKS = 4
PEER_SC_SEQ_SHARE_8THS = 5
PEER_TC_FIRST_CALL_32NDS = 13


def _cparams(sem):
    return pltpu.CompilerParams(dimension_semantics=sem, vmem_limit_bytes=VMEM_LIMIT)


def _dot(a, b, precision=None):
    return jnp.dot(a, b, preferred_element_type=F32, precision=precision)


def _dot_nt(a, b, precision=None):
    return lax.dot_general(a, b, (((1,), (1,)), ((), ())), preferred_element_type=F32,
                           precision=precision)


def _layer_norm(z, g, b):
    mu = jnp.mean(z, -1, keepdims=True)
    zc = z - mu
    var = jnp.mean(zc * zc, -1, keepdims=True)
    return zc * lax.rsqrt(var + LN_EPS) * g + b


def _proj_kernel(x_ref, w_ref, cos_ref, sin_ref, qkv_ref, pr_ref):
    xb = x_ref[...].astype(BF16)
    cos = cos_ref[...]
    sin = sin_ref[...]
    lane = lax.broadcasted_iota(jnp.int32, cos.shape, 1)
    first_half = (lane & (HEAD_DIM // 2)) == 0

    def rope(t):
        rot = jnp.where(first_half, pltpu.roll(t, LANES - HEAD_DIM // 2, 1),
                        pltpu.roll(t, HEAD_DIM // 2, 1))
        return t * cos + rot * sin

    for c in range(0, ATTN_COLS, 2 * LANES):
        acc = _dot(xb, w_ref[:, c:c + 2 * LANES])
        for half in range(2):
            col = c + half * LANES
            t = acc[:, half * LANES:(half + 1) * LANES]
            if col < ATTN_WIDTH + ATTN_KV_WIDTH:
                t = rope(t)
            qkv_ref[:, col:col + LANES] = t
    for c in range(0, RWKV_COLS_PAD, 2 * LANES):
        pr_ref[:, c:c + 2 * LANES] = _dot(xb, w_ref[:, ATTN_COLS + c:ATTN_COLS + c + 2 * LANES])


def _proj(x2, w_in_p, cos_t, sin_t, seq, tm=512):
    t, d = x2.shape
    n_pos = seq // tm
    return pl.pallas_call(
        _proj_kernel,
        grid=(t // tm,),
        in_specs=[
            pl.BlockSpec((tm, d), lambda i: (i, 0)),
            pl.BlockSpec(w_in_p.shape, lambda i: (0, 0)),
            pl.BlockSpec((tm, LANES), lambda i: (i % n_pos, 0)),
            pl.BlockSpec((tm, LANES), lambda i: (i % n_pos, 0)),
        ],
        out_specs=[
            pl.BlockSpec((tm, ATTN_COLS), lambda i: (i, 0)),
            pl.BlockSpec((tm, RWKV_COLS_PAD), lambda i: (i, 0)),
        ],
        out_shape=[
            jax.ShapeDtypeStruct((t, ATTN_COLS), F32),
            jax.ShapeDtypeStruct((t, RWKV_COLS_PAD), F32),
        ],
        compiler_params=_cparams(("parallel",)),
        name="proj",
    )(x2, w_in_p, cos_t, sin_t)


def _attn_kernel(sink_ref, q_ref, kp_ref, kc_ref, kn_ref, vp_ref, vc_ref, vn_ref, o_ref, *, nb):
    n = pl.program_id(1)
    blk = ATTN_BLOCK
    rows = Q_PER_KV * blk
    q = q_ref[0] * (HEAD_DIM ** -0.5)
    kwin = jnp.concatenate([kp_ref[0], kc_ref[0], kn_ref[0]], axis=0)
    vwin = jnp.concatenate([vp_ref[0], vc_ref[0], vn_ref[0]], axis=0)
    qi = lax.broadcasted_iota(jnp.int32, (rows, 3 * blk), 0) & (blk - 1)
    kj = lax.broadcasted_iota(jnp.int32, (rows, 3 * blk), 1)
    dist = kj - qi
    valid = ((dist >= blk - WINDOW) & (dist <= blk + WINDOW)
             & ((kj >= blk) | (n > 0)) & ((kj < 2 * blk) | (n < nb - 1)))
    rowg = lax.broadcasted_iota(jnp.int32, (rows, 1), 0) // blk
    outs = []
    for h in range(N_KV_HEADS):
        qs = jnp.concatenate(
            [q[:, (Q_PER_KV * h + g) * HEAD_DIM:(Q_PER_KV * h + g + 1) * HEAD_DIM]
             for g in range(Q_PER_KV)], axis=0)
        kh = kwin[:, h * HEAD_DIM:(h + 1) * HEAD_DIM]
        vh = vwin[:, h * HEAD_DIM:(h + 1) * HEAD_DIM]
        logits = _dot_nt(qs.astype(BF16), kh.astype(BF16))
        logits = jnp.where(valid, logits, -1e30)
        sk = jnp.zeros((rows, 1), F32)
        for g in range(Q_PER_KV):
            sk = jnp.where(rowg == g, sink_ref[Q_PER_KV * h + g], sk)
        m = jnp.maximum(jnp.max(logits, -1, keepdims=True), sk)
        e = jnp.exp(logits - m)
        den = jnp.sum(e, -1, keepdims=True) + jnp.exp(sk - m)
        p = e / den
        o = _dot(p.astype(BF16), vh.astype(BF16))
        for g in range(Q_PER_KV):
            outs.append(o[g * blk:(g + 1) * blk])
    o_ref[0] = jnp.concatenate(outs, axis=1)


def _attention(qkv3, sink):
    b, s, _ = qkv3.shape
    blk = ATTN_BLOCK
    nb = s // blk
    kcol = ATTN_WIDTH // LANES
    vcol = kcol + 1

    def spec(col, shift):
        def imap(bi, n):
            return (bi, jnp.clip(n + shift, 0, nb - 1), col)
        return pl.BlockSpec((1, blk, LANES), imap)

    return pl.pallas_call(
        functools.partial(_attn_kernel, nb=nb),
        grid=(b, nb),
        in_specs=[
            pl.BlockSpec(memory_space=pltpu.SMEM),
            pl.BlockSpec((1, blk, ATTN_WIDTH), lambda bi, n: (bi, n, 0)),
            spec(kcol, -1), spec(kcol, 0), spec(kcol, 1),
            spec(vcol, -1), spec(vcol, 0), spec(vcol, 1),
        ],
        out_specs=pl.BlockSpec((1, blk, ATTN_WIDTH), lambda bi, n: (bi, n, 0)),
        out_shape=jax.ShapeDtypeStruct((b, s, ATTN_WIDTH), F32),
        compiler_params=_cparams(("parallel", "parallel")),
        name="attn",
    )(sink, qkv3, qkv3, qkv3, qkv3, qkv3, qkv3, qkv3)


def _softplus(x):
    return jnp.maximum(x, 0.0) + jnp.log(1.0 + jnp.exp(-jnp.abs(x)))


def _sigmoid(x):
    return 1.0 / (1.0 + jnp.exp(-x))


def _prep_kernel(p_ref, hp_ref, hn_ref, mup_ref, mun_ref, wmix_ref, w0_ref, a0_ref, g2_ref,
                 kk_ref, ka_ref, bd_ref,
                 r_o, v_o, kk_o, g_o, lw0_o, lw1_o, kd0_o, kd1_o, b0_o, b1_o, *, tm, seq):
    i = pl.program_id(0)
    row = lax.broadcasted_iota(jnp.int32, (tm, 1), 0)
    seq_start = (i * tm) % seq == 0
    seq_end = ((i + 1) * tm) % seq == 0

    def shifted(c0, c1):
        p = p_ref[:, c0:c1]
        prev_row = jnp.where(seq_start, 0.0, hp_ref[SUBLANES - 1:SUBLANES, c0:c1])
        next_row = jnp.where(seq_end, 0.0, hn_ref[0:1, c0:c1])
        p_prev = jnp.where(row == 0, prev_row, pltpu.roll(p, 1, 0))
        p_next = jnp.where(row == tm - 1, next_row, pltpu.roll(p, tm - 1, 0))
        return p + mup_ref[:, c0:c1] * (p_prev - p) + mun_ref[:, c0:c1] * (p_next - p)

    w = RWKV_WIDTH
    r = shifted(0, w)
    k = shifted(w, 2 * w)
    v = shifted(2 * w, 3 * w)
    codes = shifted(3 * w, 3 * w + LANES)
    gd = shifted(3 * w + LANES, 3 * w + 2 * LANES)
    r_o[...] = r.astype(r_o.dtype)
    v_o[...] = v.astype(v_o.dtype)

    lane = lax.broadcasted_iota(jnp.int32, codes.shape, 1)
    codes = jnp.where(lane < 2 * DECAY_RANK, jnp.tanh(codes), codes)
    mm = _dot(codes, wmix_ref[...], HI)
    g_o[...] = _dot(_sigmoid(gd), g2_ref[...], HI)

    kkv = k * kk_ref[...]
    bd = bd_ref[...].astype(BF16)
    ss = sum(_dot(part, bd) for part in _split3(kkv * kkv))
    kkn = kkv * lax.rsqrt(jnp.maximum(ss, 1e-24))
    kk_o[...] = kkn.astype(kk_o.dtype)

    ka = ka_ref[...]
    for d, (lw_o, kd_o, b_o) in enumerate(((lw0_o, kd0_o, b0_o), (lw1_o, kd1_o, b1_o))):
        w_log = -_softplus(-(w0_ref[d:d + 1, :] + mm[:, d * w:(d + 1) * w])) - 0.5
        lw_o[...] = -jnp.exp(w_log)
        a = _sigmoid(a0_ref[d:d + 1, :] + mm[:, (2 + d) * w:(3 + d) * w])
        kd_o[...] = (k * (1.0 + (a - 1.0) * ka)).astype(kd_o.dtype)
        b_o[...] = (kkn * a).astype(b_o.dtype)


def _prep(pr, mup, mun, wmix, w0, a0, g2p, k_k, k_a, bd, seq, tm=256):
    t = pr.shape[0]
    nblk8 = t // SUBLANES
    per = tm // SUBLANES
    full = lambda a: pl.BlockSpec(a.shape, lambda i: (0,) * a.ndim)
    out = pl.BlockSpec((tm, RWKV_WIDTH), lambda i: (i, 0))
    return pl.pallas_call(
        functools.partial(_prep_kernel, tm=tm, seq=seq),
        grid=(t // tm,),
        in_specs=[
            pl.BlockSpec((tm, RWKV_COLS_PAD), lambda i: (i, 0)),
            pl.BlockSpec((SUBLANES, RWKV_COLS_PAD), lambda i: (jnp.maximum(i * per - 1, 0), 0)),
            pl.BlockSpec((SUBLANES, RWKV_COLS_PAD),
                         lambda i: (jnp.minimum((i + 1) * per, nblk8 - 1), 0)),
            full(mup), full(mun), full(wmix), full(w0), full(a0), full(g2p),
            full(k_k), full(k_a), full(bd),
        ],
        out_specs=[out] * 10,
        out_shape=[jax.ShapeDtypeStruct((t, RWKV_WIDTH), dt) for dt in PREP_OUT_DTYPES],
        compiler_params=_cparams(("parallel",)),
        name="prep",
    )(pr, pr, pr, mup, mun, wmix, w0, a0, g2p, k_k, k_a, bd)


def _split3(x):
    hi = x.astype(BF16)
    r1 = x - hi.astype(F32)
    mid = r1.astype(BF16)
    lo = (r1 - mid.astype(F32)).astype(BF16)
    return hi, mid, lo


def _wkv_kernel(rf, vf, kkf, lwf, kdf, bf, rb, vb, kkb, lwb, kdb, bb, yf_ref, yb_ref, state):
    c = pl.program_id(1)
    n = WKV_CHUNK
    hd = HEAD_DIM
    nh = N_RWKV_HEADS
    def bmm(a, b, ca, cb):
        return lax.dot_general(a.astype(BF16), b.astype(BF16), (((ca,), (cb,)), ((0,), (0,))),
                               preferred_element_type=F32)

    @pl.when(c == 0)
    def _():
        state[...] = jnp.zeros_like(state)

    ti = lax.broadcasted_iota(jnp.int32, (n, n), 0)
    si = lax.broadcasted_iota(jnp.int32, (n, n), 1)
    dirs = ((rf, vf, kkf, lwf, kdf, bf, si <= ti, si < ti, n - 1),
            (rb, vb, kkb, lwb, kdb, bb, si >= ti, si > ti, 0))
    heads = lambda x: jnp.stack([x[:, j * hd:(j + 1) * hd] for j in range(nh)], 0)
    parts = []
    for r_ref, v_ref, kk_ref, lw_ref, kd_ref, b_ref, incl, strict, last in dirs:
        lw = lw_ref[0]
        tri = incl.astype(BF16)
        cum = sum(_dot(tri, part) for part in _split3(lw))
        e_neg = jnp.exp(-cum)
        e_last = jnp.exp(cum[last:last + 1, :] - cum)
        g_scale = jnp.exp(cum[last:last + 1, :])
        parts.append(dict(
            at=heads((-kk_ref[0] * jnp.exp(cum - lw)).astype(BF16)),
            rt=heads((r_ref[0] * jnp.exp(cum)).astype(BF16)),
            bt=heads((b_ref[0] * e_neg).astype(BF16)),
            kt=heads((kd_ref[0] * e_neg).astype(BF16)),
            bl=heads((b_ref[0] * e_last).astype(BF16)),
            kl=heads((kd_ref[0] * e_last).astype(BF16)),
            v=heads(v_ref[0].astype(BF16)),
            gs=heads(g_scale),
            incl=jnp.broadcast_to(incl[None], (nh, n, n)),
            strict=jnp.broadcast_to(strict[None], (nh, n, n))))
    cat = lambda key: jnp.concatenate([parts[0][key], parts[1][key]], 0)
    at, rt, bt, kt, bl, kl, v, gs = (cat(k) for k in ("at", "rt", "bt", "kt", "bl", "kl", "v", "gs"))
    incl, strict = cat("incl"), cat("strict")
    g0 = state[...]
    g0b = g0.astype(BF16)
    m1 = bmm(jnp.concatenate([at, rt], 1), jnp.concatenate([bt, kt], 1), 2, 2)
    a_ab = jnp.where(strict, m1[:, :n, :n], 0.0)
    a_ak = jnp.where(strict, m1[:, :n, n:], 0.0)
    a_rb = jnp.where(incl, m1[:, n:, :n], 0.0)
    a_rk = jnp.where(incl, m1[:, n:, n:], 0.0)
    tinv = jnp.where((ti == si)[None], 1.0, a_ab)
    pw = a_ab
    for _ in range(int(np.log2(n)) - 1):
        pw = bmm(pw, pw, 2, 1)
        tinv = tinv + bmm(tinv, pw, 2, 1)
    rhs = bmm(a_ak, v, 2, 1) + bmm(at, g0b, 2, 2)
    u = bmm(tinv, rhs, 2, 1)
    uv = jnp.concatenate([u.astype(BF16), v], 1)
    y = bmm(jnp.concatenate([a_rb, a_rk], 2), uv, 2, 1) + bmm(rt, g0b, 2, 2)
    state[...] = g0 * gs + bmm(uv, jnp.concatenate([bl, kl], 1), 1, 1)
    for d, y_ref in enumerate((yf_ref, yb_ref)):
        y_ref[0] = jnp.concatenate([y[d * nh + j] for j in range(nh)], axis=1)


def _wkv(r, v, kk, lw0, lw1, kd0, kd1, b0, b1, batch, seq):
    n = WKV_CHUNK
    nc = seq // n
    shp = (batch, seq, RWKV_WIDTH)
    arrs = [a.reshape(shp) for a in (r, v, kk, lw0, kd0, b0, r, v, kk, lw1, kd1, b1)]
    fwd = pl.BlockSpec((1, n, RWKV_WIDTH), lambda b, c: (b, c, 0))
    bwd = pl.BlockSpec((1, n, RWKV_WIDTH), lambda b, c: (b, nc - 1 - c, 0))
    yf, yb = pl.pallas_call(
        _wkv_kernel,
        grid=(batch, nc),
        in_specs=[fwd] * 6 + [bwd] * 6,
        out_specs=[fwd, bwd],
        out_shape=[jax.ShapeDtypeStruct(shp, F32)] * 2,
        scratch_shapes=[pltpu.VMEM((2 * N_RWKV_HEADS, HEAD_DIM, HEAD_DIM), F32)],
        compiler_params=_cparams(("parallel", "arbitrary")),
        name="wkv",
    )(*arrs)
    return yf.reshape(batch * seq, RWKV_WIDTH), yb.reshape(batch * seq, RWKV_WIDTH)


def _mix_kernel(x_ref, ya_ref, yf_ref, yb_ref, r_ref, v_ref, kd0_ref, kd1_ref, g_ref,
                lng_ref, lnb_ref, rk_ref, bd_ref, wout_ref, l1g_ref, l1b_ref, h_ref):
    bd = bd_ref[...].astype(BF16)
    head_sum = lambda t: sum(_dot(part, bd) for part in _split3(t))
    inv = 1.0 / HEAD_DIM
    y = yf_ref[...] + yb_ref[...]
    mu = head_sum(y) * inv
    yc = y - mu
    var = head_sum(yc * yc) * inv
    yn = yc * lax.rsqrt(var + RWKV_GN_EPS) * lng_ref[...] + lnb_ref[...]
    k_mean = 0.5 * (kd0_ref[...] + kd1_ref[...])
    v = v_ref[...]
    bonus = head_sum(r_ref[...] * k_mean * rk_ref[...]) * v
    yr = (yn + bonus) * g_ref[...]
    mix = (_dot(ya_ref[...].astype(BF16), wout_ref[:ATTN_WIDTH, :])
           + _dot(yr.astype(BF16), wout_ref[ATTN_WIDTH:, :]))
    h_ref[...] = _layer_norm(DEEPNORM_ALPHA * x_ref[...] + mix, l1g_ref[...], l1b_ref[...])


def _mix(x2, ya, yf, yb, r, v, kd0, kd1, g, lng, lnb, rk, bd, wout, l1g, l1b, tm=256):
    t, d = x2.shape
    full = lambda a: pl.BlockSpec(a.shape, lambda i: (0,) * a.ndim)
    half = pl.BlockSpec((tm, RWKV_WIDTH), lambda i: (i, 0))
    wide = pl.BlockSpec((tm, d), lambda i: (i, 0))
    return pl.pallas_call(
        _mix_kernel,
        grid=(t // tm,),
        in_specs=[wide] + [half] * 8 + [full(lng), full(lnb), full(rk), full(bd), full(wout),
                                        full(l1g), full(l1b)],
        out_specs=wide,
        out_shape=jax.ShapeDtypeStruct((t, d), F32),
        compiler_params=_cparams(("parallel",)),
        name="mix",
    )(x2, ya, yf, yb, r, v, kd0, kd1, g, lng, lnb, rk, bd, wout, l1g, l1b)


def _top_rows(scs, k, payloads=None):
    n = scs[0].shape[0]
    iota = lax.broadcasted_iota(jnp.int32, scs[0].shape, 0).astype(F32)
    scs = list(scs)
    vals = [[] for _ in scs]
    picks = [[] for _ in scs]
    for _ in range(k):
        for a, sc in enumerate(scs):
            m = jnp.max(sc, axis=0, keepdims=True)
            pos = jnp.min(jnp.where(sc == m, iota, float(n)), axis=0, keepdims=True)
            hit = iota == pos
            vals[a].append(m)
            if payloads is None:
                picks[a].append(pos)
            else:
                picks[a].append(jnp.max(jnp.where(hit, payloads[a], -1.0), axis=0, keepdims=True))
            scs[a] = jnp.where(hit, -jnp.inf, sc)
    return [(jnp.concatenate(v, 0), jnp.concatenate(p, 0)) for v, p in zip(vals, picks)]


def _route_kernel(h_ref, wq_ref, keys_ref, idx_ref, gate_ref, gate_tok_ref, q_scr, idx_scr):
    k = PEER_TOPK
    q_scr[...] = _dot(h_ref[...].astype(BF16), wq_ref[...])

    def head_group(hg, carry):
        heads = [hg * ROUTE_HEADS_PER_STEP + i for i in range(ROUTE_HEADS_PER_STEP)]
        scores = []
        for hh in heads:
            for p in range(2):
                col = pl.multiple_of(hh * PEER_QDIM + p * PEER_HALF, PEER_HALF)
                qp = q_scr[:, pl.ds(col, PEER_HALF)]
                scores.append(_dot_nt(keys_ref[hh, p], qp, HI))
        tops = _top_rows(scores, k)
        cands, cand_ids = [], []
        for a in range(len(heads)):
            (s1, i1), (s2, i2) = tops[2 * a], tops[2 * a + 1]
            cands.append(jnp.concatenate([s1[i:i + 1] + s2[:k // (i + 1)] for i in range(k)], 0))
            cand_ids.append(jnp.concatenate(
                [i1[i:i + 1] * PEER_NKEYS + i2[:k // (i + 1)] for i in range(k)], 0))
        for hh, (cs, ids) in zip(heads, _top_rows(cands, k, cand_ids)):
            e = jnp.exp(cs - jnp.max(cs, axis=0, keepdims=True))
            row = pl.multiple_of(hh * k, k)
            gate_ref[pl.ds(row, k), :] = e / jnp.sum(e, axis=0, keepdims=True)
            idx_scr[pl.ds(row, k), :] = ids.astype(jnp.int32)
        return carry

    lax.fori_loop(0, PEER_HEADS // ROUTE_HEADS_PER_STEP, head_group, 0)
    idx_ref[...] = idx_scr[...].T
    gate_tok_ref[...] = gate_ref[...].T


def _route(h1, wq, keys, tm=256):
    t, d = h1.shape
    return pl.pallas_call(
        _route_kernel,
        grid=(t // tm,),
        in_specs=[
            pl.BlockSpec((tm, d), lambda i: (i, 0)),
            pl.BlockSpec(wq.shape, lambda i: (0, 0)),
            pl.BlockSpec(keys.shape, lambda i: (0, 0, 0, 0)),
        ],
        out_specs=[
            pl.BlockSpec((tm, N_SEL), lambda i: (i, 0)),
            pl.BlockSpec((N_SEL, tm), lambda i: (0, i)),
            pl.BlockSpec((tm, N_SEL), lambda i: (i, 0)),
        ],
        out_shape=[
            jax.ShapeDtypeStruct((t, N_SEL), jnp.int32),
            jax.ShapeDtypeStruct((N_SEL, t), F32),
            jax.ShapeDtypeStruct((t, N_SEL), F32),
        ],
        scratch_shapes=[pltpu.VMEM((tm, PEER_HEADS * PEER_QDIM), F32),
                        pltpu.VMEM((N_SEL, tm), jnp.int32)],
        compiler_params=_cparams(("parallel",)),
        name="route",
    )(h1, wq, keys)


def _peer_kernel(idx_ref, gate_ref, h_ref, uv_hbm, l2g_ref, l2b_ref, o_ref, buf, f_scr, sem, *, tb):
    d = h_ref.shape[1]

    def issue(t, slot):
        for j in range(N_SEL):
            pltpu.make_async_copy(uv_hbm.at[pl.ds(idx_ref[t, j], 1)], buf.at[slot, pl.ds(j, 1)],
                                  sem.at[slot]).start()

    def wait_all(slot):
        pltpu.make_async_copy(uv_hbm.at[pl.ds(0, N_SEL)], buf.at[slot], sem.at[slot]).wait()

    for t_ahead in range(PEER_SLOTS - 1):
        issue(t_ahead, t_ahead)
    lane = lax.broadcasted_iota(jnp.int32, (N_SEL, tb), 1)

    def body(t, carry):
        slot = t % PEER_SLOTS

        @pl.when(t + PEER_SLOTS - 1 < tb)
        def _():
            issue(t + PEER_SLOTS - 1, (t + PEER_SLOTS - 1) % PEER_SLOTS)

        wait_all(slot)
        z = jnp.sum(buf[slot, :, 0:d] * h_ref[pl.ds(t, 1), :], axis=1, keepdims=True)
        gcol = jnp.sum(jnp.where(lane == t, gate_ref[...], 0.0), axis=1, keepdims=True)
        coef = gcol * (0.5 * z * (1.0 + lax.erf(z * (2.0 ** -0.5))))
        f_scr[pl.ds(t, 1), :] = jnp.sum(coef * buf[slot, :, d:2 * d], axis=0, keepdims=True)
        return carry

    lax.fori_loop(0, tb, body, 0)
    o_ref[...] = _layer_norm(DEEPNORM_ALPHA * h_ref[...] + f_scr[...], l2g_ref[...], l2b_ref[...])


def _peer(idx, gate_t, h1, uv, l2g, l2b, tok0, ntok, tb=PEER_TB):
    d = h1.shape[1]
    b0 = tok0 // tb
    return pl.pallas_call(
        functools.partial(_peer_kernel, tb=tb),
        grid=(ntok // tb,),
        in_specs=[
            pl.BlockSpec((tb, N_SEL), lambda i: (b0 + i, 0), memory_space=pltpu.SMEM),
            pl.BlockSpec((N_SEL, tb), lambda i: (0, b0 + i)),
            pl.BlockSpec((tb, d), lambda i: (b0 + i, 0)),
            pl.BlockSpec(memory_space=pl.ANY),
            pl.BlockSpec(l2g.shape, lambda i: (0, 0)),
            pl.BlockSpec(l2b.shape, lambda i: (0, 0)),
        ],
        out_specs=pl.BlockSpec((tb, d), lambda i: (i, 0)),
        out_shape=jax.ShapeDtypeStruct((ntok, d), F32),
        scratch_shapes=[pltpu.VMEM((PEER_SLOTS, N_SEL, 2 * d), F32),
                        pltpu.VMEM((tb, d), F32),
                        pltpu.SemaphoreType.DMA((PEER_SLOTS,))],
        compiler_params=_cparams(("arbitrary",)),
        name="peer",
    )(idx, gate_t, h1, uv, l2g, l2b)


def _sc_mesh():
    return plsc.VectorSubcoreMesh(core_axis_name="c", subcore_axis_name="s")


def _worker_id():
    return lax.axis_index("s") * 2 + lax.axis_index("c")


def _sc_z_body(u_hbm, idx_hbm, h_hbm, z_hbm, idx_v, h_v, rows_v, z_v, sem, *, tpw, d, tok0):
    ngrp = N_SEL // SC_GROUP
    nchunk = d // SC_LANES
    nsteps = tpw * ngrp
    base = _worker_id() * tpw

    def gather(tok_buf, g, buf):
        return pltpu.make_async_copy(u_hbm.at[idx_v.at[tok_buf, g]], rows_v.at[buf], sem.at[buf])

    pltpu.sync_copy(idx_hbm.at[tok0 + base], idx_v.at[0])
    gather(0, 0, 0).start()

    @pl.loop(0, nsteps)
    def _(q):
        t_loc = q // ngrp
        g = q % ngrp
        buf = q % 2
        t = base + t_loc

        @pl.when(g == 0)
        def _():
            pltpu.sync_copy(h_hbm.at[tok0 + t], h_v)

            @pl.loop(0, N_SEL)
            def _(r):
                z_v[pl.ds(pl.multiple_of(r * SC_LANES, SC_LANES), SC_LANES)] = jnp.zeros((SC_LANES,), F32)

        @pl.when(q + 1 < nsteps)
        def _():
            tn = (q + 1) // ngrp
            gn = (q + 1) % ngrp

            @pl.when(gn == 0)
            def _():
                pltpu.sync_copy(idx_hbm.at[tok0 + base + tn], idx_v.at[tn % 2])

            gather(tn % 2, gn, 1 - buf).start()

        gather(t_loc % 2, g, buf).wait()

        @pl.loop(0, SC_GROUP, step=4)
        def _(rb):
            @pl.loop(0, nchunk, step=SC_CHUNK_UNROLL)
            def _(c0):
                accs = [jnp.zeros((SC_LANES,), F32) for _ in range(4)]
                for cc in range(SC_CHUNK_UNROLL):
                    sl = pl.ds(pl.multiple_of((c0 + cc) * SC_LANES, SC_LANES), SC_LANES)
                    hc = h_v[sl]
                    for i in range(4):
                        accs[i] = accs[i] + rows_v[buf, rb + i, sl] * hc
                for i in range(4):
                    row = pl.multiple_of((g * SC_GROUP + rb + i) * SC_LANES, SC_LANES)
                    plsc.addupdate(z_v.at[pl.ds(row, SC_LANES)], accs[i])

        @pl.when(g == ngrp - 1)
        def _():
            pltpu.sync_copy(z_v, z_hbm.at[t])


def _sc_z(u, idx, h, tok0, ntok):
    t, d = h.shape
    tpw = ntok // SC_WORKERS
    idx4 = idx.reshape(t, N_SEL // SC_GROUP, SC_GROUP)
    body = functools.partial(_sc_z_body, tpw=tpw, d=d, tok0=tok0)
    return pl.kernel(
        body,
        out_type=jax.ShapeDtypeStruct((ntok, N_SEL * SC_LANES), F32),
        mesh=_sc_mesh(),
        scratch_types=[
            pltpu.VMEM((2, N_SEL // SC_GROUP, SC_GROUP), jnp.int32),
            pltpu.VMEM((d,), F32),
            pltpu.VMEM((2, SC_GROUP, d), F32),
            pltpu.VMEM((N_SEL * SC_LANES,), F32),
            pltpu.SemaphoreType.DMA((2,)),
        ],
        name="sc_z",
    )(u, idx4, h)


def _sc_out_body(v_hbm, idx_hbm, coef_hbm, f_hbm, idx_v, coef_v, rows_v, out_v, sem, *, tpw, d, tok0):
    ngrp = N_SEL // SC_GROUP
    nchunk = d // SC_LANES
    nsteps = tpw * ngrp
    base = _worker_id() * tpw

    def gather(tok_buf, g, buf):
        return pltpu.make_async_copy(v_hbm.at[idx_v.at[tok_buf, g]], rows_v.at[buf], sem.at[buf])

    pltpu.sync_copy(idx_hbm.at[tok0 + base], idx_v.at[0])
    gather(0, 0, 0).start()

    @pl.loop(0, nsteps)
    def _(q):
        t_loc = q // ngrp
        g = q % ngrp
        buf = q % 2
        t = base + t_loc

        @pl.when(g == 0)
        def _():
            pltpu.sync_copy(coef_hbm.at[t], coef_v)

        @pl.when(q + 1 < nsteps)
        def _():
            tn = (q + 1) // ngrp
            gn = (q + 1) % ngrp

            @pl.when(gn == 0)
            def _():
                pltpu.sync_copy(idx_hbm.at[tok0 + base + tn], idx_v.at[tn % 2])

            gather(tn % 2, gn, 1 - buf).start()

        gather(t_loc % 2, g, buf).wait()

        @pl.loop(0, nchunk, step=SC_OUT_CHUNKS)
        def _(c0):
            sls = [pl.ds(pl.multiple_of((c0 + cc) * SC_LANES, SC_LANES), SC_LANES)
                   for cc in range(SC_OUT_CHUNKS)]
            accs = [None] * SC_OUT_CHUNKS
            for r in range(SC_GROUP):
                cf = coef_v[pl.ds(pl.multiple_of((g * SC_GROUP + r) * SC_LANES, SC_LANES), SC_LANES)]
                for cc in range(SC_OUT_CHUNKS):
                    p = rows_v[buf, r, sls[cc]] * cf
                    accs[cc] = p if accs[cc] is None else accs[cc] + p
            for cc in range(SC_OUT_CHUNKS):
                @pl.when(g == 0)
                def _():
                    out_v[sls[cc]] = accs[cc]

                @pl.when(g != 0)
                def _():
                    plsc.addupdate(out_v.at[sls[cc]], accs[cc])

        @pl.when(g == ngrp - 1)
        def _():
            pltpu.sync_copy(out_v, f_hbm.at[t])


def _sc_out(v, idx, coef16, tok0):
    t = idx.shape[0]
    ntok = coef16.shape[0]
    d = v.shape[1]
    tpw = ntok // SC_WORKERS
    idx4 = idx.reshape(t, N_SEL // SC_GROUP, SC_GROUP)
    body = functools.partial(_sc_out_body, tpw=tpw, d=d, tok0=tok0)
    return pl.kernel(
        body,
        out_type=jax.ShapeDtypeStruct((ntok, d), F32),
        mesh=_sc_mesh(),
        scratch_types=[
            pltpu.VMEM((2, N_SEL // SC_GROUP, SC_GROUP), jnp.int32),
            pltpu.VMEM((N_SEL * SC_LANES,), F32),
            pltpu.VMEM((2, SC_GROUP, d), F32),
            pltpu.VMEM((d,), F32),
            pltpu.SemaphoreType.DMA((2,)),
        ],
        name="sc_out",
    )(v, idx4, coef16)


def _coef_kernel(zp_ref, gate_ref, sel_ref, selt_ref, after_ref, o_ref):
    z = sum(_dot(part, sel_ref[...]) for part in _split3(zp_ref[...]))
    coef = gate_ref[...] * (0.5 * z * (1.0 + lax.erf(z * (2.0 ** -0.5))))
    o_ref[...] = sum(_dot(part, selt_ref[...]) for part in _split3(coef))


def _coef(zp, gate_tok, tok0, after, tm=256):
    ntok, wide = zp.shape
    b0 = tok0 // tm
    grp = np.arange(wide) // SC_LANES
    sel = jnp.asarray(grp[:, None] == np.arange(N_SEL)[None, :], BF16)
    return pl.pallas_call(
        _coef_kernel,
        grid=(ntok // tm,),
        in_specs=[
            pl.BlockSpec((tm, wide), lambda i: (i, 0)),
            pl.BlockSpec((tm, N_SEL), lambda i: (b0 + i, 0)),
            pl.BlockSpec(sel.shape, lambda i: (0, 0)),
            pl.BlockSpec(sel.shape[::-1], lambda i: (0, 0)),
            pl.BlockSpec((SUBLANES, LANES), lambda i: (0, 0)),
        ],
        out_specs=pl.BlockSpec((tm, wide), lambda i: (i, 0)),
        out_shape=jax.ShapeDtypeStruct((ntok, wide), F32),
        compiler_params=_cparams(("parallel",)),
        name="coef",
    )(zp, gate_tok, sel, sel.T, after)


def _ln_out_kernel(h_ref, f_ref, g_ref, b_ref, o_ref):
    o_ref[...] = _layer_norm(DEEPNORM_ALPHA * h_ref[...] + f_ref[...], g_ref[...], b_ref[...])


def _ln_out(h1, f, l2g, l2b, tok0, tm=256):
    ntok, d = f.shape
    b0 = tok0 // tm
    return pl.pallas_call(
        _ln_out_kernel,
        grid=(ntok // tm,),
        in_specs=[
            pl.BlockSpec((tm, d), lambda i: (b0 + i, 0)),
            pl.BlockSpec((tm, d), lambda i: (i, 0)),
            pl.BlockSpec(l2g.shape, lambda i: (0, 0)),
            pl.BlockSpec(l2b.shape, lambda i: (0, 0)),
        ],
        out_specs=pl.BlockSpec((tm, d), lambda i: (i, 0)),
        out_shape=jax.ShapeDtypeStruct((ntok, d), F32),
        compiler_params=_cparams(("parallel",)),
        name="ln_out",
    )(h1, f, l2g, l2b)


def _rope_tables(seq):
    pos = jnp.arange(seq, dtype=F32)
    inv_freq = ROPE_THETA ** (-jnp.arange(0, HEAD_DIM, 2, dtype=F32) / HEAD_DIM)
    ang = pos[:, None] * inv_freq[None, :]
    ang = jnp.concatenate([ang, ang], -1)
    sign = jnp.concatenate([-jnp.ones((HEAD_DIM // 2,), F32), jnp.ones((HEAD_DIM // 2,), F32)])
    reps = LANES // HEAD_DIM
    return jnp.tile(jnp.cos(ang), (1, reps)), jnp.tile(jnp.sin(ang) * sign, (1, reps))


def _layer(h2d, batch, seq, w_in, mu_prev, mu_next, w0, w2, a0, a2, g2, k_k, k_a, r_k, lnx_g, lnx_b,
           sink, w_out, ln1_g, ln1_b, peer_wq, peer_keys, peer_u, peer_v, ln2_g, ln2_b):
    w = RWKV_WIDTH
    row = lambda a: a.reshape(1, -1).astype(F32)
    w_in_p = jnp.pad(w_in, ((0, 0), (0, RWKV_COLS_PAD - RWKV_COLS))).astype(BF16)
    mup = jnp.pad(mu_prev, (0, RWKV_COLS_PAD - RWKV_COLS)).reshape(1, -1)
    mun = jnp.pad(mu_next, (0, RWKV_COLS_PAD - RWKV_COLS)).reshape(1, -1)
    wmix = jnp.zeros((LANES, 4 * w), F32)
    for d in range(2):
        wmix = wmix.at[d * DECAY_RANK:(d + 1) * DECAY_RANK, d * w:(d + 1) * w].set(w2[d])
        r0 = 2 * DECAY_RANK + d * ICLR_RANK
        wmix = wmix.at[r0:r0 + ICLR_RANK, (2 + d) * w:(3 + d) * w].set(a2[d])
    g2p = jnp.pad(g2, ((0, LANES - GATE_RANK), (0, 0)))
    head_of = np.arange(w) // HEAD_DIM
    bd = jnp.asarray(head_of[:, None] == head_of[None, :], F32)
    cos_t, sin_t = _rope_tables(seq)

    wq_b, keys_f, wout_b = peer_wq.astype(BF16), peer_keys.astype(F32), w_out.astype(BF16)

    def to_routing(x2d, nb):
        qkv, pr = _proj(x2d, w_in_p, cos_t, sin_t, seq)
        y_attn = _attention(qkv.reshape(nb, seq, ATTN_COLS), sink.astype(F32))
        r, v, kk, g, lw0, lw1, kd0, kd1, b0, b1 = _prep(
            pr, mup, mun, wmix, w0.astype(F32), a0.astype(F32), g2p, row(k_k), row(k_a), bd, seq)
        yf, yb = _wkv(r, v, kk, lw0, lw1, kd0, kd1, b0, b1, nb, seq)
        h1 = _mix(x2d, y_attn.reshape(nb * seq, ATTN_WIDTH), yf, yb, r, v, kd0, kd1, g,
                  row(lnx_g), row(lnx_b), row(r_k), bd, wout_b, row(ln1_g), row(ln1_b))
        return (h1,) + tuple(_route(h1, wq_b, keys_f))

    uv = jnp.concatenate([peer_u, peer_v], axis=1)
    l2g, l2b = row(ln2_g), row(ln2_b)

    nb_sc = batch * PEER_SC_SEQ_SHARE_8THS // 8
    t_sc = nb_sc * seq
    t_tc = (batch - nb_sc) * seq
    n_a = (t_tc * PEER_TC_FIRST_CALL_32NDS // 32) // PEER_TB * PEER_TB
    n_b = t_tc - n_a
    assert 0 < nb_sc < batch and t_sc % (SC_WORKERS * SUBLANES) == 0 and n_a > 0 and n_b % PEER_TB == 0
    h1s, idxs, _, gate_tok_s = to_routing(h2d[:t_sc], nb_sc)
    zp = _sc_z(peer_u.astype(F32), idxs, h1s, 0, t_sc)
    h1t, idxt, gate_t_t, _ = to_routing(h2d[t_sc:], batch - nb_sc)
    out_a = _peer(idxt, gate_t_t, h1t, uv, l2g, l2b, 0, n_a)
    f_sc = _sc_out(peer_v.astype(F32), idxs, _coef(zp, gate_tok_s, 0, out_a), 0)
    out_b = _peer(idxt, gate_t_t, h1t, uv, l2g, l2b, n_a, n_b)
    out_sc = _ln_out(h1s, f_sc, l2g, l2b, 0)
    return jnp.concatenate([out_sc, out_a, out_b], axis=0)


def kernel(x, w_in, mu_prev, mu_next, w0, w2, a0, a2, g2, k_k, k_a, r_k, lnx_g, lnx_b, sink, w_out,
           ln1_g, ln1_b, peer_wq, peer_keys, peer_u, peer_v, ln2_g, ln2_b):
    batch, seq, d = x.shape
    h = x.reshape(batch * seq, d)
    for l in range(DEPTH):
        h = _layer(h, batch, seq, w_in[l], mu_prev[l], mu_next[l], w0[l], w2[l], a0[l], a2[l], g2[l],
                   k_k[l], k_a[l], r_k[l], lnx_g[l], lnx_b[l], sink[l], w_out[l], ln1_g[l], ln1_b[l],
                   peer_wq[l], peer_keys[l], peer_u[l], peer_v[l], ln2_g[l], ln2_b[l])
    return h.reshape(batch, seq, d)
```

```python
import functools

import numpy as np
import jax
import jax.numpy as jnp
from jax import lax
from jax.experimental import pallas as pl
from jax.experimental.pallas import tpu as pltpu
from jax.experimental.pallas import tpu_sc as plsc

F32 = jnp.float32
BF16 = jnp.bfloat16
HI = lax.Precision.HIGHEST

HEAD_DIM = 64
N_Q_HEADS = 8
N_KV_HEADS = 2
Q_PER_KV = N_Q_HEADS // N_KV_HEADS
ATTN_WIDTH = N_Q_HEADS * HEAD_DIM
ATTN_KV_WIDTH = N_KV_HEADS * HEAD_DIM
ATTN_COLS = ATTN_WIDTH + 2 * ATTN_KV_WIDTH
WINDOW = 128
ATTN_BLOCK = 128
ROPE_THETA = 10000.0
N_RWKV_HEADS = 8
RWKV_WIDTH = N_RWKV_HEADS * HEAD_DIM
DECAY_RANK = 32
ICLR_RANK = 32
GATE_RANK = 96
RWKV_COLS = 3 * RWKV_WIDTH + 2 * DECAY_RANK + 2 * ICLR_RANK + GATE_RANK
RWKV_COLS_PAD = 1792
RWKV_GN_EPS = 64e-5
PEER_HEADS = 8
PEER_NKEYS = 128
PEER_QDIM = 256
PEER_HALF = PEER_QDIM // 2
PEER_TOPK = 16
N_SEL = PEER_HEADS * PEER_TOPK
LN_EPS = 1e-5
DEPTH = 1
DEEPNORM_ALPHA = (2.0 * DEPTH) ** 0.25

LANES = 128
SUBLANES = 8
PREP_OUT_DTYPES = (BF16, BF16, BF16, F32, F32, F32, BF16, BF16, BF16, BF16)
WKV_CHUNK = 64
VMEM_LIMIT = 48 * 1024 * 1024
PEER_SLOTS = 3
PEER_TB = 256
ROUTE_HEADS_PER_STEP = 8
SC_LANES = 16
SC_WORKERS = 32
SC_GROUP = 32
SC_CHUNK_UNROLL = 8
SC_OUT_CHUNKS = 4
PEER_FIRST_GROUP_SEQ_8THS = 5
PEER_SC_TOKEN_SHARE_128THS = 74
PEER_TC_FIRST_CALL_32NDS = 11


def _cparams(sem):
    return pltpu.CompilerParams(dimension_semantics=sem, vmem_limit_bytes=VMEM_LIMIT)


def _dot(a, b, precision=None):
    return jnp.dot(a, b, preferred_element_type=F32, precision=precision)


def _dot_nt(a, b, precision=None):
    return lax.dot_general(a, b, (((1,), (1,)), ((), ())), preferred_element_type=F32,
                           precision=precision)


def _layer_norm(z, g, b):
    mu = jnp.mean(z, -1, keepdims=True)
    zc = z - mu
    var = jnp.mean(zc * zc, -1, keepdims=True)
    return zc * lax.rsqrt(var + LN_EPS) * g + b


def _proj_kernel(x_ref, w_ref, cos_ref, sin_ref, qkv_ref, pr_ref):
    xb = x_ref[...].astype(BF16)
    cos = cos_ref[...]
    sin = sin_ref[...]
    lane = lax.broadcasted_iota(jnp.int32, cos.shape, 1)
    first_half = (lane & (HEAD_DIM // 2)) == 0

    def rope(t):
        rot = jnp.where(first_half, pltpu.roll(t, LANES - HEAD_DIM // 2, 1),
                        pltpu.roll(t, HEAD_DIM // 2, 1))
        return t * cos + rot * sin

    for c in range(0, ATTN_COLS, 2 * LANES):
        acc = _dot(xb, w_ref[:, c:c + 2 * LANES])
        for half in range(2):
            col = c + half * LANES
            t = acc[:, half * LANES:(half + 1) * LANES]
            if col < ATTN_WIDTH + ATTN_KV_WIDTH:
                t = rope(t)
            qkv_ref[:, col:col + LANES] = t
    for c in range(0, RWKV_COLS_PAD, 2 * LANES):
        pr_ref[:, c:c + 2 * LANES] = _dot(xb, w_ref[:, ATTN_COLS + c:ATTN_COLS + c + 2 * LANES])


def _proj(x2, w_in_p, cos_t, sin_t, seq, tm=512):
    t, d = x2.shape
    n_pos = seq // tm
    return pl.pallas_call(
        _proj_kernel,
        grid=(t // tm,),
        in_specs=[
            pl.BlockSpec((tm, d), lambda i: (i, 0)),
            pl.BlockSpec(w_in_p.shape, lambda i: (0, 0)),
            pl.BlockSpec((tm, LANES), lambda i: (i % n_pos, 0)),
            pl.BlockSpec((tm, LANES), lambda i: (i % n_pos, 0)),
        ],
        out_specs=[
            pl.BlockSpec((tm, ATTN_COLS), lambda i: (i, 0)),
            pl.BlockSpec((tm, RWKV_COLS_PAD), lambda i: (i, 0)),
        ],
        out_shape=[
            jax.ShapeDtypeStruct((t, ATTN_COLS), F32),
            jax.ShapeDtypeStruct((t, RWKV_COLS_PAD), F32),
        ],
        compiler_params=_cparams(("parallel",)),
        name="proj",
    )(x2, w_in_p, cos_t, sin_t)


def _attn_kernel(sink_ref, q_ref, kp_ref, kc_ref, kn_ref, vp_ref, vc_ref, vn_ref, o_ref, *, nb):
    n = pl.program_id(1)
    blk = ATTN_BLOCK
    rows = Q_PER_KV * blk
    q = q_ref[0] * (HEAD_DIM ** -0.5)
    kwin = jnp.concatenate([kp_ref[0], kc_ref[0], kn_ref[0]], axis=0)
    vwin = jnp.concatenate([vp_ref[0], vc_ref[0], vn_ref[0]], axis=0)
    qi = lax.broadcasted_iota(jnp.int32, (rows, 3 * blk), 0) & (blk - 1)
    kj = lax.broadcasted_iota(jnp.int32, (rows, 3 * blk), 1)
    dist = kj - qi
    valid = ((dist >= blk - WINDOW) & (dist <= blk + WINDOW)
             & ((kj >= blk) | (n > 0)) & ((kj < 2 * blk) | (n < nb - 1)))
    rowg = lax.broadcasted_iota(jnp.int32, (rows, 1), 0) // blk
    outs = []
    for h in range(N_KV_HEADS):
        qs = jnp.concatenate(
            [q[:, (Q_PER_KV * h + g) * HEAD_DIM:(Q_PER_KV * h + g + 1) * HEAD_DIM]
             for g in range(Q_PER_KV)], axis=0)
        kh = kwin[:, h * HEAD_DIM:(h + 1) * HEAD_DIM]
        vh = vwin[:, h * HEAD_DIM:(h + 1) * HEAD_DIM]
        logits = _dot_nt(qs.astype(BF16), kh.astype(BF16))
        logits = jnp.where(valid, logits, -1e30)
        sk = jnp.zeros((rows, 1), F32)
        for g in range(Q_PER_KV):
            sk = jnp.where(rowg == g, sink_ref[Q_PER_KV * h + g], sk)
        m = jnp.maximum(jnp.max(logits, -1, keepdims=True), sk)
        e = jnp.exp(logits - m)
        den = jnp.sum(e, -1, keepdims=True) + jnp.exp(sk - m)
        p = e / den
        o = _dot(p.astype(BF16), vh.astype(BF16))
        for g in range(Q_PER_KV):
            outs.append(o[g * blk:(g + 1) * blk])
    o_ref[0] = jnp.concatenate(outs, axis=1)


def _attention(qkv3, sink):
    b, s, _ = qkv3.shape
    blk = ATTN_BLOCK
    nb = s // blk
    kcol = ATTN_WIDTH // LANES
    vcol = kcol + 1

    def spec(col, shift):
        def imap(bi, n):
            return (bi, jnp.clip(n + shift, 0, nb - 1), col)
        return pl.BlockSpec((1, blk, LANES), imap)

    return pl.pallas_call(
        functools.partial(_attn_kernel, nb=nb),
        grid=(b, nb),
        in_specs=[
            pl.BlockSpec(memory_space=pltpu.SMEM),
            pl.BlockSpec((1, blk, ATTN_WIDTH), lambda bi, n: (bi, n, 0)),
            spec(kcol, -1), spec(kcol, 0), spec(kcol, 1),
            spec(vcol, -1), spec(vcol, 0), spec(vcol, 1),
        ],
        out_specs=pl.BlockSpec((1, blk, ATTN_WIDTH), lambda bi, n: (bi, n, 0)),
        out_shape=jax.ShapeDtypeStruct((b, s, ATTN_WIDTH), F32),
        compiler_params=_cparams(("parallel", "parallel")),
        name="attn",
    )(sink, qkv3, qkv3, qkv3, qkv3, qkv3, qkv3, qkv3)


def _softplus(x):
    return jnp.maximum(x, 0.0) + jnp.log(1.0 + jnp.exp(-jnp.abs(x)))


def _sigmoid(x):
    return 1.0 / (1.0 + jnp.exp(-x))


def _prep_kernel(p_ref, hp_ref, hn_ref, mup_ref, mun_ref, wmix_ref, w0_ref, a0_ref, g2_ref,
                 kk_ref, ka_ref, bd_ref,
                 r_o, v_o, kk_o, g_o, lw0_o, lw1_o, kd0_o, kd1_o, b0_o, b1_o, *, tm, seq):
    i = pl.program_id(0)
    row = lax.broadcasted_iota(jnp.int32, (tm, 1), 0)
    seq_start = (i * tm) % seq == 0
    seq_end = ((i + 1) * tm) % seq == 0

    def shifted(c0, c1):
        p = p_ref[:, c0:c1]
        prev_row = jnp.where(seq_start, 0.0, hp_ref[SUBLANES - 1:SUBLANES, c0:c1])
        next_row = jnp.where(seq_end, 0.0, hn_ref[0:1, c0:c1])
        p_prev = jnp.where(row == 0, prev_row, pltpu.roll(p, 1, 0))
        p_next = jnp.where(row == tm - 1, next_row, pltpu.roll(p, tm - 1, 0))
        return p + mup_ref[:, c0:c1] * (p_prev - p) + mun_ref[:, c0:c1] * (p_next - p)

    w = RWKV_WIDTH
    r = shifted(0, w)
    k = shifted(w, 2 * w)
    v = shifted(2 * w, 3 * w)
    codes = shifted(3 * w, 3 * w + LANES)
    gd = shifted(3 * w + LANES, 3 * w + 2 * LANES)
    r_o[...] = r.astype(r_o.dtype)
    v_o[...] = v.astype(v_o.dtype)

    lane = lax.broadcasted_iota(jnp.int32, codes.shape, 1)
    codes = jnp.where(lane < 2 * DECAY_RANK, jnp.tanh(codes), codes)
    mm = _dot(codes, wmix_ref[...], HI)
    g_o[...] = _dot(_sigmoid(gd), g2_ref[...], HI)

    kkv = k * kk_ref[...]
    bd = bd_ref[...].astype(BF16)
    ss = sum(_dot(part, bd) for part in _split3(kkv * kkv))
    kkn = kkv * lax.rsqrt(jnp.maximum(ss, 1e-24))
    kk_o[...] = kkn.astype(kk_o.dtype)

    ka = ka_ref[...]
    for d, (lw_o, kd_o, b_o) in enumerate(((lw0_o, kd0_o, b0_o), (lw1_o, kd1_o, b1_o))):
        w_log = -_softplus(-(w0_ref[d:d + 1, :] + mm[:, d * w:(d + 1) * w])) - 0.5
        lw_o[...] = -jnp.exp(w_log)
        a = _sigmoid(a0_ref[d:d + 1, :] + mm[:, (2 + d) * w:(3 + d) * w])
        kd_o[...] = (k * (1.0 + (a - 1.0) * ka)).astype(kd_o.dtype)
        b_o[...] = (kkn * a).astype(b_o.dtype)


def _prep(pr, mup, mun, wmix, w0, a0, g2p, k_k, k_a, bd, seq, tm=256):
    t = pr.shape[0]
    nblk8 = t // SUBLANES
    per = tm // SUBLANES
    full = lambda a: pl.BlockSpec(a.shape, lambda i: (0,) * a.ndim)
    out = pl.BlockSpec((tm, RWKV_WIDTH), lambda i: (i, 0))
    return pl.pallas_call(
        functools.partial(_prep_kernel, tm=tm, seq=seq),
        grid=(t // tm,),
        in_specs=[
            pl.BlockSpec((tm, RWKV_COLS_PAD), lambda i: (i, 0)),
            pl.BlockSpec((SUBLANES, RWKV_COLS_PAD), lambda i: (jnp.maximum(i * per - 1, 0), 0)),
            pl.BlockSpec((SUBLANES, RWKV_COLS_PAD),
                         lambda i: (jnp.minimum((i + 1) * per, nblk8 - 1), 0)),
            full(mup), full(mun), full(wmix), full(w0), full(a0), full(g2p),
            full(k_k), full(k_a), full(bd),
        ],
        out_specs=[out] * 10,
        out_shape=[jax.ShapeDtypeStruct((t, RWKV_WIDTH), dt) for dt in PREP_OUT_DTYPES],
        compiler_params=_cparams(("parallel",)),
        name="prep",
    )(pr, pr, pr, mup, mun, wmix, w0, a0, g2p, k_k, k_a, bd)


def _split3(x):
    hi = x.astype(BF16)
    r1 = x - hi.astype(F32)
    mid = r1.astype(BF16)
    lo = (r1 - mid.astype(F32)).astype(BF16)
    return hi, mid, lo


def _wkv_kernel(rf, vf, kkf, lwf, kdf, bf, rb, vb, kkb, lwb, kdb, bb, yf_ref, yb_ref, state):
    c = pl.program_id(1)
    n = WKV_CHUNK
    hd = HEAD_DIM
    nh = N_RWKV_HEADS
    def bmm(a, b, ca, cb):
        return lax.dot_general(a.astype(BF16), b.astype(BF16), (((ca,), (cb,)), ((0,), (0,))),
                               preferred_element_type=F32)

    @pl.when(c == 0)
    def _():
        state[...] = jnp.zeros_like(state)

    ti = lax.broadcasted_iota(jnp.int32, (n, n), 0)
    si = lax.broadcasted_iota(jnp.int32, (n, n), 1)
    dirs = ((rf, vf, kkf, lwf, kdf, bf, si <= ti, si < ti, n - 1),
            (rb, vb, kkb, lwb, kdb, bb, si >= ti, si > ti, 0))
    heads = lambda x: jnp.stack([x[:, j * hd:(j + 1) * hd] for j in range(nh)], 0)
    parts = []
    for r_ref, v_ref, kk_ref, lw_ref, kd_ref, b_ref, incl, strict, last in dirs:
        lw = lw_ref[0]
        tri = incl.astype(BF16)
        cum = sum(_dot(tri, part) for part in _split3(lw))
        e_neg = jnp.exp(-cum)
        e_last = jnp.exp(cum[last:last + 1, :] - cum)
        g_scale = jnp.exp(cum[last:last + 1, :])
        parts.append(dict(
            at=heads((-kk_ref[0] * jnp.exp(cum - lw)).astype(BF16)),
            rt=heads((r_ref[0] * jnp.exp(cum)).astype(BF16)),
            bt=heads((b_ref[0] * e_neg).astype(BF16)),
            kt=heads((kd_ref[0] * e_neg).astype(BF16)),
            bl=heads((b_ref[0] * e_last).astype(BF16)),
            kl=heads((kd_ref[0] * e_last).astype(BF16)),
            v=heads(v_ref[0].astype(BF16)),
            gs=heads(g_scale),
            incl=jnp.broadcast_to(incl[None], (nh, n, n)),
            strict=jnp.broadcast_to(strict[None], (nh, n, n))))
    cat = lambda key: jnp.concatenate([parts[0][key], parts[1][key]], 0)
    at, rt, bt, kt, bl, kl, v, gs = (cat(k) for k in ("at", "rt", "bt", "kt", "bl", "kl", "v", "gs"))
    incl, strict = cat("incl"), cat("strict")
    g0 = state[...]
    g0b = g0.astype(BF16)
    m1 = bmm(jnp.concatenate([at, rt], 1), jnp.concatenate([bt, kt], 1), 2, 2)
    a_ab = jnp.where(strict, m1[:, :n, :n], 0.0)
    a_ak = jnp.where(strict, m1[:, :n, n:], 0.0)
    a_rb = jnp.where(incl, m1[:, n:, :n], 0.0)
    a_rk = jnp.where(incl, m1[:, n:, n:], 0.0)
    tinv = jnp.where((ti == si)[None], 1.0, a_ab)
    pw = a_ab
    for _ in range(int(np.log2(n)) - 1):
        pw = bmm(pw, pw, 2, 1)
        tinv = tinv + bmm(tinv, pw, 2, 1)
    rhs = bmm(a_ak, v, 2, 1) + bmm(at, g0b, 2, 2)
    u = bmm(tinv, rhs, 2, 1)
    uv = jnp.concatenate([u.astype(BF16), v], 1)
    y = bmm(jnp.concatenate([a_rb, a_rk], 2), uv, 2, 1) + bmm(rt, g0b, 2, 2)
    state[...] = g0 * gs + bmm(uv, jnp.concatenate([bl, kl], 1), 1, 1)
    for d, y_ref in enumerate((yf_ref, yb_ref)):
        y_ref[0] = jnp.concatenate([y[d * nh + j] for j in range(nh)], axis=1)


def _wkv(r, v, kk, lw0, lw1, kd0, kd1, b0, b1, batch, seq):
    n = WKV_CHUNK
    nc = seq // n
    shp = (batch, seq, RWKV_WIDTH)
    arrs = [a.reshape(shp) for a in (r, v, kk, lw0, kd0, b0, r, v, kk, lw1, kd1, b1)]
    fwd = pl.BlockSpec((1, n, RWKV_WIDTH), lambda b, c: (b, c, 0))
    bwd = pl.BlockSpec((1, n, RWKV_WIDTH), lambda b, c: (b, nc - 1 - c, 0))
    yf, yb = pl.pallas_call(
        _wkv_kernel,
        grid=(batch, nc),
        in_specs=[fwd] * 6 + [bwd] * 6,
        out_specs=[fwd, bwd],
        out_shape=[jax.ShapeDtypeStruct(shp, F32)] * 2,
        scratch_shapes=[pltpu.VMEM((2 * N_RWKV_HEADS, HEAD_DIM, HEAD_DIM), F32)],
        compiler_params=_cparams(("parallel", "arbitrary")),
        name="wkv",
    )(*arrs)
    return yf.reshape(batch * seq, RWKV_WIDTH), yb.reshape(batch * seq, RWKV_WIDTH)


def _mix_kernel(x_ref, ya_ref, yf_ref, yb_ref, r_ref, v_ref, kd0_ref, kd1_ref, g_ref,
                lng_ref, lnb_ref, rk_ref, bd_ref, wout_ref, l1g_ref, l1b_ref, h_ref):
    bd = bd_ref[...].astype(BF16)
    head_sum = lambda t: sum(_dot(part, bd) for part in _split3(t))
    inv = 1.0 / HEAD_DIM
    y = yf_ref[...] + yb_ref[...]
    mu = head_sum(y) * inv
    yc = y - mu
    var = head_sum(yc * yc) * inv
    yn = yc * lax.rsqrt(var + RWKV_GN_EPS) * lng_ref[...] + lnb_ref[...]
    k_mean = 0.5 * (kd0_ref[...] + kd1_ref[...])
    v = v_ref[...]
    bonus = head_sum(r_ref[...] * k_mean * rk_ref[...]) * v
    yr = (yn + bonus) * g_ref[...]
    mix = (_dot(ya_ref[...].astype(BF16), wout_ref[:ATTN_WIDTH, :])
           + _dot(yr.astype(BF16), wout_ref[ATTN_WIDTH:, :]))
    h_ref[...] = _layer_norm(DEEPNORM_ALPHA * x_ref[...] + mix, l1g_ref[...], l1b_ref[...])


def _mix(x2, ya, yf, yb, r, v, kd0, kd1, g, lng, lnb, rk, bd, wout, l1g, l1b, tm=256):
    t, d = x2.shape
    full = lambda a: pl.BlockSpec(a.shape, lambda i: (0,) * a.ndim)
    half = pl.BlockSpec((tm, RWKV_WIDTH), lambda i: (i, 0))
    wide = pl.BlockSpec((tm, d), lambda i: (i, 0))
    return pl.pallas_call(
        _mix_kernel,
        grid=(t // tm,),
        in_specs=[wide] + [half] * 8 + [full(lng), full(lnb), full(rk), full(bd), full(wout),
                                        full(l1g), full(l1b)],
        out_specs=wide,
        out_shape=jax.ShapeDtypeStruct((t, d), F32),
        compiler_params=_cparams(("parallel",)),
        name="mix",
    )(x2, ya, yf, yb, r, v, kd0, kd1, g, lng, lnb, rk, bd, wout, l1g, l1b)


def _top_rows(scs, k, payloads=None):
    n = scs[0].shape[0]
    iota = lax.broadcasted_iota(jnp.int32, scs[0].shape, 0).astype(F32)
    scs = list(scs)
    vals = [[] for _ in scs]
    picks = [[] for _ in scs]
    for _ in range(k):
        for a, sc in enumerate(scs):
            m = jnp.max(sc, axis=0, keepdims=True)
            pos = jnp.min(jnp.where(sc == m, iota, float(n)), axis=0, keepdims=True)
            hit = iota == pos
            vals[a].append(m)
            if payloads is None:
                picks[a].append(pos)
            else:
                picks[a].append(jnp.max(jnp.where(hit, payloads[a], -1.0), axis=0, keepdims=True))
            scs[a] = jnp.where(hit, -jnp.inf, sc)
    return [(jnp.concatenate(v, 0), jnp.concatenate(p, 0)) for v, p in zip(vals, picks)]


def _route_kernel(h_ref, wq_ref, keys_ref, idx_ref, gate_ref, gate_tok_ref, q_scr, idx_scr):
    k = PEER_TOPK
    q_scr[...] = _dot(h_ref[...].astype(BF16), wq_ref[...])

    def head_group(hg, carry):
        heads = [hg * ROUTE_HEADS_PER_STEP + i for i in range(ROUTE_HEADS_PER_STEP)]
        scores = []
        for hh in heads:
            for p in range(2):
                col = pl.multiple_of(hh * PEER_QDIM + p * PEER_HALF, PEER_HALF)
                qp = q_scr[:, pl.ds(col, PEER_HALF)]
                scores.append(_dot_nt(keys_ref[hh, p], qp, HI))
        tops = _top_rows(scores, k)
        cands, cand_ids = [], []
        for a in range(len(heads)):
            (s1, i1), (s2, i2) = tops[2 * a], tops[2 * a + 1]
            cands.append(jnp.concatenate([s1[i:i + 1] + s2[:k // (i + 1)] for i in range(k)], 0))
            cand_ids.append(jnp.concatenate(
                [i1[i:i + 1] * PEER_NKEYS + i2[:k // (i + 1)] for i in range(k)], 0))
        for hh, (cs, ids) in zip(heads, _top_rows(cands, k, cand_ids)):
            e = jnp.exp(cs - jnp.max(cs, axis=0, keepdims=True))
            row = pl.multiple_of(hh * k, k)
            gate_ref[pl.ds(row, k), :] = e / jnp.sum(e, axis=0, keepdims=True)
            idx_scr[pl.ds(row, k), :] = ids.astype(jnp.int32)
        return carry

    lax.fori_loop(0, PEER_HEADS // ROUTE_HEADS_PER_STEP, head_group, 0)
    idx_ref[...] = idx_scr[...].T
    gate_tok_ref[...] = gate_ref[...].T


def _route(h1, wq, keys, tm=256):
    t, d = h1.shape
    return pl.pallas_call(
        _route_kernel,
        grid=(t // tm,),
        in_specs=[
            pl.BlockSpec((tm, d), lambda i: (i, 0)),
            pl.BlockSpec(wq.shape, lambda i: (0, 0)),
            pl.BlockSpec(keys.shape, lambda i: (0, 0, 0, 0)),
        ],
        out_specs=[
            pl.BlockSpec((tm, N_SEL), lambda i: (i, 0)),
            pl.BlockSpec((N_SEL, tm), lambda i: (0, i)),
            pl.BlockSpec((tm, N_SEL), lambda i: (i, 0)),
        ],
        out_shape=[
            jax.ShapeDtypeStruct((t, N_SEL), jnp.int32),
            jax.ShapeDtypeStruct((N_SEL, t), F32),
            jax.ShapeDtypeStruct((t, N_SEL), F32),
        ],
        scratch_shapes=[pltpu.VMEM((tm, PEER_HEADS * PEER_QDIM), F32),
                        pltpu.VMEM((N_SEL, tm), jnp.int32)],
        compiler_params=_cparams(("parallel",)),
        name="route",
    )(h1, wq, keys)


def _peer_kernel(idx_ref, gate_ref, h_ref, uv_hbm, l2g_ref, l2b_ref, o_ref, buf, f_scr, sem, *, tb):
    d = h_ref.shape[1]

    def issue(t, slot):
        for j in range(N_SEL):
            pltpu.make_async_copy(uv_hbm.at[pl.ds(idx_ref[t, j], 1)], buf.at[slot, pl.ds(j, 1)],
                                  sem.at[slot]).start()

    def wait_all(slot):
        pltpu.make_async_copy(uv_hbm.at[pl.ds(0, N_SEL)], buf.at[slot], sem.at[slot]).wait()

    for t_ahead in range(PEER_SLOTS - 1):
        issue(t_ahead, t_ahead)
    lane = lax.broadcasted_iota(jnp.int32, (N_SEL, tb), 1)

    def body(t, carry):
        slot = t % PEER_SLOTS

        @pl.when(t + PEER_SLOTS - 1 < tb)
        def _():
            issue(t + PEER_SLOTS - 1, (t + PEER_SLOTS - 1) % PEER_SLOTS)

        wait_all(slot)
        z = jnp.sum(buf[slot, :, 0:d] * h_ref[pl.ds(t, 1), :], axis=1, keepdims=True)
        gcol = jnp.sum(jnp.where(lane == t, gate_ref[...], 0.0), axis=1, keepdims=True)
        coef = gcol * (0.5 * z * (1.0 + lax.erf(z * (2.0 ** -0.5))))
        f_scr[pl.ds(t, 1), :] = jnp.sum(coef * buf[slot, :, d:2 * d], axis=0, keepdims=True)
        return carry

    lax.fori_loop(0, tb, body, 0)
    o_ref[...] = _layer_norm(DEEPNORM_ALPHA * h_ref[...] + f_scr[...], l2g_ref[...], l2b_ref[...])


def _peer(idx, gate_t, h1, uv, l2g, l2b, tok0, ntok, tb=PEER_TB):
    d = h1.shape[1]
    b0 = tok0 // tb
    return pl.pallas_call(
        functools.partial(_peer_kernel, tb=tb),
        grid=(ntok // tb,),
        in_specs=[
            pl.BlockSpec((tb, N_SEL), lambda i: (b0 + i, 0), memory_space=pltpu.SMEM),
            pl.BlockSpec((N_SEL, tb), lambda i: (0, b0 + i)),
            pl.BlockSpec((tb, d), lambda i: (b0 + i, 0)),
            pl.BlockSpec(memory_space=pl.ANY),
            pl.BlockSpec(l2g.shape, lambda i: (0, 0)),
            pl.BlockSpec(l2b.shape, lambda i: (0, 0)),
        ],
        out_specs=pl.BlockSpec((tb, d), lambda i: (i, 0)),
        out_shape=jax.ShapeDtypeStruct((ntok, d), F32),
        scratch_shapes=[pltpu.VMEM((PEER_SLOTS, N_SEL, 2 * d), F32),
                        pltpu.VMEM((tb, d), F32),
                        pltpu.SemaphoreType.DMA((PEER_SLOTS,))],
        compiler_params=_cparams(("arbitrary",)),
        name="peer",
    )(idx, gate_t, h1, uv, l2g, l2b)


def _sc_mesh():
    return plsc.VectorSubcoreMesh(core_axis_name="c", subcore_axis_name="s")


def _worker_id():
    return lax.axis_index("s") * 2 + lax.axis_index("c")


def _sc_z_body(u_hbm, idx_hbm, h_hbm, z_hbm, idx_v, h_v, rows_v, z_v, sem, *, tpw, d, tok0):
    ngrp = N_SEL // SC_GROUP
    nchunk = d // SC_LANES
    nsteps = tpw * ngrp
    base = _worker_id() * tpw

    def gather(tok_buf, g, buf):
        return pltpu.make_async_copy(u_hbm.at[idx_v.at[tok_buf, g]], rows_v.at[buf], sem.at[buf])

    pltpu.sync_copy(idx_hbm.at[tok0 + base], idx_v.at[0])
    gather(0, 0, 0).start()

    @pl.loop(0, nsteps)
    def _(q):
        t_loc = q // ngrp
        g = q % ngrp
        buf = q % 2
        t = base + t_loc

        @pl.when(g == 0)
        def _():
            pltpu.sync_copy(h_hbm.at[tok0 + t], h_v)

            @pl.loop(0, N_SEL)
            def _(r):
                z_v[pl.ds(pl.multiple_of(r * SC_LANES, SC_LANES), SC_LANES)] = jnp.zeros((SC_LANES,), F32)

        @pl.when(q + 1 < nsteps)
        def _():
            tn = (q + 1) // ngrp
            gn = (q + 1) % ngrp

            @pl.when(gn == 0)
            def _():
                pltpu.sync_copy(idx_hbm.at[tok0 + base + tn], idx_v.at[tn % 2])

            gather(tn % 2, gn, 1 - buf).start()

        gather(t_loc % 2, g, buf).wait()

        @pl.loop(0, SC_GROUP, step=4)
        def _(rb):
            @pl.loop(0, nchunk, step=SC_CHUNK_UNROLL)
            def _(c0):
                accs = [jnp.zeros((SC_LANES,), F32) for _ in range(4)]
                for cc in range(SC_CHUNK_UNROLL):
                    sl = pl.ds(pl.multiple_of((c0 + cc) * SC_LANES, SC_LANES), SC_LANES)
                    hc = h_v[sl]
                    for i in range(4):
                        accs[i] = accs[i] + rows_v[buf, rb + i, sl] * hc
                for i in range(4):
                    row = pl.multiple_of((g * SC_GROUP + rb + i) * SC_LANES, SC_LANES)
                    plsc.addupdate(z_v.at[pl.ds(row, SC_LANES)], accs[i])

        @pl.when(g == ngrp - 1)
        def _():
            pltpu.sync_copy(z_v, z_hbm.at[t])


def _sc_z(u, idx, h, tok0, ntok):
    t, d = h.shape
    tpw = ntok // SC_WORKERS
    idx4 = idx.reshape(t, N_SEL // SC_GROUP, SC_GROUP)
    body = functools.partial(_sc_z_body, tpw=tpw, d=d, tok0=tok0)
    return pl.kernel(
        body,
        out_type=jax.ShapeDtypeStruct((ntok, N_SEL * SC_LANES), F32),
        mesh=_sc_mesh(),
        scratch_types=[
            pltpu.VMEM((2, N_SEL // SC_GROUP, SC_GROUP), jnp.int32),
            pltpu.VMEM((d,), F32),
            pltpu.VMEM((2, SC_GROUP, d), F32),
            pltpu.VMEM((N_SEL * SC_LANES,), F32),
            pltpu.SemaphoreType.DMA((2,)),
        ],
        name="sc_z",
    )(u, idx4, h)


def _sc_out_body(v_hbm, idx_hbm, coef_hbm, f_hbm, idx_v, coef_v, rows_v, out_v, sem, *, tpw, d, tok0):
    ngrp = N_SEL // SC_GROUP
    nchunk = d // SC_LANES
    nsteps = tpw * ngrp
    base = _worker_id() * tpw

    def gather(tok_buf, g, buf):
        return pltpu.make_async_copy(v_hbm.at[idx_v.at[tok_buf, g]], rows_v.at[buf], sem.at[buf])

    pltpu.sync_copy(idx_hbm.at[tok0 + base], idx_v.at[0])
    gather(0, 0, 0).start()

    @pl.loop(0, nsteps)
    def _(q):
        t_loc = q // ngrp
        g = q % ngrp
        buf = q % 2
        t = base + t_loc

        @pl.when(g == 0)
        def _():
            pltpu.sync_copy(coef_hbm.at[t], coef_v)

        @pl.when(q + 1 < nsteps)
        def _():
            tn = (q + 1) // ngrp
            gn = (q + 1) % ngrp

            @pl.when(gn == 0)
            def _():
                pltpu.sync_copy(idx_hbm.at[tok0 + base + tn], idx_v.at[tn % 2])

            gather(tn % 2, gn, 1 - buf).start()

        gather(t_loc % 2, g, buf).wait()

        @pl.loop(0, nchunk, step=SC_OUT_CHUNKS)
        def _(c0):
            sls = [pl.ds(pl.multiple_of((c0 + cc) * SC_LANES, SC_LANES), SC_LANES)
                   for cc in range(SC_OUT_CHUNKS)]
            accs = [None] * SC_OUT_CHUNKS
            for r in range(SC_GROUP):
                cf = coef_v[pl.ds(pl.multiple_of((g * SC_GROUP + r) * SC_LANES, SC_LANES), SC_LANES)]
                for cc in range(SC_OUT_CHUNKS):
                    p = rows_v[buf, r, sls[cc]] * cf
                    accs[cc] = p if accs[cc] is None else accs[cc] + p
            for cc in range(SC_OUT_CHUNKS):
                @pl.when(g == 0)
                def _():
                    out_v[sls[cc]] = accs[cc]

                @pl.when(g != 0)
                def _():
                    plsc.addupdate(out_v.at[sls[cc]], accs[cc])

        @pl.when(g == ngrp - 1)
        def _():
            pltpu.sync_copy(out_v, f_hbm.at[t])


def _sc_out(v, idx, coef16, tok0):
    t = idx.shape[0]
    ntok = coef16.shape[0]
    d = v.shape[1]
    tpw = ntok // SC_WORKERS
    idx4 = idx.reshape(t, N_SEL // SC_GROUP, SC_GROUP)
    body = functools.partial(_sc_out_body, tpw=tpw, d=d, tok0=tok0)
    return pl.kernel(
        body,
        out_type=jax.ShapeDtypeStruct((ntok, d), F32),
        mesh=_sc_mesh(),
        scratch_types=[
            pltpu.VMEM((2, N_SEL // SC_GROUP, SC_GROUP), jnp.int32),
            pltpu.VMEM((N_SEL * SC_LANES,), F32),
            pltpu.VMEM((2, SC_GROUP, d), F32),
            pltpu.VMEM((d,), F32),
            pltpu.SemaphoreType.DMA((2,)),
        ],
        name="sc_out",
    )(v, idx4, coef16)


def _coef_kernel(zp_ref, gate_ref, sel_ref, selt_ref, after_ref, o_ref):
    z = sum(_dot(part, sel_ref[...]) for part in _split3(zp_ref[...]))
    coef = gate_ref[...] * (0.5 * z * (1.0 + lax.erf(z * (2.0 ** -0.5))))
    o_ref[...] = sum(_dot(part, selt_ref[...]) for part in _split3(coef))


def _coef(zp, gate_tok, tok0, after, tm=256):
    ntok, wide = zp.shape
    b0 = tok0 // tm
    grp = np.arange(wide) // SC_LANES
    sel = jnp.asarray(grp[:, None] == np.arange(N_SEL)[None, :], BF16)
    return pl.pallas_call(
        _coef_kernel,
        grid=(ntok // tm,),
        in_specs=[
            pl.BlockSpec((tm, wide), lambda i: (i, 0)),
            pl.BlockSpec((tm, N_SEL), lambda i: (b0 + i, 0)),
            pl.BlockSpec(sel.shape, lambda i: (0, 0)),
            pl.BlockSpec(sel.shape[::-1], lambda i: (0, 0)),
            pl.BlockSpec((SUBLANES, LANES), lambda i: (0, 0)),
        ],
        out_specs=pl.BlockSpec((tm, wide), lambda i: (i, 0)),
        out_shape=jax.ShapeDtypeStruct((ntok, wide), F32),
        compiler_params=_cparams(("parallel",)),
        name="coef",
    )(zp, gate_tok, sel, sel.T, after)


def _ln_out_kernel(h_ref, f_ref, g_ref, b_ref, o_ref):
    o_ref[...] = _layer_norm(DEEPNORM_ALPHA * h_ref[...] + f_ref[...], g_ref[...], b_ref[...])


def _ln_out(h1, f, l2g, l2b, tok0, tm=256):
    ntok, d = f.shape
    b0 = tok0 // tm
    return pl.pallas_call(
        _ln_out_kernel,
        grid=(ntok // tm,),
        in_specs=[
            pl.BlockSpec((tm, d), lambda i: (b0 + i, 0)),
            pl.BlockSpec((tm, d), lambda i: (i, 0)),
            pl.BlockSpec(l2g.shape, lambda i: (0, 0)),
            pl.BlockSpec(l2b.shape, lambda i: (0, 0)),
        ],
        out_specs=pl.BlockSpec((tm, d), lambda i: (i, 0)),
        out_shape=jax.ShapeDtypeStruct((ntok, d), F32),
        compiler_params=_cparams(("parallel",)),
        name="ln_out",
    )(h1, f, l2g, l2b)


def _rope_tables(seq):
    pos = jnp.arange(seq, dtype=F32)
    inv_freq = ROPE_THETA ** (-jnp.arange(0, HEAD_DIM, 2, dtype=F32) / HEAD_DIM)
    ang = pos[:, None] * inv_freq[None, :]
    ang = jnp.concatenate([ang, ang], -1)
    sign = jnp.concatenate([-jnp.ones((HEAD_DIM // 2,), F32), jnp.ones((HEAD_DIM // 2,), F32)])
    reps = LANES // HEAD_DIM
    return jnp.tile(jnp.cos(ang), (1, reps)), jnp.tile(jnp.sin(ang) * sign, (1, reps))


def _layer(h2d, batch, seq, w_in, mu_prev, mu_next, w0, w2, a0, a2, g2, k_k, k_a, r_k, lnx_g, lnx_b,
           sink, w_out, ln1_g, ln1_b, peer_wq, peer_keys, peer_u, peer_v, ln2_g, ln2_b):
    w = RWKV_WIDTH
    row = lambda a: a.reshape(1, -1).astype(F32)
    w_in_p = jnp.pad(w_in, ((0, 0), (0, RWKV_COLS_PAD - RWKV_COLS))).astype(BF16)
    mup = jnp.pad(mu_prev, (0, RWKV_COLS_PAD - RWKV_COLS)).reshape(1, -1)
    mun = jnp.pad(mu_next, (0, RWKV_COLS_PAD - RWKV_COLS)).reshape(1, -1)
    wmix = jnp.zeros((LANES, 4 * w), F32)
    for d in range(2):
        wmix = wmix.at[d * DECAY_RANK:(d + 1) * DECAY_RANK, d * w:(d + 1) * w].set(w2[d])
        r0 = 2 * DECAY_RANK + d * ICLR_RANK
        wmix = wmix.at[r0:r0 + ICLR_RANK, (2 + d) * w:(3 + d) * w].set(a2[d])
    g2p = jnp.pad(g2, ((0, LANES - GATE_RANK), (0, 0)))
    head_of = np.arange(w) // HEAD_DIM
    bd = jnp.asarray(head_of[:, None] == head_of[None, :], F32)
    cos_t, sin_t = _rope_tables(seq)

    wq_b, keys_f, wout_b = peer_wq.astype(BF16), peer_keys.astype(F32), w_out.astype(BF16)

    def to_routing(x2d, nb):
        qkv, pr = _proj(x2d, w_in_p, cos_t, sin_t, seq)
        y_attn = _attention(qkv.reshape(nb, seq, ATTN_COLS), sink.astype(F32))
        r, v, kk, g, lw0, lw1, kd0, kd1, b0, b1 = _prep(
            pr, mup, mun, wmix, w0.astype(F32), a0.astype(F32), g2p, row(k_k), row(k_a), bd, seq)
        yf, yb = _wkv(r, v, kk, lw0, lw1, kd0, kd1, b0, b1, nb, seq)
        h1 = _mix(x2d, y_attn.reshape(nb * seq, ATTN_WIDTH), yf, yb, r, v, kd0, kd1, g,
                  row(lnx_g), row(lnx_b), row(r_k), bd, wout_b, row(ln1_g), row(ln1_b))
        return (h1,) + tuple(_route(h1, wq_b, keys_f))

    uv = jnp.concatenate([peer_u, peer_v], axis=1)
    l2g, l2b = row(ln2_g), row(ln2_b)

    nb_1 = batch * PEER_FIRST_GROUP_SEQ_8THS // 8
    t_1 = nb_1 * seq
    t_2 = (batch - nb_1) * seq
    n_sc = (batch * seq * PEER_SC_TOKEN_SHARE_128THS // 128) // PEER_TB * PEER_TB
    n_a = (t_2 * PEER_TC_FIRST_CALL_32NDS // 32) // PEER_TB * PEER_TB
    n_b = t_2 - n_a
    assert 0 < nb_1 < batch and 0 < n_sc < t_1 and n_sc % (SC_WORKERS * SUBLANES) == 0
    assert n_a > 0 and n_b > 0 and n_b % PEER_TB == 0 and (t_1 - n_sc) % PEER_TB == 0
    h1_1, idx_1, gate_t_1, gate_tok_1 = to_routing(h2d[:t_1], nb_1)
    zp = _sc_z(peer_u.astype(F32), idx_1, h1_1, 0, n_sc)
    h1_2, idx_2, gate_t_2, _ = to_routing(h2d[t_1:], batch - nb_1)
    out_tail = _peer(idx_1, gate_t_1, h1_1, uv, l2g, l2b, n_sc, t_1 - n_sc)
    out_a = _peer(idx_2, gate_t_2, h1_2, uv, l2g, l2b, 0, n_a)
    f_sc = _sc_out(peer_v.astype(F32), idx_1, _coef(zp, gate_tok_1, 0, out_a), 0)
    out_b = _peer(idx_2, gate_t_2, h1_2, uv, l2g, l2b, n_a, n_b)
    out_sc = _ln_out(h1_1, f_sc, l2g, l2b, 0)
    return jnp.concatenate([out_sc, out_tail, out_a, out_b], axis=0)


def kernel(x, w_in, mu_prev, mu_next, w0, w2, a0, a2, g2, k_k, k_a, r_k, lnx_g, lnx_b, sink, w_out,
           ln1_g, ln1_b, peer_wq, peer_keys, peer_u, peer_v, ln2_g, ln2_b):
    batch, seq, d = x.shape
    h = x.reshape(batch * seq, d)
    for l in range(DEPTH):
        h = _layer(h, batch, seq, w_in[l], mu_prev[l], mu_next[l], w0[l], w2[l], a0[l], a2[l], g2[l],
                   k_k[l], k_a[l], r_k[l], lnx_g[l], lnx_b[l], sink[l], w_out[l], ln1_g[l], ln1_b[l],
                   peer_wq[l], peer_keys[l], peer_u[l], peer_v[l], ln2_g[l], ln2_b[l])
    return h.reshape(batch, seq, d)
```

```python
import functools

import numpy as np
import jax
import jax.numpy as jnp
from jax import lax
from jax.experimental import pallas as pl
from jax.experimental.pallas import tpu as pltpu
from jax.experimental.pallas import tpu_sc as plsc

F32 = jnp.float32
BF16 = jnp.bfloat16
HI = lax.Precision.HIGHEST

HEAD_DIM = 64
N_Q_HEADS = 8
N_KV_HEADS = 2
Q_PER_KV = N_Q_HEADS // N_KV_HEADS
ATTN_WIDTH = N_Q_HEADS * HEAD_DIM
ATTN_KV_WIDTH = N_KV_HEADS * HEAD_DIM
ATTN_COLS = ATTN_WIDTH + 2 * ATTN_KV_WIDTH
WINDOW = 128
ATTN_BLOCK = 128
ROPE_THETA = 10000.0
N_RWKV_HEADS = 8
RWKV_WIDTH = N_RWKV_HEADS * HEAD_DIM
DECAY_RANK = 32
ICLR_RANK = 32
GATE_RANK = 96
RWKV_COLS = 3 * RWKV_WIDTH + 2 * DECAY_RANK + 2 * ICLR_RANK + GATE_RANK
RWKV_COLS_PAD = 1792
RWKV_GN_EPS = 64e-5
PEER_HEADS = 8
PEER_NKEYS = 128
PEER_QDIM = 256
PEER_HALF = PEER_QDIM // 2
PEER_TOPK = 16
N_SEL = PEER_HEADS * PEER_TOPK
LN_EPS = 1e-5
DEPTH = 1
DEEPNORM_ALPHA = (2.0 * DEPTH) ** 0.25

LANES = 128
SUBLANES = 8
PREP_OUT_DTYPES = (BF16, BF16, BF16, F32, F32, F32, BF16, BF16, BF16, BF16)
WKV_CHUNK = 64
VMEM_LIMIT = 48 * 1024 * 1024
PEER_SLOTS = 3
PEER_TB = 256
ROUTE_HEADS_PER_STEP = 8
SC_LANES = 16
SC_WORKERS = 32
SC_GROUP = 32
SC_CHUNK_UNROLL = 8
SC_OUT_CHUNKS = 4
PEER_FIRST_GROUP_SEQ_8THS = 5
PEER_SC_TOKEN_SHARE_128THS = 74
PEER_TC_FIRST_CALL_32NDS = 11


def _cparams(sem):
    return pltpu.CompilerParams(dimension_semantics=sem, vmem_limit_bytes=VMEM_LIMIT)


def _dot(a, b, precision=None):
    return jnp.dot(a, b, preferred_element_type=F32, precision=precision)


def _dot_nt(a, b, precision=None):
    return lax.dot_general(a, b, (((1,), (1,)), ((), ())), preferred_element_type=F32,
                           precision=precision)


def _layer_norm(z, g, b):
    mu = jnp.mean(z, -1, keepdims=True)
    zc = z - mu
    var = jnp.mean(zc * zc, -1, keepdims=True)
    return zc * lax.rsqrt(var + LN_EPS) * g + b


def _proj_kernel(x_ref, w_ref, cos_ref, sin_ref, qkv_ref, pr_ref):
    xb = x_ref[...].astype(BF16)
    cos = cos_ref[...]
    sin = sin_ref[...]
    lane = lax.broadcasted_iota(jnp.int32, cos.shape, 1)
    first_half = (lane & (HEAD_DIM // 2)) == 0

    def rope(t):
        rot = jnp.where(first_half, pltpu.roll(t, LANES - HEAD_DIM // 2, 1),
                        pltpu.roll(t, HEAD_DIM // 2, 1))
        return t * cos + rot * sin

    for c in range(0, ATTN_COLS, 2 * LANES):
        acc = _dot(xb, w_ref[:, c:c + 2 * LANES])
        for half in range(2):
            col = c + half * LANES
            t = acc[:, half * LANES:(half + 1) * LANES]
            if col < ATTN_WIDTH + ATTN_KV_WIDTH:
                t = rope(t)
            qkv_ref[:, col:col + LANES] = t
    for c in range(0, RWKV_COLS_PAD, 2 * LANES):
        pr_ref[:, c:c + 2 * LANES] = _dot(xb, w_ref[:, ATTN_COLS + c:ATTN_COLS + c + 2 * LANES])


def _proj(x2, w_in_p, cos_t, sin_t, seq, tm=512):
    t, d = x2.shape
    n_pos = seq // tm
    return pl.pallas_call(
        _proj_kernel,
        grid=(t // tm,),
        in_specs=[
            pl.BlockSpec((tm, d), lambda i: (i, 0)),
            pl.BlockSpec(w_in_p.shape, lambda i: (0, 0)),
            pl.BlockSpec((tm, LANES), lambda i: (i % n_pos, 0)),
            pl.BlockSpec((tm, LANES), lambda i: (i % n_pos, 0)),
        ],
        out_specs=[
            pl.BlockSpec((tm, ATTN_COLS), lambda i: (i, 0)),
            pl.BlockSpec((tm, RWKV_COLS_PAD), lambda i: (i, 0)),
        ],
        out_shape=[
            jax.ShapeDtypeStruct((t, ATTN_COLS), F32),
            jax.ShapeDtypeStruct((t, RWKV_COLS_PAD), F32),
        ],
        compiler_params=_cparams(("parallel",)),
        name="proj",
    )(x2, w_in_p, cos_t, sin_t)


def _attn_kernel(sink_ref, q_ref, kp_ref, kc_ref, kn_ref, vp_ref, vc_ref, vn_ref, o_ref, *, nb):
    n = pl.program_id(1)
    blk = ATTN_BLOCK
    rows = Q_PER_KV * blk
    q = q_ref[0] * (HEAD_DIM ** -0.5)
    kwin = jnp.concatenate([kp_ref[0], kc_ref[0], kn_ref[0]], axis=0)
    vwin = jnp.concatenate([vp_ref[0], vc_ref[0], vn_ref[0]], axis=0)
    qi = lax.broadcasted_iota(jnp.int32, (rows, 3 * blk), 0) & (blk - 1)
    kj = lax.broadcasted_iota(jnp.int32, (rows, 3 * blk), 1)
    dist = kj - qi
    valid = ((dist >= blk - WINDOW) & (dist <= blk + WINDOW)
             & ((kj >= blk) | (n > 0)) & ((kj < 2 * blk) | (n < nb - 1)))
    rowg = lax.broadcasted_iota(jnp.int32, (rows, 1), 0) // blk
    outs = []
    for h in range(N_KV_HEADS):
        qs = jnp.concatenate(
            [q[:, (Q_PER_KV * h + g) * HEAD_DIM:(Q_PER_KV * h + g + 1) * HEAD_DIM]
             for g in range(Q_PER_KV)], axis=0)
        kh = kwin[:, h * HEAD_DIM:(h + 1) * HEAD_DIM]
        vh = vwin[:, h * HEAD_DIM:(h + 1) * HEAD_DIM]
        logits = _dot_nt(qs.astype(BF16), kh.astype(BF16))
        logits = jnp.where(valid, logits, -1e30)
        sk = jnp.zeros((rows, 1), F32)
        for g in range(Q_PER_KV):
            sk = jnp.where(rowg == g, sink_ref[Q_PER_KV * h + g], sk)
        m = jnp.maximum(jnp.max(logits, -1, keepdims=True), sk)
        e = jnp.exp(logits - m)
        den = jnp.sum(e, -1, keepdims=True) + jnp.exp(sk - m)
        p = e / den
        o = _dot(p.astype(BF16), vh.astype(BF16))
        for g in range(Q_PER_KV):
            outs.append(o[g * blk:(g + 1) * blk])
    o_ref[0] = jnp.concatenate(outs, axis=1)


def _attention(qkv3, sink):
    b, s, _ = qkv3.shape
    blk = ATTN_BLOCK
    nb = s // blk
    kcol = ATTN_WIDTH // LANES
    vcol = kcol + 1

    def spec(col, shift):
        def imap(bi, n):
            return (bi, jnp.clip(n + shift, 0, nb - 1), col)
        return pl.BlockSpec((1, blk, LANES), imap)

    return pl.pallas_call(
        functools.partial(_attn_kernel, nb=nb),
        grid=(b, nb),
        in_specs=[
            pl.BlockSpec(memory_space=pltpu.SMEM),
            pl.BlockSpec((1, blk, ATTN_WIDTH), lambda bi, n: (bi, n, 0)),
            spec(kcol, -1), spec(kcol, 0), spec(kcol, 1),
            spec(vcol, -1), spec(vcol, 0), spec(vcol, 1),
        ],
        out_specs=pl.BlockSpec((1, blk, ATTN_WIDTH), lambda bi, n: (bi, n, 0)),
        out_shape=jax.ShapeDtypeStruct((b, s, ATTN_WIDTH), F32),
        compiler_params=_cparams(("parallel", "parallel")),
        name="attn",
    )(sink, qkv3, qkv3, qkv3, qkv3, qkv3, qkv3, qkv3)


def _softplus(x):
    return jnp.maximum(x, 0.0) + jnp.log(1.0 + jnp.exp(-jnp.abs(x)))


def _sigmoid(x):
    return 1.0 / (1.0 + jnp.exp(-x))


def _prep_kernel(p_ref, hp_ref, hn_ref, mup_ref, mun_ref, wmix_ref, w0_ref, a0_ref, g2_ref,
                 kk_ref, ka_ref, bd_ref,
                 r_o, v_o, kk_o, g_o, lw0_o, lw1_o, kd0_o, kd1_o, b0_o, b1_o, *, tm, seq):
    i = pl.program_id(0)
    row = lax.broadcasted_iota(jnp.int32, (tm, 1), 0)
    seq_start = (i * tm) % seq == 0
    seq_end = ((i + 1) * tm) % seq == 0

    def shifted(c0, c1):
        p = p_ref[:, c0:c1]
        prev_row = jnp.where(seq_start, 0.0, hp_ref[SUBLANES - 1:SUBLANES, c0:c1])
        next_row = jnp.where(seq_end, 0.0, hn_ref[0:1, c0:c1])
        p_prev = jnp.where(row == 0, prev_row, pltpu.roll(p, 1, 0))
        p_next = jnp.where(row == tm - 1, next_row, pltpu.roll(p, tm - 1, 0))
        return p + mup_ref[:, c0:c1] * (p_prev - p) + mun_ref[:, c0:c1] * (p_next - p)

    w = RWKV_WIDTH
    r = shifted(0, w)
    k = shifted(w, 2 * w)
    v = shifted(2 * w, 3 * w)
    codes = shifted(3 * w, 3 * w + LANES)
    gd = shifted(3 * w + LANES, 3 * w + 2 * LANES)
    r_o[...] = r.astype(r_o.dtype)
    v_o[...] = v.astype(v_o.dtype)

    lane = lax.broadcasted_iota(jnp.int32, codes.shape, 1)
    codes = jnp.where(lane < 2 * DECAY_RANK, jnp.tanh(codes), codes)
    mm = _dot(codes, wmix_ref[...], HI)
    g_o[...] = _dot(_sigmoid(gd), g2_ref[...], HI)

    kkv = k * kk_ref[...]
    bd = bd_ref[...].astype(BF16)
    ss = sum(_dot(part, bd) for part in _split3(kkv * kkv))
    kkn = kkv * lax.rsqrt(jnp.maximum(ss, 1e-24))
    kk_o[...] = kkn.astype(kk_o.dtype)

    ka = ka_ref[...]
    for d, (lw_o, kd_o, b_o) in enumerate(((lw0_o, kd0_o, b0_o), (lw1_o, kd1_o, b1_o))):
        w_log = -_softplus(-(w0_ref[d:d + 1, :] + mm[:, d * w:(d + 1) * w])) - 0.5
        lw_o[...] = -jnp.exp(w_log)
        a = _sigmoid(a0_ref[d:d + 1, :] + mm[:, (2 + d) * w:(3 + d) * w])
        kd_o[...] = (k * (1.0 + (a - 1.0) * ka)).astype(kd_o.dtype)
        b_o[...] = (kkn * a).astype(b_o.dtype)


def _prep(pr, mup, mun, wmix, w0, a0, g2p, k_k, k_a, bd, seq, tm=256):
    t = pr.shape[0]
    nblk8 = t // SUBLANES
    per = tm // SUBLANES
    full = lambda a: pl.BlockSpec(a.shape, lambda i: (0,) * a.ndim)
    out = pl.BlockSpec((tm, RWKV_WIDTH), lambda i: (i, 0))
    return pl.pallas_call(
        functools.partial(_prep_kernel, tm=tm, seq=seq),
        grid=(t // tm,),
        in_specs=[
            pl.BlockSpec((tm, RWKV_COLS_PAD), lambda i: (i, 0)),
            pl.BlockSpec((SUBLANES, RWKV_COLS_PAD), lambda i: (jnp.maximum(i * per - 1, 0), 0)),
            pl.BlockSpec((SUBLANES, RWKV_COLS_PAD),
                         lambda i: (jnp.minimum((i + 1) * per, nblk8 - 1), 0)),
            full(mup), full(mun), full(wmix), full(w0), full(a0), full(g2p),
            full(k_k), full(k_a), full(bd),
        ],
        out_specs=[out] * 10,
        out_shape=[jax.ShapeDtypeStruct((t, RWKV_WIDTH), dt) for dt in PREP_OUT_DTYPES],
        compiler_params=_cparams(("parallel",)),
        name="prep",
    )(pr, pr, pr, mup, mun, wmix, w0, a0, g2p, k_k, k_a, bd)


def _split3(x):
    hi = x.astype(BF16)
    r1 = x - hi.astype(F32)
    mid = r1.astype(BF16)
    lo = (r1 - mid.astype(F32)).astype(BF16)
    return hi, mid, lo


def _wkv_kernel(rf, vf, kkf, lwf, kdf, bf, rb, vb, kkb, lwb, kdb, bb, yf_ref, yb_ref, state):
    c = pl.program_id(1)
    n = WKV_CHUNK
    hd = HEAD_DIM
    nh = N_RWKV_HEADS
    def bmm(a, b, ca, cb):
        return lax.dot_general(a.astype(BF16), b.astype(BF16), (((ca,), (cb,)), ((0,), (0,))),
                               preferred_element_type=F32)

    @pl.when(c == 0)
    def _():
        state[...] = jnp.zeros_like(state)

    ti = lax.broadcasted_iota(jnp.int32, (n, n), 0)
    si = lax.broadcasted_iota(jnp.int32, (n, n), 1)
    dirs = ((rf, vf, kkf, lwf, kdf, bf, si <= ti, si < ti, n - 1),
            (rb, vb, kkb, lwb, kdb, bb, si >= ti, si > ti, 0))
    heads = lambda x: jnp.stack([x[:, j * hd:(j + 1) * hd] for j in range(nh)], 0)
    parts = []
    for r_ref, v_ref, kk_ref, lw_ref, kd_ref, b_ref, incl, strict, last in dirs:
        lw = lw_ref[0]
        tri = incl.astype(BF16)
        cum = sum(_dot(tri, part) for part in _split3(lw))
        e_neg = jnp.exp(-cum)
        e_last = jnp.exp(cum[last:last + 1, :] - cum)
        g_scale = jnp.exp(cum[last:last + 1, :])
        parts.append(dict(
            at=heads((-kk_ref[0] * jnp.exp(cum - lw)).astype(BF16)),
            rt=heads((r_ref[0] * jnp.exp(cum)).astype(BF16)),
            bt=heads((b_ref[0] * e_neg).astype(BF16)),
            kt=heads((kd_ref[0] * e_neg).astype(BF16)),
            bl=heads((b_ref[0] * e_last).astype(BF16)),
            kl=heads((kd_ref[0] * e_last).astype(BF16)),
            v=heads(v_ref[0].astype(BF16)),
            gs=heads(g_scale),
            incl=jnp.broadcast_to(incl[None], (nh, n, n)),
            strict=jnp.broadcast_to(strict[None], (nh, n, n))))
    cat = lambda key: jnp.concatenate([parts[0][key], parts[1][key]], 0)
    at, rt, bt, kt, bl, kl, v, gs = (cat(k) for k in ("at", "rt", "bt", "kt", "bl", "kl", "v", "gs"))
    incl, strict = cat("incl"), cat("strict")
    g0 = state[...]
    g0b = g0.astype(BF16)
    m1 = bmm(jnp.concatenate([at, rt], 1), jnp.concatenate([bt, kt], 1), 2, 2)
    a_ab = jnp.where(strict, m1[:, :n, :n], 0.0)
    a_ak = jnp.where(strict, m1[:, :n, n:], 0.0)
    a_rb = jnp.where(incl, m1[:, n:, :n], 0.0)
    a_rk = jnp.where(incl, m1[:, n:, n:], 0.0)
    tinv = jnp.where((ti == si)[None], 1.0, a_ab)
    pw = a_ab
    for _ in range(int(np.log2(n)) - 1):
        pw = bmm(pw, pw, 2, 1)
        tinv = tinv + bmm(tinv, pw, 2, 1)
    rhs = bmm(a_ak, v, 2, 1) + bmm(at, g0b, 2, 2)
    u = bmm(tinv, rhs, 2, 1)
    uv = jnp.concatenate([u.astype(BF16), v], 1)
    y = bmm(jnp.concatenate([a_rb, a_rk], 2), uv, 2, 1) + bmm(rt, g0b, 2, 2)
    state[...] = g0 * gs + bmm(uv, jnp.concatenate([bl, kl], 1), 1, 1)
    for d, y_ref in enumerate((yf_ref, yb_ref)):
        y_ref[0] = jnp.concatenate([y[d * nh + j] for j in range(nh)], axis=1)


def _wkv(r, v, kk, lw0, lw1, kd0, kd1, b0, b1, batch, seq):
    n = WKV_CHUNK
    nc = seq // n
    shp = (batch, seq, RWKV_WIDTH)
    arrs = [a.reshape(shp) for a in (r, v, kk, lw0, kd0, b0, r, v, kk, lw1, kd1, b1)]
    fwd = pl.BlockSpec((1, n, RWKV_WIDTH), lambda b, c: (b, c, 0))
    bwd = pl.BlockSpec((1, n, RWKV_WIDTH), lambda b, c: (b, nc - 1 - c, 0))
    yf, yb = pl.pallas_call(
        _wkv_kernel,
        grid=(batch, nc),
        in_specs=[fwd] * 6 + [bwd] * 6,
        out_specs=[fwd, bwd],
        out_shape=[jax.ShapeDtypeStruct(shp, F32)] * 2,
        scratch_shapes=[pltpu.VMEM((2 * N_RWKV_HEADS, HEAD_DIM, HEAD_DIM), F32)],
        compiler_params=_cparams(("parallel", "arbitrary")),
        name="wkv",
    )(*arrs)
    return yf.reshape(batch * seq, RWKV_WIDTH), yb.reshape(batch * seq, RWKV_WIDTH)


def _mix_kernel(x_ref, ya_ref, yf_ref, yb_ref, r_ref, v_ref, kd0_ref, kd1_ref, g_ref,
                lng_ref, lnb_ref, rk_ref, bd_ref, wout_ref, l1g_ref, l1b_ref, h_ref):
    bd = bd_ref[...].astype(BF16)
    head_sum = lambda t: sum(_dot(part, bd) for part in _split3(t))
    inv = 1.0 / HEAD_DIM
    y = yf_ref[...] + yb_ref[...]
    mu = head_sum(y) * inv
    yc = y - mu
    var = head_sum(yc * yc) * inv
    yn = yc * lax.rsqrt(var + RWKV_GN_EPS) * lng_ref[...] + lnb_ref[...]
    k_mean = 0.5 * (kd0_ref[...] + kd1_ref[...])
    v = v_ref[...]
    bonus = head_sum(r_ref[...] * k_mean * rk_ref[...]) * v
    yr = (yn + bonus) * g_ref[...]
    mix = (_dot(ya_ref[...].astype(BF16), wout_ref[:ATTN_WIDTH, :])
           + _dot(yr.astype(BF16), wout_ref[ATTN_WIDTH:, :]))
    h_ref[...] = _layer_norm(DEEPNORM_ALPHA * x_ref[...] + mix, l1g_ref[...], l1b_ref[...])


def _mix(x2, ya, yf, yb, r, v, kd0, kd1, g, lng, lnb, rk, bd, wout, l1g, l1b, tm=256):
    t, d = x2.shape
    full = lambda a: pl.BlockSpec(a.shape, lambda i: (0,) * a.ndim)
    half = pl.BlockSpec((tm, RWKV_WIDTH), lambda i: (i, 0))
    wide = pl.BlockSpec((tm, d), lambda i: (i, 0))
    return pl.pallas_call(
        _mix_kernel,
        grid=(t // tm,),
        in_specs=[wide] + [half] * 8 + [full(lng), full(lnb), full(rk), full(bd), full(wout),
                                        full(l1g), full(l1b)],
        out_specs=wide,
        out_shape=jax.ShapeDtypeStruct((t, d), F32),
        compiler_params=_cparams(("parallel",)),
        name="mix",
    )(x2, ya, yf, yb, r, v, kd0, kd1, g, lng, lnb, rk, bd, wout, l1g, l1b)


def _top_rows(scs, k, payloads=None):
    n = scs[0].shape[0]
    iota = lax.broadcasted_iota(jnp.int32, scs[0].shape, 0).astype(F32)
    scs = list(scs)
    vals = [[] for _ in scs]
    picks = [[] for _ in scs]
    for _ in range(k):
        for a, sc in enumerate(scs):
            m = jnp.max(sc, axis=0, keepdims=True)
            pos = jnp.min(jnp.where(sc == m, iota, float(n)), axis=0, keepdims=True)
            hit = iota == pos
            vals[a].append(m)
            if payloads is None:
                picks[a].append(pos)
            else:
                picks[a].append(jnp.max(jnp.where(hit, payloads[a], -1.0), axis=0, keepdims=True))
            scs[a] = jnp.where(hit, -jnp.inf, sc)
    return [(jnp.concatenate(v, 0), jnp.concatenate(p, 0)) for v, p in zip(vals, picks)]


def _route_kernel(h_ref, wq_ref, keys_ref, idx_ref, gate_ref, gate_tok_ref, q_scr, idx_scr):
    k = PEER_TOPK
    q_scr[...] = _dot(h_ref[...].astype(BF16), wq_ref[...])

    def head_group(hg, carry):
        heads = [hg * ROUTE_HEADS_PER_STEP + i for i in range(ROUTE_HEADS_PER_STEP)]
        scores = []
        for hh in heads:
            for p in range(2):
                col = pl.multiple_of(hh * PEER_QDIM + p * PEER_HALF, PEER_HALF)
                qp = q_scr[:, pl.ds(col, PEER_HALF)]
                scores.append(_dot_nt(keys_ref[hh, p], qp, HI))
        tops = _top_rows(scores, k)
        cands, cand_ids = [], []
        for a in range(len(heads)):
            (s1, i1), (s2, i2) = tops[2 * a], tops[2 * a + 1]
            cands.append(jnp.concatenate([s1[i:i + 1] + s2[:k // (i + 1)] for i in range(k)], 0))
            cand_ids.append(jnp.concatenate(
                [i1[i:i + 1] * PEER_NKEYS + i2[:k // (i + 1)] for i in range(k)], 0))
        for hh, (cs, ids) in zip(heads, _top_rows(cands, k, cand_ids)):
            e = jnp.exp(cs - jnp.max(cs, axis=0, keepdims=True))
            row = pl.multiple_of(hh * k, k)
            gate_ref[pl.ds(row, k), :] = e / jnp.sum(e, axis=0, keepdims=True)
            idx_scr[pl.ds(row, k), :] = ids.astype(jnp.int32)
        return carry

    lax.fori_loop(0, PEER_HEADS // ROUTE_HEADS_PER_STEP, head_group, 0)
    idx_ref[...] = idx_scr[...].T
    gate_tok_ref[...] = gate_ref[...].T


def _route(h1, wq, keys, tm=256):
    t, d = h1.shape
    return pl.pallas_call(
        _route_kernel,
        grid=(t // tm,),
        in_specs=[
            pl.BlockSpec((tm, d), lambda i: (i, 0)),
            pl.BlockSpec(wq.shape, lambda i: (0, 0)),
            pl.BlockSpec(keys.shape, lambda i: (0, 0, 0, 0)),
        ],
        out_specs=[
            pl.BlockSpec((tm, N_SEL), lambda i: (i, 0)),
            pl.BlockSpec((N_SEL, tm), lambda i: (0, i)),
            pl.BlockSpec((tm, N_SEL), lambda i: (i, 0)),
        ],
        out_shape=[
            jax.ShapeDtypeStruct((t, N_SEL), jnp.int32),
            jax.ShapeDtypeStruct((N_SEL, t), F32),
            jax.ShapeDtypeStruct((t, N_SEL), F32),
        ],
        scratch_shapes=[pltpu.VMEM((tm, PEER_HEADS * PEER_QDIM), F32),
                        pltpu.VMEM((N_SEL, tm), jnp.int32)],
        compiler_params=_cparams(("parallel",)),
        name="route",
    )(h1, wq, keys)


def _peer_kernel(idx_ref, gate_ref, h_ref, uv_hbm, l2g_ref, l2b_ref, o_ref, buf, f_scr, sem, *, tb):
    d = h_ref.shape[1]

    def issue(t, slot):
        for j in range(N_SEL):
            pltpu.make_async_copy(uv_hbm.at[pl.ds(idx_ref[t, j], 1)], buf.at[slot, pl.ds(j, 1)],
                                  sem.at[slot]).start()

    def wait_all(slot):
        pltpu.make_async_copy(uv_hbm.at[pl.ds(0, N_SEL)], buf.at[slot], sem.at[slot]).wait()

    for t_ahead in range(PEER_SLOTS - 1):
        issue(t_ahead, t_ahead)
    lane = lax.broadcasted_iota(jnp.int32, (N_SEL, tb), 1)

    def body(t, carry):
        slot = t % PEER_SLOTS

        @pl.when(t + PEER_SLOTS - 1 < tb)
        def _():
            issue(t + PEER_SLOTS - 1, (t + PEER_SLOTS - 1) % PEER_SLOTS)

        wait_all(slot)
        z = jnp.sum(buf[slot, :, 0:d] * h_ref[pl.ds(t, 1), :], axis=1, keepdims=True)
        gcol = jnp.sum(jnp.where(lane == t, gate_ref[...], 0.0), axis=1, keepdims=True)
        coef = gcol * (0.5 * z * (1.0 + lax.erf(z * (2.0 ** -0.5))))
        f_scr[pl.ds(t, 1), :] = jnp.sum(coef * buf[slot, :, d:2 * d], axis=0, keepdims=True)
        return carry

    lax.fori_loop(0, tb, body, 0)
    o_ref[...] = _layer_norm(DEEPNORM_ALPHA * h_ref[...] + f_scr[...], l2g_ref[...], l2b_ref[...])


def _peer(idx, gate_t, h1, uv, l2g, l2b, tok0, ntok, tb=PEER_TB):
    d = h1.shape[1]
    b0 = tok0 // tb
    return pl.pallas_call(
        functools.partial(_peer_kernel, tb=tb),
        grid=(ntok // tb,),
        in_specs=[
            pl.BlockSpec((tb, N_SEL), lambda i: (b0 + i, 0), memory_space=pltpu.SMEM),
            pl.BlockSpec((N_SEL, tb), lambda i: (0, b0 + i)),
            pl.BlockSpec((tb, d), lambda i: (b0 + i, 0)),
            pl.BlockSpec(memory_space=pl.ANY),
            pl.BlockSpec(l2g.shape, lambda i: (0, 0)),
            pl.BlockSpec(l2b.shape, lambda i: (0, 0)),
        ],
        out_specs=pl.BlockSpec((tb, d), lambda i: (i, 0)),
        out_shape=jax.ShapeDtypeStruct((ntok, d), F32),
        scratch_shapes=[pltpu.VMEM((PEER_SLOTS, N_SEL, 2 * d), F32),
                        pltpu.VMEM((tb, d), F32),
                        pltpu.SemaphoreType.DMA((PEER_SLOTS,))],
        compiler_params=_cparams(("arbitrary",)),
        name="peer",
    )(idx, gate_t, h1, uv, l2g, l2b)


def _sc_mesh():
    return plsc.VectorSubcoreMesh(core_axis_name="c", subcore_axis_name="s")


def _worker_id():
    return lax.axis_index("s") * 2 + lax.axis_index("c")


def _sc_z_body(u_hbm, idx_hbm, h_hbm, z_hbm, idx_v, h_v, rows_v, z_v, sem, *, tpw, d, tok0):
    ngrp = N_SEL // SC_GROUP
    nchunk = d // SC_LANES
    nsteps = tpw * ngrp
    base = _worker_id() * tpw

    def gather(tok_buf, g, buf):
        return pltpu.make_async_copy(u_hbm.at[idx_v.at[tok_buf, g]], rows_v.at[buf], sem.at[buf])

    pltpu.sync_copy(idx_hbm.at[tok0 + base], idx_v.at[0])
    gather(0, 0, 0).start()

    @pl.loop(0, nsteps)
    def _(q):
        t_loc = q // ngrp
        g = q % ngrp
        buf = q % 2
        t = base + t_loc

        @pl.when(g == 0)
        def _():
            pltpu.sync_copy(h_hbm.at[tok0 + t], h_v)

            @pl.loop(0, N_SEL)
            def _(r):
                z_v[pl.ds(pl.multiple_of(r * SC_LANES, SC_LANES), SC_LANES)] = jnp.zeros((SC_LANES,), F32)

        @pl.when(q + 1 < nsteps)
        def _():
            tn = (q + 1) // ngrp
            gn = (q + 1) % ngrp

            @pl.when(gn == 0)
            def _():
                pltpu.sync_copy(idx_hbm.at[tok0 + base + tn], idx_v.at[tn % 2])

            gather(tn % 2, gn, 1 - buf).start()

        gather(t_loc % 2, g, buf).wait()

        @pl.loop(0, SC_GROUP, step=4)
        def _(rb):
            @pl.loop(0, nchunk, step=SC_CHUNK_UNROLL)
            def _(c0):
                accs = [jnp.zeros((SC_LANES,), F32) for _ in range(4)]
                for cc in range(SC_CHUNK_UNROLL):
                    sl = pl.ds(pl.multiple_of((c0 + cc) * SC_LANES, SC_LANES), SC_LANES)
                    hc = h_v[sl]
                    for i in range(4):
                        accs[i] = accs[i] + rows_v[buf, rb + i, sl] * hc
                for i in range(4):
                    row = pl.multiple_of((g * SC_GROUP + rb + i) * SC_LANES, SC_LANES)
                    plsc.addupdate(z_v.at[pl.ds(row, SC_LANES)], accs[i])

        @pl.when(g == ngrp - 1)
        def _():
            pltpu.sync_copy(z_v, z_hbm.at[t])


def _sc_z(u, idx, h, tok0, ntok):
    t, d = h.shape
    tpw = ntok // SC_WORKERS
    idx4 = idx.reshape(t, N_SEL // SC_GROUP, SC_GROUP)
    body = functools.partial(_sc_z_body, tpw=tpw, d=d, tok0=tok0)
    return pl.kernel(
        body,
        out_type=jax.ShapeDtypeStruct((ntok, N_SEL * SC_LANES), F32),
        mesh=_sc_mesh(),
        scratch_types=[
            pltpu.VMEM((2, N_SEL // SC_GROUP, SC_GROUP), jnp.int32),
            pltpu.VMEM((d,), F32),
            pltpu.VMEM((2, SC_GROUP, d), F32),
            pltpu.VMEM((N_SEL * SC_LANES,), F32),
            pltpu.SemaphoreType.DMA((2,)),
        ],
        name="sc_z",
    )(u, idx4, h)


def _sc_out_body(v_hbm, idx_hbm, coef_hbm, f_hbm, idx_v, coef_v, rows_v, out_v, sem, *, tpw, d, tok0):
    ngrp = N_SEL // SC_GROUP
    nchunk = d // SC_LANES
    nsteps = tpw * ngrp
    base = _worker_id() * tpw

    def gather(tok_buf, g, buf):
        return pltpu.make_async_copy(v_hbm.at[idx_v.at[tok_buf, g]], rows_v.at[buf], sem.at[buf])

    pltpu.sync_copy(idx_hbm.at[tok0 + base], idx_v.at[0])
    gather(0, 0, 0).start()

    @pl.loop(0, nsteps)
    def _(q):
        t_loc = q // ngrp
        g = q % ngrp
        buf = q % 2
        t = base + t_loc

        @pl.when(g == 0)
        def _():
            pltpu.sync_copy(coef_hbm.at[t], coef_v)

        @pl.when(q + 1 < nsteps)
        def _():
            tn = (q + 1) // ngrp
            gn = (q + 1) % ngrp

            @pl.when(gn == 0)
            def _():
                pltpu.sync_copy(idx_hbm.at[tok0 + base + tn], idx_v.at[tn % 2])

            gather(tn % 2, gn, 1 - buf).start()

        gather(t_loc % 2, g, buf).wait()

        @pl.loop(0, nchunk, step=SC_OUT_CHUNKS)
        def _(c0):
            sls = [pl.ds(pl.multiple_of((c0 + cc) * SC_LANES, SC_LANES), SC_LANES)
                   for cc in range(SC_OUT_CHUNKS)]
            accs = [None] * SC_OUT_CHUNKS
            for r in range(SC_GROUP):
                cf = coef_v[pl.ds(pl.multiple_of((g * SC_GROUP + r) * SC_LANES, SC_LANES), SC_LANES)]
                for cc in range(SC_OUT_CHUNKS):
                    p = rows_v[buf, r, sls[cc]] * cf
                    accs[cc] = p if accs[cc] is None else accs[cc] + p
            for cc in range(SC_OUT_CHUNKS):
                @pl.when(g == 0)
                def _():
                    out_v[sls[cc]] = accs[cc]

                @pl.when(g != 0)
                def _():
                    plsc.addupdate(out_v.at[sls[cc]], accs[cc])

        @pl.when(g == ngrp - 1)
        def _():
            pltpu.sync_copy(out_v, f_hbm.at[t])


def _sc_out(v, idx, coef16, tok0):
    t = idx.shape[0]
    ntok = coef16.shape[0]
    d = v.shape[1]
    tpw = ntok // SC_WORKERS
    idx4 = idx.reshape(t, N_SEL // SC_GROUP, SC_GROUP)
    body = functools.partial(_sc_out_body, tpw=tpw, d=d, tok0=tok0)
    return pl.kernel(
        body,
        out_type=jax.ShapeDtypeStruct((ntok, d), F32),
        mesh=_sc_mesh(),
        scratch_types=[
            pltpu.VMEM((2, N_SEL // SC_GROUP, SC_GROUP), jnp.int32),
            pltpu.VMEM((N_SEL * SC_LANES,), F32),
            pltpu.VMEM((2, SC_GROUP, d), F32),
            pltpu.VMEM((d,), F32),
            pltpu.SemaphoreType.DMA((2,)),
        ],
        name="sc_out",
    )(v, idx4, coef16)


def _coef_kernel(zp_ref, gate_ref, sel_ref, selt_ref, after_a_ref, after_b_ref, o_ref):
    z = sum(_dot(part, sel_ref[...]) for part in _split3(zp_ref[...]))
    coef = gate_ref[...] * (0.5 * z * (1.0 + lax.erf(z * (2.0 ** -0.5))))
    o_ref[...] = sum(_dot(part, selt_ref[...]) for part in _split3(coef))


def _coef(zp, gate_tok, tok0, after_a, after_b, tm=256):
    ntok, wide = zp.shape
    b0 = tok0 // tm
    grp = np.arange(wide) // SC_LANES
    sel = jnp.asarray(grp[:, None] == np.arange(N_SEL)[None, :], BF16)
    return pl.pallas_call(
        _coef_kernel,
        grid=(ntok // tm,),
        in_specs=[
            pl.BlockSpec((tm, wide), lambda i: (i, 0)),
            pl.BlockSpec((tm, N_SEL), lambda i: (b0 + i, 0)),
            pl.BlockSpec(sel.shape, lambda i: (0, 0)),
            pl.BlockSpec(sel.shape[::-1], lambda i: (0, 0)),
            pl.BlockSpec((SUBLANES, LANES), lambda i: (0, 0)),
            pl.BlockSpec((SUBLANES, LANES), lambda i: (0, 0)),
        ],
        out_specs=pl.BlockSpec((tm, wide), lambda i: (i, 0)),
        out_shape=jax.ShapeDtypeStruct((ntok, wide), F32),
        compiler_params=_cparams(("parallel",)),
        name="coef",
    )(zp, gate_tok, sel, sel.T, after_a, after_b)


def _ln_out_kernel(h_ref, f_ref, g_ref, b_ref, o_ref):
    o_ref[...] = _layer_norm(DEEPNORM_ALPHA * h_ref[...] + f_ref[...], g_ref[...], b_ref[...])


def _ln_out(h1, f, l2g, l2b, tok0, tm=256):
    ntok, d = f.shape
    b0 = tok0 // tm
    return pl.pallas_call(
        _ln_out_kernel,
        grid=(ntok // tm,),
        in_specs=[
            pl.BlockSpec((tm, d), lambda i: (b0 + i, 0)),
            pl.BlockSpec((tm, d), lambda i: (i, 0)),
            pl.BlockSpec(l2g.shape, lambda i: (0, 0)),
            pl.BlockSpec(l2b.shape, lambda i: (0, 0)),
        ],
        out_specs=pl.BlockSpec((tm, d), lambda i: (i, 0)),
        out_shape=jax.ShapeDtypeStruct((ntok, d), F32),
        compiler_params=_cparams(("parallel",)),
        name="ln_out",
    )(h1, f, l2g, l2b)


def _rope_tables(seq):
    pos = jnp.arange(seq, dtype=F32)
    inv_freq = ROPE_THETA ** (-jnp.arange(0, HEAD_DIM, 2, dtype=F32) / HEAD_DIM)
    ang = pos[:, None] * inv_freq[None, :]
    ang = jnp.concatenate([ang, ang], -1)
    sign = jnp.concatenate([-jnp.ones((HEAD_DIM // 2,), F32), jnp.ones((HEAD_DIM // 2,), F32)])
    reps = LANES // HEAD_DIM
    return jnp.tile(jnp.cos(ang), (1, reps)), jnp.tile(jnp.sin(ang) * sign, (1, reps))


def _layer(h2d, batch, seq, w_in, mu_prev, mu_next, w0, w2, a0, a2, g2, k_k, k_a, r_k, lnx_g, lnx_b,
           sink, w_out, ln1_g, ln1_b, peer_wq, peer_keys, peer_u, peer_v, ln2_g, ln2_b):
    w = RWKV_WIDTH
    row = lambda a: a.reshape(1, -1).astype(F32)
    w_in_p = jnp.pad(w_in, ((0, 0), (0, RWKV_COLS_PAD - RWKV_COLS))).astype(BF16)
    mup = jnp.pad(mu_prev, (0, RWKV_COLS_PAD - RWKV_COLS)).reshape(1, -1)
    mun = jnp.pad(mu_next, (0, RWKV_COLS_PAD - RWKV_COLS)).reshape(1, -1)
    wmix = jnp.zeros((LANES, 4 * w), F32)
    for d in range(2):
        wmix = wmix.at[d * DECAY_RANK:(d + 1) * DECAY_RANK, d * w:(d + 1) * w].set(w2[d])
        r0 = 2 * DECAY_RANK + d * ICLR_RANK
        wmix = wmix.at[r0:r0 + ICLR_RANK, (2 + d) * w:(3 + d) * w].set(a2[d])
    g2p = jnp.pad(g2, ((0, LANES - GATE_RANK), (0, 0)))
    head_of = np.arange(w) // HEAD_DIM
    bd = jnp.asarray(head_of[:, None] == head_of[None, :], F32)
    cos_t, sin_t = _rope_tables(seq)

    wq_b, keys_f, wout_b = peer_wq.astype(BF16), peer_keys.astype(F32), w_out.astype(BF16)

    def to_routing(x2d, nb):
        qkv, pr = _proj(x2d, w_in_p, cos_t, sin_t, seq)
        y_attn = _attention(qkv.reshape(nb, seq, ATTN_COLS), sink.astype(F32))
        r, v, kk, g, lw0, lw1, kd0, kd1, b0, b1 = _prep(
            pr, mup, mun, wmix, w0.astype(F32), a0.astype(F32), g2p, row(k_k), row(k_a), bd, seq)
        yf, yb = _wkv(r, v, kk, lw0, lw1, kd0, kd1, b0, b1, nb, seq)
        h1 = _mix(x2d, y_attn.reshape(nb * seq, ATTN_WIDTH), yf, yb, r, v, kd0, kd1, g,
                  row(lnx_g), row(lnx_b), row(r_k), bd, wout_b, row(ln1_g), row(ln1_b))
        return (h1,) + tuple(_route(h1, wq_b, keys_f))

    uv = jnp.concatenate([peer_u, peer_v], axis=1)
    l2g, l2b = row(ln2_g), row(ln2_b)

    nb_1 = batch * PEER_FIRST_GROUP_SEQ_8THS // 8
    t_1 = nb_1 * seq
    t_2 = (batch - nb_1) * seq
    n_sc = (batch * seq * PEER_SC_TOKEN_SHARE_128THS // 128) // PEER_TB * PEER_TB
    n_a = (t_2 * PEER_TC_FIRST_CALL_32NDS // 32) // PEER_TB * PEER_TB
    n_b = t_2 - n_a
    assert 0 < nb_1 < batch and 0 < n_sc < t_1 and n_sc % (SC_WORKERS * SUBLANES) == 0
    assert n_a > 0 and n_b > 0 and n_b % PEER_TB == 0 and (t_1 - n_sc) % PEER_TB == 0
    h1_1, idx_1, gate_t_1, gate_tok_1 = to_routing(h2d[:t_1], nb_1)
    zp = _sc_z(peer_u.astype(F32), idx_1, h1_1, 0, n_sc)
    h1_2, idx_2, gate_t_2, _ = to_routing(h2d[t_1:], batch - nb_1)
    out_tail = _peer(idx_1, gate_t_1, h1_1, uv, l2g, l2b, n_sc, t_1 - n_sc)
    out_a = _peer(idx_2, gate_t_2, h1_2, uv, l2g, l2b, 0, n_a)
    f_sc = _sc_out(peer_v.astype(F32), idx_1, _coef(zp, gate_tok_1, 0, out_tail, out_a), 0)
    out_b = _peer(idx_2, gate_t_2, h1_2, uv, l2g, l2b, n_a, n_b)
    out_sc = _ln_out(h1_1, f_sc, l2g, l2b, 0)
    return jnp.concatenate([out_sc, out_tail, out_a, out_b], axis=0)


def kernel(x, w_in, mu_prev, mu_next, w0, w2, a0, a2, g2, k_k, k_a, r_k, lnx_g, lnx_b, sink, w_out,
           ln1_g, ln1_b, peer_wq, peer_keys, peer_u, peer_v, ln2_g, ln2_b):
    batch, seq, d = x.shape
    h = x.reshape(batch * seq, d)
    for l in range(DEPTH):
        h = _layer(h, batch, seq, w_in[l], mu_prev[l], mu_next[l], w0[l], w2[l], a0[l], a2[l], g2[l],
                   k_k[l], k_a[l], r_k[l], lnx_g[l], lnx_b[l], sink[l], w_out[l], ln1_g[l], ln1_b[l],
                   peer_wq[l], peer_keys[l], peer_u[l], peer_v[l], ln2_g[l], ln2_b[l])
    return h.reshape(batch, seq, d)
```

```python
import functools

import numpy as np
import jax
import jax.numpy as jnp
from jax import lax
from jax.experimental import pallas as pl
from jax.experimental.pallas import tpu as pltpu
from jax.experimental.pallas import tpu_sc as plsc

F32 = jnp.float32
BF16 = jnp.bfloat16
HI = lax.Precision.HIGHEST

HEAD_DIM = 64
N_Q_HEADS = 8
N_KV_HEADS = 2
Q_PER_KV = N_Q_HEADS // N_KV_HEADS
ATTN_WIDTH = N_Q_HEADS * HEAD_DIM
ATTN_KV_WIDTH = N_KV_HEADS * HEAD_DIM
ATTN_COLS = ATTN_WIDTH + 2 * ATTN_KV_WIDTH
WINDOW = 128
ATTN_BLOCK = 128
ROPE_THETA = 10000.0
N_RWKV_HEADS = 8
RWKV_WIDTH = N_RWKV_HEADS * HEAD_DIM
DECAY_RANK = 32
ICLR_RANK = 32
GATE_RANK = 96
RWKV_COLS = 3 * RWKV_WIDTH + 2 * DECAY_RANK + 2 * ICLR_RANK + GATE_RANK
RWKV_COLS_PAD = 1792
RWKV_GN_EPS = 64e-5
PEER_HEADS = 8
PEER_NKEYS = 128
PEER_QDIM = 256
PEER_HALF = PEER_QDIM // 2
PEER_TOPK = 16
N_SEL = PEER_HEADS * PEER_TOPK
LN_EPS = 1e-5
DEPTH = 1
DEEPNORM_ALPHA = (2.0 * DEPTH) ** 0.25

LANES = 128
SUBLANES = 8
PREP_OUT_DTYPES = (BF16, BF16, BF16, F32, F32, F32, BF16, BF16, BF16, BF16)
WKV_CHUNK = 64
VMEM_LIMIT = 48 * 1024 * 1024
PEER_SLOTS = 4
PEER_TB = 256
ROUTE_HEADS_PER_STEP = 8
SC_LANES = 16
SC_WORKERS = 32
SC_GROUP = 32
SC_CHUNK_UNROLL = 8
SC_OUT_CHUNKS = 4
PEER_FIRST_GROUP_SEQ_8THS = 5
PEER_SC_TOKEN_SHARE_128THS = 73
PEER_TC_FIRST_CALL_32NDS = 10


def _cparams(sem):
    return pltpu.CompilerParams(dimension_semantics=sem, vmem_limit_bytes=VMEM_LIMIT)


def _dot(a, b, precision=None):
    return jnp.dot(a, b, preferred_element_type=F32, precision=precision)


def _dot_nt(a, b, precision=None):
    return lax.dot_general(a, b, (((1,), (1,)), ((), ())), preferred_element_type=F32,
                           precision=precision)


def _layer_norm(z, g, b):
    mu = jnp.mean(z, -1, keepdims=True)
    zc = z - mu
    var = jnp.mean(zc * zc, -1, keepdims=True)
    return zc * lax.rsqrt(var + LN_EPS) * g + b


def _proj_kernel(x_ref, w_ref, cos_ref, sin_ref, qkv_ref, pr_ref):
    xb = x_ref[...].astype(BF16)
    cos = cos_ref[...]
    sin = sin_ref[...]
    lane = lax.broadcasted_iota(jnp.int32, cos.shape, 1)
    first_half = (lane & (HEAD_DIM // 2)) == 0

    def rope(t):
        rot = jnp.where(first_half, pltpu.roll(t, LANES - HEAD_DIM // 2, 1),
                        pltpu.roll(t, HEAD_DIM // 2, 1))
        return t * cos + rot * sin

    for c in range(0, ATTN_COLS, 2 * LANES):
        acc = _dot(xb, w_ref[:, c:c + 2 * LANES])
        for half in range(2):
            col = c + half * LANES
            t = acc[:, half * LANES:(half + 1) * LANES]
            if col < ATTN_WIDTH + ATTN_KV_WIDTH:
                t = rope(t)
            qkv_ref[:, col:col + LANES] = t
    for c in range(0, RWKV_COLS_PAD, 2 * LANES):
        pr_ref[:, c:c + 2 * LANES] = _dot(xb, w_ref[:, ATTN_COLS + c:ATTN_COLS + c + 2 * LANES])


def _proj(x2, w_in_p, cos_t, sin_t, seq, tm=512):
    t, d = x2.shape
    n_pos = seq // tm
    return pl.pallas_call(
        _proj_kernel,
        grid=(t // tm,),
        in_specs=[
            pl.BlockSpec((tm, d), lambda i: (i, 0)),
            pl.BlockSpec(w_in_p.shape, lambda i: (0, 0)),
            pl.BlockSpec((tm, LANES), lambda i: (i % n_pos, 0)),
            pl.BlockSpec((tm, LANES), lambda i: (i % n_pos, 0)),
        ],
        out_specs=[
            pl.BlockSpec((tm, ATTN_COLS), lambda i: (i, 0)),
            pl.BlockSpec((tm, RWKV_COLS_PAD), lambda i: (i, 0)),
        ],
        out_shape=[
            jax.ShapeDtypeStruct((t, ATTN_COLS), F32),
            jax.ShapeDtypeStruct((t, RWKV_COLS_PAD), F32),
        ],
        compiler_params=_cparams(("parallel",)),
        name="proj",
    )(x2, w_in_p, cos_t, sin_t)


def _attn_kernel(sink_ref, q_ref, kp_ref, kc_ref, kn_ref, vp_ref, vc_ref, vn_ref, o_ref, *, nb):
    n = pl.program_id(1)
    blk = ATTN_BLOCK
    rows = Q_PER_KV * blk
    q = q_ref[0] * (HEAD_DIM ** -0.5)
    kwin = jnp.concatenate([kp_ref[0], kc_ref[0], kn_ref[0]], axis=0)
    vwin = jnp.concatenate([vp_ref[0], vc_ref[0], vn_ref[0]], axis=0)
    qi = lax.broadcasted_iota(jnp.int32, (rows, 3 * blk), 0) & (blk - 1)
    kj = lax.broadcasted_iota(jnp.int32, (rows, 3 * blk), 1)
    dist = kj - qi
    valid = ((dist >= blk - WINDOW) & (dist <= blk + WINDOW)
             & ((kj >= blk) | (n > 0)) & ((kj < 2 * blk) | (n < nb - 1)))
    rowg = lax.broadcasted_iota(jnp.int32, (rows, 1), 0) // blk
    outs = []
    for h in range(N_KV_HEADS):
        qs = jnp.concatenate(
            [q[:, (Q_PER_KV * h + g) * HEAD_DIM:(Q_PER_KV * h + g + 1) * HEAD_DIM]
             for g in range(Q_PER_KV)], axis=0)
        kh = kwin[:, h * HEAD_DIM:(h + 1) * HEAD_DIM]
        vh = vwin[:, h * HEAD_DIM:(h + 1) * HEAD_DIM]
        logits = _dot_nt(qs.astype(BF16), kh.astype(BF16))
        logits = jnp.where(valid, logits, -1e30)
        sk = jnp.zeros((rows, 1), F32)
        for g in range(Q_PER_KV):
            sk = jnp.where(rowg == g, sink_ref[Q_PER_KV * h + g], sk)
        m = jnp.maximum(jnp.max(logits, -1, keepdims=True), sk)
        e = jnp.exp(logits - m)
        den = jnp.sum(e, -1, keepdims=True) + jnp.exp(sk - m)
        p = e / den
        o = _dot(p.astype(BF16), vh.astype(BF16))
        for g in range(Q_PER_KV):
            outs.append(o[g * blk:(g + 1) * blk])
    o_ref[0] = jnp.concatenate(outs, axis=1)


def _attention(qkv3, sink):
    b, s, _ = qkv3.shape
    blk = ATTN_BLOCK
    nb = s // blk
    kcol = ATTN_WIDTH // LANES
    vcol = kcol + 1

    def spec(col, shift):
        def imap(bi, n):
            return (bi, jnp.clip(n + shift, 0, nb - 1), col)
        return pl.BlockSpec((1, blk, LANES), imap)

    return pl.pallas_call(
        functools.partial(_attn_kernel, nb=nb),
        grid=(b, nb),
        in_specs=[
            pl.BlockSpec(memory_space=pltpu.SMEM),
            pl.BlockSpec((1, blk, ATTN_WIDTH), lambda bi, n: (bi, n, 0)),
            spec(kcol, -1), spec(kcol, 0), spec(kcol, 1),
            spec(vcol, -1), spec(vcol, 0), spec(vcol, 1),
        ],
        out_specs=pl.BlockSpec((1, blk, ATTN_WIDTH), lambda bi, n: (bi, n, 0)),
        out_shape=jax.ShapeDtypeStruct((b, s, ATTN_WIDTH), F32),
        compiler_params=_cparams(("parallel", "parallel")),
        name="attn",
    )(sink, qkv3, qkv3, qkv3, qkv3, qkv3, qkv3, qkv3)


def _softplus(x):
    return jnp.maximum(x, 0.0) + jnp.log(1.0 + jnp.exp(-jnp.abs(x)))


def _sigmoid(x):
    return 1.0 / (1.0 + jnp.exp(-x))


def _prep_kernel(p_ref, hp_ref, hn_ref, mup_ref, mun_ref, wmix_ref, w0_ref, a0_ref, g2_ref,
                 kk_ref, ka_ref, bd_ref,
                 r_o, v_o, kk_o, g_o, lw0_o, lw1_o, kd0_o, kd1_o, b0_o, b1_o, *, tm, seq):
    i = pl.program_id(0)
    row = lax.broadcasted_iota(jnp.int32, (tm, 1), 0)
    seq_start = (i * tm) % seq == 0
    seq_end = ((i + 1) * tm) % seq == 0

    def shifted(c0, c1):
        p = p_ref[:, c0:c1]
        prev_row = jnp.where(seq_start, 0.0, hp_ref[SUBLANES - 1:SUBLANES, c0:c1])
        next_row = jnp.where(seq_end, 0.0, hn_ref[0:1, c0:c1])
        p_prev = jnp.where(row == 0, prev_row, pltpu.roll(p, 1, 0))
        p_next = jnp.where(row == tm - 1, next_row, pltpu.roll(p, tm - 1, 0))
        return p + mup_ref[:, c0:c1] * (p_prev - p) + mun_ref[:, c0:c1] * (p_next - p)

    w = RWKV_WIDTH
    r = shifted(0, w)
    k = shifted(w, 2 * w)
    v = shifted(2 * w, 3 * w)
    codes = shifted(3 * w, 3 * w + LANES)
    gd = shifted(3 * w + LANES, 3 * w + 2 * LANES)
    r_o[...] = r.astype(r_o.dtype)
    v_o[...] = v.astype(v_o.dtype)

    lane = lax.broadcasted_iota(jnp.int32, codes.shape, 1)
    codes = jnp.where(lane < 2 * DECAY_RANK, jnp.tanh(codes), codes)
    mm = _dot(codes, wmix_ref[...], HI)
    g_o[...] = _dot(_sigmoid(gd), g2_ref[...], HI)

    kkv = k * kk_ref[...]
    bd = bd_ref[...].astype(BF16)
    ss = sum(_dot(part, bd) for part in _split3(kkv * kkv))
    kkn = kkv * lax.rsqrt(jnp.maximum(ss, 1e-24))
    kk_o[...] = kkn.astype(kk_o.dtype)

    ka = ka_ref[...]
    for d, (lw_o, kd_o, b_o) in enumerate(((lw0_o, kd0_o, b0_o), (lw1_o, kd1_o, b1_o))):
        w_log = -_softplus(-(w0_ref[d:d + 1, :] + mm[:, d * w:(d + 1) * w])) - 0.5
        lw_o[...] = -jnp.exp(w_log)
        a = _sigmoid(a0_ref[d:d + 1, :] + mm[:, (2 + d) * w:(3 + d) * w])
        kd_o[...] = (k * (1.0 + (a - 1.0) * ka)).astype(kd_o.dtype)
        b_o[...] = (kkn * a).astype(b_o.dtype)


def _prep(pr, mup, mun, wmix, w0, a0, g2p, k_k, k_a, bd, seq, tm=256):
    t = pr.shape[0]
    nblk8 = t // SUBLANES
    per = tm // SUBLANES
    full = lambda a: pl.BlockSpec(a.shape, lambda i: (0,) * a.ndim)
    out = pl.BlockSpec((tm, RWKV_WIDTH), lambda i: (i, 0))
    return pl.pallas_call(
        functools.partial(_prep_kernel, tm=tm, seq=seq),
        grid=(t // tm,),
        in_specs=[
            pl.BlockSpec((tm, RWKV_COLS_PAD), lambda i: (i, 0)),
            pl.BlockSpec((SUBLANES, RWKV_COLS_PAD), lambda i: (jnp.maximum(i * per - 1, 0), 0)),
            pl.BlockSpec((SUBLANES, RWKV_COLS_PAD),
                         lambda i: (jnp.minimum((i + 1) * per, nblk8 - 1), 0)),
            full(mup), full(mun), full(wmix), full(w0), full(a0), full(g2p),
            full(k_k), full(k_a), full(bd),
        ],
        out_specs=[out] * 10,
        out_shape=[jax.ShapeDtypeStruct((t, RWKV_WIDTH), dt) for dt in PREP_OUT_DTYPES],
        compiler_params=_cparams(("parallel",)),
        name="prep",
    )(pr, pr, pr, mup, mun, wmix, w0, a0, g2p, k_k, k_a, bd)


def _split3(x):
    hi = x.astype(BF16)
    r1 = x - hi.astype(F32)
    mid = r1.astype(BF16)
    lo = (r1 - mid.astype(F32)).astype(BF16)
    return hi, mid, lo


def _wkv_kernel(rf, vf, kkf, lwf, kdf, bf, rb, vb, kkb, lwb, kdb, bb, yf_ref, yb_ref, state):
    c = pl.program_id(1)
    n = WKV_CHUNK
    hd = HEAD_DIM
    nh = N_RWKV_HEADS
    def bmm(a, b, ca, cb):
        return lax.dot_general(a.astype(BF16), b.astype(BF16), (((ca,), (cb,)), ((0,), (0,))),
                               preferred_element_type=F32)

    @pl.when(c == 0)
    def _():
        state[...] = jnp.zeros_like(state)

    ti = lax.broadcasted_iota(jnp.int32, (n, n), 0)
    si = lax.broadcasted_iota(jnp.int32, (n, n), 1)
    dirs = ((rf, vf, kkf, lwf, kdf, bf, si <= ti, si < ti, n - 1),
            (rb, vb, kkb, lwb, kdb, bb, si >= ti, si > ti, 0))
    heads = lambda x: jnp.stack([x[:, j * hd:(j + 1) * hd] for j in range(nh)], 0)
    parts = []
    for r_ref, v_ref, kk_ref, lw_ref, kd_ref, b_ref, incl, strict, last in dirs:
        lw = lw_ref[0]
        tri = incl.astype(BF16)
        cum = sum(_dot(tri, part) for part in _split3(lw))
        e_neg = jnp.exp(-cum)
        e_last = jnp.exp(cum[last:last + 1, :] - cum)
        g_scale = jnp.exp(cum[last:last + 1, :])
        parts.append(dict(
            at=heads((-kk_ref[0] * jnp.exp(cum - lw)).astype(BF16)),
            rt=heads((r_ref[0] * jnp.exp(cum)).astype(BF16)),
            bt=heads((b_ref[0] * e_neg).astype(BF16)),
            kt=heads((kd_ref[0] * e_neg).astype(BF16)),
            bl=heads((b_ref[0] * e_last).astype(BF16)),
            kl=heads((kd_ref[0] * e_last).astype(BF16)),
            v=heads(v_ref[0].astype(BF16)),
            gs=heads(g_scale),
            incl=jnp.broadcast_to(incl[None], (nh, n, n)),
            strict=jnp.broadcast_to(strict[None], (nh, n, n))))
    cat = lambda key: jnp.concatenate([parts[0][key], parts[1][key]], 0)
    at, rt, bt, kt, bl, kl, v, gs = (cat(k) for k in ("at", "rt", "bt", "kt", "bl", "kl", "v", "gs"))
    incl, strict = cat("incl"), cat("strict")
    g0 = state[...]
    g0b = g0.astype(BF16)
    m1 = bmm(jnp.concatenate([at, rt], 1), jnp.concatenate([bt, kt], 1), 2, 2)
    a_ab = jnp.where(strict, m1[:, :n, :n], 0.0)
    a_ak = jnp.where(strict, m1[:, :n, n:], 0.0)
    a_rb = jnp.where(incl, m1[:, n:, :n], 0.0)
    a_rk = jnp.where(incl, m1[:, n:, n:], 0.0)
    tinv = jnp.where((ti == si)[None], 1.0, a_ab)
    pw = a_ab
    for _ in range(int(np.log2(n)) - 1):
        pw = bmm(pw, pw, 2, 1)
        tinv = tinv + bmm(tinv, pw, 2, 1)
    rhs = bmm(a_ak, v, 2, 1) + bmm(at, g0b, 2, 2)
    u = bmm(tinv, rhs, 2, 1)
    uv = jnp.concatenate([u.astype(BF16), v], 1)
    y = bmm(jnp.concatenate([a_rb, a_rk], 2), uv, 2, 1) + bmm(rt, g0b, 2, 2)
    state[...] = g0 * gs + bmm(uv, jnp.concatenate([bl, kl], 1), 1, 1)
    for d, y_ref in enumerate((yf_ref, yb_ref)):
        y_ref[0] = jnp.concatenate([y[d * nh + j] for j in range(nh)], axis=1)


def _wkv(r, v, kk, lw0, lw1, kd0, kd1, b0, b1, batch, seq):
    n = WKV_CHUNK
    nc = seq // n
    shp = (batch, seq, RWKV_WIDTH)
    arrs = [a.reshape(shp) for a in (r, v, kk, lw0, kd0, b0, r, v, kk, lw1, kd1, b1)]
    fwd = pl.BlockSpec((1, n, RWKV_WIDTH), lambda b, c: (b, c, 0))
    bwd = pl.BlockSpec((1, n, RWKV_WIDTH), lambda b, c: (b, nc - 1 - c, 0))
    yf, yb = pl.pallas_call(
        _wkv_kernel,
        grid=(batch, nc),
        in_specs=[fwd] * 6 + [bwd] * 6,
        out_specs=[fwd, bwd],
        out_shape=[jax.ShapeDtypeStruct(shp, F32)] * 2,
        scratch_shapes=[pltpu.VMEM((2 * N_RWKV_HEADS, HEAD_DIM, HEAD_DIM), F32)],
        compiler_params=_cparams(("parallel", "arbitrary")),
        name="wkv",
    )(*arrs)
    return yf.reshape(batch * seq, RWKV_WIDTH), yb.reshape(batch * seq, RWKV_WIDTH)


def _mix_kernel(x_ref, ya_ref, yf_ref, yb_ref, r_ref, v_ref, kd0_ref, kd1_ref, g_ref,
                lng_ref, lnb_ref, rk_ref, bd_ref, wout_ref, l1g_ref, l1b_ref, h_ref):
    bd = bd_ref[...].astype(BF16)
    head_sum = lambda t: sum(_dot(part, bd) for part in _split3(t))
    inv = 1.0 / HEAD_DIM
    y = yf_ref[...] + yb_ref[...]
    mu = head_sum(y) * inv
    yc = y - mu
    var = head_sum(yc * yc) * inv
    yn = yc * lax.rsqrt(var + RWKV_GN_EPS) * lng_ref[...] + lnb_ref[...]
    k_mean = 0.5 * (kd0_ref[...] + kd1_ref[...])
    v = v_ref[...]
    bonus = head_sum(r_ref[...] * k_mean * rk_ref[...]) * v
    yr = (yn + bonus) * g_ref[...]
    mix = (_dot(ya_ref[...].astype(BF16), wout_ref[:ATTN_WIDTH, :])
           + _dot(yr.astype(BF16), wout_ref[ATTN_WIDTH:, :]))
    h_ref[...] = _layer_norm(DEEPNORM_ALPHA * x_ref[...] + mix, l1g_ref[...], l1b_ref[...])


def _mix(x2, ya, yf, yb, r, v, kd0, kd1, g, lng, lnb, rk, bd, wout, l1g, l1b, tm=256):
    t, d = x2.shape
    full = lambda a: pl.BlockSpec(a.shape, lambda i: (0,) * a.ndim)
    half = pl.BlockSpec((tm, RWKV_WIDTH), lambda i: (i, 0))
    wide = pl.BlockSpec((tm, d), lambda i: (i, 0))
    return pl.pallas_call(
        _mix_kernel,
        grid=(t // tm,),
        in_specs=[wide] + [half] * 8 + [full(lng), full(lnb), full(rk), full(bd), full(wout),
                                        full(l1g), full(l1b)],
        out_specs=wide,
        out_shape=jax.ShapeDtypeStruct((t, d), F32),
        compiler_params=_cparams(("parallel",)),
        name="mix",
    )(x2, ya, yf, yb, r, v, kd0, kd1, g, lng, lnb, rk, bd, wout, l1g, l1b)


def _top_rows(scs, k, payloads=None):
    n = scs[0].shape[0]
    iota = lax.broadcasted_iota(jnp.int32, scs[0].shape, 0).astype(F32)
    scs = list(scs)
    vals = [[] for _ in scs]
    picks = [[] for _ in scs]
    for _ in range(k):
        for a, sc in enumerate(scs):
            m = jnp.max(sc, axis=0, keepdims=True)
            pos = jnp.min(jnp.where(sc == m, iota, float(n)), axis=0, keepdims=True)
            hit = iota == pos
            vals[a].append(m)
            if payloads is None:
                picks[a].append(pos)
            else:
                picks[a].append(jnp.max(jnp.where(hit, payloads[a], -1.0), axis=0, keepdims=True))
            scs[a] = jnp.where(hit, -jnp.inf, sc)
    return [(jnp.concatenate(v, 0), jnp.concatenate(p, 0)) for v, p in zip(vals, picks)]


def _route_kernel(h_ref, wq_ref, keys_ref, idx_ref, gate_ref, gate_tok_ref, q_scr, idx_scr):
    k = PEER_TOPK
    q_scr[...] = _dot(h_ref[...].astype(BF16), wq_ref[...])

    def head_group(hg, carry):
        heads = [hg * ROUTE_HEADS_PER_STEP + i for i in range(ROUTE_HEADS_PER_STEP)]
        scores = []
        for hh in heads:
            for p in range(2):
                col = pl.multiple_of(hh * PEER_QDIM + p * PEER_HALF, PEER_HALF)
                qp = q_scr[:, pl.ds(col, PEER_HALF)]
                scores.append(_dot_nt(keys_ref[hh, p], qp, HI))
        tops = _top_rows(scores, k)
        cands, cand_ids = [], []
        for a in range(len(heads)):
            (s1, i1), (s2, i2) = tops[2 * a], tops[2 * a + 1]
            cands.append(jnp.concatenate([s1[i:i + 1] + s2[:k // (i + 1)] for i in range(k)], 0))
            cand_ids.append(jnp.concatenate(
                [i1[i:i + 1] * PEER_NKEYS + i2[:k // (i + 1)] for i in range(k)], 0))
        for hh, (cs, ids) in zip(heads, _top_rows(cands, k, cand_ids)):
            e = jnp.exp(cs - jnp.max(cs, axis=0, keepdims=True))
            row = pl.multiple_of(hh * k, k)
            gate_ref[pl.ds(row, k), :] = e / jnp.sum(e, axis=0, keepdims=True)
            idx_scr[pl.ds(row, k), :] = ids.astype(jnp.int32)
        return carry

    lax.fori_loop(0, PEER_HEADS // ROUTE_HEADS_PER_STEP, head_group, 0)
    idx_ref[...] = idx_scr[...].T
    gate_tok_ref[...] = gate_ref[...].T


def _route(h1, wq, keys, tm=256):
    t, d = h1.shape
    return pl.pallas_call(
        _route_kernel,
        grid=(t // tm,),
        in_specs=[
            pl.BlockSpec((tm, d), lambda i: (i, 0)),
            pl.BlockSpec(wq.shape, lambda i: (0, 0)),
            pl.BlockSpec(keys.shape, lambda i: (0, 0, 0, 0)),
        ],
        out_specs=[
            pl.BlockSpec((tm, N_SEL), lambda i: (i, 0)),
            pl.BlockSpec((N_SEL, tm), lambda i: (0, i)),
            pl.BlockSpec((tm, N_SEL), lambda i: (i, 0)),
        ],
        out_shape=[
            jax.ShapeDtypeStruct((t, N_SEL), jnp.int32),
            jax.ShapeDtypeStruct((N_SEL, t), F32),
            jax.ShapeDtypeStruct((t, N_SEL), F32),
        ],
        scratch_shapes=[pltpu.VMEM((tm, PEER_HEADS * PEER_QDIM), F32),
                        pltpu.VMEM((N_SEL, tm), jnp.int32)],
        compiler_params=_cparams(("parallel",)),
        name="route",
    )(h1, wq, keys)


def _peer_kernel(idx_ref, gate_ref, h_ref, uv_hbm, l2g_ref, l2b_ref, o_ref, buf, f_scr, sem, *, tb):
    d = h_ref.shape[1]

    def issue(t, slot):
        for j in range(N_SEL):
            pltpu.make_async_copy(uv_hbm.at[pl.ds(idx_ref[t, j], 1)], buf.at[slot, pl.ds(j, 1)],
                                  sem.at[slot]).start()

    def wait_all(slot):
        pltpu.make_async_copy(uv_hbm.at[pl.ds(0, N_SEL)], buf.at[slot], sem.at[slot]).wait()

    for t_ahead in range(PEER_SLOTS - 1):
        issue(t_ahead, t_ahead)
    lane = lax.broadcasted_iota(jnp.int32, (N_SEL, tb), 1)

    def body(t, carry):
        slot = t % PEER_SLOTS

        @pl.when(t + PEER_SLOTS - 1 < tb)
        def _():
            issue(t + PEER_SLOTS - 1, (t + PEER_SLOTS - 1) % PEER_SLOTS)

        wait_all(slot)
        z = jnp.sum(buf[slot, :, 0:d] * h_ref[pl.ds(t, 1), :], axis=1, keepdims=True)
        gcol = jnp.sum(jnp.where(lane == t, gate_ref[...], 0.0), axis=1, keepdims=True)
        coef = gcol * (0.5 * z * (1.0 + lax.erf(z * (2.0 ** -0.5))))
        f_scr[pl.ds(t, 1), :] = jnp.sum(coef * buf[slot, :, d:2 * d], axis=0, keepdims=True)
        return carry

    lax.fori_loop(0, tb, body, 0)
    o_ref[...] = _layer_norm(DEEPNORM_ALPHA * h_ref[...] + f_scr[...], l2g_ref[...], l2b_ref[...])


def _peer(idx, gate_t, h1, uv, l2g, l2b, tok0, ntok, tb=PEER_TB):
    d = h1.shape[1]
    b0 = tok0 // tb
    return pl.pallas_call(
        functools.partial(_peer_kernel, tb=tb),
        grid=(ntok // tb,),
        in_specs=[
            pl.BlockSpec((tb, N_SEL), lambda i: (b0 + i, 0), memory_space=pltpu.SMEM),
            pl.BlockSpec((N_SEL, tb), lambda i: (0, b0 + i)),
            pl.BlockSpec((tb, d), lambda i: (b0 + i, 0)),
            pl.BlockSpec(memory_space=pl.ANY),
            pl.BlockSpec(l2g.shape, lambda i: (0, 0)),
            pl.BlockSpec(l2b.shape, lambda i: (0, 0)),
        ],
        out_specs=pl.BlockSpec((tb, d), lambda i: (i, 0)),
        out_shape=jax.ShapeDtypeStruct((ntok, d), F32),
        scratch_shapes=[pltpu.VMEM((PEER_SLOTS, N_SEL, 2 * d), F32),
                        pltpu.VMEM((tb, d), F32),
                        pltpu.SemaphoreType.DMA((PEER_SLOTS,))],
        compiler_params=_cparams(("arbitrary",)),
        name="peer",
    )(idx, gate_t, h1, uv, l2g, l2b)


def _sc_mesh():
    return plsc.VectorSubcoreMesh(core_axis_name="c", subcore_axis_name="s")


def _worker_id():
    return lax.axis_index("s") * 2 + lax.axis_index("c")


def _sc_z_body(u_hbm, idx_hbm, h_hbm, z_hbm, idx_v, h_v, rows_v, z_v, sem, *, tpw, d, tok0):
    ngrp = N_SEL // SC_GROUP
    nchunk = d // SC_LANES
    nsteps = tpw * ngrp
    base = _worker_id() * tpw

    def gather(tok_buf, g, buf):
        return pltpu.make_async_copy(u_hbm.at[idx_v.at[tok_buf, g]], rows_v.at[buf], sem.at[buf])

    pltpu.sync_copy(idx_hbm.at[tok0 + base], idx_v.at[0])
    gather(0, 0, 0).start()

    @pl.loop(0, nsteps)
    def _(q):
        t_loc = q // ngrp
        g = q % ngrp
        buf = q % 2
        t = base + t_loc

        @pl.when(g == 0)
        def _():
            pltpu.sync_copy(h_hbm.at[tok0 + t], h_v)

            @pl.loop(0, N_SEL)
            def _(r):
                z_v[pl.ds(pl.multiple_of(r * SC_LANES, SC_LANES), SC_LANES)] = jnp.zeros((SC_LANES,), F32)

        @pl.when(q + 1 < nsteps)
        def _():
            tn = (q + 1) // ngrp
            gn = (q + 1) % ngrp

            @pl.when(gn == 0)
            def _():
                pltpu.sync_copy(idx_hbm.at[tok0 + base + tn], idx_v.at[tn % 2])

            gather(tn % 2, gn, 1 - buf).start()

        gather(t_loc % 2, g, buf).wait()

        @pl.loop(0, SC_GROUP, step=4)
        def _(rb):
            @pl.loop(0, nchunk, step=SC_CHUNK_UNROLL)
            def _(c0):
                accs = [jnp.zeros((SC_LANES,), F32) for _ in range(4)]
                for cc in range(SC_CHUNK_UNROLL):
                    sl = pl.ds(pl.multiple_of((c0 + cc) * SC_LANES, SC_LANES), SC_LANES)
                    hc = h_v[sl]
                    for i in range(4):
                        accs[i] = accs[i] + rows_v[buf, rb + i, sl] * hc
                for i in range(4):
                    row = pl.multiple_of((g * SC_GROUP + rb + i) * SC_LANES, SC_LANES)
                    plsc.addupdate(z_v.at[pl.ds(row, SC_LANES)], accs[i])

        @pl.when(g == ngrp - 1)
        def _():
            pltpu.sync_copy(z_v, z_hbm.at[t])


def _sc_z(u, idx, h, tok0, ntok):
    t, d = h.shape
    tpw = ntok // SC_WORKERS
    idx4 = idx.reshape(t, N_SEL // SC_GROUP, SC_GROUP)
    body = functools.partial(_sc_z_body, tpw=tpw, d=d, tok0=tok0)
    return pl.kernel(
        body,
        out_type=jax.ShapeDtypeStruct((ntok, N_SEL * SC_LANES), F32),
        mesh=_sc_mesh(),
        scratch_types=[
            pltpu.VMEM((2, N_SEL // SC_GROUP, SC_GROUP), jnp.int32),
            pltpu.VMEM((d,), F32),
            pltpu.VMEM((2, SC_GROUP, d), F32),
            pltpu.VMEM((N_SEL * SC_LANES,), F32),
            pltpu.SemaphoreType.DMA((2,)),
        ],
        name="sc_z",
    )(u, idx4, h)


def _sc_out_body(v_hbm, idx_hbm, coef_hbm, f_hbm, idx_v, coef_v, rows_v, out_v, sem, *, tpw, d, tok0):
    ngrp = N_SEL // SC_GROUP
    nchunk = d // SC_LANES
    nsteps = tpw * ngrp
    base = _worker_id() * tpw

    def gather(tok_buf, g, buf):
        return pltpu.make_async_copy(v_hbm.at[idx_v.at[tok_buf, g]], rows_v.at[buf], sem.at[buf])

    pltpu.sync_copy(idx_hbm.at[tok0 + base], idx_v.at[0])
    gather(0, 0, 0).start()

    @pl.loop(0, nsteps)
    def _(q):
        t_loc = q // ngrp
        g = q % ngrp
        buf = q % 2
        t = base + t_loc

        @pl.when(g == 0)
        def _():
            pltpu.sync_copy(coef_hbm.at[t], coef_v)

        @pl.when(q + 1 < nsteps)
        def _():
            tn = (q + 1) // ngrp
            gn = (q + 1) % ngrp

            @pl.when(gn == 0)
            def _():
                pltpu.sync_copy(idx_hbm.at[tok0 + base + tn], idx_v.at[tn % 2])

            gather(tn % 2, gn, 1 - buf).start()

        gather(t_loc % 2, g, buf).wait()

        @pl.loop(0, nchunk, step=SC_OUT_CHUNKS)
        def _(c0):
            sls = [pl.ds(pl.multiple_of((c0 + cc) * SC_LANES, SC_LANES), SC_LANES)
                   for cc in range(SC_OUT_CHUNKS)]
            accs = [None] * SC_OUT_CHUNKS
            for r in range(SC_GROUP):
                cf = coef_v[pl.ds(pl.multiple_of((g * SC_GROUP + r) * SC_LANES, SC_LANES), SC_LANES)]
                for cc in range(SC_OUT_CHUNKS):
                    p = rows_v[buf, r, sls[cc]] * cf
                    accs[cc] = p if accs[cc] is None else accs[cc] + p
            for cc in range(SC_OUT_CHUNKS):
                @pl.when(g == 0)
                def _():
                    out_v[sls[cc]] = accs[cc]

                @pl.when(g != 0)
                def _():
                    plsc.addupdate(out_v.at[sls[cc]], accs[cc])

        @pl.when(g == ngrp - 1)
        def _():
            pltpu.sync_copy(out_v, f_hbm.at[t])


def _sc_out(v, idx, coef16, tok0):
    t = idx.shape[0]
    ntok = coef16.shape[0]
    d = v.shape[1]
    tpw = ntok // SC_WORKERS
    idx4 = idx.reshape(t, N_SEL // SC_GROUP, SC_GROUP)
    body = functools.partial(_sc_out_body, tpw=tpw, d=d, tok0=tok0)
    return pl.kernel(
        body,
        out_type=jax.ShapeDtypeStruct((ntok, d), F32),
        mesh=_sc_mesh(),
        scratch_types=[
            pltpu.VMEM((2, N_SEL // SC_GROUP, SC_GROUP), jnp.int32),
            pltpu.VMEM((N_SEL * SC_LANES,), F32),
            pltpu.VMEM((2, SC_GROUP, d), F32),
            pltpu.VMEM((d,), F32),
            pltpu.SemaphoreType.DMA((2,)),
        ],
        name="sc_out",
    )(v, idx4, coef16)


def _coef_kernel(zp_ref, gate_ref, sel_ref, selt_ref, after_a_ref, after_b_ref, o_ref):
    z = sum(_dot(part, sel_ref[...]) for part in _split3(zp_ref[...]))
    coef = gate_ref[...] * (0.5 * z * (1.0 + lax.erf(z * (2.0 ** -0.5))))
    o_ref[...] = sum(_dot(part, selt_ref[...]) for part in _split3(coef))


def _coef(zp, gate_tok, tok0, after_a, after_b, tm=256):
    ntok, wide = zp.shape
    b0 = tok0 // tm
    grp = np.arange(wide) // SC_LANES
    sel = jnp.asarray(grp[:, None] == np.arange(N_SEL)[None, :], BF16)
    return pl.pallas_call(
        _coef_kernel,
        grid=(ntok // tm,),
        in_specs=[
            pl.BlockSpec((tm, wide), lambda i: (i, 0)),
            pl.BlockSpec((tm, N_SEL), lambda i: (b0 + i, 0)),
            pl.BlockSpec(sel.shape, lambda i: (0, 0)),
            pl.BlockSpec(sel.shape[::-1], lambda i: (0, 0)),
            pl.BlockSpec((SUBLANES, LANES), lambda i: (0, 0)),
            pl.BlockSpec((SUBLANES, LANES), lambda i: (0, 0)),
        ],
        out_specs=pl.BlockSpec((tm, wide), lambda i: (i, 0)),
        out_shape=jax.ShapeDtypeStruct((ntok, wide), F32),
        compiler_params=_cparams(("parallel",)),
        name="coef",
    )(zp, gate_tok, sel, sel.T, after_a, after_b)


def _ln_out_kernel(h_ref, f_ref, g_ref, b_ref, o_ref):
    o_ref[...] = _layer_norm(DEEPNORM_ALPHA * h_ref[...] + f_ref[...], g_ref[...], b_ref[...])


def _ln_out(h1, f, l2g, l2b, tok0, tm=256):
    ntok, d = f.shape
    b0 = tok0 // tm
    return pl.pallas_call(
        _ln_out_kernel,
        grid=(ntok // tm,),
        in_specs=[
            pl.BlockSpec((tm, d), lambda i: (b0 + i, 0)),
            pl.BlockSpec((tm, d), lambda i: (i, 0)),
            pl.BlockSpec(l2g.shape, lambda i: (0, 0)),
            pl.BlockSpec(l2b.shape, lambda i: (0, 0)),
        ],
        out_specs=pl.BlockSpec((tm, d), lambda i: (i, 0)),
        out_shape=jax.ShapeDtypeStruct((ntok, d), F32),
        compiler_params=_cparams(("parallel",)),
        name="ln_out",
    )(h1, f, l2g, l2b)


def _rope_tables(seq):
    pos = jnp.arange(seq, dtype=F32)
    inv_freq = ROPE_THETA ** (-jnp.arange(0, HEAD_DIM, 2, dtype=F32) / HEAD_DIM)
    ang = pos[:, None] * inv_freq[None, :]
    ang = jnp.concatenate([ang, ang], -1)
    sign = jnp.concatenate([-jnp.ones((HEAD_DIM // 2,), F32), jnp.ones((HEAD_DIM // 2,), F32)])
    reps = LANES // HEAD_DIM
    return jnp.tile(jnp.cos(ang), (1, reps)), jnp.tile(jnp.sin(ang) * sign, (1, reps))


def _layer(h2d, batch, seq, w_in, mu_prev, mu_next, w0, w2, a0, a2, g2, k_k, k_a, r_k, lnx_g, lnx_b,
           sink, w_out, ln1_g, ln1_b, peer_wq, peer_keys, peer_u, peer_v, ln2_g, ln2_b):
    w = RWKV_WIDTH
    row = lambda a: a.reshape(1, -1).astype(F32)
    w_in_p = jnp.pad(w_in, ((0, 0), (0, RWKV_COLS_PAD - RWKV_COLS))).astype(BF16)
    mup = jnp.pad(mu_prev, (0, RWKV_COLS_PAD - RWKV_COLS)).reshape(1, -1)
    mun = jnp.pad(mu_next, (0, RWKV_COLS_PAD - RWKV_COLS)).reshape(1, -1)
    wmix = jnp.zeros((LANES, 4 * w), F32)
    for d in range(2):
        wmix = wmix.at[d * DECAY_RANK:(d + 1) * DECAY_RANK, d * w:(d + 1) * w].set(w2[d])
        r0 = 2 * DECAY_RANK + d * ICLR_RANK
        wmix = wmix.at[r0:r0 + ICLR_RANK, (2 + d) * w:(3 + d) * w].set(a2[d])
    g2p = jnp.pad(g2, ((0, LANES - GATE_RANK), (0, 0)))
    head_of = np.arange(w) // HEAD_DIM
    bd = jnp.asarray(head_of[:, None] == head_of[None, :], F32)
    cos_t, sin_t = _rope_tables(seq)

    wq_b, keys_f, wout_b = peer_wq.astype(BF16), peer_keys.astype(F32), w_out.astype(BF16)

    def to_routing(x2d, nb):
        qkv, pr = _proj(x2d, w_in_p, cos_t, sin_t, seq)
        y_attn = _attention(qkv.reshape(nb, seq, ATTN_COLS), sink.astype(F32))
        r, v, kk, g, lw0, lw1, kd0, kd1, b0, b1 = _prep(
            pr, mup, mun, wmix, w0.astype(F32), a0.astype(F32), g2p, row(k_k), row(k_a), bd, seq)
        yf, yb = _wkv(r, v, kk, lw0, lw1, kd0, kd1, b0, b1, nb, seq)
        h1 = _mix(x2d, y_attn.reshape(nb * seq, ATTN_WIDTH), yf, yb, r, v, kd0, kd1, g,
                  row(lnx_g), row(lnx_b), row(r_k), bd, wout_b, row(ln1_g), row(ln1_b))
        return (h1,) + tuple(_route(h1, wq_b, keys_f))

    uv = jnp.concatenate([peer_u, peer_v], axis=1)
    l2g, l2b = row(ln2_g), row(ln2_b)

    nb_1 = batch * PEER_FIRST_GROUP_SEQ_8THS // 8
    t_1 = nb_1 * seq
    t_2 = (batch - nb_1) * seq
    n_sc = (batch * seq * PEER_SC_TOKEN_SHARE_128THS // 128) // PEER_TB * PEER_TB
    n_a = (t_2 * PEER_TC_FIRST_CALL_32NDS // 32) // PEER_TB * PEER_TB
    n_b = t_2 - n_a
    assert 0 < nb_1 < batch and 0 < n_sc < t_1 and n_sc % (SC_WORKERS * SUBLANES) == 0
    assert n_a > 0 and n_b > 0 and n_b % PEER_TB == 0 and (t_1 - n_sc) % PEER_TB == 0
    h1_1, idx_1, gate_t_1, gate_tok_1 = to_routing(h2d[:t_1], nb_1)
    zp = _sc_z(peer_u.astype(F32), idx_1, h1_1, 0, n_sc)
    h1_2, idx_2, gate_t_2, _ = to_routing(h2d[t_1:], batch - nb_1)
    out_tail = _peer(idx_1, gate_t_1, h1_1, uv, l2g, l2b, n_sc, t_1 - n_sc)
    out_a = _peer(idx_2, gate_t_2, h1_2, uv, l2g, l2b, 0, n_a)
    f_sc = _sc_out(peer_v.astype(F32), idx_1, _coef(zp, gate_tok_1, 0, out_tail, out_a), 0)
    out_b = _peer(idx_2, gate_t_2, h1_2, uv, l2g, l2b, n_a, n_b)
    out_sc = _ln_out(h1_1, f_sc, l2g, l2b, 0)
    return jnp.concatenate([out_sc, out_tail, out_a, out_b], axis=0)


def kernel(x, w_in, mu_prev, mu_next, w0, w2, a0, a2, g2, k_k, k_a, r_k, lnx_g, lnx_b, sink, w_out,
           ln1_g, ln1_b, peer_wq, peer_keys, peer_u, peer_v, ln2_g, ln2_b):
    batch, seq, d = x.shape
    h = x.reshape(batch * seq, d)
    for l in range(DEPTH):
        h = _layer(h, batch, seq, w_in[l], mu_prev[l], mu_next[l], w0[l], w2[l], a0[l], a2[l], g2[l],
                   k_k[l], k_a[l], r_k[l], lnx_g[l], lnx_b[l], sink[l], w_out[l], ln1_g[l], ln1_b[l],
                   peer_wq[l], peer_keys[l], peer_u[l], peer_v[l], ln2_g[l], ln2_b[l])
    return h.reshape(batch, seq, d)
```

```python
import functools

import numpy as np
import jax
import jax.numpy as jnp
from jax import lax
from jax.experimental import pallas as pl
from jax.experimental.pallas import tpu as pltpu
from jax.experimental.pallas import tpu_sc as plsc

F32 = jnp.float32
BF16 = jnp.bfloat16
HI = lax.Precision.HIGHEST

HEAD_DIM = 64
N_Q_HEADS = 8
N_KV_HEADS = 2
Q_PER_KV = N_Q_HEADS // N_KV_HEADS
ATTN_WIDTH = N_Q_HEADS * HEAD_DIM
ATTN_KV_WIDTH = N_KV_HEADS * HEAD_DIM
ATTN_COLS = ATTN_WIDTH + 2 * ATTN_KV_WIDTH
WINDOW = 128
ATTN_BLOCK = 128
ROPE_THETA = 10000.0
N_RWKV_HEADS = 8
RWKV_WIDTH = N_RWKV_HEADS * HEAD_DIM
DECAY_RANK = 32
ICLR_RANK = 32
GATE_RANK = 96
RWKV_COLS = 3 * RWKV_WIDTH + 2 * DECAY_RANK + 2 * ICLR_RANK + GATE_RANK
RWKV_COLS_PAD = 1792
RWKV_GN_EPS = 64e-5
PEER_HEADS = 8
PEER_NKEYS = 128
PEER_QDIM = 256
PEER_HALF = PEER_QDIM // 2
PEER_TOPK = 16
N_SEL = PEER_HEADS * PEER_TOPK
LN_EPS = 1e-5
DEPTH = 1
DEEPNORM_ALPHA = (2.0 * DEPTH) ** 0.25

LANES = 128
SUBLANES = 8
PREP_OUT_DTYPES = (BF16, BF16, BF16, F32, F32, F32, BF16, BF16, BF16, BF16)
WKV_CHUNK = 64
VMEM_LIMIT = 48 * 1024 * 1024
PEER_SLOTS = 6
PEER_TB = 256
ROUTE_HEADS_PER_STEP = 8
SC_LANES = 16
SC_WORKERS = 32
SC_GROUP = 32
SC_CHUNK_UNROLL = 8
SC_OUT_CHUNKS = 4
PEER_FIRST_GROUP_SEQ_8THS = 5
PEER_SC_TOKEN_SHARE_128THS = 69
PEER_TC_FIRST_CALL_32NDS = 8


def _cparams(sem):
    return pltpu.CompilerParams(dimension_semantics=sem, vmem_limit_bytes=VMEM_LIMIT)


def _dot(a, b, precision=None):
    return jnp.dot(a, b, preferred_element_type=F32, precision=precision)


def _dot_nt(a, b, precision=None):
    return lax.dot_general(a, b, (((1,), (1,)), ((), ())), preferred_element_type=F32,
                           precision=precision)


def _layer_norm(z, g, b):
    mu = jnp.mean(z, -1, keepdims=True)
    zc = z - mu
    var = jnp.mean(zc * zc, -1, keepdims=True)
    return zc * lax.rsqrt(var + LN_EPS) * g + b


def _proj_kernel(x_ref, w_ref, cos_ref, sin_ref, qkv_ref, pr_ref):
    xb = x_ref[...].astype(BF16)
    cos = cos_ref[...]
    sin = sin_ref[...]
    lane = lax.broadcasted_iota(jnp.int32, cos.shape, 1)
    first_half = (lane & (HEAD_DIM // 2)) == 0

    def rope(t):
        rot = jnp.where(first_half, pltpu.roll(t, LANES - HEAD_DIM // 2, 1),
                        pltpu.roll(t, HEAD_DIM // 2, 1))
        return t * cos + rot * sin

    for c in range(0, ATTN_COLS, 2 * LANES):
        acc = _dot(xb, w_ref[:, c:c + 2 * LANES])
        for half in range(2):
            col = c + half * LANES
            t = acc[:, half * LANES:(half + 1) * LANES]
            if col < ATTN_WIDTH + ATTN_KV_WIDTH:
                t = rope(t)
            qkv_ref[:, col:col + LANES] = t
    for c in range(0, RWKV_COLS_PAD, 2 * LANES):
        pr_ref[:, c:c + 2 * LANES] = _dot(xb, w_ref[:, ATTN_COLS + c:ATTN_COLS + c + 2 * LANES])


def _proj(x2, w_in_p, cos_t, sin_t, seq, tm=512):
    t, d = x2.shape
    n_pos = seq // tm
    return pl.pallas_call(
        _proj_kernel,
        grid=(t // tm,),
        in_specs=[
            pl.BlockSpec((tm, d), lambda i: (i, 0)),
            pl.BlockSpec(w_in_p.shape, lambda i: (0, 0)),
            pl.BlockSpec((tm, LANES), lambda i: (i % n_pos, 0)),
            pl.BlockSpec((tm, LANES), lambda i: (i % n_pos, 0)),
        ],
        out_specs=[
            pl.BlockSpec((tm, ATTN_COLS), lambda i: (i, 0)),
            pl.BlockSpec((tm, RWKV_COLS_PAD), lambda i: (i, 0)),
        ],
        out_shape=[
            jax.ShapeDtypeStruct((t, ATTN_COLS), F32),
            jax.ShapeDtypeStruct((t, RWKV_COLS_PAD), F32),
        ],
        compiler_params=_cparams(("parallel",)),
        name="proj",
    )(x2, w_in_p, cos_t, sin_t)


def _attn_kernel(sink_ref, q_ref, kp_ref, kc_ref, kn_ref, vp_ref, vc_ref, vn_ref, o_ref, *, nb):
    n = pl.program_id(1)
    blk = ATTN_BLOCK
    rows = Q_PER_KV * blk
    q = q_ref[0] * (HEAD_DIM ** -0.5)
    kwin = jnp.concatenate([kp_ref[0], kc_ref[0], kn_ref[0]], axis=0)
    vwin = jnp.concatenate([vp_ref[0], vc_ref[0], vn_ref[0]], axis=0)
    qi = lax.broadcasted_iota(jnp.int32, (rows, 3 * blk), 0) & (blk - 1)
    kj = lax.broadcasted_iota(jnp.int32, (rows, 3 * blk), 1)
    dist = kj - qi
    valid = ((dist >= blk - WINDOW) & (dist <= blk + WINDOW)
             & ((kj >= blk) | (n > 0)) & ((kj < 2 * blk) | (n < nb - 1)))
    rowg = lax.broadcasted_iota(jnp.int32, (rows, 1), 0) // blk
    outs = []
    for h in range(N_KV_HEADS):
        qs = jnp.concatenate(
            [q[:, (Q_PER_KV * h + g) * HEAD_DIM:(Q_PER_KV * h + g + 1) * HEAD_DIM]
             for g in range(Q_PER_KV)], axis=0)
        kh = kwin[:, h * HEAD_DIM:(h + 1) * HEAD_DIM]
        vh = vwin[:, h * HEAD_DIM:(h + 1) * HEAD_DIM]
        logits = _dot_nt(qs.astype(BF16), kh.astype(BF16))
        logits = jnp.where(valid, logits, -1e30)
        sk = jnp.zeros((rows, 1), F32)
        for g in range(Q_PER_KV):
            sk = jnp.where(rowg == g, sink_ref[Q_PER_KV * h + g], sk)
        m = jnp.maximum(jnp.max(logits, -1, keepdims=True), sk)
        e = jnp.exp(logits - m)
        den = jnp.sum(e, -1, keepdims=True) + jnp.exp(sk - m)
        p = e / den
        o = _dot(p.astype(BF16), vh.astype(BF16))
        for g in range(Q_PER_KV):
            outs.append(o[g * blk:(g + 1) * blk])
    o_ref[0] = jnp.concatenate(outs, axis=1)


def _attention(qkv3, sink):
    b, s, _ = qkv3.shape
    blk = ATTN_BLOCK
    nb = s // blk
    kcol = ATTN_WIDTH // LANES
    vcol = kcol + 1

    def spec(col, shift):
        def imap(bi, n):
            return (bi, jnp.clip(n + shift, 0, nb - 1), col)
        return pl.BlockSpec((1, blk, LANES), imap)

    return pl.pallas_call(
        functools.partial(_attn_kernel, nb=nb),
        grid=(b, nb),
        in_specs=[
            pl.BlockSpec(memory_space=pltpu.SMEM),
            pl.BlockSpec((1, blk, ATTN_WIDTH), lambda bi, n: (bi, n, 0)),
            spec(kcol, -1), spec(kcol, 0), spec(kcol, 1),
            spec(vcol, -1), spec(vcol, 0), spec(vcol, 1),
        ],
        out_specs=pl.BlockSpec((1, blk, ATTN_WIDTH), lambda bi, n: (bi, n, 0)),
        out_shape=jax.ShapeDtypeStruct((b, s, ATTN_WIDTH), F32),
        compiler_params=_cparams(("parallel", "parallel")),
        name="attn",
    )(sink, qkv3, qkv3, qkv3, qkv3, qkv3, qkv3, qkv3)


def _softplus(x):
    return jnp.maximum(x, 0.0) + jnp.log(1.0 + jnp.exp(-jnp.abs(x)))


def _sigmoid(x):
    return 1.0 / (1.0 + jnp.exp(-x))


def _prep_kernel(p_ref, hp_ref, hn_ref, mup_ref, mun_ref, wmix_ref, w0_ref, a0_ref, g2_ref,
                 kk_ref, ka_ref, bd_ref,
                 r_o, v_o, kk_o, g_o, lw0_o, lw1_o, kd0_o, kd1_o, b0_o, b1_o, *, tm, seq):
    i = pl.program_id(0)
    row = lax.broadcasted_iota(jnp.int32, (tm, 1), 0)
    seq_start = (i * tm) % seq == 0
    seq_end = ((i + 1) * tm) % seq == 0

    def shifted(c0, c1):
        p = p_ref[:, c0:c1]
        prev_row = jnp.where(seq_start, 0.0, hp_ref[SUBLANES - 1:SUBLANES, c0:c1])
        next_row = jnp.where(seq_end, 0.0, hn_ref[0:1, c0:c1])
        p_prev = jnp.where(row == 0, prev_row, pltpu.roll(p, 1, 0))
        p_next = jnp.where(row == tm - 1, next_row, pltpu.roll(p, tm - 1, 0))
        return p + mup_ref[:, c0:c1] * (p_prev - p) + mun_ref[:, c0:c1] * (p_next - p)

    w = RWKV_WIDTH
    r = shifted(0, w)
    k = shifted(w, 2 * w)
    v = shifted(2 * w, 3 * w)
    codes = shifted(3 * w, 3 * w + LANES)
    gd = shifted(3 * w + LANES, 3 * w + 2 * LANES)
    r_o[...] = r.astype(r_o.dtype)
    v_o[...] = v.astype(v_o.dtype)

    lane = lax.broadcasted_iota(jnp.int32, codes.shape, 1)
    codes = jnp.where(lane < 2 * DECAY_RANK, jnp.tanh(codes), codes)
    mm = _dot(codes, wmix_ref[...], HI)
    g_o[...] = _dot(_sigmoid(gd), g2_ref[...], HI)

    kkv = k * kk_ref[...]
    bd = bd_ref[...].astype(BF16)
    ss = sum(_dot(part, bd) for part in _split3(kkv * kkv))
    kkn = kkv * lax.rsqrt(jnp.maximum(ss, 1e-24))
    kk_o[...] = kkn.astype(kk_o.dtype)

    ka = ka_ref[...]
    for d, (lw_o, kd_o, b_o) in enumerate(((lw0_o, kd0_o, b0_o), (lw1_o, kd1_o, b1_o))):
        w_log = -_softplus(-(w0_ref[d:d + 1, :] + mm[:, d * w:(d + 1) * w])) - 0.5
        lw_o[...] = -jnp.exp(w_log)
        a = _sigmoid(a0_ref[d:d + 1, :] + mm[:, (2 + d) * w:(3 + d) * w])
        kd_o[...] = (k * (1.0 + (a - 1.0) * ka)).astype(kd_o.dtype)
        b_o[...] = (kkn * a).astype(b_o.dtype)


def _prep(pr, mup, mun, wmix, w0, a0, g2p, k_k, k_a, bd, seq, tm=256):
    t = pr.shape[0]
    nblk8 = t // SUBLANES
    per = tm // SUBLANES
    full = lambda a: pl.BlockSpec(a.shape, lambda i: (0,) * a.ndim)
    out = pl.BlockSpec((tm, RWKV_WIDTH), lambda i: (i, 0))
    return pl.pallas_call(
        functools.partial(_prep_kernel, tm=tm, seq=seq),
        grid=(t // tm,),
        in_specs=[
            pl.BlockSpec((tm, RWKV_COLS_PAD), lambda i: (i, 0)),
            pl.BlockSpec((SUBLANES, RWKV_COLS_PAD), lambda i: (jnp.maximum(i * per - 1, 0), 0)),
            pl.BlockSpec((SUBLANES, RWKV_COLS_PAD),
                         lambda i: (jnp.minimum((i + 1) * per, nblk8 - 1), 0)),
            full(mup), full(mun), full(wmix), full(w0), full(a0), full(g2p),
            full(k_k), full(k_a), full(bd),
        ],
        out_specs=[out] * 10,
        out_shape=[jax.ShapeDtypeStruct((t, RWKV_WIDTH), dt) for dt in PREP_OUT_DTYPES],
        compiler_params=_cparams(("parallel",)),
        name="prep",
    )(pr, pr, pr, mup, mun, wmix, w0, a0, g2p, k_k, k_a, bd)


def _split3(x):
    hi = x.astype(BF16)
    r1 = x - hi.astype(F32)
    mid = r1.astype(BF16)
    lo = (r1 - mid.astype(F32)).astype(BF16)
    return hi, mid, lo


def _wkv_kernel(rf, vf, kkf, lwf, kdf, bf, rb, vb, kkb, lwb, kdb, bb, yf_ref, yb_ref, state):
    c = pl.program_id(1)
    n = WKV_CHUNK
    hd = HEAD_DIM
    nh = N_RWKV_HEADS
    def bmm(a, b, ca, cb):
        return lax.dot_general(a.astype(BF16), b.astype(BF16), (((ca,), (cb,)), ((0,), (0,))),
                               preferred_element_type=F32)

    @pl.when(c == 0)
    def _():
        state[...] = jnp.zeros_like(state)

    ti = lax.broadcasted_iota(jnp.int32, (n, n), 0)
    si = lax.broadcasted_iota(jnp.int32, (n, n), 1)
    dirs = ((rf, vf, kkf, lwf, kdf, bf, si <= ti, si < ti, n - 1),
            (rb, vb, kkb, lwb, kdb, bb, si >= ti, si > ti, 0))
    heads = lambda x: jnp.stack([x[:, j * hd:(j + 1) * hd] for j in range(nh)], 0)
    parts = []
    for r_ref, v_ref, kk_ref, lw_ref, kd_ref, b_ref, incl, strict, last in dirs:
        lw = lw_ref[0]
        tri = incl.astype(BF16)
        cum = sum(_dot(tri, part) for part in _split3(lw))
        e_neg = jnp.exp(-cum)
        e_last = jnp.exp(cum[last:last + 1, :] - cum)
        g_scale = jnp.exp(cum[last:last + 1, :])
        parts.append(dict(
            at=heads((-kk_ref[0] * jnp.exp(cum - lw)).astype(BF16)),
            rt=heads((r_ref[0] * jnp.exp(cum)).astype(BF16)),
            bt=heads((b_ref[0] * e_neg).astype(BF16)),
            kt=heads((kd_ref[0] * e_neg).astype(BF16)),
            bl=heads((b_ref[0] * e_last).astype(BF16)),
            kl=heads((kd_ref[0] * e_last).astype(BF16)),
            v=heads(v_ref[0].astype(BF16)),
            gs=heads(g_scale),
            incl=jnp.broadcast_to(incl[None], (nh, n, n)),
            strict=jnp.broadcast_to(strict[None], (nh, n, n))))
    cat = lambda key: jnp.concatenate([parts[0][key], parts[1][key]], 0)
    at, rt, bt, kt, bl, kl, v, gs = (cat(k) for k in ("at", "rt", "bt", "kt", "bl", "kl", "v", "gs"))
    incl, strict = cat("incl"), cat("strict")
    g0 = state[...]
    g0b = g0.astype(BF16)
    m1 = bmm(jnp.concatenate([at, rt], 1), jnp.concatenate([bt, kt], 1), 2, 2)
    a_ab = jnp.where(strict, m1[:, :n, :n], 0.0)
    a_ak = jnp.where(strict, m1[:, :n, n:], 0.0)
    a_rb = jnp.where(incl, m1[:, n:, :n], 0.0)
    a_rk = jnp.where(incl, m1[:, n:, n:], 0.0)
    tinv = jnp.where((ti == si)[None], 1.0, a_ab)
    pw = a_ab
    for _ in range(int(np.log2(n)) - 1):
        pw = bmm(pw, pw, 2, 1)
        tinv = tinv + bmm(tinv, pw, 2, 1)
    rhs = bmm(a_ak, v, 2, 1) + bmm(at, g0b, 2, 2)
    u = bmm(tinv, rhs, 2, 1)
    uv = jnp.concatenate([u.astype(BF16), v], 1)
    y = bmm(jnp.concatenate([a_rb, a_rk], 2), uv, 2, 1) + bmm(rt, g0b, 2, 2)
    state[...] = g0 * gs + bmm(uv, jnp.concatenate([bl, kl], 1), 1, 1)
    for d, y_ref in enumerate((yf_ref, yb_ref)):
        y_ref[0] = jnp.concatenate([y[d * nh + j] for j in range(nh)], axis=1)


def _wkv(r, v, kk, lw0, lw1, kd0, kd1, b0, b1, batch, seq):
    n = WKV_CHUNK
    nc = seq // n
    shp = (batch, seq, RWKV_WIDTH)
    arrs = [a.reshape(shp) for a in (r, v, kk, lw0, kd0, b0, r, v, kk, lw1, kd1, b1)]
    fwd = pl.BlockSpec((1, n, RWKV_WIDTH), lambda b, c: (b, c, 0))
    bwd = pl.BlockSpec((1, n, RWKV_WIDTH), lambda b, c: (b, nc - 1 - c, 0))
    yf, yb = pl.pallas_call(
        _wkv_kernel,
        grid=(batch, nc),
        in_specs=[fwd] * 6 + [bwd] * 6,
        out_specs=[fwd, bwd],
        out_shape=[jax.ShapeDtypeStruct(shp, F32)] * 2,
        scratch_shapes=[pltpu.VMEM((2 * N_RWKV_HEADS, HEAD_DIM, HEAD_DIM), F32)],
        compiler_params=_cparams(("parallel", "arbitrary")),
        name="wkv",
    )(*arrs)
    return yf.reshape(batch * seq, RWKV_WIDTH), yb.reshape(batch * seq, RWKV_WIDTH)


def _mix_kernel(x_ref, ya_ref, yf_ref, yb_ref, r_ref, v_ref, kd0_ref, kd1_ref, g_ref,
                lng_ref, lnb_ref, rk_ref, bd_ref, wout_ref, l1g_ref, l1b_ref, h_ref):
    bd = bd_ref[...].astype(BF16)
    head_sum = lambda t: sum(_dot(part, bd) for part in _split3(t))
    inv = 1.0 / HEAD_DIM
    y = yf_ref[...] + yb_ref[...]
    mu = head_sum(y) * inv
    yc = y - mu
    var = head_sum(yc * yc) * inv
    yn = yc * lax.rsqrt(var + RWKV_GN_EPS) * lng_ref[...] + lnb_ref[...]
    k_mean = 0.5 * (kd0_ref[...] + kd1_ref[...])
    v = v_ref[...]
    bonus = head_sum(r_ref[...] * k_mean * rk_ref[...]) * v
    yr = (yn + bonus) * g_ref[...]
    mix = (_dot(ya_ref[...].astype(BF16), wout_ref[:ATTN_WIDTH, :])
           + _dot(yr.astype(BF16), wout_ref[ATTN_WIDTH:, :]))
    h_ref[...] = _layer_norm(DEEPNORM_ALPHA * x_ref[...] + mix, l1g_ref[...], l1b_ref[...])


def _mix(x2, ya, yf, yb, r, v, kd0, kd1, g, lng, lnb, rk, bd, wout, l1g, l1b, tm=256):
    t, d = x2.shape
    full = lambda a: pl.BlockSpec(a.shape, lambda i: (0,) * a.ndim)
    half = pl.BlockSpec((tm, RWKV_WIDTH), lambda i: (i, 0))
    wide = pl.BlockSpec((tm, d), lambda i: (i, 0))
    return pl.pallas_call(
        _mix_kernel,
        grid=(t // tm,),
        in_specs=[wide] + [half] * 8 + [full(lng), full(lnb), full(rk), full(bd), full(wout),
                                        full(l1g), full(l1b)],
        out_specs=wide,
        out_shape=jax.ShapeDtypeStruct((t, d), F32),
        compiler_params=_cparams(("parallel",)),
        name="mix",
    )(x2, ya, yf, yb, r, v, kd0, kd1, g, lng, lnb, rk, bd, wout, l1g, l1b)


def _top_rows(scs, k, payloads=None):
    n = scs[0].shape[0]
    iota = lax.broadcasted_iota(jnp.int32, scs[0].shape, 0).astype(F32)
    scs = list(scs)
    vals = [[] for _ in scs]
    picks = [[] for _ in scs]
    for _ in range(k):
        for a, sc in enumerate(scs):
            m = jnp.max(sc, axis=0, keepdims=True)
            pos = jnp.min(jnp.where(sc == m, iota, float(n)), axis=0, keepdims=True)
            hit = iota == pos
            vals[a].append(m)
            if payloads is None:
                picks[a].append(pos)
            else:
                picks[a].append(jnp.max(jnp.where(hit, payloads[a], -1.0), axis=0, keepdims=True))
            scs[a] = jnp.where(hit, -jnp.inf, sc)
    return [(jnp.concatenate(v, 0), jnp.concatenate(p, 0)) for v, p in zip(vals, picks)]


def _route_kernel(h_ref, wq_ref, keys_ref, idx_ref, gate_ref, gate_tok_ref, q_scr, idx_scr):
    k = PEER_TOPK
    q_scr[...] = _dot(h_ref[...].astype(BF16), wq_ref[...])

    def head_group(hg, carry):
        heads = [hg * ROUTE_HEADS_PER_STEP + i for i in range(ROUTE_HEADS_PER_STEP)]
        scores = []
        for hh in heads:
            for p in range(2):
                col = pl.multiple_of(hh * PEER_QDIM + p * PEER_HALF, PEER_HALF)
                qp = q_scr[:, pl.ds(col, PEER_HALF)]
                scores.append(_dot_nt(keys_ref[hh, p], qp, HI))
        tops = _top_rows(scores, k)
        cands, cand_ids = [], []
        for a in range(len(heads)):
            (s1, i1), (s2, i2) = tops[2 * a], tops[2 * a + 1]
            cands.append(jnp.concatenate([s1[i:i + 1] + s2[:k // (i + 1)] for i in range(k)], 0))
            cand_ids.append(jnp.concatenate(
                [i1[i:i + 1] * PEER_NKEYS + i2[:k // (i + 1)] for i in range(k)], 0))
        for hh, (cs, ids) in zip(heads, _top_rows(cands, k, cand_ids)):
            e = jnp.exp(cs - jnp.max(cs, axis=0, keepdims=True))
            row = pl.multiple_of(hh * k, k)
            gate_ref[pl.ds(row, k), :] = e / jnp.sum(e, axis=0, keepdims=True)
            idx_scr[pl.ds(row, k), :] = ids.astype(jnp.int32)
        return carry

    lax.fori_loop(0, PEER_HEADS // ROUTE_HEADS_PER_STEP, head_group, 0)
    idx_ref[...] = idx_scr[...].T
    gate_tok_ref[...] = gate_ref[...].T


def _route(h1, wq, keys, tm=256):
    t, d = h1.shape
    return pl.pallas_call(
        _route_kernel,
        grid=(t // tm,),
        in_specs=[
            pl.BlockSpec((tm, d), lambda i: (i, 0)),
            pl.BlockSpec(wq.shape, lambda i: (0, 0)),
            pl.BlockSpec(keys.shape, lambda i: (0, 0, 0, 0)),
        ],
        out_specs=[
            pl.BlockSpec((tm, N_SEL), lambda i: (i, 0)),
            pl.BlockSpec((N_SEL, tm), lambda i: (0, i)),
            pl.BlockSpec((tm, N_SEL), lambda i: (i, 0)),
        ],
        out_shape=[
            jax.ShapeDtypeStruct((t, N_SEL), jnp.int32),
            jax.ShapeDtypeStruct((N_SEL, t), F32),
            jax.ShapeDtypeStruct((t, N_SEL), F32),
        ],
        scratch_shapes=[pltpu.VMEM((tm, PEER_HEADS * PEER_QDIM), F32),
                        pltpu.VMEM((N_SEL, tm), jnp.int32)],
        compiler_params=_cparams(("parallel",)),
        name="route",
    )(h1, wq, keys)


def _peer_kernel(idx_ref, gate_ref, h_ref, uv_hbm, l2g_ref, l2b_ref, o_ref, buf, f_scr, sem, *, tb):
    d = h_ref.shape[1]

    def issue(t, slot):
        for j in range(N_SEL):
            pltpu.make_async_copy(uv_hbm.at[pl.ds(idx_ref[t, j], 1)], buf.at[slot, pl.ds(j, 1)],
                                  sem.at[slot]).start()

    def wait_all(slot):
        pltpu.make_async_copy(uv_hbm.at[pl.ds(0, N_SEL)], buf.at[slot], sem.at[slot]).wait()

    for t_ahead in range(PEER_SLOTS - 1):
        issue(t_ahead, t_ahead)
    lane = lax.broadcasted_iota(jnp.int32, (N_SEL, tb), 1)

    def body(t, carry):
        slot = t % PEER_SLOTS

        @pl.when(t + PEER_SLOTS - 1 < tb)
        def _():
            issue(t + PEER_SLOTS - 1, (t + PEER_SLOTS - 1) % PEER_SLOTS)

        wait_all(slot)
        z = jnp.sum(buf[slot, :, 0:d] * h_ref[pl.ds(t, 1), :], axis=1, keepdims=True)
        gcol = jnp.sum(jnp.where(lane == t, gate_ref[...], 0.0), axis=1, keepdims=True)
        coef = gcol * (0.5 * z * (1.0 + lax.erf(z * (2.0 ** -0.5))))
        f_scr[pl.ds(t, 1), :] = jnp.sum(coef * buf[slot, :, d:2 * d], axis=0, keepdims=True)
        return carry

    lax.fori_loop(0, tb, body, 0)
    o_ref[...] = _layer_norm(DEEPNORM_ALPHA * h_ref[...] + f_scr[...], l2g_ref[...], l2b_ref[...])


def _peer(idx, gate_t, h1, uv, l2g, l2b, tok0, ntok, tb=PEER_TB):
    d = h1.shape[1]
    b0 = tok0 // tb
    return pl.pallas_call(
        functools.partial(_peer_kernel, tb=tb),
        grid=(ntok // tb,),
        in_specs=[
            pl.BlockSpec((tb, N_SEL), lambda i: (b0 + i, 0), memory_space=pltpu.SMEM),
            pl.BlockSpec((N_SEL, tb), lambda i: (0, b0 + i)),
            pl.BlockSpec((tb, d), lambda i: (b0 + i, 0)),
            pl.BlockSpec(memory_space=pl.ANY),
            pl.BlockSpec(l2g.shape, lambda i: (0, 0)),
            pl.BlockSpec(l2b.shape, lambda i: (0, 0)),
        ],
        out_specs=pl.BlockSpec((tb, d), lambda i: (i, 0)),
        out_shape=jax.ShapeDtypeStruct((ntok, d), F32),
        scratch_shapes=[pltpu.VMEM((PEER_SLOTS, N_SEL, 2 * d), F32),
                        pltpu.VMEM((tb, d), F32),
                        pltpu.SemaphoreType.DMA((PEER_SLOTS,))],
        compiler_params=_cparams(("arbitrary",)),
        name="peer",
    )(idx, gate_t, h1, uv, l2g, l2b)


def _sc_mesh():
    return plsc.VectorSubcoreMesh(core_axis_name="c", subcore_axis_name="s")


def _worker_id():
    return lax.axis_index("s") * 2 + lax.axis_index("c")


def _sc_z_body(u_hbm, idx_hbm, h_hbm, z_hbm, idx_v, h_v, rows_v, z_v, sem, *, tpw, d, tok0):
    ngrp = N_SEL // SC_GROUP
    nchunk = d // SC_LANES
    nsteps = tpw * ngrp
    base = _worker_id() * tpw

    def gather(tok_buf, g, buf):
        return pltpu.make_async_copy(u_hbm.at[idx_v.at[tok_buf, g]], rows_v.at[buf], sem.at[buf])

    pltpu.sync_copy(idx_hbm.at[tok0 + base], idx_v.at[0])
    gather(0, 0, 0).start()

    @pl.loop(0, nsteps)
    def _(q):
        t_loc = q // ngrp
        g = q % ngrp
        buf = q % 2
        t = base + t_loc

        @pl.when(g == 0)
        def _():
            pltpu.sync_copy(h_hbm.at[tok0 + t], h_v)

            @pl.loop(0, N_SEL)
            def _(r):
                z_v[pl.ds(pl.multiple_of(r * SC_LANES, SC_LANES), SC_LANES)] = jnp.zeros((SC_LANES,), F32)

        @pl.when(q + 1 < nsteps)
        def _():
            tn = (q + 1) // ngrp
            gn = (q + 1) % ngrp

            @pl.when(gn == 0)
            def _():
                pltpu.sync_copy(idx_hbm.at[tok0 + base + tn], idx_v.at[tn % 2])

            gather(tn % 2, gn, 1 - buf).start()

        gather(t_loc % 2, g, buf).wait()

        @pl.loop(0, SC_GROUP, step=4)
        def _(rb):
            @pl.loop(0, nchunk, step=SC_CHUNK_UNROLL)
            def _(c0):
                accs = [jnp.zeros((SC_LANES,), F32) for _ in range(4)]
                for cc in range(SC_CHUNK_UNROLL):
                    sl = pl.ds(pl.multiple_of((c0 + cc) * SC_LANES, SC_LANES), SC_LANES)
                    hc = h_v[sl]
                    for i in range(4):
                        accs[i] = accs[i] + rows_v[buf, rb + i, sl] * hc
                for i in range(4):
                    row = pl.multiple_of((g * SC_GROUP + rb + i) * SC_LANES, SC_LANES)
                    plsc.addupdate(z_v.at[pl.ds(row, SC_LANES)], accs[i])

        @pl.when(g == ngrp - 1)
        def _():
            pltpu.sync_copy(z_v, z_hbm.at[t])


def _sc_z(u, idx, h, tok0, ntok):
    t, d = h.shape
    tpw = ntok // SC_WORKERS
    idx4 = idx.reshape(t, N_SEL // SC_GROUP, SC_GROUP)
    body = functools.partial(_sc_z_body, tpw=tpw, d=d, tok0=tok0)
    return pl.kernel(
        body,
        out_type=jax.ShapeDtypeStruct((ntok, N_SEL * SC_LANES), F32),
        mesh=_sc_mesh(),
        scratch_types=[
            pltpu.VMEM((2, N_SEL // SC_GROUP, SC_GROUP), jnp.int32),
            pltpu.VMEM((d,), F32),
            pltpu.VMEM((2, SC_GROUP, d), F32),
            pltpu.VMEM((N_SEL * SC_LANES,), F32),
            pltpu.SemaphoreType.DMA((2,)),
        ],
        name="sc_z",
    )(u, idx4, h)


def _sc_out_body(v_hbm, idx_hbm, coef_hbm, f_hbm, idx_v, coef_v, rows_v, out_v, sem, *, tpw, d, tok0):
    ngrp = N_SEL // SC_GROUP
    nchunk = d // SC_LANES
    nsteps = tpw * ngrp
    base = _worker_id() * tpw

    def gather(tok_buf, g, buf):
        return pltpu.make_async_copy(v_hbm.at[idx_v.at[tok_buf, g]], rows_v.at[buf], sem.at[buf])

    pltpu.sync_copy(idx_hbm.at[tok0 + base], idx_v.at[0])
    gather(0, 0, 0).start()

    @pl.loop(0, nsteps)
    def _(q):
        t_loc = q // ngrp
        g = q % ngrp
        buf = q % 2
        t = base + t_loc

        @pl.when(g == 0)
        def _():
            pltpu.sync_copy(coef_hbm.at[t], coef_v)

        @pl.when(q + 1 < nsteps)
        def _():
            tn = (q + 1) // ngrp
            gn = (q + 1) % ngrp

            @pl.when(gn == 0)
            def _():
                pltpu.sync_copy(idx_hbm.at[tok0 + base + tn], idx_v.at[tn % 2])

            gather(tn % 2, gn, 1 - buf).start()

        gather(t_loc % 2, g, buf).wait()

        @pl.loop(0, nchunk, step=SC_OUT_CHUNKS)
        def _(c0):
            sls = [pl.ds(pl.multiple_of((c0 + cc) * SC_LANES, SC_LANES), SC_LANES)
                   for cc in range(SC_OUT_CHUNKS)]
            accs = [None] * SC_OUT_CHUNKS
            for r in range(SC_GROUP):
                cf = coef_v[pl.ds(pl.multiple_of((g * SC_GROUP + r) * SC_LANES, SC_LANES), SC_LANES)]
                for cc in range(SC_OUT_CHUNKS):
                    p = rows_v[buf, r, sls[cc]] * cf
                    accs[cc] = p if accs[cc] is None else accs[cc] + p
            for cc in range(SC_OUT_CHUNKS):
                @pl.when(g == 0)
                def _():
                    out_v[sls[cc]] = accs[cc]

                @pl.when(g != 0)
                def _():
                    plsc.addupdate(out_v.at[sls[cc]], accs[cc])

        @pl.when(g == ngrp - 1)
        def _():
            pltpu.sync_copy(out_v, f_hbm.at[t])


def _sc_out(v, idx, coef16, tok0):
    t = idx.shape[0]
    ntok = coef16.shape[0]
    d = v.shape[1]
    tpw = ntok // SC_WORKERS
    idx4 = idx.reshape(t, N_SEL // SC_GROUP, SC_GROUP)
    body = functools.partial(_sc_out_body, tpw=tpw, d=d, tok0=tok0)
    return pl.kernel(
        body,
        out_type=jax.ShapeDtypeStruct((ntok, d), F32),
        mesh=_sc_mesh(),
        scratch_types=[
            pltpu.VMEM((2, N_SEL // SC_GROUP, SC_GROUP), jnp.int32),
            pltpu.VMEM((N_SEL * SC_LANES,), F32),
            pltpu.VMEM((2, SC_GROUP, d), F32),
            pltpu.VMEM((d,), F32),
            pltpu.SemaphoreType.DMA((2,)),
        ],
        name="sc_out",
    )(v, idx4, coef16)


def _coef_kernel(zp_ref, gate_ref, sel_ref, selt_ref, after_a_ref, after_b_ref, o_ref):
    z = sum(_dot(part, sel_ref[...]) for part in _split3(zp_ref[...]))
    coef = gate_ref[...] * (0.5 * z * (1.0 + lax.erf(z * (2.0 ** -0.5))))
    o_ref[...] = sum(_dot(part, selt_ref[...]) for part in _split3(coef))


def _coef(zp, gate_tok, tok0, after_a, after_b, tm=256):
    ntok, wide = zp.shape
    b0 = tok0 // tm
    grp = np.arange(wide) // SC_LANES
    sel = jnp.asarray(grp[:, None] == np.arange(N_SEL)[None, :], BF16)
    return pl.pallas_call(
        _coef_kernel,
        grid=(ntok // tm,),
        in_specs=[
            pl.BlockSpec((tm, wide), lambda i: (i, 0)),
            pl.BlockSpec((tm, N_SEL), lambda i: (b0 + i, 0)),
            pl.BlockSpec(sel.shape, lambda i: (0, 0)),
            pl.BlockSpec(sel.shape[::-1], lambda i: (0, 0)),
            pl.BlockSpec((SUBLANES, LANES), lambda i: (0, 0)),
            pl.BlockSpec((SUBLANES, LANES), lambda i: (0, 0)),
        ],
        out_specs=pl.BlockSpec((tm, wide), lambda i: (i, 0)),
        out_shape=jax.ShapeDtypeStruct((ntok, wide), F32),
        compiler_params=_cparams(("parallel",)),
        name="coef",
    )(zp, gate_tok, sel, sel.T, after_a, after_b)


def _ln_out_kernel(h_ref, f_ref, g_ref, b_ref, o_ref):
    o_ref[...] = _layer_norm(DEEPNORM_ALPHA * h_ref[...] + f_ref[...], g_ref[...], b_ref[...])


def _ln_out(h1, f, l2g, l2b, tok0, tm=256):
    ntok, d = f.shape
    b0 = tok0 // tm
    return pl.pallas_call(
        _ln_out_kernel,
        grid=(ntok // tm,),
        in_specs=[
            pl.BlockSpec((tm, d), lambda i: (b0 + i, 0)),
            pl.BlockSpec((tm, d), lambda i: (i, 0)),
            pl.BlockSpec(l2g.shape, lambda i: (0, 0)),
            pl.BlockSpec(l2b.shape, lambda i: (0, 0)),
        ],
        out_specs=pl.BlockSpec((tm, d), lambda i: (i, 0)),
        out_shape=jax.ShapeDtypeStruct((ntok, d), F32),
        compiler_params=_cparams(("parallel",)),
        name="ln_out",
    )(h1, f, l2g, l2b)


def _rope_tables(seq):
    pos = jnp.arange(seq, dtype=F32)
    inv_freq = ROPE_THETA ** (-jnp.arange(0, HEAD_DIM, 2, dtype=F32) / HEAD_DIM)
    ang = pos[:, None] * inv_freq[None, :]
    ang = jnp.concatenate([ang, ang], -1)
    sign = jnp.concatenate([-jnp.ones((HEAD_DIM // 2,), F32), jnp.ones((HEAD_DIM // 2,), F32)])
    reps = LANES // HEAD_DIM
    return jnp.tile(jnp.cos(ang), (1, reps)), jnp.tile(jnp.sin(ang) * sign, (1, reps))


def _layer(h2d, batch, seq, w_in, mu_prev, mu_next, w0, w2, a0, a2, g2, k_k, k_a, r_k, lnx_g, lnx_b,
           sink, w_out, ln1_g, ln1_b, peer_wq, peer_keys, peer_u, peer_v, ln2_g, ln2_b):
    w = RWKV_WIDTH
    row = lambda a: a.reshape(1, -1).astype(F32)
    w_in_p = jnp.pad(w_in, ((0, 0), (0, RWKV_COLS_PAD - RWKV_COLS))).astype(BF16)
    mup = jnp.pad(mu_prev, (0, RWKV_COLS_PAD - RWKV_COLS)).reshape(1, -1)
    mun = jnp.pad(mu_next, (0, RWKV_COLS_PAD - RWKV_COLS)).reshape(1, -1)
    wmix = jnp.zeros((LANES, 4 * w), F32)
    for d in range(2):
        wmix = wmix.at[d * DECAY_RANK:(d + 1) * DECAY_RANK, d * w:(d + 1) * w].set(w2[d])
        r0 = 2 * DECAY_RANK + d * ICLR_RANK
        wmix = wmix.at[r0:r0 + ICLR_RANK, (2 + d) * w:(3 + d) * w].set(a2[d])
    g2p = jnp.pad(g2, ((0, LANES - GATE_RANK), (0, 0)))
    head_of = np.arange(w) // HEAD_DIM
    bd = jnp.asarray(head_of[:, None] == head_of[None, :], F32)
    cos_t, sin_t = _rope_tables(seq)

    wq_b, keys_f, wout_b = peer_wq.astype(BF16), peer_keys.astype(F32), w_out.astype(BF16)

    def to_routing(x2d, nb):
        qkv, pr = _proj(x2d, w_in_p, cos_t, sin_t, seq)
        y_attn = _attention(qkv.reshape(nb, seq, ATTN_COLS), sink.astype(F32))
        r, v, kk, g, lw0, lw1, kd0, kd1, b0, b1 = _prep(
            pr, mup, mun, wmix, w0.astype(F32), a0.astype(F32), g2p, row(k_k), row(k_a), bd, seq)
        yf, yb = _wkv(r, v, kk, lw0, lw1, kd0, kd1, b0, b1, nb, seq)
        h1 = _mix(x2d, y_attn.reshape(nb * seq, ATTN_WIDTH), yf, yb, r, v, kd0, kd1, g,
                  row(lnx_g), row(lnx_b), row(r_k), bd, wout_b, row(ln1_g), row(ln1_b))
        return (h1,) + tuple(_route(h1, wq_b, keys_f))

    uv = jnp.concatenate([peer_u, peer_v], axis=1)
    l2g, l2b = row(ln2_g), row(ln2_b)

    nb_1 = batch * PEER_FIRST_GROUP_SEQ_8THS // 8
    t_1 = nb_1 * seq
    t_2 = (batch - nb_1) * seq
    n_sc = (batch * seq * PEER_SC_TOKEN_SHARE_128THS // 128) // PEER_TB * PEER_TB
    n_a = (t_2 * PEER_TC_FIRST_CALL_32NDS // 32) // PEER_TB * PEER_TB
    n_b = t_2 - n_a
    assert 0 < nb_1 < batch and 0 < n_sc < t_1 and n_sc % (SC_WORKERS * SUBLANES) == 0
    assert n_a > 0 and n_b > 0 and n_b % PEER_TB == 0 and (t_1 - n_sc) % PEER_TB == 0
    h1_1, idx_1, gate_t_1, gate_tok_1 = to_routing(h2d[:t_1], nb_1)
    zp = _sc_z(peer_u.astype(F32), idx_1, h1_1, 0, n_sc)
    h1_2, idx_2, gate_t_2, _ = to_routing(h2d[t_1:], batch - nb_1)
    out_tail = _peer(idx_1, gate_t_1, h1_1, uv, l2g, l2b, n_sc, t_1 - n_sc)
    out_a = _peer(idx_2, gate_t_2, h1_2, uv, l2g, l2b, 0, n_a)
    f_sc = _sc_out(peer_v.astype(F32), idx_1, _coef(zp, gate_tok_1, 0, out_tail, out_a), 0)
    out_b = _peer(idx_2, gate_t_2, h1_2, uv, l2g, l2b, n_a, n_b)
    out_sc = _ln_out(h1_1, f_sc, l2g, l2b, 0)
    return jnp.concatenate([out_sc, out_tail, out_a, out_b], axis=0)


def kernel(x, w_in, mu_prev, mu_next, w0, w2, a0, a2, g2, k_k, k_a, r_k, lnx_g, lnx_b, sink, w_out,
           ln1_g, ln1_b, peer_wq, peer_keys, peer_u, peer_v, ln2_g, ln2_b):
    batch, seq, d = x.shape
    h = x.reshape(batch * seq, d)
    for l in range(DEPTH):
        h = _layer(h, batch, seq, w_in[l], mu_prev[l], mu_next[l], w0[l], w2[l], a0[l], a2[l], g2[l],
                   k_k[l], k_a[l], r_k[l], lnx_g[l], lnx_b[l], sink[l], w_out[l], ln1_g[l], ln1_b[l],
                   peer_wq[l], peer_keys[l], peer_u[l], peer_v[l], ln2_g[l], ln2_b[l])
    return h.reshape(batch, seq, d)
```

```python
import functools

import numpy as np
import jax
import jax.numpy as jnp
from jax import lax
from jax.experimental import pallas as pl
from jax.experimental.pallas import tpu as pltpu
from jax.experimental.pallas import tpu_sc as plsc

F32 = jnp.float32
BF16 = jnp.bfloat16
HI = lax.Precision.HIGHEST

HEAD_DIM = 64
N_Q_HEADS = 8
N_KV_HEADS = 2
Q_PER_KV = N_Q_HEADS // N_KV_HEADS
ATTN_WIDTH = N_Q_HEADS * HEAD_DIM
ATTN_KV_WIDTH = N_KV_HEADS * HEAD_DIM
ATTN_COLS = ATTN_WIDTH + 2 * ATTN_KV_WIDTH
WINDOW = 128
ATTN_BLOCK = 128
ROPE_THETA = 10000.0
N_RWKV_HEADS = 8
RWKV_WIDTH = N_RWKV_HEADS * HEAD_DIM
DECAY_RANK = 32
ICLR_RANK = 32
GATE_RANK = 96
RWKV_COLS = 3 * RWKV_WIDTH + 2 * DECAY_RANK + 2 * ICLR_RANK + GATE_RANK
RWKV_COLS_PAD = 1792
RWKV_GN_EPS = 64e-5
PEER_HEADS = 8
PEER_NKEYS = 128
PEER_QDIM = 256
PEER_HALF = PEER_QDIM // 2
PEER_TOPK = 16
N_SEL = PEER_HEADS * PEER_TOPK
LN_EPS = 1e-5
DEPTH = 1
DEEPNORM_ALPHA = (2.0 * DEPTH) ** 0.25

LANES = 128
SUBLANES = 8
PREP_OUT_DTYPES = (BF16, BF16, BF16, F32, F32, F32, BF16, BF16, BF16, BF16)
WKV_CHUNK = 64
VMEM_LIMIT = 48 * 1024 * 1024
PEER_SLOTS = 6
PEER_TB = 256
ROUTE_HEADS_PER_STEP = 8
SC_LANES = 16
SC_WORKERS = 32
SC_GROUP = 32
SC_CHUNK_UNROLL = 8
SC_Z_ROWS = 8
SC_OUT_CHUNKS = 4
PEER_FIRST_GROUP_SEQ_8THS = 5
PEER_SC_TOKEN_SHARE_128THS = 74
PEER_TC_FIRST_CALL_32NDS = 6


def _cparams(sem):
    return pltpu.CompilerParams(dimension_semantics=sem, vmem_limit_bytes=VMEM_LIMIT)


def _dot(a, b, precision=None):
    return jnp.dot(a, b, preferred_element_type=F32, precision=precision)


def _dot_nt(a, b, precision=None):
    return lax.dot_general(a, b, (((1,), (1,)), ((), ())), preferred_element_type=F32,
                           precision=precision)


def _layer_norm(z, g, b):
    mu = jnp.mean(z, -1, keepdims=True)
    zc = z - mu
    var = jnp.mean(zc * zc, -1, keepdims=True)
    return zc * lax.rsqrt(var + LN_EPS) * g + b


def _proj_kernel(x_ref, w_ref, cos_ref, sin_ref, qkv_ref, pr_ref):
    xb = x_ref[...].astype(BF16)
    cos = cos_ref[...]
    sin = sin_ref[...]
    lane = lax.broadcasted_iota(jnp.int32, cos.shape, 1)
    first_half = (lane & (HEAD_DIM // 2)) == 0

    def rope(t):
        rot = jnp.where(first_half, pltpu.roll(t, LANES - HEAD_DIM // 2, 1),
                        pltpu.roll(t, HEAD_DIM // 2, 1))
        return t * cos + rot * sin

    for c in range(0, ATTN_COLS, 2 * LANES):
        acc = _dot(xb, w_ref[:, c:c + 2 * LANES])
        for half in range(2):
            col = c + half * LANES
            t = acc[:, half * LANES:(half + 1) * LANES]
            if col < ATTN_WIDTH + ATTN_KV_WIDTH:
                t = rope(t)
            qkv_ref[:, col:col + LANES] = t
    for c in range(0, RWKV_COLS_PAD, 2 * LANES):
        pr_ref[:, c:c + 2 * LANES] = _dot(xb, w_ref[:, ATTN_COLS + c:ATTN_COLS + c + 2 * LANES])


def _proj(x2, w_in_p, cos_t, sin_t, seq, tm=512):
    t, d = x2.shape
    n_pos = seq // tm
    return pl.pallas_call(
        _proj_kernel,
        grid=(t // tm,),
        in_specs=[
            pl.BlockSpec((tm, d), lambda i: (i, 0)),
            pl.BlockSpec(w_in_p.shape, lambda i: (0, 0)),
            pl.BlockSpec((tm, LANES), lambda i: (i % n_pos, 0)),
            pl.BlockSpec((tm, LANES), lambda i: (i % n_pos, 0)),
        ],
        out_specs=[
            pl.BlockSpec((tm, ATTN_COLS), lambda i: (i, 0)),
            pl.BlockSpec((tm, RWKV_COLS_PAD), lambda i: (i, 0)),
        ],
        out_shape=[
            jax.ShapeDtypeStruct((t, ATTN_COLS), F32),
            jax.ShapeDtypeStruct((t, RWKV_COLS_PAD), F32),
        ],
        compiler_params=_cparams(("parallel",)),
        name="proj",
    )(x2, w_in_p, cos_t, sin_t)


def _attn_kernel(sink_ref, q_ref, kp_ref, kc_ref, kn_ref, vp_ref, vc_ref, vn_ref, o_ref, *, nb):
    n = pl.program_id(1)
    blk = ATTN_BLOCK
    rows = Q_PER_KV * blk
    q = q_ref[0] * (HEAD_DIM ** -0.5)
    kwin = jnp.concatenate([kp_ref[0], kc_ref[0], kn_ref[0]], axis=0)
    vwin = jnp.concatenate([vp_ref[0], vc_ref[0], vn_ref[0]], axis=0)
    qi = lax.broadcasted_iota(jnp.int32, (rows, 3 * blk), 0) & (blk - 1)
    kj = lax.broadcasted_iota(jnp.int32, (rows, 3 * blk), 1)
    dist = kj - qi
    valid = ((dist >= blk - WINDOW) & (dist <= blk + WINDOW)
             & ((kj >= blk) | (n > 0)) & ((kj < 2 * blk) | (n < nb - 1)))
    rowg = lax.broadcasted_iota(jnp.int32, (rows, 1), 0) // blk
    outs = []
    for h in range(N_KV_HEADS):
        qs = jnp.concatenate(
            [q[:, (Q_PER_KV * h + g) * HEAD_DIM:(Q_PER_KV * h + g + 1) * HEAD_DIM]
             for g in range(Q_PER_KV)], axis=0)
        kh = kwin[:, h * HEAD_DIM:(h + 1) * HEAD_DIM]
        vh = vwin[:, h * HEAD_DIM:(h + 1) * HEAD_DIM]
        logits = _dot_nt(qs.astype(BF16), kh.astype(BF16))
        logits = jnp.where(valid, logits, -1e30)
        sk = jnp.zeros((rows, 1), F32)
        for g in range(Q_PER_KV):
            sk = jnp.where(rowg == g, sink_ref[Q_PER_KV * h + g], sk)
        m = jnp.maximum(jnp.max(logits, -1, keepdims=True), sk)
        e = jnp.exp(logits - m)
        den = jnp.sum(e, -1, keepdims=True) + jnp.exp(sk - m)
        p = e / den
        o = _dot(p.astype(BF16), vh.astype(BF16))
        for g in range(Q_PER_KV):
            outs.append(o[g * blk:(g + 1) * blk])
    o_ref[0] = jnp.concatenate(outs, axis=1)


def _attention(qkv3, sink):
    b, s, _ = qkv3.shape
    blk = ATTN_BLOCK
    nb = s // blk
    kcol = ATTN_WIDTH // LANES
    vcol = kcol + 1

    def spec(col, shift):
        def imap(bi, n):
            return (bi, jnp.clip(n + shift, 0, nb - 1), col)
        return pl.BlockSpec((1, blk, LANES), imap)

    return pl.pallas_call(
        functools.partial(_attn_kernel, nb=nb),
        grid=(b, nb),
        in_specs=[
            pl.BlockSpec(memory_space=pltpu.SMEM),
            pl.BlockSpec((1, blk, ATTN_WIDTH), lambda bi, n: (bi, n, 0)),
            spec(kcol, -1), spec(kcol, 0), spec(kcol, 1),
            spec(vcol, -1), spec(vcol, 0), spec(vcol, 1),
        ],
        out_specs=pl.BlockSpec((1, blk, ATTN_WIDTH), lambda bi, n: (bi, n, 0)),
        out_shape=jax.ShapeDtypeStruct((b, s, ATTN_WIDTH), F32),
        compiler_params=_cparams(("parallel", "parallel")),
        name="attn",
    )(sink, qkv3, qkv3, qkv3, qkv3, qkv3, qkv3, qkv3)


def _softplus(x):
    return jnp.maximum(x, 0.0) + jnp.log(1.0 + jnp.exp(-jnp.abs(x)))


def _sigmoid(x):
    return 1.0 / (1.0 + jnp.exp(-x))


def _prep_kernel(p_ref, hp_ref, hn_ref, mup_ref, mun_ref, wmix_ref, w0_ref, a0_ref, g2_ref,
                 kk_ref, ka_ref, bd_ref,
                 r_o, v_o, kk_o, g_o, lw0_o, lw1_o, kd0_o, kd1_o, b0_o, b1_o, *, tm, seq):
    i = pl.program_id(0)
    row = lax.broadcasted_iota(jnp.int32, (tm, 1), 0)
    seq_start = (i * tm) % seq == 0
    seq_end = ((i + 1) * tm) % seq == 0

    def shifted(c0, c1):
        p = p_ref[:, c0:c1]
        prev_row = jnp.where(seq_start, 0.0, hp_ref[SUBLANES - 1:SUBLANES, c0:c1])
        next_row = jnp.where(seq_end, 0.0, hn_ref[0:1, c0:c1])
        p_prev = jnp.where(row == 0, prev_row, pltpu.roll(p, 1, 0))
        p_next = jnp.where(row == tm - 1, next_row, pltpu.roll(p, tm - 1, 0))
        return p + mup_ref[:, c0:c1] * (p_prev - p) + mun_ref[:, c0:c1] * (p_next - p)

    w = RWKV_WIDTH
    r = shifted(0, w)
    k = shifted(w, 2 * w)
    v = shifted(2 * w, 3 * w)
    codes = shifted(3 * w, 3 * w + LANES)
    gd = shifted(3 * w + LANES, 3 * w + 2 * LANES)
    r_o[...] = r.astype(r_o.dtype)
    v_o[...] = v.astype(v_o.dtype)

    lane = lax.broadcasted_iota(jnp.int32, codes.shape, 1)
    codes = jnp.where(lane < 2 * DECAY_RANK, jnp.tanh(codes), codes)
    mm = _dot(codes, wmix_ref[...], HI)
    g_o[...] = _dot(_sigmoid(gd), g2_ref[...], HI)

    kkv = k * kk_ref[...]
    bd = bd_ref[...].astype(BF16)
    ss = sum(_dot(part, bd) for part in _split3(kkv * kkv))
    kkn = kkv * lax.rsqrt(jnp.maximum(ss, 1e-24))
    kk_o[...] = kkn.astype(kk_o.dtype)

    ka = ka_ref[...]
    for d, (lw_o, kd_o, b_o) in enumerate(((lw0_o, kd0_o, b0_o), (lw1_o, kd1_o, b1_o))):
        w_log = -_softplus(-(w0_ref[d:d + 1, :] + mm[:, d * w:(d + 1) * w])) - 0.5
        lw_o[...] = -jnp.exp(w_log)
        a = _sigmoid(a0_ref[d:d + 1, :] + mm[:, (2 + d) * w:(3 + d) * w])
        kd_o[...] = (k * (1.0 + (a - 1.0) * ka)).astype(kd_o.dtype)
        b_o[...] = (kkn * a).astype(b_o.dtype)


def _prep(pr, mup, mun, wmix, w0, a0, g2p, k_k, k_a, bd, seq, tm=256):
    t = pr.shape[0]
    nblk8 = t // SUBLANES
    per = tm // SUBLANES
    full = lambda a: pl.BlockSpec(a.shape, lambda i: (0,) * a.ndim)
    out = pl.BlockSpec((tm, RWKV_WIDTH), lambda i: (i, 0))
    return pl.pallas_call(
        functools.partial(_prep_kernel, tm=tm, seq=seq),
        grid=(t // tm,),
        in_specs=[
            pl.BlockSpec((tm, RWKV_COLS_PAD), lambda i: (i, 0)),
            pl.BlockSpec((SUBLANES, RWKV_COLS_PAD), lambda i: (jnp.maximum(i * per - 1, 0), 0)),
            pl.BlockSpec((SUBLANES, RWKV_COLS_PAD),
                         lambda i: (jnp.minimum((i + 1) * per, nblk8 - 1), 0)),
            full(mup), full(mun), full(wmix), full(w0), full(a0), full(g2p),
            full(k_k), full(k_a), full(bd),
        ],
        out_specs=[out] * 10,
        out_shape=[jax.ShapeDtypeStruct((t, RWKV_WIDTH), dt) for dt in PREP_OUT_DTYPES],
        compiler_params=_cparams(("parallel",)),
        name="prep",
    )(pr, pr, pr, mup, mun, wmix, w0, a0, g2p, k_k, k_a, bd)


def _split3(x):
    hi = x.astype(BF16)
    r1 = x - hi.astype(F32)
    mid = r1.astype(BF16)
    lo = (r1 - mid.astype(F32)).astype(BF16)
    return hi, mid, lo


def _wkv_kernel(rf, vf, kkf, lwf, kdf, bf, rb, vb, kkb, lwb, kdb, bb, yf_ref, yb_ref, state):
    c = pl.program_id(1)
    n = WKV_CHUNK
    hd = HEAD_DIM
    nh = N_RWKV_HEADS
    def bmm(a, b, ca, cb):
        return lax.dot_general(a.astype(BF16), b.astype(BF16), (((ca,), (cb,)), ((0,), (0,))),
                               preferred_element_type=F32)

    @pl.when(c == 0)
    def _():
        state[...] = jnp.zeros_like(state)

    ti = lax.broadcasted_iota(jnp.int32, (n, n), 0)
    si = lax.broadcasted_iota(jnp.int32, (n, n), 1)
    dirs = ((rf, vf, kkf, lwf, kdf, bf, si <= ti, si < ti, n - 1),
            (rb, vb, kkb, lwb, kdb, bb, si >= ti, si > ti, 0))
    heads = lambda x: jnp.stack([x[:, j * hd:(j + 1) * hd] for j in range(nh)], 0)
    parts = []
    for r_ref, v_ref, kk_ref, lw_ref, kd_ref, b_ref, incl, strict, last in dirs:
        lw = lw_ref[0]
        tri = incl.astype(BF16)
        cum = sum(_dot(tri, part) for part in _split3(lw))
        e_neg = jnp.exp(-cum)
        e_last = jnp.exp(cum[last:last + 1, :] - cum)
        g_scale = jnp.exp(cum[last:last + 1, :])
        parts.append(dict(
            at=heads((-kk_ref[0] * jnp.exp(cum - lw)).astype(BF16)),
            rt=heads((r_ref[0] * jnp.exp(cum)).astype(BF16)),
            bt=heads((b_ref[0] * e_neg).astype(BF16)),
            kt=heads((kd_ref[0] * e_neg).astype(BF16)),
            bl=heads((b_ref[0] * e_last).astype(BF16)),
            kl=heads((kd_ref[0] * e_last).astype(BF16)),
            v=heads(v_ref[0].astype(BF16)),
            gs=heads(g_scale),
            incl=jnp.broadcast_to(incl[None], (nh, n, n)),
            strict=jnp.broadcast_to(strict[None], (nh, n, n))))
    cat = lambda key: jnp.concatenate([parts[0][key], parts[1][key]], 0)
    at, rt, bt, kt, bl, kl, v, gs = (cat(k) for k in ("at", "rt", "bt", "kt", "bl", "kl", "v", "gs"))
    incl, strict = cat("incl"), cat("strict")
    g0 = state[...]
    g0b = g0.astype(BF16)
    m1 = bmm(jnp.concatenate([at, rt], 1), jnp.concatenate([bt, kt], 1), 2, 2)
    a_ab = jnp.where(strict, m1[:, :n, :n], 0.0)
    a_ak = jnp.where(strict, m1[:, :n, n:], 0.0)
    a_rb = jnp.where(incl, m1[:, n:, :n], 0.0)
    a_rk = jnp.where(incl, m1[:, n:, n:], 0.0)
    tinv = jnp.where((ti == si)[None], 1.0, a_ab)
    pw = a_ab
    for _ in range(int(np.log2(n)) - 1):
        pw = bmm(pw, pw, 2, 1)
        tinv = tinv + bmm(tinv, pw, 2, 1)
    rhs = bmm(a_ak, v, 2, 1) + bmm(at, g0b, 2, 2)
    u = bmm(tinv, rhs, 2, 1)
    uv = jnp.concatenate([u.astype(BF16), v], 1)
    y = bmm(jnp.concatenate([a_rb, a_rk], 2), uv, 2, 1) + bmm(rt, g0b, 2, 2)
    state[...] = g0 * gs + bmm(uv, jnp.concatenate([bl, kl], 1), 1, 1)
    for d, y_ref in enumerate((yf_ref, yb_ref)):
        y_ref[0] = jnp.concatenate([y[d * nh + j] for j in range(nh)], axis=1)


def _wkv(r, v, kk, lw0, lw1, kd0, kd1, b0, b1, batch, seq):
    n = WKV_CHUNK
    nc = seq // n
    shp = (batch, seq, RWKV_WIDTH)
    arrs = [a.reshape(shp) for a in (r, v, kk, lw0, kd0, b0, r, v, kk, lw1, kd1, b1)]
    fwd = pl.BlockSpec((1, n, RWKV_WIDTH), lambda b, c: (b, c, 0))
    bwd = pl.BlockSpec((1, n, RWKV_WIDTH), lambda b, c: (b, nc - 1 - c, 0))
    yf, yb = pl.pallas_call(
        _wkv_kernel,
        grid=(batch, nc),
        in_specs=[fwd] * 6 + [bwd] * 6,
        out_specs=[fwd, bwd],
        out_shape=[jax.ShapeDtypeStruct(shp, F32)] * 2,
        scratch_shapes=[pltpu.VMEM((2 * N_RWKV_HEADS, HEAD_DIM, HEAD_DIM), F32)],
        compiler_params=_cparams(("parallel", "arbitrary")),
        name="wkv",
    )(*arrs)
    return yf.reshape(batch * seq, RWKV_WIDTH), yb.reshape(batch * seq, RWKV_WIDTH)


def _mix_kernel(x_ref, ya_ref, yf_ref, yb_ref, r_ref, v_ref, kd0_ref, kd1_ref, g_ref,
                lng_ref, lnb_ref, rk_ref, bd_ref, wout_ref, l1g_ref, l1b_ref, h_ref):
    bd = bd_ref[...].astype(BF16)
    head_sum = lambda t: sum(_dot(part, bd) for part in _split3(t))
    inv = 1.0 / HEAD_DIM
    y = yf_ref[...] + yb_ref[...]
    mu = head_sum(y) * inv
    yc = y - mu
    var = head_sum(yc * yc) * inv
    yn = yc * lax.rsqrt(var + RWKV_GN_EPS) * lng_ref[...] + lnb_ref[...]
    k_mean = 0.5 * (kd0_ref[...] + kd1_ref[...])
    v = v_ref[...]
    bonus = head_sum(r_ref[...] * k_mean * rk_ref[...]) * v
    yr = (yn + bonus) * g_ref[...]
    mix = (_dot(ya_ref[...].astype(BF16), wout_ref[:ATTN_WIDTH, :])
           + _dot(yr.astype(BF16), wout_ref[ATTN_WIDTH:, :]))
    h_ref[...] = _layer_norm(DEEPNORM_ALPHA * x_ref[...] + mix, l1g_ref[...], l1b_ref[...])


def _mix(x2, ya, yf, yb, r, v, kd0, kd1, g, lng, lnb, rk, bd, wout, l1g, l1b, tm=256):
    t, d = x2.shape
    full = lambda a: pl.BlockSpec(a.shape, lambda i: (0,) * a.ndim)
    half = pl.BlockSpec((tm, RWKV_WIDTH), lambda i: (i, 0))
    wide = pl.BlockSpec((tm, d), lambda i: (i, 0))
    return pl.pallas_call(
        _mix_kernel,
        grid=(t // tm,),
        in_specs=[wide] + [half] * 8 + [full(lng), full(lnb), full(rk), full(bd), full(wout),
                                        full(l1g), full(l1b)],
        out_specs=wide,
        out_shape=jax.ShapeDtypeStruct((t, d), F32),
        compiler_params=_cparams(("parallel",)),
        name="mix",
    )(x2, ya, yf, yb, r, v, kd0, kd1, g, lng, lnb, rk, bd, wout, l1g, l1b)


def _top_rows(scs, k, payloads=None):
    n = scs[0].shape[0]
    iota = lax.broadcasted_iota(jnp.int32, scs[0].shape, 0).astype(F32)
    scs = list(scs)
    vals = [[] for _ in scs]
    picks = [[] for _ in scs]
    for _ in range(k):
        for a, sc in enumerate(scs):
            m = jnp.max(sc, axis=0, keepdims=True)
            pos = jnp.min(jnp.where(sc == m, iota, float(n)), axis=0, keepdims=True)
            hit = iota == pos
            vals[a].append(m)
            if payloads is None:
                picks[a].append(pos)
            else:
                picks[a].append(jnp.max(jnp.where(hit, payloads[a], -1.0), axis=0, keepdims=True))
            scs[a] = jnp.where(hit, -jnp.inf, sc)
    return [(jnp.concatenate(v, 0), jnp.concatenate(p, 0)) for v, p in zip(vals, picks)]


def _route_kernel(h_ref, wq_ref, keys_ref, idx_ref, gate_ref, gate_tok_ref, q_scr, idx_scr):
    k = PEER_TOPK
    q_scr[...] = _dot(h_ref[...].astype(BF16), wq_ref[...])

    def head_group(hg, carry):
        heads = [hg * ROUTE_HEADS_PER_STEP + i for i in range(ROUTE_HEADS_PER_STEP)]
        scores = []
        for hh in heads:
            for p in range(2):
                col = pl.multiple_of(hh * PEER_QDIM + p * PEER_HALF, PEER_HALF)
                qp = q_scr[:, pl.ds(col, PEER_HALF)]
                scores.append(_dot_nt(keys_ref[hh, p], qp, HI))
        tops = _top_rows(scores, k)
        cands, cand_ids = [], []
        for a in range(len(heads)):
            (s1, i1), (s2, i2) = tops[2 * a], tops[2 * a + 1]
            cands.append(jnp.concatenate([s1[i:i + 1] + s2[:k // (i + 1)] for i in range(k)], 0))
            cand_ids.append(jnp.concatenate(
                [i1[i:i + 1] * PEER_NKEYS + i2[:k // (i + 1)] for i in range(k)], 0))
        for hh, (cs, ids) in zip(heads, _top_rows(cands, k, cand_ids)):
            e = jnp.exp(cs - jnp.max(cs, axis=0, keepdims=True))
            row = pl.multiple_of(hh * k, k)
            gate_ref[pl.ds(row, k), :] = e / jnp.sum(e, axis=0, keepdims=True)
            idx_scr[pl.ds(row, k), :] = ids.astype(jnp.int32)
        return carry

    lax.fori_loop(0, PEER_HEADS // ROUTE_HEADS_PER_STEP, head_group, 0)
    idx_ref[...] = idx_scr[...].T
    gate_tok_ref[...] = gate_ref[...].T


def _route(h1, wq, keys, tm=256):
    t, d = h1.shape
    return pl.pallas_call(
        _route_kernel,
        grid=(t // tm,),
        in_specs=[
            pl.BlockSpec((tm, d), lambda i: (i, 0)),
            pl.BlockSpec(wq.shape, lambda i: (0, 0)),
            pl.BlockSpec(keys.shape, lambda i: (0, 0, 0, 0)),
        ],
        out_specs=[
            pl.BlockSpec((tm, N_SEL), lambda i: (i, 0)),
            pl.BlockSpec((N_SEL, tm), lambda i: (0, i)),
            pl.BlockSpec((tm, N_SEL), lambda i: (i, 0)),
        ],
        out_shape=[
            jax.ShapeDtypeStruct((t, N_SEL), jnp.int32),
            jax.ShapeDtypeStruct((N_SEL, t), F32),
            jax.ShapeDtypeStruct((t, N_SEL), F32),
        ],
        scratch_shapes=[pltpu.VMEM((tm, PEER_HEADS * PEER_QDIM), F32),
                        pltpu.VMEM((N_SEL, tm), jnp.int32)],
        compiler_params=_cparams(("parallel",)),
        name="route",
    )(h1, wq, keys)


def _peer_kernel(idx_ref, gate_ref, h_ref, uv_hbm, l2g_ref, l2b_ref, o_ref, buf, f_scr, sem, *, tb):
    d = h_ref.shape[1]

    def issue(t, slot):
        for j in range(N_SEL):
            pltpu.make_async_copy(uv_hbm.at[pl.ds(idx_ref[t, j], 1)], buf.at[slot, pl.ds(j, 1)],
                                  sem.at[slot]).start()

    def wait_all(slot):
        pltpu.make_async_copy(uv_hbm.at[pl.ds(0, N_SEL)], buf.at[slot], sem.at[slot]).wait()

    for t_ahead in range(PEER_SLOTS - 1):
        issue(t_ahead, t_ahead)
    lane = lax.broadcasted_iota(jnp.int32, (N_SEL, tb), 1)

    def body(t, carry):
        slot = t % PEER_SLOTS

        @pl.when(t + PEER_SLOTS - 1 < tb)
        def _():
            issue(t + PEER_SLOTS - 1, (t + PEER_SLOTS - 1) % PEER_SLOTS)

        wait_all(slot)
        z = jnp.sum(buf[slot, :, 0:d] * h_ref[pl.ds(t, 1), :], axis=1, keepdims=True)
        gcol = jnp.sum(jnp.where(lane == t, gate_ref[...], 0.0), axis=1, keepdims=True)
        coef = gcol * (0.5 * z * (1.0 + lax.erf(z * (2.0 ** -0.5))))
        f_scr[pl.ds(t, 1), :] = jnp.sum(coef * buf[slot, :, d:2 * d], axis=0, keepdims=True)
        return carry

    lax.fori_loop(0, tb, body, 0)
    o_ref[...] = _layer_norm(DEEPNORM_ALPHA * h_ref[...] + f_scr[...], l2g_ref[...], l2b_ref[...])


def _peer(idx, gate_t, h1, uv, l2g, l2b, tok0, ntok, tb=PEER_TB):
    d = h1.shape[1]
    b0 = tok0 // tb
    return pl.pallas_call(
        functools.partial(_peer_kernel, tb=tb),
        grid=(ntok // tb,),
        in_specs=[
            pl.BlockSpec((tb, N_SEL), lambda i: (b0 + i, 0), memory_space=pltpu.SMEM),
            pl.BlockSpec((N_SEL, tb), lambda i: (0, b0 + i)),
            pl.BlockSpec((tb, d), lambda i: (b0 + i, 0)),
            pl.BlockSpec(memory_space=pl.ANY),
            pl.BlockSpec(l2g.shape, lambda i: (0, 0)),
            pl.BlockSpec(l2b.shape, lambda i: (0, 0)),
        ],
        out_specs=pl.BlockSpec((tb, d), lambda i: (i, 0)),
        out_shape=jax.ShapeDtypeStruct((ntok, d), F32),
        scratch_shapes=[pltpu.VMEM((PEER_SLOTS, N_SEL, 2 * d), F32),
                        pltpu.VMEM((tb, d), F32),
                        pltpu.SemaphoreType.DMA((PEER_SLOTS,))],
        compiler_params=_cparams(("arbitrary",)),
        name="peer",
    )(idx, gate_t, h1, uv, l2g, l2b)


def _sc_mesh():
    return plsc.VectorSubcoreMesh(core_axis_name="c", subcore_axis_name="s")


def _worker_id():
    return lax.axis_index("s") * 2 + lax.axis_index("c")


def _sc_z_body(u_hbm, idx_hbm, h_hbm, z_hbm, idx_v, h_v, rows_v, z_v, sem, *, tpw, d, tok0):
    ngrp = N_SEL // SC_GROUP
    nchunk = d // SC_LANES
    nsteps = tpw * ngrp
    base = _worker_id() * tpw

    def gather(tok_buf, g, buf):
        return pltpu.make_async_copy(u_hbm.at[idx_v.at[tok_buf, g]], rows_v.at[buf], sem.at[buf])

    pltpu.sync_copy(idx_hbm.at[tok0 + base], idx_v.at[0])
    gather(0, 0, 0).start()

    @pl.loop(0, nsteps)
    def _(q):
        t_loc = q // ngrp
        g = q % ngrp
        buf = q % 2
        t = base + t_loc

        @pl.when(g == 0)
        def _():
            pltpu.sync_copy(h_hbm.at[tok0 + t], h_v)

            @pl.loop(0, N_SEL)
            def _(r):
                z_v[pl.ds(pl.multiple_of(r * SC_LANES, SC_LANES), SC_LANES)] = jnp.zeros((SC_LANES,), F32)

        @pl.when(q + 1 < nsteps)
        def _():
            tn = (q + 1) // ngrp
            gn = (q + 1) % ngrp

            @pl.when(gn == 0)
            def _():
                pltpu.sync_copy(idx_hbm.at[tok0 + base + tn], idx_v.at[tn % 2])

            gather(tn % 2, gn, 1 - buf).start()

        gather(t_loc % 2, g, buf).wait()

        @pl.loop(0, SC_GROUP, step=SC_Z_ROWS)
        def _(rb):
            @pl.loop(0, nchunk, step=SC_CHUNK_UNROLL)
            def _(c0):
                accs = [jnp.zeros((SC_LANES,), F32) for _ in range(SC_Z_ROWS)]
                for cc in range(SC_CHUNK_UNROLL):
                    sl = pl.ds(pl.multiple_of((c0 + cc) * SC_LANES, SC_LANES), SC_LANES)
                    hc = h_v[sl]
                    for i in range(SC_Z_ROWS):
                        accs[i] = accs[i] + rows_v[buf, rb + i, sl] * hc
                for i in range(SC_Z_ROWS):
                    row = pl.multiple_of((g * SC_GROUP + rb + i) * SC_LANES, SC_LANES)
                    plsc.addupdate(z_v.at[pl.ds(row, SC_LANES)], accs[i])

        @pl.when(g == ngrp - 1)
        def _():
            pltpu.sync_copy(z_v, z_hbm.at[t])


def _sc_z(u, idx, h, tok0, ntok):
    t, d = h.shape
    tpw = ntok // SC_WORKERS
    idx4 = idx.reshape(t, N_SEL // SC_GROUP, SC_GROUP)
    body = functools.partial(_sc_z_body, tpw=tpw, d=d, tok0=tok0)
    return pl.kernel(
        body,
        out_type=jax.ShapeDtypeStruct((ntok, N_SEL * SC_LANES), F32),
        mesh=_sc_mesh(),
        scratch_types=[
            pltpu.VMEM((2, N_SEL // SC_GROUP, SC_GROUP), jnp.int32),
            pltpu.VMEM((d,), F32),
            pltpu.VMEM((2, SC_GROUP, d), F32),
            pltpu.VMEM((N_SEL * SC_LANES,), F32),
            pltpu.SemaphoreType.DMA((2,)),
        ],
        name="sc_z",
    )(u, idx4, h)


def _sc_out_body(v_hbm, idx_hbm, coef_hbm, f_hbm, idx_v, coef_v, rows_v, out_v, sem, *, tpw, d, tok0):
    ngrp = N_SEL // SC_GROUP
    nchunk = d // SC_LANES
    nsteps = tpw * ngrp
    base = _worker_id() * tpw

    def gather(tok_buf, g, buf):
        return pltpu.make_async_copy(v_hbm.at[idx_v.at[tok_buf, g]], rows_v.at[buf], sem.at[buf])

    pltpu.sync_copy(idx_hbm.at[tok0 + base], idx_v.at[0])
    gather(0, 0, 0).start()

    @pl.loop(0, nsteps)
    def _(q):
        t_loc = q // ngrp
        g = q % ngrp
        buf = q % 2
        t = base + t_loc

        @pl.when(g == 0)
        def _():
            pltpu.sync_copy(coef_hbm.at[t], coef_v)

        @pl.when(q + 1 < nsteps)
        def _():
            tn = (q + 1) // ngrp
            gn = (q + 1) % ngrp

            @pl.when(gn == 0)
            def _():
                pltpu.sync_copy(idx_hbm.at[tok0 + base + tn], idx_v.at[tn % 2])

            gather(tn % 2, gn, 1 - buf).start()

        gather(t_loc % 2, g, buf).wait()

        @pl.loop(0, nchunk, step=SC_OUT_CHUNKS)
        def _(c0):
            sls = [pl.ds(pl.multiple_of((c0 + cc) * SC_LANES, SC_LANES), SC_LANES)
                   for cc in range(SC_OUT_CHUNKS)]
            accs = [None] * SC_OUT_CHUNKS
            for r in range(SC_GROUP):
                cf = coef_v[pl.ds(pl.multiple_of((g * SC_GROUP + r) * SC_LANES, SC_LANES), SC_LANES)]
                for cc in range(SC_OUT_CHUNKS):
                    p = rows_v[buf, r, sls[cc]] * cf
                    accs[cc] = p if accs[cc] is None else accs[cc] + p
            for cc in range(SC_OUT_CHUNKS):
                @pl.when(g == 0)
                def _():
                    out_v[sls[cc]] = accs[cc]

                @pl.when(g != 0)
                def _():
                    plsc.addupdate(out_v.at[sls[cc]], accs[cc])

        @pl.when(g == ngrp - 1)
        def _():
            pltpu.sync_copy(out_v, f_hbm.at[t])


def _sc_out(v, idx, coef16, tok0):
    t = idx.shape[0]
    ntok = coef16.shape[0]
    d = v.shape[1]
    tpw = ntok // SC_WORKERS
    idx4 = idx.reshape(t, N_SEL // SC_GROUP, SC_GROUP)
    body = functools.partial(_sc_out_body, tpw=tpw, d=d, tok0=tok0)
    return pl.kernel(
        body,
        out_type=jax.ShapeDtypeStruct((ntok, d), F32),
        mesh=_sc_mesh(),
        scratch_types=[
            pltpu.VMEM((2, N_SEL // SC_GROUP, SC_GROUP), jnp.int32),
            pltpu.VMEM((N_SEL * SC_LANES,), F32),
            pltpu.VMEM((2, SC_GROUP, d), F32),
            pltpu.VMEM((d,), F32),
            pltpu.SemaphoreType.DMA((2,)),
        ],
        name="sc_out",
    )(v, idx4, coef16)


def _coef_kernel(zp_ref, gate_ref, sel_ref, selt_ref, after_a_ref, after_b_ref, o_ref):
    z = sum(_dot(part, sel_ref[...]) for part in _split3(zp_ref[...]))
    coef = gate_ref[...] * (0.5 * z * (1.0 + lax.erf(z * (2.0 ** -0.5))))
    o_ref[...] = sum(_dot(part, selt_ref[...]) for part in _split3(coef))


def _coef(zp, gate_tok, tok0, after_a, after_b, tm=256):
    ntok, wide = zp.shape
    b0 = tok0 // tm
    grp = np.arange(wide) // SC_LANES
    sel = jnp.asarray(grp[:, None] == np.arange(N_SEL)[None, :], BF16)
    return pl.pallas_call(
        _coef_kernel,
        grid=(ntok // tm,),
        in_specs=[
            pl.BlockSpec((tm, wide), lambda i: (i, 0)),
            pl.BlockSpec((tm, N_SEL), lambda i: (b0 + i, 0)),
            pl.BlockSpec(sel.shape, lambda i: (0, 0)),
            pl.BlockSpec(sel.shape[::-1], lambda i: (0, 0)),
            pl.BlockSpec((SUBLANES, LANES), lambda i: (0, 0)),
            pl.BlockSpec((SUBLANES, LANES), lambda i: (0, 0)),
        ],
        out_specs=pl.BlockSpec((tm, wide), lambda i: (i, 0)),
        out_shape=jax.ShapeDtypeStruct((ntok, wide), F32),
        compiler_params=_cparams(("parallel",)),
        name="coef",
    )(zp, gate_tok, sel, sel.T, after_a, after_b)


def _ln_out_kernel(h_ref, f_ref, g_ref, b_ref, o_ref):
    o_ref[...] = _layer_norm(DEEPNORM_ALPHA * h_ref[...] + f_ref[...], g_ref[...], b_ref[...])


def _ln_out(h1, f, l2g, l2b, tok0, tm=256):
    ntok, d = f.shape
    b0 = tok0 // tm
    return pl.pallas_call(
        _ln_out_kernel,
        grid=(ntok // tm,),
        in_specs=[
            pl.BlockSpec((tm, d), lambda i: (b0 + i, 0)),
            pl.BlockSpec((tm, d), lambda i: (i, 0)),
            pl.BlockSpec(l2g.shape, lambda i: (0, 0)),
            pl.BlockSpec(l2b.shape, lambda i: (0, 0)),
        ],
        out_specs=pl.BlockSpec((tm, d), lambda i: (i, 0)),
        out_shape=jax.ShapeDtypeStruct((ntok, d), F32),
        compiler_params=_cparams(("parallel",)),
        name="ln_out",
    )(h1, f, l2g, l2b)


def _rope_tables(seq):
    pos = jnp.arange(seq, dtype=F32)
    inv_freq = ROPE_THETA ** (-jnp.arange(0, HEAD_DIM, 2, dtype=F32) / HEAD_DIM)
    ang = pos[:, None] * inv_freq[None, :]
    ang = jnp.concatenate([ang, ang], -1)
    sign = jnp.concatenate([-jnp.ones((HEAD_DIM // 2,), F32), jnp.ones((HEAD_DIM // 2,), F32)])
    reps = LANES // HEAD_DIM
    return jnp.tile(jnp.cos(ang), (1, reps)), jnp.tile(jnp.sin(ang) * sign, (1, reps))


def _layer(h2d, batch, seq, w_in, mu_prev, mu_next, w0, w2, a0, a2, g2, k_k, k_a, r_k, lnx_g, lnx_b,
           sink, w_out, ln1_g, ln1_b, peer_wq, peer_keys, peer_u, peer_v, ln2_g, ln2_b):
    w = RWKV_WIDTH
    row = lambda a: a.reshape(1, -1).astype(F32)
    w_in_p = jnp.pad(w_in, ((0, 0), (0, RWKV_COLS_PAD - RWKV_COLS))).astype(BF16)
    mup = jnp.pad(mu_prev, (0, RWKV_COLS_PAD - RWKV_COLS)).reshape(1, -1)
    mun = jnp.pad(mu_next, (0, RWKV_COLS_PAD - RWKV_COLS)).reshape(1, -1)
    wmix = jnp.zeros((LANES, 4 * w), F32)
    for d in range(2):
        wmix = wmix.at[d * DECAY_RANK:(d + 1) * DECAY_RANK, d * w:(d + 1) * w].set(w2[d])
        r0 = 2 * DECAY_RANK + d * ICLR_RANK
        wmix = wmix.at[r0:r0 + ICLR_RANK, (2 + d) * w:(3 + d) * w].set(a2[d])
    g2p = jnp.pad(g2, ((0, LANES - GATE_RANK), (0, 0)))
    head_of = np.arange(w) // HEAD_DIM
    bd = jnp.asarray(head_of[:, None] == head_of[None, :], F32)
    cos_t, sin_t = _rope_tables(seq)

    wq_b, keys_f, wout_b = peer_wq.astype(BF16), peer_keys.astype(F32), w_out.astype(BF16)

    def to_routing(x2d, nb):
        qkv, pr = _proj(x2d, w_in_p, cos_t, sin_t, seq)
        y_attn = _attention(qkv.reshape(nb, seq, ATTN_COLS), sink.astype(F32))
        r, v, kk, g, lw0, lw1, kd0, kd1, b0, b1 = _prep(
            pr, mup, mun, wmix, w0.astype(F32), a0.astype(F32), g2p, row(k_k), row(k_a), bd, seq)
        yf, yb = _wkv(r, v, kk, lw0, lw1, kd0, kd1, b0, b1, nb, seq)
        h1 = _mix(x2d, y_attn.reshape(nb * seq, ATTN_WIDTH), yf, yb, r, v, kd0, kd1, g,
                  row(lnx_g), row(lnx_b), row(r_k), bd, wout_b, row(ln1_g), row(ln1_b))
        return (h1,) + tuple(_route(h1, wq_b, keys_f))

    uv = jnp.concatenate([peer_u, peer_v], axis=1)
    l2g, l2b = row(ln2_g), row(ln2_b)

    nb_1 = batch * PEER_FIRST_GROUP_SEQ_8THS // 8
    t_1 = nb_1 * seq
    t_2 = (batch - nb_1) * seq
    n_sc = (batch * seq * PEER_SC_TOKEN_SHARE_128THS // 128) // PEER_TB * PEER_TB
    n_a = (t_2 * PEER_TC_FIRST_CALL_32NDS // 32) // PEER_TB * PEER_TB
    n_b = t_2 - n_a
    assert 0 < nb_1 < batch and 0 < n_sc < t_1 and n_sc % (SC_WORKERS * SUBLANES) == 0
    assert n_a > 0 and n_b > 0 and n_b % PEER_TB == 0 and (t_1 - n_sc) % PEER_TB == 0
    h1_1, idx_1, gate_t_1, gate_tok_1 = to_routing(h2d[:t_1], nb_1)
    zp = _sc_z(peer_u.astype(F32), idx_1, h1_1, 0, n_sc)
    h1_2, idx_2, gate_t_2, _ = to_routing(h2d[t_1:], batch - nb_1)
    out_tail = _peer(idx_1, gate_t_1, h1_1, uv, l2g, l2b, n_sc, t_1 - n_sc)
    out_a = _peer(idx_2, gate_t_2, h1_2, uv, l2g, l2b, 0, n_a)
    f_sc = _sc_out(peer_v.astype(F32), idx_1, _coef(zp, gate_tok_1, 0, out_tail, out_a), 0)
    out_b = _peer(idx_2, gate_t_2, h1_2, uv, l2g, l2b, n_a, n_b)
    out_sc = _ln_out(h1_1, f_sc, l2g, l2b, 0)
    return jnp.concatenate([out_sc, out_tail, out_a, out_b], axis=0)


def kernel(x, w_in, mu_prev, mu_next, w0, w2, a0, a2, g2, k_k, k_a, r_k, lnx_g, lnx_b, sink, w_out,
           ln1_g, ln1_b, peer_wq, peer_keys, peer_u, peer_v, ln2_g, ln2_b):
    batch, seq, d = x.shape
    h = x.reshape(batch * seq, d)
    for l in range(DEPTH):
        h = _layer(h, batch, seq, w_in[l], mu_prev[l], mu_next[l], w0[l], w2[l], a0[l], a2[l], g2[l],
                   k_k[l], k_a[l], r_k[l], lnx_g[l], lnx_b[l], sink[l], w_out[l], ln1_g[l], ln1_b[l],
                   peer_wq[l], peer_keys[l], peer_u[l], peer_v[l], ln2_g[l], ln2_b[l])
    return h.reshape(batch, seq, d)
```

```python
import functools

import numpy as np
import jax
import jax.numpy as jnp
from jax import lax
from jax.experimental import pallas as pl
from jax.experimental.pallas import tpu as pltpu
from jax.experimental.pallas import tpu_sc as plsc

F32 = jnp.float32
BF16 = jnp.bfloat16
HI = lax.Precision.HIGHEST

HEAD_DIM = 64
N_Q_HEADS = 8
N_KV_HEADS = 2
Q_PER_KV = N_Q_HEADS // N_KV_HEADS
ATTN_WIDTH = N_Q_HEADS * HEAD_DIM
ATTN_KV_WIDTH = N_KV_HEADS * HEAD_DIM
ATTN_COLS = ATTN_WIDTH + 2 * ATTN_KV_WIDTH
WINDOW = 128
ATTN_BLOCK = 128
ROPE_THETA = 10000.0
N_RWKV_HEADS = 8
RWKV_WIDTH = N_RWKV_HEADS * HEAD_DIM
DECAY_RANK = 32
ICLR_RANK = 32
GATE_RANK = 96
RWKV_COLS = 3 * RWKV_WIDTH + 2 * DECAY_RANK + 2 * ICLR_RANK + GATE_RANK
RWKV_COLS_PAD = 1792
RWKV_GN_EPS = 64e-5
PEER_HEADS = 8
PEER_NKEYS = 128
PEER_QDIM = 256
PEER_HALF = PEER_QDIM // 2
PEER_TOPK = 16
N_SEL = PEER_HEADS * PEER_TOPK
LN_EPS = 1e-5
DEPTH = 1
DEEPNORM_ALPHA = (2.0 * DEPTH) ** 0.25

LANES = 128
SUBLANES = 8
PREP_OUT_DTYPES = (BF16, BF16, BF16, F32, F32, F32, BF16, BF16, BF16, BF16)
WKV_CHUNK = 64
VMEM_LIMIT = 48 * 1024 * 1024
PEER_SLOTS = 6
PEER_TB = 256
ROUTE_HEADS_PER_STEP = 8
SC_LANES = 16
SC_WORKERS = 32
SC_GROUP = 32
SC_CHUNK_UNROLL = 8
SC_Z_ROWS = 8
SC_OUT_CHUNKS = 4
PEER_FIRST_GROUP_SEQ_8THS = 5
PEER_SC_TOKEN_SHARE_128THS = 70
PEER_TC_FIRST_CALL_32NDS = 7


def _cparams(sem):
    return pltpu.CompilerParams(dimension_semantics=sem, vmem_limit_bytes=VMEM_LIMIT)


def _dot(a, b, precision=None):
    return jnp.dot(a, b, preferred_element_type=F32, precision=precision)


def _dot_nt(a, b, precision=None):
    return lax.dot_general(a, b, (((1,), (1,)), ((), ())), preferred_element_type=F32,
                           precision=precision)


def _layer_norm(z, g, b):
    mu = jnp.mean(z, -1, keepdims=True)
    zc = z - mu
    var = jnp.mean(zc * zc, -1, keepdims=True)
    return zc * lax.rsqrt(var + LN_EPS) * g + b


def _proj_kernel(x_ref, w_ref, cos_ref, sin_ref, qkv_ref, pr_ref):
    xb = x_ref[...].astype(BF16)
    cos = cos_ref[...]
    sin = sin_ref[...]
    lane = lax.broadcasted_iota(jnp.int32, cos.shape, 1)
    first_half = (lane & (HEAD_DIM // 2)) == 0

    def rope(t):
        rot = jnp.where(first_half, pltpu.roll(t, LANES - HEAD_DIM // 2, 1),
                        pltpu.roll(t, HEAD_DIM // 2, 1))
        return t * cos + rot * sin

    for c in range(0, ATTN_COLS, 2 * LANES):
        acc = _dot(xb, w_ref[:, c:c + 2 * LANES])
        for half in range(2):
            col = c + half * LANES
            t = acc[:, half * LANES:(half + 1) * LANES]
            if col < ATTN_WIDTH + ATTN_KV_WIDTH:
                t = rope(t)
            qkv_ref[:, col:col + LANES] = t
    for c in range(0, RWKV_COLS_PAD, 2 * LANES):
        pr_ref[:, c:c + 2 * LANES] = _dot(xb, w_ref[:, ATTN_COLS + c:ATTN_COLS + c + 2 * LANES])


def _proj(x2, w_in_p, cos_t, sin_t, seq, tm=512):
    t, d = x2.shape
    n_pos = seq // tm
    return pl.pallas_call(
        _proj_kernel,
        grid=(t // tm,),
        in_specs=[
            pl.BlockSpec((tm, d), lambda i: (i, 0)),
            pl.BlockSpec(w_in_p.shape, lambda i: (0, 0)),
            pl.BlockSpec((tm, LANES), lambda i: (i % n_pos, 0)),
            pl.BlockSpec((tm, LANES), lambda i: (i % n_pos, 0)),
        ],
        out_specs=[
            pl.BlockSpec((tm, ATTN_COLS), lambda i: (i, 0)),
            pl.BlockSpec((tm, RWKV_COLS_PAD), lambda i: (i, 0)),
        ],
        out_shape=[
            jax.ShapeDtypeStruct((t, ATTN_COLS), F32),
            jax.ShapeDtypeStruct((t, RWKV_COLS_PAD), F32),
        ],
        compiler_params=_cparams(("parallel",)),
        name="proj",
    )(x2, w_in_p, cos_t, sin_t)


def _attn_kernel(sink_ref, q_ref, kp_ref, kc_ref, kn_ref, vp_ref, vc_ref, vn_ref, o_ref, *, nb):
    n = pl.program_id(1)
    blk = ATTN_BLOCK
    rows = Q_PER_KV * blk
    q = q_ref[0] * (HEAD_DIM ** -0.5)
    kwin = jnp.concatenate([kp_ref[0], kc_ref[0], kn_ref[0]], axis=0)
    vwin = jnp.concatenate([vp_ref[0], vc_ref[0], vn_ref[0]], axis=0)
    qi = lax.broadcasted_iota(jnp.int32, (rows, 3 * blk), 0) & (blk - 1)
    kj = lax.broadcasted_iota(jnp.int32, (rows, 3 * blk), 1)
    dist = kj - qi
    valid = ((dist >= blk - WINDOW) & (dist <= blk + WINDOW)
             & ((kj >= blk) | (n > 0)) & ((kj < 2 * blk) | (n < nb - 1)))
    rowg = lax.broadcasted_iota(jnp.int32, (rows, 1), 0) // blk
    outs = []
    for h in range(N_KV_HEADS):
        qs = jnp.concatenate(
            [q[:, (Q_PER_KV * h + g) * HEAD_DIM:(Q_PER_KV * h + g + 1) * HEAD_DIM]
             for g in range(Q_PER_KV)], axis=0)
        kh = kwin[:, h * HEAD_DIM:(h + 1) * HEAD_DIM]
        vh = vwin[:, h * HEAD_DIM:(h + 1) * HEAD_DIM]
        logits = _dot_nt(qs.astype(BF16), kh.astype(BF16))
        logits = jnp.where(valid, logits, -1e30)
        sk = jnp.zeros((rows, 1), F32)
        for g in range(Q_PER_KV):
            sk = jnp.where(rowg == g, sink_ref[Q_PER_KV * h + g], sk)
        m = jnp.maximum(jnp.max(logits, -1, keepdims=True), sk)
        e = jnp.exp(logits - m)
        den = jnp.sum(e, -1, keepdims=True) + jnp.exp(sk - m)
        p = e / den
        o = _dot(p.astype(BF16), vh.astype(BF16))
        for g in range(Q_PER_KV):
            outs.append(o[g * blk:(g + 1) * blk])
    o_ref[0] = jnp.concatenate(outs, axis=1)


def _attention(qkv3, sink):
    b, s, _ = qkv3.shape
    blk = ATTN_BLOCK
    nb = s // blk
    kcol = ATTN_WIDTH // LANES
    vcol = kcol + 1

    def spec(col, shift):
        def imap(bi, n):
            return (bi, jnp.clip(n + shift, 0, nb - 1), col)
        return pl.BlockSpec((1, blk, LANES), imap)

    return pl.pallas_call(
        functools.partial(_attn_kernel, nb=nb),
        grid=(b, nb),
        in_specs=[
            pl.BlockSpec(memory_space=pltpu.SMEM),
            pl.BlockSpec((1, blk, ATTN_WIDTH), lambda bi, n: (bi, n, 0)),
            spec(kcol, -1), spec(kcol, 0), spec(kcol, 1),
            spec(vcol, -1), spec(vcol, 0), spec(vcol, 1),
        ],
        out_specs=pl.BlockSpec((1, blk, ATTN_WIDTH), lambda bi, n: (bi, n, 0)),
        out_shape=jax.ShapeDtypeStruct((b, s, ATTN_WIDTH), F32),
        compiler_params=_cparams(("parallel", "parallel")),
        name="attn",
    )(sink, qkv3, qkv3, qkv3, qkv3, qkv3, qkv3, qkv3)


def _softplus(x):
    return jnp.maximum(x, 0.0) + jnp.log(1.0 + jnp.exp(-jnp.abs(x)))


def _sigmoid(x):
    return 1.0 / (1.0 + jnp.exp(-x))


def _prep_kernel(p_ref, hp_ref, hn_ref, mup_ref, mun_ref, wmix_ref, w0_ref, a0_ref, g2_ref,
                 kk_ref, ka_ref, bd_ref,
                 r_o, v_o, kk_o, g_o, lw0_o, lw1_o, kd0_o, kd1_o, b0_o, b1_o, *, tm, seq):
    i = pl.program_id(0)
    row = lax.broadcasted_iota(jnp.int32, (tm, 1), 0)
    seq_start = (i * tm) % seq == 0
    seq_end = ((i + 1) * tm) % seq == 0

    def shifted(c0, c1):
        p = p_ref[:, c0:c1]
        prev_row = jnp.where(seq_start, 0.0, hp_ref[SUBLANES - 1:SUBLANES, c0:c1])
        next_row = jnp.where(seq_end, 0.0, hn_ref[0:1, c0:c1])
        p_prev = jnp.where(row == 0, prev_row, pltpu.roll(p, 1, 0))
        p_next = jnp.where(row == tm - 1, next_row, pltpu.roll(p, tm - 1, 0))
        return p + mup_ref[:, c0:c1] * (p_prev - p) + mun_ref[:, c0:c1] * (p_next - p)

    w = RWKV_WIDTH
    r = shifted(0, w)
    k = shifted(w, 2 * w)
    v = shifted(2 * w, 3 * w)
    codes = shifted(3 * w, 3 * w + LANES)
    gd = shifted(3 * w + LANES, 3 * w + 2 * LANES)
    r_o[...] = r.astype(r_o.dtype)
    v_o[...] = v.astype(v_o.dtype)

    lane = lax.broadcasted_iota(jnp.int32, codes.shape, 1)
    codes = jnp.where(lane < 2 * DECAY_RANK, jnp.tanh(codes), codes)
    mm = _dot(codes, wmix_ref[...], HI)
    g_o[...] = _dot(_sigmoid(gd), g2_ref[...], HI)

    kkv = k * kk_ref[...]
    bd = bd_ref[...].astype(BF16)
    ss = sum(_dot(part, bd) for part in _split3(kkv * kkv))
    kkn = kkv * lax.rsqrt(jnp.maximum(ss, 1e-24))
    kk_o[...] = kkn.astype(kk_o.dtype)

    ka = ka_ref[...]
    for d, (lw_o, kd_o, b_o) in enumerate(((lw0_o, kd0_o, b0_o), (lw1_o, kd1_o, b1_o))):
        w_log = -_softplus(-(w0_ref[d:d + 1, :] + mm[:, d * w:(d + 1) * w])) - 0.5
        lw_o[...] = -jnp.exp(w_log)
        a = _sigmoid(a0_ref[d:d + 1, :] + mm[:, (2 + d) * w:(3 + d) * w])
        kd_o[...] = (k * (1.0 + (a - 1.0) * ka)).astype(kd_o.dtype)
        b_o[...] = (kkn * a).astype(b_o.dtype)


def _prep(pr, mup, mun, wmix, w0, a0, g2p, k_k, k_a, bd, seq, tm=256):
    t = pr.shape[0]
    nblk8 = t // SUBLANES
    per = tm // SUBLANES
    full = lambda a: pl.BlockSpec(a.shape, lambda i: (0,) * a.ndim)
    out = pl.BlockSpec((tm, RWKV_WIDTH), lambda i: (i, 0))
    return pl.pallas_call(
        functools.partial(_prep_kernel, tm=tm, seq=seq),
        grid=(t // tm,),
        in_specs=[
            pl.BlockSpec((tm, RWKV_COLS_PAD), lambda i: (i, 0)),
            pl.BlockSpec((SUBLANES, RWKV_COLS_PAD), lambda i: (jnp.maximum(i * per - 1, 0), 0)),
            pl.BlockSpec((SUBLANES, RWKV_COLS_PAD),
                         lambda i: (jnp.minimum((i + 1) * per, nblk8 - 1), 0)),
            full(mup), full(mun), full(wmix), full(w0), full(a0), full(g2p),
            full(k_k), full(k_a), full(bd),
        ],
        out_specs=[out] * 10,
        out_shape=[jax.ShapeDtypeStruct((t, RWKV_WIDTH), dt) for dt in PREP_OUT_DTYPES],
        compiler_params=_cparams(("parallel",)),
        name="prep",
    )(pr, pr, pr, mup, mun, wmix, w0, a0, g2p, k_k, k_a, bd)


def _split3(x):
    hi = x.astype(BF16)
    r1 = x - hi.astype(F32)
    mid = r1.astype(BF16)
    lo = (r1 - mid.astype(F32)).astype(BF16)
    return hi, mid, lo


def _wkv_kernel(rf, vf, kkf, lwf, kdf, bf, rb, vb, kkb, lwb, kdb, bb, yf_ref, yb_ref, state):
    c = pl.program_id(1)
    n = WKV_CHUNK
    hd = HEAD_DIM
    nh = N_RWKV_HEADS
    def bmm(a, b, ca, cb):
        return lax.dot_general(a.astype(BF16), b.astype(BF16), (((ca,), (cb,)), ((0,), (0,))),
                               preferred_element_type=F32)

    @pl.when(c == 0)
    def _():
        state[...] = jnp.zeros_like(state)

    ti = lax.broadcasted_iota(jnp.int32, (n, n), 0)
    si = lax.broadcasted_iota(jnp.int32, (n, n), 1)
    dirs = ((rf, vf, kkf, lwf, kdf, bf, si <= ti, si < ti, n - 1),
            (rb, vb, kkb, lwb, kdb, bb, si >= ti, si > ti, 0))
    heads = lambda x: jnp.stack([x[:, j * hd:(j + 1) * hd] for j in range(nh)], 0)
    parts = []
    for r_ref, v_ref, kk_ref, lw_ref, kd_ref, b_ref, incl, strict, last in dirs:
        lw = lw_ref[0]
        tri = incl.astype(BF16)
        cum = sum(_dot(tri, part) for part in _split3(lw))
        e_neg = jnp.exp(-cum)
        e_last = jnp.exp(cum[last:last + 1, :] - cum)
        g_scale = jnp.exp(cum[last:last + 1, :])
        parts.append(dict(
            at=heads((-kk_ref[0] * jnp.exp(cum - lw)).astype(BF16)),
            rt=heads((r_ref[0] * jnp.exp(cum)).astype(BF16)),
            bt=heads((b_ref[0] * e_neg).astype(BF16)),
            kt=heads((kd_ref[0] * e_neg).astype(BF16)),
            bl=heads((b_ref[0] * e_last).astype(BF16)),
            kl=heads((kd_ref[0] * e_last).astype(BF16)),
            v=heads(v_ref[0].astype(BF16)),
            gs=heads(g_scale),
            incl=jnp.broadcast_to(incl[None], (nh, n, n)),
            strict=jnp.broadcast_to(strict[None], (nh, n, n))))
    cat = lambda key: jnp.concatenate([parts[0][key], parts[1][key]], 0)
    at, rt, bt, kt, bl, kl, v, gs = (cat(k) for k in ("at", "rt", "bt", "kt", "bl", "kl", "v", "gs"))
    incl, strict = cat("incl"), cat("strict")
    g0 = state[...]
    g0b = g0.astype(BF16)
    m1 = bmm(jnp.concatenate([at, rt], 1), jnp.concatenate([bt, kt], 1), 2, 2)
    a_ab = jnp.where(strict, m1[:, :n, :n], 0.0)
    a_ak = jnp.where(strict, m1[:, :n, n:], 0.0)
    a_rb = jnp.where(incl, m1[:, n:, :n], 0.0)
    a_rk = jnp.where(incl, m1[:, n:, n:], 0.0)
    tinv = jnp.where((ti == si)[None], 1.0, a_ab)
    pw = a_ab
    for _ in range(int(np.log2(n)) - 1):
        pw = bmm(pw, pw, 2, 1)
        tinv = tinv + bmm(tinv, pw, 2, 1)
    rhs = bmm(a_ak, v, 2, 1) + bmm(at, g0b, 2, 2)
    u = bmm(tinv, rhs, 2, 1)
    uv = jnp.concatenate([u.astype(BF16), v], 1)
    y = bmm(jnp.concatenate([a_rb, a_rk], 2), uv, 2, 1) + bmm(rt, g0b, 2, 2)
    state[...] = g0 * gs + bmm(uv, jnp.concatenate([bl, kl], 1), 1, 1)
    for d, y_ref in enumerate((yf_ref, yb_ref)):
        y_ref[0] = jnp.concatenate([y[d * nh + j] for j in range(nh)], axis=1)


def _wkv(r, v, kk, lw0, lw1, kd0, kd1, b0, b1, batch, seq):
    n = WKV_CHUNK
    nc = seq // n
    shp = (batch, seq, RWKV_WIDTH)
    arrs = [a.reshape(shp) for a in (r, v, kk, lw0, kd0, b0, r, v, kk, lw1, kd1, b1)]
    fwd = pl.BlockSpec((1, n, RWKV_WIDTH), lambda b, c: (b, c, 0))
    bwd = pl.BlockSpec((1, n, RWKV_WIDTH), lambda b, c: (b, nc - 1 - c, 0))
    yf, yb = pl.pallas_call(
        _wkv_kernel,
        grid=(batch, nc),
        in_specs=[fwd] * 6 + [bwd] * 6,
        out_specs=[fwd, bwd],
        out_shape=[jax.ShapeDtypeStruct(shp, F32)] * 2,
        scratch_shapes=[pltpu.VMEM((2 * N_RWKV_HEADS, HEAD_DIM, HEAD_DIM), F32)],
        compiler_params=_cparams(("parallel", "arbitrary")),
        name="wkv",
    )(*arrs)
    return yf.reshape(batch * seq, RWKV_WIDTH), yb.reshape(batch * seq, RWKV_WIDTH)


def _mix_kernel(x_ref, ya_ref, yf_ref, yb_ref, r_ref, v_ref, kd0_ref, kd1_ref, g_ref,
                lng_ref, lnb_ref, rk_ref, bd_ref, wout_ref, l1g_ref, l1b_ref, h_ref):
    bd = bd_ref[...].astype(BF16)
    head_sum = lambda t: sum(_dot(part, bd) for part in _split3(t))
    inv = 1.0 / HEAD_DIM
    y = yf_ref[...] + yb_ref[...]
    mu = head_sum(y) * inv
    yc = y - mu
    var = head_sum(yc * yc) * inv
    yn = yc * lax.rsqrt(var + RWKV_GN_EPS) * lng_ref[...] + lnb_ref[...]
    k_mean = 0.5 * (kd0_ref[...] + kd1_ref[...])
    v = v_ref[...]
    bonus = head_sum(r_ref[...] * k_mean * rk_ref[...]) * v
    yr = (yn + bonus) * g_ref[...]
    mix = (_dot(ya_ref[...].astype(BF16), wout_ref[:ATTN_WIDTH, :])
           + _dot(yr.astype(BF16), wout_ref[ATTN_WIDTH:, :]))
    h_ref[...] = _layer_norm(DEEPNORM_ALPHA * x_ref[...] + mix, l1g_ref[...], l1b_ref[...])


def _mix(x2, ya, yf, yb, r, v, kd0, kd1, g, lng, lnb, rk, bd, wout, l1g, l1b, tm=256):
    t, d = x2.shape
    full = lambda a: pl.BlockSpec(a.shape, lambda i: (0,) * a.ndim)
    half = pl.BlockSpec((tm, RWKV_WIDTH), lambda i: (i, 0))
    wide = pl.BlockSpec((tm, d), lambda i: (i, 0))
    return pl.pallas_call(
        _mix_kernel,
        grid=(t // tm,),
        in_specs=[wide] + [half] * 8 + [full(lng), full(lnb), full(rk), full(bd), full(wout),
                                        full(l1g), full(l1b)],
        out_specs=wide,
        out_shape=jax.ShapeDtypeStruct((t, d), F32),
        compiler_params=_cparams(("parallel",)),
        name="mix",
    )(x2, ya, yf, yb, r, v, kd0, kd1, g, lng, lnb, rk, bd, wout, l1g, l1b)


def _top_rows(scs, k, payloads=None):
    n = scs[0].shape[0]
    iota = lax.broadcasted_iota(jnp.int32, scs[0].shape, 0).astype(F32)
    scs = list(scs)
    vals = [[] for _ in scs]
    picks = [[] for _ in scs]
    for _ in range(k):
        for a, sc in enumerate(scs):
            m = jnp.max(sc, axis=0, keepdims=True)
            pos = jnp.min(jnp.where(sc == m, iota, float(n)), axis=0, keepdims=True)
            hit = iota == pos
            vals[a].append(m)
            if payloads is None:
                picks[a].append(pos)
            else:
                picks[a].append(jnp.max(jnp.where(hit, payloads[a], -1.0), axis=0, keepdims=True))
            scs[a] = jnp.where(hit, -jnp.inf, sc)
    return [(jnp.concatenate(v, 0), jnp.concatenate(p, 0)) for v, p in zip(vals, picks)]


def _route_kernel(h_ref, wq_ref, keys_ref, idx_ref, gate_ref, gate_tok_ref, q_scr, idx_scr):
    k = PEER_TOPK
    q_scr[...] = _dot(h_ref[...].astype(BF16), wq_ref[...])

    def head_group(hg, carry):
        heads = [hg * ROUTE_HEADS_PER_STEP + i for i in range(ROUTE_HEADS_PER_STEP)]
        scores = []
        for hh in heads:
            for p in range(2):
                col = pl.multiple_of(hh * PEER_QDIM + p * PEER_HALF, PEER_HALF)
                qp = q_scr[:, pl.ds(col, PEER_HALF)]
                scores.append(_dot_nt(keys_ref[hh, p], qp, HI))
        tops = _top_rows(scores, k)
        cands, cand_ids = [], []
        for a in range(len(heads)):
            (s1, i1), (s2, i2) = tops[2 * a], tops[2 * a + 1]
            cands.append(jnp.concatenate([s1[i:i + 1] + s2[:k // (i + 1)] for i in range(k)], 0))
            cand_ids.append(jnp.concatenate(
                [i1[i:i + 1] * PEER_NKEYS + i2[:k // (i + 1)] for i in range(k)], 0))
        for hh, (cs, ids) in zip(heads, _top_rows(cands, k, cand_ids)):
            e = jnp.exp(cs - jnp.max(cs, axis=0, keepdims=True))
            row = pl.multiple_of(hh * k, k)
            gate_ref[pl.ds(row, k), :] = e / jnp.sum(e, axis=0, keepdims=True)
            idx_scr[pl.ds(row, k), :] = ids.astype(jnp.int32)
        return carry

    lax.fori_loop(0, PEER_HEADS // ROUTE_HEADS_PER_STEP, head_group, 0)
    idx_ref[...] = idx_scr[...].T
    gate_tok_ref[...] = gate_ref[...].T


def _route(h1, wq, keys, tm=256):
    t, d = h1.shape
    return pl.pallas_call(
        _route_kernel,
        grid=(t // tm,),
        in_specs=[
            pl.BlockSpec((tm, d), lambda i: (i, 0)),
            pl.BlockSpec(wq.shape, lambda i: (0, 0)),
            pl.BlockSpec(keys.shape, lambda i: (0, 0, 0, 0)),
        ],
        out_specs=[
            pl.BlockSpec((tm, N_SEL), lambda i: (i, 0)),
            pl.BlockSpec((N_SEL, tm), lambda i: (0, i)),
            pl.BlockSpec((tm, N_SEL), lambda i: (i, 0)),
        ],
        out_shape=[
            jax.ShapeDtypeStruct((t, N_SEL), jnp.int32),
            jax.ShapeDtypeStruct((N_SEL, t), F32),
            jax.ShapeDtypeStruct((t, N_SEL), F32),
        ],
        scratch_shapes=[pltpu.VMEM((tm, PEER_HEADS * PEER_QDIM), F32),
                        pltpu.VMEM((N_SEL, tm), jnp.int32)],
        compiler_params=_cparams(("parallel",)),
        name="route",
    )(h1, wq, keys)


def _peer_kernel(idx_ref, gate_ref, h_ref, uv_hbm, l2g_ref, l2b_ref, o_ref, buf, f_scr, sem, *, tb):
    d = h_ref.shape[1]

    def issue(t, slot):
        for j in range(N_SEL):
            pltpu.make_async_copy(uv_hbm.at[pl.ds(idx_ref[t, j], 1)], buf.at[slot, pl.ds(j, 1)],
                                  sem.at[slot]).start()

    def wait_all(slot):
        pltpu.make_async_copy(uv_hbm.at[pl.ds(0, N_SEL)], buf.at[slot], sem.at[slot]).wait()

    for t_ahead in range(PEER_SLOTS - 1):
        issue(t_ahead, t_ahead)
    lane = lax.broadcasted_iota(jnp.int32, (N_SEL, tb), 1)

    def body(t, carry):
        slot = t % PEER_SLOTS

        @pl.when(t + PEER_SLOTS - 1 < tb)
        def _():
            issue(t + PEER_SLOTS - 1, (t + PEER_SLOTS - 1) % PEER_SLOTS)

        wait_all(slot)
        z = jnp.sum(buf[slot, :, 0:d] * h_ref[pl.ds(t, 1), :], axis=1, keepdims=True)
        gcol = jnp.sum(jnp.where(lane == t, gate_ref[...], 0.0), axis=1, keepdims=True)
        coef = gcol * (0.5 * z * (1.0 + lax.erf(z * (2.0 ** -0.5))))
        f_scr[pl.ds(t, 1), :] = jnp.sum(coef * buf[slot, :, d:2 * d], axis=0, keepdims=True)
        return carry

    lax.fori_loop(0, tb, body, 0)
    o_ref[...] = _layer_norm(DEEPNORM_ALPHA * h_ref[...] + f_scr[...], l2g_ref[...], l2b_ref[...])


def _peer(idx, gate_t, h1, uv, l2g, l2b, tok0, ntok, tb=PEER_TB):
    d = h1.shape[1]
    b0 = tok0 // tb
    return pl.pallas_call(
        functools.partial(_peer_kernel, tb=tb),
        grid=(ntok // tb,),
        in_specs=[
            pl.BlockSpec((tb, N_SEL), lambda i: (b0 + i, 0), memory_space=pltpu.SMEM),
            pl.BlockSpec((N_SEL, tb), lambda i: (0, b0 + i)),
            pl.BlockSpec((tb, d), lambda i: (b0 + i, 0)),
            pl.BlockSpec(memory_space=pl.ANY),
            pl.BlockSpec(l2g.shape, lambda i: (0, 0)),
            pl.BlockSpec(l2b.shape, lambda i: (0, 0)),
        ],
        out_specs=pl.BlockSpec((tb, d), lambda i: (i, 0)),
        out_shape=jax.ShapeDtypeStruct((ntok, d), F32),
        scratch_shapes=[pltpu.VMEM((PEER_SLOTS, N_SEL, 2 * d), F32),
                        pltpu.VMEM((tb, d), F32),
                        pltpu.SemaphoreType.DMA((PEER_SLOTS,))],
        compiler_params=_cparams(("arbitrary",)),
        name="peer",
    )(idx, gate_t, h1, uv, l2g, l2b)


def _sc_mesh():
    return plsc.VectorSubcoreMesh(core_axis_name="c", subcore_axis_name="s")


def _worker_id():
    return lax.axis_index("s") * 2 + lax.axis_index("c")


def _sc_z_body(u_hbm, idx_hbm, h_hbm, z_hbm, idx_v, h_v, rows_v, z_v, sem, *, tpw, d, tok0):
    ngrp = N_SEL // SC_GROUP
    nchunk = d // SC_LANES
    nsteps = tpw * ngrp
    base = _worker_id() * tpw

    def gather(tok_buf, g, buf):
        return pltpu.make_async_copy(u_hbm.at[idx_v.at[tok_buf, g]], rows_v.at[buf], sem.at[buf])

    pltpu.sync_copy(idx_hbm.at[tok0 + base], idx_v.at[0])
    gather(0, 0, 0).start()

    @pl.loop(0, nsteps)
    def _(q):
        t_loc = q // ngrp
        g = q % ngrp
        buf = q % 2
        t = base + t_loc

        @pl.when(g == 0)
        def _():
            pltpu.sync_copy(h_hbm.at[tok0 + t], h_v)

            @pl.loop(0, N_SEL)
            def _(r):
                z_v[pl.ds(pl.multiple_of(r * SC_LANES, SC_LANES), SC_LANES)] = jnp.zeros((SC_LANES,), F32)

        @pl.when(q + 1 < nsteps)
        def _():
            tn = (q + 1) // ngrp
            gn = (q + 1) % ngrp

            @pl.when(gn == 0)
            def _():
                pltpu.sync_copy(idx_hbm.at[tok0 + base + tn], idx_v.at[tn % 2])

            gather(tn % 2, gn, 1 - buf).start()

        gather(t_loc % 2, g, buf).wait()

        @pl.loop(0, SC_GROUP, step=SC_Z_ROWS)
        def _(rb):
            @pl.loop(0, nchunk, step=SC_CHUNK_UNROLL)
            def _(c0):
                accs = [jnp.zeros((SC_LANES,), F32) for _ in range(SC_Z_ROWS)]
                for cc in range(SC_CHUNK_UNROLL):
                    sl = pl.ds(pl.multiple_of((c0 + cc) * SC_LANES, SC_LANES), SC_LANES)
                    hc = h_v[sl]
                    for i in range(SC_Z_ROWS):
                        accs[i] = accs[i] + rows_v[buf, rb + i, sl] * hc
                for i in range(SC_Z_ROWS):
                    row = pl.multiple_of((g * SC_GROUP + rb + i) * SC_LANES, SC_LANES)
                    plsc.addupdate(z_v.at[pl.ds(row, SC_LANES)], accs[i])

        @pl.when(g == ngrp - 1)
        def _():
            pltpu.sync_copy(z_v, z_hbm.at[t])


def _sc_z(u, idx, h, tok0, ntok):
    t, d = h.shape
    tpw = ntok // SC_WORKERS
    idx4 = idx.reshape(t, N_SEL // SC_GROUP, SC_GROUP)
    body = functools.partial(_sc_z_body, tpw=tpw, d=d, tok0=tok0)
    return pl.kernel(
        body,
        out_type=jax.ShapeDtypeStruct((ntok, N_SEL * SC_LANES), F32),
        mesh=_sc_mesh(),
        scratch_types=[
            pltpu.VMEM((2, N_SEL // SC_GROUP, SC_GROUP), jnp.int32),
            pltpu.VMEM((d,), F32),
            pltpu.VMEM((2, SC_GROUP, d), F32),
            pltpu.VMEM((N_SEL * SC_LANES,), F32),
            pltpu.SemaphoreType.DMA((2,)),
        ],
        name="sc_z",
    )(u, idx4, h)


def _sc_out_body(v_hbm, idx_hbm, coef_hbm, f_hbm, idx_v, coef_v, rows_v, out_v, sem, *, tpw, d, tok0):
    ngrp = N_SEL // SC_GROUP
    nchunk = d // SC_LANES
    nsteps = tpw * ngrp
    base = _worker_id() * tpw

    def gather(tok_buf, g, buf):
        return pltpu.make_async_copy(v_hbm.at[idx_v.at[tok_buf, g]], rows_v.at[buf], sem.at[buf])

    pltpu.sync_copy(idx_hbm.at[tok0 + base], idx_v.at[0])
    gather(0, 0, 0).start()

    @pl.loop(0, nsteps)
    def _(q):
        t_loc = q // ngrp
        g = q % ngrp
        buf = q % 2
        t = base + t_loc

        @pl.when(g == 0)
        def _():
            pltpu.sync_copy(coef_hbm.at[t], coef_v)

        @pl.when(q + 1 < nsteps)
        def _():
            tn = (q + 1) // ngrp
            gn = (q + 1) % ngrp

            @pl.when(gn == 0)
            def _():
                pltpu.sync_copy(idx_hbm.at[tok0 + base + tn], idx_v.at[tn % 2])

            gather(tn % 2, gn, 1 - buf).start()

        gather(t_loc % 2, g, buf).wait()

        @pl.loop(0, nchunk, step=SC_OUT_CHUNKS)
        def _(c0):
            sls = [pl.ds(pl.multiple_of((c0 + cc) * SC_LANES, SC_LANES), SC_LANES)
                   for cc in range(SC_OUT_CHUNKS)]
            accs = [None] * SC_OUT_CHUNKS
            for r in range(SC_GROUP):
                cf = coef_v[pl.ds(pl.multiple_of((g * SC_GROUP + r) * SC_LANES, SC_LANES), SC_LANES)]
                for cc in range(SC_OUT_CHUNKS):
                    p = rows_v[buf, r, sls[cc]] * cf
                    accs[cc] = p if accs[cc] is None else accs[cc] + p
            for cc in range(SC_OUT_CHUNKS):
                @pl.when(g == 0)
                def _():
                    out_v[sls[cc]] = accs[cc]

                @pl.when(g != 0)
                def _():
                    plsc.addupdate(out_v.at[sls[cc]], accs[cc])

        @pl.when(g == ngrp - 1)
        def _():
            pltpu.sync_copy(out_v, f_hbm.at[t])


def _sc_out(v, idx, coef16, tok0):
    t = idx.shape[0]
    ntok = coef16.shape[0]
    d = v.shape[1]
    tpw = ntok // SC_WORKERS
    idx4 = idx.reshape(t, N_SEL // SC_GROUP, SC_GROUP)
    body = functools.partial(_sc_out_body, tpw=tpw, d=d, tok0=tok0)
    return pl.kernel(
        body,
        out_type=jax.ShapeDtypeStruct((ntok, d), F32),
        mesh=_sc_mesh(),
        scratch_types=[
            pltpu.VMEM((2, N_SEL // SC_GROUP, SC_GROUP), jnp.int32),
            pltpu.VMEM((N_SEL * SC_LANES,), F32),
            pltpu.VMEM((2, SC_GROUP, d), F32),
            pltpu.VMEM((d,), F32),
            pltpu.SemaphoreType.DMA((2,)),
        ],
        name="sc_out",
    )(v, idx4, coef16)


def _coef_kernel(zp_ref, gate_ref, sel_ref, selt_ref, after_a_ref, after_b_ref, o_ref):
    z = sum(_dot(part, sel_ref[...]) for part in _split3(zp_ref[...]))
    coef = gate_ref[...] * (0.5 * z * (1.0 + lax.erf(z * (2.0 ** -0.5))))
    o_ref[...] = sum(_dot(part, selt_ref[...]) for part in _split3(coef))


def _coef(zp, gate_tok, tok0, after_a, after_b, tm=256):
    ntok, wide = zp.shape
    b0 = tok0 // tm
    grp = np.arange(wide) // SC_LANES
    sel = jnp.asarray(grp[:, None] == np.arange(N_SEL)[None, :], BF16)
    return pl.pallas_call(
        _coef_kernel,
        grid=(ntok // tm,),
        in_specs=[
            pl.BlockSpec((tm, wide), lambda i: (i, 0)),
            pl.BlockSpec((tm, N_SEL), lambda i: (b0 + i, 0)),
            pl.BlockSpec(sel.shape, lambda i: (0, 0)),
            pl.BlockSpec(sel.shape[::-1], lambda i: (0, 0)),
            pl.BlockSpec((SUBLANES, LANES), lambda i: (0, 0)),
            pl.BlockSpec((SUBLANES, LANES), lambda i: (0, 0)),
        ],
        out_specs=pl.BlockSpec((tm, wide), lambda i: (i, 0)),
        out_shape=jax.ShapeDtypeStruct((ntok, wide), F32),
        compiler_params=_cparams(("parallel",)),
        name="coef",
    )(zp, gate_tok, sel, sel.T, after_a, after_b)


def _ln_out_kernel(h_ref, f_ref, g_ref, b_ref, o_ref):
    o_ref[...] = _layer_norm(DEEPNORM_ALPHA * h_ref[...] + f_ref[...], g_ref[...], b_ref[...])


def _ln_out(h1, f, l2g, l2b, tok0, tm=256):
    ntok, d = f.shape
    b0 = tok0 // tm
    return pl.pallas_call(
        _ln_out_kernel,
        grid=(ntok // tm,),
        in_specs=[
            pl.BlockSpec((tm, d), lambda i: (b0 + i, 0)),
            pl.BlockSpec((tm, d), lambda i: (i, 0)),
            pl.BlockSpec(l2g.shape, lambda i: (0, 0)),
            pl.BlockSpec(l2b.shape, lambda i: (0, 0)),
        ],
        out_specs=pl.BlockSpec((tm, d), lambda i: (i, 0)),
        out_shape=jax.ShapeDtypeStruct((ntok, d), F32),
        compiler_params=_cparams(("parallel",)),
        name="ln_out",
    )(h1, f, l2g, l2b)


def _rope_tables(seq):
    pos = jnp.arange(seq, dtype=F32)
    inv_freq = ROPE_THETA ** (-jnp.arange(0, HEAD_DIM, 2, dtype=F32) / HEAD_DIM)
    ang = pos[:, None] * inv_freq[None, :]
    ang = jnp.concatenate([ang, ang], -1)
    sign = jnp.concatenate([-jnp.ones((HEAD_DIM // 2,), F32), jnp.ones((HEAD_DIM // 2,), F32)])
    reps = LANES // HEAD_DIM
    return jnp.tile(jnp.cos(ang), (1, reps)), jnp.tile(jnp.sin(ang) * sign, (1, reps))


def _layer(h2d, batch, seq, w_in, mu_prev, mu_next, w0, w2, a0, a2, g2, k_k, k_a, r_k, lnx_g, lnx_b,
           sink, w_out, ln1_g, ln1_b, peer_wq, peer_keys, peer_u, peer_v, ln2_g, ln2_b):
    w = RWKV_WIDTH
    row = lambda a: a.reshape(1, -1).astype(F32)
    w_in_p = jnp.pad(w_in, ((0, 0), (0, RWKV_COLS_PAD - RWKV_COLS))).astype(BF16)
    mup = jnp.pad(mu_prev, (0, RWKV_COLS_PAD - RWKV_COLS)).reshape(1, -1)
    mun = jnp.pad(mu_next, (0, RWKV_COLS_PAD - RWKV_COLS)).reshape(1, -1)
    wmix = jnp.zeros((LANES, 4 * w), F32)
    for d in range(2):
        wmix = wmix.at[d * DECAY_RANK:(d + 1) * DECAY_RANK, d * w:(d + 1) * w].set(w2[d])
        r0 = 2 * DECAY_RANK + d * ICLR_RANK
        wmix = wmix.at[r0:r0 + ICLR_RANK, (2 + d) * w:(3 + d) * w].set(a2[d])
    g2p = jnp.pad(g2, ((0, LANES - GATE_RANK), (0, 0)))
    head_of = np.arange(w) // HEAD_DIM
    bd = jnp.asarray(head_of[:, None] == head_of[None, :], F32)
    cos_t, sin_t = _rope_tables(seq)

    wq_b, keys_f, wout_b = peer_wq.astype(BF16), peer_keys.astype(F32), w_out.astype(BF16)

    def to_routing(x2d, nb):
        qkv, pr = _proj(x2d, w_in_p, cos_t, sin_t, seq)
        y_attn = _attention(qkv.reshape(nb, seq, ATTN_COLS), sink.astype(F32))
        r, v, kk, g, lw0, lw1, kd0, kd1, b0, b1 = _prep(
            pr, mup, mun, wmix, w0.astype(F32), a0.astype(F32), g2p, row(k_k), row(k_a), bd, seq)
        yf, yb = _wkv(r, v, kk, lw0, lw1, kd0, kd1, b0, b1, nb, seq)
        h1 = _mix(x2d, y_attn.reshape(nb * seq, ATTN_WIDTH), yf, yb, r, v, kd0, kd1, g,
                  row(lnx_g), row(lnx_b), row(r_k), bd, wout_b, row(ln1_g), row(ln1_b))
        return (h1,) + tuple(_route(h1, wq_b, keys_f))

    uv = jnp.concatenate([peer_u, peer_v], axis=1)
    l2g, l2b = row(ln2_g), row(ln2_b)

    nb_1 = batch * PEER_FIRST_GROUP_SEQ_8THS // 8
    t_1 = nb_1 * seq
    t_2 = (batch - nb_1) * seq
    n_sc = (batch * seq * PEER_SC_TOKEN_SHARE_128THS // 128) // PEER_TB * PEER_TB
    n_a = (t_2 * PEER_TC_FIRST_CALL_32NDS // 32) // PEER_TB * PEER_TB
    n_b = t_2 - n_a
    assert 0 < nb_1 < batch and 0 < n_sc < t_1 and n_sc % (SC_WORKERS * SUBLANES) == 0
    assert n_a > 0 and n_b > 0 and n_b % PEER_TB == 0 and (t_1 - n_sc) % PEER_TB == 0
    h1_1, idx_1, gate_t_1, gate_tok_1 = to_routing(h2d[:t_1], nb_1)
    zp = _sc_z(peer_u.astype(F32), idx_1, h1_1, 0, n_sc)
    h1_2, idx_2, gate_t_2, _ = to_routing(h2d[t_1:], batch - nb_1)
    out_tail = _peer(idx_1, gate_t_1, h1_1, uv, l2g, l2b, n_sc, t_1 - n_sc)
    out_a = _peer(idx_2, gate_t_2, h1_2, uv, l2g, l2b, 0, n_a)
    f_sc = _sc_out(peer_v.astype(F32), idx_1, _coef(zp, gate_tok_1, 0, out_tail, out_a), 0)
    out_b = _peer(idx_2, gate_t_2, h1_2, uv, l2g, l2b, n_a, n_b)
    out_sc = _ln_out(h1_1, f_sc, l2g, l2b, 0)
    return jnp.concatenate([out_sc, out_tail, out_a, out_b], axis=0)


def kernel(x, w_in, mu_prev, mu_next, w0, w2, a0, a2, g2, k_k, k_a, r_k, lnx_g, lnx_b, sink, w_out,
           ln1_g, ln1_b, peer_wq, peer_keys, peer_u, peer_v, ln2_g, ln2_b):
    batch, seq, d = x.shape
    h = x.reshape(batch * seq, d)
    for l in range(DEPTH):
        h = _layer(h, batch, seq, w_in[l], mu_prev[l], mu_next[l], w0[l], w2[l], a0[l], a2[l], g2[l],
                   k_k[l], k_a[l], r_k[l], lnx_g[l], lnx_b[l], sink[l], w_out[l], ln1_g[l], ln1_b[l],
                   peer_wq[l], peer_keys[l], peer_u[l], peer_v[l], ln2_g[l], ln2_b[l])
    return h.reshape(batch, seq, d)
```
